```python
import jax
import jax.numpy as jnp
from jax import lax
import numpy as np


D_MODEL = 1024
BATCH = 8
SEQ = 16384
DEPTH = 4

N_HEADS = 8
HEAD_DIM = 128
GDN_WIDTH = N_HEADS * HEAD_DIM
SC_WIDTH = D_MODEL
CONV_W = 3
CHUNK = 64
D_FF = 4 * D_MODEL
LN_EPS = 1e-5
RMS_EPS = 1e-6
L2_EPS = 1e-6
DEEPNORM_ALPHA = (2 * DEPTH) ** 0.25
DEEPNORM_BETA = (8 * DEPTH) ** -0.25

IN_SPLITS = (3 * GDN_WIDTH,
             GDN_WIDTH,
             2 * N_HEADS,
             2 * N_HEADS,
             SC_WIDTH,
             SC_WIDTH,
             SC_WIDTH,
             D_MODEL,
             D_MODEL)
IN_COLS = sum(IN_SPLITS)

kernel_name = 'hybrid_gdn_shortconv_deepnorm_encoder'


def _layer_norm(x, g, b):
    xf = x.astype(jnp.float32)
    mu = jnp.mean(xf, axis=-1, keepdims=True)
    var = jnp.mean(jnp.square(xf - mu), axis=-1, keepdims=True)
    y = (xf - mu) * lax.rsqrt(var + LN_EPS) * g.astype(jnp.float32) + b.astype(jnp.float32)
    return y.astype(x.dtype)


def _l2norm(x):
    return x * lax.rsqrt(jnp.sum(jnp.square(x), axis=-1, keepdims=True) + L2_EPS)


def _centred_dwconv(x, w):
    c = x.shape[-1]
    return lax.conv_general_dilated(
        x, w[:, None, :].astype(x.dtype), window_strides=(1,),
        padding=[(CONV_W // 2, CONV_W // 2)],
        dimension_numbers=('NWC', 'WIO', 'NWC'), feature_group_count=c)


def _split_in(proj):
    idx = []
    acc = 0
    for s in IN_SPLITS[:-1]:
        acc += s
        idx.append(acc)
    return jnp.split(proj, idx, axis=-1)


def _gated_delta_chunked(q, k, v, g, beta):
    b, t, h, dk = q.shape
    dv = v.shape[-1]
    n = t // CHUNK

    def blk(a):
        return a.reshape(b, n, CHUNK, h, -1).transpose(1, 0, 3, 2, 4)

    q, k, v = blk(q), blk(k), blk(v)
    g = g.reshape(b, n, CHUNK, h).transpose(1, 0, 3, 2)
    beta = beta.reshape(b, n, CHUNK, h).transpose(1, 0, 3, 2)
    gc = jnp.cumsum(g, axis=-1)
    incl = jnp.tril(jnp.ones((CHUNK, CHUNK), dtype=bool))
    strict = jnp.tril(jnp.ones((CHUNK, CHUNK), dtype=bool), -1)
    diff = gc[..., :, None] - gc[..., None, :]
    decay = jnp.where(incl, jnp.exp(jnp.where(incl, diff, 0.0)), 0.0)
    kb = k * beta[..., None]
    lmat = jnp.where(strict, jnp.einsum('nbhcd,nbhsd->nbhcs', kb, k) * decay, 0.0)
    eye = jnp.eye(CHUNK, dtype=lmat.dtype)
    tmat = lax.linalg.triangular_solve(lmat + eye, jnp.broadcast_to(eye, lmat.shape),
                                       left_side=True, lower=True, unit_diagonal=True)
    u = jnp.einsum('nbhcs,nbhsd->nbhcd', tmat, v * beta[..., None])
    w = jnp.einsum('nbhcs,nbhsd->nbhcd', tmat, kb * jnp.exp(gc)[..., None])
    attn = jnp.einsum('nbhcd,nbhsd->nbhcs', q, k) * decay
    q_dec = q * jnp.exp(gc)[..., None]
    k_dec = k * jnp.exp(gc[..., -1:] - gc)[..., None]
    g_last = jnp.exp(gc[..., -1])

    def step(state, xs):
        u_i, w_i, a_i, qd_i, kd_i, gl_i = xs
        v_new = u_i - jnp.einsum('bhcd,bhde->bhce', w_i, state)
        o_i = jnp.einsum('bhcd,bhde->bhce', qd_i, state) + jnp.einsum('bhcs,bhse->bhce', a_i, v_new)
        state = state * gl_i[..., None, None] + jnp.einsum('bhcd,bhce->bhde', kd_i, v_new)
        return state, o_i

    s0 = jnp.zeros((b, h, dk, dv), dtype=jnp.float32)
    _, o = lax.scan(step, s0, (u, w, attn, q_dec, k_dec, g_last))
    return o.transpose(1, 0, 3, 2, 4).reshape(b, t, h, dv)


def _gdn_branch(qkv, z, a, bb, conv_w, a_log, dt_bias, norm_w, w_o):
    bsz, t, _ = qkv.shape
    out_dtype = qkv.dtype
    qkv = jax.nn.silu(_centred_dwconv(qkv, conv_w)).astype(jnp.float32)
    q, k, v = jnp.split(qkv, 3, axis=-1)
    q = _l2norm(q.reshape(bsz, t, N_HEADS, HEAD_DIM)) * (HEAD_DIM ** -0.5)
    k = _l2norm(k.reshape(bsz, t, N_HEADS, HEAD_DIM))
    v = v.reshape(bsz, t, N_HEADS, HEAD_DIM)
    a = a.astype(jnp.float32).reshape(bsz, t, 2, N_HEADS)
    bb = bb.astype(jnp.float32).reshape(bsz, t, 2, N_HEADS)
    g = -jnp.exp(a_log.astype(jnp.float32)) * jax.nn.softplus(a + dt_bias.astype(jnp.float32))
    beta = jax.nn.sigmoid(bb)
    o_fwd = _gated_delta_chunked(q, k, v, g[:, :, 0], beta[:, :, 0])
    fl = lambda arr: jnp.flip(arr, axis=1)
    o_bwd = fl(_gated_delta_chunked(fl(q), fl(k), fl(v), fl(g[:, :, 1]), fl(beta[:, :, 1])))
    o = o_fwd + o_bwd
    zf = z.astype(jnp.float32).reshape(bsz, t, N_HEADS, HEAD_DIM)
    o = (o * lax.rsqrt(jnp.mean(jnp.square(o), axis=-1, keepdims=True) + RMS_EPS)
         * norm_w.astype(jnp.float32) * jax.nn.silu(zf))
    return o.reshape(bsz, t, GDN_WIDTH).astype(out_dtype) @ w_o


def _shortconv_branch(sc_b, sc_c, sc_x, conv_w, w_o):
    return (sc_b * _centred_dwconv(sc_c * sc_x, conv_w)) @ w_o


def _fwd_setup_inputs(seed: int = 0) -> dict:
    key = jax.random.key(seed)
    ks = jax.random.split(key, 20)
    f32 = jnp.float32
    nrm = lambda k, shape, s: jax.random.normal(k, shape, f32) * s
    x = jax.random.normal(ks[0], (BATCH, SEQ, D_MODEL), f32)
    w_in = nrm(ks[1], (DEPTH, D_MODEL, IN_COLS), D_MODEL ** -0.5)
    conv_qkv = nrm(ks[2], (DEPTH, CONV_W, 3 * GDN_WIDTH), CONV_W ** -0.5)
    a_log = jnp.log(jax.random.uniform(ks[3], (DEPTH, 2, N_HEADS), f32, 1.0, 16.0))
    dt = jnp.exp(jax.random.uniform(ks[4], (DEPTH, 2, N_HEADS), f32,
                                    math_log(1e-3), math_log(1e-1)))
    dt_bias = dt + jnp.log(-jnp.expm1(-dt))
    gdn_norm_w = 1.0 + nrm(ks[5], (DEPTH, HEAD_DIM), 0.02)
    w_o_gdn = nrm(ks[6], (DEPTH, GDN_WIDTH, D_MODEL), GDN_WIDTH ** -0.5)
    conv_sc = nrm(ks[7], (DEPTH, CONV_W, SC_WIDTH), CONV_W ** -0.5)
    w_o_sc = nrm(ks[8], (DEPTH, SC_WIDTH, D_MODEL), SC_WIDTH ** -0.5)
    w_out = nrm(ks[9], (DEPTH, D_MODEL, D_MODEL), DEEPNORM_BETA * D_MODEL ** -0.5)
    ln1_g = 1.0 + nrm(ks[10], (DEPTH, D_MODEL), 0.02)
    ln1_b = nrm(ks[11], (DEPTH, D_MODEL), 0.02)
    w_up = nrm(ks[12], (DEPTH, D_MODEL, D_FF), D_MODEL ** -0.5)
    b_up = nrm(ks[13], (DEPTH, D_FF), 0.02)
    w_down = nrm(ks[14], (DEPTH, D_FF, D_MODEL), DEEPNORM_BETA * D_FF ** -0.5)
    b_down = nrm(ks[15], (DEPTH, D_MODEL), 0.02)
    ln2_g = 1.0 + nrm(ks[16], (DEPTH, D_MODEL), 0.02)
    ln2_b = nrm(ks[17], (DEPTH, D_MODEL), 0.02)
    return {'x': x, 'w_in': w_in, 'conv_qkv': conv_qkv, 'a_log': a_log, 'dt_bias': dt_bias,
            'gdn_norm_w': gdn_norm_w, 'w_o_gdn': w_o_gdn, 'conv_sc': conv_sc, 'w_o_sc': w_o_sc,
            'w_out': w_out, 'ln1_g': ln1_g, 'ln1_b': ln1_b, 'w_up': w_up, 'b_up': b_up,
            'w_down': w_down, 'b_down': b_down, 'ln2_g': ln2_g, 'ln2_b': ln2_b}


def math_log(v):
    return float(np.log(v))


def _fwd_reference(x, w_in, conv_qkv, a_log, dt_bias, gdn_norm_w, w_o_gdn, conv_sc, w_o_sc,
              w_out, ln1_g, ln1_b, w_up, b_up, w_down, b_down, ln2_g, ln2_b):
    for l in range(DEPTH):
        proj = x @ w_in[l]
        qkv, z, a, bb, sc_b, sc_c, sc_x, gate_a, gate_b = _split_in(proj)
        y_a = _gdn_branch(qkv, z, a, bb, conv_qkv[l], a_log[l], dt_bias[l],
                          gdn_norm_w[l], w_o_gdn[l])
        y_b = _shortconv_branch(sc_b, sc_c, sc_x, conv_sc[l], w_o_sc[l])
        mixed = jax.nn.sigmoid(gate_a) * y_a + jax.nn.sigmoid(gate_b) * y_b
        x = _layer_norm(DEEPNORM_ALPHA * x + mixed @ w_out[l], ln1_g[l], ln1_b[l])
        h = jnp.square(jax.nn.relu(x @ w_up[l] + b_up[l]))
        x = _layer_norm(DEEPNORM_ALPHA * x + h @ w_down[l] + b_down[l], ln2_g[l], ln2_b[l])
    return x


import jax as _jax
import jax.numpy as _jnp

TWIN_FORMAT = 'train_step'
FWD_PARAMS = ['x', 'w_in', 'conv_qkv', 'a_log', 'dt_bias', 'gdn_norm_w', 'w_o_gdn', 'conv_sc', 'w_o_sc', 'w_out', 'ln1_g', 'ln1_b', 'w_up', 'b_up', 'w_down', 'b_down', 'ln2_g', 'ln2_b']
TWIN_WEIGHTS = ['w_in', 'conv_qkv', 'a_log', 'dt_bias', 'gdn_norm_w', 'w_o_gdn', 'conv_sc', 'w_o_sc', 'w_out', 'ln1_g', 'ln1_b', 'w_up', 'b_up', 'w_down', 'b_down', 'ln2_g', 'ln2_b']
TWIN_DIFF_INPUT = 'x'
TWIN_INPUTS = ['x', 'w_in', 'conv_qkv', 'a_log', 'dt_bias', 'gdn_norm_w', 'w_o_gdn', 'conv_sc', 'w_o_sc', 'w_out', 'ln1_g', 'ln1_b', 'w_up', 'b_up', 'w_down', 'b_down', 'ln2_g', 'ln2_b', 'loss_target', 'm_w_in', 'm_conv_qkv', 'm_a_log', 'm_dt_bias', 'm_gdn_norm_w', 'm_w_o_gdn', 'm_conv_sc', 'm_w_o_sc', 'm_w_out', 'm_ln1_g', 'm_ln1_b', 'm_w_up', 'm_b_up', 'm_w_down', 'm_b_down', 'm_ln2_g', 'm_ln2_b', 'v_w_in', 'v_conv_qkv', 'v_a_log', 'v_dt_bias', 'v_gdn_norm_w', 'v_w_o_gdn', 'v_conv_sc', 'v_w_o_sc', 'v_w_out', 'v_ln1_g', 'v_ln1_b', 'v_w_up', 'v_b_up', 'v_w_down', 'v_b_down', 'v_ln2_g', 'v_ln2_b']
TWIN_OUTPUTS = ['loss', 'grad_x', 'grad_w_in', 'grad_conv_qkv', 'grad_a_log', 'grad_dt_bias', 'grad_gdn_norm_w', 'grad_w_o_gdn', 'grad_conv_sc', 'grad_w_o_sc', 'grad_w_out', 'grad_ln1_g', 'grad_ln1_b', 'grad_w_up', 'grad_b_up', 'grad_w_down', 'grad_b_down', 'grad_ln2_g', 'grad_ln2_b', 'delta_w_in', 'delta_conv_qkv', 'delta_a_log', 'delta_dt_bias', 'delta_gdn_norm_w', 'delta_w_o_gdn', 'delta_conv_sc', 'delta_w_o_sc', 'delta_w_out', 'delta_ln1_g', 'delta_ln1_b', 'delta_w_up', 'delta_b_up', 'delta_w_down', 'delta_b_down', 'delta_ln2_g', 'delta_ln2_b', 'new_m_w_in', 'new_m_conv_qkv', 'new_m_a_log', 'new_m_dt_bias', 'new_m_gdn_norm_w', 'new_m_w_o_gdn', 'new_m_conv_sc', 'new_m_w_o_sc', 'new_m_w_out', 'new_m_ln1_g', 'new_m_ln1_b', 'new_m_w_up', 'new_m_b_up', 'new_m_w_down', 'new_m_b_down', 'new_m_ln2_g', 'new_m_ln2_b', 'new_v_w_in', 'new_v_conv_qkv', 'new_v_a_log', 'new_v_dt_bias', 'new_v_gdn_norm_w', 'new_v_w_o_gdn', 'new_v_conv_sc', 'new_v_w_o_sc', 'new_v_w_out', 'new_v_ln1_g', 'new_v_ln1_b', 'new_v_w_up', 'new_v_b_up', 'new_v_w_down', 'new_v_b_down', 'new_v_ln2_g', 'new_v_ln2_b']
TWIN_LEAF_KINDS = {'loss': 'loss', 'grad_x': 'grad_x', 'grad_w_in': 'grad_w', 'grad_conv_qkv': 'grad_w', 'grad_a_log': 'grad_w', 'grad_dt_bias': 'grad_w', 'grad_gdn_norm_w': 'grad_w', 'grad_w_o_gdn': 'grad_w', 'grad_conv_sc': 'grad_w', 'grad_w_o_sc': 'grad_w', 'grad_w_out': 'grad_w', 'grad_ln1_g': 'grad_w', 'grad_ln1_b': 'grad_w', 'grad_w_up': 'grad_w', 'grad_b_up': 'grad_w', 'grad_w_down': 'grad_w', 'grad_b_down': 'grad_w', 'grad_ln2_g': 'grad_w', 'grad_ln2_b': 'grad_w', 'delta_w_in': 'delta_w', 'delta_conv_qkv': 'delta_w', 'delta_a_log': 'delta_w', 'delta_dt_bias': 'delta_w', 'delta_gdn_norm_w': 'delta_w', 'delta_w_o_gdn': 'delta_w', 'delta_conv_sc': 'delta_w', 'delta_w_o_sc': 'delta_w', 'delta_w_out': 'delta_w', 'delta_ln1_g': 'delta_w', 'delta_ln1_b': 'delta_w', 'delta_w_up': 'delta_w', 'delta_b_up': 'delta_w', 'delta_w_down': 'delta_w', 'delta_b_down': 'delta_w', 'delta_ln2_g': 'delta_w', 'delta_ln2_b': 'delta_w', 'new_m_w_in': 'new_m', 'new_m_conv_qkv': 'new_m', 'new_m_a_log': 'new_m', 'new_m_dt_bias': 'new_m', 'new_m_gdn_norm_w': 'new_m', 'new_m_w_o_gdn': 'new_m', 'new_m_conv_sc': 'new_m', 'new_m_w_o_sc': 'new_m', 'new_m_w_out': 'new_m', 'new_m_ln1_g': 'new_m', 'new_m_ln1_b': 'new_m', 'new_m_w_up': 'new_m', 'new_m_b_up': 'new_m', 'new_m_w_down': 'new_m', 'new_m_b_down': 'new_m', 'new_m_ln2_g': 'new_m', 'new_m_ln2_b': 'new_m', 'new_v_w_in': 'new_v', 'new_v_conv_qkv': 'new_v', 'new_v_a_log': 'new_v', 'new_v_dt_bias': 'new_v', 'new_v_gdn_norm_w': 'new_v', 'new_v_w_o_gdn': 'new_v', 'new_v_conv_sc': 'new_v', 'new_v_w_o_sc': 'new_v', 'new_v_w_out': 'new_v', 'new_v_ln1_g': 'new_v', 'new_v_ln1_b': 'new_v', 'new_v_w_up': 'new_v', 'new_v_b_up': 'new_v', 'new_v_w_down': 'new_v', 'new_v_b_down': 'new_v', 'new_v_ln2_g': 'new_v', 'new_v_ln2_b': 'new_v'}


def _forward(args):
    return _fwd_reference(*[args[k] for k in FWD_PARAMS])


def _output_shape():
    def fwd():
        inp = _fwd_setup_inputs(0)
        return _fwd_reference(*[inp[k] for k in FWD_PARAMS])
    out = _jax.eval_shape(fwd)
    return out.shape, out.dtype

N_MICROBATCH = 1
ADAM_LR = 0.001
ADAM_B1 = 0.9
ADAM_B2 = 0.999
ADAM_EPS = 1e-08
ADAM_WD = 0.01
ADAM_STEP = 10
PER_EXAMPLE_BATCH_AXIS = {'x': 0, 'loss_target': 0}
SHARED_INPUTS = []
_WEIGHT_DTYPES = {'w_in': _jnp.float32, 'conv_qkv': _jnp.float32, 'a_log': _jnp.float32, 'dt_bias': _jnp.float32, 'gdn_norm_w': _jnp.float32, 'w_o_gdn': _jnp.float32, 'conv_sc': _jnp.float32, 'w_o_sc': _jnp.float32, 'w_out': _jnp.float32, 'ln1_g': _jnp.float32, 'ln1_b': _jnp.float32, 'w_up': _jnp.float32, 'b_up': _jnp.float32, 'w_down': _jnp.float32, 'b_down': _jnp.float32, 'ln2_g': _jnp.float32, 'ln2_b': _jnp.float32}
MOMENT_SCALE = {'w_in': 3.449811e-02, 'conv_qkv': 2.517994e-02, 'a_log': 1.260905e-01, 'dt_bias': 1.241743e-01, 'gdn_norm_w': 1.083761e-01, 'w_o_gdn': 3.761926e-02, 'conv_sc': 5.107935e-02, 'w_o_sc': 5.065687e-02, 'w_out': 1.489107e-01, 'ln1_g': 3.115284e+00, 'ln1_b': 1.773132e+00, 'w_up': 6.573089e-02, 'b_up': 1.495988e-01, 'w_down': 4.538545e-01, 'b_down': 1.048813e+00, 'ln2_g': 6.435243e+01, 'ln2_b': 1.446274e+01}


def _to_microbatches(a, axis):
    t = _jnp.moveaxis(a, axis, 0)
    t = t.reshape((N_MICROBATCH, t.shape[0] // N_MICROBATCH) + t.shape[1:])
    return _jnp.moveaxis(t, 1, axis + 1)


def setup_inputs(seed: int = 0) -> dict:
    inp = _fwd_setup_inputs(seed)
    key = _jax.random.fold_in(_jax.random.key(seed), 7919)
    shape, _ = _output_shape()
    out = dict(inp)
    out["loss_target"] = _jax.random.normal(_jax.random.fold_in(key, 0), shape, _jnp.float32)
    for i, name in enumerate(TWIN_WEIGHTS):
        w = inp[name].astype(_jnp.float32)
        if MOMENT_SCALE is None:
            s = _jnp.sqrt(_jnp.mean(_jnp.square(w)) + 1e-30)
        else:
            s = MOMENT_SCALE[name]
        km, kv = _jax.random.split(_jax.random.fold_in(key, i + 1))
        out[name] = w
        out["m_" + name] = s * _jax.random.normal(km, w.shape, _jnp.float32)
        out["v_" + name] = (s * s) * _jax.random.uniform(kv, w.shape, _jnp.float32, 0.5, 1.5)
    if N_MICROBATCH > 1:
        for name, axis in PER_EXAMPLE_BATCH_AXIS.items():
            out[name] = _to_microbatches(out[name], axis)
    return {'x': out['x'], 'w_in': out['w_in'], 'conv_qkv': out['conv_qkv'], 'a_log': out['a_log'], 'dt_bias': out['dt_bias'], 'gdn_norm_w': out['gdn_norm_w'], 'w_o_gdn': out['w_o_gdn'], 'conv_sc': out['conv_sc'], 'w_o_sc': out['w_o_sc'], 'w_out': out['w_out'], 'ln1_g': out['ln1_g'], 'ln1_b': out['ln1_b'], 'w_up': out['w_up'], 'b_up': out['b_up'], 'w_down': out['w_down'], 'b_down': out['b_down'], 'ln2_g': out['ln2_g'], 'ln2_b': out['ln2_b'], 'loss_target': out['loss_target'], 'm_w_in': out['m_w_in'], 'm_conv_qkv': out['m_conv_qkv'], 'm_a_log': out['m_a_log'], 'm_dt_bias': out['m_dt_bias'], 'm_gdn_norm_w': out['m_gdn_norm_w'], 'm_w_o_gdn': out['m_w_o_gdn'], 'm_conv_sc': out['m_conv_sc'], 'm_w_o_sc': out['m_w_o_sc'], 'm_w_out': out['m_w_out'], 'm_ln1_g': out['m_ln1_g'], 'm_ln1_b': out['m_ln1_b'], 'm_w_up': out['m_w_up'], 'm_b_up': out['m_b_up'], 'm_w_down': out['m_w_down'], 'm_b_down': out['m_b_down'], 'm_ln2_g': out['m_ln2_g'], 'm_ln2_b': out['m_ln2_b'], 'v_w_in': out['v_w_in'], 'v_conv_qkv': out['v_conv_qkv'], 'v_a_log': out['v_a_log'], 'v_dt_bias': out['v_dt_bias'], 'v_gdn_norm_w': out['v_gdn_norm_w'], 'v_w_o_gdn': out['v_w_o_gdn'], 'v_conv_sc': out['v_conv_sc'], 'v_w_o_sc': out['v_w_o_sc'], 'v_w_out': out['v_w_out'], 'v_ln1_g': out['v_ln1_g'], 'v_ln1_b': out['v_ln1_b'], 'v_w_up': out['v_w_up'], 'v_b_up': out['v_b_up'], 'v_w_down': out['v_w_down'], 'v_b_down': out['v_b_down'], 'v_ln2_g': out['v_ln2_g'], 'v_ln2_b': out['v_ln2_b']}


def _loss(weights, diff, rest, loss_target):
    with _jax.named_scope("forward"):
        args = {**rest, TWIN_DIFF_INPUT: diff, **{k: w.astype(_WEIGHT_DTYPES[k]) for k, w in weights.items()}}
        y = _forward(args)
    with _jax.named_scope("loss_head"):
        err = _jnp.square(y.astype(_jnp.float32) - loss_target)
        return 0.5 * _jnp.sum(_jnp.mean(err, axis=-1)) if err.ndim else 0.5 * err


def _adamw(w, g, m, v):
    m = ADAM_B1 * m + (1.0 - ADAM_B1) * g
    v = ADAM_B2 * v + (1.0 - ADAM_B2) * _jnp.square(g)
    m_hat = m / (1.0 - ADAM_B1 ** ADAM_STEP)
    v_hat = v / (1.0 - ADAM_B2 ** ADAM_STEP)
    delta = -ADAM_LR * (m_hat / (_jnp.sqrt(v_hat) + ADAM_EPS) + ADAM_WD * w)
    return delta, m, v


def reference(x, w_in, conv_qkv, a_log, dt_bias, gdn_norm_w, w_o_gdn, conv_sc, w_o_sc, w_out, ln1_g, ln1_b, w_up, b_up, w_down, b_down, ln2_g, ln2_b, loss_target, m_w_in, m_conv_qkv, m_a_log, m_dt_bias, m_gdn_norm_w, m_w_o_gdn, m_conv_sc, m_w_o_sc, m_w_out, m_ln1_g, m_ln1_b, m_w_up, m_b_up, m_w_down, m_b_down, m_ln2_g, m_ln2_b, v_w_in, v_conv_qkv, v_a_log, v_dt_bias, v_gdn_norm_w, v_w_o_gdn, v_conv_sc, v_w_o_sc, v_w_out, v_ln1_g, v_ln1_b, v_w_up, v_b_up, v_w_down, v_b_down, v_ln2_g, v_ln2_b):
    given = dict(x=x, w_in=w_in, conv_qkv=conv_qkv, a_log=a_log, dt_bias=dt_bias, gdn_norm_w=gdn_norm_w, w_o_gdn=w_o_gdn, conv_sc=conv_sc, w_o_sc=w_o_sc, w_out=w_out, ln1_g=ln1_g, ln1_b=ln1_b, w_up=w_up, b_up=b_up, w_down=w_down, b_down=b_down, ln2_g=ln2_g, ln2_b=ln2_b, loss_target=loss_target, m_w_in=m_w_in, m_conv_qkv=m_conv_qkv, m_a_log=m_a_log, m_dt_bias=m_dt_bias, m_gdn_norm_w=m_gdn_norm_w, m_w_o_gdn=m_w_o_gdn, m_conv_sc=m_conv_sc, m_w_o_sc=m_w_o_sc, m_w_out=m_w_out, m_ln1_g=m_ln1_g, m_ln1_b=m_ln1_b, m_w_up=m_w_up, m_b_up=m_b_up, m_w_down=m_w_down, m_b_down=m_b_down, m_ln2_g=m_ln2_g, m_ln2_b=m_ln2_b, v_w_in=v_w_in, v_conv_qkv=v_conv_qkv, v_a_log=v_a_log, v_dt_bias=v_dt_bias, v_gdn_norm_w=v_gdn_norm_w, v_w_o_gdn=v_w_o_gdn, v_conv_sc=v_conv_sc, v_w_o_sc=v_w_o_sc, v_w_out=v_w_out, v_ln1_g=v_ln1_g, v_ln1_b=v_ln1_b, v_w_up=v_w_up, v_b_up=v_b_up, v_w_down=v_w_down, v_b_down=v_b_down, v_ln2_g=v_ln2_g, v_ln2_b=v_ln2_b)
    weights = {n: given[n] for n in TWIN_WEIGHTS}
    shared = {n: given[n] for n in SHARED_INPUTS}
    per_example = {n: given[n] for n in ['x']}
    grad_fn = _jax.value_and_grad(_loss, argnums=(0, 1))

    def one_microbatch(ex, loss_target):
        ex = dict(ex)
        diff = ex.pop(TWIN_DIFF_INPUT)
        return grad_fn(weights, diff, {**shared, **ex}, loss_target)

    if N_MICROBATCH == 1:
        loss, (grad_w, grad_x) = one_microbatch(per_example, given["loss_target"])
    else:
        def body(carry, xs):
            loss_sum, grad_sum = carry
            l_k, (gw_k, gx_k) = one_microbatch(xs[0], xs[1])
            with _jax.named_scope("update"):
                return (loss_sum + l_k, _jax.tree.map(_jnp.add, grad_sum, gw_k)), gx_k

        init = (_jnp.zeros((), _jnp.float32), _jax.tree.map(_jnp.zeros_like, weights))
        (loss, grad_w), grad_x = _jax.lax.scan(body, init, (per_example, given["loss_target"]))
    with _jax.named_scope("update"):
        delta_w, new_m, new_v = {}, {}, {}
        for n in TWIN_WEIGHTS:
            delta_w[n], new_m[n], new_v[n] = _adamw(weights[n], grad_w[n], given["m_" + n], given["v_" + n])
    return (loss, grad_x, *[grad_w[n] for n in TWIN_WEIGHTS], *[delta_w[n] for n in TWIN_WEIGHTS],
            *[new_m[n] for n in TWIN_WEIGHTS], *[new_v[n] for n in TWIN_WEIGHTS])
```

```python
import functools

import jax
import jax.numpy as jnp
from jax import lax
from jax.experimental import pallas as pl
from jax.experimental.pallas import tpu as pltpu

F32 = jnp.float32
MM = jnp.bfloat16
HI = lax.Precision.HIGHEST

D = 1024
NH = 8
HD = 128
CH = 64
DFF = 4 * D
DEPTH = 4
LN_EPS = 1e-5
RMS_EPS = 1e-6
L2_EPS = 1e-6
ALPHA = (2 * DEPTH) ** 0.25
LR, B1, B2, EPS, WD, STEP = 0.001, 0.9, 0.999, 1e-08, 0.01, 10

NMAIN = 9 * D
CQ, CK, CV, CSB, CSC, CSX, CGA, CGB, CZ = range(9)
HALO = 8
VMEM_LIMIT = 56 * 1024 * 1024
MESH = pl.DeviceIdType.MESH


def _cp(sem=None, vmem=VMEM_LIMIT):
    return pltpu.CompilerParams(dimension_semantics=sem, vmem_limit_bytes=vmem)


def _sds(shape, dtype=F32):
    return jax.ShapeDtypeStruct(tuple(shape), dtype)


def mm_nn(a, b, *, name, bias=None, relu2=False, add=None, out_dtype=F32, tm=1024, tn=1024, tk=1024):
    M, K = a.shape
    N = b.shape[1]
    tm, tn, tk = min(tm, M), min(tn, N), min(tk, K)
    nk = K // tk

    def body(*refs):
        it = iter(refs)
        a_ref, b_ref = next(it), next(it)
        bias_ref = next(it) if bias is not None else None
        add_ref = next(it) if add is not None else None
        o_ref = next(it)
        h_ref = next(it) if relu2 else None
        acc = next(it)
        k = pl.program_id(2)

        @pl.when(k == 0)
        def _():
            acc[...] = jnp.zeros_like(acc)

        acc[...] += jnp.dot(a_ref[...].astype(MM), b_ref[...].astype(MM), preferred_element_type=F32)

        @pl.when(k == nk - 1)
        def _():
            r = acc[...]
            if bias_ref is not None:
                r = r + bias_ref[...]
            if add_ref is not None:
                r = r + add_ref[...]
            o_ref[...] = r.astype(o_ref.dtype)
            if relu2:
                t = jnp.maximum(r, 0.0)
                h_ref[...] = (t * t).astype(h_ref.dtype)

    in_specs = [pl.BlockSpec((tm, tk), lambda i, j, k: (i, k)), pl.BlockSpec((tk, tn), lambda i, j, k: (k, j))]
    args = [a, b]
    if bias is not None:
        in_specs.append(pl.BlockSpec((1, tn), lambda i, j, k: (0, j)))
        args.append(bias)
    if add is not None:
        in_specs.append(pl.BlockSpec((tm, tn), lambda i, j, k: (i, j)))
        args.append(add)
    out_shape = [_sds((M, N), out_dtype)]
    out_specs = [pl.BlockSpec((tm, tn), lambda i, j, k: (i, j))]
    if relu2:
        out_shape.append(_sds((M, N), MM))
        out_specs.append(pl.BlockSpec((tm, tn), lambda i, j, k: (i, j)))
    res = pl.pallas_call(
        body, name=name, grid=(M // tm, N // tn, nk), in_specs=in_specs, out_specs=out_specs, out_shape=out_shape,
        scratch_shapes=[pltpu.VMEM((tm, tn), F32)],
        compiler_params=_cp(("parallel", "parallel", "arbitrary")))(*args)
    return res if relu2 else res[0]


def mm_nt(a, b, *, name, add=None, dact=None, out_dtype=F32, tm=1024, tn=1024, tk=1024):
    M, Nc = a.shape
    Ko = b.shape[0]
    tm, tn, tk = min(tm, M), min(tn, Ko), min(tk, Nc)
    nk = Nc // tk
    ni = M // tm

    def body(*refs):
        it = iter(refs)
        a_ref, b_ref = next(it), next(it)
        add_ref = next(it) if add is not None else None
        d_ref = next(it) if dact is not None else None
        o_ref = next(it)
        db_ref = next(it) if dact is not None else None
        acc = next(it)
        i, k = pl.program_id(1), pl.program_id(2)

        @pl.when(k == 0)
        def _():
            acc[...] = jnp.zeros_like(acc)

        acc[...] += lax.dot_general(a_ref[...].astype(MM), b_ref[...].astype(MM), (((1,), (1,)), ((), ())),
                                    preferred_element_type=F32)

        @pl.when(k == nk - 1)
        def _():
            r = acc[...]
            if add_ref is not None:
                r = r + add_ref[...]
            if d_ref is not None:
                r = r * (2.0 * jnp.maximum(d_ref[...], 0.0))
                s = jnp.sum(r, axis=0, keepdims=True)
                row0 = lax.broadcasted_iota(jnp.int32, db_ref.shape, 0) == 0

                @pl.when(i == 0)
                def _():
                    db_ref[...] = jnp.zeros_like(db_ref)

                db_ref[...] += jnp.where(row0, s, 0.0)
            o_ref[...] = r.astype(o_ref.dtype)

    in_specs = [pl.BlockSpec((tm, tk), lambda j, i, k: (i, k)), pl.BlockSpec((tn, tk), lambda j, i, k: (j, k))]
    args = [a, b]
    for extra in (add, dact):
        if extra is not None:
            in_specs.append(pl.BlockSpec((tm, tn), lambda j, i, k: (i, j)))
            args.append(extra)
    out_shape = [_sds((M, Ko), out_dtype)]
    out_specs = [pl.BlockSpec((tm, tn), lambda j, i, k: (i, j))]
    if dact is not None:
        out_shape.append(_sds((8, Ko), F32))
        out_specs.append(pl.BlockSpec((8, tn), lambda j, i, k: (0, j)))
    res = pl.pallas_call(
        body, name=name, grid=(Ko // tn, ni, nk), in_specs=in_specs, out_specs=out_specs, out_shape=out_shape,
        scratch_shapes=[pltpu.VMEM((tm, tn), F32)],
        compiler_params=_cp(("parallel", "arbitrary", "arbitrary")))(*args)
    return res if dact is not None else res[0]


def mm_tn(a, b, *, name, tm=1024, tn=1024, tk=512):
    T, M = a.shape
    N = b.shape[1]
    tm, tn, tk = min(tm, M), min(tn, N), min(tk, T)

    def body(a_ref, b_ref, o_ref):
        @pl.when(pl.program_id(2) == 0)
        def _():
            o_ref[...] = jnp.zeros_like(o_ref)

        o_ref[...] += lax.dot_general(a_ref[...].astype(MM), b_ref[...].astype(MM), (((0,), (0,)), ((), ())),
                                      preferred_element_type=F32)

    return pl.pallas_call(
        body, name=name, grid=(M // tm, N // tn, T // tk),
        in_specs=[pl.BlockSpec((tk, tm), lambda i, j, k: (k, i)), pl.BlockSpec((tk, tn), lambda i, j, k: (k, j))],
        out_specs=pl.BlockSpec((tm, tn), lambda i, j, k: (i, j)), out_shape=_sds((M, N)),
        compiler_params=_cp(("parallel", "parallel", "arbitrary")))(a, b)


def _sigmoid(x):
    return 1.0 / (1.0 + jnp.exp(-x))


def _silu(x):
    return x * _sigmoid(x)


def _softplus(x):
    return jnp.maximum(x, 0.0) + jnp.log1p(jnp.exp(-jnp.abs(x)))


def _ext(main_ref, prev_ref, next_ref, first, last):
    p = jnp.where(first, 0.0, prev_ref[...].astype(F32))
    n = jnp.where(last, 0.0, next_ref[...].astype(F32))
    return jnp.concatenate([p, main_ref[...].astype(F32), n], axis=0)


def _shift_dn(x):
    return pltpu.roll(x, 1, 0)


def _shift_up(x):
    return pltpu.roll(x, x.shape[0] - 1, 0)


def _conv3(xe, w):
    return w[0:1, :] * _shift_dn(xe) + w[1:2, :] * xe + w[2:3, :] * _shift_up(xe)


def _conv3_t(de, w):
    return w[0:1, :] * _shift_up(de) + w[1:2, :] * de + w[2:3, :] * _shift_dn(de)


def _halo_specs(bt, T, col, lead=None):
    r = bt // HALO
    last = T // HALO - 1
    if lead is None:
        return [pl.BlockSpec((bt, D), lambda i: (i, col)),
                pl.BlockSpec((HALO, D), lambda i: (jnp.maximum(i * r - 1, 0), col)),
                pl.BlockSpec((HALO, D), lambda i: (jnp.minimum((i + 1) * r, last), col))]
    return [pl.BlockSpec((lead, bt, D), lambda i: (0, i, col)),
            pl.BlockSpec((lead, HALO, D), lambda i: (0, jnp.maximum(i * r - 1, 0), col)),
            pl.BlockSpec((lead, HALO, D), lambda i: (0, jnp.minimum((i + 1) * r, last), col))]


def _qkv_rows(cq, ck, cv):
    sq, sk, sv = _silu(cq), _silu(ck), _silu(cv)
    qs, ks = [], []
    for h in range(NH):
        s = slice(h * HD, (h + 1) * HD)
        qh, kh = sq[:, s], sk[:, s]
        qs.append(qh * lax.rsqrt(jnp.sum(qh * qh, axis=-1, keepdims=True) + L2_EPS) * (HD ** -0.5))
        ks.append(kh * lax.rsqrt(jnp.sum(kh * kh, axis=-1, keepdims=True) + L2_EPS))
    return jnp.concatenate(qs, axis=1), jnp.concatenate(ks, axis=1), sv


def _chunk_masks(bt):
    row = lax.broadcasted_iota(jnp.int32, (bt, bt), 0)
    col = lax.broadcasted_iota(jnp.int32, (bt, bt), 1)
    same = (row // CH) == (col // CH)
    lower = jnp.where(same & (col <= row), 1.0, 0.0).astype(F32)
    upper = jnp.where(same & (col >= row), 1.0, 0.0).astype(F32)
    return lower, upper


def _gate_rows(ab, gp, lower, upper):
    lane = lax.broadcasted_iota(jnp.int32, ab.shape, 1)
    g = -jnp.exp(gp[0:1, :]) * _softplus(ab + gp[1:2, :])
    g = jnp.where(lane < 2 * NH, g, 0.0)
    gf = jnp.dot(lower, g, precision=HI, preferred_element_type=F32)
    gr = jnp.dot(upper, g, precision=HI, preferred_element_type=F32)
    gc = jnp.where(lane < NH, gf, gr)
    beta = _sigmoid(ab)
    return jnp.where(lane < 2 * NH, gc, jnp.where(lane < 4 * NH, beta, 0.0))


def pre_qkv_fwd(pm, pab, cw, gp, *, bt=256):
    T = pm.shape[0]
    bt = min(bt, T)
    n = T // bt

    def body(q0, q1, q2, k0, k1, k2, v0, v1, v2, ab_ref, cw_ref, gp_ref, q_ref, k_ref, v_ref, gb_ref):
        i = pl.program_id(0)
        first, last = i == 0, i == n - 1
        cs = []
        for c, (m, p, x) in enumerate(((q0, q1, q2), (k0, k1, k2), (v0, v1, v2))):
            xe = _ext(m, p, x, first, last)
            cs.append(_conv3(xe, cw_ref[:, c * D:(c + 1) * D])[HALO:HALO + bt])
        q, k, v = _qkv_rows(*cs)
        q_ref[...], k_ref[...], v_ref[...] = q, k, v
        lower, upper = _chunk_masks(bt)
        gb_ref[...] = _gate_rows(ab_ref[...], gp_ref[...], lower, upper)

    in_specs = (_halo_specs(bt, T, CQ) + _halo_specs(bt, T, CK) + _halo_specs(bt, T, CV)
                + [pl.BlockSpec((bt, 128), lambda i: (i, 0)), pl.BlockSpec((8, 3 * D), lambda i: (0, 0)),
                   pl.BlockSpec((8, 128), lambda i: (0, 0))])
    row = pl.BlockSpec((bt, D), lambda i: (i, 0))
    return pl.pallas_call(
        body, name="pre_qkv_fwd", grid=(n,), in_specs=in_specs,
        out_specs=[row, row, row, pl.BlockSpec((bt, 128), lambda i: (i, 0))],
        out_shape=[_sds((T, D)), _sds((T, D)), _sds((T, D)), _sds((T, 128))],
        compiler_params=_cp(("parallel",)))(*([pm] * 9), pab, cw, gp)


def pre_qkv_bwd(pm, pab, cw, gp, dq2, dk2, dv2, dgb2, dpm, *, bt=128):
    T = pm.shape[0]
    bt = min(bt, T)
    n = T // bt
    E = bt + 2 * HALO

    def body(*refs):
        it = iter(refs)
        xs = [[next(it) for _ in range(3)] for _ in range(3)]
        ds = [[next(it) for _ in range(3)] for _ in range(3)]
        ab_ref, dgb_ref, cw_ref, gp_ref, _alias = next(it), next(it), next(it), next(it), next(it)
        o_ref, dab_ref, dcw_ref, dgp_ref = (next(it) for _ in range(4))
        i = pl.program_id(0)
        first, last = i == 0, i == n - 1

        @pl.when(first)
        def _():
            dcw_ref[...] = jnp.zeros_like(dcw_ref)
            dgp_ref[...] = jnp.zeros_like(dgp_ref)

        xes = [_ext(*xs[c], first, last) for c in range(3)]
        ces = [_conv3(xes[c], cw_ref[:, c * D:(c + 1) * D]) for c in range(3)]
        cts = []
        for c in range(3):
            m, p, x = ds[c]
            pe = jnp.where(first, 0.0, p[0] + p[1])
            ne = jnp.where(last, 0.0, x[0] + x[1])
            cts.append(jnp.concatenate([pe, m[0] + m[1], ne], axis=0))
        _, vjp = jax.vjp(_qkv_rows, *ces)
        dces = vjp(tuple(cts))
        rowi = lax.broadcasted_iota(jnp.int32, (E, 1), 0)
        central = (rowi >= HALO) & (rowi < HALO + bt)
        row8 = lax.broadcasted_iota(jnp.int32, (8, D), 0)
        for c in range(3):
            w = cw_ref[:, c * D:(c + 1) * D]
            o_ref[:, c * D:(c + 1) * D] = _conv3_t(dces[c], w)[HALO:HALO + bt]
            dc = jnp.where(central, dces[c], 0.0)
            taps = (jnp.sum(dc * _shift_dn(xes[c]), axis=0, keepdims=True),
                    jnp.sum(dc * xes[c], axis=0, keepdims=True),
                    jnp.sum(dc * _shift_up(xes[c]), axis=0, keepdims=True))
            upd = jnp.where(row8 == 0, taps[0], jnp.where(row8 == 1, taps[1], jnp.where(row8 == 2, taps[2], 0.0)))
            dcw_ref[:, c * D:(c + 1) * D] += upd
        lower, upper = _chunk_masks(bt)
        _, gvjp = jax.vjp(lambda ab, gp: _gate_rows(ab, gp, lower, upper), ab_ref[...], gp_ref[...])
        dab, dgp = gvjp(dgb_ref[0] + dgb_ref[1])
        dab_ref[...] = dab
        dgp_ref[...] += dgp

    in_specs = (_halo_specs(bt, T, CQ) + _halo_specs(bt, T, CK) + _halo_specs(bt, T, CV)
                + _halo_specs(bt, T, 0, lead=2) * 3
                + [pl.BlockSpec((bt, 128), lambda i: (i, 0)), pl.BlockSpec((2, bt, 128), lambda i: (0, i, 0)),
                   pl.BlockSpec((8, 3 * D), lambda i: (0, 0)), pl.BlockSpec((8, 128), lambda i: (0, 0)),
                   pl.BlockSpec(memory_space=pl.ANY)])
    out_specs = [pl.BlockSpec((bt, 3 * D), lambda i: (i, 0)), pl.BlockSpec((bt, 128), lambda i: (i, 0)),
                 pl.BlockSpec((8, 3 * D), lambda i: (0, 0)), pl.BlockSpec((8, 128), lambda i: (0, 0))]
    return pl.pallas_call(
        body, name="pre_qkv_bwd", grid=(n,), in_specs=in_specs, out_specs=out_specs,
        out_shape=[_sds(dpm.shape), _sds((T, 128)), _sds((8, 3 * D)), _sds((8, 128))],
        input_output_aliases={len(in_specs) - 1: 0},
        compiler_params=_cp(("arbitrary",)))(
            *([pm] * 9), dq2, dq2, dq2, dk2, dk2, dk2, dv2, dv2, dv2, pab, dgb2, cw, gp, dpm)


def pre_sc_fwd(pm, cw, *, bt=256):
    T = pm.shape[0]
    bt = min(bt, T)
    n = T // bt

    def body(b_ref, c0, c1, c2, x0, x1, x2, cw_ref, o_ref):
        i = pl.program_id(0)
        first, last = i == 0, i == n - 1
        pe = _ext(c0, c1, c2, first, last) * _ext(x0, x1, x2, first, last)
        o_ref[...] = (b_ref[...] * _conv3(pe, cw_ref[...])[HALO:HALO + bt]).astype(o_ref.dtype)

    in_specs = ([pl.BlockSpec((bt, D), lambda i: (i, CSB))] + _halo_specs(bt, T, CSC) + _halo_specs(bt, T, CSX)
                + [pl.BlockSpec((8, D), lambda i: (0, 0))])
    return pl.pallas_call(
        body, name="pre_sc_fwd", grid=(n,), in_specs=in_specs, out_specs=pl.BlockSpec((bt, D), lambda i: (i, 0)),
        out_shape=_sds((T, D), MM), compiler_params=_cp(("parallel",)))(*([pm] * 7), cw)


def pre_sc_bwd(pm, cw, dsc, dpm, *, bt=256):
    T = pm.shape[0]
    bt = min(bt, T)
    n = T // bt
    E = bt + 2 * HALO

    def body(b0, b1, b2, c0, c1, c2, x0, x1, x2, d0, d1, d2, cw_ref, _alias, o_ref, dcw_ref):
        i = pl.program_id(0)
        first, last = i == 0, i == n - 1

        @pl.when(first)
        def _():
            dcw_ref[...] = jnp.zeros_like(dcw_ref)

        ce, xe = _ext(c0, c1, c2, first, last), _ext(x0, x1, x2, first, last)
        pe = ce * xe
        w = cw_ref[...]
        dout = d0[...]
        o_ref[:, 0:D] = dout * _conv3(pe, w)[HALO:HALO + bt]
        dce = _ext(d0, d1, d2, first, last) * _ext(b0, b1, b2, first, last)
        dp = _conv3_t(dce, w)[HALO:HALO + bt]
        o_ref[:, D:2 * D] = dp * x0[...]
        o_ref[:, 2 * D:3 * D] = dp * c0[...]
        rowi = lax.broadcasted_iota(jnp.int32, (E, 1), 0)
        dc = jnp.where((rowi >= HALO) & (rowi < HALO + bt), dce, 0.0)
        row8 = lax.broadcasted_iota(jnp.int32, (8, D), 0)
        taps = (jnp.sum(dc * _shift_dn(pe), axis=0, keepdims=True), jnp.sum(dc * pe, axis=0, keepdims=True),
                jnp.sum(dc * _shift_up(pe), axis=0, keepdims=True))
        dcw_ref[...] += jnp.where(row8 == 0, taps[0], jnp.where(row8 == 1, taps[1], jnp.where(row8 == 2, taps[2], 0.0)))

    dsc_specs = [pl.BlockSpec((bt, D), lambda i: (i, 0)),
                 pl.BlockSpec((HALO, D), lambda i: (jnp.maximum(i * (bt // HALO) - 1, 0), 0)),
                 pl.BlockSpec((HALO, D), lambda i: (jnp.minimum((i + 1) * (bt // HALO), T // HALO - 1), 0))]
    in_specs = (_halo_specs(bt, T, CSB) + _halo_specs(bt, T, CSC) + _halo_specs(bt, T, CSX) + dsc_specs
                + [pl.BlockSpec((8, D), lambda i: (0, 0)), pl.BlockSpec(memory_space=pl.ANY)])
    return pl.pallas_call(
        body, name="pre_sc_bwd", grid=(n,), in_specs=in_specs,
        out_specs=[pl.BlockSpec((bt, 3 * D), lambda i: (i, 1)), pl.BlockSpec((8, D), lambda i: (0, 0))],
        out_shape=[_sds(dpm.shape), _sds((8, D))], input_output_aliases={len(in_specs) - 1: 0},
        compiler_params=_cp(("arbitrary",)))(*([pm] * 9), dsc, dsc, dsc, cw, dpm)


def _bdot(a, b, dims):
    return lax.dot_general(a.astype(MM), b.astype(MM), (dims, ((), ())), preferred_element_type=F32)


def _hdot(a, b, dims):
    return lax.dot_general(a, b, (dims, ((), ())), precision=HI, preferred_element_type=F32)


_NN, _NT, _TN = ((1,), (0,)), ((1,), (1,)), ((0,), (0,))


def _raw_nn(a, b):
    return _bdot(a, b, _NN)


def _raw_nt(a, b):
    return _bdot(a, b, _NT)


def _raw_tn(a, b):
    return _bdot(a, b, _TN)


def _tri_inv_raw(A):
    r = lax.broadcasted_iota(jnp.int32, A.shape, 0)
    c = lax.broadcasted_iota(jnp.int32, A.shape, 1)
    X = jnp.where(r == c, 1.0, 0.0).astype(F32) - A
    P = _hdot(A, A, _NN)
    X = X + _hdot(X, P, _NN)
    for _ in range(4):
        P = _hdot(P, P, _NN)
        X = X + _hdot(X, P, _NN)
    return X


def _make_vjp_ops():
    @jax.custom_vjp
    def nn(a, b):
        return _raw_nn(a, b)

    @jax.custom_vjp
    def nt(a, b):
        return _raw_nt(a, b)

    @jax.custom_vjp
    def tn(a, b):
        return _raw_tn(a, b)

    nn.defvjp(lambda a, b: (_raw_nn(a, b), (a, b)), lambda r, g: (_raw_nt(g, r[1]), _raw_tn(r[0], g)))
    nt.defvjp(lambda a, b: (_raw_nt(a, b), (a, b)), lambda r, g: (_raw_nn(g, r[1]), _raw_tn(g, r[0])))
    tn.defvjp(lambda a, b: (_raw_tn(a, b), (a, b)), lambda r, g: (_raw_nt(r[1], g), _raw_nn(r[0], g)))

    @jax.custom_vjp
    def inv(A):
        return _tri_inv_raw(A)

    def inv_fwd(A):
        X = _tri_inv_raw(A)
        return X, X

    def inv_bwd(X, g):
        return (-_hdot(X, _hdot(g, X, _NT), _TN),)

    inv.defvjp(inv_fwd, inv_bwd)
    return nn, nt, tn, inv


def _chunk_step(S, q, k, v, gcol, bcol, incl, strict, eye, lastsel, ops):
    nn, nt, tn, inv = ops
    grow = jnp.sum(eye * gcol, axis=0, keepdims=True)
    gam = jnp.where(incl, jnp.exp(jnp.where(incl, gcol - grow, 0.0)), 0.0)
    eg = jnp.exp(gcol)
    kb = k * bcol
    A = jnp.where(strict, nt(kb, k) * gam, 0.0)
    Tm = inv(A)
    u = nn(Tm, v * bcol)
    w = nn(Tm, kb * eg)
    P = nt(q, k) * gam
    glast = jnp.sum(jnp.where(lastsel, gcol, 0.0), axis=0, keepdims=True)
    vn = u - nn(w, S)
    o = nn(q * eg, S) + nn(P, vn)
    S2 = S * jnp.exp(glast) + tn(k * jnp.exp(glast - gcol), vn)
    return o, S2


def _dir_masks(rev):
    r = lax.broadcasted_iota(jnp.int32, (CH, CH), 0)
    c = lax.broadcasted_iota(jnp.int32, (CH, CH), 1)
    ahead = jnp.where(rev, c - r, r - c)
    incl = ahead >= 0
    strict = ahead > 0
    eye = jnp.where(r == c, 1.0, 0.0).astype(F32)
    rr = lax.broadcasted_iota(jnp.int32, (CH, 1), 0)
    lastsel = rr == jnp.where(rev, 0, CH - 1)
    return incl, strict, eye, lastsel


def _head_gates(gb, h, rev):
    gcol = jnp.where(rev, gb[:, NH + h:NH + h + 1], gb[:, h:h + 1])
    bcol = jnp.where(rev, gb[:, 3 * NH + h:3 * NH + h + 1], gb[:, 2 * NH + h:2 * NH + h + 1])
    return gcol, bcol


def gdn_fwd(q, k, v, gb):
    T = q.shape[0]
    N = T // CH
    ops = (_raw_nn, _raw_nt, _raw_tn, _tri_inv_raw)

    def body(q_ref, k_ref, v_ref, gb_ref, o_ref, s0_ref, S):
        d, n = pl.program_id(0), pl.program_id(1)
        rev = d == 1

        @pl.when(n == 0)
        def _():
            S[...] = jnp.zeros_like(S)

        masks = _dir_masks(rev)
        gb_t = gb_ref[...]
        for h in range(NH):
            s = slice(h * HD, (h + 1) * HD)
            gcol, bcol = _head_gates(gb_t, h, rev)
            Sh = S[h]
            s0_ref[h] = Sh
            o, S2 = _chunk_step(Sh, q_ref[:, s], k_ref[:, s], v_ref[:, s], gcol, bcol, *masks, ops)
            o_ref[:, s] = o
            S[h] = S2

    cidx = lambda d, n: n + d * (N - 1 - 2 * n)
    row = pl.BlockSpec((CH, D), lambda d, n: (cidx(d, n), 0))
    return pl.pallas_call(
        body, name="gdn_fwd", grid=(2, N),
        in_specs=[row, row, row, pl.BlockSpec((CH, 128), lambda d, n: (cidx(d, n), 0))],
        out_specs=[pl.BlockSpec((None, CH, D), lambda d, n: (d, cidx(d, n), 0)),
                   pl.BlockSpec((None, None, NH, HD, HD), lambda d, n: (d, cidx(d, n), 0, 0, 0))],
        out_shape=[_sds((2, T, D)), _sds((2, N, NH, HD, HD))],
        scratch_shapes=[pltpu.VMEM((NH, HD, HD), F32)],
        compiler_params=_cp(("arbitrary", "arbitrary")))(q, k, v, gb)


def gdn_bwd(q, k, v, gb, s0, do):
    T = q.shape[0]
    N = T // CH
    ops = _make_vjp_ops()

    def body(q_ref, k_ref, v_ref, gb_ref, s0_ref, do_ref, dq_ref, dk_ref, dv_ref, dgb_ref, dS):
        d, n = pl.program_id(0), pl.program_id(1)
        rev = d == 1

        @pl.when(n == 0)
        def _():
            dS[...] = jnp.zeros_like(dS)

        masks = _dir_masks(rev)
        gb_t = gb_ref[...]
        lane = lax.broadcasted_iota(jnp.int32, (CH, 128), 1)
        dgb = jnp.zeros((CH, 128), F32)
        for h in range(NH):
            s = slice(h * HD, (h + 1) * HD)
            gcol, bcol = _head_gates(gb_t, h, rev)
            step = lambda S, q_, k_, v_, g_, b_: _chunk_step(S, q_, k_, v_, g_, b_, *masks, ops)
            _, vjp = jax.vjp(step, s0_ref[h], q_ref[:, s], k_ref[:, s], v_ref[:, s], gcol, bcol)
            dS_h, dq, dk, dv, dg, db = vjp((do_ref[:, s], dS[h]))
            dS[h] = dS_h
            dq_ref[:, s], dk_ref[:, s], dv_ref[:, s] = dq, dk, dv
            glane = jnp.where(rev, NH + h, h)
            dgb = dgb + jnp.where(lane == glane, dg, 0.0) + jnp.where(lane == glane + 2 * NH, db, 0.0)
        dgb_ref[...] = dgb

    cidx = lambda d, n: (N - 1 - n) + d * (2 * n - (N - 1))
    row = pl.BlockSpec((CH, D), lambda d, n: (cidx(d, n), 0))
    drow = pl.BlockSpec((None, CH, D), lambda d, n: (d, cidx(d, n), 0))
    return pl.pallas_call(
        body, name="gdn_bwd", grid=(2, N),
        in_specs=[row, row, row, pl.BlockSpec((CH, 128), lambda d, n: (cidx(d, n), 0)),
                  pl.BlockSpec((None, None, NH, HD, HD), lambda d, n: (d, cidx(d, n), 0, 0, 0)), row],
        out_specs=[drow, drow, drow, pl.BlockSpec((None, CH, 128), lambda d, n: (d, cidx(d, n), 0))],
        out_shape=[_sds((2, T, D))] * 3 + [_sds((2, T, 128))],
        scratch_shapes=[pltpu.VMEM((NH, HD, HD), F32)],
        compiler_params=_cp(("arbitrary", "arbitrary")))(q, k, v, gb, s0, do)


def _post_rows(o2a, o2b, z, nw):
    o = o2a + o2b
    outs = []
    for h in range(NH):
        s = slice(h * HD, (h + 1) * HD)
        oh = o[:, s]
        outs.append(oh * lax.rsqrt(jnp.mean(oh * oh, axis=-1, keepdims=True) + RMS_EPS) * nw * _silu(z[:, s]))
    return jnp.concatenate(outs, axis=1)


def post_fwd(o2, pm, nw, *, bt=512):
    T = pm.shape[0]
    bt = min(bt, T)

    def body(o_ref, z_ref, nw_ref, og_ref):
        og_ref[...] = _post_rows(o_ref[0], o_ref[1], z_ref[...], nw_ref[0:1, :]).astype(og_ref.dtype)

    return pl.pallas_call(
        body, name="post_fwd", grid=(T // bt,),
        in_specs=[pl.BlockSpec((2, bt, D), lambda i: (0, i, 0)), pl.BlockSpec((bt, D), lambda i: (i, CZ)),
                  pl.BlockSpec((8, 128), lambda i: (0, 0))],
        out_specs=pl.BlockSpec((bt, D), lambda i: (i, 0)), out_shape=_sds((T, D), MM),
        compiler_params=_cp(("parallel",)))(o2, pm, nw)


def post_bwd(o2, pm, nw, dog, dpm, *, bt=512):
    T = pm.shape[0]
    bt = min(bt, T)

    def body(o_ref, z_ref, nw_ref, dog_ref, _alias, do_ref, dz_ref, dnw_ref):
        @pl.when(pl.program_id(0) == 0)
        def _():
            dnw_ref[...] = jnp.zeros_like(dnw_ref)

        _, vjp = jax.vjp(_post_rows, o_ref[0], o_ref[1], z_ref[...], nw_ref[0:1, :])
        doa, _unused, dz, dnw = vjp(dog_ref[...])
        do_ref[...] = doa
        dz_ref[...] = dz
        row8 = lax.broadcasted_iota(jnp.int32, (8, 128), 0)
        dnw_ref[...] += jnp.where(row8 == 0, dnw, 0.0)

    in_specs = [pl.BlockSpec((2, bt, D), lambda i: (0, i, 0)), pl.BlockSpec((bt, D), lambda i: (i, CZ)),
                pl.BlockSpec((8, 128), lambda i: (0, 0)), pl.BlockSpec((bt, D), lambda i: (i, 0)),
                pl.BlockSpec(memory_space=pl.ANY)]
    return pl.pallas_call(
        body, name="post_bwd", grid=(T // bt,), in_specs=in_specs,
        out_specs=[pl.BlockSpec((bt, D), lambda i: (i, 0)), pl.BlockSpec((bt, D), lambda i: (i, CZ)),
                   pl.BlockSpec((8, 128), lambda i: (0, 0))],
        out_shape=[_sds((T, D)), _sds(dpm.shape), _sds((8, 128))], input_output_aliases={4: 1},
        compiler_params=_cp(("arbitrary",)))(o2, pm, nw, dog, dpm)


def merge_fwd(ya, yb, pm, *, bt=512):
    T = pm.shape[0]
    bt = min(bt, T)

    def body(ya_ref, yb_ref, ga_ref, gb_ref, o_ref):
        o_ref[...] = (_sigmoid(ga_ref[...]) * ya_ref[...] + _sigmoid(gb_ref[...]) * yb_ref[...]).astype(o_ref.dtype)

    row = pl.BlockSpec((bt, D), lambda i: (i, 0))
    return pl.pallas_call(
        body, name="merge_fwd", grid=(T // bt,),
        in_specs=[row, row, pl.BlockSpec((bt, D), lambda i: (i, CGA)), pl.BlockSpec((bt, D), lambda i: (i, CGB))],
        out_specs=row, out_shape=_sds((T, D), MM), compiler_params=_cp(("parallel",)))(ya, yb, pm, pm)


def merge_bwd(ya, yb, pm, dmix, *, bt=512):
    T = pm.shape[0]
    bt = min(bt, T)

    def body(ya_ref, yb_ref, ga_ref, gb_ref, dm_ref, dya_ref, dyb_ref, dg_ref):
        dm = dm_ref[...]
        sa, sb = _sigmoid(ga_ref[...]), _sigmoid(gb_ref[...])
        dya_ref[...] = (dm * sa).astype(dya_ref.dtype)
        dyb_ref[...] = (dm * sb).astype(dyb_ref.dtype)
        dg_ref[:, 0:D] = dm * ya_ref[...] * sa * (1.0 - sa)
        dg_ref[:, D:2 * D] = dm * yb_ref[...] * sb * (1.0 - sb)

    row = pl.BlockSpec((bt, D), lambda i: (i, 0))
    return pl.pallas_call(
        body, name="merge_bwd", grid=(T // bt,),
        in_specs=[row, row, pl.BlockSpec((bt, D), lambda i: (i, CGA)), pl.BlockSpec((bt, D), lambda i: (i, CGB)), row],
        out_specs=[row, row, pl.BlockSpec((bt, 2 * D), lambda i: (i, CGA // 2))],
        out_shape=[_sds((T, D), MM), _sds((T, D), MM), _sds((T, NMAIN))],
        compiler_params=_cp(("parallel",)))(ya, yb, pm, pm, dmix)


def _ln_rows(x, y, bias, g, b):
    r = ALPHA * x + y + bias
    mu = jnp.mean(r, axis=-1, keepdims=True)
    var = jnp.mean(jnp.square(r - mu), axis=-1, keepdims=True)
    return (r - mu) * lax.rsqrt(var + LN_EPS) * g + b


def ln_fwd(x, y, p, *, name, bt=512):
    T = x.shape[0]
    bt = min(bt, T)

    def body(x_ref, y_ref, p_ref, o_ref, ob_ref):
        r = _ln_rows(x_ref[...], y_ref[...], p_ref[0:1, :], p_ref[1:2, :], p_ref[2:3, :])
        o_ref[...] = r
        ob_ref[...] = r.astype(ob_ref.dtype)

    row = pl.BlockSpec((bt, D), lambda i: (i, 0))
    return pl.pallas_call(
        body, name=name, grid=(T // bt,), in_specs=[row, row, pl.BlockSpec((8, D), lambda i: (0, 0))],
        out_specs=[row, row], out_shape=[_sds((T, D)), _sds((T, D), MM)],
        compiler_params=_cp(("parallel",)))(x, y, p)


def ln_bwd(x, y, p, ct, ct2=None, *, name, bt=512):
    T = x.shape[0]
    bt = min(bt, T)

    def body(*refs):
        it = iter(refs)
        x_ref, y_ref, p_ref, c_ref = next(it), next(it), next(it), next(it)
        c2_ref = next(it) if ct2 is not None else None
        dxa_ref, dr_ref, dp_ref = next(it), next(it), next(it)

        @pl.when(pl.program_id(0) == 0)
        def _():
            dp_ref[...] = jnp.zeros_like(dp_ref)

        c = c_ref[...]
        if c2_ref is not None:
            c = c + c2_ref[...]
        _, vjp = jax.vjp(_ln_rows, x_ref[...], y_ref[...], p_ref[0:1, :], p_ref[1:2, :], p_ref[2:3, :])
        _dx, dy, dbias, dg, db = vjp(c)
        dxa_ref[...] = ALPHA * dy
        dr_ref[...] = dy.astype(dr_ref.dtype)
        row8 = lax.broadcasted_iota(jnp.int32, (8, D), 0)
        dp_ref[...] += jnp.where(row8 == 0, dbias, jnp.where(row8 == 1, dg, jnp.where(row8 == 2, db, 0.0)))

    row = pl.BlockSpec((bt, D), lambda i: (i, 0))
    in_specs = [row, row, pl.BlockSpec((8, D), lambda i: (0, 0)), row] + ([row] if ct2 is not None else [])
    args = [x, y, p, ct] + ([ct2] if ct2 is not None else [])
    return pl.pallas_call(
        body, name=name, grid=(T // bt,), in_specs=in_specs,
        out_specs=[row, row, pl.BlockSpec((8, D), lambda i: (0, 0))],
        out_shape=[_sds((T, D)), _sds((T, D), MM), _sds((8, D))],
        compiler_params=_cp(("arbitrary",)))(*args)


def loss_fwd_bwd(xl, target, *, bt=512):
    T = xl.shape[0]
    bt = min(bt, T)

    def body(x_ref, t_ref, l_ref, d_ref):
        @pl.when(pl.program_id(0) == 0)
        def _():
            l_ref[...] = jnp.zeros_like(l_ref)

        e = x_ref[...] - t_ref[...]
        d_ref[...] = e * (1.0 / D)
        l_ref[...] += 0.5 * jnp.sum(jnp.mean(e * e, axis=-1, keepdims=True), axis=0, keepdims=True)

    row = pl.BlockSpec((bt, D), lambda i: (i, 0))
    return pl.pallas_call(
        body, name="loss", grid=(T // bt,), in_specs=[row, row],
        out_specs=[pl.BlockSpec((8, 128), lambda i: (0, 0)), row], out_shape=[_sds((8, 128)), _sds((T, D))],
        compiler_params=_cp(("arbitrary",)))(xl, target)


def _row_tile(R, Cc, elems=1 << 18):
    if R * Cc <= elems:
        return R
    tr = 8
    while tr * 2 * Cc <= elems and R % (tr * 2) == 0:
        tr *= 2
    return tr


def adam(w, m, v, ga, gb=None, *, name):
    R, Cc = w.shape
    tr = _row_tile(R, Cc)

    def body(*refs):
        it = iter(refs)
        w_ref, m_ref, v_ref, a_ref = next(it), next(it), next(it), next(it)
        b_ref = next(it) if gb is not None else None
        g_ref, d_ref, mo_ref, vo_ref = next(it), next(it), next(it), next(it)
        g = a_ref[...]
        if b_ref is not None:
            g = g + b_ref[...]
        m2 = B1 * m_ref[...] + (1.0 - B1) * g
        v2 = B2 * v_ref[...] + (1.0 - B2) * jnp.square(g)
        m_hat = m2 / (1.0 - B1 ** STEP)
        v_hat = v2 / (1.0 - B2 ** STEP)
        g_ref[...] = g
        d_ref[...] = -LR * (m_hat / (jnp.sqrt(v_hat) + EPS) + WD * w_ref[...])
        mo_ref[...] = m2
        vo_ref[...] = v2

    blk = pl.BlockSpec((tr, Cc), lambda i: (i, 0))
    args = [w, m, v, ga] + ([gb] if gb is not None else [])
    return pl.pallas_call(
        body, name=name, grid=(R // tr,), in_specs=[blk] * len(args), out_specs=[blk] * 4,
        out_shape=[_sds((R, Cc))] * 4, compiler_params=_cp(("parallel",)))(*args)


def sum4(own, recv, *, name):
    R, Cc = own.shape
    tr = _row_tile(R, Cc)

    def body(o_ref, r_ref, out_ref):
        out_ref[...] = ((o_ref[...] + r_ref[0]) + r_ref[1]) + r_ref[2]

    return pl.pallas_call(
        body, name=name, grid=(R // tr,),
        in_specs=[pl.BlockSpec((tr, Cc), lambda i: (i, 0)), pl.BlockSpec((3, tr, Cc), lambda i: (0, i, 0))],
        out_specs=pl.BlockSpec((tr, Cc), lambda i: (i, 0)), out_shape=_sds((R, Cc)),
        compiler_params=_cp(("parallel",)))(own, recv)


def _place():
    return lax.axis_index("x"), lax.axis_index("y"), lax.axis_index("c")


def _other_chips(x, y):
    return [(1 - x, y), (x, 1 - y), (1 - x, 1 - y)]


_ANY = pl.BlockSpec(memory_space=pl.ANY)


def allgather_xy(arrs):
    n = len(arrs)

    def body(*refs):
        ins, outs = refs[:n], refs[n:2 * n]
        send, recv, loc = refs[2 * n:]
        x, y, c = _place()
        me = 2 * x + y
        peers = _other_chips(x, y)
        local = [pltpu.make_async_copy(ins[a], outs[a].at[me], loc.at[a]) for a in range(n)]
        for cp in local:
            cp.start()

        def remote(a, j, block):
            px, py = peers[j]
            return pltpu.make_async_remote_copy(
                src_ref=ins[a], dst_ref=outs[a].at[block], send_sem=send.at[3 * a + j], recv_sem=recv.at[3 * a + j],
                device_id=(px, py, c), device_id_type=MESH)

        sends = [remote(a, j, me) for a in range(n) for j in range(3)]
        for cp in sends:
            cp.start()
        for a in range(n):
            for j, (px, py) in enumerate(peers):
                remote(a, j, 2 * px + py).wait_recv()
        for cp in sends:
            cp.wait_send()
        for cp in local:
            cp.wait()

    return pl.pallas_call(
        body, name="allgather_xy", in_specs=[_ANY] * n, out_specs=[_ANY] * n,
        out_shape=[_sds((4,) + a.shape, a.dtype) for a in arrs],
        scratch_shapes=[pltpu.SemaphoreType.DMA((3 * n,)), pltpu.SemaphoreType.DMA((3 * n,)),
                        pltpu.SemaphoreType.DMA((n,))],
        compiler_params=pltpu.CompilerParams(has_side_effects=True))(*arrs)


def scatter_xy(groups):
    L = len(groups[0])
    n = len(groups) * L
    flat = [g for grp in groups for g in grp]

    def body(*refs):
        ins = refs[:n]
        owns, recvs = refs[n:n + len(groups)], refs[n + len(groups):n + 2 * len(groups)]
        send, recv, loc = refs[n + 2 * len(groups):]
        x, y, c = _place()
        me = 2 * x + y
        peers = _other_chips(x, y)
        local, sends = [], []
        for a in range(len(groups)):
            for l in range(L):
                i = a * L + l
                local.append(pltpu.make_async_copy(ins[i].at[me], owns[a].at[l], loc.at[i]))
                for j, (px, py) in enumerate(peers):
                    sends.append(pltpu.make_async_remote_copy(
                        src_ref=ins[i].at[2 * px + py], dst_ref=recvs[a].at[j, l], send_sem=send.at[3 * i + j],
                        recv_sem=recv.at[3 * i + j], device_id=(px, py, c), device_id_type=MESH))
        for cp in local + sends:
            cp.start()
        for cp in sends:
            cp.wait_recv()
        for cp in sends:
            cp.wait_send()
        for cp in local:
            cp.wait()

    out_shape = ([_sds((L,) + grp[0].shape[1:]) for grp in groups]
                 + [_sds((3, L) + grp[0].shape[1:]) for grp in groups])
    outs = pl.pallas_call(
        body, name="scatter_xy", in_specs=[_ANY] * n, out_specs=[_ANY] * (2 * len(groups)), out_shape=out_shape,
        scratch_shapes=[pltpu.SemaphoreType.DMA((3 * n,)), pltpu.SemaphoreType.DMA((3 * n,)),
                        pltpu.SemaphoreType.DMA((n,))],
        compiler_params=pltpu.CompilerParams(has_side_effects=True))(*flat)
    return outs[:len(groups)], outs[len(groups):]


def swap_c(arrs):
    n = len(arrs)

    def body(*refs):
        ins, outs = refs[:n], refs[n:2 * n]
        send, recv = refs[2 * n:]
        x, y, c = _place()
        cps = [pltpu.make_async_remote_copy(src_ref=ins[a], dst_ref=outs[a], send_sem=send.at[a], recv_sem=recv.at[a],
                                            device_id=(x, y, 1 - c), device_id_type=MESH) for a in range(n)]
        for cp in cps:
            cp.start()
        for cp in cps:
            cp.wait_recv()
        for cp in cps:
            cp.wait_send()

    return pl.pallas_call(
        body, name="swap_c", in_specs=[_ANY] * n, out_specs=[_ANY] * n, out_shape=[_sds(a.shape, a.dtype) for a in arrs],
        scratch_shapes=[pltpu.SemaphoreType.DMA((n,)), pltpu.SemaphoreType.DMA((n,))],
        compiler_params=pltpu.CompilerParams(has_side_effects=True))(*arrs)


def allreduce_small(v):
    R = v.shape[0]

    def body(v_ref, o_ref, buf, send, recv):
        x, y, c = _place()
        me = 4 * x + 2 * y + c
        buf[0] = v_ref[...]

        def cp(k):
            dx, dy, dc = (k >> 2) & 1, (k >> 1) & 1, k & 1
            return pltpu.make_async_remote_copy(
                src_ref=v_ref, dst_ref=buf.at[k], send_sem=send.at[k - 1], recv_sem=recv.at[k - 1],
                device_id=(x ^ dx, y ^ dy, c ^ dc), device_id_type=MESH)

        cps = [cp(k) for k in range(1, 8)]
        for t in cps:
            t.start()
        for t in cps:
            t.wait_recv()
        acc = buf[me]
        for dev in range(1, 8):
            acc = acc + buf[jnp.bitwise_xor(me, dev)]
        o_ref[...] = acc
        for t in cps:
            t.wait_send()

    vm = pl.BlockSpec(memory_space=pltpu.VMEM)
    return pl.pallas_call(
        body, name="allreduce_small", in_specs=[vm], out_specs=vm, out_shape=_sds((R, 128)),
        scratch_shapes=[pltpu.VMEM((8, R, 128), F32), pltpu.SemaphoreType.DMA((7,)), pltpu.SemaphoreType.DMA((7,))],
        compiler_params=pltpu.CompilerParams(has_side_effects=True, vmem_limit_bytes=VMEM_LIMIT))(v)


def _rows8(*rows):
    n = rows[0].shape[-1]
    t = jnp.stack([r.reshape(n).astype(F32) for r in rows])
    return jnp.pad(t, ((0, 8 - len(rows)), (0, 0)))


def _lanes128(a):
    f = a.reshape(-1).astype(F32)
    return jnp.pad(f, (0, 128 - f.shape[0]))


def _layer_fwd(x, xb, W):
    pm = mm_nn(xb, W["w_main"], name="proj_main")
    pab = mm_nn(xb, W["w_ab"], name="proj_ab")
    q, k, v, gb = pre_qkv_fwd(pm, pab, W["cw"], W["gp"])
    sc = pre_sc_fwd(pm, W["csc"])
    o2, s0 = gdn_fwd(q, k, v, gb)
    og = post_fwd(o2, pm, W["nw"])
    ya = mm_nn(og, W["w_og"], name="proj_og")
    yb = mm_nn(sc, W["w_osc"], name="proj_osc")
    mixed = merge_fwd(ya, yb, pm)
    out = mm_nn(mixed, W["w_out"], name="proj_out")
    x1, x1b = ln_fwd(x, out, W["ln1"], name="ln1_fwd")
    hpre, h = mm_nn(x1b, W["w_up"], bias=W["b_up"], relu2=True, name="mlp_up")
    dn = mm_nn(h, W["w_down"], name="mlp_down")
    x2, x2b = ln_fwd(x1, dn, W["ln2"], name="ln2_fwd")
    saved = dict(x=x, xb=xb, pm=pm, pab=pab, q=q, k=k, v=v, gb=gb, sc=sc, o2=o2, s0=s0, og=og, ya=ya, yb=yb,
                 mixed=mixed, out=out, x1=x1, x1b=x1b, hpre=hpre, h=h, dn=dn)
    return x2, x2b, saved


def _layer_bwd(ct, W, S):
    dxa2, dr2b, dp2 = ln_bwd(S["x1"], S["dn"], W["ln2"], ct, name="ln2_bwd")
    g_down = mm_tn(S["h"], dr2b, name="dw_down")
    dhpre, db_up = mm_nt(dr2b, W["w_down"], dact=S["hpre"], out_dtype=MM, name="mlp_down_bwd")
    g_up = mm_tn(S["x1b"], dhpre, name="dw_up")
    dx1 = mm_nt(dhpre, W["w_up"], add=dxa2, name="mlp_up_bwd")
    dxa1, dr1b, dp1 = ln_bwd(S["x"], S["out"], W["ln1"], dx1, name="ln1_bwd")
    g_out = mm_tn(S["mixed"], dr1b, name="dw_out")
    dmix = mm_nt(dr1b, W["w_out"], name="proj_out_bwd")
    dya, dyb, dpm = merge_bwd(S["ya"], S["yb"], S["pm"], dmix)
    g_og = mm_tn(S["og"], dya, name="dw_og")
    g_osc = mm_tn(S["sc"], dyb, name="dw_osc")
    dog = mm_nt(dya, W["w_og"], name="proj_og_bwd")
    dsc = mm_nt(dyb, W["w_osc"], name="proj_osc_bwd")
    do, dpm, dnw = post_bwd(S["o2"], S["pm"], W["nw"], dog, dpm)
    dq2, dk2, dv2, dgb2 = gdn_bwd(S["q"], S["k"], S["v"], S["gb"], S["s0"], do)
    dpm, dpab, dcw, dgp = pre_qkv_bwd(S["pm"], S["pab"], W["cw"], W["gp"], dq2, dk2, dv2, dgb2, dpm)
    dpm, dcsc = pre_sc_bwd(S["pm"], W["csc"], dsc, dpm)
    g_main = mm_tn(S["xb"], dpm, name="dw_main")
    g_ab = mm_tn(S["xb"], dpab, name="dw_ab")
    t = mm_nt(dpab, W["w_ab"], add=dxa1, name="proj_ab_bwd")
    dx = mm_nt(dpm, W["w_main"], add=t, name="proj_main_bwd")
    g_in = jnp.concatenate([g_main[:, :3 * D], g_main[:, 8 * D:], g_ab[:, :4 * NH], g_main[:, 3 * D:8 * D]], axis=1)
    grads = dict(
        w_in=g_in, w_o_gdn=g_og, w_o_sc=g_osc, w_out=g_out, w_up=g_up, w_down=g_down,
        conv_qkv=dcw[:3], conv_sc=dcsc[:3], a_log=dgp[0, :2 * NH].reshape(2, NH), dt_bias=dgp[1, :2 * NH].reshape(2, NH),
        gdn_norm_w=dnw[0], ln1_g=dp1[1], ln1_b=dp1[2], b_up=db_up[0], b_down=dp2[0], ln2_g=dp2[1], ln2_b=dp2[2])
    return dx, grads


def _layer_weights(l, full, a_log, dt_bias, gdn_norm_w, ln1_g, ln1_b, b_up, b_down, ln2_g, ln2_b):
    w_in = full["w_in"][l]
    w_main = jnp.concatenate([w_in[:, :3 * D], w_in[:, 4 * D + 4 * NH:], w_in[:, 3 * D:4 * D]], axis=1)
    w_ab = jnp.pad(w_in[:, 4 * D:4 * D + 4 * NH], ((0, 0), (0, 128 - 4 * NH)))
    return dict(
        w_main=w_main, w_ab=w_ab, w_og=full["w_o_gdn"][l], w_osc=full["w_o_sc"][l], w_out=full["w_out"][l],
        w_up=full["w_up"][l], w_down=full["w_down"][l],
        cw=jnp.pad(full["conv_qkv"][l].astype(F32), ((0, 5), (0, 0))),
        csc=jnp.pad(full["conv_sc"][l].astype(F32), ((0, 5), (0, 0))),
        gp=_rows8(_lanes128(a_log[l]), _lanes128(dt_bias[l])), nw=_rows8(gdn_norm_w[l]),
        ln1=_rows8(jnp.zeros((D,), F32), ln1_g[l], ln1_b[l]), ln2=_rows8(b_down[l], ln2_g[l], ln2_b[l]),
        b_up=b_up[l].reshape(1, DFF).astype(F32))


def local_step(xs, target, full, a_log, dt_bias, gdn_norm_w, ln1_g, ln1_b, b_up, b_down, ln2_g, ln2_b):
    Ws = [_layer_weights(l, full, a_log, dt_bias, gdn_norm_w, ln1_g, ln1_b, b_up, b_down, ln2_g, ln2_b)
          for l in range(DEPTH)]
    x, xb = xs, xs.astype(MM)
    saved = []
    for l in range(DEPTH):
        x, xb, S = _layer_fwd(x, xb, Ws[l])
        saved.append(S)
    loss_tile, ct = loss_fwd_bwd(x, target)
    grads = [None] * DEPTH
    for l in reversed(range(DEPTH)):
        ct, grads[l] = _layer_bwd(ct, Ws[l], saved[l])
    return loss_tile, ct, grads


BIG = ("w_in", "w_o_gdn", "w_o_sc", "w_out", "w_up", "w_down")
SMALL = ("conv_qkv", "a_log", "dt_bias", "gdn_norm_w", "conv_sc", "ln1_g", "ln1_b", "b_up", "b_down", "ln2_g", "ln2_b")
ORDER = ("w_in", "conv_qkv", "a_log", "dt_bias", "gdn_norm_w", "w_o_gdn", "conv_sc", "w_o_sc", "w_out", "ln1_g",
         "ln1_b", "w_up", "b_up", "w_down", "b_down", "ln2_g", "ln2_b")


def _pack(arrs):
    flat = jnp.concatenate([a.reshape(-1).astype(F32) for a in arrs])
    n = flat.shape[0]
    rows = -(-n // 1024) * 8
    return jnp.pad(flat, (0, rows * 128 - n)).reshape(rows, 128)


def _unpack(buf, like):
    flat = buf.reshape(-1)
    out, o = [], 0
    for a in like:
        n = 1
        for s in a.shape:
            n *= s
        out.append(flat[o:o + n].reshape(a.shape))
        o += n
    return out


def _gathered(name, g):
    if name in ("w_in", "w_up", "conv_qkv", "conv_sc"):
        t = jnp.moveaxis(g, 0, -2)
        return t.reshape(t.shape[:-2] + (t.shape[-2] * t.shape[-1],))
    t = jnp.moveaxis(g, 0, 1)
    return t.reshape((t.shape[0], t.shape[1] * t.shape[2]) + t.shape[3:])


def _by_chip(name, g):
    if name in ("w_in", "w_up"):
        r, ccols = g.shape
        return jnp.moveaxis(g.reshape(r, 4, ccols // 4), 1, 0)
    return g.reshape((4, g.shape[0] // 4) + g.shape[1:])


def kernel(x, w_in, conv_qkv, a_log, dt_bias, gdn_norm_w, w_o_gdn, conv_sc, w_o_sc, w_out, ln1_g, ln1_b, w_up, b_up, w_down, b_down, ln2_g, ln2_b, loss_target, m_w_in, m_conv_qkv, m_a_log, m_dt_bias, m_gdn_norm_w, m_w_o_gdn, m_conv_sc, m_w_o_sc, m_w_out, m_ln1_g, m_ln1_b, m_w_up, m_b_up, m_w_down, m_b_down, m_ln2_g, m_ln2_b, v_w_in, v_conv_qkv, v_a_log, v_dt_bias, v_gdn_norm_w, v_w_o_gdn, v_conv_sc, v_w_o_sc, v_w_out, v_ln1_g, v_ln1_b, v_w_up, v_b_up, v_w_down, v_b_down, v_ln2_g, v_ln2_b):
    w = dict(w_in=w_in, conv_qkv=conv_qkv, a_log=a_log, dt_bias=dt_bias, gdn_norm_w=gdn_norm_w, w_o_gdn=w_o_gdn,
             conv_sc=conv_sc, w_o_sc=w_o_sc, w_out=w_out, ln1_g=ln1_g, ln1_b=ln1_b, w_up=w_up, b_up=b_up,
             w_down=w_down, b_down=b_down, ln2_g=ln2_g, ln2_b=ln2_b)
    m = dict(w_in=m_w_in, conv_qkv=m_conv_qkv, a_log=m_a_log, dt_bias=m_dt_bias, gdn_norm_w=m_gdn_norm_w,
             w_o_gdn=m_w_o_gdn, conv_sc=m_conv_sc, w_o_sc=m_w_o_sc, w_out=m_w_out, ln1_g=m_ln1_g, ln1_b=m_ln1_b,
             w_up=m_w_up, b_up=m_b_up, w_down=m_w_down, b_down=m_b_down, ln2_g=m_ln2_g, ln2_b=m_ln2_b)
    v = dict(w_in=v_w_in, conv_qkv=v_conv_qkv, a_log=v_a_log, dt_bias=v_dt_bias, gdn_norm_w=v_gdn_norm_w,
             w_o_gdn=v_w_o_gdn, conv_sc=v_conv_sc, w_o_sc=v_w_o_sc, w_out=v_w_out, ln1_g=v_ln1_g, ln1_b=v_ln1_b,
             w_up=v_w_up, b_up=v_b_up, w_down=v_w_down, b_down=v_b_down, ln2_g=v_ln2_g, ln2_b=v_ln2_b)
    chip = 2 * lax.axis_index("x") + lax.axis_index("y")

    names = BIG + ("conv_qkv", "conv_sc")
    got = allgather_xy([w[n].astype(MM) if n in BIG else w[n] for n in names])
    full = {n: _gathered(n, g) for n, g in zip(names, got)}

    loss_tile, dx, grads = local_step(x[0], loss_target[0], full, a_log, dt_bias, gdn_norm_w, ln1_g, ln1_b, b_up,
                                      b_down, ln2_g, ln2_b)
    loss = lax.psum(loss_tile[0, 0], ("x", "y", "c"))

    groups = [[_by_chip(n, grads[l][n]) for l in range(DEPTH)] for n in BIG]
    owns, recvs = scatter_xy(groups)
    part = []
    for n, own, rec in zip(BIG, owns, recvs):
        cols = own.shape[-1]
        part.append(sum4(own.reshape(-1, cols), rec.reshape(3, -1, cols), name="sum_" + n))
    other = swap_c(part)
    out = {}
    for n, mine, theirs in zip(BIG, part, other):
        cols = mine.shape[-1]
        res = adam(w[n].reshape(-1, cols), m[n].reshape(-1, cols), v[n].reshape(-1, cols), mine, theirs, name="adam_" + n)
        out[n] = [r.reshape(w[n].shape) for r in res]

    stacked = [jnp.stack([grads[l][n] for l in range(DEPTH)]) for n in SMALL]
    summed = _unpack(allreduce_small(_pack(stacked)), stacked)
    gs = []
    for n, g in zip(SMALL, summed):
        if n in ("conv_qkv", "conv_sc"):
            blk = w[n].shape[-1]
            g = lax.dynamic_slice_in_dim(g, chip * blk, blk, axis=2)
        gs.append(g)
    res = adam(_pack([w[n] for n in SMALL]), _pack([m[n] for n in SMALL]), _pack([v[n] for n in SMALL]), _pack(gs),
               name="adam_small")
    for n, parts in zip(SMALL, zip(*[_unpack(r, gs) for r in res])):
        out[n] = list(parts)

    outs = [loss, dx[None]]
    for kind in range(4):
        outs += [out[n][kind] for n in ORDER]
    return tuple(outs)
```

```python
import functools

import jax
import jax.numpy as jnp
from jax import lax
from jax.experimental import pallas as pl
from jax.experimental.pallas import tpu as pltpu

F32 = jnp.float32
MM = jnp.bfloat16
HI = lax.Precision.HIGHEST

D = 1024
NH = 8
HD = 128
CH = 64
DFF = 4 * D
DEPTH = 4
LN_EPS = 1e-5
RMS_EPS = 1e-6
L2_EPS = 1e-6
ALPHA = (2 * DEPTH) ** 0.25
LR, B1, B2, EPS, WD, STEP = 0.001, 0.9, 0.999, 1e-08, 0.01, 10

NMAIN = 9 * D
CQ, CK, CV, CSB, CSC, CSX, CGA, CGB, CZ = range(9)
HALO = 8
VMEM_LIMIT = 56 * 1024 * 1024
MESH = pl.DeviceIdType.MESH


def _cp(sem=None, vmem=VMEM_LIMIT):
    return pltpu.CompilerParams(dimension_semantics=sem, vmem_limit_bytes=vmem)


def _sds(shape, dtype=F32):
    return jax.ShapeDtypeStruct(tuple(shape), dtype)


def mm_nn(a, b, *, name, bias=None, relu2=False, add=None, out_dtype=F32, tm=1024, tn=1024, tk=1024):
    M, K = a.shape
    N = b.shape[1]
    tm, tn, tk = min(tm, M), min(tn, N), min(tk, K)
    nk = K // tk

    def body(*refs):
        it = iter(refs)
        a_ref, b_ref = next(it), next(it)
        bias_ref = next(it) if bias is not None else None
        add_ref = next(it) if add is not None else None
        o_ref = next(it)
        h_ref = next(it) if relu2 else None
        acc = next(it)
        k = pl.program_id(2)

        @pl.when(k == 0)
        def _():
            acc[...] = jnp.zeros_like(acc)

        acc[...] += jnp.dot(a_ref[...].astype(MM), b_ref[...].astype(MM), preferred_element_type=F32)

        @pl.when(k == nk - 1)
        def _():
            r = acc[...]
            if bias_ref is not None:
                r = r + bias_ref[...]
            if add_ref is not None:
                r = r + add_ref[...]
            o_ref[...] = r.astype(o_ref.dtype)
            if relu2:
                t = jnp.maximum(r, 0.0)
                h_ref[...] = (t * t).astype(h_ref.dtype)

    in_specs = [pl.BlockSpec((tm, tk), lambda i, j, k: (i, k)), pl.BlockSpec((tk, tn), lambda i, j, k: (k, j))]
    args = [a, b]
    if bias is not None:
        in_specs.append(pl.BlockSpec((1, tn), lambda i, j, k: (0, j)))
        args.append(bias)
    if add is not None:
        in_specs.append(pl.BlockSpec((tm, tn), lambda i, j, k: (i, j)))
        args.append(add)
    out_shape = [_sds((M, N), out_dtype)]
    out_specs = [pl.BlockSpec((tm, tn), lambda i, j, k: (i, j))]
    if relu2:
        out_shape.append(_sds((M, N), MM))
        out_specs.append(pl.BlockSpec((tm, tn), lambda i, j, k: (i, j)))
    res = pl.pallas_call(
        body, name=name, grid=(M // tm, N // tn, nk), in_specs=in_specs, out_specs=out_specs, out_shape=out_shape,
        scratch_shapes=[pltpu.VMEM((tm, tn), F32)],
        compiler_params=_cp(("parallel", "parallel", "arbitrary")))(*args)
    return res if relu2 else res[0]


def mm_nt(a, b, *, name, add=None, dact=None, out_dtype=F32, tm=1024, tn=1024, tk=1024):
    M, Nc = a.shape
    Ko = b.shape[0]
    tm, tn, tk = min(tm, M), min(tn, Ko), min(tk, Nc)
    nk = Nc // tk
    ni = M // tm

    def body(*refs):
        it = iter(refs)
        a_ref, b_ref = next(it), next(it)
        add_ref = next(it) if add is not None else None
        d_ref = next(it) if dact is not None else None
        o_ref = next(it)
        db_ref = next(it) if dact is not None else None
        acc = next(it)
        i, k = pl.program_id(1), pl.program_id(2)

        @pl.when(k == 0)
        def _():
            acc[...] = jnp.zeros_like(acc)

        acc[...] += lax.dot_general(a_ref[...].astype(MM), b_ref[...].astype(MM), (((1,), (1,)), ((), ())),
                                    preferred_element_type=F32)

        @pl.when(k == nk - 1)
        def _():
            r = acc[...]
            if add_ref is not None:
                r = r + add_ref[...]
            if d_ref is not None:
                r = r * (2.0 * jnp.maximum(d_ref[...], 0.0))
                s = jnp.sum(r, axis=0, keepdims=True)
                row0 = lax.broadcasted_iota(jnp.int32, db_ref.shape, 0) == 0

                @pl.when(i == 0)
                def _():
                    db_ref[...] = jnp.zeros_like(db_ref)

                db_ref[...] += jnp.where(row0, s, 0.0)
            o_ref[...] = r.astype(o_ref.dtype)

    in_specs = [pl.BlockSpec((tm, tk), lambda j, i, k: (i, k)), pl.BlockSpec((tn, tk), lambda j, i, k: (j, k))]
    args = [a, b]
    for extra in (add, dact):
        if extra is not None:
            in_specs.append(pl.BlockSpec((tm, tn), lambda j, i, k: (i, j)))
            args.append(extra)
    out_shape = [_sds((M, Ko), out_dtype)]
    out_specs = [pl.BlockSpec((tm, tn), lambda j, i, k: (i, j))]
    if dact is not None:
        out_shape.append(_sds((8, Ko), F32))
        out_specs.append(pl.BlockSpec((8, tn), lambda j, i, k: (0, j)))
    res = pl.pallas_call(
        body, name=name, grid=(Ko // tn, ni, nk), in_specs=in_specs, out_specs=out_specs, out_shape=out_shape,
        scratch_shapes=[pltpu.VMEM((tm, tn), F32)],
        compiler_params=_cp(("parallel", "arbitrary", "arbitrary")))(*args)
    return res if dact is not None else res[0]


def mm_tn(a, b, *, name, tm=1024, tn=1024, tk=512):
    T, M = a.shape
    N = b.shape[1]
    tm, tn, tk = min(tm, M), min(tn, N), min(tk, T)

    def body(a_ref, b_ref, o_ref):
        @pl.when(pl.program_id(2) == 0)
        def _():
            o_ref[...] = jnp.zeros_like(o_ref)

        o_ref[...] += lax.dot_general(a_ref[...].astype(MM), b_ref[...].astype(MM), (((0,), (0,)), ((), ())),
                                      preferred_element_type=F32)

    return pl.pallas_call(
        body, name=name, grid=(M // tm, N // tn, T // tk),
        in_specs=[pl.BlockSpec((tk, tm), lambda i, j, k: (k, i)), pl.BlockSpec((tk, tn), lambda i, j, k: (k, j))],
        out_specs=pl.BlockSpec((tm, tn), lambda i, j, k: (i, j)), out_shape=_sds((M, N)),
        compiler_params=_cp(("parallel", "parallel", "arbitrary")))(a, b)


def _sigmoid(x):
    return 1.0 / (1.0 + jnp.exp(-x))


def _silu(x):
    return x * _sigmoid(x)


def _softplus(x):
    return jnp.maximum(x, 0.0) + jnp.log1p(jnp.exp(-jnp.abs(x)))


def _ext(main_ref, prev_ref, next_ref, first, last):
    p = jnp.where(first, 0.0, prev_ref[...].astype(F32))
    n = jnp.where(last, 0.0, next_ref[...].astype(F32))
    return jnp.concatenate([p, main_ref[...].astype(F32), n], axis=0)


def _shift_dn(x):
    return pltpu.roll(x, 1, 0)


def _shift_up(x):
    return pltpu.roll(x, x.shape[0] - 1, 0)


def _conv3(xe, w):
    return w[0:1, :] * _shift_dn(xe) + w[1:2, :] * xe + w[2:3, :] * _shift_up(xe)


def _conv3_t(de, w):
    return w[0:1, :] * _shift_up(de) + w[1:2, :] * de + w[2:3, :] * _shift_dn(de)


def _halo_specs(bt, T, col, lead=None):
    r = bt // HALO
    last = T // HALO - 1
    if lead is None:
        return [pl.BlockSpec((bt, D), lambda i: (i, col)),
                pl.BlockSpec((HALO, D), lambda i: (jnp.maximum(i * r - 1, 0), col)),
                pl.BlockSpec((HALO, D), lambda i: (jnp.minimum((i + 1) * r, last), col))]
    return [pl.BlockSpec((lead, bt, D), lambda i: (0, i, col)),
            pl.BlockSpec((lead, HALO, D), lambda i: (0, jnp.maximum(i * r - 1, 0), col)),
            pl.BlockSpec((lead, HALO, D), lambda i: (0, jnp.minimum((i + 1) * r, last), col))]


def _qkv_rows(cq, ck, cv):
    sq, sk, sv = _silu(cq), _silu(ck), _silu(cv)
    qs, ks = [], []
    for h in range(NH):
        s = slice(h * HD, (h + 1) * HD)
        qh, kh = sq[:, s], sk[:, s]
        qs.append(qh * lax.rsqrt(jnp.sum(qh * qh, axis=-1, keepdims=True) + L2_EPS) * (HD ** -0.5))
        ks.append(kh * lax.rsqrt(jnp.sum(kh * kh, axis=-1, keepdims=True) + L2_EPS))
    return jnp.concatenate(qs, axis=1), jnp.concatenate(ks, axis=1), sv


def _chunk_masks(bt):
    row = lax.broadcasted_iota(jnp.int32, (bt, bt), 0)
    col = lax.broadcasted_iota(jnp.int32, (bt, bt), 1)
    same = (row // CH) == (col // CH)
    lower = jnp.where(same & (col <= row), 1.0, 0.0).astype(F32)
    upper = jnp.where(same & (col >= row), 1.0, 0.0).astype(F32)
    return lower, upper


def _gate_rows(ab, gp, lower, upper):
    lane = lax.broadcasted_iota(jnp.int32, ab.shape, 1)
    g = -jnp.exp(gp[0:1, :]) * _softplus(ab + gp[1:2, :])
    g = jnp.where(lane < 2 * NH, g, 0.0)
    gf = jnp.dot(lower, g, precision=HI, preferred_element_type=F32)
    gr = jnp.dot(upper, g, precision=HI, preferred_element_type=F32)
    gc = jnp.where(lane < NH, gf, gr)
    beta = _sigmoid(ab)
    return jnp.where(lane < 2 * NH, gc, jnp.where(lane < 4 * NH, beta, 0.0))


def pre_qkv_fwd(pm, pab, cw, gp, *, bt=256):
    T = pm.shape[0]
    bt = min(bt, T)
    n = T // bt

    def body(q0, q1, q2, k0, k1, k2, v0, v1, v2, ab_ref, cw_ref, gp_ref, q_ref, k_ref, v_ref, gb_ref):
        i = pl.program_id(0)
        first, last = i == 0, i == n - 1
        cs = []
        for c, (m, p, x) in enumerate(((q0, q1, q2), (k0, k1, k2), (v0, v1, v2))):
            xe = _ext(m, p, x, first, last)
            cs.append(_conv3(xe, cw_ref[:, c * D:(c + 1) * D])[HALO:HALO + bt])
        q, k, v = _qkv_rows(*cs)
        q_ref[...], k_ref[...], v_ref[...] = q, k, v
        lower, upper = _chunk_masks(bt)
        gb_ref[...] = _gate_rows(ab_ref[...], gp_ref[...], lower, upper)

    in_specs = (_halo_specs(bt, T, CQ) + _halo_specs(bt, T, CK) + _halo_specs(bt, T, CV)
                + [pl.BlockSpec((bt, 128), lambda i: (i, 0)), pl.BlockSpec((8, 3 * D), lambda i: (0, 0)),
                   pl.BlockSpec((8, 128), lambda i: (0, 0))])
    row = pl.BlockSpec((bt, D), lambda i: (i, 0))
    return pl.pallas_call(
        body, name="pre_qkv_fwd", grid=(n,), in_specs=in_specs,
        out_specs=[row, row, row, pl.BlockSpec((bt, 128), lambda i: (i, 0))],
        out_shape=[_sds((T, D)), _sds((T, D)), _sds((T, D)), _sds((T, 128))],
        compiler_params=_cp(("parallel",)))(*([pm] * 9), pab, cw, gp)


def pre_qkv_bwd(pm, pab, cw, gp, dq2, dk2, dv2, dgb2, dpm, *, bt=128):
    T = pm.shape[0]
    bt = min(bt, T)
    n = T // bt
    E = bt + 2 * HALO

    def body(*refs):
        it = iter(refs)
        xs = [[next(it) for _ in range(3)] for _ in range(3)]
        ds = [[next(it) for _ in range(3)] for _ in range(3)]
        ab_ref, dgb_ref, cw_ref, gp_ref, _alias = next(it), next(it), next(it), next(it), next(it)
        o_ref, dab_ref, dcw_ref, dgp_ref = (next(it) for _ in range(4))
        i = pl.program_id(0)
        first, last = i == 0, i == n - 1

        @pl.when(first)
        def _():
            dcw_ref[...] = jnp.zeros_like(dcw_ref)
            dgp_ref[...] = jnp.zeros_like(dgp_ref)

        xes = [_ext(*xs[c], first, last) for c in range(3)]
        ces = [_conv3(xes[c], cw_ref[:, c * D:(c + 1) * D]) for c in range(3)]
        cts = []
        for c in range(3):
            m, p, x = ds[c]
            pe = jnp.where(first, 0.0, p[0] + p[1])
            ne = jnp.where(last, 0.0, x[0] + x[1])
            cts.append(jnp.concatenate([pe, m[0] + m[1], ne], axis=0))
        _, vjp = jax.vjp(_qkv_rows, *ces)
        dces = vjp(tuple(cts))
        rowi = lax.broadcasted_iota(jnp.int32, (E, 1), 0)
        central = (rowi >= HALO) & (rowi < HALO + bt)
        row8 = lax.broadcasted_iota(jnp.int32, (8, D), 0)
        for c in range(3):
            w = cw_ref[:, c * D:(c + 1) * D]
            o_ref[:, c * D:(c + 1) * D] = _conv3_t(dces[c], w)[HALO:HALO + bt]
            dc = jnp.where(central, dces[c], 0.0)
            taps = (jnp.sum(dc * _shift_dn(xes[c]), axis=0, keepdims=True),
                    jnp.sum(dc * xes[c], axis=0, keepdims=True),
                    jnp.sum(dc * _shift_up(xes[c]), axis=0, keepdims=True))
            upd = jnp.where(row8 == 0, taps[0], jnp.where(row8 == 1, taps[1], jnp.where(row8 == 2, taps[2], 0.0)))
            dcw_ref[:, c * D:(c + 1) * D] += upd
        lower, upper = _chunk_masks(bt)
        _, gvjp = jax.vjp(lambda ab, gp: _gate_rows(ab, gp, lower, upper), ab_ref[...], gp_ref[...])
        dab, dgp = gvjp(dgb_ref[0] + dgb_ref[1])
        dab_ref[...] = dab
        dgp_ref[...] += dgp

    in_specs = (_halo_specs(bt, T, CQ) + _halo_specs(bt, T, CK) + _halo_specs(bt, T, CV)
                + _halo_specs(bt, T, 0, lead=2) * 3
                + [pl.BlockSpec((bt, 128), lambda i: (i, 0)), pl.BlockSpec((2, bt, 128), lambda i: (0, i, 0)),
                   pl.BlockSpec((8, 3 * D), lambda i: (0, 0)), pl.BlockSpec((8, 128), lambda i: (0, 0)),
                   pl.BlockSpec(memory_space=pl.ANY)])
    out_specs = [pl.BlockSpec((bt, 3 * D), lambda i: (i, 0)), pl.BlockSpec((bt, 128), lambda i: (i, 0)),
                 pl.BlockSpec((8, 3 * D), lambda i: (0, 0)), pl.BlockSpec((8, 128), lambda i: (0, 0))]
    return pl.pallas_call(
        body, name="pre_qkv_bwd", grid=(n,), in_specs=in_specs, out_specs=out_specs,
        out_shape=[_sds(dpm.shape), _sds((T, 128)), _sds((8, 3 * D)), _sds((8, 128))],
        input_output_aliases={len(in_specs) - 1: 0},
        compiler_params=_cp(("arbitrary",)))(
            *([pm] * 9), dq2, dq2, dq2, dk2, dk2, dk2, dv2, dv2, dv2, pab, dgb2, cw, gp, dpm)


def pre_sc_fwd(pm, cw, *, bt=256):
    T = pm.shape[0]
    bt = min(bt, T)
    n = T // bt

    def body(b_ref, c0, c1, c2, x0, x1, x2, cw_ref, o_ref):
        i = pl.program_id(0)
        first, last = i == 0, i == n - 1
        pe = _ext(c0, c1, c2, first, last) * _ext(x0, x1, x2, first, last)
        o_ref[...] = (b_ref[...] * _conv3(pe, cw_ref[...])[HALO:HALO + bt]).astype(o_ref.dtype)

    in_specs = ([pl.BlockSpec((bt, D), lambda i: (i, CSB))] + _halo_specs(bt, T, CSC) + _halo_specs(bt, T, CSX)
                + [pl.BlockSpec((8, D), lambda i: (0, 0))])
    return pl.pallas_call(
        body, name="pre_sc_fwd", grid=(n,), in_specs=in_specs, out_specs=pl.BlockSpec((bt, D), lambda i: (i, 0)),
        out_shape=_sds((T, D), MM), compiler_params=_cp(("parallel",)))(*([pm] * 7), cw)


def pre_sc_bwd(pm, cw, dsc, dpm, *, bt=256):
    T = pm.shape[0]
    bt = min(bt, T)
    n = T // bt
    E = bt + 2 * HALO

    def body(b0, b1, b2, c0, c1, c2, x0, x1, x2, d0, d1, d2, cw_ref, _alias, o_ref, dcw_ref):
        i = pl.program_id(0)
        first, last = i == 0, i == n - 1

        @pl.when(first)
        def _():
            dcw_ref[...] = jnp.zeros_like(dcw_ref)

        ce, xe = _ext(c0, c1, c2, first, last), _ext(x0, x1, x2, first, last)
        pe = ce * xe
        w = cw_ref[...]
        dout = d0[...]
        o_ref[:, 0:D] = dout * _conv3(pe, w)[HALO:HALO + bt]
        dce = _ext(d0, d1, d2, first, last) * _ext(b0, b1, b2, first, last)
        dp = _conv3_t(dce, w)[HALO:HALO + bt]
        o_ref[:, D:2 * D] = dp * x0[...]
        o_ref[:, 2 * D:3 * D] = dp * c0[...]
        rowi = lax.broadcasted_iota(jnp.int32, (E, 1), 0)
        dc = jnp.where((rowi >= HALO) & (rowi < HALO + bt), dce, 0.0)
        row8 = lax.broadcasted_iota(jnp.int32, (8, D), 0)
        taps = (jnp.sum(dc * _shift_dn(pe), axis=0, keepdims=True), jnp.sum(dc * pe, axis=0, keepdims=True),
                jnp.sum(dc * _shift_up(pe), axis=0, keepdims=True))
        dcw_ref[...] += jnp.where(row8 == 0, taps[0], jnp.where(row8 == 1, taps[1], jnp.where(row8 == 2, taps[2], 0.0)))

    dsc_specs = [pl.BlockSpec((bt, D), lambda i: (i, 0)),
                 pl.BlockSpec((HALO, D), lambda i: (jnp.maximum(i * (bt // HALO) - 1, 0), 0)),
                 pl.BlockSpec((HALO, D), lambda i: (jnp.minimum((i + 1) * (bt // HALO), T // HALO - 1), 0))]
    in_specs = (_halo_specs(bt, T, CSB) + _halo_specs(bt, T, CSC) + _halo_specs(bt, T, CSX) + dsc_specs
                + [pl.BlockSpec((8, D), lambda i: (0, 0)), pl.BlockSpec(memory_space=pl.ANY)])
    return pl.pallas_call(
        body, name="pre_sc_bwd", grid=(n,), in_specs=in_specs,
        out_specs=[pl.BlockSpec((bt, 3 * D), lambda i: (i, 1)), pl.BlockSpec((8, D), lambda i: (0, 0))],
        out_shape=[_sds(dpm.shape), _sds((8, D))], input_output_aliases={len(in_specs) - 1: 0},
        compiler_params=_cp(("arbitrary",)))(*([pm] * 9), dsc, dsc, dsc, cw, dpm)


def _bdot(a, b, dims):
    return lax.dot_general(a.astype(MM), b.astype(MM), (dims, ((), ())), preferred_element_type=F32)


_NN, _NT, _TN = ((1,), (0,)), ((1,), (1,)), ((0,), (0,))


def _raw_nn(a, b):
    return _bdot(a, b, _NN)


def _raw_nt(a, b):
    return _bdot(a, b, _NT)


def _raw_tn(a, b):
    return _bdot(a, b, _TN)


def _tri_inv_y(A):
    Y = -A
    P = _raw_nn(A, A)
    Y = Y + P + _raw_nn(Y, P)
    for _ in range(4):
        P = _raw_nn(P, P)
        Y = Y + P + _raw_nn(Y, P)
    return Y


def _make_vjp_ops():
    @jax.custom_vjp
    def nn(a, b):
        return _raw_nn(a, b)

    @jax.custom_vjp
    def nt(a, b):
        return _raw_nt(a, b)

    @jax.custom_vjp
    def tn(a, b):
        return _raw_tn(a, b)

    nn.defvjp(lambda a, b: (_raw_nn(a, b), (a, b)), lambda r, g: (_raw_nt(g, r[1]), _raw_tn(r[0], g)))
    nt.defvjp(lambda a, b: (_raw_nt(a, b), (a, b)), lambda r, g: (_raw_nn(g, r[1]), _raw_tn(g, r[0])))
    tn.defvjp(lambda a, b: (_raw_tn(a, b), (a, b)), lambda r, g: (_raw_nt(r[1], g), _raw_nn(r[0], g)))

    @jax.custom_vjp
    def inv_saved(A, Y):
        return Y

    def inv_bwd(Y, g):
        M = g + _raw_tn(Y, g)
        return -(M + _raw_nt(M, Y)), jnp.zeros_like(Y)

    inv_saved.defvjp(lambda A, Y: (Y, Y), inv_bwd)
    return nn, nt, tn, inv_saved


def _intra(q, k, v, gcol, bcol, incl, strict, eye, lastsel, ops):
    nn, nt, tn, inv = ops
    grow = jnp.sum(eye * gcol, axis=0, keepdims=True)
    gam = jnp.where(incl, jnp.exp(jnp.where(incl, gcol - grow, 0.0)), 0.0)
    eg = jnp.exp(gcol)
    kb = k * bcol
    A = jnp.where(strict, nt(kb, k) * gam, 0.0)
    Y = inv(A)
    vb, kg = v * bcol, kb * eg
    u = vb + nn(Y, vb)
    w = kg + nn(Y, kg)
    P = nt(q, k) * gam
    glast = jnp.sum(jnp.where(lastsel, gcol, 0.0), axis=0, keepdims=True)
    return (u, w, P, q * eg, k * jnp.exp(glast - gcol), jnp.exp(glast)), Y


def _scan_step(S, u, w, P, qd, kd, egl, ops):
    nn, nt, tn, _ = ops
    vn = u - nn(w, S)
    o = nn(qd, S) + nn(P, vn)
    return o, S * egl + tn(kd, vn)


def _dir_masks(rev):
    r = lax.broadcasted_iota(jnp.int32, (CH, CH), 0)
    c = lax.broadcasted_iota(jnp.int32, (CH, CH), 1)
    ahead = jnp.where(rev, c - r, r - c)
    incl = ahead >= 0
    strict = ahead > 0
    eye = jnp.where(r == c, 1.0, 0.0).astype(F32)
    rr = lax.broadcasted_iota(jnp.int32, (CH, 1), 0)
    lastsel = rr == jnp.where(rev, 0, CH - 1)
    return incl, strict, eye, lastsel


def _head_gates(gb, h, rev):
    gcol = jnp.where(rev, gb[:, NH + h:NH + h + 1], gb[:, h:h + 1])
    bcol = jnp.where(rev, gb[:, 3 * NH + h:3 * NH + h + 1], gb[:, 2 * NH + h:2 * NH + h + 1])
    return gcol, bcol


def _hs(h, width=HD):
    return slice(h * HD, h * HD + width)


def gdn_intra_fwd(q, k, v, gb):
    T = q.shape[0]
    N = T // CH
    ops = (_raw_nn, _raw_nt, _raw_tn, _tri_inv_y)

    def body(q_ref, k_ref, v_ref, gb_ref, u_ref, w_ref, qd_ref, kd_ref, pp_ref, ys_ref, eg_ref):
        rev = pl.program_id(0) == 1
        masks = _dir_masks(rev)
        gb_t = gb_ref[...]
        for h in range(NH):
            s = _hs(h)
            gcol, bcol = _head_gates(gb_t, h, rev)
            (u, w, P, qd, kd, egl), Y = _intra(q_ref[:, s], k_ref[:, s], v_ref[:, s], gcol, bcol, *masks, ops)
            u_ref[:, s] = u
            w_ref[:, s] = w.astype(MM)
            qd_ref[:, s] = qd.astype(MM)
            kd_ref[:, s] = kd.astype(MM)
            pp_ref[:, _hs(h, CH)] = P.astype(MM)
            ys_ref[:, _hs(h, CH)] = Y.astype(MM)
            eg_ref[h:h + 1, :] = jnp.broadcast_to(egl, (1, 128))

    row = pl.BlockSpec((CH, D), lambda d, n: (n, 0))
    drow = pl.BlockSpec((None, CH, D), lambda d, n: (d, n, 0))
    return pl.pallas_call(
        body, name="gdn_intra_fwd", grid=(2, N),
        in_specs=[row, row, row, pl.BlockSpec((CH, 128), lambda d, n: (n, 0))],
        out_specs=[drow] * 6 + [pl.BlockSpec((None, None, NH, 128), lambda d, n: (d, n, 0, 0))],
        out_shape=[_sds((2, T, D))] + [_sds((2, T, D), MM)] * 5 + [_sds((2, N, NH, 128))],
        compiler_params=_cp(("parallel", "parallel")))(q, k, v, gb)


def gdn_scan_fwd(u, w, qd, kd, pp, eg):
    T = u.shape[1]
    N = T // CH
    ops = (_raw_nn, _raw_nt, _raw_tn, None)

    def body(u_ref, w_ref, qd_ref, kd_ref, pp_ref, eg_ref, o_ref, s0_ref, S):
        @pl.when(pl.program_id(1) == 0)
        def _():
            S[...] = jnp.zeros_like(S)

        for h in range(NH):
            s = _hs(h)
            Sh = S[h]
            s0_ref[h] = Sh
            o, S2 = _scan_step(Sh, u_ref[:, s], w_ref[:, s], pp_ref[:, _hs(h, CH)], qd_ref[:, s], kd_ref[:, s],
                               eg_ref[h:h + 1, :], ops)
            o_ref[:, s] = o
            S[h] = S2

    cidx = lambda d, n: n + d * (N - 1 - 2 * n)
    drow = pl.BlockSpec((None, CH, D), lambda d, n: (d, cidx(d, n), 0))
    return pl.pallas_call(
        body, name="gdn_scan_fwd", grid=(2, N),
        in_specs=[drow] * 5 + [pl.BlockSpec((None, None, NH, 128), lambda d, n: (d, cidx(d, n), 0, 0))],
        out_specs=[drow, pl.BlockSpec((None, None, NH, HD, HD), lambda d, n: (d, cidx(d, n), 0, 0, 0))],
        out_shape=[_sds((2, T, D)), _sds((2, N, NH, HD, HD))],
        scratch_shapes=[pltpu.VMEM((NH, HD, HD), F32)],
        compiler_params=_cp(("arbitrary", "arbitrary")))(u, w, qd, kd, pp, eg)


def gdn_scan_bwd(u, w, qd, kd, pp, eg, s0, do):
    T = u.shape[1]
    N = T // CH
    ops = _make_vjp_ops()

    def body(u_ref, w_ref, qd_ref, kd_ref, pp_ref, eg_ref, s0_ref, do_ref,
             du_ref, dw_ref, dqd_ref, dkd_ref, dpp_ref, deg_ref, dS):
        @pl.when(pl.program_id(1) == 0)
        def _():
            dS[...] = jnp.zeros_like(dS)

        for h in range(NH):
            s, sp = _hs(h), _hs(h, CH)
            step = lambda *a: _scan_step(*a, ops)
            _, vjp = jax.vjp(step, s0_ref[h], u_ref[:, s], w_ref[:, s].astype(F32), pp_ref[:, sp].astype(F32),
                             qd_ref[:, s].astype(F32), kd_ref[:, s].astype(F32), eg_ref[h:h + 1, :])
            dS_h, du, dw, dP, dqd, dkd, deg = vjp((do_ref[:, s], dS[h]))
            dS[h] = dS_h
            du_ref[:, s] = du.astype(MM)
            dw_ref[:, s] = dw.astype(MM)
            dqd_ref[:, s], dkd_ref[:, s], dpp_ref[:, sp] = dqd, dkd, dP
            deg_ref[h:h + 1, :] = deg

    cidx = lambda d, n: (N - 1 - n) + d * (2 * n - (N - 1))
    drow = pl.BlockSpec((None, CH, D), lambda d, n: (d, cidx(d, n), 0))
    erow = pl.BlockSpec((None, None, NH, 128), lambda d, n: (d, cidx(d, n), 0, 0))
    return pl.pallas_call(
        body, name="gdn_scan_bwd", grid=(2, N),
        in_specs=[drow] * 5 + [erow, pl.BlockSpec((None, None, NH, HD, HD), lambda d, n: (d, cidx(d, n), 0, 0, 0)),
                               pl.BlockSpec((CH, D), lambda d, n: (cidx(d, n), 0))],
        out_specs=[drow] * 5 + [erow],
        out_shape=[_sds((2, T, D), MM)] * 2 + [_sds((2, T, D))] * 3 + [_sds((2, N, NH, 128))],
        scratch_shapes=[pltpu.VMEM((NH, HD, HD), F32)],
        compiler_params=_cp(("arbitrary", "arbitrary")))(u, w, qd, kd, pp, eg, s0, do)


def gdn_intra_bwd(q, k, v, gb, ys, du, dw, dqd, dkd, dpp, deg):
    T = q.shape[0]
    N = T // CH
    nn, nt, tn, inv_saved = _make_vjp_ops()

    def body(q_ref, k_ref, v_ref, gb_ref, ys_ref, du_ref, dw_ref, dqd_ref, dkd_ref, dpp_ref, deg_ref,
             dq_ref, dk_ref, dv_ref, dgb_ref):
        rev = pl.program_id(0) == 1
        masks = _dir_masks(rev)
        gb_t = gb_ref[...]
        lane = lax.broadcasted_iota(jnp.int32, (CH, 128), 1)
        dgb = jnp.zeros((CH, 128), F32)
        for h in range(NH):
            s, sp = _hs(h), _hs(h, CH)
            gcol, bcol = _head_gates(gb_t, h, rev)
            Y = ys_ref[:, sp].astype(F32)
            f = lambda q_, k_, v_, g_, b_: _intra(q_, k_, v_, g_, b_, *masks, (nn, nt, tn, lambda A: inv_saved(A, Y)))
            _, vjp = jax.vjp(f, q_ref[:, s], k_ref[:, s], v_ref[:, s], gcol, bcol)
            degl = jnp.sum(deg_ref[h:h + 1, :], axis=1, keepdims=True)
            cts = (du_ref[:, s].astype(F32), dw_ref[:, s].astype(F32), dpp_ref[:, sp], dqd_ref[:, s], dkd_ref[:, s], degl)
            dq, dk, dv, dg, db = vjp((cts, jnp.zeros((CH, CH), F32)))
            dq_ref[:, s], dk_ref[:, s], dv_ref[:, s] = dq, dk, dv
            glane = jnp.where(rev, NH + h, h)
            dgb = dgb + jnp.where(lane == glane, dg, 0.0) + jnp.where(lane == glane + 2 * NH, db, 0.0)
        dgb_ref[...] = dgb

    row = pl.BlockSpec((CH, D), lambda d, n: (n, 0))
    drow = pl.BlockSpec((None, CH, D), lambda d, n: (d, n, 0))
    return pl.pallas_call(
        body, name="gdn_intra_bwd", grid=(2, N),
        in_specs=[row, row, row, pl.BlockSpec((CH, 128), lambda d, n: (n, 0))] + [drow] * 6
                 + [pl.BlockSpec((None, None, NH, 128), lambda d, n: (d, n, 0, 0))],
        out_specs=[drow, drow, drow, pl.BlockSpec((None, CH, 128), lambda d, n: (d, n, 0))],
        out_shape=[_sds((2, T, D))] * 3 + [_sds((2, T, 128))],
        compiler_params=_cp(("parallel", "parallel")))(q, k, v, gb, ys, du, dw, dqd, dkd, dpp, deg)


def _post_rows(o2a, o2b, z, nw):
    o = o2a + o2b
    outs = []
    for h in range(NH):
        s = slice(h * HD, (h + 1) * HD)
        oh = o[:, s]
        outs.append(oh * lax.rsqrt(jnp.mean(oh * oh, axis=-1, keepdims=True) + RMS_EPS) * nw * _silu(z[:, s]))
    return jnp.concatenate(outs, axis=1)


def post_fwd(o2, pm, nw, *, bt=512):
    T = pm.shape[0]
    bt = min(bt, T)

    def body(o_ref, z_ref, nw_ref, og_ref):
        og_ref[...] = _post_rows(o_ref[0], o_ref[1], z_ref[...], nw_ref[0:1, :]).astype(og_ref.dtype)

    return pl.pallas_call(
        body, name="post_fwd", grid=(T // bt,),
        in_specs=[pl.BlockSpec((2, bt, D), lambda i: (0, i, 0)), pl.BlockSpec((bt, D), lambda i: (i, CZ)),
                  pl.BlockSpec((8, 128), lambda i: (0, 0))],
        out_specs=pl.BlockSpec((bt, D), lambda i: (i, 0)), out_shape=_sds((T, D), MM),
        compiler_params=_cp(("parallel",)))(o2, pm, nw)


def post_bwd(o2, pm, nw, dog, dpm, *, bt=512):
    T = pm.shape[0]
    bt = min(bt, T)

    def body(o_ref, z_ref, nw_ref, dog_ref, _alias, do_ref, dz_ref, dnw_ref):
        @pl.when(pl.program_id(0) == 0)
        def _():
            dnw_ref[...] = jnp.zeros_like(dnw_ref)

        _, vjp = jax.vjp(_post_rows, o_ref[0], o_ref[1], z_ref[...], nw_ref[0:1, :])
        doa, _unused, dz, dnw = vjp(dog_ref[...])
        do_ref[...] = doa
        dz_ref[...] = dz
        row8 = lax.broadcasted_iota(jnp.int32, (8, 128), 0)
        dnw_ref[...] += jnp.where(row8 == 0, dnw, 0.0)

    in_specs = [pl.BlockSpec((2, bt, D), lambda i: (0, i, 0)), pl.BlockSpec((bt, D), lambda i: (i, CZ)),
                pl.BlockSpec((8, 128), lambda i: (0, 0)), pl.BlockSpec((bt, D), lambda i: (i, 0)),
                pl.BlockSpec(memory_space=pl.ANY)]
    return pl.pallas_call(
        body, name="post_bwd", grid=(T // bt,), in_specs=in_specs,
        out_specs=[pl.BlockSpec((bt, D), lambda i: (i, 0)), pl.BlockSpec((bt, D), lambda i: (i, CZ)),
                   pl.BlockSpec((8, 128), lambda i: (0, 0))],
        out_shape=[_sds((T, D)), _sds(dpm.shape), _sds((8, 128))], input_output_aliases={4: 1},
        compiler_params=_cp(("arbitrary",)))(o2, pm, nw, dog, dpm)


def merge_fwd(ya, yb, pm, *, bt=512):
    T = pm.shape[0]
    bt = min(bt, T)

    def body(ya_ref, yb_ref, ga_ref, gb_ref, o_ref):
        o_ref[...] = (_sigmoid(ga_ref[...]) * ya_ref[...] + _sigmoid(gb_ref[...]) * yb_ref[...]).astype(o_ref.dtype)

    row = pl.BlockSpec((bt, D), lambda i: (i, 0))
    return pl.pallas_call(
        body, name="merge_fwd", grid=(T // bt,),
        in_specs=[row, row, pl.BlockSpec((bt, D), lambda i: (i, CGA)), pl.BlockSpec((bt, D), lambda i: (i, CGB))],
        out_specs=row, out_shape=_sds((T, D), MM), compiler_params=_cp(("parallel",)))(ya, yb, pm, pm)


def merge_bwd(ya, yb, pm, dmix, *, bt=512):
    T = pm.shape[0]
    bt = min(bt, T)

    def body(ya_ref, yb_ref, ga_ref, gb_ref, dm_ref, dya_ref, dyb_ref, dg_ref):
        dm = dm_ref[...]
        sa, sb = _sigmoid(ga_ref[...]), _sigmoid(gb_ref[...])
        dya_ref[...] = (dm * sa).astype(dya_ref.dtype)
        dyb_ref[...] = (dm * sb).astype(dyb_ref.dtype)
        dg_ref[:, 0:D] = dm * ya_ref[...] * sa * (1.0 - sa)
        dg_ref[:, D:2 * D] = dm * yb_ref[...] * sb * (1.0 - sb)

    row = pl.BlockSpec((bt, D), lambda i: (i, 0))
    return pl.pallas_call(
        body, name="merge_bwd", grid=(T // bt,),
        in_specs=[row, row, pl.BlockSpec((bt, D), lambda i: (i, CGA)), pl.BlockSpec((bt, D), lambda i: (i, CGB)), row],
        out_specs=[row, row, pl.BlockSpec((bt, 2 * D), lambda i: (i, CGA // 2))],
        out_shape=[_sds((T, D), MM), _sds((T, D), MM), _sds((T, NMAIN))],
        compiler_params=_cp(("parallel",)))(ya, yb, pm, pm, dmix)


def _ln_rows(x, y, bias, g, b):
    r = ALPHA * x + y + bias
    mu = jnp.mean(r, axis=-1, keepdims=True)
    var = jnp.mean(jnp.square(r - mu), axis=-1, keepdims=True)
    return (r - mu) * lax.rsqrt(var + LN_EPS) * g + b


def ln_fwd(x, y, p, *, name, bt=512):
    T = x.shape[0]
    bt = min(bt, T)

    def body(x_ref, y_ref, p_ref, o_ref, ob_ref):
        r = _ln_rows(x_ref[...], y_ref[...], p_ref[0:1, :], p_ref[1:2, :], p_ref[2:3, :])
        o_ref[...] = r
        ob_ref[...] = r.astype(ob_ref.dtype)

    row = pl.BlockSpec((bt, D), lambda i: (i, 0))
    return pl.pallas_call(
        body, name=name, grid=(T // bt,), in_specs=[row, row, pl.BlockSpec((8, D), lambda i: (0, 0))],
        out_specs=[row, row], out_shape=[_sds((T, D)), _sds((T, D), MM)],
        compiler_params=_cp(("parallel",)))(x, y, p)


def ln_bwd(x, y, p, ct, ct2=None, *, name, bt=512):
    T = x.shape[0]
    bt = min(bt, T)

    def body(*refs):
        it = iter(refs)
        x_ref, y_ref, p_ref, c_ref = next(it), next(it), next(it), next(it)
        c2_ref = next(it) if ct2 is not None else None
        dxa_ref, dr_ref, dp_ref = next(it), next(it), next(it)

        @pl.when(pl.program_id(0) == 0)
        def _():
            dp_ref[...] = jnp.zeros_like(dp_ref)

        c = c_ref[...]
        if c2_ref is not None:
            c = c + c2_ref[...]
        _, vjp = jax.vjp(_ln_rows, x_ref[...], y_ref[...], p_ref[0:1, :], p_ref[1:2, :], p_ref[2:3, :])
        _dx, dy, dbias, dg, db = vjp(c)
        dxa_ref[...] = ALPHA * dy
        dr_ref[...] = dy.astype(dr_ref.dtype)
        row8 = lax.broadcasted_iota(jnp.int32, (8, D), 0)
        dp_ref[...] += jnp.where(row8 == 0, dbias, jnp.where(row8 == 1, dg, jnp.where(row8 == 2, db, 0.0)))

    row = pl.BlockSpec((bt, D), lambda i: (i, 0))
    in_specs = [row, row, pl.BlockSpec((8, D), lambda i: (0, 0)), row] + ([row] if ct2 is not None else [])
    args = [x, y, p, ct] + ([ct2] if ct2 is not None else [])
    return pl.pallas_call(
        body, name=name, grid=(T // bt,), in_specs=in_specs,
        out_specs=[row, row, pl.BlockSpec((8, D), lambda i: (0, 0))],
        out_shape=[_sds((T, D)), _sds((T, D), MM), _sds((8, D))],
        compiler_params=_cp(("arbitrary",)))(*args)


def loss_fwd_bwd(xl, target, *, bt=512):
    T = xl.shape[0]
    bt = min(bt, T)

    def body(x_ref, t_ref, l_ref, d_ref):
        @pl.when(pl.program_id(0) == 0)
        def _():
            l_ref[...] = jnp.zeros_like(l_ref)

        e = x_ref[...] - t_ref[...]
        d_ref[...] = e * (1.0 / D)
        l_ref[...] += 0.5 * jnp.sum(jnp.mean(e * e, axis=-1, keepdims=True), axis=0, keepdims=True)

    row = pl.BlockSpec((bt, D), lambda i: (i, 0))
    return pl.pallas_call(
        body, name="loss", grid=(T // bt,), in_specs=[row, row],
        out_specs=[pl.BlockSpec((8, 128), lambda i: (0, 0)), row], out_shape=[_sds((8, 128)), _sds((T, D))],
        compiler_params=_cp(("arbitrary",)))(xl, target)


def _row_tile(R, Cc, elems=1 << 18):
    if R * Cc <= elems:
        return R
    tr = 8
    while tr * 2 * Cc <= elems and R % (tr * 2) == 0:
        tr *= 2
    return tr


def adam(w, m, v, ga, gb=None, *, name):
    R, Cc = w.shape
    tr = _row_tile(R, Cc)

    def body(*refs):
        it = iter(refs)
        w_ref, m_ref, v_ref, a_ref = next(it), next(it), next(it), next(it)
        b_ref = next(it) if gb is not None else None
        g_ref, d_ref, mo_ref, vo_ref = next(it), next(it), next(it), next(it)
        g = a_ref[...]
        if b_ref is not None:
            g = g + b_ref[...]
        m2 = B1 * m_ref[...] + (1.0 - B1) * g
        v2 = B2 * v_ref[...] + (1.0 - B2) * jnp.square(g)
        m_hat = m2 / (1.0 - B1 ** STEP)
        v_hat = v2 / (1.0 - B2 ** STEP)
        g_ref[...] = g
        d_ref[...] = -LR * (m_hat / (jnp.sqrt(v_hat) + EPS) + WD * w_ref[...])
        mo_ref[...] = m2
        vo_ref[...] = v2

    blk = pl.BlockSpec((tr, Cc), lambda i: (i, 0))
    args = [w, m, v, ga] + ([gb] if gb is not None else [])
    return pl.pallas_call(
        body, name=name, grid=(R // tr,), in_specs=[blk] * len(args), out_specs=[blk] * 4,
        out_shape=[_sds((R, Cc))] * 4, compiler_params=_cp(("parallel",)))(*args)


def sum4(own, recv, *, name):
    R, Cc = own.shape
    tr = _row_tile(R, Cc)

    def body(o_ref, r_ref, out_ref):
        out_ref[...] = ((o_ref[...] + r_ref[0]) + r_ref[1]) + r_ref[2]

    return pl.pallas_call(
        body, name=name, grid=(R // tr,),
        in_specs=[pl.BlockSpec((tr, Cc), lambda i: (i, 0)), pl.BlockSpec((3, tr, Cc), lambda i: (0, i, 0))],
        out_specs=pl.BlockSpec((tr, Cc), lambda i: (i, 0)), out_shape=_sds((R, Cc)),
        compiler_params=_cp(("parallel",)))(own, recv)


def _place():
    return lax.axis_index("x"), lax.axis_index("y"), lax.axis_index("c")


def _other_chips(x, y):
    return [(1 - x, y), (x, 1 - y), (1 - x, 1 - y)]


_ANY = pl.BlockSpec(memory_space=pl.ANY)


def allgather_xy(arrs):
    n = len(arrs)

    def body(*refs):
        ins, outs = refs[:n], refs[n:2 * n]
        send, recv, loc = refs[2 * n:]
        x, y, c = _place()
        me = 2 * x + y
        peers = _other_chips(x, y)
        local = [pltpu.make_async_copy(ins[a], outs[a].at[me], loc.at[a]) for a in range(n)]
        for cp in local:
            cp.start()

        def remote(a, j, block):
            px, py = peers[j]
            return pltpu.make_async_remote_copy(
                src_ref=ins[a], dst_ref=outs[a].at[block], send_sem=send.at[3 * a + j], recv_sem=recv.at[3 * a + j],
                device_id=(px, py, c), device_id_type=MESH)

        sends = [remote(a, j, me) for a in range(n) for j in range(3)]
        for cp in sends:
            cp.start()
        for a in range(n):
            for j, (px, py) in enumerate(peers):
                remote(a, j, 2 * px + py).wait_recv()
        for cp in sends:
            cp.wait_send()
        for cp in local:
            cp.wait()

    return pl.pallas_call(
        body, name="allgather_xy", in_specs=[_ANY] * n, out_specs=[_ANY] * n,
        out_shape=[_sds((4,) + a.shape, a.dtype) for a in arrs],
        scratch_shapes=[pltpu.SemaphoreType.DMA((3 * n,)), pltpu.SemaphoreType.DMA((3 * n,)),
                        pltpu.SemaphoreType.DMA((n,))],
        compiler_params=pltpu.CompilerParams(has_side_effects=True))(*arrs)


def scatter_xy(groups):
    L = len(groups[0])
    n = len(groups) * L
    flat = [g for grp in groups for g in grp]

    def body(*refs):
        ins = refs[:n]
        owns, recvs = refs[n:n + len(groups)], refs[n + len(groups):n + 2 * len(groups)]
        send, recv, loc = refs[n + 2 * len(groups):]
        x, y, c = _place()
        me = 2 * x + y
        peers = _other_chips(x, y)
        local, sends = [], []
        for a in range(len(groups)):
            for l in range(L):
                i = a * L + l
                local.append(pltpu.make_async_copy(ins[i].at[me], owns[a].at[l], loc.at[i]))
                for j, (px, py) in enumerate(peers):
                    sends.append(pltpu.make_async_remote_copy(
                        src_ref=ins[i].at[2 * px + py], dst_ref=recvs[a].at[j, l], send_sem=send.at[3 * i + j],
                        recv_sem=recv.at[3 * i + j], device_id=(px, py, c), device_id_type=MESH))
        for cp in local + sends:
            cp.start()
        for cp in sends:
            cp.wait_recv()
        for cp in sends:
            cp.wait_send()
        for cp in local:
            cp.wait()

    out_shape = ([_sds((L,) + grp[0].shape[1:]) for grp in groups]
                 + [_sds((3, L) + grp[0].shape[1:]) for grp in groups])
    outs = pl.pallas_call(
        body, name="scatter_xy", in_specs=[_ANY] * n, out_specs=[_ANY] * (2 * len(groups)), out_shape=out_shape,
        scratch_shapes=[pltpu.SemaphoreType.DMA((3 * n,)), pltpu.SemaphoreType.DMA((3 * n,)),
                        pltpu.SemaphoreType.DMA((n,))],
        compiler_params=pltpu.CompilerParams(has_side_effects=True))(*flat)
    return outs[:len(groups)], outs[len(groups):]


def swap_c(arrs):
    n = len(arrs)

    def body(*refs):
        ins, outs = refs[:n], refs[n:2 * n]
        send, recv = refs[2 * n:]
        x, y, c = _place()
        cps = [pltpu.make_async_remote_copy(src_ref=ins[a], dst_ref=outs[a], send_sem=send.at[a], recv_sem=recv.at[a],
                                            device_id=(x, y, 1 - c), device_id_type=MESH) for a in range(n)]
        for cp in cps:
            cp.start()
        for cp in cps:
            cp.wait_recv()
        for cp in cps:
            cp.wait_send()

    return pl.pallas_call(
        body, name="swap_c", in_specs=[_ANY] * n, out_specs=[_ANY] * n, out_shape=[_sds(a.shape, a.dtype) for a in arrs],
        scratch_shapes=[pltpu.SemaphoreType.DMA((n,)), pltpu.SemaphoreType.DMA((n,))],
        compiler_params=pltpu.CompilerParams(has_side_effects=True))(*arrs)


def allreduce_small(v):
    R = v.shape[0]

    def body(v_ref, o_ref, buf, send, recv):
        x, y, c = _place()
        me = 4 * x + 2 * y + c
        buf[0] = v_ref[...]

        def cp(k):
            dx, dy, dc = (k >> 2) & 1, (k >> 1) & 1, k & 1
            return pltpu.make_async_remote_copy(
                src_ref=v_ref, dst_ref=buf.at[k], send_sem=send.at[k - 1], recv_sem=recv.at[k - 1],
                device_id=(x ^ dx, y ^ dy, c ^ dc), device_id_type=MESH)

        cps = [cp(k) for k in range(1, 8)]
        for t in cps:
            t.start()
        for t in cps:
            t.wait_recv()
        acc = buf[me]
        for dev in range(1, 8):
            acc = acc + buf[jnp.bitwise_xor(me, dev)]
        o_ref[...] = acc
        for t in cps:
            t.wait_send()

    vm = pl.BlockSpec(memory_space=pltpu.VMEM)
    return pl.pallas_call(
        body, name="allreduce_small", in_specs=[vm], out_specs=vm, out_shape=_sds((R, 128)),
        scratch_shapes=[pltpu.VMEM((8, R, 128), F32), pltpu.SemaphoreType.DMA((7,)), pltpu.SemaphoreType.DMA((7,))],
        compiler_params=pltpu.CompilerParams(has_side_effects=True, vmem_limit_bytes=VMEM_LIMIT))(v)


def _rows8(*rows):
    n = rows[0].shape[-1]
    t = jnp.stack([r.reshape(n).astype(F32) for r in rows])
    return jnp.pad(t, ((0, 8 - len(rows)), (0, 0)))


def _lanes128(a):
    f = a.reshape(-1).astype(F32)
    return jnp.pad(f, (0, 128 - f.shape[0]))


def _layer_fwd(x, xb, W):
    pm = mm_nn(xb, W["w_main"], name="proj_main")
    pab = mm_nn(xb, W["w_ab"], name="proj_ab")
    q, k, v, gb = pre_qkv_fwd(pm, pab, W["cw"], W["gp"])
    sc = pre_sc_fwd(pm, W["csc"])
    u, w, qd, kd, pp, ys, eg = gdn_intra_fwd(q, k, v, gb)
    o2, s0 = gdn_scan_fwd(u, w, qd, kd, pp, eg)
    og = post_fwd(o2, pm, W["nw"])
    ya = mm_nn(og, W["w_og"], name="proj_og")
    yb = mm_nn(sc, W["w_osc"], name="proj_osc")
    mixed = merge_fwd(ya, yb, pm)
    out = mm_nn(mixed, W["w_out"], name="proj_out")
    x1, x1b = ln_fwd(x, out, W["ln1"], name="ln1_fwd")
    hpre, h = mm_nn(x1b, W["w_up"], bias=W["b_up"], relu2=True, name="mlp_up")
    dn = mm_nn(h, W["w_down"], name="mlp_down")
    x2, x2b = ln_fwd(x1, dn, W["ln2"], name="ln2_fwd")
    saved = dict(x=x, xb=xb, pm=pm, pab=pab, q=q, k=k, v=v, gb=gb, sc=sc, o2=o2, s0=s0, og=og, ya=ya, yb=yb,
                 u=u, w=w, qd=qd, kd=kd, pp=pp, ys=ys, eg=eg,
                 mixed=mixed, out=out, x1=x1, x1b=x1b, hpre=hpre, h=h, dn=dn)
    return x2, x2b, saved


def _layer_bwd(ct, W, S):
    dxa2, dr2b, dp2 = ln_bwd(S["x1"], S["dn"], W["ln2"], ct, name="ln2_bwd")
    g_down = mm_tn(S["h"], dr2b, name="dw_down")
    dhpre, db_up = mm_nt(dr2b, W["w_down"], dact=S["hpre"], out_dtype=MM, name="mlp_down_bwd")
    g_up = mm_tn(S["x1b"], dhpre, name="dw_up")
    dx1 = mm_nt(dhpre, W["w_up"], add=dxa2, name="mlp_up_bwd")
    dxa1, dr1b, dp1 = ln_bwd(S["x"], S["out"], W["ln1"], dx1, name="ln1_bwd")
    g_out = mm_tn(S["mixed"], dr1b, name="dw_out")
    dmix = mm_nt(dr1b, W["w_out"], name="proj_out_bwd")
    dya, dyb, dpm = merge_bwd(S["ya"], S["yb"], S["pm"], dmix)
    g_og = mm_tn(S["og"], dya, name="dw_og")
    g_osc = mm_tn(S["sc"], dyb, name="dw_osc")
    dog = mm_nt(dya, W["w_og"], name="proj_og_bwd")
    dsc = mm_nt(dyb, W["w_osc"], name="proj_osc_bwd")
    do, dpm, dnw = post_bwd(S["o2"], S["pm"], W["nw"], dog, dpm)
    du, dw, dqd, dkd, dpp, deg = gdn_scan_bwd(S["u"], S["w"], S["qd"], S["kd"], S["pp"], S["eg"], S["s0"], do)
    dq2, dk2, dv2, dgb2 = gdn_intra_bwd(S["q"], S["k"], S["v"], S["gb"], S["ys"], du, dw, dqd, dkd, dpp, deg)
    dpm, dpab, dcw, dgp = pre_qkv_bwd(S["pm"], S["pab"], W["cw"], W["gp"], dq2, dk2, dv2, dgb2, dpm)
    dpm, dcsc = pre_sc_bwd(S["pm"], W["csc"], dsc, dpm)
    g_main = mm_tn(S["xb"], dpm, name="dw_main")
    g_ab = mm_tn(S["xb"], dpab, name="dw_ab")
    t = mm_nt(dpab, W["w_ab"], add=dxa1, name="proj_ab_bwd")
    dx = mm_nt(dpm, W["w_main"], add=t, name="proj_main_bwd")
    g_in = jnp.concatenate([g_main[:, :3 * D], g_main[:, 8 * D:], g_ab[:, :4 * NH], g_main[:, 3 * D:8 * D]], axis=1)
    grads = dict(
        w_in=g_in, w_o_gdn=g_og, w_o_sc=g_osc, w_out=g_out, w_up=g_up, w_down=g_down,
        conv_qkv=dcw[:3], conv_sc=dcsc[:3], a_log=dgp[0, :2 * NH].reshape(2, NH), dt_bias=dgp[1, :2 * NH].reshape(2, NH),
        gdn_norm_w=dnw[0], ln1_g=dp1[1], ln1_b=dp1[2], b_up=db_up[0], b_down=dp2[0], ln2_g=dp2[1], ln2_b=dp2[2])
    return dx, grads


def _layer_weights(l, full, a_log, dt_bias, gdn_norm_w, ln1_g, ln1_b, b_up, b_down, ln2_g, ln2_b):
    w_in = full["w_in"][l]
    w_main = jnp.concatenate([w_in[:, :3 * D], w_in[:, 4 * D + 4 * NH:], w_in[:, 3 * D:4 * D]], axis=1)
    w_ab = jnp.pad(w_in[:, 4 * D:4 * D + 4 * NH], ((0, 0), (0, 128 - 4 * NH)))
    return dict(
        w_main=w_main, w_ab=w_ab, w_og=full["w_o_gdn"][l], w_osc=full["w_o_sc"][l], w_out=full["w_out"][l],
        w_up=full["w_up"][l], w_down=full["w_down"][l],
        cw=jnp.pad(full["conv_qkv"][l].astype(F32), ((0, 5), (0, 0))),
        csc=jnp.pad(full["conv_sc"][l].astype(F32), ((0, 5), (0, 0))),
        gp=_rows8(_lanes128(a_log[l]), _lanes128(dt_bias[l])), nw=_rows8(gdn_norm_w[l]),
        ln1=_rows8(jnp.zeros((D,), F32), ln1_g[l], ln1_b[l]), ln2=_rows8(b_down[l], ln2_g[l], ln2_b[l]),
        b_up=b_up[l].reshape(1, DFF).astype(F32))


def local_step(xs, target, full, a_log, dt_bias, gdn_norm_w, ln1_g, ln1_b, b_up, b_down, ln2_g, ln2_b):
    Ws = [_layer_weights(l, full, a_log, dt_bias, gdn_norm_w, ln1_g, ln1_b, b_up, b_down, ln2_g, ln2_b)
          for l in range(DEPTH)]
    x, xb = xs, xs.astype(MM)
    saved = []
    for l in range(DEPTH):
        x, xb, S = _layer_fwd(x, xb, Ws[l])
        saved.append(S)
    loss_tile, ct = loss_fwd_bwd(x, target)
    grads = [None] * DEPTH
    for l in reversed(range(DEPTH)):
        ct, grads[l] = _layer_bwd(ct, Ws[l], saved[l])
    return loss_tile, ct, grads


BIG = ("w_in", "w_o_gdn", "w_o_sc", "w_out", "w_up", "w_down")
SMALL = ("conv_qkv", "a_log", "dt_bias", "gdn_norm_w", "conv_sc", "ln1_g", "ln1_b", "b_up", "b_down", "ln2_g", "ln2_b")
ORDER = ("w_in", "conv_qkv", "a_log", "dt_bias", "gdn_norm_w", "w_o_gdn", "conv_sc", "w_o_sc", "w_out", "ln1_g",
         "ln1_b", "w_up", "b_up", "w_down", "b_down", "ln2_g", "ln2_b")


def _pack(arrs):
    flat = jnp.concatenate([a.reshape(-1).astype(F32) for a in arrs])
    n = flat.shape[0]
    rows = -(-n // 1024) * 8
    return jnp.pad(flat, (0, rows * 128 - n)).reshape(rows, 128)


def _unpack(buf, like):
    flat = buf.reshape(-1)
    out, o = [], 0
    for a in like:
        n = 1
        for s in a.shape:
            n *= s
        out.append(flat[o:o + n].reshape(a.shape))
        o += n
    return out


def _gathered(name, g):
    if name in ("w_in", "w_up", "conv_qkv", "conv_sc"):
        t = jnp.moveaxis(g, 0, -2)
        return t.reshape(t.shape[:-2] + (t.shape[-2] * t.shape[-1],))
    t = jnp.moveaxis(g, 0, 1)
    return t.reshape((t.shape[0], t.shape[1] * t.shape[2]) + t.shape[3:])


def _by_chip(name, g):
    if name in ("w_in", "w_up"):
        r, ccols = g.shape
        return jnp.moveaxis(g.reshape(r, 4, ccols // 4), 1, 0)
    return g.reshape((4, g.shape[0] // 4) + g.shape[1:])


def kernel(x, w_in, conv_qkv, a_log, dt_bias, gdn_norm_w, w_o_gdn, conv_sc, w_o_sc, w_out, ln1_g, ln1_b, w_up, b_up, w_down, b_down, ln2_g, ln2_b, loss_target, m_w_in, m_conv_qkv, m_a_log, m_dt_bias, m_gdn_norm_w, m_w_o_gdn, m_conv_sc, m_w_o_sc, m_w_out, m_ln1_g, m_ln1_b, m_w_up, m_b_up, m_w_down, m_b_down, m_ln2_g, m_ln2_b, v_w_in, v_conv_qkv, v_a_log, v_dt_bias, v_gdn_norm_w, v_w_o_gdn, v_conv_sc, v_w_o_sc, v_w_out, v_ln1_g, v_ln1_b, v_w_up, v_b_up, v_w_down, v_b_down, v_ln2_g, v_ln2_b):
    w = dict(w_in=w_in, conv_qkv=conv_qkv, a_log=a_log, dt_bias=dt_bias, gdn_norm_w=gdn_norm_w, w_o_gdn=w_o_gdn,
             conv_sc=conv_sc, w_o_sc=w_o_sc, w_out=w_out, ln1_g=ln1_g, ln1_b=ln1_b, w_up=w_up, b_up=b_up,
             w_down=w_down, b_down=b_down, ln2_g=ln2_g, ln2_b=ln2_b)
    m = dict(w_in=m_w_in, conv_qkv=m_conv_qkv, a_log=m_a_log, dt_bias=m_dt_bias, gdn_norm_w=m_gdn_norm_w,
             w_o_gdn=m_w_o_gdn, conv_sc=m_conv_sc, w_o_sc=m_w_o_sc, w_out=m_w_out, ln1_g=m_ln1_g, ln1_b=m_ln1_b,
             w_up=m_w_up, b_up=m_b_up, w_down=m_w_down, b_down=m_b_down, ln2_g=m_ln2_g, ln2_b=m_ln2_b)
    v = dict(w_in=v_w_in, conv_qkv=v_conv_qkv, a_log=v_a_log, dt_bias=v_dt_bias, gdn_norm_w=v_gdn_norm_w,
             w_o_gdn=v_w_o_gdn, conv_sc=v_conv_sc, w_o_sc=v_w_o_sc, w_out=v_w_out, ln1_g=v_ln1_g, ln1_b=v_ln1_b,
             w_up=v_w_up, b_up=v_b_up, w_down=v_w_down, b_down=v_b_down, ln2_g=v_ln2_g, ln2_b=v_ln2_b)
    chip = 2 * lax.axis_index("x") + lax.axis_index("y")

    names = BIG + ("conv_qkv", "conv_sc")
    got = allgather_xy([w[n].astype(MM) if n in BIG else w[n] for n in names])
    full = {n: _gathered(n, g) for n, g in zip(names, got)}

    loss_tile, dx, grads = local_step(x[0], loss_target[0], full, a_log, dt_bias, gdn_norm_w, ln1_g, ln1_b, b_up,
                                      b_down, ln2_g, ln2_b)
    loss = lax.psum(loss_tile[0, 0], ("x", "y", "c"))

    groups = [[_by_chip(n, grads[l][n]) for l in range(DEPTH)] for n in BIG]
    owns, recvs = scatter_xy(groups)
    part = []
    for n, own, rec in zip(BIG, owns, recvs):
        cols = own.shape[-1]
        part.append(sum4(own.reshape(-1, cols), rec.reshape(3, -1, cols), name="sum_" + n))
    other = swap_c(part)
    out = {}
    for n, mine, theirs in zip(BIG, part, other):
        cols = mine.shape[-1]
        res = adam(w[n].reshape(-1, cols), m[n].reshape(-1, cols), v[n].reshape(-1, cols), mine, theirs, name="adam_" + n)
        out[n] = [r.reshape(w[n].shape) for r in res]

    stacked = [jnp.stack([grads[l][n] for l in range(DEPTH)]) for n in SMALL]
    summed = _unpack(allreduce_small(_pack(stacked)), stacked)
    gs = []
    for n, g in zip(SMALL, summed):
        if n in ("conv_qkv", "conv_sc"):
            blk = w[n].shape[-1]
            g = lax.dynamic_slice_in_dim(g, chip * blk, blk, axis=2)
        gs.append(g)
    res = adam(_pack([w[n] for n in SMALL]), _pack([m[n] for n in SMALL]), _pack([v[n] for n in SMALL]), _pack(gs),
               name="adam_small")
    for n, parts in zip(SMALL, zip(*[_unpack(r, gs) for r in res])):
        out[n] = list(parts)

    outs = [loss, dx[None]]
    for kind in range(4):
        outs += [out[n][kind] for n in ORDER]
    return tuple(outs)
```

```python
import functools

import jax
import jax.numpy as jnp
from jax import lax
from jax.experimental import pallas as pl
from jax.experimental.pallas import tpu as pltpu

F32 = jnp.float32
MM = jnp.bfloat16
HI = lax.Precision.HIGHEST

D = 1024
NH = 8
HD = 128
CH = 64
DFF = 4 * D
DEPTH = 4
LN_EPS = 1e-5
RMS_EPS = 1e-6
L2_EPS = 1e-6
ALPHA = (2 * DEPTH) ** 0.25
LR, B1, B2, EPS, WD, STEP = 0.001, 0.9, 0.999, 1e-08, 0.01, 10

NMAIN = 9 * D
CQ, CK, CV, CSB, CSC, CSX, CGA, CGB, CZ = range(9)
HALO = 8
VMEM_LIMIT = 56 * 1024 * 1024
MESH = pl.DeviceIdType.MESH


def _cp(sem=None, vmem=VMEM_LIMIT):
    return pltpu.CompilerParams(dimension_semantics=sem, vmem_limit_bytes=vmem)


def _sds(shape, dtype=F32):
    return jax.ShapeDtypeStruct(tuple(shape), dtype)


def mm_nn(a, b, *, name, bias=None, relu2=False, add=None, out_dtype=F32, tm=1024, tn=1024, tk=1024):
    M, K = a.shape
    N = b.shape[1]
    tm, tn, tk = min(tm, M), min(tn, N), min(tk, K)
    nk = K // tk

    def body(*refs):
        it = iter(refs)
        a_ref, b_ref = next(it), next(it)
        bias_ref = next(it) if bias is not None else None
        add_ref = next(it) if add is not None else None
        o_ref = next(it)
        h_ref = next(it) if relu2 else None
        acc = next(it)
        k = pl.program_id(2)

        @pl.when(k == 0)
        def _():
            acc[...] = jnp.zeros_like(acc)

        acc[...] += jnp.dot(a_ref[...].astype(MM), b_ref[...].astype(MM), preferred_element_type=F32)

        @pl.when(k == nk - 1)
        def _():
            r = acc[...]
            if bias_ref is not None:
                r = r + bias_ref[...]
            if add_ref is not None:
                r = r + add_ref[...]
            o_ref[...] = r.astype(o_ref.dtype)
            if relu2:
                t = jnp.maximum(r, 0.0)
                h_ref[...] = (t * t).astype(h_ref.dtype)

    in_specs = [pl.BlockSpec((tm, tk), lambda i, j, k: (i, k)), pl.BlockSpec((tk, tn), lambda i, j, k: (k, j))]
    args = [a, b]
    if bias is not None:
        in_specs.append(pl.BlockSpec((1, tn), lambda i, j, k: (0, j)))
        args.append(bias)
    if add is not None:
        in_specs.append(pl.BlockSpec((tm, tn), lambda i, j, k: (i, j)))
        args.append(add)
    out_shape = [_sds((M, N), out_dtype)]
    out_specs = [pl.BlockSpec((tm, tn), lambda i, j, k: (i, j))]
    if relu2:
        out_shape.append(_sds((M, N), MM))
        out_specs.append(pl.BlockSpec((tm, tn), lambda i, j, k: (i, j)))
    res = pl.pallas_call(
        body, name=name, grid=(M // tm, N // tn, nk), in_specs=in_specs, out_specs=out_specs, out_shape=out_shape,
        scratch_shapes=[pltpu.VMEM((tm, tn), F32)],
        compiler_params=_cp(("parallel", "parallel", "arbitrary")))(*args)
    return res if relu2 else res[0]


def mm_nt(a, b, *, name, add=None, dact=None, out_dtype=F32, tm=1024, tn=1024, tk=1024):
    M, Nc = a.shape
    Ko = b.shape[0]
    tm, tn, tk = min(tm, M), min(tn, Ko), min(tk, Nc)
    nk = Nc // tk
    ni = M // tm

    def body(*refs):
        it = iter(refs)
        a_ref, b_ref = next(it), next(it)
        add_ref = next(it) if add is not None else None
        d_ref = next(it) if dact is not None else None
        o_ref = next(it)
        db_ref = next(it) if dact is not None else None
        acc = next(it)
        i, k = pl.program_id(1), pl.program_id(2)

        @pl.when(k == 0)
        def _():
            acc[...] = jnp.zeros_like(acc)

        acc[...] += lax.dot_general(a_ref[...].astype(MM), b_ref[...].astype(MM), (((1,), (1,)), ((), ())),
                                    preferred_element_type=F32)

        @pl.when(k == nk - 1)
        def _():
            r = acc[...]
            if add_ref is not None:
                r = r + add_ref[...]
            if d_ref is not None:
                r = r * (2.0 * jnp.maximum(d_ref[...], 0.0))
                s = jnp.sum(r, axis=0, keepdims=True)
                row0 = lax.broadcasted_iota(jnp.int32, db_ref.shape, 0) == 0

                @pl.when(i == 0)
                def _():
                    db_ref[...] = jnp.zeros_like(db_ref)

                db_ref[...] += jnp.where(row0, s, 0.0)
            o_ref[...] = r.astype(o_ref.dtype)

    in_specs = [pl.BlockSpec((tm, tk), lambda j, i, k: (i, k)), pl.BlockSpec((tn, tk), lambda j, i, k: (j, k))]
    args = [a, b]
    for extra in (add, dact):
        if extra is not None:
            in_specs.append(pl.BlockSpec((tm, tn), lambda j, i, k: (i, j)))
            args.append(extra)
    out_shape = [_sds((M, Ko), out_dtype)]
    out_specs = [pl.BlockSpec((tm, tn), lambda j, i, k: (i, j))]
    if dact is not None:
        out_shape.append(_sds((8, Ko), F32))
        out_specs.append(pl.BlockSpec((8, tn), lambda j, i, k: (0, j)))
    res = pl.pallas_call(
        body, name=name, grid=(Ko // tn, ni, nk), in_specs=in_specs, out_specs=out_specs, out_shape=out_shape,
        scratch_shapes=[pltpu.VMEM((tm, tn), F32)],
        compiler_params=_cp(("parallel", "arbitrary", "arbitrary")))(*args)
    return res if dact is not None else res[0]


def mm_tn(a, b, *, name, tm=1024, tn=1024, tk=512):
    T, M = a.shape
    N = b.shape[1]
    tm, tn, tk = min(tm, M), min(tn, N), min(tk, T)

    def body(a_ref, b_ref, o_ref):
        @pl.when(pl.program_id(2) == 0)
        def _():
            o_ref[...] = jnp.zeros_like(o_ref)

        o_ref[...] += lax.dot_general(a_ref[...].astype(MM), b_ref[...].astype(MM), (((0,), (0,)), ((), ())),
                                      preferred_element_type=F32)

    return pl.pallas_call(
        body, name=name, grid=(M // tm, N // tn, T // tk),
        in_specs=[pl.BlockSpec((tk, tm), lambda i, j, k: (k, i)), pl.BlockSpec((tk, tn), lambda i, j, k: (k, j))],
        out_specs=pl.BlockSpec((tm, tn), lambda i, j, k: (i, j)), out_shape=_sds((M, N)),
        compiler_params=_cp(("parallel", "parallel", "arbitrary")))(a, b)


def _sigmoid(x):
    return 1.0 / (1.0 + jnp.exp(-x))


def _silu(x):
    return x * _sigmoid(x)


def _softplus(x):
    return jnp.maximum(x, 0.0) + jnp.log1p(jnp.exp(-jnp.abs(x)))


def _ext(main_ref, prev_ref, next_ref, first, last):
    p = jnp.where(first, 0.0, prev_ref[...].astype(F32))
    n = jnp.where(last, 0.0, next_ref[...].astype(F32))
    return jnp.concatenate([p, main_ref[...].astype(F32), n], axis=0)


def _shift_dn(x):
    return pltpu.roll(x, 1, 0)


def _shift_up(x):
    return pltpu.roll(x, x.shape[0] - 1, 0)


def _conv3(xe, w):
    return w[0:1, :] * _shift_dn(xe) + w[1:2, :] * xe + w[2:3, :] * _shift_up(xe)


def _conv3_t(de, w):
    return w[0:1, :] * _shift_up(de) + w[1:2, :] * de + w[2:3, :] * _shift_dn(de)


def _halo_specs(bt, T, col, lead=None):
    r = bt // HALO
    last = T // HALO - 1
    if lead is None:
        return [pl.BlockSpec((bt, D), lambda i: (i, col)),
                pl.BlockSpec((HALO, D), lambda i: (jnp.maximum(i * r - 1, 0), col)),
                pl.BlockSpec((HALO, D), lambda i: (jnp.minimum((i + 1) * r, last), col))]
    return [pl.BlockSpec((lead, bt, D), lambda i: (0, i, col)),
            pl.BlockSpec((lead, HALO, D), lambda i: (0, jnp.maximum(i * r - 1, 0), col)),
            pl.BlockSpec((lead, HALO, D), lambda i: (0, jnp.minimum((i + 1) * r, last), col))]


def _qkv_rows(cq, ck, cv):
    sq, sk, sv = _silu(cq), _silu(ck), _silu(cv)
    qs, ks = [], []
    for h in range(NH):
        s = slice(h * HD, (h + 1) * HD)
        qh, kh = sq[:, s], sk[:, s]
        qs.append(qh * lax.rsqrt(jnp.sum(qh * qh, axis=-1, keepdims=True) + L2_EPS) * (HD ** -0.5))
        ks.append(kh * lax.rsqrt(jnp.sum(kh * kh, axis=-1, keepdims=True) + L2_EPS))
    return jnp.concatenate(qs, axis=1), jnp.concatenate(ks, axis=1), sv


def _chunk_masks(bt):
    row = lax.broadcasted_iota(jnp.int32, (bt, bt), 0)
    col = lax.broadcasted_iota(jnp.int32, (bt, bt), 1)
    same = (row // CH) == (col // CH)
    lower = jnp.where(same & (col <= row), 1.0, 0.0).astype(F32)
    upper = jnp.where(same & (col >= row), 1.0, 0.0).astype(F32)
    return lower, upper


def _gate_rows(ab, gp, lower, upper):
    lane = lax.broadcasted_iota(jnp.int32, ab.shape, 1)
    g = -jnp.exp(gp[0:1, :]) * _softplus(ab + gp[1:2, :])
    g = jnp.where(lane < 2 * NH, g, 0.0)
    gf = jnp.dot(lower, g, precision=HI, preferred_element_type=F32)
    gr = jnp.dot(upper, g, precision=HI, preferred_element_type=F32)
    gc = jnp.where(lane < NH, gf, gr)
    beta = _sigmoid(ab)
    return jnp.where(lane < 2 * NH, gc, jnp.where(lane < 4 * NH, beta, 0.0))


def pre_qkv_fwd(pm, pab, cw, gp, *, bt=256):
    T = pm.shape[0]
    bt = min(bt, T)
    n = T // bt

    def body(q0, q1, q2, k0, k1, k2, v0, v1, v2, ab_ref, cw_ref, gp_ref, q_ref, k_ref, v_ref, gb_ref):
        i = pl.program_id(0)
        first, last = i == 0, i == n - 1
        cs = []
        for c, (m, p, x) in enumerate(((q0, q1, q2), (k0, k1, k2), (v0, v1, v2))):
            xe = _ext(m, p, x, first, last)
            cs.append(_conv3(xe, cw_ref[:, c * D:(c + 1) * D])[HALO:HALO + bt])
        q, k, v = _qkv_rows(*cs)
        q_ref[...], k_ref[...], v_ref[...] = q, k, v
        lower, upper = _chunk_masks(bt)
        gb_ref[...] = _gate_rows(ab_ref[...], gp_ref[...], lower, upper)

    in_specs = (_halo_specs(bt, T, CQ) + _halo_specs(bt, T, CK) + _halo_specs(bt, T, CV)
                + [pl.BlockSpec((bt, 128), lambda i: (i, 0)), pl.BlockSpec((8, 3 * D), lambda i: (0, 0)),
                   pl.BlockSpec((8, 128), lambda i: (0, 0))])
    row = pl.BlockSpec((bt, D), lambda i: (i, 0))
    return pl.pallas_call(
        body, name="pre_qkv_fwd", grid=(n,), in_specs=in_specs,
        out_specs=[row, row, row, pl.BlockSpec((bt, 128), lambda i: (i, 0))],
        out_shape=[_sds((T, D)), _sds((T, D)), _sds((T, D)), _sds((T, 128))],
        compiler_params=_cp(("parallel",)))(*([pm] * 9), pab, cw, gp)


def pre_qkv_bwd(pm, pab, cw, gp, dq2, dk2, dv2, dgb2, dpm, *, bt=128):
    T = pm.shape[0]
    bt = min(bt, T)
    n = T // bt
    E = bt + 2 * HALO

    def body(*refs):
        it = iter(refs)
        xs = [[next(it) for _ in range(3)] for _ in range(3)]
        ds = [[next(it) for _ in range(3)] for _ in range(3)]
        ab_ref, dgb_ref, cw_ref, gp_ref, _alias = next(it), next(it), next(it), next(it), next(it)
        o_ref, dab_ref, dcw_ref, dgp_ref = (next(it) for _ in range(4))
        i = pl.program_id(0)
        first, last = i == 0, i == n - 1

        @pl.when(first)
        def _():
            dcw_ref[...] = jnp.zeros_like(dcw_ref)
            dgp_ref[...] = jnp.zeros_like(dgp_ref)

        xes = [_ext(*xs[c], first, last) for c in range(3)]
        ces = [_conv3(xes[c], cw_ref[:, c * D:(c + 1) * D]) for c in range(3)]
        cts = []
        for c in range(3):
            m, p, x = ds[c]
            pe = jnp.where(first, 0.0, p[0] + p[1])
            ne = jnp.where(last, 0.0, x[0] + x[1])
            cts.append(jnp.concatenate([pe, m[0] + m[1], ne], axis=0))
        _, vjp = jax.vjp(_qkv_rows, *ces)
        dces = vjp(tuple(cts))
        rowi = lax.broadcasted_iota(jnp.int32, (E, 1), 0)
        central = (rowi >= HALO) & (rowi < HALO + bt)
        row8 = lax.broadcasted_iota(jnp.int32, (8, D), 0)
        for c in range(3):
            w = cw_ref[:, c * D:(c + 1) * D]
            o_ref[:, c * D:(c + 1) * D] = _conv3_t(dces[c], w)[HALO:HALO + bt]
            dc = jnp.where(central, dces[c], 0.0)
            taps = (jnp.sum(dc * _shift_dn(xes[c]), axis=0, keepdims=True),
                    jnp.sum(dc * xes[c], axis=0, keepdims=True),
                    jnp.sum(dc * _shift_up(xes[c]), axis=0, keepdims=True))
            upd = jnp.where(row8 == 0, taps[0], jnp.where(row8 == 1, taps[1], jnp.where(row8 == 2, taps[2], 0.0)))
            dcw_ref[:, c * D:(c + 1) * D] += upd
        lower, upper = _chunk_masks(bt)
        _, gvjp = jax.vjp(lambda ab, gp: _gate_rows(ab, gp, lower, upper), ab_ref[...], gp_ref[...])
        dab, dgp = gvjp(dgb_ref[0] + dgb_ref[1])
        dab_ref[...] = dab
        dgp_ref[...] += dgp

    in_specs = (_halo_specs(bt, T, CQ) + _halo_specs(bt, T, CK) + _halo_specs(bt, T, CV)
                + _halo_specs(bt, T, 0, lead=2) * 3
                + [pl.BlockSpec((bt, 128), lambda i: (i, 0)), pl.BlockSpec((2, bt, 128), lambda i: (0, i, 0)),
                   pl.BlockSpec((8, 3 * D), lambda i: (0, 0)), pl.BlockSpec((8, 128), lambda i: (0, 0)),
                   pl.BlockSpec(memory_space=pl.ANY)])
    out_specs = [pl.BlockSpec((bt, 3 * D), lambda i: (i, 0)), pl.BlockSpec((bt, 128), lambda i: (i, 0)),
                 pl.BlockSpec((8, 3 * D), lambda i: (0, 0)), pl.BlockSpec((8, 128), lambda i: (0, 0))]
    return pl.pallas_call(
        body, name="pre_qkv_bwd", grid=(n,), in_specs=in_specs, out_specs=out_specs,
        out_shape=[_sds(dpm.shape), _sds((T, 128)), _sds((8, 3 * D)), _sds((8, 128))],
        input_output_aliases={len(in_specs) - 1: 0},
        compiler_params=_cp(("arbitrary",)))(
            *([pm] * 9), dq2, dq2, dq2, dk2, dk2, dk2, dv2, dv2, dv2, pab, dgb2, cw, gp, dpm)


def pre_sc_fwd(pm, cw, *, bt=256):
    T = pm.shape[0]
    bt = min(bt, T)
    n = T // bt

    def body(b_ref, c0, c1, c2, x0, x1, x2, cw_ref, o_ref):
        i = pl.program_id(0)
        first, last = i == 0, i == n - 1
        pe = _ext(c0, c1, c2, first, last) * _ext(x0, x1, x2, first, last)
        o_ref[...] = (b_ref[...] * _conv3(pe, cw_ref[...])[HALO:HALO + bt]).astype(o_ref.dtype)

    in_specs = ([pl.BlockSpec((bt, D), lambda i: (i, CSB))] + _halo_specs(bt, T, CSC) + _halo_specs(bt, T, CSX)
                + [pl.BlockSpec((8, D), lambda i: (0, 0))])
    return pl.pallas_call(
        body, name="pre_sc_fwd", grid=(n,), in_specs=in_specs, out_specs=pl.BlockSpec((bt, D), lambda i: (i, 0)),
        out_shape=_sds((T, D), MM), compiler_params=_cp(("parallel",)))(*([pm] * 7), cw)


def pre_sc_bwd(pm, cw, dsc, dpm, *, bt=256):
    T = pm.shape[0]
    bt = min(bt, T)
    n = T // bt
    E = bt + 2 * HALO

    def body(b0, b1, b2, c0, c1, c2, x0, x1, x2, d0, d1, d2, cw_ref, _alias, o_ref, dcw_ref):
        i = pl.program_id(0)
        first, last = i == 0, i == n - 1

        @pl.when(first)
        def _():
            dcw_ref[...] = jnp.zeros_like(dcw_ref)

        ce, xe = _ext(c0, c1, c2, first, last), _ext(x0, x1, x2, first, last)
        pe = ce * xe
        w = cw_ref[...]
        dout = d0[...]
        o_ref[:, 0:D] = dout * _conv3(pe, w)[HALO:HALO + bt]
        dce = _ext(d0, d1, d2, first, last) * _ext(b0, b1, b2, first, last)
        dp = _conv3_t(dce, w)[HALO:HALO + bt]
        o_ref[:, D:2 * D] = dp * x0[...]
        o_ref[:, 2 * D:3 * D] = dp * c0[...]
        rowi = lax.broadcasted_iota(jnp.int32, (E, 1), 0)
        dc = jnp.where((rowi >= HALO) & (rowi < HALO + bt), dce, 0.0)
        row8 = lax.broadcasted_iota(jnp.int32, (8, D), 0)
        taps = (jnp.sum(dc * _shift_dn(pe), axis=0, keepdims=True), jnp.sum(dc * pe, axis=0, keepdims=True),
                jnp.sum(dc * _shift_up(pe), axis=0, keepdims=True))
        dcw_ref[...] += jnp.where(row8 == 0, taps[0], jnp.where(row8 == 1, taps[1], jnp.where(row8 == 2, taps[2], 0.0)))

    dsc_specs = [pl.BlockSpec((bt, D), lambda i: (i, 0)),
                 pl.BlockSpec((HALO, D), lambda i: (jnp.maximum(i * (bt // HALO) - 1, 0), 0)),
                 pl.BlockSpec((HALO, D), lambda i: (jnp.minimum((i + 1) * (bt // HALO), T // HALO - 1), 0))]
    in_specs = (_halo_specs(bt, T, CSB) + _halo_specs(bt, T, CSC) + _halo_specs(bt, T, CSX) + dsc_specs
                + [pl.BlockSpec((8, D), lambda i: (0, 0)), pl.BlockSpec(memory_space=pl.ANY)])
    return pl.pallas_call(
        body, name="pre_sc_bwd", grid=(n,), in_specs=in_specs,
        out_specs=[pl.BlockSpec((bt, 3 * D), lambda i: (i, 1)), pl.BlockSpec((8, D), lambda i: (0, 0))],
        out_shape=[_sds(dpm.shape), _sds((8, D))], input_output_aliases={len(in_specs) - 1: 0},
        compiler_params=_cp(("arbitrary",)))(*([pm] * 9), dsc, dsc, dsc, cw, dpm)


def _bdot(a, b, dims):
    return lax.dot_general(a.astype(MM), b.astype(MM), (dims, ((), ())), preferred_element_type=F32)


_NN, _NT, _TN = ((1,), (0,)), ((1,), (1,)), ((0,), (0,))


def _raw_nn(a, b):
    return _bdot(a, b, _NN)


def _raw_nt(a, b):
    return _bdot(a, b, _NT)


def _raw_tn(a, b):
    return _bdot(a, b, _TN)


def _tri_inv_y(A):
    Y = -A
    P = _raw_nn(A, A)
    Y = Y + P + _raw_nn(Y, P)
    for _ in range(4):
        P = _raw_nn(P, P)
        Y = Y + P + _raw_nn(Y, P)
    return Y


def _make_vjp_ops():
    @jax.custom_vjp
    def nn(a, b):
        return _raw_nn(a, b)

    @jax.custom_vjp
    def nt(a, b):
        return _raw_nt(a, b)

    @jax.custom_vjp
    def tn(a, b):
        return _raw_tn(a, b)

    nn.defvjp(lambda a, b: (_raw_nn(a, b), (a, b)), lambda r, g: (_raw_nt(g, r[1]), _raw_tn(r[0], g)))
    nt.defvjp(lambda a, b: (_raw_nt(a, b), (a, b)), lambda r, g: (_raw_nn(g, r[1]), _raw_tn(g, r[0])))
    tn.defvjp(lambda a, b: (_raw_tn(a, b), (a, b)), lambda r, g: (_raw_nt(r[1], g), _raw_nn(r[0], g)))

    @jax.custom_vjp
    def inv_saved(A, Y):
        return Y

    def inv_bwd(Y, g):
        M = g + _raw_tn(Y, g)
        return -(M + _raw_nt(M, Y)), jnp.zeros_like(Y)

    inv_saved.defvjp(lambda A, Y: (Y, Y), inv_bwd)
    return nn, nt, tn, inv_saved


GH = 4
GR = GH * CH
NG = NH // GH


def _intra_group(qs, ks, vs, gcols, bcols, incl, strict, eye, lastc, ops):
    nn, nt, tn, inv = ops
    q, k, v = (jnp.concatenate(t, axis=0) for t in (qs, ks, vs))
    gcol, bcol = jnp.concatenate(gcols, axis=0), jnp.concatenate(bcols, axis=0)
    grow = jnp.sum(eye * gcol, axis=0, keepdims=True)
    gam = jnp.where(incl, jnp.exp(jnp.where(incl, gcol - grow, 0.0)), 0.0)
    glast = jnp.sum(jnp.where(lastc, grow, 0.0), axis=1, keepdims=True)
    eg = jnp.exp(gcol)
    kb = k * bcol
    A = jnp.where(strict, nt(kb, k) * gam, 0.0)
    Y = inv(A)
    vb, kg = v * bcol, kb * eg
    u = vb + nn(Y, vb)
    w = kg + nn(Y, kg)
    P = nt(q, k) * gam
    return (u, w, P, q * eg, k * jnp.exp(glast - gcol), jnp.exp(glast)), Y


def _scan_group(Ss, us, ws, P, qds, kds, egls, ops):
    nn, nt, tn, _ = ops
    vns = [us[j] - nn(ws[j], Ss[j]) for j in range(GH)]
    o = jnp.concatenate([nn(qds[j], Ss[j]) for j in range(GH)], axis=0) + nn(P, jnp.concatenate(vns, axis=0))
    return o, [Ss[j] * egls[j] + tn(kds[j], vns[j]) for j in range(GH)]


def _group_masks(rev):
    r = lax.broadcasted_iota(jnp.int32, (GR, GR), 0)
    c = lax.broadcasted_iota(jnp.int32, (GR, GR), 1)
    same = (r // CH) == (c // CH)
    ahead = jnp.where(rev, c - r, r - c)
    incl = same & (ahead >= 0)
    strict = same & (ahead > 0)
    eye = jnp.where(r == c, 1.0, 0.0).astype(F32)
    lastc = same & ((c % CH) == jnp.where(rev, 0, CH - 1))
    return incl, strict, eye, lastc


def _head_gates(gb, h, rev):
    gcol = jnp.where(rev, gb[:, NH + h:NH + h + 1], gb[:, h:h + 1])
    bcol = jnp.where(rev, gb[:, 3 * NH + h:3 * NH + h + 1], gb[:, 2 * NH + h:2 * NH + h + 1])
    return gcol, bcol


def _hs(h, width=HD):
    return slice(h * HD, h * HD + width)


def gdn_intra_fwd(q, k, v, gb):
    T = q.shape[0]
    N = T // CH
    ops = (_raw_nn, _raw_nt, _raw_tn, _tri_inv_y)

    def body(q_ref, k_ref, v_ref, gb_ref, u_ref, w_ref, qd_ref, kd_ref, pp_ref, ys_ref, eg_ref):
        rev = pl.program_id(0) == 1
        masks = _group_masks(rev)
        gb_t = gb_ref[...]
        for g in range(NG):
            heads = range(g * GH, (g + 1) * GH)
            gates = [_head_gates(gb_t, h, rev) for h in heads]
            (u, w, P, qd, kd, egl), Y = _intra_group(
                [q_ref[:, _hs(h)] for h in heads], [k_ref[:, _hs(h)] for h in heads], [v_ref[:, _hs(h)] for h in heads],
                [t[0] for t in gates], [t[1] for t in gates], *masks, ops)
            pp_ref[g] = P.astype(MM)
            ys_ref[g] = Y.astype(MM)
            for j, h in enumerate(heads):
                rows = slice(j * CH, (j + 1) * CH)
                u_ref[:, _hs(h)] = u[rows]
                w_ref[:, _hs(h)] = w[rows].astype(MM)
                qd_ref[:, _hs(h)] = qd[rows].astype(MM)
                kd_ref[:, _hs(h)] = kd[rows].astype(MM)
                eg_ref[h:h + 1, :] = jnp.broadcast_to(egl[j * CH:j * CH + 1, :], (1, 128))

    row = pl.BlockSpec((CH, D), lambda d, n: (n, 0))
    drow = pl.BlockSpec((None, CH, D), lambda d, n: (d, n, 0))
    mat = pl.BlockSpec((None, None, NG, GR, GR), lambda d, n: (d, n, 0, 0, 0))
    return pl.pallas_call(
        body, name="gdn_intra_fwd", grid=(2, N),
        in_specs=[row, row, row, pl.BlockSpec((CH, 128), lambda d, n: (n, 0))],
        out_specs=[drow] * 4 + [mat, mat, pl.BlockSpec((None, None, NH, 128), lambda d, n: (d, n, 0, 0))],
        out_shape=[_sds((2, T, D))] + [_sds((2, T, D), MM)] * 3 + [_sds((2, N, NG, GR, GR), MM)] * 2
                  + [_sds((2, N, NH, 128))],
        compiler_params=_cp(("parallel", "parallel")))(q, k, v, gb)


def gdn_scan_fwd(u, w, qd, kd, pp, eg):
    T = u.shape[1]
    N = T // CH
    ops = (_raw_nn, _raw_nt, _raw_tn, None)

    def body(u_ref, w_ref, qd_ref, kd_ref, pp_ref, eg_ref, o_ref, s0_ref, S):
        @pl.when(pl.program_id(1) == 0)
        def _():
            S[...] = jnp.zeros_like(S)

        for g in range(NG):
            heads = range(g * GH, (g + 1) * GH)
            Ss = [S[h] for h in heads]
            for h, Sh in zip(heads, Ss):
                s0_ref[h] = Sh
            o, S2 = _scan_group(Ss, [u_ref[:, _hs(h)] for h in heads], [w_ref[:, _hs(h)] for h in heads], pp_ref[g],
                                [qd_ref[:, _hs(h)] for h in heads], [kd_ref[:, _hs(h)] for h in heads],
                                [eg_ref[h:h + 1, :] for h in heads], ops)
            for j, h in enumerate(heads):
                o_ref[:, _hs(h)] = o[j * CH:(j + 1) * CH]
                S[h] = S2[j]

    cidx = lambda d, n: n + d * (N - 1 - 2 * n)
    drow = pl.BlockSpec((None, CH, D), lambda d, n: (d, cidx(d, n), 0))
    mat = pl.BlockSpec((None, None, NG, GR, GR), lambda d, n: (d, cidx(d, n), 0, 0, 0))
    return pl.pallas_call(
        body, name="gdn_scan_fwd", grid=(2, N),
        in_specs=[drow] * 4 + [mat, pl.BlockSpec((None, None, NH, 128), lambda d, n: (d, cidx(d, n), 0, 0))],
        out_specs=[drow, pl.BlockSpec((None, None, NH, HD, HD), lambda d, n: (d, cidx(d, n), 0, 0, 0))],
        out_shape=[_sds((2, T, D)), _sds((2, N, NH, HD, HD))],
        scratch_shapes=[pltpu.VMEM((NH, HD, HD), F32)],
        compiler_params=_cp(("arbitrary", "arbitrary")))(u, w, qd, kd, pp, eg)


def gdn_scan_bwd(u, w, qd, kd, pp, eg, s0, do):
    T = u.shape[1]
    N = T // CH
    ops = _make_vjp_ops()

    def body(u_ref, w_ref, qd_ref, kd_ref, pp_ref, eg_ref, s0_ref, do_ref,
             du_ref, dw_ref, dqd_ref, dkd_ref, dpp_ref, deg_ref, dS):
        @pl.when(pl.program_id(1) == 0)
        def _():
            dS[...] = jnp.zeros_like(dS)

        for g in range(NG):
            heads = range(g * GH, (g + 1) * GH)
            step = lambda *a: _scan_group(*a, ops)
            _, vjp = jax.vjp(
                step, [s0_ref[h] for h in heads], [u_ref[:, _hs(h)] for h in heads],
                [w_ref[:, _hs(h)].astype(F32) for h in heads], pp_ref[g].astype(F32),
                [qd_ref[:, _hs(h)].astype(F32) for h in heads], [kd_ref[:, _hs(h)].astype(F32) for h in heads],
                [eg_ref[h:h + 1, :] for h in heads])
            do = jnp.concatenate([do_ref[:, _hs(h)] for h in heads], axis=0)
            dSs, dus, dws, dP, dqds, dkds, degs = vjp((do, [dS[h] for h in heads]))
            dpp_ref[g] = dP
            for j, h in enumerate(heads):
                dS[h] = dSs[j]
                du_ref[:, _hs(h)] = dus[j].astype(MM)
                dw_ref[:, _hs(h)] = dws[j].astype(MM)
                dqd_ref[:, _hs(h)], dkd_ref[:, _hs(h)] = dqds[j], dkds[j]
                deg_ref[h:h + 1, :] = degs[j]

    cidx = lambda d, n: (N - 1 - n) + d * (2 * n - (N - 1))
    drow = pl.BlockSpec((None, CH, D), lambda d, n: (d, cidx(d, n), 0))
    erow = pl.BlockSpec((None, None, NH, 128), lambda d, n: (d, cidx(d, n), 0, 0))
    mat = pl.BlockSpec((None, None, NG, GR, GR), lambda d, n: (d, cidx(d, n), 0, 0, 0))
    return pl.pallas_call(
        body, name="gdn_scan_bwd", grid=(2, N),
        in_specs=[drow] * 4 + [mat, erow, pl.BlockSpec((None, None, NH, HD, HD), lambda d, n: (d, cidx(d, n), 0, 0, 0)),
                               pl.BlockSpec((CH, D), lambda d, n: (cidx(d, n), 0))],
        out_specs=[drow] * 4 + [mat, erow],
        out_shape=[_sds((2, T, D), MM)] * 2 + [_sds((2, T, D))] * 2 + [_sds((2, N, NG, GR, GR))]
                  + [_sds((2, N, NH, 128))],
        scratch_shapes=[pltpu.VMEM((NH, HD, HD), F32)],
        compiler_params=_cp(("arbitrary", "arbitrary")))(u, w, qd, kd, pp, eg, s0, do)


def gdn_intra_bwd(q, k, v, gb, ys, du, dw, dqd, dkd, dpp, deg):
    T = q.shape[0]
    N = T // CH
    nn, nt, tn, inv_saved = _make_vjp_ops()

    def body(q_ref, k_ref, v_ref, gb_ref, ys_ref, du_ref, dw_ref, dqd_ref, dkd_ref, dpp_ref, deg_ref,
             dq_ref, dk_ref, dv_ref, dgb_ref):
        rev = pl.program_id(0) == 1
        masks = _group_masks(rev)
        gb_t = gb_ref[...]
        lane = lax.broadcasted_iota(jnp.int32, (CH, 128), 1)
        grow = lax.broadcasted_iota(jnp.int32, (GR, 1), 0)
        dgb = jnp.zeros((CH, 128), F32)
        for g in range(NG):
            heads = range(g * GH, (g + 1) * GH)
            gates = [_head_gates(gb_t, h, rev) for h in heads]
            Y = ys_ref[g].astype(F32)
            f = lambda *a: _intra_group(*a, *masks, (nn, nt, tn, lambda A: inv_saved(A, Y)))
            _, vjp = jax.vjp(f, [q_ref[:, _hs(h)] for h in heads], [k_ref[:, _hs(h)] for h in heads],
                             [v_ref[:, _hs(h)] for h in heads], [t[0] for t in gates], [t[1] for t in gates])
            stack = lambda ref: jnp.concatenate([ref[:, _hs(h)].astype(F32) for h in heads], axis=0)
            degl = jnp.zeros((GR, 1), F32)
            for j, h in enumerate(heads):
                degl = degl + jnp.where(grow == j * CH, jnp.sum(deg_ref[h:h + 1, :], axis=1, keepdims=True), 0.0)
            cts = (stack(du_ref), stack(dw_ref), dpp_ref[g], stack(dqd_ref), stack(dkd_ref), degl)
            dqs, dks, dvs, dgs, dbs = vjp((cts, jnp.zeros((GR, GR), F32)))
            for j, h in enumerate(heads):
                dq_ref[:, _hs(h)], dk_ref[:, _hs(h)], dv_ref[:, _hs(h)] = dqs[j], dks[j], dvs[j]
                glane = jnp.where(rev, NH + h, h)
                dgb = dgb + jnp.where(lane == glane, dgs[j], 0.0) + jnp.where(lane == glane + 2 * NH, dbs[j], 0.0)
        dgb_ref[...] = dgb

    row = pl.BlockSpec((CH, D), lambda d, n: (n, 0))
    drow = pl.BlockSpec((None, CH, D), lambda d, n: (d, n, 0))
    mat = pl.BlockSpec((None, None, NG, GR, GR), lambda d, n: (d, n, 0, 0, 0))
    return pl.pallas_call(
        body, name="gdn_intra_bwd", grid=(2, N),
        in_specs=[row, row, row, pl.BlockSpec((CH, 128), lambda d, n: (n, 0)), mat, drow, drow, drow, drow, mat]
                 + [pl.BlockSpec((None, None, NH, 128), lambda d, n: (d, n, 0, 0))],
        out_specs=[drow, drow, drow, pl.BlockSpec((None, CH, 128), lambda d, n: (d, n, 0))],
        out_shape=[_sds((2, T, D))] * 3 + [_sds((2, T, 128))],
        compiler_params=_cp(("parallel", "parallel")))(q, k, v, gb, ys, du, dw, dqd, dkd, dpp, deg)


def _post_rows(o2a, o2b, z, nw):
    o = o2a + o2b
    outs = []
    for h in range(NH):
        s = slice(h * HD, (h + 1) * HD)
        oh = o[:, s]
        outs.append(oh * lax.rsqrt(jnp.mean(oh * oh, axis=-1, keepdims=True) + RMS_EPS) * nw * _silu(z[:, s]))
    return jnp.concatenate(outs, axis=1)


def post_fwd(o2, pm, nw, *, bt=512):
    T = pm.shape[0]
    bt = min(bt, T)

    def body(o_ref, z_ref, nw_ref, og_ref):
        og_ref[...] = _post_rows(o_ref[0], o_ref[1], z_ref[...], nw_ref[0:1, :]).astype(og_ref.dtype)

    return pl.pallas_call(
        body, name="post_fwd", grid=(T // bt,),
        in_specs=[pl.BlockSpec((2, bt, D), lambda i: (0, i, 0)), pl.BlockSpec((bt, D), lambda i: (i, CZ)),
                  pl.BlockSpec((8, 128), lambda i: (0, 0))],
        out_specs=pl.BlockSpec((bt, D), lambda i: (i, 0)), out_shape=_sds((T, D), MM),
        compiler_params=_cp(("parallel",)))(o2, pm, nw)


def post_bwd(o2, pm, nw, dog, dpm, *, bt=512):
    T = pm.shape[0]
    bt = min(bt, T)

    def body(o_ref, z_ref, nw_ref, dog_ref, _alias, do_ref, dz_ref, dnw_ref):
        @pl.when(pl.program_id(0) == 0)
        def _():
            dnw_ref[...] = jnp.zeros_like(dnw_ref)

        _, vjp = jax.vjp(_post_rows, o_ref[0], o_ref[1], z_ref[...], nw_ref[0:1, :])
        doa, _unused, dz, dnw = vjp(dog_ref[...])
        do_ref[...] = doa
        dz_ref[...] = dz
        row8 = lax.broadcasted_iota(jnp.int32, (8, 128), 0)
        dnw_ref[...] += jnp.where(row8 == 0, dnw, 0.0)

    in_specs = [pl.BlockSpec((2, bt, D), lambda i: (0, i, 0)), pl.BlockSpec((bt, D), lambda i: (i, CZ)),
                pl.BlockSpec((8, 128), lambda i: (0, 0)), pl.BlockSpec((bt, D), lambda i: (i, 0)),
                pl.BlockSpec(memory_space=pl.ANY)]
    return pl.pallas_call(
        body, name="post_bwd", grid=(T // bt,), in_specs=in_specs,
        out_specs=[pl.BlockSpec((bt, D), lambda i: (i, 0)), pl.BlockSpec((bt, D), lambda i: (i, CZ)),
                   pl.BlockSpec((8, 128), lambda i: (0, 0))],
        out_shape=[_sds((T, D)), _sds(dpm.shape), _sds((8, 128))], input_output_aliases={4: 1},
        compiler_params=_cp(("arbitrary",)))(o2, pm, nw, dog, dpm)


def merge_fwd(ya, yb, pm, *, bt=512):
    T = pm.shape[0]
    bt = min(bt, T)

    def body(ya_ref, yb_ref, ga_ref, gb_ref, o_ref):
        o_ref[...] = (_sigmoid(ga_ref[...]) * ya_ref[...] + _sigmoid(gb_ref[...]) * yb_ref[...]).astype(o_ref.dtype)

    row = pl.BlockSpec((bt, D), lambda i: (i, 0))
    return pl.pallas_call(
        body, name="merge_fwd", grid=(T // bt,),
        in_specs=[row, row, pl.BlockSpec((bt, D), lambda i: (i, CGA)), pl.BlockSpec((bt, D), lambda i: (i, CGB))],
        out_specs=row, out_shape=_sds((T, D), MM), compiler_params=_cp(("parallel",)))(ya, yb, pm, pm)


def merge_bwd(ya, yb, pm, dmix, *, bt=512):
    T = pm.shape[0]
    bt = min(bt, T)

    def body(ya_ref, yb_ref, ga_ref, gb_ref, dm_ref, dya_ref, dyb_ref, dg_ref):
        dm = dm_ref[...]
        sa, sb = _sigmoid(ga_ref[...]), _sigmoid(gb_ref[...])
        dya_ref[...] = (dm * sa).astype(dya_ref.dtype)
        dyb_ref[...] = (dm * sb).astype(dyb_ref.dtype)
        dg_ref[:, 0:D] = dm * ya_ref[...] * sa * (1.0 - sa)
        dg_ref[:, D:2 * D] = dm * yb_ref[...] * sb * (1.0 - sb)

    row = pl.BlockSpec((bt, D), lambda i: (i, 0))
    return pl.pallas_call(
        body, name="merge_bwd", grid=(T // bt,),
        in_specs=[row, row, pl.BlockSpec((bt, D), lambda i: (i, CGA)), pl.BlockSpec((bt, D), lambda i: (i, CGB)), row],
        out_specs=[row, row, pl.BlockSpec((bt, 2 * D), lambda i: (i, CGA // 2))],
        out_shape=[_sds((T, D), MM), _sds((T, D), MM), _sds((T, NMAIN))],
        compiler_params=_cp(("parallel",)))(ya, yb, pm, pm, dmix)


def _ln_rows(x, y, bias, g, b):
    r = ALPHA * x + y + bias
    mu = jnp.mean(r, axis=-1, keepdims=True)
    var = jnp.mean(jnp.square(r - mu), axis=-1, keepdims=True)
    return (r - mu) * lax.rsqrt(var + LN_EPS) * g + b


def ln_fwd(x, y, p, *, name, bt=512):
    T = x.shape[0]
    bt = min(bt, T)

    def body(x_ref, y_ref, p_ref, o_ref, ob_ref):
        r = _ln_rows(x_ref[...], y_ref[...], p_ref[0:1, :], p_ref[1:2, :], p_ref[2:3, :])
        o_ref[...] = r
        ob_ref[...] = r.astype(ob_ref.dtype)

    row = pl.BlockSpec((bt, D), lambda i: (i, 0))
    return pl.pallas_call(
        body, name=name, grid=(T // bt,), in_specs=[row, row, pl.BlockSpec((8, D), lambda i: (0, 0))],
        out_specs=[row, row], out_shape=[_sds((T, D)), _sds((T, D), MM)],
        compiler_params=_cp(("parallel",)))(x, y, p)


def ln_bwd(x, y, p, ct, ct2=None, *, name, bt=512):
    T = x.shape[0]
    bt = min(bt, T)

    def body(*refs):
        it = iter(refs)
        x_ref, y_ref, p_ref, c_ref = next(it), next(it), next(it), next(it)
        c2_ref = next(it) if ct2 is not None else None
        dxa_ref, dr_ref, dp_ref = next(it), next(it), next(it)

        @pl.when(pl.program_id(0) == 0)
        def _():
            dp_ref[...] = jnp.zeros_like(dp_ref)

        c = c_ref[...]
        if c2_ref is not None:
            c = c + c2_ref[...]
        _, vjp = jax.vjp(_ln_rows, x_ref[...], y_ref[...], p_ref[0:1, :], p_ref[1:2, :], p_ref[2:3, :])
        _dx, dy, dbias, dg, db = vjp(c)
        dxa_ref[...] = ALPHA * dy
        dr_ref[...] = dy.astype(dr_ref.dtype)
        row8 = lax.broadcasted_iota(jnp.int32, (8, D), 0)
        dp_ref[...] += jnp.where(row8 == 0, dbias, jnp.where(row8 == 1, dg, jnp.where(row8 == 2, db, 0.0)))

    row = pl.BlockSpec((bt, D), lambda i: (i, 0))
    in_specs = [row, row, pl.BlockSpec((8, D), lambda i: (0, 0)), row] + ([row] if ct2 is not None else [])
    args = [x, y, p, ct] + ([ct2] if ct2 is not None else [])
    return pl.pallas_call(
        body, name=name, grid=(T // bt,), in_specs=in_specs,
        out_specs=[row, row, pl.BlockSpec((8, D), lambda i: (0, 0))],
        out_shape=[_sds((T, D)), _sds((T, D), MM), _sds((8, D))],
        compiler_params=_cp(("arbitrary",)))(*args)


def loss_fwd_bwd(xl, target, *, bt=512):
    T = xl.shape[0]
    bt = min(bt, T)

    def body(x_ref, t_ref, l_ref, d_ref):
        @pl.when(pl.program_id(0) == 0)
        def _():
            l_ref[...] = jnp.zeros_like(l_ref)

        e = x_ref[...] - t_ref[...]
        d_ref[...] = e * (1.0 / D)
        l_ref[...] += 0.5 * jnp.sum(jnp.mean(e * e, axis=-1, keepdims=True), axis=0, keepdims=True)

    row = pl.BlockSpec((bt, D), lambda i: (i, 0))
    return pl.pallas_call(
        body, name="loss", grid=(T // bt,), in_specs=[row, row],
        out_specs=[pl.BlockSpec((8, 128), lambda i: (0, 0)), row], out_shape=[_sds((8, 128)), _sds((T, D))],
        compiler_params=_cp(("arbitrary",)))(xl, target)


def _row_tile(R, Cc, elems=1 << 18):
    if R * Cc <= elems:
        return R
    tr = 8
    while tr * 2 * Cc <= elems and R % (tr * 2) == 0:
        tr *= 2
    return tr


def adam(w, m, v, ga, gb=None, *, name):
    R, Cc = w.shape
    tr = _row_tile(R, Cc)

    def body(*refs):
        it = iter(refs)
        w_ref, m_ref, v_ref, a_ref = next(it), next(it), next(it), next(it)
        b_ref = next(it) if gb is not None else None
        g_ref, d_ref, mo_ref, vo_ref = next(it), next(it), next(it), next(it)
        g = a_ref[...]
        if b_ref is not None:
            g = g + b_ref[...]
        m2 = B1 * m_ref[...] + (1.0 - B1) * g
        v2 = B2 * v_ref[...] + (1.0 - B2) * jnp.square(g)
        m_hat = m2 / (1.0 - B1 ** STEP)
        v_hat = v2 / (1.0 - B2 ** STEP)
        g_ref[...] = g
        d_ref[...] = -LR * (m_hat / (jnp.sqrt(v_hat) + EPS) + WD * w_ref[...])
        mo_ref[...] = m2
        vo_ref[...] = v2

    blk = pl.BlockSpec((tr, Cc), lambda i: (i, 0))
    args = [w, m, v, ga] + ([gb] if gb is not None else [])
    return pl.pallas_call(
        body, name=name, grid=(R // tr,), in_specs=[blk] * len(args), out_specs=[blk] * 4,
        out_shape=[_sds((R, Cc))] * 4, compiler_params=_cp(("parallel",)))(*args)


def sum4(own, recv, *, name):
    R, Cc = own.shape
    tr = _row_tile(R, Cc)

    def body(o_ref, r_ref, out_ref):
        out_ref[...] = ((o_ref[...] + r_ref[0]) + r_ref[1]) + r_ref[2]

    return pl.pallas_call(
        body, name=name, grid=(R // tr,),
        in_specs=[pl.BlockSpec((tr, Cc), lambda i: (i, 0)), pl.BlockSpec((3, tr, Cc), lambda i: (0, i, 0))],
        out_specs=pl.BlockSpec((tr, Cc), lambda i: (i, 0)), out_shape=_sds((R, Cc)),
        compiler_params=_cp(("parallel",)))(own, recv)


def _place():
    return lax.axis_index("x"), lax.axis_index("y"), lax.axis_index("c")


def _other_chips(x, y):
    return [(1 - x, y), (x, 1 - y), (1 - x, 1 - y)]


_ANY = pl.BlockSpec(memory_space=pl.ANY)


def allgather_xy(arrs):
    n = len(arrs)

    def body(*refs):
        ins, outs = refs[:n], refs[n:2 * n]
        send, recv, loc = refs[2 * n:]
        x, y, c = _place()
        me = 2 * x + y
        peers = _other_chips(x, y)
        local = [pltpu.make_async_copy(ins[a], outs[a].at[me], loc.at[a]) for a in range(n)]
        for cp in local:
            cp.start()

        def remote(a, j, block):
            px, py = peers[j]
            return pltpu.make_async_remote_copy(
                src_ref=ins[a], dst_ref=outs[a].at[block], send_sem=send.at[3 * a + j], recv_sem=recv.at[3 * a + j],
                device_id=(px, py, c), device_id_type=MESH)

        sends = [remote(a, j, me) for a in range(n) for j in range(3)]
        for cp in sends:
            cp.start()
        for a in range(n):
            for j, (px, py) in enumerate(peers):
                remote(a, j, 2 * px + py).wait_recv()
        for cp in sends:
            cp.wait_send()
        for cp in local:
            cp.wait()

    return pl.pallas_call(
        body, name="allgather_xy", in_specs=[_ANY] * n, out_specs=[_ANY] * n,
        out_shape=[_sds((4,) + a.shape, a.dtype) for a in arrs],
        scratch_shapes=[pltpu.SemaphoreType.DMA((3 * n,)), pltpu.SemaphoreType.DMA((3 * n,)),
                        pltpu.SemaphoreType.DMA((n,))],
        compiler_params=pltpu.CompilerParams(has_side_effects=True))(*arrs)


def scatter_xy(groups):
    L = len(groups[0])
    n = len(groups) * L
    flat = [g for grp in groups for g in grp]

    def body(*refs):
        ins = refs[:n]
        owns, recvs = refs[n:n + len(groups)], refs[n + len(groups):n + 2 * len(groups)]
        send, recv, loc = refs[n + 2 * len(groups):]
        x, y, c = _place()
        me = 2 * x + y
        peers = _other_chips(x, y)
        local, sends = [], []
        for a in range(len(groups)):
            for l in range(L):
                i = a * L + l
                local.append(pltpu.make_async_copy(ins[i].at[me], owns[a].at[l], loc.at[i]))
                for j, (px, py) in enumerate(peers):
                    sends.append(pltpu.make_async_remote_copy(
                        src_ref=ins[i].at[2 * px + py], dst_ref=recvs[a].at[j, l], send_sem=send.at[3 * i + j],
                        recv_sem=recv.at[3 * i + j], device_id=(px, py, c), device_id_type=MESH))
        for cp in local + sends:
            cp.start()
        for cp in sends:
            cp.wait_recv()
        for cp in sends:
            cp.wait_send()
        for cp in local:
            cp.wait()

    out_shape = ([_sds((L,) + grp[0].shape[1:]) for grp in groups]
                 + [_sds((3, L) + grp[0].shape[1:]) for grp in groups])
    outs = pl.pallas_call(
        body, name="scatter_xy", in_specs=[_ANY] * n, out_specs=[_ANY] * (2 * len(groups)), out_shape=out_shape,
        scratch_shapes=[pltpu.SemaphoreType.DMA((3 * n,)), pltpu.SemaphoreType.DMA((3 * n,)),
                        pltpu.SemaphoreType.DMA((n,))],
        compiler_params=pltpu.CompilerParams(has_side_effects=True))(*flat)
    return outs[:len(groups)], outs[len(groups):]


def swap_c(arrs):
    n = len(arrs)

    def body(*refs):
        ins, outs = refs[:n], refs[n:2 * n]
        send, recv = refs[2 * n:]
        x, y, c = _place()
        cps = [pltpu.make_async_remote_copy(src_ref=ins[a], dst_ref=outs[a], send_sem=send.at[a], recv_sem=recv.at[a],
                                            device_id=(x, y, 1 - c), device_id_type=MESH) for a in range(n)]
        for cp in cps:
            cp.start()
        for cp in cps:
            cp.wait_recv()
        for cp in cps:
            cp.wait_send()

    return pl.pallas_call(
        body, name="swap_c", in_specs=[_ANY] * n, out_specs=[_ANY] * n, out_shape=[_sds(a.shape, a.dtype) for a in arrs],
        scratch_shapes=[pltpu.SemaphoreType.DMA((n,)), pltpu.SemaphoreType.DMA((n,))],
        compiler_params=pltpu.CompilerParams(has_side_effects=True))(*arrs)


def allreduce_small(v):
    R = v.shape[0]

    def body(v_ref, o_ref, buf, send, recv):
        x, y, c = _place()
        me = 4 * x + 2 * y + c
        buf[0] = v_ref[...]

        def cp(k):
            dx, dy, dc = (k >> 2) & 1, (k >> 1) & 1, k & 1
            return pltpu.make_async_remote_copy(
                src_ref=v_ref, dst_ref=buf.at[k], send_sem=send.at[k - 1], recv_sem=recv.at[k - 1],
                device_id=(x ^ dx, y ^ dy, c ^ dc), device_id_type=MESH)

        cps = [cp(k) for k in range(1, 8)]
        for t in cps:
            t.start()
        for t in cps:
            t.wait_recv()
        acc = buf[me]
        for dev in range(1, 8):
            acc = acc + buf[jnp.bitwise_xor(me, dev)]
        o_ref[...] = acc
        for t in cps:
            t.wait_send()

    vm = pl.BlockSpec(memory_space=pltpu.VMEM)
    return pl.pallas_call(
        body, name="allreduce_small", in_specs=[vm], out_specs=vm, out_shape=_sds((R, 128)),
        scratch_shapes=[pltpu.VMEM((8, R, 128), F32), pltpu.SemaphoreType.DMA((7,)), pltpu.SemaphoreType.DMA((7,))],
        compiler_params=pltpu.CompilerParams(has_side_effects=True, vmem_limit_bytes=VMEM_LIMIT))(v)


def _rows8(*rows):
    n = rows[0].shape[-1]
    t = jnp.stack([r.reshape(n).astype(F32) for r in rows])
    return jnp.pad(t, ((0, 8 - len(rows)), (0, 0)))


def _lanes128(a):
    f = a.reshape(-1).astype(F32)
    return jnp.pad(f, (0, 128 - f.shape[0]))


def _layer_fwd(x, xb, W):
    pm = mm_nn(xb, W["w_main"], name="proj_main")
    pab = mm_nn(xb, W["w_ab"], name="proj_ab")
    q, k, v, gb = pre_qkv_fwd(pm, pab, W["cw"], W["gp"])
    sc = pre_sc_fwd(pm, W["csc"])
    u, w, qd, kd, pp, ys, eg = gdn_intra_fwd(q, k, v, gb)
    o2, s0 = gdn_scan_fwd(u, w, qd, kd, pp, eg)
    og = post_fwd(o2, pm, W["nw"])
    ya = mm_nn(og, W["w_og"], name="proj_og")
    yb = mm_nn(sc, W["w_osc"], name="proj_osc")
    mixed = merge_fwd(ya, yb, pm)
    out = mm_nn(mixed, W["w_out"], name="proj_out")
    x1, x1b = ln_fwd(x, out, W["ln1"], name="ln1_fwd")
    hpre, h = mm_nn(x1b, W["w_up"], bias=W["b_up"], relu2=True, name="mlp_up")
    dn = mm_nn(h, W["w_down"], name="mlp_down")
    x2, x2b = ln_fwd(x1, dn, W["ln2"], name="ln2_fwd")
    saved = dict(x=x, xb=xb, pm=pm, pab=pab, q=q, k=k, v=v, gb=gb, sc=sc, o2=o2, s0=s0, og=og, ya=ya, yb=yb,
                 u=u, w=w, qd=qd, kd=kd, pp=pp, ys=ys, eg=eg,
                 mixed=mixed, out=out, x1=x1, x1b=x1b, hpre=hpre, h=h, dn=dn)
    return x2, x2b, saved


def _layer_bwd(ct, W, S):
    dxa2, dr2b, dp2 = ln_bwd(S["x1"], S["dn"], W["ln2"], ct, name="ln2_bwd")
    g_down = mm_tn(S["h"], dr2b, name="dw_down")
    dhpre, db_up = mm_nt(dr2b, W["w_down"], dact=S["hpre"], out_dtype=MM, name="mlp_down_bwd")
    g_up = mm_tn(S["x1b"], dhpre, name="dw_up")
    dx1 = mm_nt(dhpre, W["w_up"], add=dxa2, name="mlp_up_bwd")
    dxa1, dr1b, dp1 = ln_bwd(S["x"], S["out"], W["ln1"], dx1, name="ln1_bwd")
    g_out = mm_tn(S["mixed"], dr1b, name="dw_out")
    dmix = mm_nt(dr1b, W["w_out"], name="proj_out_bwd")
    dya, dyb, dpm = merge_bwd(S["ya"], S["yb"], S["pm"], dmix)
    g_og = mm_tn(S["og"], dya, name="dw_og")
    g_osc = mm_tn(S["sc"], dyb, name="dw_osc")
    dog = mm_nt(dya, W["w_og"], name="proj_og_bwd")
    dsc = mm_nt(dyb, W["w_osc"], name="proj_osc_bwd")
    do, dpm, dnw = post_bwd(S["o2"], S["pm"], W["nw"], dog, dpm)
    du, dw, dqd, dkd, dpp, deg = gdn_scan_bwd(S["u"], S["w"], S["qd"], S["kd"], S["pp"], S["eg"], S["s0"], do)
    dq2, dk2, dv2, dgb2 = gdn_intra_bwd(S["q"], S["k"], S["v"], S["gb"], S["ys"], du, dw, dqd, dkd, dpp, deg)
    dpm, dpab, dcw, dgp = pre_qkv_bwd(S["pm"], S["pab"], W["cw"], W["gp"], dq2, dk2, dv2, dgb2, dpm)
    dpm, dcsc = pre_sc_bwd(S["pm"], W["csc"], dsc, dpm)
    g_main = mm_tn(S["xb"], dpm, name="dw_main")
    g_ab = mm_tn(S["xb"], dpab, name="dw_ab")
    t = mm_nt(dpab, W["w_ab"], add=dxa1, name="proj_ab_bwd")
    dx = mm_nt(dpm, W["w_main"], add=t, name="proj_main_bwd")
    g_in = jnp.concatenate([g_main[:, :3 * D], g_main[:, 8 * D:], g_ab[:, :4 * NH], g_main[:, 3 * D:8 * D]], axis=1)
    grads = dict(
        w_in=g_in, w_o_gdn=g_og, w_o_sc=g_osc, w_out=g_out, w_up=g_up, w_down=g_down,
        conv_qkv=dcw[:3], conv_sc=dcsc[:3], a_log=dgp[0, :2 * NH].reshape(2, NH), dt_bias=dgp[1, :2 * NH].reshape(2, NH),
        gdn_norm_w=dnw[0], ln1_g=dp1[1], ln1_b=dp1[2], b_up=db_up[0], b_down=dp2[0], ln2_g=dp2[1], ln2_b=dp2[2])
    return dx, grads


def _layer_weights(l, full, a_log, dt_bias, gdn_norm_w, ln1_g, ln1_b, b_up, b_down, ln2_g, ln2_b):
    w_in = full["w_in"][l]
    w_main = jnp.concatenate([w_in[:, :3 * D], w_in[:, 4 * D + 4 * NH:], w_in[:, 3 * D:4 * D]], axis=1)
    w_ab = jnp.pad(w_in[:, 4 * D:4 * D + 4 * NH], ((0, 0), (0, 128 - 4 * NH)))
    return dict(
        w_main=w_main, w_ab=w_ab, w_og=full["w_o_gdn"][l], w_osc=full["w_o_sc"][l], w_out=full["w_out"][l],
        w_up=full["w_up"][l], w_down=full["w_down"][l],
        cw=jnp.pad(full["conv_qkv"][l].astype(F32), ((0, 5), (0, 0))),
        csc=jnp.pad(full["conv_sc"][l].astype(F32), ((0, 5), (0, 0))),
        gp=_rows8(_lanes128(a_log[l]), _lanes128(dt_bias[l])), nw=_rows8(gdn_norm_w[l]),
        ln1=_rows8(jnp.zeros((D,), F32), ln1_g[l], ln1_b[l]), ln2=_rows8(b_down[l], ln2_g[l], ln2_b[l]),
        b_up=b_up[l].reshape(1, DFF).astype(F32))


def local_step(xs, target, full, a_log, dt_bias, gdn_norm_w, ln1_g, ln1_b, b_up, b_down, ln2_g, ln2_b):
    Ws = [_layer_weights(l, full, a_log, dt_bias, gdn_norm_w, ln1_g, ln1_b, b_up, b_down, ln2_g, ln2_b)
          for l in range(DEPTH)]
    x, xb = xs, xs.astype(MM)
    saved = []
    for l in range(DEPTH):
        x, xb, S = _layer_fwd(x, xb, Ws[l])
        saved.append(S)
    loss_tile, ct = loss_fwd_bwd(x, target)
    grads = [None] * DEPTH
    for l in reversed(range(DEPTH)):
        ct, grads[l] = _layer_bwd(ct, Ws[l], saved[l])
    return loss_tile, ct, grads


BIG = ("w_in", "w_o_gdn", "w_o_sc", "w_out", "w_up", "w_down")
SMALL = ("conv_qkv", "a_log", "dt_bias", "gdn_norm_w", "conv_sc", "ln1_g", "ln1_b", "b_up", "b_down", "ln2_g", "ln2_b")
ORDER = ("w_in", "conv_qkv", "a_log", "dt_bias", "gdn_norm_w", "w_o_gdn", "conv_sc", "w_o_sc", "w_out", "ln1_g",
         "ln1_b", "w_up", "b_up", "w_down", "b_down", "ln2_g", "ln2_b")


def _pack(arrs):
    flat = jnp.concatenate([a.reshape(-1).astype(F32) for a in arrs])
    n = flat.shape[0]
    rows = -(-n // 1024) * 8
    return jnp.pad(flat, (0, rows * 128 - n)).reshape(rows, 128)


def _unpack(buf, like):
    flat = buf.reshape(-1)
    out, o = [], 0
    for a in like:
        n = 1
        for s in a.shape:
            n *= s
        out.append(flat[o:o + n].reshape(a.shape))
        o += n
    return out


def _gathered(name, g):
    if name in ("w_in", "w_up", "conv_qkv", "conv_sc"):
        t = jnp.moveaxis(g, 0, -2)
        return t.reshape(t.shape[:-2] + (t.shape[-2] * t.shape[-1],))
    t = jnp.moveaxis(g, 0, 1)
    return t.reshape((t.shape[0], t.shape[1] * t.shape[2]) + t.shape[3:])


def _by_chip(name, g):
    if name in ("w_in", "w_up"):
        r, ccols = g.shape
        return jnp.moveaxis(g.reshape(r, 4, ccols // 4), 1, 0)
    return g.reshape((4, g.shape[0] // 4) + g.shape[1:])


def kernel(x, w_in, conv_qkv, a_log, dt_bias, gdn_norm_w, w_o_gdn, conv_sc, w_o_sc, w_out, ln1_g, ln1_b, w_up, b_up, w_down, b_down, ln2_g, ln2_b, loss_target, m_w_in, m_conv_qkv, m_a_log, m_dt_bias, m_gdn_norm_w, m_w_o_gdn, m_conv_sc, m_w_o_sc, m_w_out, m_ln1_g, m_ln1_b, m_w_up, m_b_up, m_w_down, m_b_down, m_ln2_g, m_ln2_b, v_w_in, v_conv_qkv, v_a_log, v_dt_bias, v_gdn_norm_w, v_w_o_gdn, v_conv_sc, v_w_o_sc, v_w_out, v_ln1_g, v_ln1_b, v_w_up, v_b_up, v_w_down, v_b_down, v_ln2_g, v_ln2_b):
    w = dict(w_in=w_in, conv_qkv=conv_qkv, a_log=a_log, dt_bias=dt_bias, gdn_norm_w=gdn_norm_w, w_o_gdn=w_o_gdn,
             conv_sc=conv_sc, w_o_sc=w_o_sc, w_out=w_out, ln1_g=ln1_g, ln1_b=ln1_b, w_up=w_up, b_up=b_up,
             w_down=w_down, b_down=b_down, ln2_g=ln2_g, ln2_b=ln2_b)
    m = dict(w_in=m_w_in, conv_qkv=m_conv_qkv, a_log=m_a_log, dt_bias=m_dt_bias, gdn_norm_w=m_gdn_norm_w,
             w_o_gdn=m_w_o_gdn, conv_sc=m_conv_sc, w_o_sc=m_w_o_sc, w_out=m_w_out, ln1_g=m_ln1_g, ln1_b=m_ln1_b,
             w_up=m_w_up, b_up=m_b_up, w_down=m_w_down, b_down=m_b_down, ln2_g=m_ln2_g, ln2_b=m_ln2_b)
    v = dict(w_in=v_w_in, conv_qkv=v_conv_qkv, a_log=v_a_log, dt_bias=v_dt_bias, gdn_norm_w=v_gdn_norm_w,
             w_o_gdn=v_w_o_gdn, conv_sc=v_conv_sc, w_o_sc=v_w_o_sc, w_out=v_w_out, ln1_g=v_ln1_g, ln1_b=v_ln1_b,
             w_up=v_w_up, b_up=v_b_up, w_down=v_w_down, b_down=v_b_down, ln2_g=v_ln2_g, ln2_b=v_ln2_b)
    chip = 2 * lax.axis_index("x") + lax.axis_index("y")

    names = BIG + ("conv_qkv", "conv_sc")
    got = allgather_xy([w[n].astype(MM) if n in BIG else w[n] for n in names])
    full = {n: _gathered(n, g) for n, g in zip(names, got)}

    loss_tile, dx, grads = local_step(x[0], loss_target[0], full, a_log, dt_bias, gdn_norm_w, ln1_g, ln1_b, b_up,
                                      b_down, ln2_g, ln2_b)
    loss = lax.psum(loss_tile[0, 0], ("x", "y", "c"))

    groups = [[_by_chip(n, grads[l][n]) for l in range(DEPTH)] for n in BIG]
    owns, recvs = scatter_xy(groups)
    part = []
    for n, own, rec in zip(BIG, owns, recvs):
        cols = own.shape[-1]
        part.append(sum4(own.reshape(-1, cols), rec.reshape(3, -1, cols), name="sum_" + n))
    other = swap_c(part)
    out = {}
    for n, mine, theirs in zip(BIG, part, other):
        cols = mine.shape[-1]
        res = adam(w[n].reshape(-1, cols), m[n].reshape(-1, cols), v[n].reshape(-1, cols), mine, theirs, name="adam_" + n)
        out[n] = [r.reshape(w[n].shape) for r in res]

    stacked = [jnp.stack([grads[l][n] for l in range(DEPTH)]) for n in SMALL]
    summed = _unpack(allreduce_small(_pack(stacked)), stacked)
    gs = []
    for n, g in zip(SMALL, summed):
        if n in ("conv_qkv", "conv_sc"):
            blk = w[n].shape[-1]
            g = lax.dynamic_slice_in_dim(g, chip * blk, blk, axis=2)
        gs.append(g)
    res = adam(_pack([w[n] for n in SMALL]), _pack([m[n] for n in SMALL]), _pack([v[n] for n in SMALL]), _pack(gs),
               name="adam_small")
    for n, parts in zip(SMALL, zip(*[_unpack(r, gs) for r in res])):
        out[n] = list(parts)

    outs = [loss, dx[None]]
    for kind in range(4):
        outs += [out[n][kind] for n in ORDER]
    return tuple(outs)
```

```python
import functools

import jax
import jax.numpy as jnp
from jax import lax
from jax.experimental import pallas as pl
from jax.experimental.pallas import tpu as pltpu

F32 = jnp.float32
MM = jnp.bfloat16
HI = lax.Precision.HIGHEST

D = 1024
NH = 8
HD = 128
CH = 64
DFF = 4 * D
DEPTH = 4
LN_EPS = 1e-5
RMS_EPS = 1e-6
L2_EPS = 1e-6
ALPHA = (2 * DEPTH) ** 0.25
LR, B1, B2, EPS, WD, STEP = 0.001, 0.9, 0.999, 1e-08, 0.01, 10

NMAIN = 9 * D
CQ, CK, CV, CSB, CSC, CSX, CGA, CGB, CZ = range(9)
HALO = 8
VMEM_LIMIT = 56 * 1024 * 1024
MESH = pl.DeviceIdType.MESH


def _cp(sem=None, vmem=VMEM_LIMIT):
    return pltpu.CompilerParams(dimension_semantics=sem, vmem_limit_bytes=vmem)


def _sds(shape, dtype=F32):
    return jax.ShapeDtypeStruct(tuple(shape), dtype)


def mm_nn(a, b, *, name, bias=None, relu2=False, add=None, out_dtype=F32, tm=1024, tn=1024, tk=1024):
    M, K = a.shape
    N = b.shape[1]
    tm, tn, tk = min(tm, M), min(tn, N), min(tk, K)
    nk = K // tk

    def body(*refs):
        it = iter(refs)
        a_ref, b_ref = next(it), next(it)
        bias_ref = next(it) if bias is not None else None
        add_ref = next(it) if add is not None else None
        o_ref = next(it)
        h_ref = next(it) if relu2 else None
        acc = next(it)
        k = pl.program_id(2)

        @pl.when(k == 0)
        def _():
            acc[...] = jnp.zeros_like(acc)

        acc[...] += jnp.dot(a_ref[...].astype(MM), b_ref[...].astype(MM), preferred_element_type=F32)

        @pl.when(k == nk - 1)
        def _():
            r = acc[...]
            if bias_ref is not None:
                r = r + bias_ref[...]
            if add_ref is not None:
                r = r + add_ref[...]
            o_ref[...] = r.astype(o_ref.dtype)
            if relu2:
                t = jnp.maximum(r, 0.0)
                h_ref[...] = (t * t).astype(h_ref.dtype)

    in_specs = [pl.BlockSpec((tm, tk), lambda i, j, k: (i, k)), pl.BlockSpec((tk, tn), lambda i, j, k: (k, j))]
    args = [a, b]
    if bias is not None:
        in_specs.append(pl.BlockSpec((1, tn), lambda i, j, k: (0, j)))
        args.append(bias)
    if add is not None:
        in_specs.append(pl.BlockSpec((tm, tn), lambda i, j, k: (i, j)))
        args.append(add)
    out_shape = [_sds((M, N), out_dtype)]
    out_specs = [pl.BlockSpec((tm, tn), lambda i, j, k: (i, j))]
    if relu2:
        out_shape.append(_sds((M, N), MM))
        out_specs.append(pl.BlockSpec((tm, tn), lambda i, j, k: (i, j)))
    res = pl.pallas_call(
        body, name=name, grid=(M // tm, N // tn, nk), in_specs=in_specs, out_specs=out_specs, out_shape=out_shape,
        scratch_shapes=[pltpu.VMEM((tm, tn), F32)],
        compiler_params=_cp(("parallel", "parallel", "arbitrary")))(*args)
    return res if relu2 else res[0]


def mm_nt(a, b, *, name, add=None, dact=None, out_dtype=F32, tm=1024, tn=1024, tk=1024):
    M, Nc = a.shape
    Ko = b.shape[0]
    tm, tn, tk = min(tm, M), min(tn, Ko), min(tk, Nc)
    nk = Nc // tk
    ni = M // tm

    def body(*refs):
        it = iter(refs)
        a_ref, b_ref = next(it), next(it)
        add_ref = next(it) if add is not None else None
        d_ref = next(it) if dact is not None else None
        o_ref = next(it)
        db_ref = next(it) if dact is not None else None
        acc = next(it)
        i, k = pl.program_id(1), pl.program_id(2)

        @pl.when(k == 0)
        def _():
            acc[...] = jnp.zeros_like(acc)

        acc[...] += lax.dot_general(a_ref[...].astype(MM), b_ref[...].astype(MM), (((1,), (1,)), ((), ())),
                                    preferred_element_type=F32)

        @pl.when(k == nk - 1)
        def _():
            r = acc[...]
            if add_ref is not None:
                r = r + add_ref[...]
            if d_ref is not None:
                r = r * (2.0 * jnp.maximum(d_ref[...], 0.0))
                s = jnp.sum(r, axis=0, keepdims=True)
                row0 = lax.broadcasted_iota(jnp.int32, db_ref.shape, 0) == 0

                @pl.when(i == 0)
                def _():
                    db_ref[...] = jnp.zeros_like(db_ref)

                db_ref[...] += jnp.where(row0, s, 0.0)
            o_ref[...] = r.astype(o_ref.dtype)

    in_specs = [pl.BlockSpec((tm, tk), lambda j, i, k: (i, k)), pl.BlockSpec((tn, tk), lambda j, i, k: (j, k))]
    args = [a, b]
    for extra in (add, dact):
        if extra is not None:
            in_specs.append(pl.BlockSpec((tm, tn), lambda j, i, k: (i, j)))
            args.append(extra)
    out_shape = [_sds((M, Ko), out_dtype)]
    out_specs = [pl.BlockSpec((tm, tn), lambda j, i, k: (i, j))]
    if dact is not None:
        out_shape.append(_sds((8, Ko), F32))
        out_specs.append(pl.BlockSpec((8, tn), lambda j, i, k: (0, j)))
    res = pl.pallas_call(
        body, name=name, grid=(Ko // tn, ni, nk), in_specs=in_specs, out_specs=out_specs, out_shape=out_shape,
        scratch_shapes=[pltpu.VMEM((tm, tn), F32)],
        compiler_params=_cp(("parallel", "arbitrary", "arbitrary")))(*args)
    return res if dact is not None else res[0]


def mm_tn(a, b, *, name, tm=1024, tn=1024, tk=512):
    T, M = a.shape
    N = b.shape[1]
    tm, tn, tk = min(tm, M), min(tn, N), min(tk, T)

    def body(a_ref, b_ref, o_ref):
        @pl.when(pl.program_id(2) == 0)
        def _():
            o_ref[...] = jnp.zeros_like(o_ref)

        o_ref[...] += lax.dot_general(a_ref[...].astype(MM), b_ref[...].astype(MM), (((0,), (0,)), ((), ())),
                                      preferred_element_type=F32)

    return pl.pallas_call(
        body, name=name, grid=(M // tm, N // tn, T // tk),
        in_specs=[pl.BlockSpec((tk, tm), lambda i, j, k: (k, i)), pl.BlockSpec((tk, tn), lambda i, j, k: (k, j))],
        out_specs=pl.BlockSpec((tm, tn), lambda i, j, k: (i, j)), out_shape=_sds((M, N)),
        compiler_params=_cp(("parallel", "parallel", "arbitrary")))(a, b)


def _sigmoid(x):
    return 1.0 / (1.0 + jnp.exp(-x))


def _silu(x):
    return x * _sigmoid(x)


def _softplus(x):
    return jnp.maximum(x, 0.0) + jnp.log1p(jnp.exp(-jnp.abs(x)))


def _ext(main_ref, prev_ref, next_ref, first, last):
    p = jnp.where(first, 0.0, prev_ref[...].astype(F32))
    n = jnp.where(last, 0.0, next_ref[...].astype(F32))
    return jnp.concatenate([p, main_ref[...].astype(F32), n], axis=0)


def _shift_dn(x):
    return pltpu.roll(x, 1, 0)


def _shift_up(x):
    return pltpu.roll(x, x.shape[0] - 1, 0)


def _conv3(xe, w):
    return w[0:1, :] * _shift_dn(xe) + w[1:2, :] * xe + w[2:3, :] * _shift_up(xe)


def _conv3_t(de, w):
    return w[0:1, :] * _shift_up(de) + w[1:2, :] * de + w[2:3, :] * _shift_dn(de)


def _halo_specs(bt, T, col, lead=None):
    r = bt // HALO
    last = T // HALO - 1
    if lead is None:
        return [pl.BlockSpec((bt, D), lambda i: (i, col)),
                pl.BlockSpec((HALO, D), lambda i: (jnp.maximum(i * r - 1, 0), col)),
                pl.BlockSpec((HALO, D), lambda i: (jnp.minimum((i + 1) * r, last), col))]
    return [pl.BlockSpec((lead, bt, D), lambda i: (0, i, col)),
            pl.BlockSpec((lead, HALO, D), lambda i: (0, jnp.maximum(i * r - 1, 0), col)),
            pl.BlockSpec((lead, HALO, D), lambda i: (0, jnp.minimum((i + 1) * r, last), col))]


def _qkv_rows(cq, ck, cv):
    sq, sk, sv = _silu(cq), _silu(ck), _silu(cv)
    qs, ks = [], []
    for h in range(NH):
        s = slice(h * HD, (h + 1) * HD)
        qh, kh = sq[:, s], sk[:, s]
        qs.append(qh * lax.rsqrt(jnp.sum(qh * qh, axis=-1, keepdims=True) + L2_EPS) * (HD ** -0.5))
        ks.append(kh * lax.rsqrt(jnp.sum(kh * kh, axis=-1, keepdims=True) + L2_EPS))
    return jnp.concatenate(qs, axis=1), jnp.concatenate(ks, axis=1), sv


def _chunk_masks(bt):
    row = lax.broadcasted_iota(jnp.int32, (bt, bt), 0)
    col = lax.broadcasted_iota(jnp.int32, (bt, bt), 1)
    same = (row // CH) == (col // CH)
    lower = jnp.where(same & (col <= row), 1.0, 0.0).astype(F32)
    upper = jnp.where(same & (col >= row), 1.0, 0.0).astype(F32)
    return lower, upper


def _gate_rows(ab, gp, lower, upper):
    lane = lax.broadcasted_iota(jnp.int32, ab.shape, 1)
    g = -jnp.exp(gp[0:1, :]) * _softplus(ab + gp[1:2, :])
    g = jnp.where(lane < 2 * NH, g, 0.0)
    gf = jnp.dot(lower, g, precision=HI, preferred_element_type=F32)
    gr = jnp.dot(upper, g, precision=HI, preferred_element_type=F32)
    gc = jnp.where(lane < NH, gf, gr)
    beta = _sigmoid(ab)
    return jnp.where(lane < 2 * NH, gc, jnp.where(lane < 4 * NH, beta, 0.0))


def pre_qkv_fwd(pm, pab, cw, gp, *, bt=256):
    T = pm.shape[0]
    bt = min(bt, T)
    n = T // bt

    def body(q0, q1, q2, k0, k1, k2, v0, v1, v2, ab_ref, cw_ref, gp_ref, q_ref, k_ref, v_ref, gb_ref):
        i = pl.program_id(0)
        first, last = i == 0, i == n - 1
        cs = []
        for c, (m, p, x) in enumerate(((q0, q1, q2), (k0, k1, k2), (v0, v1, v2))):
            xe = _ext(m, p, x, first, last)
            cs.append(_conv3(xe, cw_ref[:, c * D:(c + 1) * D])[HALO:HALO + bt])
        q, k, v = _qkv_rows(*cs)
        q_ref[...], k_ref[...], v_ref[...] = q, k, v
        lower, upper = _chunk_masks(bt)
        gb_ref[...] = _gate_rows(ab_ref[...], gp_ref[...], lower, upper)

    in_specs = (_halo_specs(bt, T, CQ) + _halo_specs(bt, T, CK) + _halo_specs(bt, T, CV)
                + [pl.BlockSpec((bt, 128), lambda i: (i, 0)), pl.BlockSpec((8, 3 * D), lambda i: (0, 0)),
                   pl.BlockSpec((8, 128), lambda i: (0, 0))])
    row = pl.BlockSpec((bt, D), lambda i: (i, 0))
    return pl.pallas_call(
        body, name="pre_qkv_fwd", grid=(n,), in_specs=in_specs,
        out_specs=[row, row, row, pl.BlockSpec((bt, 128), lambda i: (i, 0))],
        out_shape=[_sds((T, D)), _sds((T, D)), _sds((T, D)), _sds((T, 128))],
        compiler_params=_cp(("parallel",)))(*([pm] * 9), pab, cw, gp)


def pre_qkv_bwd(pm, pab, cw, gp, dq2, dk2, dv2, dgb2, dpm, *, bt=128):
    T = pm.shape[0]
    bt = min(bt, T)
    n = T // bt
    E = bt + 2 * HALO

    def body(*refs):
        it = iter(refs)
        xs = [[next(it) for _ in range(3)] for _ in range(3)]
        ds = [[next(it) for _ in range(3)] for _ in range(3)]
        ab_ref, dgb_ref, cw_ref, gp_ref, _alias = next(it), next(it), next(it), next(it), next(it)
        o_ref, dab_ref, dcw_ref, dgp_ref = (next(it) for _ in range(4))
        i = pl.program_id(0)
        first, last = i == 0, i == n - 1

        @pl.when(first)
        def _():
            dcw_ref[...] = jnp.zeros_like(dcw_ref)
            dgp_ref[...] = jnp.zeros_like(dgp_ref)

        xes = [_ext(*xs[c], first, last) for c in range(3)]
        ces = [_conv3(xes[c], cw_ref[:, c * D:(c + 1) * D]) for c in range(3)]
        cts = []
        for c in range(3):
            m, p, x = ds[c]
            pe = jnp.where(first, 0.0, p[0] + p[1])
            ne = jnp.where(last, 0.0, x[0] + x[1])
            cts.append(jnp.concatenate([pe, m[0] + m[1], ne], axis=0))
        _, vjp = jax.vjp(_qkv_rows, *ces)
        dces = vjp(tuple(cts))
        rowi = lax.broadcasted_iota(jnp.int32, (E, 1), 0)
        central = (rowi >= HALO) & (rowi < HALO + bt)
        row8 = lax.broadcasted_iota(jnp.int32, (8, D), 0)
        for c in range(3):
            w = cw_ref[:, c * D:(c + 1) * D]
            o_ref[:, c * D:(c + 1) * D] = _conv3_t(dces[c], w)[HALO:HALO + bt]
            dc = jnp.where(central, dces[c], 0.0)
            taps = (jnp.sum(dc * _shift_dn(xes[c]), axis=0, keepdims=True),
                    jnp.sum(dc * xes[c], axis=0, keepdims=True),
                    jnp.sum(dc * _shift_up(xes[c]), axis=0, keepdims=True))
            upd = jnp.where(row8 == 0, taps[0], jnp.where(row8 == 1, taps[1], jnp.where(row8 == 2, taps[2], 0.0)))
            dcw_ref[:, c * D:(c + 1) * D] += upd
        lower, upper = _chunk_masks(bt)
        _, gvjp = jax.vjp(lambda ab, gp: _gate_rows(ab, gp, lower, upper), ab_ref[...], gp_ref[...])
        dab, dgp = gvjp(dgb_ref[0] + dgb_ref[1])
        dab_ref[...] = dab
        dgp_ref[...] += dgp

    in_specs = (_halo_specs(bt, T, CQ) + _halo_specs(bt, T, CK) + _halo_specs(bt, T, CV)
                + _halo_specs(bt, T, 0, lead=2) * 3
                + [pl.BlockSpec((bt, 128), lambda i: (i, 0)), pl.BlockSpec((2, bt, 128), lambda i: (0, i, 0)),
                   pl.BlockSpec((8, 3 * D), lambda i: (0, 0)), pl.BlockSpec((8, 128), lambda i: (0, 0)),
                   pl.BlockSpec(memory_space=pl.ANY)])
    out_specs = [pl.BlockSpec((bt, 3 * D), lambda i: (i, 0)), pl.BlockSpec((bt, 128), lambda i: (i, 0)),
                 pl.BlockSpec((8, 3 * D), lambda i: (0, 0)), pl.BlockSpec((8, 128), lambda i: (0, 0))]
    return pl.pallas_call(
        body, name="pre_qkv_bwd", grid=(n,), in_specs=in_specs, out_specs=out_specs,
        out_shape=[_sds(dpm.shape), _sds((T, 128)), _sds((8, 3 * D)), _sds((8, 128))],
        input_output_aliases={len(in_specs) - 1: 0},
        compiler_params=_cp(("arbitrary",)))(
            *([pm] * 9), dq2, dq2, dq2, dk2, dk2, dk2, dv2, dv2, dv2, pab, dgb2, cw, gp, dpm)


def pre_sc_fwd(pm, cw, *, bt=256):
    T = pm.shape[0]
    bt = min(bt, T)
    n = T // bt

    def body(b_ref, c0, c1, c2, x0, x1, x2, cw_ref, o_ref):
        i = pl.program_id(0)
        first, last = i == 0, i == n - 1
        pe = _ext(c0, c1, c2, first, last) * _ext(x0, x1, x2, first, last)
        o_ref[...] = (b_ref[...] * _conv3(pe, cw_ref[...])[HALO:HALO + bt]).astype(o_ref.dtype)

    in_specs = ([pl.BlockSpec((bt, D), lambda i: (i, CSB))] + _halo_specs(bt, T, CSC) + _halo_specs(bt, T, CSX)
                + [pl.BlockSpec((8, D), lambda i: (0, 0))])
    return pl.pallas_call(
        body, name="pre_sc_fwd", grid=(n,), in_specs=in_specs, out_specs=pl.BlockSpec((bt, D), lambda i: (i, 0)),
        out_shape=_sds((T, D), MM), compiler_params=_cp(("parallel",)))(*([pm] * 7), cw)


def pre_sc_bwd(pm, cw, dsc, dpm, *, bt=256):
    T = pm.shape[0]
    bt = min(bt, T)
    n = T // bt
    E = bt + 2 * HALO

    def body(b0, b1, b2, c0, c1, c2, x0, x1, x2, d0, d1, d2, cw_ref, _alias, o_ref, dcw_ref):
        i = pl.program_id(0)
        first, last = i == 0, i == n - 1

        @pl.when(first)
        def _():
            dcw_ref[...] = jnp.zeros_like(dcw_ref)

        ce, xe = _ext(c0, c1, c2, first, last), _ext(x0, x1, x2, first, last)
        pe = ce * xe
        w = cw_ref[...]
        dout = d0[...]
        o_ref[:, 0:D] = dout * _conv3(pe, w)[HALO:HALO + bt]
        dce = _ext(d0, d1, d2, first, last) * _ext(b0, b1, b2, first, last)
        dp = _conv3_t(dce, w)[HALO:HALO + bt]
        o_ref[:, D:2 * D] = dp * x0[...]
        o_ref[:, 2 * D:3 * D] = dp * c0[...]
        rowi = lax.broadcasted_iota(jnp.int32, (E, 1), 0)
        dc = jnp.where((rowi >= HALO) & (rowi < HALO + bt), dce, 0.0)
        row8 = lax.broadcasted_iota(jnp.int32, (8, D), 0)
        taps = (jnp.sum(dc * _shift_dn(pe), axis=0, keepdims=True), jnp.sum(dc * pe, axis=0, keepdims=True),
                jnp.sum(dc * _shift_up(pe), axis=0, keepdims=True))
        dcw_ref[...] += jnp.where(row8 == 0, taps[0], jnp.where(row8 == 1, taps[1], jnp.where(row8 == 2, taps[2], 0.0)))

    dsc_specs = [pl.BlockSpec((bt, D), lambda i: (i, 0)),
                 pl.BlockSpec((HALO, D), lambda i: (jnp.maximum(i * (bt // HALO) - 1, 0), 0)),
                 pl.BlockSpec((HALO, D), lambda i: (jnp.minimum((i + 1) * (bt // HALO), T // HALO - 1), 0))]
    in_specs = (_halo_specs(bt, T, CSB) + _halo_specs(bt, T, CSC) + _halo_specs(bt, T, CSX) + dsc_specs
                + [pl.BlockSpec((8, D), lambda i: (0, 0)), pl.BlockSpec(memory_space=pl.ANY)])
    return pl.pallas_call(
        body, name="pre_sc_bwd", grid=(n,), in_specs=in_specs,
        out_specs=[pl.BlockSpec((bt, 3 * D), lambda i: (i, 1)), pl.BlockSpec((8, D), lambda i: (0, 0))],
        out_shape=[_sds(dpm.shape), _sds((8, D))], input_output_aliases={len(in_specs) - 1: 0},
        compiler_params=_cp(("arbitrary",)))(*([pm] * 9), dsc, dsc, dsc, cw, dpm)


def _bdot(a, b, dims):
    return lax.dot_general(a.astype(MM), b.astype(MM), (dims, ((), ())), preferred_element_type=F32)


_NN, _NT, _TN = ((1,), (0,)), ((1,), (1,)), ((0,), (0,))


def _raw_nn(a, b):
    return _bdot(a, b, _NN)


def _raw_nt(a, b):
    return _bdot(a, b, _NT)


def _raw_tn(a, b):
    return _bdot(a, b, _TN)


def _tri_inv_y(A):
    Y = -A
    P = _raw_nn(A, A)
    Y = Y + P + _raw_nn(Y, P)
    for _ in range(4):
        P = _raw_nn(P, P)
        Y = Y + P + _raw_nn(Y, P)
    return Y


def _make_vjp_ops():
    @jax.custom_vjp
    def nn(a, b):
        return _raw_nn(a, b)

    @jax.custom_vjp
    def nt(a, b):
        return _raw_nt(a, b)

    @jax.custom_vjp
    def tn(a, b):
        return _raw_tn(a, b)

    nn.defvjp(lambda a, b: (_raw_nn(a, b), (a, b)), lambda r, g: (_raw_nt(g, r[1]), _raw_tn(r[0], g)))
    nt.defvjp(lambda a, b: (_raw_nt(a, b), (a, b)), lambda r, g: (_raw_nn(g, r[1]), _raw_tn(g, r[0])))
    tn.defvjp(lambda a, b: (_raw_tn(a, b), (a, b)), lambda r, g: (_raw_nt(r[1], g), _raw_nn(r[0], g)))

    @jax.custom_vjp
    def inv_saved(A, Y):
        return Y

    def inv_bwd(Y, g):
        M = g + _raw_tn(Y, g)
        return -(M + _raw_nt(M, Y)), jnp.zeros_like(Y)

    inv_saved.defvjp(lambda A, Y: (Y, Y), inv_bwd)
    return nn, nt, tn, inv_saved


GH = 4
GR = GH * CH
NG = NH // GH


def _intra_group(qs, ks, vs, gcols, bcols, incl, strict, eye, lastc, ops):
    nn, nt, tn, inv = ops
    q, k, v = (jnp.concatenate(t, axis=0) for t in (qs, ks, vs))
    gcol, bcol = jnp.concatenate(gcols, axis=0), jnp.concatenate(bcols, axis=0)
    grow = jnp.sum(eye * gcol, axis=0, keepdims=True)
    gam = jnp.where(incl, jnp.exp(jnp.where(incl, gcol - grow, 0.0)), 0.0)
    glast = jnp.sum(jnp.where(lastc, grow, 0.0), axis=1, keepdims=True)
    eg = jnp.exp(gcol)
    kb = k * bcol
    A = jnp.where(strict, nt(kb, k) * gam, 0.0)
    Y = inv(A)
    vb, kg = v * bcol, kb * eg
    u = vb + nn(Y, vb)
    w = kg + nn(Y, kg)
    P = nt(q, k) * gam
    return (u, w, P, q * eg, k * jnp.exp(glast - gcol), jnp.exp(glast)), Y


def _scan_group(Ss, us, ws, P, qds, kds, egls, ops):
    nn, nt, tn, _ = ops
    vns = [us[j] - nn(ws[j], Ss[j]) for j in range(GH)]
    o = jnp.concatenate([nn(qds[j], Ss[j]) for j in range(GH)], axis=0) + nn(P, jnp.concatenate(vns, axis=0))
    return o, [Ss[j] * egls[j] + tn(kds[j], vns[j]) for j in range(GH)]


def _group_masks(rev):
    r = lax.broadcasted_iota(jnp.int32, (GR, GR), 0)
    c = lax.broadcasted_iota(jnp.int32, (GR, GR), 1)
    same = (r // CH) == (c // CH)
    ahead = jnp.where(rev, c - r, r - c)
    incl = same & (ahead >= 0)
    strict = same & (ahead > 0)
    eye = jnp.where(r == c, 1.0, 0.0).astype(F32)
    lastc = same & ((c % CH) == jnp.where(rev, 0, CH - 1))
    return incl, strict, eye, lastc


def _head_gates(gb, h, rev):
    gcol = jnp.where(rev, gb[:, NH + h:NH + h + 1], gb[:, h:h + 1])
    bcol = jnp.where(rev, gb[:, 3 * NH + h:3 * NH + h + 1], gb[:, 2 * NH + h:2 * NH + h + 1])
    return gcol, bcol


def _hs(h, width=HD):
    return slice(h * HD, h * HD + width)


def gdn_intra_fwd(q, k, v, gb):
    T = q.shape[0]
    N = T // CH
    ops = (_raw_nn, _raw_nt, _raw_tn, _tri_inv_y)

    def body(q_ref, k_ref, v_ref, gb_ref, u_ref, w_ref, qd_ref, kd_ref, pp_ref, ys_ref, eg_ref):
        rev = pl.program_id(0) == 1
        masks = _group_masks(rev)
        gb_t = gb_ref[...]
        for g in range(NG):
            heads = range(g * GH, (g + 1) * GH)
            gates = [_head_gates(gb_t, h, rev) for h in heads]
            (u, w, P, qd, kd, egl), Y = _intra_group(
                [q_ref[:, _hs(h)] for h in heads], [k_ref[:, _hs(h)] for h in heads], [v_ref[:, _hs(h)] for h in heads],
                [t[0] for t in gates], [t[1] for t in gates], *masks, ops)
            pp_ref[g] = P.astype(MM)
            ys_ref[g] = Y.astype(MM)
            for j, h in enumerate(heads):
                rows = slice(j * CH, (j + 1) * CH)
                u_ref[:, _hs(h)] = u[rows]
                w_ref[:, _hs(h)] = w[rows].astype(MM)
                qd_ref[:, _hs(h)] = qd[rows].astype(MM)
                kd_ref[:, _hs(h)] = kd[rows].astype(MM)
                eg_ref[h:h + 1, :] = jnp.broadcast_to(egl[j * CH:j * CH + 1, :], (1, 128))

    row = pl.BlockSpec((CH, D), lambda d, n: (n, 0))
    drow = pl.BlockSpec((None, CH, D), lambda d, n: (d, n, 0))
    mat = pl.BlockSpec((None, None, NG, GR, GR), lambda d, n: (d, n, 0, 0, 0))
    return pl.pallas_call(
        body, name="gdn_intra_fwd", grid=(2, N),
        in_specs=[row, row, row, pl.BlockSpec((CH, 128), lambda d, n: (n, 0))],
        out_specs=[drow] * 4 + [mat, mat, pl.BlockSpec((None, None, NH, 128), lambda d, n: (d, n, 0, 0))],
        out_shape=[_sds((2, T, D))] + [_sds((2, T, D), MM)] * 3 + [_sds((2, N, NG, GR, GR), MM)] * 2
                  + [_sds((2, N, NH, 128))],
        compiler_params=_cp(("parallel", "parallel")))(q, k, v, gb)


def gdn_scan_fwd(u, w, qd, kd, pp, eg):
    T = u.shape[1]
    N = T // CH
    ops = (_raw_nn, _raw_nt, _raw_tn, None)

    def body(u_ref, w_ref, qd_ref, kd_ref, pp_ref, eg_ref, o_ref, s0_ref, S):
        @pl.when(pl.program_id(1) == 0)
        def _():
            S[...] = jnp.zeros_like(S)

        for g in range(NG):
            heads = range(g * GH, (g + 1) * GH)
            Ss = [S[h] for h in heads]
            for h, Sh in zip(heads, Ss):
                s0_ref[h] = Sh
            o, S2 = _scan_group(Ss, [u_ref[:, _hs(h)] for h in heads], [w_ref[:, _hs(h)] for h in heads], pp_ref[g],
                                [qd_ref[:, _hs(h)] for h in heads], [kd_ref[:, _hs(h)] for h in heads],
                                [eg_ref[h:h + 1, :] for h in heads], ops)
            for j, h in enumerate(heads):
                o_ref[:, _hs(h)] = o[j * CH:(j + 1) * CH]
                S[h] = S2[j]

    cidx = lambda d, n: n + d * (N - 1 - 2 * n)
    drow = pl.BlockSpec((None, CH, D), lambda d, n: (d, cidx(d, n), 0))
    mat = pl.BlockSpec((None, None, NG, GR, GR), lambda d, n: (d, cidx(d, n), 0, 0, 0))
    return pl.pallas_call(
        body, name="gdn_scan_fwd", grid=(2, N),
        in_specs=[drow] * 4 + [mat, pl.BlockSpec((None, None, NH, 128), lambda d, n: (d, cidx(d, n), 0, 0))],
        out_specs=[drow, pl.BlockSpec((None, None, NH, HD, HD), lambda d, n: (d, cidx(d, n), 0, 0, 0))],
        out_shape=[_sds((2, T, D)), _sds((2, N, NH, HD, HD))],
        scratch_shapes=[pltpu.VMEM((NH, HD, HD), F32)],
        compiler_params=_cp(("arbitrary", "arbitrary")))(u, w, qd, kd, pp, eg)


def gdn_scan_bwd(u, w, qd, kd, pp, eg, s0, do):
    T = u.shape[1]
    N = T // CH
    ops = _make_vjp_ops()

    def body(u_ref, w_ref, qd_ref, kd_ref, pp_ref, eg_ref, s0_ref, do_ref,
             du_ref, dw_ref, dqd_ref, dkd_ref, dpp_ref, deg_ref, dS):
        @pl.when(pl.program_id(1) == 0)
        def _():
            dS[...] = jnp.zeros_like(dS)

        for g in range(NG):
            heads = range(g * GH, (g + 1) * GH)
            step = lambda *a: _scan_group(*a, ops)
            _, vjp = jax.vjp(
                step, [s0_ref[h] for h in heads], [u_ref[:, _hs(h)] for h in heads],
                [w_ref[:, _hs(h)].astype(F32) for h in heads], pp_ref[g].astype(F32),
                [qd_ref[:, _hs(h)].astype(F32) for h in heads], [kd_ref[:, _hs(h)].astype(F32) for h in heads],
                [eg_ref[h:h + 1, :] for h in heads])
            do = jnp.concatenate([do_ref[:, _hs(h)] for h in heads], axis=0)
            dSs, dus, dws, dP, dqds, dkds, degs = vjp((do, [dS[h] for h in heads]))
            dpp_ref[g] = dP
            for j, h in enumerate(heads):
                dS[h] = dSs[j]
                du_ref[:, _hs(h)] = dus[j].astype(MM)
                dw_ref[:, _hs(h)] = dws[j].astype(MM)
                dqd_ref[:, _hs(h)], dkd_ref[:, _hs(h)] = dqds[j], dkds[j]
                deg_ref[h:h + 1, :] = degs[j]

    cidx = lambda d, n: (N - 1 - n) + d * (2 * n - (N - 1))
    drow = pl.BlockSpec((None, CH, D), lambda d, n: (d, cidx(d, n), 0))
    erow = pl.BlockSpec((None, None, NH, 128), lambda d, n: (d, cidx(d, n), 0, 0))
    mat = pl.BlockSpec((None, None, NG, GR, GR), lambda d, n: (d, cidx(d, n), 0, 0, 0))
    return pl.pallas_call(
        body, name="gdn_scan_bwd", grid=(2, N),
        in_specs=[drow] * 4 + [mat, erow, pl.BlockSpec((None, None, NH, HD, HD), lambda d, n: (d, cidx(d, n), 0, 0, 0)),
                               pl.BlockSpec((CH, D), lambda d, n: (cidx(d, n), 0))],
        out_specs=[drow] * 4 + [mat, erow],
        out_shape=[_sds((2, T, D), MM)] * 2 + [_sds((2, T, D))] * 2 + [_sds((2, N, NG, GR, GR))]
                  + [_sds((2, N, NH, 128))],
        scratch_shapes=[pltpu.VMEM((NH, HD, HD), F32)],
        compiler_params=_cp(("arbitrary", "arbitrary")))(u, w, qd, kd, pp, eg, s0, do)


def gdn_intra_bwd(q, k, v, gb, ys, du, dw, dqd, dkd, dpp, deg):
    T = q.shape[0]
    N = T // CH
    nn, nt, tn, inv_saved = _make_vjp_ops()

    def body(q_ref, k_ref, v_ref, gb_ref, ys_ref, du_ref, dw_ref, dqd_ref, dkd_ref, dpp_ref, deg_ref,
             dq_ref, dk_ref, dv_ref, dgb_ref):
        rev = pl.program_id(0) == 1
        masks = _group_masks(rev)
        gb_t = gb_ref[...]
        lane = lax.broadcasted_iota(jnp.int32, (CH, 128), 1)
        grow = lax.broadcasted_iota(jnp.int32, (GR, 1), 0)
        dgb = jnp.zeros((CH, 128), F32)
        for g in range(NG):
            heads = range(g * GH, (g + 1) * GH)
            gates = [_head_gates(gb_t, h, rev) for h in heads]
            Y = ys_ref[g].astype(F32)
            f = lambda *a: _intra_group(*a, *masks, (nn, nt, tn, lambda A: inv_saved(A, Y)))
            _, vjp = jax.vjp(f, [q_ref[:, _hs(h)] for h in heads], [k_ref[:, _hs(h)] for h in heads],
                             [v_ref[:, _hs(h)] for h in heads], [t[0] for t in gates], [t[1] for t in gates])
            stack = lambda ref: jnp.concatenate([ref[:, _hs(h)].astype(F32) for h in heads], axis=0)
            degl = jnp.zeros((GR, 1), F32)
            for j, h in enumerate(heads):
                degl = degl + jnp.where(grow == j * CH, jnp.sum(deg_ref[h:h + 1, :], axis=1, keepdims=True), 0.0)
            cts = (stack(du_ref), stack(dw_ref), dpp_ref[g], stack(dqd_ref), stack(dkd_ref), degl)
            dqs, dks, dvs, dgs, dbs = vjp((cts, jnp.zeros((GR, GR), F32)))
            for j, h in enumerate(heads):
                dq_ref[:, _hs(h)], dk_ref[:, _hs(h)], dv_ref[:, _hs(h)] = dqs[j], dks[j], dvs[j]
                glane = jnp.where(rev, NH + h, h)
                dgb = dgb + jnp.where(lane == glane, dgs[j], 0.0) + jnp.where(lane == glane + 2 * NH, dbs[j], 0.0)
        dgb_ref[...] = dgb

    row = pl.BlockSpec((CH, D), lambda d, n: (n, 0))
    drow = pl.BlockSpec((None, CH, D), lambda d, n: (d, n, 0))
    mat = pl.BlockSpec((None, None, NG, GR, GR), lambda d, n: (d, n, 0, 0, 0))
    return pl.pallas_call(
        body, name="gdn_intra_bwd", grid=(2, N),
        in_specs=[row, row, row, pl.BlockSpec((CH, 128), lambda d, n: (n, 0)), mat, drow, drow, drow, drow, mat]
                 + [pl.BlockSpec((None, None, NH, 128), lambda d, n: (d, n, 0, 0))],
        out_specs=[drow, drow, drow, pl.BlockSpec((None, CH, 128), lambda d, n: (d, n, 0))],
        out_shape=[_sds((2, T, D))] * 3 + [_sds((2, T, 128))],
        compiler_params=_cp(("parallel", "parallel")))(q, k, v, gb, ys, du, dw, dqd, dkd, dpp, deg)


def _post_rows(o2a, o2b, z, nw):
    o = o2a + o2b
    outs = []
    for h in range(NH):
        s = slice(h * HD, (h + 1) * HD)
        oh = o[:, s]
        outs.append(oh * lax.rsqrt(jnp.mean(oh * oh, axis=-1, keepdims=True) + RMS_EPS) * nw * _silu(z[:, s]))
    return jnp.concatenate(outs, axis=1)


def post_fwd(o2, pm, nw, *, bt=512):
    T = pm.shape[0]
    bt = min(bt, T)

    def body(o_ref, z_ref, nw_ref, og_ref):
        og_ref[...] = _post_rows(o_ref[0], o_ref[1], z_ref[...], nw_ref[0:1, :]).astype(og_ref.dtype)

    return pl.pallas_call(
        body, name="post_fwd", grid=(T // bt,),
        in_specs=[pl.BlockSpec((2, bt, D), lambda i: (0, i, 0)), pl.BlockSpec((bt, D), lambda i: (i, CZ)),
                  pl.BlockSpec((8, 128), lambda i: (0, 0))],
        out_specs=pl.BlockSpec((bt, D), lambda i: (i, 0)), out_shape=_sds((T, D), MM),
        compiler_params=_cp(("parallel",)))(o2, pm, nw)


def post_bwd(o2, pm, nw, dog, dpm, *, bt=512):
    T = pm.shape[0]
    bt = min(bt, T)

    def body(o_ref, z_ref, nw_ref, dog_ref, _alias, do_ref, dz_ref, dnw_ref):
        @pl.when(pl.program_id(0) == 0)
        def _():
            dnw_ref[...] = jnp.zeros_like(dnw_ref)

        _, vjp = jax.vjp(_post_rows, o_ref[0], o_ref[1], z_ref[...], nw_ref[0:1, :])
        doa, _unused, dz, dnw = vjp(dog_ref[...])
        do_ref[...] = doa
        dz_ref[...] = dz
        row8 = lax.broadcasted_iota(jnp.int32, (8, 128), 0)
        dnw_ref[...] += jnp.where(row8 == 0, dnw, 0.0)

    in_specs = [pl.BlockSpec((2, bt, D), lambda i: (0, i, 0)), pl.BlockSpec((bt, D), lambda i: (i, CZ)),
                pl.BlockSpec((8, 128), lambda i: (0, 0)), pl.BlockSpec((bt, D), lambda i: (i, 0)),
                pl.BlockSpec(memory_space=pl.ANY)]
    return pl.pallas_call(
        body, name="post_bwd", grid=(T // bt,), in_specs=in_specs,
        out_specs=[pl.BlockSpec((bt, D), lambda i: (i, 0)), pl.BlockSpec((bt, D), lambda i: (i, CZ)),
                   pl.BlockSpec((8, 128), lambda i: (0, 0))],
        out_shape=[_sds((T, D)), _sds(dpm.shape), _sds((8, 128))], input_output_aliases={4: 1},
        compiler_params=_cp(("arbitrary",)))(o2, pm, nw, dog, dpm)


def merge_fwd(ya, yb, pm, *, bt=512):
    T = pm.shape[0]
    bt = min(bt, T)

    def body(ya_ref, yb_ref, ga_ref, gb_ref, o_ref):
        o_ref[...] = (_sigmoid(ga_ref[...]) * ya_ref[...] + _sigmoid(gb_ref[...]) * yb_ref[...]).astype(o_ref.dtype)

    row = pl.BlockSpec((bt, D), lambda i: (i, 0))
    return pl.pallas_call(
        body, name="merge_fwd", grid=(T // bt,),
        in_specs=[row, row, pl.BlockSpec((bt, D), lambda i: (i, CGA)), pl.BlockSpec((bt, D), lambda i: (i, CGB))],
        out_specs=row, out_shape=_sds((T, D), MM), compiler_params=_cp(("parallel",)))(ya, yb, pm, pm)


def merge_bwd(ya, yb, pm, dmix, *, bt=512):
    T = pm.shape[0]
    bt = min(bt, T)

    def body(ya_ref, yb_ref, ga_ref, gb_ref, dm_ref, dya_ref, dyb_ref, dg_ref):
        dm = dm_ref[...]
        sa, sb = _sigmoid(ga_ref[...]), _sigmoid(gb_ref[...])
        dya_ref[...] = (dm * sa).astype(dya_ref.dtype)
        dyb_ref[...] = (dm * sb).astype(dyb_ref.dtype)
        dg_ref[:, 0:D] = dm * ya_ref[...] * sa * (1.0 - sa)
        dg_ref[:, D:2 * D] = dm * yb_ref[...] * sb * (1.0 - sb)

    row = pl.BlockSpec((bt, D), lambda i: (i, 0))
    return pl.pallas_call(
        body, name="merge_bwd", grid=(T // bt,),
        in_specs=[row, row, pl.BlockSpec((bt, D), lambda i: (i, CGA)), pl.BlockSpec((bt, D), lambda i: (i, CGB)), row],
        out_specs=[row, row, pl.BlockSpec((bt, 2 * D), lambda i: (i, CGA // 2))],
        out_shape=[_sds((T, D), MM), _sds((T, D), MM), _sds((T, NMAIN))],
        compiler_params=_cp(("parallel",)))(ya, yb, pm, pm, dmix)


def _ln_rows(x, y, bias, g, b):
    r = ALPHA * x + y + bias
    mu = jnp.mean(r, axis=-1, keepdims=True)
    var = jnp.mean(jnp.square(r - mu), axis=-1, keepdims=True)
    return (r - mu) * lax.rsqrt(var + LN_EPS) * g + b


def ln_fwd(x, y, p, *, name, bt=512):
    T = x.shape[0]
    bt = min(bt, T)

    def body(x_ref, y_ref, p_ref, o_ref, ob_ref):
        r = _ln_rows(x_ref[...], y_ref[...], p_ref[0:1, :], p_ref[1:2, :], p_ref[2:3, :])
        o_ref[...] = r
        ob_ref[...] = r.astype(ob_ref.dtype)

    row = pl.BlockSpec((bt, D), lambda i: (i, 0))
    return pl.pallas_call(
        body, name=name, grid=(T // bt,), in_specs=[row, row, pl.BlockSpec((8, D), lambda i: (0, 0))],
        out_specs=[row, row], out_shape=[_sds((T, D)), _sds((T, D), MM)],
        compiler_params=_cp(("parallel",)))(x, y, p)


def ln_bwd(x, y, p, ct, ct2=None, *, name, bt=512):
    T = x.shape[0]
    bt = min(bt, T)

    def body(*refs):
        it = iter(refs)
        x_ref, y_ref, p_ref, c_ref = next(it), next(it), next(it), next(it)
        c2_ref = next(it) if ct2 is not None else None
        dxa_ref, dr_ref, dp_ref = next(it), next(it), next(it)

        @pl.when(pl.program_id(0) == 0)
        def _():
            dp_ref[...] = jnp.zeros_like(dp_ref)

        c = c_ref[...]
        if c2_ref is not None:
            c = c + c2_ref[...]
        _, vjp = jax.vjp(_ln_rows, x_ref[...], y_ref[...], p_ref[0:1, :], p_ref[1:2, :], p_ref[2:3, :])
        _dx, dy, dbias, dg, db = vjp(c)
        dxa_ref[...] = ALPHA * dy
        dr_ref[...] = dy.astype(dr_ref.dtype)
        row8 = lax.broadcasted_iota(jnp.int32, (8, D), 0)
        dp_ref[...] += jnp.where(row8 == 0, dbias, jnp.where(row8 == 1, dg, jnp.where(row8 == 2, db, 0.0)))

    row = pl.BlockSpec((bt, D), lambda i: (i, 0))
    in_specs = [row, row, pl.BlockSpec((8, D), lambda i: (0, 0)), row] + ([row] if ct2 is not None else [])
    args = [x, y, p, ct] + ([ct2] if ct2 is not None else [])
    return pl.pallas_call(
        body, name=name, grid=(T // bt,), in_specs=in_specs,
        out_specs=[row, row, pl.BlockSpec((8, D), lambda i: (0, 0))],
        out_shape=[_sds((T, D)), _sds((T, D), MM), _sds((8, D))],
        compiler_params=_cp(("arbitrary",)))(*args)


def loss_fwd_bwd(xl, target, *, bt=512):
    T = xl.shape[0]
    bt = min(bt, T)

    def body(x_ref, t_ref, l_ref, d_ref):
        @pl.when(pl.program_id(0) == 0)
        def _():
            l_ref[...] = jnp.zeros_like(l_ref)

        e = x_ref[...] - t_ref[...]
        d_ref[...] = e * (1.0 / D)
        l_ref[...] += 0.5 * jnp.sum(jnp.mean(e * e, axis=-1, keepdims=True), axis=0, keepdims=True)

    row = pl.BlockSpec((bt, D), lambda i: (i, 0))
    return pl.pallas_call(
        body, name="loss", grid=(T // bt,), in_specs=[row, row],
        out_specs=[pl.BlockSpec((8, 128), lambda i: (0, 0)), row], out_shape=[_sds((8, 128)), _sds((T, D))],
        compiler_params=_cp(("arbitrary",)))(xl, target)


def _row_tile(R, Cc, elems=1 << 18):
    if R * Cc <= elems:
        return R
    tr = 8
    while tr * 2 * Cc <= elems and R % (tr * 2) == 0:
        tr *= 2
    return tr


def adam(w, m, v, ga, gb=None, *, name):
    R, Cc = w.shape
    tr = _row_tile(R, Cc)

    def body(*refs):
        it = iter(refs)
        w_ref, m_ref, v_ref, a_ref = next(it), next(it), next(it), next(it)
        b_ref = next(it) if gb is not None else None
        g_ref, d_ref, mo_ref, vo_ref = next(it), next(it), next(it), next(it)
        g = a_ref[...]
        if b_ref is not None:
            g = g + b_ref[...]
        m2 = B1 * m_ref[...] + (1.0 - B1) * g
        v2 = B2 * v_ref[...] + (1.0 - B2) * jnp.square(g)
        m_hat = m2 / (1.0 - B1 ** STEP)
        v_hat = v2 / (1.0 - B2 ** STEP)
        g_ref[...] = g
        d_ref[...] = -LR * (m_hat / (jnp.sqrt(v_hat) + EPS) + WD * w_ref[...])
        mo_ref[...] = m2
        vo_ref[...] = v2

    blk = pl.BlockSpec((tr, Cc), lambda i: (i, 0))
    args = [w, m, v, ga] + ([gb] if gb is not None else [])
    return pl.pallas_call(
        body, name=name, grid=(R // tr,), in_specs=[blk] * len(args), out_specs=[blk] * 4,
        out_shape=[_sds((R, Cc))] * 4, compiler_params=_cp(("parallel",)))(*args)


def sum4(own, recv, *, name):
    R, Cc = own.shape
    tr = _row_tile(R, Cc)

    def body(o_ref, r_ref, out_ref):
        f = lambda t: t.astype(F32)
        out_ref[...] = ((f(o_ref[...]) + f(r_ref[0])) + f(r_ref[1])) + f(r_ref[2])

    return pl.pallas_call(
        body, name=name, grid=(R // tr,),
        in_specs=[pl.BlockSpec((tr, Cc), lambda i: (i, 0)), pl.BlockSpec((3, tr, Cc), lambda i: (0, i, 0))],
        out_specs=pl.BlockSpec((tr, Cc), lambda i: (i, 0)), out_shape=_sds((R, Cc)),
        compiler_params=_cp(("parallel",)))(own, recv)


def _place():
    return lax.axis_index("x"), lax.axis_index("y"), lax.axis_index("c")


def _other_chips(x, y):
    return [(1 - x, y), (x, 1 - y), (1 - x, 1 - y)]


_ANY = pl.BlockSpec(memory_space=pl.ANY)


def allgather_xy(arrs):
    n = len(arrs)
    halves = [a.shape[0] // 2 for a in arrs]

    def body(*refs):
        ins, outs = refs[:n], refs[n:2 * n]
        send, recv, fsend, frecv, loc = refs[2 * n:]
        x, y, c = _place()
        me = 2 * x + y
        peers = _other_chips(x, y)
        local = [pltpu.make_async_copy(ins[a], outs[a].at[me], loc.at[a]) for a in range(n)]
        for cp in local:
            cp.start()

        def over_ici(a, j, block, src=None):
            px, py = peers[j]
            mine = pl.ds(c * halves[a], halves[a])
            dst = outs[a].at[block, mine]
            return pltpu.make_async_remote_copy(
                src_ref=dst if src is None else src.at[mine], dst_ref=dst, send_sem=send.at[3 * a + j],
                recv_sem=recv.at[3 * a + j], device_id=(px, py, c), device_id_type=MESH)

        def over_d2d(a, j, half):
            px, py = peers[j]
            rows = outs[a].at[2 * px + py, pl.ds(half * halves[a], halves[a])]
            return pltpu.make_async_remote_copy(
                src_ref=rows, dst_ref=rows, send_sem=fsend.at[3 * a + j], recv_sem=frecv.at[3 * a + j],
                device_id=(x, y, 1 - c), device_id_type=MESH)

        sends = [over_ici(a, j, me, src=ins[a]) for a in range(n) for j in range(3)]
        for cp in sends:
            cp.start()
        passed = []
        for a in range(n):
            for j, (px, py) in enumerate(peers):
                over_ici(a, j, 2 * px + py).wait_recv()
                passed.append(over_d2d(a, j, c))
                passed[-1].start()
        for a in range(n):
            for j in range(3):
                over_d2d(a, j, 1 - c).wait_recv()
        for cp in sends + passed:
            cp.wait_send()
        for cp in local:
            cp.wait()

    return pl.pallas_call(
        body, name="allgather_xy", in_specs=[_ANY] * n, out_specs=[_ANY] * n,
        out_shape=[_sds((4,) + a.shape, a.dtype) for a in arrs],
        scratch_shapes=[pltpu.SemaphoreType.DMA((3 * n,))] * 4 + [pltpu.SemaphoreType.DMA((n,))],
        compiler_params=pltpu.CompilerParams(has_side_effects=True))(*arrs)


def scatter_xy(groups):
    L = len(groups[0])
    n = len(groups) * L
    flat = [g for grp in groups for g in grp]

    def body(*refs):
        ins = refs[:n]
        owns, recvs = refs[n:n + len(groups)], refs[n + len(groups):n + 2 * len(groups)]
        send, recv, loc = refs[n + 2 * len(groups):]
        x, y, c = _place()
        me = 2 * x + y
        peers = _other_chips(x, y)
        local, sends = [], []
        for a in range(len(groups)):
            for l in range(L):
                i = a * L + l
                local.append(pltpu.make_async_copy(ins[i].at[me], owns[a].at[l], loc.at[i]))
                for j, (px, py) in enumerate(peers):
                    sends.append(pltpu.make_async_remote_copy(
                        src_ref=ins[i].at[2 * px + py], dst_ref=recvs[a].at[j, l], send_sem=send.at[3 * i + j],
                        recv_sem=recv.at[3 * i + j], device_id=(px, py, c), device_id_type=MESH))
        for cp in local + sends:
            cp.start()
        for cp in sends:
            cp.wait_recv()
        for cp in sends:
            cp.wait_send()
        for cp in local:
            cp.wait()

    out_shape = ([_sds((L,) + grp[0].shape[1:], grp[0].dtype) for grp in groups]
                 + [_sds((3, L) + grp[0].shape[1:], grp[0].dtype) for grp in groups])
    outs = pl.pallas_call(
        body, name="scatter_xy", in_specs=[_ANY] * n, out_specs=[_ANY] * (2 * len(groups)), out_shape=out_shape,
        scratch_shapes=[pltpu.SemaphoreType.DMA((3 * n,)), pltpu.SemaphoreType.DMA((3 * n,)),
                        pltpu.SemaphoreType.DMA((n,))],
        compiler_params=pltpu.CompilerParams(has_side_effects=True))(*flat)
    return outs[:len(groups)], outs[len(groups):]


def swap_c(arrs):
    n = len(arrs)

    def body(*refs):
        ins, outs = refs[:n], refs[n:2 * n]
        send, recv = refs[2 * n:]
        x, y, c = _place()
        cps = [pltpu.make_async_remote_copy(src_ref=ins[a], dst_ref=outs[a], send_sem=send.at[a], recv_sem=recv.at[a],
                                            device_id=(x, y, 1 - c), device_id_type=MESH) for a in range(n)]
        for cp in cps:
            cp.start()
        for cp in cps:
            cp.wait_recv()
        for cp in cps:
            cp.wait_send()

    return pl.pallas_call(
        body, name="swap_c", in_specs=[_ANY] * n, out_specs=[_ANY] * n, out_shape=[_sds(a.shape, a.dtype) for a in arrs],
        scratch_shapes=[pltpu.SemaphoreType.DMA((n,)), pltpu.SemaphoreType.DMA((n,))],
        compiler_params=pltpu.CompilerParams(has_side_effects=True))(*arrs)


def allreduce_small(v):
    R = v.shape[0]

    def body(v_ref, o_ref, buf, send, recv):
        x, y, c = _place()
        me = 4 * x + 2 * y + c
        buf[0] = v_ref[...]

        def cp(k):
            dx, dy, dc = (k >> 2) & 1, (k >> 1) & 1, k & 1
            return pltpu.make_async_remote_copy(
                src_ref=v_ref, dst_ref=buf.at[k], send_sem=send.at[k - 1], recv_sem=recv.at[k - 1],
                device_id=(x ^ dx, y ^ dy, c ^ dc), device_id_type=MESH)

        cps = [cp(k) for k in range(1, 8)]
        for t in cps:
            t.start()
        for t in cps:
            t.wait_recv()
        acc = buf[me]
        for dev in range(1, 8):
            acc = acc + buf[jnp.bitwise_xor(me, dev)]
        o_ref[...] = acc
        for t in cps:
            t.wait_send()

    vm = pl.BlockSpec(memory_space=pltpu.VMEM)
    return pl.pallas_call(
        body, name="allreduce_small", in_specs=[vm], out_specs=vm, out_shape=_sds((R, 128)),
        scratch_shapes=[pltpu.VMEM((8, R, 128), F32), pltpu.SemaphoreType.DMA((7,)), pltpu.SemaphoreType.DMA((7,))],
        compiler_params=pltpu.CompilerParams(has_side_effects=True, vmem_limit_bytes=VMEM_LIMIT))(v)


def _rows8(*rows):
    n = rows[0].shape[-1]
    t = jnp.stack([r.reshape(n).astype(F32) for r in rows])
    return jnp.pad(t, ((0, 8 - len(rows)), (0, 0)))


def _lanes128(a):
    f = a.reshape(-1).astype(F32)
    return jnp.pad(f, (0, 128 - f.shape[0]))


def _layer_fwd(x, xb, W):
    pm = mm_nn(xb, W["w_main"], name="proj_main")
    pab = mm_nn(xb, W["w_ab"], name="proj_ab")
    q, k, v, gb = pre_qkv_fwd(pm, pab, W["cw"], W["gp"])
    sc = pre_sc_fwd(pm, W["csc"])
    u, w, qd, kd, pp, ys, eg = gdn_intra_fwd(q, k, v, gb)
    o2, s0 = gdn_scan_fwd(u, w, qd, kd, pp, eg)
    og = post_fwd(o2, pm, W["nw"])
    ya = mm_nn(og, W["w_og"], name="proj_og")
    yb = mm_nn(sc, W["w_osc"], name="proj_osc")
    mixed = merge_fwd(ya, yb, pm)
    out = mm_nn(mixed, W["w_out"], name="proj_out")
    x1, x1b = ln_fwd(x, out, W["ln1"], name="ln1_fwd")
    hpre, h = mm_nn(x1b, W["w_up"], bias=W["b_up"], relu2=True, name="mlp_up")
    dn = mm_nn(h, W["w_down"], name="mlp_down")
    x2, x2b = ln_fwd(x1, dn, W["ln2"], name="ln2_fwd")
    saved = dict(x=x, xb=xb, pm=pm, pab=pab, q=q, k=k, v=v, gb=gb, sc=sc, o2=o2, s0=s0, og=og, ya=ya, yb=yb,
                 u=u, w=w, qd=qd, kd=kd, pp=pp, ys=ys, eg=eg,
                 mixed=mixed, out=out, x1=x1, x1b=x1b, hpre=hpre, h=h, dn=dn)
    return x2, x2b, saved


def _layer_bwd(ct, W, S):
    dxa2, dr2b, dp2 = ln_bwd(S["x1"], S["dn"], W["ln2"], ct, name="ln2_bwd")
    g_down = mm_tn(S["h"], dr2b, name="dw_down")
    dhpre, db_up = mm_nt(dr2b, W["w_down"], dact=S["hpre"], out_dtype=MM, name="mlp_down_bwd")
    g_up = mm_tn(S["x1b"], dhpre, name="dw_up")
    dx1 = mm_nt(dhpre, W["w_up"], add=dxa2, name="mlp_up_bwd")
    dxa1, dr1b, dp1 = ln_bwd(S["x"], S["out"], W["ln1"], dx1, name="ln1_bwd")
    g_out = mm_tn(S["mixed"], dr1b, name="dw_out")
    dmix = mm_nt(dr1b, W["w_out"], name="proj_out_bwd")
    dya, dyb, dpm = merge_bwd(S["ya"], S["yb"], S["pm"], dmix)
    g_og = mm_tn(S["og"], dya, name="dw_og")
    g_osc = mm_tn(S["sc"], dyb, name="dw_osc")
    dog = mm_nt(dya, W["w_og"], name="proj_og_bwd")
    dsc = mm_nt(dyb, W["w_osc"], name="proj_osc_bwd")
    do, dpm, dnw = post_bwd(S["o2"], S["pm"], W["nw"], dog, dpm)
    du, dw, dqd, dkd, dpp, deg = gdn_scan_bwd(S["u"], S["w"], S["qd"], S["kd"], S["pp"], S["eg"], S["s0"], do)
    dq2, dk2, dv2, dgb2 = gdn_intra_bwd(S["q"], S["k"], S["v"], S["gb"], S["ys"], du, dw, dqd, dkd, dpp, deg)
    dpm, dpab, dcw, dgp = pre_qkv_bwd(S["pm"], S["pab"], W["cw"], W["gp"], dq2, dk2, dv2, dgb2, dpm)
    dpm, dcsc = pre_sc_bwd(S["pm"], W["csc"], dsc, dpm)
    g_main = mm_tn(S["xb"], dpm, name="dw_main")
    g_ab = mm_tn(S["xb"], dpab, name="dw_ab")
    t = mm_nt(dpab, W["w_ab"], add=dxa1, name="proj_ab_bwd")
    dx = mm_nt(dpm, W["w_main"], add=t, name="proj_main_bwd")
    g_in = jnp.concatenate([g_main[:, :3 * D], g_main[:, 8 * D:], g_ab[:, :4 * NH], g_main[:, 3 * D:8 * D]], axis=1)
    grads = dict(
        w_in=g_in, w_o_gdn=g_og, w_o_sc=g_osc, w_out=g_out, w_up=g_up, w_down=g_down,
        conv_qkv=dcw[:3], conv_sc=dcsc[:3], a_log=dgp[0, :2 * NH].reshape(2, NH), dt_bias=dgp[1, :2 * NH].reshape(2, NH),
        gdn_norm_w=dnw[0], ln1_g=dp1[1], ln1_b=dp1[2], b_up=db_up[0], b_down=dp2[0], ln2_g=dp2[1], ln2_b=dp2[2])
    return dx, grads


def _layer_weights(l, full, a_log, dt_bias, gdn_norm_w, ln1_g, ln1_b, b_up, b_down, ln2_g, ln2_b):
    w_in = full["w_in"][l]
    w_main = jnp.concatenate([w_in[:, :3 * D], w_in[:, 4 * D + 4 * NH:], w_in[:, 3 * D:4 * D]], axis=1)
    w_ab = jnp.pad(w_in[:, 4 * D:4 * D + 4 * NH], ((0, 0), (0, 128 - 4 * NH)))
    return dict(
        w_main=w_main, w_ab=w_ab, w_og=full["w_o_gdn"][l], w_osc=full["w_o_sc"][l], w_out=full["w_out"][l],
        w_up=full["w_up"][l], w_down=full["w_down"][l],
        cw=jnp.pad(full["conv_qkv"][l].astype(F32), ((0, 5), (0, 0))),
        csc=jnp.pad(full["conv_sc"][l].astype(F32), ((0, 5), (0, 0))),
        gp=_rows8(_lanes128(a_log[l]), _lanes128(dt_bias[l])), nw=_rows8(gdn_norm_w[l]),
        ln1=_rows8(jnp.zeros((D,), F32), ln1_g[l], ln1_b[l]), ln2=_rows8(b_down[l], ln2_g[l], ln2_b[l]),
        b_up=b_up[l].reshape(1, DFF).astype(F32))


def local_step(xs, target, full, a_log, dt_bias, gdn_norm_w, ln1_g, ln1_b, b_up, b_down, ln2_g, ln2_b):
    Ws = [_layer_weights(l, full, a_log, dt_bias, gdn_norm_w, ln1_g, ln1_b, b_up, b_down, ln2_g, ln2_b)
          for l in range(DEPTH)]
    x, xb = xs, xs.astype(MM)
    saved = []
    for l in range(DEPTH):
        x, xb, S = _layer_fwd(x, xb, Ws[l])
        saved.append(S)
    loss_tile, ct = loss_fwd_bwd(x, target)
    grads = [None] * DEPTH
    for l in reversed(range(DEPTH)):
        ct, grads[l] = _layer_bwd(ct, Ws[l], saved[l])
    return loss_tile, ct, grads


BIG = ("w_in", "w_o_gdn", "w_o_sc", "w_out", "w_up", "w_down")
SMALL = ("conv_qkv", "a_log", "dt_bias", "gdn_norm_w", "conv_sc", "ln1_g", "ln1_b", "b_up", "b_down", "ln2_g", "ln2_b")
ORDER = ("w_in", "conv_qkv", "a_log", "dt_bias", "gdn_norm_w", "w_o_gdn", "conv_sc", "w_o_sc", "w_out", "ln1_g",
         "ln1_b", "w_up", "b_up", "w_down", "b_down", "ln2_g", "ln2_b")


def _pack(arrs):
    flat = jnp.concatenate([a.reshape(-1).astype(F32) for a in arrs])
    n = flat.shape[0]
    rows = -(-n // 1024) * 8
    return jnp.pad(flat, (0, rows * 128 - n)).reshape(rows, 128)


def _unpack(buf, like):
    flat = buf.reshape(-1)
    out, o = [], 0
    for a in like:
        n = 1
        for s in a.shape:
            n *= s
        out.append(flat[o:o + n].reshape(a.shape))
        o += n
    return out


def _gathered(name, g):
    if name in ("w_in", "w_up", "conv_qkv", "conv_sc"):
        t = jnp.moveaxis(g, 0, -2)
        return t.reshape(t.shape[:-2] + (t.shape[-2] * t.shape[-1],))
    t = jnp.moveaxis(g, 0, 1)
    return t.reshape((t.shape[0], t.shape[1] * t.shape[2]) + t.shape[3:])


def _by_chip(name, g):
    if name in ("w_in", "w_up"):
        r, ccols = g.shape
        return jnp.moveaxis(g.reshape(r, 4, ccols // 4), 1, 0)
    return g.reshape((4, g.shape[0] // 4) + g.shape[1:])


def kernel(x, w_in, conv_qkv, a_log, dt_bias, gdn_norm_w, w_o_gdn, conv_sc, w_o_sc, w_out, ln1_g, ln1_b, w_up, b_up, w_down, b_down, ln2_g, ln2_b, loss_target, m_w_in, m_conv_qkv, m_a_log, m_dt_bias, m_gdn_norm_w, m_w_o_gdn, m_conv_sc, m_w_o_sc, m_w_out, m_ln1_g, m_ln1_b, m_w_up, m_b_up, m_w_down, m_b_down, m_ln2_g, m_ln2_b, v_w_in, v_conv_qkv, v_a_log, v_dt_bias, v_gdn_norm_w, v_w_o_gdn, v_conv_sc, v_w_o_sc, v_w_out, v_ln1_g, v_ln1_b, v_w_up, v_b_up, v_w_down, v_b_down, v_ln2_g, v_ln2_b):
    w = dict(w_in=w_in, conv_qkv=conv_qkv, a_log=a_log, dt_bias=dt_bias, gdn_norm_w=gdn_norm_w, w_o_gdn=w_o_gdn,
             conv_sc=conv_sc, w_o_sc=w_o_sc, w_out=w_out, ln1_g=ln1_g, ln1_b=ln1_b, w_up=w_up, b_up=b_up,
             w_down=w_down, b_down=b_down, ln2_g=ln2_g, ln2_b=ln2_b)
    m = dict(w_in=m_w_in, conv_qkv=m_conv_qkv, a_log=m_a_log, dt_bias=m_dt_bias, gdn_norm_w=m_gdn_norm_w,
             w_o_gdn=m_w_o_gdn, conv_sc=m_conv_sc, w_o_sc=m_w_o_sc, w_out=m_w_out, ln1_g=m_ln1_g, ln1_b=m_ln1_b,
             w_up=m_w_up, b_up=m_b_up, w_down=m_w_down, b_down=m_b_down, ln2_g=m_ln2_g, ln2_b=m_ln2_b)
    v = dict(w_in=v_w_in, conv_qkv=v_conv_qkv, a_log=v_a_log, dt_bias=v_dt_bias, gdn_norm_w=v_gdn_norm_w,
             w_o_gdn=v_w_o_gdn, conv_sc=v_conv_sc, w_o_sc=v_w_o_sc, w_out=v_w_out, ln1_g=v_ln1_g, ln1_b=v_ln1_b,
             w_up=v_w_up, b_up=v_b_up, w_down=v_w_down, b_down=v_b_down, ln2_g=v_ln2_g, ln2_b=v_ln2_b)
    chip = 2 * lax.axis_index("x") + lax.axis_index("y")

    names = BIG + ("conv_qkv", "conv_sc")
    got = allgather_xy([w[n].astype(MM) if n in BIG else w[n] for n in names])
    full = {n: _gathered(n, g) for n, g in zip(names, got)}

    loss_tile, dx, grads = local_step(x[0], loss_target[0], full, a_log, dt_bias, gdn_norm_w, ln1_g, ln1_b, b_up,
                                      b_down, ln2_g, ln2_b)
    loss = lax.psum(loss_tile[0, 0], ("x", "y", "c"))

    groups = [[_by_chip(n, grads[l][n]).astype(MM) for l in range(DEPTH)] for n in BIG]
    owns, recvs = scatter_xy(groups)
    part = []
    for n, own, rec in zip(BIG, owns, recvs):
        cols = own.shape[-1]
        part.append(sum4(own.reshape(-1, cols), rec.reshape(3, -1, cols), name="sum_" + n))
    other = swap_c(part)
    out = {}
    for n, mine, theirs in zip(BIG, part, other):
        cols = mine.shape[-1]
        res = adam(w[n].reshape(-1, cols), m[n].reshape(-1, cols), v[n].reshape(-1, cols), mine, theirs, name="adam_" + n)
        out[n] = [r.reshape(w[n].shape) for r in res]

    stacked = [jnp.stack([grads[l][n] for l in range(DEPTH)]) for n in SMALL]
    summed = _unpack(allreduce_small(_pack(stacked)), stacked)
    gs = []
    for n, g in zip(SMALL, summed):
        if n in ("conv_qkv", "conv_sc"):
            blk = w[n].shape[-1]
            g = lax.dynamic_slice_in_dim(g, chip * blk, blk, axis=2)
        gs.append(g)
    res = adam(_pack([w[n] for n in SMALL]), _pack([m[n] for n in SMALL]), _pack([v[n] for n in SMALL]), _pack(gs),
               name="adam_small")
    for n, parts in zip(SMALL, zip(*[_unpack(r, gs) for r in res])):
        out[n] = list(parts)

    outs = [loss, dx[None]]
    for kind in range(4):
        outs += [out[n][kind] for n in ORDER]
    return tuple(outs)
```

```python
import functools

import jax
import jax.numpy as jnp
from jax import lax
from jax.experimental import pallas as pl
from jax.experimental.pallas import tpu as pltpu

F32 = jnp.float32
MM = jnp.bfloat16
HI = lax.Precision.HIGHEST

D = 1024
NH = 8
HD = 128
CH = 64
DFF = 4 * D
DEPTH = 4
LN_EPS = 1e-5
RMS_EPS = 1e-6
L2_EPS = 1e-6
ALPHA = (2 * DEPTH) ** 0.25
LR, B1, B2, EPS, WD, STEP = 0.001, 0.9, 0.999, 1e-08, 0.01, 10

NMAIN = 9 * D
CQ, CK, CV, CSB, CSC, CSX, CGA, CGB, CZ = range(9)
HALO = 8
VMEM_LIMIT = 56 * 1024 * 1024
MESH = pl.DeviceIdType.MESH


def _cp(sem=None, vmem=VMEM_LIMIT):
    return pltpu.CompilerParams(dimension_semantics=sem, vmem_limit_bytes=vmem)


def _sds(shape, dtype=F32):
    return jax.ShapeDtypeStruct(tuple(shape), dtype)


def _accumulate(acc, prod, k, nk, finish):
    if nk == 1:
        finish(prod)
        return

    @pl.when(k == 0)
    def _():
        acc[...] = prod

    @pl.when(k > 0)
    def _():
        acc[...] += prod

    @pl.when(k == nk - 1)
    def _():
        finish(acc[...])


def mm_nn(a, b, *, name, bias=None, relu2=False, add=None, out_dtype=F32, tm=1024, tn=1024, tk=1024):
    M, K = a.shape
    N = b.shape[1]
    tm, tn, tk = min(tm, M), min(tn, N), min(tk, K)
    nk = K // tk

    def body(*refs):
        it = iter(refs)
        a_ref, b_ref = next(it), next(it)
        bias_ref = next(it) if bias is not None else None
        add_ref = next(it) if add is not None else None
        o_ref = next(it)
        h_ref = next(it) if relu2 else None
        acc = next(it) if nk > 1 else None
        prod = jnp.dot(a_ref[...].astype(MM), b_ref[...].astype(MM), preferred_element_type=F32)

        def finish(r):
            if bias_ref is not None:
                r = r + bias_ref[...]
            if add_ref is not None:
                r = r + add_ref[...]
            o_ref[...] = r.astype(o_ref.dtype)
            if relu2:
                t = jnp.maximum(r, 0.0)
                h_ref[...] = (t * t).astype(h_ref.dtype)

        _accumulate(acc, prod, pl.program_id(2), nk, finish)

    in_specs = [pl.BlockSpec((tm, tk), lambda i, j, k: (i, k)), pl.BlockSpec((tk, tn), lambda i, j, k: (k, j))]
    args = [a, b]
    if bias is not None:
        in_specs.append(pl.BlockSpec((1, tn), lambda i, j, k: (0, j)))
        args.append(bias)
    if add is not None:
        in_specs.append(pl.BlockSpec((tm, tn), lambda i, j, k: (i, j)))
        args.append(add)
    out_shape = [_sds((M, N), out_dtype)]
    out_specs = [pl.BlockSpec((tm, tn), lambda i, j, k: (i, j))]
    if relu2:
        out_shape.append(_sds((M, N), MM))
        out_specs.append(pl.BlockSpec((tm, tn), lambda i, j, k: (i, j)))
    res = pl.pallas_call(
        body, name=name, grid=(M // tm, N // tn, nk), in_specs=in_specs, out_specs=out_specs, out_shape=out_shape,
        scratch_shapes=[pltpu.VMEM((tm, tn), F32)] if nk > 1 else [],
        compiler_params=_cp(("parallel", "parallel", "arbitrary")))(*args)
    return res if relu2 else res[0]


def mm_nt(a, b, *, name, add=None, dact=None, out_dtype=F32, tm=1024, tn=1024, tk=1024):
    M, Nc = a.shape
    Ko = b.shape[0]
    tm, tn, tk = min(tm, M), min(tn, Ko), min(tk, Nc)
    nk = Nc // tk
    ni = M // tm

    def body(*refs):
        it = iter(refs)
        a_ref, b_ref = next(it), next(it)
        add_ref = next(it) if add is not None else None
        d_ref = next(it) if dact is not None else None
        o_ref = next(it)
        db_ref = next(it) if dact is not None else None
        acc = next(it) if nk > 1 else None
        i = pl.program_id(1)
        prod = lax.dot_general(a_ref[...].astype(MM), b_ref[...].astype(MM), (((1,), (1,)), ((), ())),
                               preferred_element_type=F32)

        def finish(r):
            if add_ref is not None:
                r = r + add_ref[...]
            if d_ref is not None:
                r = r * (2.0 * jnp.maximum(d_ref[...], 0.0))
                s = jnp.sum(r, axis=0, keepdims=True)
                row0 = lax.broadcasted_iota(jnp.int32, db_ref.shape, 0) == 0

                @pl.when(i == 0)
                def _():
                    db_ref[...] = jnp.zeros_like(db_ref)

                db_ref[...] += jnp.where(row0, s, 0.0)
            o_ref[...] = r.astype(o_ref.dtype)

        _accumulate(acc, prod, pl.program_id(2), nk, finish)

    in_specs = [pl.BlockSpec((tm, tk), lambda j, i, k: (i, k)), pl.BlockSpec((tn, tk), lambda j, i, k: (j, k))]
    args = [a, b]
    for extra in (add, dact):
        if extra is not None:
            in_specs.append(pl.BlockSpec((tm, tn), lambda j, i, k: (i, j)))
            args.append(extra)
    out_shape = [_sds((M, Ko), out_dtype)]
    out_specs = [pl.BlockSpec((tm, tn), lambda j, i, k: (i, j))]
    if dact is not None:
        out_shape.append(_sds((8, Ko), F32))
        out_specs.append(pl.BlockSpec((8, tn), lambda j, i, k: (0, j)))
    res = pl.pallas_call(
        body, name=name, grid=(Ko // tn, ni, nk), in_specs=in_specs, out_specs=out_specs, out_shape=out_shape,
        scratch_shapes=[pltpu.VMEM((tm, tn), F32)] if nk > 1 else [],
        compiler_params=_cp(("parallel", "arbitrary", "arbitrary")))(*args)
    return res if dact is not None else res[0]


def mm_tn(a, b, *, name, tm=1024, tn=1024, tk=1024):
    T, M = a.shape
    N = b.shape[1]
    tm, tn, tk = min(tm, M), min(tn, N), min(tk, T)

    def body(a_ref, b_ref, o_ref):
        prod = lax.dot_general(a_ref[...].astype(MM), b_ref[...].astype(MM), (((0,), (0,)), ((), ())),
                               preferred_element_type=F32)
        k = pl.program_id(2)

        @pl.when(k == 0)
        def _():
            o_ref[...] = prod

        @pl.when(k > 0)
        def _():
            o_ref[...] += prod

    return pl.pallas_call(
        body, name=name, grid=(M // tm, N // tn, T // tk),
        in_specs=[pl.BlockSpec((tk, tm), lambda i, j, k: (k, i)), pl.BlockSpec((tk, tn), lambda i, j, k: (k, j))],
        out_specs=pl.BlockSpec((tm, tn), lambda i, j, k: (i, j)), out_shape=_sds((M, N)),
        compiler_params=_cp(("parallel", "parallel", "arbitrary")))(a, b)


def _sigmoid(x):
    return 1.0 / (1.0 + jnp.exp(-x))


def _silu(x):
    return x * _sigmoid(x)


def _softplus(x):
    return jnp.maximum(x, 0.0) + jnp.log1p(jnp.exp(-jnp.abs(x)))


def _ext(main_ref, prev_ref, next_ref, first, last):
    p = jnp.where(first, 0.0, prev_ref[...].astype(F32))
    n = jnp.where(last, 0.0, next_ref[...].astype(F32))
    return jnp.concatenate([p, main_ref[...].astype(F32), n], axis=0)


def _shift_dn(x):
    return pltpu.roll(x, 1, 0)


def _shift_up(x):
    return pltpu.roll(x, x.shape[0] - 1, 0)


def _conv3(xe, w):
    return w[0:1, :] * _shift_dn(xe) + w[1:2, :] * xe + w[2:3, :] * _shift_up(xe)


def _conv3_t(de, w):
    return w[0:1, :] * _shift_up(de) + w[1:2, :] * de + w[2:3, :] * _shift_dn(de)


def _halo_specs(bt, T, col, lead=None):
    r = bt // HALO
    last = T // HALO - 1
    if lead is None:
        return [pl.BlockSpec((bt, D), lambda i: (i, col)),
                pl.BlockSpec((HALO, D), lambda i: (jnp.maximum(i * r - 1, 0), col)),
                pl.BlockSpec((HALO, D), lambda i: (jnp.minimum((i + 1) * r, last), col))]
    return [pl.BlockSpec((lead, bt, D), lambda i: (0, i, col)),
            pl.BlockSpec((lead, HALO, D), lambda i: (0, jnp.maximum(i * r - 1, 0), col)),
            pl.BlockSpec((lead, HALO, D), lambda i: (0, jnp.minimum((i + 1) * r, last), col))]


def _qkv_rows(cq, ck, cv):
    sq, sk, sv = _silu(cq), _silu(ck), _silu(cv)
    qs, ks = [], []
    for h in range(NH):
        s = slice(h * HD, (h + 1) * HD)
        qh, kh = sq[:, s], sk[:, s]
        qs.append(qh * lax.rsqrt(jnp.sum(qh * qh, axis=-1, keepdims=True) + L2_EPS) * (HD ** -0.5))
        ks.append(kh * lax.rsqrt(jnp.sum(kh * kh, axis=-1, keepdims=True) + L2_EPS))
    return jnp.concatenate(qs, axis=1), jnp.concatenate(ks, axis=1), sv


def _chunk_masks(bt):
    row = lax.broadcasted_iota(jnp.int32, (bt, bt), 0)
    col = lax.broadcasted_iota(jnp.int32, (bt, bt), 1)
    same = (row // CH) == (col // CH)
    lower = jnp.where(same & (col <= row), 1.0, 0.0).astype(F32)
    upper = jnp.where(same & (col >= row), 1.0, 0.0).astype(F32)
    return lower, upper


def _gate_rows(ab, gp, lower, upper):
    lane = lax.broadcasted_iota(jnp.int32, ab.shape, 1)
    g = -jnp.exp(gp[0:1, :]) * _softplus(ab + gp[1:2, :])
    g = jnp.where(lane < 2 * NH, g, 0.0)
    gf = jnp.dot(lower, g, precision=HI, preferred_element_type=F32)
    gr = jnp.dot(upper, g, precision=HI, preferred_element_type=F32)
    gc = jnp.where(lane < NH, gf, gr)
    beta = _sigmoid(ab)
    return jnp.where(lane < 2 * NH, gc, jnp.where(lane < 4 * NH, beta, 0.0))


def pre_qkv_fwd(pm, pab, cw, gp, *, bt=256):
    T = pm.shape[0]
    bt = min(bt, T)
    n = T // bt

    def body(q0, q1, q2, k0, k1, k2, v0, v1, v2, ab_ref, cw_ref, gp_ref, q_ref, k_ref, v_ref, gb_ref):
        i = pl.program_id(0)
        first, last = i == 0, i == n - 1
        cs = []
        for c, (m, p, x) in enumerate(((q0, q1, q2), (k0, k1, k2), (v0, v1, v2))):
            xe = _ext(m, p, x, first, last)
            cs.append(_conv3(xe, cw_ref[:, c * D:(c + 1) * D])[HALO:HALO + bt])
        q, k, v = _qkv_rows(*cs)
        q_ref[...], k_ref[...], v_ref[...] = q, k, v
        lower, upper = _chunk_masks(bt)
        gb_ref[...] = _gate_rows(ab_ref[...], gp_ref[...], lower, upper)

    in_specs = (_halo_specs(bt, T, CQ) + _halo_specs(bt, T, CK) + _halo_specs(bt, T, CV)
                + [pl.BlockSpec((bt, 128), lambda i: (i, 0)), pl.BlockSpec((8, 3 * D), lambda i: (0, 0)),
                   pl.BlockSpec((8, 128), lambda i: (0, 0))])
    row = pl.BlockSpec((bt, D), lambda i: (i, 0))
    return pl.pallas_call(
        body, name="pre_qkv_fwd", grid=(n,), in_specs=in_specs,
        out_specs=[row, row, row, pl.BlockSpec((bt, 128), lambda i: (i, 0))],
        out_shape=[_sds((T, D)), _sds((T, D)), _sds((T, D)), _sds((T, 128))],
        compiler_params=_cp(("parallel",)))(*([pm] * 9), pab, cw, gp)


def pre_qkv_bwd(pm, pab, cw, gp, dq2, dk2, dv2, dgb2, dpm, *, bt=128):
    T = pm.shape[0]
    bt = min(bt, T)
    n = T // bt
    E = bt + 2 * HALO

    def body(*refs):
        it = iter(refs)
        xs = [[next(it) for _ in range(3)] for _ in range(3)]
        ds = [[next(it) for _ in range(3)] for _ in range(3)]
        ab_ref, dgb_ref, cw_ref, gp_ref, _alias = next(it), next(it), next(it), next(it), next(it)
        o_ref, dab_ref, dcw_ref, dgp_ref = (next(it) for _ in range(4))
        i = pl.program_id(0)
        first, last = i == 0, i == n - 1

        @pl.when(first)
        def _():
            dcw_ref[...] = jnp.zeros_like(dcw_ref)
            dgp_ref[...] = jnp.zeros_like(dgp_ref)

        xes = [_ext(*xs[c], first, last) for c in range(3)]
        ces = [_conv3(xes[c], cw_ref[:, c * D:(c + 1) * D]) for c in range(3)]
        cts = []
        for c in range(3):
            m, p, x = ds[c]
            pe = jnp.where(first, 0.0, p[0] + p[1])
            ne = jnp.where(last, 0.0, x[0] + x[1])
            cts.append(jnp.concatenate([pe, m[0] + m[1], ne], axis=0))
        _, vjp = jax.vjp(_qkv_rows, *ces)
        dces = vjp(tuple(cts))
        rowi = lax.broadcasted_iota(jnp.int32, (E, 1), 0)
        central = (rowi >= HALO) & (rowi < HALO + bt)
        row8 = lax.broadcasted_iota(jnp.int32, (8, D), 0)
        for c in range(3):
            w = cw_ref[:, c * D:(c + 1) * D]
            o_ref[:, c * D:(c + 1) * D] = _conv3_t(dces[c], w)[HALO:HALO + bt]
            dc = jnp.where(central, dces[c], 0.0)
            taps = (jnp.sum(dc * _shift_dn(xes[c]), axis=0, keepdims=True),
                    jnp.sum(dc * xes[c], axis=0, keepdims=True),
                    jnp.sum(dc * _shift_up(xes[c]), axis=0, keepdims=True))
            upd = jnp.where(row8 == 0, taps[0], jnp.where(row8 == 1, taps[1], jnp.where(row8 == 2, taps[2], 0.0)))
            dcw_ref[:, c * D:(c + 1) * D] += upd
        lower, upper = _chunk_masks(bt)
        _, gvjp = jax.vjp(lambda ab, gp: _gate_rows(ab, gp, lower, upper), ab_ref[...], gp_ref[...])
        dab, dgp = gvjp(dgb_ref[0] + dgb_ref[1])
        dab_ref[...] = dab
        dgp_ref[...] += dgp

    in_specs = (_halo_specs(bt, T, CQ) + _halo_specs(bt, T, CK) + _halo_specs(bt, T, CV)
                + _halo_specs(bt, T, 0, lead=2) * 3
                + [pl.BlockSpec((bt, 128), lambda i: (i, 0)), pl.BlockSpec((2, bt, 128), lambda i: (0, i, 0)),
                   pl.BlockSpec((8, 3 * D), lambda i: (0, 0)), pl.BlockSpec((8, 128), lambda i: (0, 0)),
                   pl.BlockSpec(memory_space=pl.ANY)])
    out_specs = [pl.BlockSpec((bt, 3 * D), lambda i: (i, 0)), pl.BlockSpec((bt, 128), lambda i: (i, 0)),
                 pl.BlockSpec((8, 3 * D), lambda i: (0, 0)), pl.BlockSpec((8, 128), lambda i: (0, 0))]
    return pl.pallas_call(
        body, name="pre_qkv_bwd", grid=(n,), in_specs=in_specs, out_specs=out_specs,
        out_shape=[_sds(dpm.shape), _sds((T, 128)), _sds((8, 3 * D)), _sds((8, 128))],
        input_output_aliases={len(in_specs) - 1: 0},
        compiler_params=_cp(("arbitrary",)))(
            *([pm] * 9), dq2, dq2, dq2, dk2, dk2, dk2, dv2, dv2, dv2, pab, dgb2, cw, gp, dpm)


def pre_sc_fwd(pm, cw, *, bt=256):
    T = pm.shape[0]
    bt = min(bt, T)
    n = T // bt

    def body(b_ref, c0, c1, c2, x0, x1, x2, cw_ref, o_ref):
        i = pl.program_id(0)
        first, last = i == 0, i == n - 1
        pe = _ext(c0, c1, c2, first, last) * _ext(x0, x1, x2, first, last)
        o_ref[...] = (b_ref[...] * _conv3(pe, cw_ref[...])[HALO:HALO + bt]).astype(o_ref.dtype)

    in_specs = ([pl.BlockSpec((bt, D), lambda i: (i, CSB))] + _halo_specs(bt, T, CSC) + _halo_specs(bt, T, CSX)
                + [pl.BlockSpec((8, D), lambda i: (0, 0))])
    return pl.pallas_call(
        body, name="pre_sc_fwd", grid=(n,), in_specs=in_specs, out_specs=pl.BlockSpec((bt, D), lambda i: (i, 0)),
        out_shape=_sds((T, D), MM), compiler_params=_cp(("parallel",)))(*([pm] * 7), cw)


def pre_sc_bwd(pm, cw, dsc, dpm, *, bt=256):
    T = pm.shape[0]
    bt = min(bt, T)
    n = T // bt
    E = bt + 2 * HALO

    def body(b0, b1, b2, c0, c1, c2, x0, x1, x2, d0, d1, d2, cw_ref, _alias, o_ref, dcw_ref):
        i = pl.program_id(0)
        first, last = i == 0, i == n - 1

        @pl.when(first)
        def _():
            dcw_ref[...] = jnp.zeros_like(dcw_ref)

        ce, xe = _ext(c0, c1, c2, first, last), _ext(x0, x1, x2, first, last)
        pe = ce * xe
        w = cw_ref[...]
        dout = d0[...]
        o_ref[:, 0:D] = dout * _conv3(pe, w)[HALO:HALO + bt]
        dce = _ext(d0, d1, d2, first, last) * _ext(b0, b1, b2, first, last)
        dp = _conv3_t(dce, w)[HALO:HALO + bt]
        o_ref[:, D:2 * D] = dp * x0[...]
        o_ref[:, 2 * D:3 * D] = dp * c0[...]
        rowi = lax.broadcasted_iota(jnp.int32, (E, 1), 0)
        dc = jnp.where((rowi >= HALO) & (rowi < HALO + bt), dce, 0.0)
        row8 = lax.broadcasted_iota(jnp.int32, (8, D), 0)
        taps = (jnp.sum(dc * _shift_dn(pe), axis=0, keepdims=True), jnp.sum(dc * pe, axis=0, keepdims=True),
                jnp.sum(dc * _shift_up(pe), axis=0, keepdims=True))
        dcw_ref[...] += jnp.where(row8 == 0, taps[0], jnp.where(row8 == 1, taps[1], jnp.where(row8 == 2, taps[2], 0.0)))

    dsc_specs = [pl.BlockSpec((bt, D), lambda i: (i, 0)),
                 pl.BlockSpec((HALO, D), lambda i: (jnp.maximum(i * (bt // HALO) - 1, 0), 0)),
                 pl.BlockSpec((HALO, D), lambda i: (jnp.minimum((i + 1) * (bt // HALO), T // HALO - 1), 0))]
    in_specs = (_halo_specs(bt, T, CSB) + _halo_specs(bt, T, CSC) + _halo_specs(bt, T, CSX) + dsc_specs
                + [pl.BlockSpec((8, D), lambda i: (0, 0)), pl.BlockSpec(memory_space=pl.ANY)])
    return pl.pallas_call(
        body, name="pre_sc_bwd", grid=(n,), in_specs=in_specs,
        out_specs=[pl.BlockSpec((bt, 3 * D), lambda i: (i, 1)), pl.BlockSpec((8, D), lambda i: (0, 0))],
        out_shape=[_sds(dpm.shape), _sds((8, D))], input_output_aliases={len(in_specs) - 1: 0},
        compiler_params=_cp(("arbitrary",)))(*([pm] * 9), dsc, dsc, dsc, cw, dpm)


def _bdot(a, b, dims):
    return lax.dot_general(a.astype(MM), b.astype(MM), (dims, ((), ())), preferred_element_type=F32)


_NN, _NT, _TN = ((1,), (0,)), ((1,), (1,)), ((0,), (0,))


def _raw_nn(a, b):
    return _bdot(a, b, _NN)


def _raw_nt(a, b):
    return _bdot(a, b, _NT)


def _raw_tn(a, b):
    return _bdot(a, b, _TN)


def _tri_inv_y(A):
    Y = -A
    P = _raw_nn(A, A)
    Y = Y + P + _raw_nn(Y, P)
    for _ in range(4):
        P = _raw_nn(P, P)
        Y = Y + P + _raw_nn(Y, P)
    return Y


def _make_vjp_ops():
    @jax.custom_vjp
    def nn(a, b):
        return _raw_nn(a, b)

    @jax.custom_vjp
    def nt(a, b):
        return _raw_nt(a, b)

    @jax.custom_vjp
    def tn(a, b):
        return _raw_tn(a, b)

    nn.defvjp(lambda a, b: (_raw_nn(a, b), (a, b)), lambda r, g: (_raw_nt(g, r[1]), _raw_tn(r[0], g)))
    nt.defvjp(lambda a, b: (_raw_nt(a, b), (a, b)), lambda r, g: (_raw_nn(g, r[1]), _raw_tn(g, r[0])))
    tn.defvjp(lambda a, b: (_raw_tn(a, b), (a, b)), lambda r, g: (_raw_nt(r[1], g), _raw_nn(r[0], g)))

    @jax.custom_vjp
    def inv_saved(A, Y):
        return Y

    def inv_bwd(Y, g):
        M = g + _raw_tn(Y, g)
        return -(M + _raw_nt(M, Y)), jnp.zeros_like(Y)

    inv_saved.defvjp(lambda A, Y: (Y, Y), inv_bwd)
    return nn, nt, tn, inv_saved


GH = 4
GR = GH * CH
NG = NH // GH
CBI = 2


def _scan_chunks(n_chunks):
    return 4 if n_chunks % 4 == 0 else 2


def _intra_group(qs, ks, vs, gcols, bcols, incl, strict, eye, lastc, ops):
    nn, nt, tn, inv = ops
    q, k, v = (jnp.concatenate(t, axis=0) for t in (qs, ks, vs))
    gcol, bcol = jnp.concatenate(gcols, axis=0), jnp.concatenate(bcols, axis=0)
    grow = jnp.sum(eye * gcol, axis=0, keepdims=True)
    gam = jnp.where(incl, jnp.exp(jnp.where(incl, gcol - grow, 0.0)), 0.0)
    glast = jnp.sum(jnp.where(lastc, grow, 0.0), axis=1, keepdims=True)
    eg = jnp.exp(gcol)
    kb = k * bcol
    A = jnp.where(strict, nt(kb, k) * gam, 0.0)
    Y = inv(A)
    vb, kg = v * bcol, kb * eg
    u = vb + nn(Y, vb)
    w = kg + nn(Y, kg)
    P = nt(q, k) * gam
    return (u, w, P, q * eg, k * jnp.exp(glast - gcol), jnp.exp(glast)), Y


def _scan_group(Ss, us, ws, P, qds, kds, egls, ops):
    nn, nt, tn, _ = ops
    vns = [us[j] - nn(ws[j], Ss[j]) for j in range(GH)]
    o = jnp.concatenate([nn(qds[j], Ss[j]) for j in range(GH)], axis=0) + nn(P, jnp.concatenate(vns, axis=0))
    return o, [Ss[j] * egls[j] + tn(kds[j], vns[j]) for j in range(GH)]


def _group_masks(rev):
    r = lax.broadcasted_iota(jnp.int32, (GR, GR), 0)
    c = lax.broadcasted_iota(jnp.int32, (GR, GR), 1)
    same = (r // CH) == (c // CH)
    ahead = jnp.where(rev, c - r, r - c)
    incl = same & (ahead >= 0)
    strict = same & (ahead > 0)
    eye = jnp.where(r == c, 1.0, 0.0).astype(F32)
    lastc = same & ((c % CH) == jnp.where(rev, 0, CH - 1))
    return incl, strict, eye, lastc


def _head_gates(gb, h, rev):
    gcol = jnp.where(rev, gb[:, NH + h:NH + h + 1], gb[:, h:h + 1])
    bcol = jnp.where(rev, gb[:, 3 * NH + h:3 * NH + h + 1], gb[:, 2 * NH + h:2 * NH + h + 1])
    return gcol, bcol


def _hs(h, width=HD):
    return slice(h * HD, h * HD + width)


def gdn_intra_fwd(q, k, v, gb):
    T = q.shape[0]
    N = T // CH
    ops = (_raw_nn, _raw_nt, _raw_tn, _tri_inv_y)

    def body(q_ref, k_ref, v_ref, gb_ref, u_ref, w_ref, qd_ref, kd_ref, pp_ref, ys_ref, eg_ref):
        rev = pl.program_id(0) == 1
        masks = _group_masks(rev)
        for c in range(CBI):
            r = slice(c * CH, (c + 1) * CH)
            gb_t = gb_ref[r, :]
            for g in range(NG):
                heads = range(g * GH, (g + 1) * GH)
                gates = [_head_gates(gb_t, h, rev) for h in heads]
                (u, w, P, qd, kd, egl), Y = _intra_group(
                    [q_ref[r, _hs(h)] for h in heads], [k_ref[r, _hs(h)] for h in heads],
                    [v_ref[r, _hs(h)] for h in heads], [t[0] for t in gates], [t[1] for t in gates], *masks, ops)
                pp_ref[c, g] = P.astype(MM)
                ys_ref[c, g] = Y.astype(MM)
                for j, h in enumerate(heads):
                    rows = slice(j * CH, (j + 1) * CH)
                    u_ref[r, _hs(h)] = u[rows]
                    w_ref[r, _hs(h)] = w[rows].astype(MM)
                    qd_ref[r, _hs(h)] = qd[rows].astype(MM)
                    kd_ref[r, _hs(h)] = kd[rows].astype(MM)
                    eg_ref[c, h:h + 1, :] = jnp.broadcast_to(egl[j * CH:j * CH + 1, :], (1, 128))

    row = pl.BlockSpec((CBI * CH, D), lambda d, n: (n, 0))
    drow = pl.BlockSpec((None, CBI * CH, D), lambda d, n: (d, n, 0))
    mat = pl.BlockSpec((None, CBI, NG, GR, GR), lambda d, n: (d, n, 0, 0, 0))
    return pl.pallas_call(
        body, name="gdn_intra_fwd", grid=(2, N // CBI),
        in_specs=[row, row, row, pl.BlockSpec((CBI * CH, 128), lambda d, n: (n, 0))],
        out_specs=[drow] * 4 + [mat, mat, pl.BlockSpec((None, CBI, NH, 128), lambda d, n: (d, n, 0, 0))],
        out_shape=[_sds((2, T, D))] + [_sds((2, T, D), MM)] * 3 + [_sds((2, N, NG, GR, GR), MM)] * 2
                  + [_sds((2, N, NH, 128))],
        compiler_params=_cp(("parallel", "parallel")))(q, k, v, gb)


def gdn_scan_fwd(u, w, qd, kd, pp, eg):
    T = u.shape[1]
    N = T // CH
    ops = (_raw_nn, _raw_nt, _raw_tn, None)

    cbs = _scan_chunks(N)
    NB = N // cbs

    def body(u_ref, w_ref, qd_ref, kd_ref, pp_ref, eg_ref, o_ref, s0_ref, S):
        d = pl.program_id(0)

        @pl.when(pl.program_id(1) == 0)
        def _():
            S[...] = jnp.zeros_like(S)

        def chunk(c, carry):
            pc = c + d * (cbs - 1 - 2 * c)
            r = pl.ds(pl.multiple_of(pc * CH, CH), CH)
            for g in range(NG):
                heads = range(g * GH, (g + 1) * GH)
                Ss = [S[h] for h in heads]
                for h, Sh in zip(heads, Ss):
                    s0_ref[pc, h] = Sh
                o, S2 = _scan_group(Ss, [u_ref[r, _hs(h)] for h in heads], [w_ref[r, _hs(h)] for h in heads],
                                    pp_ref[pc, g], [qd_ref[r, _hs(h)] for h in heads],
                                    [kd_ref[r, _hs(h)] for h in heads], [eg_ref[pc, h:h + 1, :] for h in heads], ops)
                for j, h in enumerate(heads):
                    o_ref[r, _hs(h)] = o[j * CH:(j + 1) * CH]
                    S[h] = S2[j]
            return carry

        lax.fori_loop(0, cbs, chunk, 0)

    bidx = lambda d, n: n + d * (NB - 1 - 2 * n)
    drow = pl.BlockSpec((None, cbs * CH, D), lambda d, n: (d, bidx(d, n), 0))
    mat = pl.BlockSpec((None, cbs, NG, GR, GR), lambda d, n: (d, bidx(d, n), 0, 0, 0))
    return pl.pallas_call(
        body, name="gdn_scan_fwd", grid=(2, NB),
        in_specs=[drow] * 4 + [mat, pl.BlockSpec((None, cbs, NH, 128), lambda d, n: (d, bidx(d, n), 0, 0))],
        out_specs=[drow, pl.BlockSpec((None, cbs, NH, HD, HD), lambda d, n: (d, bidx(d, n), 0, 0, 0))],
        out_shape=[_sds((2, T, D)), _sds((2, N, NH, HD, HD))],
        scratch_shapes=[pltpu.VMEM((NH, HD, HD), F32)],
        compiler_params=_cp(("arbitrary", "arbitrary")))(u, w, qd, kd, pp, eg)


def gdn_scan_bwd(u, w, qd, kd, pp, eg, s0, do):
    T = u.shape[1]
    N = T // CH
    ops = _make_vjp_ops()
    cbs = _scan_chunks(N)
    NB = N // cbs

    def body(u_ref, w_ref, qd_ref, kd_ref, pp_ref, eg_ref, s0_ref, do_ref,
             du_ref, dw_ref, dqd_ref, dkd_ref, dpp_ref, deg_ref, dS):
        d = pl.program_id(0)

        @pl.when(pl.program_id(1) == 0)
        def _():
            dS[...] = jnp.zeros_like(dS)

        def chunk(c, carry):
            pc = (cbs - 1 - c) + d * (2 * c - (cbs - 1))
            r = pl.ds(pl.multiple_of(pc * CH, CH), CH)
            for g in range(NG):
                heads = range(g * GH, (g + 1) * GH)
                step = lambda *a: _scan_group(*a, ops)
                _, vjp = jax.vjp(
                    step, [s0_ref[pc, h] for h in heads], [u_ref[r, _hs(h)] for h in heads],
                    [w_ref[r, _hs(h)].astype(F32) for h in heads], pp_ref[pc, g].astype(F32),
                    [qd_ref[r, _hs(h)].astype(F32) for h in heads], [kd_ref[r, _hs(h)].astype(F32) for h in heads],
                    [eg_ref[pc, h:h + 1, :] for h in heads])
                do = jnp.concatenate([do_ref[r, _hs(h)] for h in heads], axis=0)
                dSs, dus, dws, dP, dqds, dkds, degs = vjp((do, [dS[h] for h in heads]))
                dpp_ref[pc, g] = dP
                for j, h in enumerate(heads):
                    dS[h] = dSs[j]
                    du_ref[r, _hs(h)] = dus[j].astype(MM)
                    dw_ref[r, _hs(h)] = dws[j].astype(MM)
                    dqd_ref[r, _hs(h)], dkd_ref[r, _hs(h)] = dqds[j], dkds[j]
                    deg_ref[pc, h:h + 1, :] = degs[j]
            return carry

        lax.fori_loop(0, cbs, chunk, 0)

    bidx = lambda d, n: (NB - 1 - n) + d * (2 * n - (NB - 1))
    drow = pl.BlockSpec((None, cbs * CH, D), lambda d, n: (d, bidx(d, n), 0))
    erow = pl.BlockSpec((None, cbs, NH, 128), lambda d, n: (d, bidx(d, n), 0, 0))
    mat = pl.BlockSpec((None, cbs, NG, GR, GR), lambda d, n: (d, bidx(d, n), 0, 0, 0))
    return pl.pallas_call(
        body, name="gdn_scan_bwd", grid=(2, NB),
        in_specs=[drow] * 4 + [mat, erow, pl.BlockSpec((None, cbs, NH, HD, HD), lambda d, n: (d, bidx(d, n), 0, 0, 0)),
                               pl.BlockSpec((cbs * CH, D), lambda d, n: (bidx(d, n), 0))],
        out_specs=[drow] * 4 + [mat, erow],
        out_shape=[_sds((2, T, D), MM)] * 2 + [_sds((2, T, D))] * 2 + [_sds((2, N, NG, GR, GR))]
                  + [_sds((2, N, NH, 128))],
        scratch_shapes=[pltpu.VMEM((NH, HD, HD), F32)],
        compiler_params=_cp(("arbitrary", "arbitrary")))(u, w, qd, kd, pp, eg, s0, do)


def gdn_intra_bwd(q, k, v, gb, ys, du, dw, dqd, dkd, dpp, deg):
    T = q.shape[0]
    N = T // CH
    nn, nt, tn, inv_saved = _make_vjp_ops()

    def body(q_ref, k_ref, v_ref, gb_ref, ys_ref, du_ref, dw_ref, dqd_ref, dkd_ref, dpp_ref, deg_ref,
             dq_ref, dk_ref, dv_ref, dgb_ref):
        rev = pl.program_id(0) == 1
        masks = _group_masks(rev)
        lane = lax.broadcasted_iota(jnp.int32, (CH, 128), 1)
        grow = lax.broadcasted_iota(jnp.int32, (GR, 1), 0)
        for c in range(CBI):
            r = slice(c * CH, (c + 1) * CH)
            gb_t = gb_ref[r, :]
            dgb = jnp.zeros((CH, 128), F32)
            for g in range(NG):
                heads = range(g * GH, (g + 1) * GH)
                gates = [_head_gates(gb_t, h, rev) for h in heads]
                Y = ys_ref[c, g].astype(F32)
                f = lambda *a, Y=Y: _intra_group(*a, *masks, (nn, nt, tn, lambda A: inv_saved(A, Y)))
                _, vjp = jax.vjp(f, [q_ref[r, _hs(h)] for h in heads], [k_ref[r, _hs(h)] for h in heads],
                                 [v_ref[r, _hs(h)] for h in heads], [t[0] for t in gates], [t[1] for t in gates])
                stack = lambda ref: jnp.concatenate([ref[r, _hs(h)].astype(F32) for h in heads], axis=0)
                degl = jnp.zeros((GR, 1), F32)
                for j, h in enumerate(heads):
                    degl = degl + jnp.where(grow == j * CH, jnp.sum(deg_ref[c, h:h + 1, :], axis=1, keepdims=True), 0.0)
                cts = (stack(du_ref), stack(dw_ref), dpp_ref[c, g], stack(dqd_ref), stack(dkd_ref), degl)
                dqs, dks, dvs, dgs, dbs = vjp((cts, jnp.zeros((GR, GR), F32)))
                for j, h in enumerate(heads):
                    dq_ref[r, _hs(h)], dk_ref[r, _hs(h)], dv_ref[r, _hs(h)] = dqs[j], dks[j], dvs[j]
                    glane = jnp.where(rev, NH + h, h)
                    dgb = dgb + jnp.where(lane == glane, dgs[j], 0.0) + jnp.where(lane == glane + 2 * NH, dbs[j], 0.0)
            dgb_ref[r, :] = dgb

    row = pl.BlockSpec((CBI * CH, D), lambda d, n: (n, 0))
    drow = pl.BlockSpec((None, CBI * CH, D), lambda d, n: (d, n, 0))
    mat = pl.BlockSpec((None, CBI, NG, GR, GR), lambda d, n: (d, n, 0, 0, 0))
    return pl.pallas_call(
        body, name="gdn_intra_bwd", grid=(2, N // CBI),
        in_specs=[row, row, row, pl.BlockSpec((CBI * CH, 128), lambda d, n: (n, 0)), mat, drow, drow, drow, drow, mat]
                 + [pl.BlockSpec((None, CBI, NH, 128), lambda d, n: (d, n, 0, 0))],
        out_specs=[drow, drow, drow, pl.BlockSpec((None, CBI * CH, 128), lambda d, n: (d, n, 0))],
        out_shape=[_sds((2, T, D))] * 3 + [_sds((2, T, 128))],
        compiler_params=_cp(("parallel", "parallel")))(q, k, v, gb, ys, du, dw, dqd, dkd, dpp, deg)


def _post_rows(o2a, o2b, z, nw):
    o = o2a + o2b
    outs = []
    for h in range(NH):
        s = slice(h * HD, (h + 1) * HD)
        oh = o[:, s]
        outs.append(oh * lax.rsqrt(jnp.mean(oh * oh, axis=-1, keepdims=True) + RMS_EPS) * nw * _silu(z[:, s]))
    return jnp.concatenate(outs, axis=1)


def post_fwd(o2, pm, nw, *, bt=512):
    T = pm.shape[0]
    bt = min(bt, T)

    def body(o_ref, z_ref, nw_ref, og_ref):
        og_ref[...] = _post_rows(o_ref[0], o_ref[1], z_ref[...], nw_ref[0:1, :]).astype(og_ref.dtype)

    return pl.pallas_call(
        body, name="post_fwd", grid=(T // bt,),
        in_specs=[pl.BlockSpec((2, bt, D), lambda i: (0, i, 0)), pl.BlockSpec((bt, D), lambda i: (i, CZ)),
                  pl.BlockSpec((8, 128), lambda i: (0, 0))],
        out_specs=pl.BlockSpec((bt, D), lambda i: (i, 0)), out_shape=_sds((T, D), MM),
        compiler_params=_cp(("parallel",)))(o2, pm, nw)


def post_bwd(o2, pm, nw, dog, dpm, *, bt=512):
    T = pm.shape[0]
    bt = min(bt, T)

    def body(o_ref, z_ref, nw_ref, dog_ref, _alias, do_ref, dz_ref, dnw_ref):
        @pl.when(pl.program_id(0) == 0)
        def _():
            dnw_ref[...] = jnp.zeros_like(dnw_ref)

        _, vjp = jax.vjp(_post_rows, o_ref[0], o_ref[1], z_ref[...], nw_ref[0:1, :])
        doa, _unused, dz, dnw = vjp(dog_ref[...])
        do_ref[...] = doa
        dz_ref[...] = dz
        row8 = lax.broadcasted_iota(jnp.int32, (8, 128), 0)
        dnw_ref[...] += jnp.where(row8 == 0, dnw, 0.0)

    in_specs = [pl.BlockSpec((2, bt, D), lambda i: (0, i, 0)), pl.BlockSpec((bt, D), lambda i: (i, CZ)),
                pl.BlockSpec((8, 128), lambda i: (0, 0)), pl.BlockSpec((bt, D), lambda i: (i, 0)),
                pl.BlockSpec(memory_space=pl.ANY)]
    return pl.pallas_call(
        body, name="post_bwd", grid=(T // bt,), in_specs=in_specs,
        out_specs=[pl.BlockSpec((bt, D), lambda i: (i, 0)), pl.BlockSpec((bt, D), lambda i: (i, CZ)),
                   pl.BlockSpec((8, 128), lambda i: (0, 0))],
        out_shape=[_sds((T, D)), _sds(dpm.shape), _sds((8, 128))], input_output_aliases={4: 1},
        compiler_params=_cp(("arbitrary",)))(o2, pm, nw, dog, dpm)


def merge_fwd(ya, yb, pm, *, bt=512):
    T = pm.shape[0]
    bt = min(bt, T)

    def body(ya_ref, yb_ref, ga_ref, gb_ref, o_ref):
        o_ref[...] = (_sigmoid(ga_ref[...]) * ya_ref[...] + _sigmoid(gb_ref[...]) * yb_ref[...]).astype(o_ref.dtype)

    row = pl.BlockSpec((bt, D), lambda i: (i, 0))
    return pl.pallas_call(
        body, name="merge_fwd", grid=(T // bt,),
        in_specs=[row, row, pl.BlockSpec((bt, D), lambda i: (i, CGA)), pl.BlockSpec((bt, D), lambda i: (i, CGB))],
        out_specs=row, out_shape=_sds((T, D), MM), compiler_params=_cp(("parallel",)))(ya, yb, pm, pm)


def merge_bwd(ya, yb, pm, dmix, *, bt=512):
    T = pm.shape[0]
    bt = min(bt, T)

    def body(ya_ref, yb_ref, ga_ref, gb_ref, dm_ref, dya_ref, dyb_ref, dg_ref):
        dm = dm_ref[...]
        sa, sb = _sigmoid(ga_ref[...]), _sigmoid(gb_ref[...])
        dya_ref[...] = (dm * sa).astype(dya_ref.dtype)
        dyb_ref[...] = (dm * sb).astype(dyb_ref.dtype)
        dg_ref[:, 0:D] = dm * ya_ref[...] * sa * (1.0 - sa)
        dg_ref[:, D:2 * D] = dm * yb_ref[...] * sb * (1.0 - sb)

    row = pl.BlockSpec((bt, D), lambda i: (i, 0))
    return pl.pallas_call(
        body, name="merge_bwd", grid=(T // bt,),
        in_specs=[row, row, pl.BlockSpec((bt, D), lambda i: (i, CGA)), pl.BlockSpec((bt, D), lambda i: (i, CGB)), row],
        out_specs=[row, row, pl.BlockSpec((bt, 2 * D), lambda i: (i, CGA // 2))],
        out_shape=[_sds((T, D), MM), _sds((T, D), MM), _sds((T, NMAIN))],
        compiler_params=_cp(("parallel",)))(ya, yb, pm, pm, dmix)


def _ln_rows(x, y, bias, g, b):
    r = ALPHA * x + y + bias
    mu = jnp.mean(r, axis=-1, keepdims=True)
    var = jnp.mean(jnp.square(r - mu), axis=-1, keepdims=True)
    return (r - mu) * lax.rsqrt(var + LN_EPS) * g + b


def ln_fwd(x, y, p, *, name, bt=512):
    T = x.shape[0]
    bt = min(bt, T)

    def body(x_ref, y_ref, p_ref, o_ref, ob_ref):
        r = _ln_rows(x_ref[...], y_ref[...], p_ref[0:1, :], p_ref[1:2, :], p_ref[2:3, :])
        o_ref[...] = r
        ob_ref[...] = r.astype(ob_ref.dtype)

    row = pl.BlockSpec((bt, D), lambda i: (i, 0))
    return pl.pallas_call(
        body, name=name, grid=(T // bt,), in_specs=[row, row, pl.BlockSpec((8, D), lambda i: (0, 0))],
        out_specs=[row, row], out_shape=[_sds((T, D)), _sds((T, D), MM)],
        compiler_params=_cp(("parallel",)))(x, y, p)


def ln_bwd(x, y, p, ct, ct2=None, *, name, bt=512):
    T = x.shape[0]
    bt = min(bt, T)

    def body(*refs):
        it = iter(refs)
        x_ref, y_ref, p_ref, c_ref = next(it), next(it), next(it), next(it)
        c2_ref = next(it) if ct2 is not None else None
        dxa_ref, dr_ref, dp_ref = next(it), next(it), next(it)

        @pl.when(pl.program_id(0) == 0)
        def _():
            dp_ref[...] = jnp.zeros_like(dp_ref)

        c = c_ref[...]
        if c2_ref is not None:
            c = c + c2_ref[...]
        _, vjp = jax.vjp(_ln_rows, x_ref[...], y_ref[...], p_ref[0:1, :], p_ref[1:2, :], p_ref[2:3, :])
        _dx, dy, dbias, dg, db = vjp(c)
        dxa_ref[...] = ALPHA * dy
        dr_ref[...] = dy.astype(dr_ref.dtype)
        row8 = lax.broadcasted_iota(jnp.int32, (8, D), 0)
        dp_ref[...] += jnp.where(row8 == 0, dbias, jnp.where(row8 == 1, dg, jnp.where(row8 == 2, db, 0.0)))

    row = pl.BlockSpec((bt, D), lambda i: (i, 0))
    in_specs = [row, row, pl.BlockSpec((8, D), lambda i: (0, 0)), row] + ([row] if ct2 is not None else [])
    args = [x, y, p, ct] + ([ct2] if ct2 is not None else [])
    return pl.pallas_call(
        body, name=name, grid=(T // bt,), in_specs=in_specs,
        out_specs=[row, row, pl.BlockSpec((8, D), lambda i: (0, 0))],
        out_shape=[_sds((T, D)), _sds((T, D), MM), _sds((8, D))],
        compiler_params=_cp(("arbitrary",)))(*args)


def loss_fwd_bwd(xl, target, *, bt=512):
    T = xl.shape[0]
    bt = min(bt, T)

    def body(x_ref, t_ref, l_ref, d_ref):
        @pl.when(pl.program_id(0) == 0)
        def _():
            l_ref[...] = jnp.zeros_like(l_ref)

        e = x_ref[...] - t_ref[...]
        d_ref[...] = e * (1.0 / D)
        l_ref[...] += 0.5 * jnp.sum(jnp.mean(e * e, axis=-1, keepdims=True), axis=0, keepdims=True)

    row = pl.BlockSpec((bt, D), lambda i: (i, 0))
    return pl.pallas_call(
        body, name="loss", grid=(T // bt,), in_specs=[row, row],
        out_specs=[pl.BlockSpec((8, 128), lambda i: (0, 0)), row], out_shape=[_sds((8, 128)), _sds((T, D))],
        compiler_params=_cp(("arbitrary",)))(xl, target)


def _row_tile(R, Cc, elems=1 << 18):
    if R * Cc <= elems:
        return R
    tr = 8
    while tr * 2 * Cc <= elems and R % (tr * 2) == 0:
        tr *= 2
    return tr


def adam(w, m, v, ga, gb=None, *, name):
    R, Cc = w.shape
    tr = _row_tile(R, Cc)

    def body(*refs):
        it = iter(refs)
        w_ref, m_ref, v_ref, a_ref = next(it), next(it), next(it), next(it)
        b_ref = next(it) if gb is not None else None
        g_ref, d_ref, mo_ref, vo_ref = next(it), next(it), next(it), next(it)
        g = a_ref[...]
        if b_ref is not None:
            g = g + b_ref[...]
        m2 = B1 * m_ref[...] + (1.0 - B1) * g
        v2 = B2 * v_ref[...] + (1.0 - B2) * jnp.square(g)
        m_hat = m2 / (1.0 - B1 ** STEP)
        v_hat = v2 / (1.0 - B2 ** STEP)
        g_ref[...] = g
        d_ref[...] = -LR * (m_hat / (jnp.sqrt(v_hat) + EPS) + WD * w_ref[...])
        mo_ref[...] = m2
        vo_ref[...] = v2

    blk = pl.BlockSpec((tr, Cc), lambda i: (i, 0))
    args = [w, m, v, ga] + ([gb] if gb is not None else [])
    return pl.pallas_call(
        body, name=name, grid=(R // tr,), in_specs=[blk] * len(args), out_specs=[blk] * 4,
        out_shape=[_sds((R, Cc))] * 4, compiler_params=_cp(("parallel",)))(*args)


def sum4(own, recv, *, name):
    R, Cc = own.shape
    tr = _row_tile(R, Cc)

    def body(o_ref, r_ref, out_ref):
        f = lambda t: t.astype(F32)
        out_ref[...] = ((f(o_ref[...]) + f(r_ref[0])) + f(r_ref[1])) + f(r_ref[2])

    return pl.pallas_call(
        body, name=name, grid=(R // tr,),
        in_specs=[pl.BlockSpec((tr, Cc), lambda i: (i, 0)), pl.BlockSpec((3, tr, Cc), lambda i: (0, i, 0))],
        out_specs=pl.BlockSpec((tr, Cc), lambda i: (i, 0)), out_shape=_sds((R, Cc)),
        compiler_params=_cp(("parallel",)))(own, recv)


def _place():
    return lax.axis_index("x"), lax.axis_index("y"), lax.axis_index("c")


def _other_chips(x, y):
    return [(1 - x, y), (x, 1 - y), (1 - x, 1 - y)]


_ANY = pl.BlockSpec(memory_space=pl.ANY)


def allgather_xy(arrs):
    n = len(arrs)
    halves = [a.shape[0] // 2 for a in arrs]

    def body(*refs):
        ins, outs = refs[:n], refs[n:2 * n]
        send, recv, fsend, frecv, loc = refs[2 * n:]
        x, y, c = _place()
        me = 2 * x + y
        peers = _other_chips(x, y)
        local = [pltpu.make_async_copy(ins[a], outs[a].at[me], loc.at[a]) for a in range(n)]
        for cp in local:
            cp.start()

        def over_ici(a, j, block, src=None):
            px, py = peers[j]
            mine = pl.ds(c * halves[a], halves[a])
            dst = outs[a].at[block, mine]
            return pltpu.make_async_remote_copy(
                src_ref=dst if src is None else src.at[mine], dst_ref=dst, send_sem=send.at[3 * a + j],
                recv_sem=recv.at[3 * a + j], device_id=(px, py, c), device_id_type=MESH)

        def over_d2d(a, j, half):
            px, py = peers[j]
            rows = outs[a].at[2 * px + py, pl.ds(half * halves[a], halves[a])]
            return pltpu.make_async_remote_copy(
                src_ref=rows, dst_ref=rows, send_sem=fsend.at[3 * a + j], recv_sem=frecv.at[3 * a + j],
                device_id=(x, y, 1 - c), device_id_type=MESH)

        sends = [over_ici(a, j, me, src=ins[a]) for a in range(n) for j in range(3)]
        for cp in sends:
            cp.start()
        passed = []
        for a in range(n):
            for j, (px, py) in enumerate(peers):
                over_ici(a, j, 2 * px + py).wait_recv()
                passed.append(over_d2d(a, j, c))
                passed[-1].start()
        for a in range(n):
            for j in range(3):
                over_d2d(a, j, 1 - c).wait_recv()
        for cp in sends + passed:
            cp.wait_send()
        for cp in local:
            cp.wait()

    return pl.pallas_call(
        body, name="allgather_xy", in_specs=[_ANY] * n, out_specs=[_ANY] * n,
        out_shape=[_sds((4,) + a.shape, a.dtype) for a in arrs],
        scratch_shapes=[pltpu.SemaphoreType.DMA((3 * n,))] * 4 + [pltpu.SemaphoreType.DMA((n,))],
        compiler_params=pltpu.CompilerParams(has_side_effects=True))(*arrs)


def scatter_xy(groups):
    L = len(groups[0])
    n = len(groups) * L
    flat = [g for grp in groups for g in grp]

    def body(*refs):
        ins = refs[:n]
        owns, recvs = refs[n:n + len(groups)], refs[n + len(groups):n + 2 * len(groups)]
        send, recv, loc = refs[n + 2 * len(groups):]
        x, y, c = _place()
        me = 2 * x + y
        peers = _other_chips(x, y)
        local, sends = [], []
        for a in range(len(groups)):
            for l in range(L):
                i = a * L + l
                local.append(pltpu.make_async_copy(ins[i].at[me], owns[a].at[l], loc.at[i]))
                for j, (px, py) in enumerate(peers):
                    sends.append(pltpu.make_async_remote_copy(
                        src_ref=ins[i].at[2 * px + py], dst_ref=recvs[a].at[j, l], send_sem=send.at[3 * i + j],
                        recv_sem=recv.at[3 * i + j], device_id=(px, py, c), device_id_type=MESH))
        for cp in local + sends:
            cp.start()
        for cp in sends:
            cp.wait_recv()
        for cp in sends:
            cp.wait_send()
        for cp in local:
            cp.wait()

    out_shape = ([_sds((L,) + grp[0].shape[1:], grp[0].dtype) for grp in groups]
                 + [_sds((3, L) + grp[0].shape[1:], grp[0].dtype) for grp in groups])
    outs = pl.pallas_call(
        body, name="scatter_xy", in_specs=[_ANY] * n, out_specs=[_ANY] * (2 * len(groups)), out_shape=out_shape,
        scratch_shapes=[pltpu.SemaphoreType.DMA((3 * n,)), pltpu.SemaphoreType.DMA((3 * n,)),
                        pltpu.SemaphoreType.DMA((n,))],
        compiler_params=pltpu.CompilerParams(has_side_effects=True))(*flat)
    return outs[:len(groups)], outs[len(groups):]


def swap_c(arrs):
    n = len(arrs)

    def body(*refs):
        ins, outs = refs[:n], refs[n:2 * n]
        send, recv = refs[2 * n:]
        x, y, c = _place()
        cps = [pltpu.make_async_remote_copy(src_ref=ins[a], dst_ref=outs[a], send_sem=send.at[a], recv_sem=recv.at[a],
                                            device_id=(x, y, 1 - c), device_id_type=MESH) for a in range(n)]
        for cp in cps:
            cp.start()
        for cp in cps:
            cp.wait_recv()
        for cp in cps:
            cp.wait_send()

    return pl.pallas_call(
        body, name="swap_c", in_specs=[_ANY] * n, out_specs=[_ANY] * n, out_shape=[_sds(a.shape, a.dtype) for a in arrs],
        scratch_shapes=[pltpu.SemaphoreType.DMA((n,)), pltpu.SemaphoreType.DMA((n,))],
        compiler_params=pltpu.CompilerParams(has_side_effects=True))(*arrs)


def allreduce_small(v):
    R = v.shape[0]

    def body(v_ref, o_ref, buf, send, recv):
        x, y, c = _place()
        me = 4 * x + 2 * y + c
        buf[0] = v_ref[...]

        def cp(k):
            dx, dy, dc = (k >> 2) & 1, (k >> 1) & 1, k & 1
            return pltpu.make_async_remote_copy(
                src_ref=v_ref, dst_ref=buf.at[k], send_sem=send.at[k - 1], recv_sem=recv.at[k - 1],
                device_id=(x ^ dx, y ^ dy, c ^ dc), device_id_type=MESH)

        cps = [cp(k) for k in range(1, 8)]
        for t in cps:
            t.start()
        for t in cps:
            t.wait_recv()
        acc = buf[me]
        for dev in range(1, 8):
            acc = acc + buf[jnp.bitwise_xor(me, dev)]
        o_ref[...] = acc
        for t in cps:
            t.wait_send()

    vm = pl.BlockSpec(memory_space=pltpu.VMEM)
    return pl.pallas_call(
        body, name="allreduce_small", in_specs=[vm], out_specs=vm, out_shape=_sds((R, 128)),
        scratch_shapes=[pltpu.VMEM((8, R, 128), F32), pltpu.SemaphoreType.DMA((7,)), pltpu.SemaphoreType.DMA((7,))],
        compiler_params=pltpu.CompilerParams(has_side_effects=True, vmem_limit_bytes=VMEM_LIMIT))(v)


def _rows8(*rows):
    n = rows[0].shape[-1]
    t = jnp.stack([r.reshape(n).astype(F32) for r in rows])
    return jnp.pad(t, ((0, 8 - len(rows)), (0, 0)))


def _lanes128(a):
    f = a.reshape(-1).astype(F32)
    return jnp.pad(f, (0, 128 - f.shape[0]))


def _layer_fwd(x, xb, W):
    pm = mm_nn(xb, W["w_main"], name="proj_main")
    pab = mm_nn(xb, W["w_ab"], name="proj_ab")
    q, k, v, gb = pre_qkv_fwd(pm, pab, W["cw"], W["gp"])
    sc = pre_sc_fwd(pm, W["csc"])
    u, w, qd, kd, pp, ys, eg = gdn_intra_fwd(q, k, v, gb)
    o2, s0 = gdn_scan_fwd(u, w, qd, kd, pp, eg)
    og = post_fwd(o2, pm, W["nw"])
    ya = mm_nn(og, W["w_og"], name="proj_og")
    yb = mm_nn(sc, W["w_osc"], name="proj_osc")
    mixed = merge_fwd(ya, yb, pm)
    out = mm_nn(mixed, W["w_out"], name="proj_out")
    x1, x1b = ln_fwd(x, out, W["ln1"], name="ln1_fwd")
    hpre, h = mm_nn(x1b, W["w_up"], bias=W["b_up"], relu2=True, name="mlp_up")
    dn = mm_nn(h, W["w_down"], name="mlp_down")
    x2, x2b = ln_fwd(x1, dn, W["ln2"], name="ln2_fwd")
    saved = dict(x=x, xb=xb, pm=pm, pab=pab, q=q, k=k, v=v, gb=gb, sc=sc, o2=o2, s0=s0, og=og, ya=ya, yb=yb,
                 u=u, w=w, qd=qd, kd=kd, pp=pp, ys=ys, eg=eg,
                 mixed=mixed, out=out, x1=x1, x1b=x1b, hpre=hpre, h=h, dn=dn)
    return x2, x2b, saved


def _layer_bwd(ct, W, S):
    dxa2, dr2b, dp2 = ln_bwd(S["x1"], S["dn"], W["ln2"], ct, name="ln2_bwd")
    g_down = mm_tn(S["h"], dr2b, name="dw_down")
    dhpre, db_up = mm_nt(dr2b, W["w_down"], dact=S["hpre"], out_dtype=MM, name="mlp_down_bwd")
    g_up = mm_tn(S["x1b"], dhpre, name="dw_up")
    dx1 = mm_nt(dhpre, W["w_up"], add=dxa2, name="mlp_up_bwd")
    dxa1, dr1b, dp1 = ln_bwd(S["x"], S["out"], W["ln1"], dx1, name="ln1_bwd")
    g_out = mm_tn(S["mixed"], dr1b, name="dw_out")
    dmix = mm_nt(dr1b, W["w_out"], name="proj_out_bwd")
    dya, dyb, dpm = merge_bwd(S["ya"], S["yb"], S["pm"], dmix)
    g_og = mm_tn(S["og"], dya, name="dw_og")
    g_osc = mm_tn(S["sc"], dyb, name="dw_osc")
    dog = mm_nt(dya, W["w_og"], name="proj_og_bwd")
    dsc = mm_nt(dyb, W["w_osc"], name="proj_osc_bwd")
    do, dpm, dnw = post_bwd(S["o2"], S["pm"], W["nw"], dog, dpm)
    du, dw, dqd, dkd, dpp, deg = gdn_scan_bwd(S["u"], S["w"], S["qd"], S["kd"], S["pp"], S["eg"], S["s0"], do)
    dq2, dk2, dv2, dgb2 = gdn_intra_bwd(S["q"], S["k"], S["v"], S["gb"], S["ys"], du, dw, dqd, dkd, dpp, deg)
    dpm, dpab, dcw, dgp = pre_qkv_bwd(S["pm"], S["pab"], W["cw"], W["gp"], dq2, dk2, dv2, dgb2, dpm)
    dpm, dcsc = pre_sc_bwd(S["pm"], W["csc"], dsc, dpm)
    g_main = mm_tn(S["xb"], dpm, name="dw_main")
    g_ab = mm_tn(S["xb"], dpab, name="dw_ab")
    t = mm_nt(dpab, W["w_ab"], add=dxa1, name="proj_ab_bwd")
    dx = mm_nt(dpm, W["w_main"], add=t, name="proj_main_bwd")
    g_in = jnp.concatenate([g_main[:, :3 * D], g_main[:, 8 * D:], g_ab[:, :4 * NH], g_main[:, 3 * D:8 * D]], axis=1)
    grads = dict(
        w_in=g_in, w_o_gdn=g_og, w_o_sc=g_osc, w_out=g_out, w_up=g_up, w_down=g_down,
        conv_qkv=dcw[:3], conv_sc=dcsc[:3], a_log=dgp[0, :2 * NH].reshape(2, NH), dt_bias=dgp[1, :2 * NH].reshape(2, NH),
        gdn_norm_w=dnw[0], ln1_g=dp1[1], ln1_b=dp1[2], b_up=db_up[0], b_down=dp2[0], ln2_g=dp2[1], ln2_b=dp2[2])
    return dx, grads


def _layer_weights(l, full, a_log, dt_bias, gdn_norm_w, ln1_g, ln1_b, b_up, b_down, ln2_g, ln2_b):
    w_in = full["w_in"][l]
    w_main = jnp.concatenate([w_in[:, :3 * D], w_in[:, 4 * D + 4 * NH:], w_in[:, 3 * D:4 * D]], axis=1)
    w_ab = jnp.pad(w_in[:, 4 * D:4 * D + 4 * NH], ((0, 0), (0, 128 - 4 * NH)))
    return dict(
        w_main=w_main, w_ab=w_ab, w_og=full["w_o_gdn"][l], w_osc=full["w_o_sc"][l], w_out=full["w_out"][l],
        w_up=full["w_up"][l], w_down=full["w_down"][l],
        cw=jnp.pad(full["conv_qkv"][l].astype(F32), ((0, 5), (0, 0))),
        csc=jnp.pad(full["conv_sc"][l].astype(F32), ((0, 5), (0, 0))),
        gp=_rows8(_lanes128(a_log[l]), _lanes128(dt_bias[l])), nw=_rows8(gdn_norm_w[l]),
        ln1=_rows8(jnp.zeros((D,), F32), ln1_g[l], ln1_b[l]), ln2=_rows8(b_down[l], ln2_g[l], ln2_b[l]),
        b_up=b_up[l].reshape(1, DFF).astype(F32))


def local_step(xs, target, full, a_log, dt_bias, gdn_norm_w, ln1_g, ln1_b, b_up, b_down, ln2_g, ln2_b):
    Ws = [_layer_weights(l, full, a_log, dt_bias, gdn_norm_w, ln1_g, ln1_b, b_up, b_down, ln2_g, ln2_b)
          for l in range(DEPTH)]
    x, xb = xs, xs.astype(MM)
    saved = []
    for l in range(DEPTH):
        x, xb, S = _layer_fwd(x, xb, Ws[l])
        saved.append(S)
    loss_tile, ct = loss_fwd_bwd(x, target)
    grads = [None] * DEPTH
    for l in reversed(range(DEPTH)):
        ct, grads[l] = _layer_bwd(ct, Ws[l], saved[l])
    return loss_tile, ct, grads


BIG = ("w_in", "w_o_gdn", "w_o_sc", "w_out", "w_up", "w_down")
SMALL = ("conv_qkv", "a_log", "dt_bias", "gdn_norm_w", "conv_sc", "ln1_g", "ln1_b", "b_up", "b_down", "ln2_g", "ln2_b")
ORDER = ("w_in", "conv_qkv", "a_log", "dt_bias", "gdn_norm_w", "w_o_gdn", "conv_sc", "w_o_sc", "w_out", "ln1_g",
         "ln1_b", "w_up", "b_up", "w_down", "b_down", "ln2_g", "ln2_b")


def _pack(arrs):
    flat = jnp.concatenate([a.reshape(-1).astype(F32) for a in arrs])
    n = flat.shape[0]
    rows = -(-n // 1024) * 8
    return jnp.pad(flat, (0, rows * 128 - n)).reshape(rows, 128)


def _unpack(buf, like):
    flat = buf.reshape(-1)
    out, o = [], 0
    for a in like:
        n = 1
        for s in a.shape:
            n *= s
        out.append(flat[o:o + n].reshape(a.shape))
        o += n
    return out


def _gathered(name, g):
    if name in ("w_in", "w_up", "conv_qkv", "conv_sc"):
        t = jnp.moveaxis(g, 0, -2)
        return t.reshape(t.shape[:-2] + (t.shape[-2] * t.shape[-1],))
    t = jnp.moveaxis(g, 0, 1)
    return t.reshape((t.shape[0], t.shape[1] * t.shape[2]) + t.shape[3:])


def _by_chip(name, g):
    if name in ("w_in", "w_up"):
        r, ccols = g.shape
        return jnp.moveaxis(g.reshape(r, 4, ccols // 4), 1, 0)
    return g.reshape((4, g.shape[0] // 4) + g.shape[1:])


def kernel(x, w_in, conv_qkv, a_log, dt_bias, gdn_norm_w, w_o_gdn, conv_sc, w_o_sc, w_out, ln1_g, ln1_b, w_up, b_up, w_down, b_down, ln2_g, ln2_b, loss_target, m_w_in, m_conv_qkv, m_a_log, m_dt_bias, m_gdn_norm_w, m_w_o_gdn, m_conv_sc, m_w_o_sc, m_w_out, m_ln1_g, m_ln1_b, m_w_up, m_b_up, m_w_down, m_b_down, m_ln2_g, m_ln2_b, v_w_in, v_conv_qkv, v_a_log, v_dt_bias, v_gdn_norm_w, v_w_o_gdn, v_conv_sc, v_w_o_sc, v_w_out, v_ln1_g, v_ln1_b, v_w_up, v_b_up, v_w_down, v_b_down, v_ln2_g, v_ln2_b):
    w = dict(w_in=w_in, conv_qkv=conv_qkv, a_log=a_log, dt_bias=dt_bias, gdn_norm_w=gdn_norm_w, w_o_gdn=w_o_gdn,
             conv_sc=conv_sc, w_o_sc=w_o_sc, w_out=w_out, ln1_g=ln1_g, ln1_b=ln1_b, w_up=w_up, b_up=b_up,
             w_down=w_down, b_down=b_down, ln2_g=ln2_g, ln2_b=ln2_b)
    m = dict(w_in=m_w_in, conv_qkv=m_conv_qkv, a_log=m_a_log, dt_bias=m_dt_bias, gdn_norm_w=m_gdn_norm_w,
             w_o_gdn=m_w_o_gdn, conv_sc=m_conv_sc, w_o_sc=m_w_o_sc, w_out=m_w_out, ln1_g=m_ln1_g, ln1_b=m_ln1_b,
             w_up=m_w_up, b_up=m_b_up, w_down=m_w_down, b_down=m_b_down, ln2_g=m_ln2_g, ln2_b=m_ln2_b)
    v = dict(w_in=v_w_in, conv_qkv=v_conv_qkv, a_log=v_a_log, dt_bias=v_dt_bias, gdn_norm_w=v_gdn_norm_w,
             w_o_gdn=v_w_o_gdn, conv_sc=v_conv_sc, w_o_sc=v_w_o_sc, w_out=v_w_out, ln1_g=v_ln1_g, ln1_b=v_ln1_b,
             w_up=v_w_up, b_up=v_b_up, w_down=v_w_down, b_down=v_b_down, ln2_g=v_ln2_g, ln2_b=v_ln2_b)
    chip = 2 * lax.axis_index("x") + lax.axis_index("y")

    names = BIG + ("conv_qkv", "conv_sc")
    got = allgather_xy([w[n].astype(MM) if n in BIG else w[n] for n in names])
    full = {n: _gathered(n, g) for n, g in zip(names, got)}

    loss_tile, dx, grads = local_step(x[0], loss_target[0], full, a_log, dt_bias, gdn_norm_w, ln1_g, ln1_b, b_up,
                                      b_down, ln2_g, ln2_b)
    loss = lax.psum(loss_tile[0, 0], ("x", "y", "c"))

    groups = [[_by_chip(n, grads[l][n]).astype(MM) for l in range(DEPTH)] for n in BIG]
    owns, recvs = scatter_xy(groups)
    part = []
    for n, own, rec in zip(BIG, owns, recvs):
        cols = own.shape[-1]
        part.append(sum4(own.reshape(-1, cols), rec.reshape(3, -1, cols), name="sum_" + n))
    other = swap_c(part)
    out = {}
    for n, mine, theirs in zip(BIG, part, other):
        cols = mine.shape[-1]
        res = adam(w[n].reshape(-1, cols), m[n].reshape(-1, cols), v[n].reshape(-1, cols), mine, theirs, name="adam_" + n)
        out[n] = [r.reshape(w[n].shape) for r in res]

    stacked = [jnp.stack([grads[l][n] for l in range(DEPTH)]) for n in SMALL]
    summed = _unpack(allreduce_small(_pack(stacked)), stacked)
    gs = []
    for n, g in zip(SMALL, summed):
        if n in ("conv_qkv", "conv_sc"):
            blk = w[n].shape[-1]
            g = lax.dynamic_slice_in_dim(g, chip * blk, blk, axis=2)
        gs.append(g)
    res = adam(_pack([w[n] for n in SMALL]), _pack([m[n] for n in SMALL]), _pack([v[n] for n in SMALL]), _pack(gs),
               name="adam_small")
    for n, parts in zip(SMALL, zip(*[_unpack(r, gs) for r in res])):
        out[n] = list(parts)

    outs = [loss, dx[None]]
    for kind in range(4):
        outs += [out[n][kind] for n in ORDER]
    return tuple(outs)
```

```python
import functools

import jax
import jax.numpy as jnp
from jax import lax
from jax.experimental import pallas as pl
from jax.experimental.pallas import tpu as pltpu

F32 = jnp.float32
MM = jnp.bfloat16
HI = lax.Precision.HIGHEST

D = 1024
NH = 8
HD = 128
CH = 64
DFF = 4 * D
DEPTH = 4
LN_EPS = 1e-5
RMS_EPS = 1e-6
L2_EPS = 1e-6
ALPHA = (2 * DEPTH) ** 0.25
LR, B1, B2, EPS, WD, STEP = 0.001, 0.9, 0.999, 1e-08, 0.01, 10

NMAIN = 9 * D
CQ, CK, CV, CSB, CSC, CSX, CGA, CGB, CZ = range(9)
HALO = 8
VMEM_LIMIT = 56 * 1024 * 1024
MESH = pl.DeviceIdType.MESH


def _cp(sem=None, vmem=VMEM_LIMIT):
    return pltpu.CompilerParams(dimension_semantics=sem, vmem_limit_bytes=vmem)


def _sds(shape, dtype=F32):
    return jax.ShapeDtypeStruct(tuple(shape), dtype)


def _accumulate(acc, prod, k, nk, finish):
    if nk == 1:
        finish(prod)
        return

    @pl.when(k == 0)
    def _():
        acc[...] = jnp.zeros_like(acc)

    acc[...] += prod

    @pl.when(k == nk - 1)
    def _():
        finish(acc[...])


def mm_nn(a, b, *, name, bias=None, relu2=False, add=None, out_dtype=F32, tm=1024, tn=1024, tk=1024):
    M, K = a.shape
    N = b.shape[1]
    tm, tn, tk = min(tm, M), min(tn, N), min(tk, K)
    nk = K // tk

    def body(*refs):
        it = iter(refs)
        a_ref, b_ref = next(it), next(it)
        bias_ref = next(it) if bias is not None else None
        add_ref = next(it) if add is not None else None
        o_ref = next(it)
        h_ref = next(it) if relu2 else None
        acc = next(it) if nk > 1 else None
        prod = jnp.dot(a_ref[...].astype(MM), b_ref[...].astype(MM), preferred_element_type=F32)

        def finish(r):
            if bias_ref is not None:
                r = r + bias_ref[...]
            if add_ref is not None:
                r = r + add_ref[...]
            o_ref[...] = r.astype(o_ref.dtype)
            if relu2:
                t = jnp.maximum(r, 0.0)
                h_ref[...] = (t * t).astype(h_ref.dtype)

        _accumulate(acc, prod, pl.program_id(2), nk, finish)

    in_specs = [pl.BlockSpec((tm, tk), lambda i, j, k: (i, k)), pl.BlockSpec((tk, tn), lambda i, j, k: (k, j))]
    args = [a, b]
    if bias is not None:
        in_specs.append(pl.BlockSpec((1, tn), lambda i, j, k: (0, j)))
        args.append(bias)
    if add is not None:
        in_specs.append(pl.BlockSpec((tm, tn), lambda i, j, k: (i, j)))
        args.append(add)
    out_shape = [_sds((M, N), out_dtype)]
    out_specs = [pl.BlockSpec((tm, tn), lambda i, j, k: (i, j))]
    if relu2:
        out_shape.append(_sds((M, N), MM))
        out_specs.append(pl.BlockSpec((tm, tn), lambda i, j, k: (i, j)))
    res = pl.pallas_call(
        body, name=name, grid=(M // tm, N // tn, nk), in_specs=in_specs, out_specs=out_specs, out_shape=out_shape,
        scratch_shapes=[pltpu.VMEM((tm, tn), F32)] if nk > 1 else [],
        compiler_params=_cp(("parallel", "parallel", "arbitrary")))(*args)
    return res if relu2 else res[0]


def mm_nt(a, b, *, name, add=None, dact=None, out_dtype=F32, tm=1024, tn=1024, tk=1024):
    M, Nc = a.shape
    Ko = b.shape[0]
    tm, tn, tk = min(tm, M), min(tn, Ko), min(tk, Nc)
    nk = Nc // tk
    ni = M // tm

    def body(*refs):
        it = iter(refs)
        a_ref, b_ref = next(it), next(it)
        add_ref = next(it) if add is not None else None
        d_ref = next(it) if dact is not None else None
        o_ref = next(it)
        db_ref = next(it) if dact is not None else None
        acc = next(it) if nk > 1 else None
        i = pl.program_id(1)
        prod = lax.dot_general(a_ref[...].astype(MM), b_ref[...].astype(MM), (((1,), (1,)), ((), ())),
                               preferred_element_type=F32)

        def finish(r):
            if add_ref is not None:
                r = r + add_ref[...]
            if d_ref is not None:
                r = r * (2.0 * jnp.maximum(d_ref[...], 0.0))
                s = jnp.sum(r, axis=0, keepdims=True)
                row0 = lax.broadcasted_iota(jnp.int32, db_ref.shape, 0) == 0

                @pl.when(i == 0)
                def _():
                    db_ref[...] = jnp.zeros_like(db_ref)

                db_ref[...] += jnp.where(row0, s, 0.0)
            o_ref[...] = r.astype(o_ref.dtype)

        _accumulate(acc, prod, pl.program_id(2), nk, finish)

    in_specs = [pl.BlockSpec((tm, tk), lambda j, i, k: (i, k)), pl.BlockSpec((tn, tk), lambda j, i, k: (j, k))]
    args = [a, b]
    for extra in (add, dact):
        if extra is not None:
            in_specs.append(pl.BlockSpec((tm, tn), lambda j, i, k: (i, j)))
            args.append(extra)
    out_shape = [_sds((M, Ko), out_dtype)]
    out_specs = [pl.BlockSpec((tm, tn), lambda j, i, k: (i, j))]
    if dact is not None:
        out_shape.append(_sds((8, Ko), F32))
        out_specs.append(pl.BlockSpec((8, tn), lambda j, i, k: (0, j)))
    res = pl.pallas_call(
        body, name=name, grid=(Ko // tn, ni, nk), in_specs=in_specs, out_specs=out_specs, out_shape=out_shape,
        scratch_shapes=[pltpu.VMEM((tm, tn), F32)] if nk > 1 else [],
        compiler_params=_cp(("parallel", "arbitrary", "arbitrary")))(*args)
    return res if dact is not None else res[0]


def mm_tn(a, b, *, name, tm=1024, tn=1024, tk=1024):
    T, M = a.shape
    N = b.shape[1]
    tm, tn, tk = min(tm, M), min(tn, N), min(tk, T)

    def body(a_ref, b_ref, o_ref):
        @pl.when(pl.program_id(2) == 0)
        def _():
            o_ref[...] = jnp.zeros_like(o_ref)

        o_ref[...] += lax.dot_general(a_ref[...].astype(MM), b_ref[...].astype(MM), (((0,), (0,)), ((), ())),
                                      preferred_element_type=F32)

    return pl.pallas_call(
        body, name=name, grid=(M // tm, N // tn, T // tk),
        in_specs=[pl.BlockSpec((tk, tm), lambda i, j, k: (k, i)), pl.BlockSpec((tk, tn), lambda i, j, k: (k, j))],
        out_specs=pl.BlockSpec((tm, tn), lambda i, j, k: (i, j)), out_shape=_sds((M, N)),
        compiler_params=_cp(("parallel", "parallel", "arbitrary")))(a, b)


def _sigmoid(x):
    return 1.0 / (1.0 + jnp.exp(-x))


def _silu(x):
    return x * _sigmoid(x)


def _softplus(x):
    return jnp.maximum(x, 0.0) + jnp.log1p(jnp.exp(-jnp.abs(x)))


def _ext(main_ref, prev_ref, next_ref, first, last):
    p = jnp.where(first, 0.0, prev_ref[...].astype(F32))
    n = jnp.where(last, 0.0, next_ref[...].astype(F32))
    return jnp.concatenate([p, main_ref[...].astype(F32), n], axis=0)


def _shift_dn(x):
    return pltpu.roll(x, 1, 0)


def _shift_up(x):
    return pltpu.roll(x, x.shape[0] - 1, 0)


def _conv3(xe, w):
    return w[0:1, :] * _shift_dn(xe) + w[1:2, :] * xe + w[2:3, :] * _shift_up(xe)


def _conv3_t(de, w):
    return w[0:1, :] * _shift_up(de) + w[1:2, :] * de + w[2:3, :] * _shift_dn(de)


def _halo_specs(bt, T, col, lead=None):
    r = bt // HALO
    last = T // HALO - 1
    if lead is None:
        return [pl.BlockSpec((bt, D), lambda i: (i, col)),
                pl.BlockSpec((HALO, D), lambda i: (jnp.maximum(i * r - 1, 0), col)),
                pl.BlockSpec((HALO, D), lambda i: (jnp.minimum((i + 1) * r, last), col))]
    return [pl.BlockSpec((lead, bt, D), lambda i: (0, i, col)),
            pl.BlockSpec((lead, HALO, D), lambda i: (0, jnp.maximum(i * r - 1, 0), col)),
            pl.BlockSpec((lead, HALO, D), lambda i: (0, jnp.minimum((i + 1) * r, last), col))]


def _qkv_rows(cq, ck, cv):
    sq, sk, sv = _silu(cq), _silu(ck), _silu(cv)
    qs, ks = [], []
    for h in range(NH):
        s = slice(h * HD, (h + 1) * HD)
        qh, kh = sq[:, s], sk[:, s]
        qs.append(qh * lax.rsqrt(jnp.sum(qh * qh, axis=-1, keepdims=True) + L2_EPS) * (HD ** -0.5))
        ks.append(kh * lax.rsqrt(jnp.sum(kh * kh, axis=-1, keepdims=True) + L2_EPS))
    return jnp.concatenate(qs, axis=1), jnp.concatenate(ks, axis=1), sv


def _chunk_masks(bt):
    row = lax.broadcasted_iota(jnp.int32, (bt, bt), 0)
    col = lax.broadcasted_iota(jnp.int32, (bt, bt), 1)
    same = (row // CH) == (col // CH)
    lower = jnp.where(same & (col <= row), 1.0, 0.0).astype(F32)
    upper = jnp.where(same & (col >= row), 1.0, 0.0).astype(F32)
    return lower, upper


def _gate_rows(ab, gp, lower, upper):
    lane = lax.broadcasted_iota(jnp.int32, ab.shape, 1)
    g = -jnp.exp(gp[0:1, :]) * _softplus(ab + gp[1:2, :])
    g = jnp.where(lane < 2 * NH, g, 0.0)
    gf = jnp.dot(lower, g, precision=HI, preferred_element_type=F32)
    gr = jnp.dot(upper, g, precision=HI, preferred_element_type=F32)
    gc = jnp.where(lane < NH, gf, gr)
    beta = _sigmoid(ab)
    return jnp.where(lane < 2 * NH, gc, jnp.where(lane < 4 * NH, beta, 0.0))


def pre_qkv_fwd(pm, pab, cw, gp, *, bt=256):
    T = pm.shape[0]
    bt = min(bt, T)
    n = T // bt

    def body(q0, q1, q2, k0, k1, k2, v0, v1, v2, ab_ref, cw_ref, gp_ref, q_ref, k_ref, v_ref, gb_ref):
        i = pl.program_id(0)
        first, last = i == 0, i == n - 1
        cs = []
        for c, (m, p, x) in enumerate(((q0, q1, q2), (k0, k1, k2), (v0, v1, v2))):
            xe = _ext(m, p, x, first, last)
            cs.append(_conv3(xe, cw_ref[:, c * D:(c + 1) * D])[HALO:HALO + bt])
        q, k, v = _qkv_rows(*cs)
        q_ref[...], k_ref[...], v_ref[...] = q, k, v
        lower, upper = _chunk_masks(bt)
        gb_ref[...] = _gate_rows(ab_ref[...], gp_ref[...], lower, upper)

    in_specs = (_halo_specs(bt, T, CQ) + _halo_specs(bt, T, CK) + _halo_specs(bt, T, CV)
                + [pl.BlockSpec((bt, 128), lambda i: (i, 0)), pl.BlockSpec((8, 3 * D), lambda i: (0, 0)),
                   pl.BlockSpec((8, 128), lambda i: (0, 0))])
    row = pl.BlockSpec((bt, D), lambda i: (i, 0))
    return pl.pallas_call(
        body, name="pre_qkv_fwd", grid=(n,), in_specs=in_specs,
        out_specs=[row, row, row, pl.BlockSpec((bt, 128), lambda i: (i, 0))],
        out_shape=[_sds((T, D)), _sds((T, D)), _sds((T, D)), _sds((T, 128))],
        compiler_params=_cp(("parallel",)))(*([pm] * 9), pab, cw, gp)


def pre_qkv_bwd(pm, pab, cw, gp, dq2, dk2, dv2, dgb2, dpm, *, bt=128):
    T = pm.shape[0]
    bt = min(bt, T)
    n = T // bt
    E = bt + 2 * HALO

    def body(*refs):
        it = iter(refs)
        xs = [[next(it) for _ in range(3)] for _ in range(3)]
        ds = [[next(it) for _ in range(3)] for _ in range(3)]
        ab_ref, dgb_ref, cw_ref, gp_ref, _alias = next(it), next(it), next(it), next(it), next(it)
        o_ref, dab_ref, dcw_ref, dgp_ref = (next(it) for _ in range(4))
        i = pl.program_id(0)
        first, last = i == 0, i == n - 1

        @pl.when(first)
        def _():
            dcw_ref[...] = jnp.zeros_like(dcw_ref)
            dgp_ref[...] = jnp.zeros_like(dgp_ref)

        xes = [_ext(*xs[c], first, last) for c in range(3)]
        ces = [_conv3(xes[c], cw_ref[:, c * D:(c + 1) * D]) for c in range(3)]
        cts = []
        for c in range(3):
            m, p, x = ds[c]
            pe = jnp.where(first, 0.0, p[0] + p[1])
            ne = jnp.where(last, 0.0, x[0] + x[1])
            cts.append(jnp.concatenate([pe, m[0] + m[1], ne], axis=0))
        _, vjp = jax.vjp(_qkv_rows, *ces)
        dces = vjp(tuple(cts))
        rowi = lax.broadcasted_iota(jnp.int32, (E, 1), 0)
        central = (rowi >= HALO) & (rowi < HALO + bt)
        row8 = lax.broadcasted_iota(jnp.int32, (8, D), 0)
        for c in range(3):
            w = cw_ref[:, c * D:(c + 1) * D]
            o_ref[:, c * D:(c + 1) * D] = _conv3_t(dces[c], w)[HALO:HALO + bt]
            dc = jnp.where(central, dces[c], 0.0)
            taps = (jnp.sum(dc * _shift_dn(xes[c]), axis=0, keepdims=True),
                    jnp.sum(dc * xes[c], axis=0, keepdims=True),
                    jnp.sum(dc * _shift_up(xes[c]), axis=0, keepdims=True))
            upd = jnp.where(row8 == 0, taps[0], jnp.where(row8 == 1, taps[1], jnp.where(row8 == 2, taps[2], 0.0)))
            dcw_ref[:, c * D:(c + 1) * D] += upd
        lower, upper = _chunk_masks(bt)
        _, gvjp = jax.vjp(lambda ab, gp: _gate_rows(ab, gp, lower, upper), ab_ref[...], gp_ref[...])
        dab, dgp = gvjp(dgb_ref[0] + dgb_ref[1])
        dab_ref[...] = dab
        dgp_ref[...] += dgp

    in_specs = (_halo_specs(bt, T, CQ) + _halo_specs(bt, T, CK) + _halo_specs(bt, T, CV)
                + _halo_specs(bt, T, 0, lead=2) * 3
                + [pl.BlockSpec((bt, 128), lambda i: (i, 0)), pl.BlockSpec((2, bt, 128), lambda i: (0, i, 0)),
                   pl.BlockSpec((8, 3 * D), lambda i: (0, 0)), pl.BlockSpec((8, 128), lambda i: (0, 0)),
                   pl.BlockSpec(memory_space=pl.ANY)])
    out_specs = [pl.BlockSpec((bt, 3 * D), lambda i: (i, 0)), pl.BlockSpec((bt, 128), lambda i: (i, 0)),
                 pl.BlockSpec((8, 3 * D), lambda i: (0, 0)), pl.BlockSpec((8, 128), lambda i: (0, 0))]
    return pl.pallas_call(
        body, name="pre_qkv_bwd", grid=(n,), in_specs=in_specs, out_specs=out_specs,
        out_shape=[_sds(dpm.shape), _sds((T, 128)), _sds((8, 3 * D)), _sds((8, 128))],
        input_output_aliases={len(in_specs) - 1: 0},
        compiler_params=_cp(("arbitrary",)))(
            *([pm] * 9), dq2, dq2, dq2, dk2, dk2, dk2, dv2, dv2, dv2, pab, dgb2, cw, gp, dpm)


def pre_sc_fwd(pm, cw, *, bt=256):
    T = pm.shape[0]
    bt = min(bt, T)
    n = T // bt

    def body(b_ref, c0, c1, c2, x0, x1, x2, cw_ref, o_ref):
        i = pl.program_id(0)
        first, last = i == 0, i == n - 1
        pe = _ext(c0, c1, c2, first, last) * _ext(x0, x1, x2, first, last)
        o_ref[...] = (b_ref[...] * _conv3(pe, cw_ref[...])[HALO:HALO + bt]).astype(o_ref.dtype)

    in_specs = ([pl.BlockSpec((bt, D), lambda i: (i, CSB))] + _halo_specs(bt, T, CSC) + _halo_specs(bt, T, CSX)
                + [pl.BlockSpec((8, D), lambda i: (0, 0))])
    return pl.pallas_call(
        body, name="pre_sc_fwd", grid=(n,), in_specs=in_specs, out_specs=pl.BlockSpec((bt, D), lambda i: (i, 0)),
        out_shape=_sds((T, D), MM), compiler_params=_cp(("parallel",)))(*([pm] * 7), cw)


def pre_sc_bwd(pm, cw, dsc, dpm, *, bt=256):
    T = pm.shape[0]
    bt = min(bt, T)
    n = T // bt
    E = bt + 2 * HALO

    def body(b0, b1, b2, c0, c1, c2, x0, x1, x2, d0, d1, d2, cw_ref, _alias, o_ref, dcw_ref):
        i = pl.program_id(0)
        first, last = i == 0, i == n - 1

        @pl.when(first)
        def _():
            dcw_ref[...] = jnp.zeros_like(dcw_ref)

        ce, xe = _ext(c0, c1, c2, first, last), _ext(x0, x1, x2, first, last)
        pe = ce * xe
        w = cw_ref[...]
        dout = d0[...]
        o_ref[:, 0:D] = dout * _conv3(pe, w)[HALO:HALO + bt]
        dce = _ext(d0, d1, d2, first, last) * _ext(b0, b1, b2, first, last)
        dp = _conv3_t(dce, w)[HALO:HALO + bt]
        o_ref[:, D:2 * D] = dp * x0[...]
        o_ref[:, 2 * D:3 * D] = dp * c0[...]
        rowi = lax.broadcasted_iota(jnp.int32, (E, 1), 0)
        dc = jnp.where((rowi >= HALO) & (rowi < HALO + bt), dce, 0.0)
        row8 = lax.broadcasted_iota(jnp.int32, (8, D), 0)
        taps = (jnp.sum(dc * _shift_dn(pe), axis=0, keepdims=True), jnp.sum(dc * pe, axis=0, keepdims=True),
                jnp.sum(dc * _shift_up(pe), axis=0, keepdims=True))
        dcw_ref[...] += jnp.where(row8 == 0, taps[0], jnp.where(row8 == 1, taps[1], jnp.where(row8 == 2, taps[2], 0.0)))

    dsc_specs = [pl.BlockSpec((bt, D), lambda i: (i, 0)),
                 pl.BlockSpec((HALO, D), lambda i: (jnp.maximum(i * (bt // HALO) - 1, 0), 0)),
                 pl.BlockSpec((HALO, D), lambda i: (jnp.minimum((i + 1) * (bt // HALO), T // HALO - 1), 0))]
    in_specs = (_halo_specs(bt, T, CSB) + _halo_specs(bt, T, CSC) + _halo_specs(bt, T, CSX) + dsc_specs
                + [pl.BlockSpec((8, D), lambda i: (0, 0)), pl.BlockSpec(memory_space=pl.ANY)])
    return pl.pallas_call(
        body, name="pre_sc_bwd", grid=(n,), in_specs=in_specs,
        out_specs=[pl.BlockSpec((bt, 3 * D), lambda i: (i, 1)), pl.BlockSpec((8, D), lambda i: (0, 0))],
        out_shape=[_sds(dpm.shape), _sds((8, D))], input_output_aliases={len(in_specs) - 1: 0},
        compiler_params=_cp(("arbitrary",)))(*([pm] * 9), dsc, dsc, dsc, cw, dpm)


def _bdot(a, b, dims):
    return lax.dot_general(a.astype(MM), b.astype(MM), (dims, ((), ())), preferred_element_type=F32)


_NN, _NT, _TN = ((1,), (0,)), ((1,), (1,)), ((0,), (0,))


def _raw_nn(a, b):
    return _bdot(a, b, _NN)


def _raw_nt(a, b):
    return _bdot(a, b, _NT)


def _raw_tn(a, b):
    return _bdot(a, b, _TN)


def _make_vjp_ops():
    @jax.custom_vjp
    def nn(a, b):
        return _raw_nn(a, b)

    @jax.custom_vjp
    def nt(a, b):
        return _raw_nt(a, b)

    @jax.custom_vjp
    def tn(a, b):
        return _raw_tn(a, b)

    nn.defvjp(lambda a, b: (_raw_nn(a, b), (a, b)), lambda r, g: (_raw_nt(g, r[1]), _raw_tn(r[0], g)))
    nt.defvjp(lambda a, b: (_raw_nt(a, b), (a, b)), lambda r, g: (_raw_nn(g, r[1]), _raw_tn(g, r[0])))
    tn.defvjp(lambda a, b: (_raw_tn(a, b), (a, b)), lambda r, g: (_raw_nt(r[1], g), _raw_nn(r[0], g)))

    @jax.custom_vjp
    def inv_saved(A, Y):
        return Y

    def inv_bwd(Y, g):
        M = g + _raw_tn(Y, g)
        return -(M + _raw_nt(M, Y)), jnp.zeros_like(Y)

    inv_saved.defvjp(lambda A, Y: (Y, Y), inv_bwd)
    return nn, nt, tn, inv_saved


GH = 4
GR = GH * CH
NG = NH // GH
CBI = 2


def _scan_chunks(n_chunks):
    return 4 if n_chunks % 4 == 0 else 2


def _tri_inv_y_all(As):
    Ys = [-A for A in As]
    Ps = [_raw_nn(A, A) for A in As]
    for stage in range(5):
        squares = [_raw_nn(P, P) for P in Ps] if stage < 4 else None
        Ys = [Y + P + _raw_nn(Y, P) for Y, P in zip(Ys, Ps)]
        Ps = squares
    return Ys


def _intra_groups(chains, incl, strict, eye, lastc, ops, inv_all):
    nn, nt, tn = ops
    st = []
    for qs, ks, vs, gcols, bcols in chains:
        q, k, v = (jnp.concatenate(t, axis=0) for t in (qs, ks, vs))
        gcol, bcol = jnp.concatenate(gcols, axis=0), jnp.concatenate(bcols, axis=0)
        grow = jnp.sum(eye * gcol, axis=0, keepdims=True)
        gam = jnp.where(incl, jnp.exp(jnp.where(incl, gcol - grow, 0.0)), 0.0)
        glast = jnp.sum(jnp.where(lastc, grow, 0.0), axis=1, keepdims=True)
        st.append((q, k, v, gcol, bcol, gam, glast, jnp.exp(gcol), k * bcol))
    As = [jnp.where(strict, nt(kb, k) * gam, 0.0) for (q, k, v, gcol, bcol, gam, glast, eg, kb) in st]
    Ys = inv_all(As)
    vbs = [v * bcol for (q, k, v, gcol, bcol, gam, glast, eg, kb) in st]
    kgs = [kb * eg for (q, k, v, gcol, bcol, gam, glast, eg, kb) in st]
    us = [vb + nn(Y, vb) for Y, vb in zip(Ys, vbs)]
    ws = [kg + nn(Y, kg) for Y, kg in zip(Ys, kgs)]
    Ps = [nt(q, k) * gam for (q, k, v, gcol, bcol, gam, glast, eg, kb) in st]
    return [((u, w, P, q * eg, k * jnp.exp(glast - gcol), jnp.exp(glast)), Y)
            for u, w, P, Y, (q, k, v, gcol, bcol, gam, glast, eg, kb) in zip(us, ws, Ps, Ys, st)]


def _scan_group(Ss, us, ws, P, qds, kds, egls, ops):
    nn, nt, tn, _ = ops
    vns = [us[j] - nn(ws[j], Ss[j]) for j in range(GH)]
    o = jnp.concatenate([nn(qds[j], Ss[j]) for j in range(GH)], axis=0) + nn(P, jnp.concatenate(vns, axis=0))
    return o, [Ss[j] * egls[j] + tn(kds[j], vns[j]) for j in range(GH)]


def _group_masks(rev):
    r = lax.broadcasted_iota(jnp.int32, (GR, GR), 0)
    c = lax.broadcasted_iota(jnp.int32, (GR, GR), 1)
    same = (r // CH) == (c // CH)
    ahead = jnp.where(rev, c - r, r - c)
    incl = same & (ahead >= 0)
    strict = same & (ahead > 0)
    eye = jnp.where(r == c, 1.0, 0.0).astype(F32)
    lastc = same & ((c % CH) == jnp.where(rev, 0, CH - 1))
    return incl, strict, eye, lastc


def _head_gates(gb, h, rev):
    gcol = jnp.where(rev, gb[:, NH + h:NH + h + 1], gb[:, h:h + 1])
    bcol = jnp.where(rev, gb[:, 3 * NH + h:3 * NH + h + 1], gb[:, 2 * NH + h:2 * NH + h + 1])
    return gcol, bcol


def _hs(h):
    return slice(h * HD, (h + 1) * HD)


def _load_chain(q_ref, k_ref, v_ref, gb_ref, c, g, rev):
    r = slice(c * CH, (c + 1) * CH)
    heads = range(g * GH, (g + 1) * GH)
    gates = [_head_gates(gb_ref[r, :], h, rev) for h in heads]
    return ([q_ref[r, _hs(h)] for h in heads], [k_ref[r, _hs(h)] for h in heads], [v_ref[r, _hs(h)] for h in heads],
            [t[0] for t in gates], [t[1] for t in gates])


def gdn_intra_fwd(q, k, v, gb):
    T = q.shape[0]
    N = T // CH
    ops = (_raw_nn, _raw_nt, _raw_tn)

    def body(q_ref, k_ref, v_ref, gb_ref, u_ref, w_ref, qd_ref, kd_ref, pp_ref, ys_ref, eg_ref):
        rev = pl.program_id(0) == 1
        masks = _group_masks(rev)
        where = [(c, g) for c in range(CBI) for g in range(NG)]
        chains = [_load_chain(q_ref, k_ref, v_ref, gb_ref, c, g, rev) for c, g in where]
        for (c, g), ((u, w, P, qd, kd, egl), Y) in zip(where, _intra_groups(chains, *masks, ops, _tri_inv_y_all)):
            r = slice(c * CH, (c + 1) * CH)
            pp_ref[c, g] = P.astype(MM)
            ys_ref[c, g] = Y.astype(MM)
            for j, h in enumerate(range(g * GH, (g + 1) * GH)):
                rows = slice(j * CH, (j + 1) * CH)
                u_ref[r, _hs(h)] = u[rows]
                w_ref[r, _hs(h)] = w[rows].astype(MM)
                qd_ref[r, _hs(h)] = qd[rows].astype(MM)
                kd_ref[r, _hs(h)] = kd[rows].astype(MM)
                eg_ref[c, h:h + 1, :] = jnp.broadcast_to(egl[j * CH:j * CH + 1, :], (1, 128))

    row = pl.BlockSpec((CBI * CH, D), lambda d, n: (n, 0))
    drow = pl.BlockSpec((None, CBI * CH, D), lambda d, n: (d, n, 0))
    mat = pl.BlockSpec((None, CBI, NG, GR, GR), lambda d, n: (d, n, 0, 0, 0))
    return pl.pallas_call(
        body, name="gdn_intra_fwd", grid=(2, N // CBI),
        in_specs=[row, row, row, pl.BlockSpec((CBI * CH, 128), lambda d, n: (n, 0))],
        out_specs=[drow] * 4 + [mat, mat, pl.BlockSpec((None, CBI, NH, 128), lambda d, n: (d, n, 0, 0))],
        out_shape=[_sds((2, T, D))] + [_sds((2, T, D), MM)] * 3 + [_sds((2, N, NG, GR, GR), MM)] * 2
                  + [_sds((2, N, NH, 128))],
        compiler_params=_cp(("parallel", "parallel")))(q, k, v, gb)


def gdn_scan_fwd(u, w, qd, kd, pp, eg):
    T = u.shape[1]
    N = T // CH
    ops = (_raw_nn, _raw_nt, _raw_tn, None)

    cbs = _scan_chunks(N)
    NB = N // cbs

    def body(u_ref, w_ref, qd_ref, kd_ref, pp_ref, eg_ref, o_ref, s0_ref, S):
        d = pl.program_id(0)

        @pl.when(pl.program_id(1) == 0)
        def _():
            S[...] = jnp.zeros_like(S)

        def chunk(c, carry):
            pc = c + d * (cbs - 1 - 2 * c)
            r = pl.ds(pl.multiple_of(pc * CH, CH), CH)
            for g in range(NG):
                heads = range(g * GH, (g + 1) * GH)
                Ss = [S[h] for h in heads]
                for h, Sh in zip(heads, Ss):
                    s0_ref[pc, h] = Sh
                o, S2 = _scan_group(Ss, [u_ref[r, _hs(h)] for h in heads], [w_ref[r, _hs(h)] for h in heads],
                                    pp_ref[pc, g], [qd_ref[r, _hs(h)] for h in heads],
                                    [kd_ref[r, _hs(h)] for h in heads], [eg_ref[pc, h:h + 1, :] for h in heads], ops)
                for j, h in enumerate(heads):
                    o_ref[r, _hs(h)] = o[j * CH:(j + 1) * CH]
                    S[h] = S2[j]
            return carry

        lax.fori_loop(0, cbs, chunk, 0)

    bidx = lambda d, n: n + d * (NB - 1 - 2 * n)
    drow = pl.BlockSpec((None, cbs * CH, D), lambda d, n: (d, bidx(d, n), 0))
    mat = pl.BlockSpec((None, cbs, NG, GR, GR), lambda d, n: (d, bidx(d, n), 0, 0, 0))
    return pl.pallas_call(
        body, name="gdn_scan_fwd", grid=(2, NB),
        in_specs=[drow] * 4 + [mat, pl.BlockSpec((None, cbs, NH, 128), lambda d, n: (d, bidx(d, n), 0, 0))],
        out_specs=[drow, pl.BlockSpec((None, cbs, NH, HD, HD), lambda d, n: (d, bidx(d, n), 0, 0, 0))],
        out_shape=[_sds((2, T, D)), _sds((2, N, NH, HD, HD))],
        scratch_shapes=[pltpu.VMEM((NH, HD, HD), F32)],
        compiler_params=_cp(("arbitrary", "arbitrary")))(u, w, qd, kd, pp, eg)


def gdn_scan_bwd(u, w, qd, kd, pp, eg, s0, do):
    T = u.shape[1]
    N = T // CH
    ops = _make_vjp_ops()
    cbs = _scan_chunks(N)
    NB = N // cbs

    def body(u_ref, w_ref, qd_ref, kd_ref, pp_ref, eg_ref, s0_ref, do_ref,
             du_ref, dw_ref, dqd_ref, dkd_ref, dpp_ref, deg_ref, dS):
        d = pl.program_id(0)

        @pl.when(pl.program_id(1) == 0)
        def _():
            dS[...] = jnp.zeros_like(dS)

        def chunk(c, carry):
            pc = (cbs - 1 - c) + d * (2 * c - (cbs - 1))
            r = pl.ds(pl.multiple_of(pc * CH, CH), CH)
            for g in range(NG):
                heads = range(g * GH, (g + 1) * GH)
                step = lambda *a: _scan_group(*a, ops)
                _, vjp = jax.vjp(
                    step, [s0_ref[pc, h] for h in heads], [u_ref[r, _hs(h)] for h in heads],
                    [w_ref[r, _hs(h)].astype(F32) for h in heads], pp_ref[pc, g].astype(F32),
                    [qd_ref[r, _hs(h)].astype(F32) for h in heads], [kd_ref[r, _hs(h)].astype(F32) for h in heads],
                    [eg_ref[pc, h:h + 1, :] for h in heads])
                do = jnp.concatenate([do_ref[r, _hs(h)] for h in heads], axis=0)
                dSs, dus, dws, dP, dqds, dkds, degs = vjp((do, [dS[h] for h in heads]))
                dpp_ref[pc, g] = dP
                for j, h in enumerate(heads):
                    dS[h] = dSs[j]
                    du_ref[r, _hs(h)] = dus[j].astype(MM)
                    dw_ref[r, _hs(h)] = dws[j].astype(MM)
                    dqd_ref[r, _hs(h)], dkd_ref[r, _hs(h)] = dqds[j], dkds[j]
                    deg_ref[pc, h:h + 1, :] = degs[j]
            return carry

        lax.fori_loop(0, cbs, chunk, 0)

    bidx = lambda d, n: (NB - 1 - n) + d * (2 * n - (NB - 1))
    drow = pl.BlockSpec((None, cbs * CH, D), lambda d, n: (d, bidx(d, n), 0))
    erow = pl.BlockSpec((None, cbs, NH, 128), lambda d, n: (d, bidx(d, n), 0, 0))
    mat = pl.BlockSpec((None, cbs, NG, GR, GR), lambda d, n: (d, bidx(d, n), 0, 0, 0))
    return pl.pallas_call(
        body, name="gdn_scan_bwd", grid=(2, NB),
        in_specs=[drow] * 4 + [mat, erow, pl.BlockSpec((None, cbs, NH, HD, HD), lambda d, n: (d, bidx(d, n), 0, 0, 0)),
                               pl.BlockSpec((cbs * CH, D), lambda d, n: (bidx(d, n), 0))],
        out_specs=[drow] * 4 + [mat, erow],
        out_shape=[_sds((2, T, D), MM)] * 2 + [_sds((2, T, D))] * 2 + [_sds((2, N, NG, GR, GR))]
                  + [_sds((2, N, NH, 128))],
        scratch_shapes=[pltpu.VMEM((NH, HD, HD), F32)],
        compiler_params=_cp(("arbitrary", "arbitrary")))(u, w, qd, kd, pp, eg, s0, do)


def gdn_intra_bwd(q, k, v, gb, ys, du, dw, dqd, dkd, dpp, deg):
    T = q.shape[0]
    N = T // CH
    nn, nt, tn, inv_saved = _make_vjp_ops()

    def body(q_ref, k_ref, v_ref, gb_ref, ys_ref, du_ref, dw_ref, dqd_ref, dkd_ref, dpp_ref, deg_ref,
             dq_ref, dk_ref, dv_ref, dgb_ref):
        rev = pl.program_id(0) == 1
        masks = _group_masks(rev)
        lane = lax.broadcasted_iota(jnp.int32, (CH, 128), 1)
        grow = lax.broadcasted_iota(jnp.int32, (GR, 1), 0)
        where = [(c, g) for c in range(CBI) for g in range(NG)]
        chains = [_load_chain(q_ref, k_ref, v_ref, gb_ref, c, g, rev) for c, g in where]
        Ys = [ys_ref[c, g].astype(F32) for c, g in where]
        inv_all = lambda As: [inv_saved(A, Y) for A, Y in zip(As, Ys)]
        _, vjp = jax.vjp(lambda ch: _intra_groups(ch, *masks, (nn, nt, tn), inv_all), chains)
        cts = []
        for c, g in where:
            r = slice(c * CH, (c + 1) * CH)
            heads = range(g * GH, (g + 1) * GH)
            stack = lambda ref: jnp.concatenate([ref[r, _hs(h)].astype(F32) for h in heads], axis=0)
            degl = jnp.zeros((GR, 1), F32)
            for j, h in enumerate(heads):
                degl = degl + jnp.where(grow == j * CH, jnp.sum(deg_ref[c, h:h + 1, :], axis=1, keepdims=True), 0.0)
            cts.append(((stack(du_ref), stack(dw_ref), dpp_ref[c, g], stack(dqd_ref), stack(dkd_ref), degl),
                        jnp.zeros((GR, GR), F32)))
        (dchains,) = vjp(cts)
        dgbs = [jnp.zeros((CH, 128), F32) for _ in range(CBI)]
        for (c, g), (dqs, dks, dvs, dgs, dbs) in zip(where, dchains):
            r = slice(c * CH, (c + 1) * CH)
            for j, h in enumerate(range(g * GH, (g + 1) * GH)):
                dq_ref[r, _hs(h)], dk_ref[r, _hs(h)], dv_ref[r, _hs(h)] = dqs[j], dks[j], dvs[j]
                glane = jnp.where(rev, NH + h, h)
                dgbs[c] = dgbs[c] + jnp.where(lane == glane, dgs[j], 0.0) + jnp.where(lane == glane + 2 * NH, dbs[j], 0.0)
        for c in range(CBI):
            dgb_ref[c * CH:(c + 1) * CH, :] = dgbs[c]

    row = pl.BlockSpec((CBI * CH, D), lambda d, n: (n, 0))
    drow = pl.BlockSpec((None, CBI * CH, D), lambda d, n: (d, n, 0))
    mat = pl.BlockSpec((None, CBI, NG, GR, GR), lambda d, n: (d, n, 0, 0, 0))
    return pl.pallas_call(
        body, name="gdn_intra_bwd", grid=(2, N // CBI),
        in_specs=[row, row, row, pl.BlockSpec((CBI * CH, 128), lambda d, n: (n, 0)), mat, drow, drow, drow, drow, mat]
                 + [pl.BlockSpec((None, CBI, NH, 128), lambda d, n: (d, n, 0, 0))],
        out_specs=[drow, drow, drow, pl.BlockSpec((None, CBI * CH, 128), lambda d, n: (d, n, 0))],
        out_shape=[_sds((2, T, D))] * 3 + [_sds((2, T, 128))],
        compiler_params=_cp(("parallel", "parallel")))(q, k, v, gb, ys, du, dw, dqd, dkd, dpp, deg)


def _post_rows(o2a, o2b, z, nw):
    o = o2a + o2b
    outs = []
    for h in range(NH):
        s = slice(h * HD, (h + 1) * HD)
        oh = o[:, s]
        outs.append(oh * lax.rsqrt(jnp.mean(oh * oh, axis=-1, keepdims=True) + RMS_EPS) * nw * _silu(z[:, s]))
    return jnp.concatenate(outs, axis=1)


def post_fwd(o2, pm, nw, *, bt=512):
    T = pm.shape[0]
    bt = min(bt, T)

    def body(o_ref, z_ref, nw_ref, og_ref):
        og_ref[...] = _post_rows(o_ref[0], o_ref[1], z_ref[...], nw_ref[0:1, :]).astype(og_ref.dtype)

    return pl.pallas_call(
        body, name="post_fwd", grid=(T // bt,),
        in_specs=[pl.BlockSpec((2, bt, D), lambda i: (0, i, 0)), pl.BlockSpec((bt, D), lambda i: (i, CZ)),
                  pl.BlockSpec((8, 128), lambda i: (0, 0))],
        out_specs=pl.BlockSpec((bt, D), lambda i: (i, 0)), out_shape=_sds((T, D), MM),
        compiler_params=_cp(("parallel",)))(o2, pm, nw)


def post_bwd(o2, pm, nw, dog, dpm, *, bt=512):
    T = pm.shape[0]
    bt = min(bt, T)

    def body(o_ref, z_ref, nw_ref, dog_ref, _alias, do_ref, dz_ref, dnw_ref):
        @pl.when(pl.program_id(0) == 0)
        def _():
            dnw_ref[...] = jnp.zeros_like(dnw_ref)

        _, vjp = jax.vjp(_post_rows, o_ref[0], o_ref[1], z_ref[...], nw_ref[0:1, :])
        doa, _unused, dz, dnw = vjp(dog_ref[...])
        do_ref[...] = doa
        dz_ref[...] = dz
        row8 = lax.broadcasted_iota(jnp.int32, (8, 128), 0)
        dnw_ref[...] += jnp.where(row8 == 0, dnw, 0.0)

    in_specs = [pl.BlockSpec((2, bt, D), lambda i: (0, i, 0)), pl.BlockSpec((bt, D), lambda i: (i, CZ)),
                pl.BlockSpec((8, 128), lambda i: (0, 0)), pl.BlockSpec((bt, D), lambda i: (i, 0)),
                pl.BlockSpec(memory_space=pl.ANY)]
    return pl.pallas_call(
        body, name="post_bwd", grid=(T // bt,), in_specs=in_specs,
        out_specs=[pl.BlockSpec((bt, D), lambda i: (i, 0)), pl.BlockSpec((bt, D), lambda i: (i, CZ)),
                   pl.BlockSpec((8, 128), lambda i: (0, 0))],
        out_shape=[_sds((T, D)), _sds(dpm.shape), _sds((8, 128))], input_output_aliases={4: 1},
        compiler_params=_cp(("arbitrary",)))(o2, pm, nw, dog, dpm)


def merge_fwd(ya, yb, pm, *, bt=512):
    T = pm.shape[0]
    bt = min(bt, T)

    def body(ya_ref, yb_ref, ga_ref, gb_ref, o_ref):
        o_ref[...] = (_sigmoid(ga_ref[...]) * ya_ref[...] + _sigmoid(gb_ref[...]) * yb_ref[...]).astype(o_ref.dtype)

    row = pl.BlockSpec((bt, D), lambda i: (i, 0))
    return pl.pallas_call(
        body, name="merge_fwd", grid=(T // bt,),
        in_specs=[row, row, pl.BlockSpec((bt, D), lambda i: (i, CGA)), pl.BlockSpec((bt, D), lambda i: (i, CGB))],
        out_specs=row, out_shape=_sds((T, D), MM), compiler_params=_cp(("parallel",)))(ya, yb, pm, pm)


def merge_bwd(ya, yb, pm, dmix, *, bt=512):
    T = pm.shape[0]
    bt = min(bt, T)

    def body(ya_ref, yb_ref, ga_ref, gb_ref, dm_ref, dya_ref, dyb_ref, dg_ref):
        dm = dm_ref[...]
        sa, sb = _sigmoid(ga_ref[...]), _sigmoid(gb_ref[...])
        dya_ref[...] = (dm * sa).astype(dya_ref.dtype)
        dyb_ref[...] = (dm * sb).astype(dyb_ref.dtype)
        dg_ref[:, 0:D] = dm * ya_ref[...] * sa * (1.0 - sa)
        dg_ref[:, D:2 * D] = dm * yb_ref[...] * sb * (1.0 - sb)

    row = pl.BlockSpec((bt, D), lambda i: (i, 0))
    return pl.pallas_call(
        body, name="merge_bwd", grid=(T // bt,),
        in_specs=[row, row, pl.BlockSpec((bt, D), lambda i: (i, CGA)), pl.BlockSpec((bt, D), lambda i: (i, CGB)), row],
        out_specs=[row, row, pl.BlockSpec((bt, 2 * D), lambda i: (i, CGA // 2))],
        out_shape=[_sds((T, D), MM), _sds((T, D), MM), _sds((T, NMAIN))],
        compiler_params=_cp(("parallel",)))(ya, yb, pm, pm, dmix)


def _ln_rows(x, y, bias, g, b):
    r = ALPHA * x + y + bias
    mu = jnp.mean(r, axis=-1, keepdims=True)
    var = jnp.mean(jnp.square(r - mu), axis=-1, keepdims=True)
    return (r - mu) * lax.rsqrt(var + LN_EPS) * g + b


def ln_fwd(x, y, p, *, name, bt=512):
    T = x.shape[0]
    bt = min(bt, T)

    def body(x_ref, y_ref, p_ref, o_ref, ob_ref):
        r = _ln_rows(x_ref[...], y_ref[...], p_ref[0:1, :], p_ref[1:2, :], p_ref[2:3, :])
        o_ref[...] = r
        ob_ref[...] = r.astype(ob_ref.dtype)

    row = pl.BlockSpec((bt, D), lambda i: (i, 0))
    return pl.pallas_call(
        body, name=name, grid=(T // bt,), in_specs=[row, row, pl.BlockSpec((8, D), lambda i: (0, 0))],
        out_specs=[row, row], out_shape=[_sds((T, D)), _sds((T, D), MM)],
        compiler_params=_cp(("parallel",)))(x, y, p)


def ln_bwd(x, y, p, ct, ct2=None, *, name, bt=512):
    T = x.shape[0]
    bt = min(bt, T)

    def body(*refs):
        it = iter(refs)
        x_ref, y_ref, p_ref, c_ref = next(it), next(it), next(it), next(it)
        c2_ref = next(it) if ct2 is not None else None
        dxa_ref, dr_ref, dp_ref = next(it), next(it), next(it)

        @pl.when(pl.program_id(0) == 0)
        def _():
            dp_ref[...] = jnp.zeros_like(dp_ref)

        c = c_ref[...]
        if c2_ref is not None:
            c = c + c2_ref[...]
        _, vjp = jax.vjp(_ln_rows, x_ref[...], y_ref[...], p_ref[0:1, :], p_ref[1:2, :], p_ref[2:3, :])
        _dx, dy, dbias, dg, db = vjp(c)
        dxa_ref[...] = ALPHA * dy
        dr_ref[...] = dy.astype(dr_ref.dtype)
        row8 = lax.broadcasted_iota(jnp.int32, (8, D), 0)
        dp_ref[...] += jnp.where(row8 == 0, dbias, jnp.where(row8 == 1, dg, jnp.where(row8 == 2, db, 0.0)))

    row = pl.BlockSpec((bt, D), lambda i: (i, 0))
    in_specs = [row, row, pl.BlockSpec((8, D), lambda i: (0, 0)), row] + ([row] if ct2 is not None else [])
    args = [x, y, p, ct] + ([ct2] if ct2 is not None else [])
    return pl.pallas_call(
        body, name=name, grid=(T // bt,), in_specs=in_specs,
        out_specs=[row, row, pl.BlockSpec((8, D), lambda i: (0, 0))],
        out_shape=[_sds((T, D)), _sds((T, D), MM), _sds((8, D))],
        compiler_params=_cp(("arbitrary",)))(*args)


def loss_fwd_bwd(xl, target, *, bt=512):
    T = xl.shape[0]
    bt = min(bt, T)

    def body(x_ref, t_ref, l_ref, d_ref):
        @pl.when(pl.program_id(0) == 0)
        def _():
            l_ref[...] = jnp.zeros_like(l_ref)

        e = x_ref[...] - t_ref[...]
        d_ref[...] = e * (1.0 / D)
        l_ref[...] += 0.5 * jnp.sum(jnp.mean(e * e, axis=-1, keepdims=True), axis=0, keepdims=True)

    row = pl.BlockSpec((bt, D), lambda i: (i, 0))
    return pl.pallas_call(
        body, name="loss", grid=(T // bt,), in_specs=[row, row],
        out_specs=[pl.BlockSpec((8, 128), lambda i: (0, 0)), row], out_shape=[_sds((8, 128)), _sds((T, D))],
        compiler_params=_cp(("arbitrary",)))(xl, target)


def _row_tile(R, Cc, elems=1 << 18):
    if R * Cc <= elems:
        return R
    tr = 8
    while tr * 2 * Cc <= elems and R % (tr * 2) == 0:
        tr *= 2
    return tr


def adam(w, m, v, ga, gb=None, *, name):
    R, Cc = w.shape
    tr = _row_tile(R, Cc)

    def body(*refs):
        it = iter(refs)
        w_ref, m_ref, v_ref, a_ref = next(it), next(it), next(it), next(it)
        b_ref = next(it) if gb is not None else None
        g_ref, d_ref, mo_ref, vo_ref = next(it), next(it), next(it), next(it)
        g = a_ref[...]
        if b_ref is not None:
            g = g + b_ref[...]
        m2 = B1 * m_ref[...] + (1.0 - B1) * g
        v2 = B2 * v_ref[...] + (1.0 - B2) * jnp.square(g)
        m_hat = m2 / (1.0 - B1 ** STEP)
        v_hat = v2 / (1.0 - B2 ** STEP)
        g_ref[...] = g
        d_ref[...] = -LR * (m_hat / (jnp.sqrt(v_hat) + EPS) + WD * w_ref[...])
        mo_ref[...] = m2
        vo_ref[...] = v2

    blk = pl.BlockSpec((tr, Cc), lambda i: (i, 0))
    args = [w, m, v, ga] + ([gb] if gb is not None else [])
    return pl.pallas_call(
        body, name=name, grid=(R // tr,), in_specs=[blk] * len(args), out_specs=[blk] * 4,
        out_shape=[_sds((R, Cc))] * 4, compiler_params=_cp(("parallel",)))(*args)


def sum4(own, recv, *, name):
    R, Cc = own.shape
    tr = _row_tile(R, Cc)

    def body(o_ref, r_ref, out_ref):
        f = lambda t: t.astype(F32)
        out_ref[...] = ((f(o_ref[...]) + f(r_ref[0])) + f(r_ref[1])) + f(r_ref[2])

    return pl.pallas_call(
        body, name=name, grid=(R // tr,),
        in_specs=[pl.BlockSpec((tr, Cc), lambda i: (i, 0)), pl.BlockSpec((3, tr, Cc), lambda i: (0, i, 0))],
        out_specs=pl.BlockSpec((tr, Cc), lambda i: (i, 0)), out_shape=_sds((R, Cc)),
        compiler_params=_cp(("parallel",)))(own, recv)


def _place():
    return lax.axis_index("x"), lax.axis_index("y"), lax.axis_index("c")


def _other_chips(x, y):
    return [(1 - x, y), (x, 1 - y), (1 - x, 1 - y)]


_ANY = pl.BlockSpec(memory_space=pl.ANY)


def allgather_xy(arrs):
    n = len(arrs)
    halves = [a.shape[0] // 2 for a in arrs]

    def body(*refs):
        ins, outs = refs[:n], refs[n:2 * n]
        send, recv, fsend, frecv, loc = refs[2 * n:]
        x, y, c = _place()
        me = 2 * x + y
        peers = _other_chips(x, y)
        local = [pltpu.make_async_copy(ins[a], outs[a].at[me], loc.at[a]) for a in range(n)]
        for cp in local:
            cp.start()

        def over_ici(a, j, block, src=None):
            px, py = peers[j]
            mine = pl.ds(c * halves[a], halves[a])
            dst = outs[a].at[block, mine]
            return pltpu.make_async_remote_copy(
                src_ref=dst if src is None else src.at[mine], dst_ref=dst, send_sem=send.at[3 * a + j],
                recv_sem=recv.at[3 * a + j], device_id=(px, py, c), device_id_type=MESH)

        def over_d2d(a, j, half):
            px, py = peers[j]
            rows = outs[a].at[2 * px + py, pl.ds(half * halves[a], halves[a])]
            return pltpu.make_async_remote_copy(
                src_ref=rows, dst_ref=rows, send_sem=fsend.at[3 * a + j], recv_sem=frecv.at[3 * a + j],
                device_id=(x, y, 1 - c), device_id_type=MESH)

        sends = [over_ici(a, j, me, src=ins[a]) for a in range(n) for j in range(3)]
        for cp in sends:
            cp.start()
        passed = []
        for a in range(n):
            for j, (px, py) in enumerate(peers):
                over_ici(a, j, 2 * px + py).wait_recv()
                passed.append(over_d2d(a, j, c))
                passed[-1].start()
        for a in range(n):
            for j in range(3):
                over_d2d(a, j, 1 - c).wait_recv()
        for cp in sends + passed:
            cp.wait_send()
        for cp in local:
            cp.wait()

    return pl.pallas_call(
        body, name="allgather_xy", in_specs=[_ANY] * n, out_specs=[_ANY] * n,
        out_shape=[_sds((4,) + a.shape, a.dtype) for a in arrs],
        scratch_shapes=[pltpu.SemaphoreType.DMA((3 * n,))] * 4 + [pltpu.SemaphoreType.DMA((n,))],
        compiler_params=pltpu.CompilerParams(has_side_effects=True))(*arrs)


def scatter_xy(groups):
    L = len(groups[0])
    n = len(groups) * L
    flat = [g for grp in groups for g in grp]

    def body(*refs):
        ins = refs[:n]
        owns, recvs = refs[n:n + len(groups)], refs[n + len(groups):n + 2 * len(groups)]
        send, recv, loc = refs[n + 2 * len(groups):]
        x, y, c = _place()
        me = 2 * x + y
        peers = _other_chips(x, y)
        local, sends = [], []
        for a in range(len(groups)):
            for l in range(L):
                i = a * L + l
                local.append(pltpu.make_async_copy(ins[i].at[me], owns[a].at[l], loc.at[i]))
                for j, (px, py) in enumerate(peers):
                    sends.append(pltpu.make_async_remote_copy(
                        src_ref=ins[i].at[2 * px + py], dst_ref=recvs[a].at[j, l], send_sem=send.at[3 * i + j],
                        recv_sem=recv.at[3 * i + j], device_id=(px, py, c), device_id_type=MESH))
        for cp in local + sends:
            cp.start()
        for cp in sends:
            cp.wait_recv()
        for cp in sends:
            cp.wait_send()
        for cp in local:
            cp.wait()

    out_shape = ([_sds((L,) + grp[0].shape[1:], grp[0].dtype) for grp in groups]
                 + [_sds((3, L) + grp[0].shape[1:], grp[0].dtype) for grp in groups])
    outs = pl.pallas_call(
        body, name="scatter_xy", in_specs=[_ANY] * n, out_specs=[_ANY] * (2 * len(groups)), out_shape=out_shape,
        scratch_shapes=[pltpu.SemaphoreType.DMA((3 * n,)), pltpu.SemaphoreType.DMA((3 * n,)),
                        pltpu.SemaphoreType.DMA((n,))],
        compiler_params=pltpu.CompilerParams(has_side_effects=True))(*flat)
    return outs[:len(groups)], outs[len(groups):]


def swap_c(arrs):
    n = len(arrs)

    def body(*refs):
        ins, outs = refs[:n], refs[n:2 * n]
        send, recv = refs[2 * n:]
        x, y, c = _place()
        cps = [pltpu.make_async_remote_copy(src_ref=ins[a], dst_ref=outs[a], send_sem=send.at[a], recv_sem=recv.at[a],
                                            device_id=(x, y, 1 - c), device_id_type=MESH) for a in range(n)]
        for cp in cps:
            cp.start()
        for cp in cps:
            cp.wait_recv()
        for cp in cps:
            cp.wait_send()

    return pl.pallas_call(
        body, name="swap_c", in_specs=[_ANY] * n, out_specs=[_ANY] * n, out_shape=[_sds(a.shape, a.dtype) for a in arrs],
        scratch_shapes=[pltpu.SemaphoreType.DMA((n,)), pltpu.SemaphoreType.DMA((n,))],
        compiler_params=pltpu.CompilerParams(has_side_effects=True))(*arrs)


def allreduce_small(v):
    R = v.shape[0]

    def body(v_ref, o_ref, buf, send, recv):
        x, y, c = _place()
        me = 4 * x + 2 * y + c
        buf[0] = v_ref[...]

        def cp(k):
            dx, dy, dc = (k >> 2) & 1, (k >> 1) & 1, k & 1
            return pltpu.make_async_remote_copy(
                src_ref=v_ref, dst_ref=buf.at[k], send_sem=send.at[k - 1], recv_sem=recv.at[k - 1],
                device_id=(x ^ dx, y ^ dy, c ^ dc), device_id_type=MESH)

        cps = [cp(k) for k in range(1, 8)]
        for t in cps:
            t.start()
        for t in cps:
            t.wait_recv()
        acc = buf[me]
        for dev in range(1, 8):
            acc = acc + buf[jnp.bitwise_xor(me, dev)]
        o_ref[...] = acc
        for t in cps:
            t.wait_send()

    vm = pl.BlockSpec(memory_space=pltpu.VMEM)
    return pl.pallas_call(
        body, name="allreduce_small", in_specs=[vm], out_specs=vm, out_shape=_sds((R, 128)),
        scratch_shapes=[pltpu.VMEM((8, R, 128), F32), pltpu.SemaphoreType.DMA((7,)), pltpu.SemaphoreType.DMA((7,))],
        compiler_params=pltpu.CompilerParams(has_side_effects=True, vmem_limit_bytes=VMEM_LIMIT))(v)


def _rows8(*rows):
    n = rows[0].shape[-1]
    t = jnp.stack([r.reshape(n).astype(F32) for r in rows])
    return jnp.pad(t, ((0, 8 - len(rows)), (0, 0)))


def _lanes128(a):
    f = a.reshape(-1).astype(F32)
    return jnp.pad(f, (0, 128 - f.shape[0]))


def _layer_fwd(x, xb, W):
    pm = mm_nn(xb, W["w_main"], name="proj_main")
    pab = mm_nn(xb, W["w_ab"], name="proj_ab")
    q, k, v, gb = pre_qkv_fwd(pm, pab, W["cw"], W["gp"])
    sc = pre_sc_fwd(pm, W["csc"])
    u, w, qd, kd, pp, ys, eg = gdn_intra_fwd(q, k, v, gb)
    o2, s0 = gdn_scan_fwd(u, w, qd, kd, pp, eg)
    og = post_fwd(o2, pm, W["nw"])
    ya = mm_nn(og, W["w_og"], name="proj_og")
    yb = mm_nn(sc, W["w_osc"], name="proj_osc")
    mixed = merge_fwd(ya, yb, pm)
    out = mm_nn(mixed, W["w_out"], name="proj_out")
    x1, x1b = ln_fwd(x, out, W["ln1"], name="ln1_fwd")
    hpre, h = mm_nn(x1b, W["w_up"], bias=W["b_up"], relu2=True, name="mlp_up")
    dn = mm_nn(h, W["w_down"], name="mlp_down")
    x2, x2b = ln_fwd(x1, dn, W["ln2"], name="ln2_fwd")
    saved = dict(x=x, xb=xb, pm=pm, pab=pab, q=q, k=k, v=v, gb=gb, sc=sc, o2=o2, s0=s0, og=og, ya=ya, yb=yb,
                 u=u, w=w, qd=qd, kd=kd, pp=pp, ys=ys, eg=eg,
                 mixed=mixed, out=out, x1=x1, x1b=x1b, hpre=hpre, h=h, dn=dn)
    return x2, x2b, saved


def _layer_bwd(ct, W, S):
    dxa2, dr2b, dp2 = ln_bwd(S["x1"], S["dn"], W["ln2"], ct, name="ln2_bwd")
    g_down = mm_tn(S["h"], dr2b, name="dw_down")
    dhpre, db_up = mm_nt(dr2b, W["w_down"], dact=S["hpre"], out_dtype=MM, name="mlp_down_bwd")
    g_up = mm_tn(S["x1b"], dhpre, name="dw_up")
    dx1 = mm_nt(dhpre, W["w_up"], add=dxa2, name="mlp_up_bwd")
    dxa1, dr1b, dp1 = ln_bwd(S["x"], S["out"], W["ln1"], dx1, name="ln1_bwd")
    g_out = mm_tn(S["mixed"], dr1b, name="dw_out")
    dmix = mm_nt(dr1b, W["w_out"], name="proj_out_bwd")
    dya, dyb, dpm = merge_bwd(S["ya"], S["yb"], S["pm"], dmix)
    g_og = mm_tn(S["og"], dya, name="dw_og")
    g_osc = mm_tn(S["sc"], dyb, name="dw_osc")
    dog = mm_nt(dya, W["w_og"], name="proj_og_bwd")
    dsc = mm_nt(dyb, W["w_osc"], name="proj_osc_bwd")
    do, dpm, dnw = post_bwd(S["o2"], S["pm"], W["nw"], dog, dpm)
    du, dw, dqd, dkd, dpp, deg = gdn_scan_bwd(S["u"], S["w"], S["qd"], S["kd"], S["pp"], S["eg"], S["s0"], do)
    dq2, dk2, dv2, dgb2 = gdn_intra_bwd(S["q"], S["k"], S["v"], S["gb"], S["ys"], du, dw, dqd, dkd, dpp, deg)
    dpm, dpab, dcw, dgp = pre_qkv_bwd(S["pm"], S["pab"], W["cw"], W["gp"], dq2, dk2, dv2, dgb2, dpm)
    dpm, dcsc = pre_sc_bwd(S["pm"], W["csc"], dsc, dpm)
    g_main = mm_tn(S["xb"], dpm, name="dw_main")
    g_ab = mm_tn(S["xb"], dpab, name="dw_ab")
    t = mm_nt(dpab, W["w_ab"], add=dxa1, name="proj_ab_bwd")
    dx = mm_nt(dpm, W["w_main"], add=t, name="proj_main_bwd")
    g_in = jnp.concatenate([g_main[:, :3 * D], g_main[:, 8 * D:], g_ab[:, :4 * NH], g_main[:, 3 * D:8 * D]], axis=1)
    grads = dict(
        w_in=g_in, w_o_gdn=g_og, w_o_sc=g_osc, w_out=g_out, w_up=g_up, w_down=g_down,
        conv_qkv=dcw[:3], conv_sc=dcsc[:3], a_log=dgp[0, :2 * NH].reshape(2, NH), dt_bias=dgp[1, :2 * NH].reshape(2, NH),
        gdn_norm_w=dnw[0], ln1_g=dp1[1], ln1_b=dp1[2], b_up=db_up[0], b_down=dp2[0], ln2_g=dp2[1], ln2_b=dp2[2])
    return dx, grads


def _layer_weights(l, full, a_log, dt_bias, gdn_norm_w, ln1_g, ln1_b, b_up, b_down, ln2_g, ln2_b):
    w_in = full["w_in"][l]
    w_main = jnp.concatenate([w_in[:, :3 * D], w_in[:, 4 * D + 4 * NH:], w_in[:, 3 * D:4 * D]], axis=1)
    w_ab = jnp.pad(w_in[:, 4 * D:4 * D + 4 * NH], ((0, 0), (0, 128 - 4 * NH)))
    return dict(
        w_main=w_main, w_ab=w_ab, w_og=full["w_o_gdn"][l], w_osc=full["w_o_sc"][l], w_out=full["w_out"][l],
        w_up=full["w_up"][l], w_down=full["w_down"][l],
        cw=jnp.pad(full["conv_qkv"][l].astype(F32), ((0, 5), (0, 0))),
        csc=jnp.pad(full["conv_sc"][l].astype(F32), ((0, 5), (0, 0))),
        gp=_rows8(_lanes128(a_log[l]), _lanes128(dt_bias[l])), nw=_rows8(gdn_norm_w[l]),
        ln1=_rows8(jnp.zeros((D,), F32), ln1_g[l], ln1_b[l]), ln2=_rows8(b_down[l], ln2_g[l], ln2_b[l]),
        b_up=b_up[l].reshape(1, DFF).astype(F32))


def local_step(xs, target, full, a_log, dt_bias, gdn_norm_w, ln1_g, ln1_b, b_up, b_down, ln2_g, ln2_b):
    Ws = [_layer_weights(l, full, a_log, dt_bias, gdn_norm_w, ln1_g, ln1_b, b_up, b_down, ln2_g, ln2_b)
          for l in range(DEPTH)]
    x, xb = xs, xs.astype(MM)
    saved = []
    for l in range(DEPTH):
        x, xb, S = _layer_fwd(x, xb, Ws[l])
        saved.append(S)
    loss_tile, ct = loss_fwd_bwd(x, target)
    grads = [None] * DEPTH
    for l in reversed(range(DEPTH)):
        ct, grads[l] = _layer_bwd(ct, Ws[l], saved[l])
    return loss_tile, ct, grads


BIG = ("w_in", "w_o_gdn", "w_o_sc", "w_out", "w_up", "w_down")
SMALL = ("conv_qkv", "a_log", "dt_bias", "gdn_norm_w", "conv_sc", "ln1_g", "ln1_b", "b_up", "b_down", "ln2_g", "ln2_b")
ORDER = ("w_in", "conv_qkv", "a_log", "dt_bias", "gdn_norm_w", "w_o_gdn", "conv_sc", "w_o_sc", "w_out", "ln1_g",
         "ln1_b", "w_up", "b_up", "w_down", "b_down", "ln2_g", "ln2_b")


def _pack(arrs):
    flat = jnp.concatenate([a.reshape(-1).astype(F32) for a in arrs])
    n = flat.shape[0]
    rows = -(-n // 1024) * 8
    return jnp.pad(flat, (0, rows * 128 - n)).reshape(rows, 128)


def _unpack(buf, like):
    flat = buf.reshape(-1)
    out, o = [], 0
    for a in like:
        n = 1
        for s in a.shape:
            n *= s
        out.append(flat[o:o + n].reshape(a.shape))
        o += n
    return out


def _gathered(name, g):
    if name in ("w_in", "w_up", "conv_qkv", "conv_sc"):
        t = jnp.moveaxis(g, 0, -2)
        return t.reshape(t.shape[:-2] + (t.shape[-2] * t.shape[-1],))
    t = jnp.moveaxis(g, 0, 1)
    return t.reshape((t.shape[0], t.shape[1] * t.shape[2]) + t.shape[3:])


def _by_chip(name, g):
    if name in ("w_in", "w_up"):
        r, ccols = g.shape
        return jnp.moveaxis(g.reshape(r, 4, ccols // 4), 1, 0)
    return g.reshape((4, g.shape[0] // 4) + g.shape[1:])


def kernel(x, w_in, conv_qkv, a_log, dt_bias, gdn_norm_w, w_o_gdn, conv_sc, w_o_sc, w_out, ln1_g, ln1_b, w_up, b_up, w_down, b_down, ln2_g, ln2_b, loss_target, m_w_in, m_conv_qkv, m_a_log, m_dt_bias, m_gdn_norm_w, m_w_o_gdn, m_conv_sc, m_w_o_sc, m_w_out, m_ln1_g, m_ln1_b, m_w_up, m_b_up, m_w_down, m_b_down, m_ln2_g, m_ln2_b, v_w_in, v_conv_qkv, v_a_log, v_dt_bias, v_gdn_norm_w, v_w_o_gdn, v_conv_sc, v_w_o_sc, v_w_out, v_ln1_g, v_ln1_b, v_w_up, v_b_up, v_w_down, v_b_down, v_ln2_g, v_ln2_b):
    w = dict(w_in=w_in, conv_qkv=conv_qkv, a_log=a_log, dt_bias=dt_bias, gdn_norm_w=gdn_norm_w, w_o_gdn=w_o_gdn,
             conv_sc=conv_sc, w_o_sc=w_o_sc, w_out=w_out, ln1_g=ln1_g, ln1_b=ln1_b, w_up=w_up, b_up=b_up,
             w_down=w_down, b_down=b_down, ln2_g=ln2_g, ln2_b=ln2_b)
    m = dict(w_in=m_w_in, conv_qkv=m_conv_qkv, a_log=m_a_log, dt_bias=m_dt_bias, gdn_norm_w=m_gdn_norm_w,
             w_o_gdn=m_w_o_gdn, conv_sc=m_conv_sc, w_o_sc=m_w_o_sc, w_out=m_w_out, ln1_g=m_ln1_g, ln1_b=m_ln1_b,
             w_up=m_w_up, b_up=m_b_up, w_down=m_w_down, b_down=m_b_down, ln2_g=m_ln2_g, ln2_b=m_ln2_b)
    v = dict(w_in=v_w_in, conv_qkv=v_conv_qkv, a_log=v_a_log, dt_bias=v_dt_bias, gdn_norm_w=v_gdn_norm_w,
             w_o_gdn=v_w_o_gdn, conv_sc=v_conv_sc, w_o_sc=v_w_o_sc, w_out=v_w_out, ln1_g=v_ln1_g, ln1_b=v_ln1_b,
             w_up=v_w_up, b_up=v_b_up, w_down=v_w_down, b_down=v_b_down, ln2_g=v_ln2_g, ln2_b=v_ln2_b)
    chip = 2 * lax.axis_index("x") + lax.axis_index("y")

    names = BIG + ("conv_qkv", "conv_sc")
    got = allgather_xy([w[n].astype(MM) if n in BIG else w[n] for n in names])
    full = {n: _gathered(n, g) for n, g in zip(names, got)}

    loss_tile, dx, grads = local_step(x[0], loss_target[0], full, a_log, dt_bias, gdn_norm_w, ln1_g, ln1_b, b_up,
                                      b_down, ln2_g, ln2_b)
    loss = lax.psum(loss_tile[0, 0], ("x", "y", "c"))

    groups = [[_by_chip(n, grads[l][n]).astype(MM) for l in range(DEPTH)] for n in BIG]
    owns, recvs = scatter_xy(groups)
    part = []
    for n, own, rec in zip(BIG, owns, recvs):
        cols = own.shape[-1]
        part.append(sum4(own.reshape(-1, cols), rec.reshape(3, -1, cols), name="sum_" + n))
    other = swap_c(part)
    out = {}
    for n, mine, theirs in zip(BIG, part, other):
        cols = mine.shape[-1]
        res = adam(w[n].reshape(-1, cols), m[n].reshape(-1, cols), v[n].reshape(-1, cols), mine, theirs, name="adam_" + n)
        out[n] = [r.reshape(w[n].shape) for r in res]

    stacked = [jnp.stack([grads[l][n] for l in range(DEPTH)]) for n in SMALL]
    summed = _unpack(allreduce_small(_pack(stacked)), stacked)
    gs = []
    for n, g in zip(SMALL, summed):
        if n in ("conv_qkv", "conv_sc"):
            blk = w[n].shape[-1]
            g = lax.dynamic_slice_in_dim(g, chip * blk, blk, axis=2)
        gs.append(g)
    res = adam(_pack([w[n] for n in SMALL]), _pack([m[n] for n in SMALL]), _pack([v[n] for n in SMALL]), _pack(gs),
               name="adam_small")
    for n, parts in zip(SMALL, zip(*[_unpack(r, gs) for r in res])):
        out[n] = list(parts)

    outs = [loss, dx[None]]
    for kind in range(4):
        outs += [out[n][kind] for n in ORDER]
    return tuple(outs)
```

```python
import functools

import jax
import jax.numpy as jnp
from jax import lax
from jax.experimental import pallas as pl
from jax.experimental.pallas import tpu as pltpu

F32 = jnp.float32
MM = jnp.bfloat16
HI = lax.Precision.HIGHEST

D = 1024
NH = 8
HD = 128
CH = 64
DFF = 4 * D
DEPTH = 4
LN_EPS = 1e-5
RMS_EPS = 1e-6
L2_EPS = 1e-6
ALPHA = (2 * DEPTH) ** 0.25
LR, B1, B2, EPS, WD, STEP = 0.001, 0.9, 0.999, 1e-08, 0.01, 10

NMAIN = 9 * D
CQ, CK, CV, CSB, CSC, CSX, CGA, CGB, CZ = range(9)
HALO = 8
VMEM_LIMIT = 56 * 1024 * 1024
MESH = pl.DeviceIdType.MESH


def _cp(sem=None, vmem=VMEM_LIMIT):
    return pltpu.CompilerParams(dimension_semantics=sem, vmem_limit_bytes=vmem)


def _sds(shape, dtype=F32):
    return jax.ShapeDtypeStruct(tuple(shape), dtype)


def _accumulate(acc, product, k, nk, finish):
    if nk == 1:
        finish(product())
        return

    @pl.when(k == 0)
    def _():
        acc[...] = jnp.zeros_like(acc)

    acc[...] += product()

    @pl.when(k == nk - 1)
    def _():
        finish(acc[...])


def mm_nn(a, b, *, name, bias=None, relu2=False, add=None, out_dtype=F32, tm=1024, tn=1024, tk=1024):
    M, K = a.shape
    N = b.shape[1]
    tm, tn, tk = min(tm, M), min(tn, N), min(tk, K)
    nk = K // tk

    def body(*refs):
        it = iter(refs)
        a_ref, b_ref = next(it), next(it)
        bias_ref = next(it) if bias is not None else None
        add_ref = next(it) if add is not None else None
        o_ref = next(it)
        h_ref = next(it) if relu2 else None
        acc = next(it) if nk > 1 else None
        prod = lambda: jnp.dot(a_ref[...].astype(MM), b_ref[...].astype(MM), preferred_element_type=F32)

        def finish(r):
            if bias_ref is not None:
                r = r + bias_ref[...]
            if add_ref is not None:
                r = r + add_ref[...]
            o_ref[...] = r.astype(o_ref.dtype)
            if relu2:
                t = jnp.maximum(r, 0.0)
                h_ref[...] = (t * t).astype(h_ref.dtype)

        _accumulate(acc, prod, pl.program_id(2), nk, finish)

    in_specs = [pl.BlockSpec((tm, tk), lambda i, j, k: (i, k)), pl.BlockSpec((tk, tn), lambda i, j, k: (k, j))]
    args = [a, b]
    if bias is not None:
        in_specs.append(pl.BlockSpec((1, tn), lambda i, j, k: (0, j)))
        args.append(bias)
    if add is not None:
        in_specs.append(pl.BlockSpec((tm, tn), lambda i, j, k: (i, j)))
        args.append(add)
    out_shape = [_sds((M, N), out_dtype)]
    out_specs = [pl.BlockSpec((tm, tn), lambda i, j, k: (i, j))]
    if relu2:
        out_shape.append(_sds((M, N), MM))
        out_specs.append(pl.BlockSpec((tm, tn), lambda i, j, k: (i, j)))
    res = pl.pallas_call(
        body, name=name, grid=(M // tm, N // tn, nk), in_specs=in_specs, out_specs=out_specs, out_shape=out_shape,
        scratch_shapes=[pltpu.VMEM((tm, tn), F32)] if nk > 1 else [],
        compiler_params=_cp(("parallel", "parallel", "arbitrary")))(*args)
    return res if relu2 else res[0]


def mm_nt(a, b, *, name, add=None, dact=None, out_dtype=F32, tm=1024, tn=1024, tk=1024):
    M, Nc = a.shape
    Ko = b.shape[0]
    tm, tn, tk = min(tm, M), min(tn, Ko), min(tk, Nc)
    nk = Nc // tk
    ni = M // tm

    def body(*refs):
        it = iter(refs)
        a_ref, b_ref = next(it), next(it)
        add_ref = next(it) if add is not None else None
        d_ref = next(it) if dact is not None else None
        o_ref = next(it)
        db_ref = next(it) if dact is not None else None
        acc = next(it) if nk > 1 else None
        i = pl.program_id(1)
        prod = lambda: lax.dot_general(a_ref[...].astype(MM), b_ref[...].astype(MM), (((1,), (1,)), ((), ())),
                                       preferred_element_type=F32)

        def finish(r):
            if add_ref is not None:
                r = r + add_ref[...]
            if d_ref is not None:
                r = r * (2.0 * jnp.maximum(d_ref[...], 0.0))
                s = jnp.sum(r, axis=0, keepdims=True)
                row0 = lax.broadcasted_iota(jnp.int32, db_ref.shape, 0) == 0

                @pl.when(i == 0)
                def _():
                    db_ref[...] = jnp.zeros_like(db_ref)

                db_ref[...] += jnp.where(row0, s, 0.0)
            o_ref[...] = r.astype(o_ref.dtype)

        _accumulate(acc, prod, pl.program_id(2), nk, finish)

    in_specs = [pl.BlockSpec((tm, tk), lambda j, i, k: (i, k)), pl.BlockSpec((tn, tk), lambda j, i, k: (j, k))]
    args = [a, b]
    for extra in (add, dact):
        if extra is not None:
            in_specs.append(pl.BlockSpec((tm, tn), lambda j, i, k: (i, j)))
            args.append(extra)
    out_shape = [_sds((M, Ko), out_dtype)]
    out_specs = [pl.BlockSpec((tm, tn), lambda j, i, k: (i, j))]
    if dact is not None:
        out_shape.append(_sds((8, Ko), F32))
        out_specs.append(pl.BlockSpec((8, tn), lambda j, i, k: (0, j)))
    res = pl.pallas_call(
        body, name=name, grid=(Ko // tn, ni, nk), in_specs=in_specs, out_specs=out_specs, out_shape=out_shape,
        scratch_shapes=[pltpu.VMEM((tm, tn), F32)] if nk > 1 else [],
        compiler_params=_cp(("parallel", "arbitrary", "arbitrary")))(*args)
    return res if dact is not None else res[0]


def mm_tn(a, b, *, name, tm=1024, tn=1024, tk=1024):
    T, M = a.shape
    N = b.shape[1]
    tm, tn, tk = min(tm, M), min(tn, N), min(tk, T)

    def body(a_ref, b_ref, o_ref):
        @pl.when(pl.program_id(2) == 0)
        def _():
            o_ref[...] = jnp.zeros_like(o_ref)

        o_ref[...] += lax.dot_general(a_ref[...].astype(MM), b_ref[...].astype(MM), (((0,), (0,)), ((), ())),
                                      preferred_element_type=F32)

    return pl.pallas_call(
        body, name=name, grid=(M // tm, N // tn, T // tk),
        in_specs=[pl.BlockSpec((tk, tm), lambda i, j, k: (k, i)), pl.BlockSpec((tk, tn), lambda i, j, k: (k, j))],
        out_specs=pl.BlockSpec((tm, tn), lambda i, j, k: (i, j)), out_shape=_sds((M, N)),
        compiler_params=_cp(("parallel", "parallel", "arbitrary")))(a, b)


def _sigmoid(x):
    return 1.0 / (1.0 + jnp.exp(-x))


def _silu(x):
    return x * _sigmoid(x)


def _softplus(x):
    return jnp.maximum(x, 0.0) + jnp.log1p(jnp.exp(-jnp.abs(x)))


def _ext(main_ref, prev_ref, next_ref, first, last):
    p = jnp.where(first, 0.0, prev_ref[...].astype(F32))
    n = jnp.where(last, 0.0, next_ref[...].astype(F32))
    return jnp.concatenate([p, main_ref[...].astype(F32), n], axis=0)


def _shift_dn(x):
    return pltpu.roll(x, 1, 0)


def _shift_up(x):
    return pltpu.roll(x, x.shape[0] - 1, 0)


def _conv3(xe, w):
    return w[0:1, :] * _shift_dn(xe) + w[1:2, :] * xe + w[2:3, :] * _shift_up(xe)


def _conv3_t(de, w):
    return w[0:1, :] * _shift_up(de) + w[1:2, :] * de + w[2:3, :] * _shift_dn(de)


def _halo_specs(bt, T, col, lead=None):
    r = bt // HALO
    last = T // HALO - 1
    if lead is None:
        return [pl.BlockSpec((bt, D), lambda i: (i, col)),
                pl.BlockSpec((HALO, D), lambda i: (jnp.maximum(i * r - 1, 0), col)),
                pl.BlockSpec((HALO, D), lambda i: (jnp.minimum((i + 1) * r, last), col))]
    return [pl.BlockSpec((lead, bt, D), lambda i: (0, i, col)),
            pl.BlockSpec((lead, HALO, D), lambda i: (0, jnp.maximum(i * r - 1, 0), col)),
            pl.BlockSpec((lead, HALO, D), lambda i: (0, jnp.minimum((i + 1) * r, last), col))]


def _qkv_rows(cq, ck, cv):
    sq, sk, sv = _silu(cq), _silu(ck), _silu(cv)
    qs, ks = [], []
    for h in range(NH):
        s = slice(h * HD, (h + 1) * HD)
        qh, kh = sq[:, s], sk[:, s]
        qs.append(qh * lax.rsqrt(jnp.sum(qh * qh, axis=-1, keepdims=True) + L2_EPS) * (HD ** -0.5))
        ks.append(kh * lax.rsqrt(jnp.sum(kh * kh, axis=-1, keepdims=True) + L2_EPS))
    return jnp.concatenate(qs, axis=1), jnp.concatenate(ks, axis=1), sv


def _chunk_masks(bt):
    row = lax.broadcasted_iota(jnp.int32, (bt, bt), 0)
    col = lax.broadcasted_iota(jnp.int32, (bt, bt), 1)
    same = (row // CH) == (col // CH)
    lower = jnp.where(same & (col <= row), 1.0, 0.0).astype(F32)
    upper = jnp.where(same & (col >= row), 1.0, 0.0).astype(F32)
    return lower, upper


def _gate_rows(ab, gp, lower, upper):
    lane = lax.broadcasted_iota(jnp.int32, ab.shape, 1)
    g = -jnp.exp(gp[0:1, :]) * _softplus(ab + gp[1:2, :])
    g = jnp.where(lane < 2 * NH, g, 0.0)
    gf = jnp.dot(lower, g, precision=HI, preferred_element_type=F32)
    gr = jnp.dot(upper, g, precision=HI, preferred_element_type=F32)
    gc = jnp.where(lane < NH, gf, gr)
    beta = _sigmoid(ab)
    return jnp.where(lane < 2 * NH, gc, jnp.where(lane < 4 * NH, beta, 0.0))


def pre_qkv_fwd(pm, pab, cw, gp, *, bt=256):
    T = pm.shape[0]
    bt = min(bt, T)
    n = T // bt

    def body(q0, q1, q2, k0, k1, k2, v0, v1, v2, ab_ref, cw_ref, gp_ref, q_ref, k_ref, v_ref, gb_ref):
        i = pl.program_id(0)
        first, last = i == 0, i == n - 1
        cs = []
        for c, (m, p, x) in enumerate(((q0, q1, q2), (k0, k1, k2), (v0, v1, v2))):
            xe = _ext(m, p, x, first, last)
            cs.append(_conv3(xe, cw_ref[:, c * D:(c + 1) * D])[HALO:HALO + bt])
        q, k, v = _qkv_rows(*cs)
        q_ref[...], k_ref[...], v_ref[...] = q, k, v
        lower, upper = _chunk_masks(bt)
        gb_ref[...] = _gate_rows(ab_ref[...], gp_ref[...], lower, upper)

    in_specs = (_halo_specs(bt, T, CQ) + _halo_specs(bt, T, CK) + _halo_specs(bt, T, CV)
                + [pl.BlockSpec((bt, 128), lambda i: (i, 0)), pl.BlockSpec((8, 3 * D), lambda i: (0, 0)),
                   pl.BlockSpec((8, 128), lambda i: (0, 0))])
    row = pl.BlockSpec((bt, D), lambda i: (i, 0))
    return pl.pallas_call(
        body, name="pre_qkv_fwd", grid=(n,), in_specs=in_specs,
        out_specs=[row, row, row, pl.BlockSpec((bt, 128), lambda i: (i, 0))],
        out_shape=[_sds((T, D)), _sds((T, D)), _sds((T, D)), _sds((T, 128))],
        compiler_params=_cp(("parallel",)))(*([pm] * 9), pab, cw, gp)


def pre_qkv_bwd(pm, pab, cw, gp, dq2, dk2, dv2, dgb2, dpm, *, bt=128):
    T = pm.shape[0]
    bt = min(bt, T)
    n = T // bt
    E = bt + 2 * HALO

    def body(*refs):
        it = iter(refs)
        xs = [[next(it) for _ in range(3)] for _ in range(3)]
        ds = [[next(it) for _ in range(3)] for _ in range(3)]
        ab_ref, dgb_ref, cw_ref, gp_ref, _alias = next(it), next(it), next(it), next(it), next(it)
        o_ref, dab_ref, dcw_ref, dgp_ref = (next(it) for _ in range(4))
        i = pl.program_id(0)
        first, last = i == 0, i == n - 1

        @pl.when(first)
        def _():
            dcw_ref[...] = jnp.zeros_like(dcw_ref)
            dgp_ref[...] = jnp.zeros_like(dgp_ref)

        xes = [_ext(*xs[c], first, last) for c in range(3)]
        ces = [_conv3(xes[c], cw_ref[:, c * D:(c + 1) * D]) for c in range(3)]
        cts = []
        for c in range(3):
            m, p, x = ds[c]
            pe = jnp.where(first, 0.0, p[0] + p[1])
            ne = jnp.where(last, 0.0, x[0] + x[1])
            cts.append(jnp.concatenate([pe, m[0] + m[1], ne], axis=0))
        _, vjp = jax.vjp(_qkv_rows, *ces)
        dces = vjp(tuple(cts))
        rowi = lax.broadcasted_iota(jnp.int32, (E, 1), 0)
        central = (rowi >= HALO) & (rowi < HALO + bt)
        row8 = lax.broadcasted_iota(jnp.int32, (8, D), 0)
        for c in range(3):
            w = cw_ref[:, c * D:(c + 1) * D]
            o_ref[:, c * D:(c + 1) * D] = _conv3_t(dces[c], w)[HALO:HALO + bt].astype(o_ref.dtype)
            dc = jnp.where(central, dces[c], 0.0)
            taps = (jnp.sum(dc * _shift_dn(xes[c]), axis=0, keepdims=True),
                    jnp.sum(dc * xes[c], axis=0, keepdims=True),
                    jnp.sum(dc * _shift_up(xes[c]), axis=0, keepdims=True))
            upd = jnp.where(row8 == 0, taps[0], jnp.where(row8 == 1, taps[1], jnp.where(row8 == 2, taps[2], 0.0)))
            dcw_ref[:, c * D:(c + 1) * D] += upd
        lower, upper = _chunk_masks(bt)
        _, gvjp = jax.vjp(lambda ab, gp: _gate_rows(ab, gp, lower, upper), ab_ref[...], gp_ref[...])
        dab, dgp = gvjp(dgb_ref[0] + dgb_ref[1])
        dab_ref[...] = dab
        dgp_ref[...] += dgp

    in_specs = (_halo_specs(bt, T, CQ) + _halo_specs(bt, T, CK) + _halo_specs(bt, T, CV)
                + _halo_specs(bt, T, 0, lead=2) * 3
                + [pl.BlockSpec((bt, 128), lambda i: (i, 0)), pl.BlockSpec((2, bt, 128), lambda i: (0, i, 0)),
                   pl.BlockSpec((8, 3 * D), lambda i: (0, 0)), pl.BlockSpec((8, 128), lambda i: (0, 0)),
                   pl.BlockSpec(memory_space=pl.ANY)])
    out_specs = [pl.BlockSpec((bt, 3 * D), lambda i: (i, 0)), pl.BlockSpec((bt, 128), lambda i: (i, 0)),
                 pl.BlockSpec((8, 3 * D), lambda i: (0, 0)), pl.BlockSpec((8, 128), lambda i: (0, 0))]
    return pl.pallas_call(
        body, name="pre_qkv_bwd", grid=(n,), in_specs=in_specs, out_specs=out_specs,
        out_shape=[_sds(dpm.shape, dpm.dtype), _sds((T, 128)), _sds((8, 3 * D)), _sds((8, 128))],
        input_output_aliases={len(in_specs) - 1: 0},
        compiler_params=_cp(("arbitrary",)))(
            *([pm] * 9), dq2, dq2, dq2, dk2, dk2, dk2, dv2, dv2, dv2, pab, dgb2, cw, gp, dpm)


def pre_sc_fwd(pm, cw, *, bt=256):
    T = pm.shape[0]
    bt = min(bt, T)
    n = T // bt

    def body(b_ref, c0, c1, c2, x0, x1, x2, cw_ref, o_ref):
        i = pl.program_id(0)
        first, last = i == 0, i == n - 1
        pe = _ext(c0, c1, c2, first, last) * _ext(x0, x1, x2, first, last)
        o_ref[...] = (b_ref[...] * _conv3(pe, cw_ref[...])[HALO:HALO + bt]).astype(o_ref.dtype)

    in_specs = ([pl.BlockSpec((bt, D), lambda i: (i, CSB))] + _halo_specs(bt, T, CSC) + _halo_specs(bt, T, CSX)
                + [pl.BlockSpec((8, D), lambda i: (0, 0))])
    return pl.pallas_call(
        body, name="pre_sc_fwd", grid=(n,), in_specs=in_specs, out_specs=pl.BlockSpec((bt, D), lambda i: (i, 0)),
        out_shape=_sds((T, D), MM), compiler_params=_cp(("parallel",)))(*([pm] * 7), cw)


def pre_sc_bwd(pm, cw, dsc, dpm, *, bt=256):
    T = pm.shape[0]
    bt = min(bt, T)
    n = T // bt
    E = bt + 2 * HALO

    def body(b0, b1, b2, c0, c1, c2, x0, x1, x2, d0, d1, d2, cw_ref, _alias, o_ref, dcw_ref):
        i = pl.program_id(0)
        first, last = i == 0, i == n - 1

        @pl.when(first)
        def _():
            dcw_ref[...] = jnp.zeros_like(dcw_ref)

        ce, xe = _ext(c0, c1, c2, first, last), _ext(x0, x1, x2, first, last)
        pe = ce * xe
        w = cw_ref[...]
        dout = d0[...]
        o_ref[:, 0:D] = (dout * _conv3(pe, w)[HALO:HALO + bt]).astype(o_ref.dtype)
        dce = _ext(d0, d1, d2, first, last) * _ext(b0, b1, b2, first, last)
        dp = _conv3_t(dce, w)[HALO:HALO + bt]
        o_ref[:, D:2 * D] = (dp * x0[...]).astype(o_ref.dtype)
        o_ref[:, 2 * D:3 * D] = (dp * c0[...]).astype(o_ref.dtype)
        rowi = lax.broadcasted_iota(jnp.int32, (E, 1), 0)
        dc = jnp.where((rowi >= HALO) & (rowi < HALO + bt), dce, 0.0)
        row8 = lax.broadcasted_iota(jnp.int32, (8, D), 0)
        taps = (jnp.sum(dc * _shift_dn(pe), axis=0, keepdims=True), jnp.sum(dc * pe, axis=0, keepdims=True),
                jnp.sum(dc * _shift_up(pe), axis=0, keepdims=True))
        dcw_ref[...] += jnp.where(row8 == 0, taps[0], jnp.where(row8 == 1, taps[1], jnp.where(row8 == 2, taps[2], 0.0)))

    dsc_specs = [pl.BlockSpec((bt, D), lambda i: (i, 0)),
                 pl.BlockSpec((HALO, D), lambda i: (jnp.maximum(i * (bt // HALO) - 1, 0), 0)),
                 pl.BlockSpec((HALO, D), lambda i: (jnp.minimum((i + 1) * (bt // HALO), T // HALO - 1), 0))]
    in_specs = (_halo_specs(bt, T, CSB) + _halo_specs(bt, T, CSC) + _halo_specs(bt, T, CSX) + dsc_specs
                + [pl.BlockSpec((8, D), lambda i: (0, 0)), pl.BlockSpec(memory_space=pl.ANY)])
    return pl.pallas_call(
        body, name="pre_sc_bwd", grid=(n,), in_specs=in_specs,
        out_specs=[pl.BlockSpec((bt, 3 * D), lambda i: (i, 1)), pl.BlockSpec((8, D), lambda i: (0, 0))],
        out_shape=[_sds(dpm.shape, dpm.dtype), _sds((8, D))], input_output_aliases={len(in_specs) - 1: 0},
        compiler_params=_cp(("arbitrary",)))(*([pm] * 9), dsc, dsc, dsc, cw, dpm)


def _bdot(a, b, dims):
    return lax.dot_general(a.astype(MM), b.astype(MM), (dims, ((), ())), preferred_element_type=F32)


_NN, _NT, _TN = ((1,), (0,)), ((1,), (1,)), ((0,), (0,))


def _raw_nn(a, b):
    return _bdot(a, b, _NN)


def _raw_nt(a, b):
    return _bdot(a, b, _NT)


def _raw_tn(a, b):
    return _bdot(a, b, _TN)


def _make_vjp_ops():
    @jax.custom_vjp
    def nn(a, b):
        return _raw_nn(a, b)

    @jax.custom_vjp
    def nt(a, b):
        return _raw_nt(a, b)

    @jax.custom_vjp
    def tn(a, b):
        return _raw_tn(a, b)

    nn.defvjp(lambda a, b: (_raw_nn(a, b), (a, b)), lambda r, g: (_raw_nt(g, r[1]), _raw_tn(r[0], g)))
    nt.defvjp(lambda a, b: (_raw_nt(a, b), (a, b)), lambda r, g: (_raw_nn(g, r[1]), _raw_tn(g, r[0])))
    tn.defvjp(lambda a, b: (_raw_tn(a, b), (a, b)), lambda r, g: (_raw_nt(r[1], g), _raw_nn(r[0], g)))

    @jax.custom_vjp
    def inv_saved(A, Y):
        return Y

    def inv_bwd(Y, g):
        M = g + _raw_tn(Y, g)
        return -(M + _raw_nt(M, Y)), jnp.zeros_like(Y)

    inv_saved.defvjp(lambda A, Y: (Y, Y), inv_bwd)
    return nn, nt, tn, inv_saved


GH = 4
GR = GH * CH
NG = NH // GH
CBI = 2


def _scan_chunks(n_chunks):
    return 4 if n_chunks % 4 == 0 else 2


def _tri_inv_y_all(As):
    Ys = [-A for A in As]
    Ps = [_raw_nn(A, A) for A in As]
    for stage in range(5):
        squares = [_raw_nn(P, P) for P in Ps] if stage < 4 else None
        Ys = [Y + P + _raw_nn(Y, P) for Y, P in zip(Ys, Ps)]
        Ps = squares
    return Ys


def _intra_groups(chains, incl, strict, eye, lastc, ops, inv_all):
    nn, nt, tn = ops
    st = []
    for qs, ks, vs, gcols, bcols in chains:
        q, k, v = (jnp.concatenate(t, axis=0) for t in (qs, ks, vs))
        gcol, bcol = jnp.concatenate(gcols, axis=0), jnp.concatenate(bcols, axis=0)
        grow = jnp.sum(eye * gcol, axis=0, keepdims=True)
        gam = jnp.where(incl, jnp.exp(jnp.where(incl, gcol - grow, 0.0)), 0.0)
        glast = jnp.sum(jnp.where(lastc, grow, 0.0), axis=1, keepdims=True)
        st.append((q, k, v, gcol, bcol, gam, glast, jnp.exp(gcol), k * bcol))
    As = [jnp.where(strict, nt(kb, k) * gam, 0.0) for (q, k, v, gcol, bcol, gam, glast, eg, kb) in st]
    Ys = inv_all(As)
    vbs = [v * bcol for (q, k, v, gcol, bcol, gam, glast, eg, kb) in st]
    kgs = [kb * eg for (q, k, v, gcol, bcol, gam, glast, eg, kb) in st]
    us = [vb + nn(Y, vb) for Y, vb in zip(Ys, vbs)]
    ws = [kg + nn(Y, kg) for Y, kg in zip(Ys, kgs)]
    Ps = [nt(q, k) * gam for (q, k, v, gcol, bcol, gam, glast, eg, kb) in st]
    return [((u, w, P, q * eg, k * jnp.exp(glast - gcol), jnp.exp(glast)), Y)
            for u, w, P, Y, (q, k, v, gcol, bcol, gam, glast, eg, kb) in zip(us, ws, Ps, Ys, st)]


def _scan_groups(chains, ops):
    nn, nt, tn, _ = ops
    vns = [[us[j] - nn(ws[j], Ss[j]) for j in range(GH)] for Ss, us, ws, P, qds, kds, egls in chains]
    os_ = [jnp.concatenate([nn(qds[j], Ss[j]) for j in range(GH)], axis=0) + nn(P, jnp.concatenate(vn, axis=0))
           for (Ss, us, ws, P, qds, kds, egls), vn in zip(chains, vns)]
    S2s = [[Ss[j] * egls[j] + tn(kds[j], vn[j]) for j in range(GH)]
           for (Ss, us, ws, P, qds, kds, egls), vn in zip(chains, vns)]
    return list(zip(os_, S2s))


def _group_masks(rev):
    r = lax.broadcasted_iota(jnp.int32, (GR, GR), 0)
    c = lax.broadcasted_iota(jnp.int32, (GR, GR), 1)
    same = (r // CH) == (c // CH)
    ahead = jnp.where(rev, c - r, r - c)
    incl = same & (ahead >= 0)
    strict = same & (ahead > 0)
    eye = jnp.where(r == c, 1.0, 0.0).astype(F32)
    lastc = same & ((c % CH) == jnp.where(rev, 0, CH - 1))
    return incl, strict, eye, lastc


def _head_gates(gb, h, rev):
    gcol = jnp.where(rev, gb[:, NH + h:NH + h + 1], gb[:, h:h + 1])
    bcol = jnp.where(rev, gb[:, 3 * NH + h:3 * NH + h + 1], gb[:, 2 * NH + h:2 * NH + h + 1])
    return gcol, bcol


def _hs(h):
    return slice(h * HD, (h + 1) * HD)


def _load_chain(q_ref, k_ref, v_ref, gb_ref, c, g, rev):
    r = slice(c * CH, (c + 1) * CH)
    heads = range(g * GH, (g + 1) * GH)
    gates = [_head_gates(gb_ref[r, :], h, rev) for h in heads]
    return ([q_ref[r, _hs(h)] for h in heads], [k_ref[r, _hs(h)] for h in heads], [v_ref[r, _hs(h)] for h in heads],
            [t[0] for t in gates], [t[1] for t in gates])


def gdn_intra_fwd(q, k, v, gb):
    T = q.shape[0]
    N = T // CH
    ops = (_raw_nn, _raw_nt, _raw_tn)

    def body(q_ref, k_ref, v_ref, gb_ref, u_ref, w_ref, qd_ref, kd_ref, pp_ref, ys_ref, eg_ref):
        rev = pl.program_id(0) == 1
        masks = _group_masks(rev)
        where = [(c, g) for c in range(CBI) for g in range(NG)]
        chains = [_load_chain(q_ref, k_ref, v_ref, gb_ref, c, g, rev) for c, g in where]
        for (c, g), ((u, w, P, qd, kd, egl), Y) in zip(where, _intra_groups(chains, *masks, ops, _tri_inv_y_all)):
            r = slice(c * CH, (c + 1) * CH)
            pp_ref[c, g] = P.astype(MM)
            ys_ref[c, g] = Y.astype(MM)
            for j, h in enumerate(range(g * GH, (g + 1) * GH)):
                rows = slice(j * CH, (j + 1) * CH)
                u_ref[r, _hs(h)] = u[rows]
                w_ref[r, _hs(h)] = w[rows].astype(MM)
                qd_ref[r, _hs(h)] = qd[rows].astype(MM)
                kd_ref[r, _hs(h)] = kd[rows].astype(MM)
                eg_ref[c, h:h + 1, :] = jnp.broadcast_to(egl[j * CH:j * CH + 1, :], (1, 128))

    row = pl.BlockSpec((CBI * CH, D), lambda d, n: (n, 0))
    drow = pl.BlockSpec((None, CBI * CH, D), lambda d, n: (d, n, 0))
    mat = pl.BlockSpec((None, CBI, NG, GR, GR), lambda d, n: (d, n, 0, 0, 0))
    return pl.pallas_call(
        body, name="gdn_intra_fwd", grid=(2, N // CBI),
        in_specs=[row, row, row, pl.BlockSpec((CBI * CH, 128), lambda d, n: (n, 0))],
        out_specs=[drow] * 4 + [mat, mat, pl.BlockSpec((None, CBI, NH, 128), lambda d, n: (d, n, 0, 0))],
        out_shape=[_sds((2, T, D))] + [_sds((2, T, D), MM)] * 3 + [_sds((2, N, NG, GR, GR), MM)] * 2
                  + [_sds((2, N, NH, 128))],
        compiler_params=_cp(("parallel", "parallel")))(q, k, v, gb)


def gdn_scan_fwd(u, w, qd, kd, pp, eg):
    T = u.shape[1]
    N = T // CH
    ops = (_raw_nn, _raw_nt, _raw_tn, None)

    cbs = _scan_chunks(N)
    NB = N // cbs

    def body(u_ref, w_ref, qd_ref, kd_ref, pp_ref, eg_ref, o_ref, s0_ref, S):
        d = pl.program_id(0)

        @pl.when(pl.program_id(1) == 0)
        def _():
            S[...] = jnp.zeros_like(S)

        def chunk(c, carry):
            pc = c + d * (cbs - 1 - 2 * c)
            r = pl.ds(pl.multiple_of(pc * CH, CH), CH)
            chains = []
            for g in range(NG):
                heads = range(g * GH, (g + 1) * GH)
                Ss = [S[h] for h in heads]
                for h, Sh in zip(heads, Ss):
                    s0_ref[pc, h] = Sh
                chains.append((Ss, [u_ref[r, _hs(h)] for h in heads], [w_ref[r, _hs(h)] for h in heads], pp_ref[pc, g],
                               [qd_ref[r, _hs(h)] for h in heads], [kd_ref[r, _hs(h)] for h in heads],
                               [eg_ref[pc, h:h + 1, :] for h in heads]))
            for g, (o, S2) in enumerate(_scan_groups(chains, ops)):
                for j, h in enumerate(range(g * GH, (g + 1) * GH)):
                    o_ref[r, _hs(h)] = o[j * CH:(j + 1) * CH]
                    S[h] = S2[j]
            return carry

        lax.fori_loop(0, cbs, chunk, 0)

    bidx = lambda d, n: n + d * (NB - 1 - 2 * n)
    drow = pl.BlockSpec((None, cbs * CH, D), lambda d, n: (d, bidx(d, n), 0))
    mat = pl.BlockSpec((None, cbs, NG, GR, GR), lambda d, n: (d, bidx(d, n), 0, 0, 0))
    return pl.pallas_call(
        body, name="gdn_scan_fwd", grid=(2, NB),
        in_specs=[drow] * 4 + [mat, pl.BlockSpec((None, cbs, NH, 128), lambda d, n: (d, bidx(d, n), 0, 0))],
        out_specs=[drow, pl.BlockSpec((None, cbs, NH, HD, HD), lambda d, n: (d, bidx(d, n), 0, 0, 0))],
        out_shape=[_sds((2, T, D)), _sds((2, N, NH, HD, HD))],
        scratch_shapes=[pltpu.VMEM((NH, HD, HD), F32)],
        compiler_params=_cp(("arbitrary", "arbitrary")))(u, w, qd, kd, pp, eg)


def gdn_scan_bwd(u, w, qd, kd, pp, eg, s0, do):
    T = u.shape[1]
    N = T // CH
    ops = _make_vjp_ops()
    cbs = _scan_chunks(N)
    NB = N // cbs

    def body(u_ref, w_ref, qd_ref, kd_ref, pp_ref, eg_ref, s0_ref, do_ref,
             du_ref, dw_ref, dqd_ref, dkd_ref, dpp_ref, deg_ref, dS):
        d = pl.program_id(0)

        @pl.when(pl.program_id(1) == 0)
        def _():
            dS[...] = jnp.zeros_like(dS)

        def chunk(c, carry):
            pc = (cbs - 1 - c) + d * (2 * c - (cbs - 1))
            r = pl.ds(pl.multiple_of(pc * CH, CH), CH)
            chains, cts = [], []
            for g in range(NG):
                heads = range(g * GH, (g + 1) * GH)
                chains.append(([s0_ref[pc, h] for h in heads], [u_ref[r, _hs(h)] for h in heads],
                               [w_ref[r, _hs(h)].astype(F32) for h in heads], pp_ref[pc, g].astype(F32),
                               [qd_ref[r, _hs(h)].astype(F32) for h in heads],
                               [kd_ref[r, _hs(h)].astype(F32) for h in heads], [eg_ref[pc, h:h + 1, :] for h in heads]))
                cts.append((jnp.concatenate([do_ref[r, _hs(h)].astype(F32) for h in heads], axis=0),
                            [dS[h] for h in heads]))
            _, vjp = jax.vjp(lambda ch: _scan_groups(ch, ops), chains)
            (dchains,) = vjp(cts)
            for g, (dSs, dus, dws, dP, dqds, dkds, degs) in enumerate(dchains):
                dpp_ref[pc, g] = dP
                for j, h in enumerate(range(g * GH, (g + 1) * GH)):
                    dS[h] = dSs[j]
                    du_ref[r, _hs(h)] = dus[j].astype(MM)
                    dw_ref[r, _hs(h)] = dws[j].astype(MM)
                    dqd_ref[r, _hs(h)], dkd_ref[r, _hs(h)] = dqds[j], dkds[j]
                    deg_ref[pc, h:h + 1, :] = degs[j]
            return carry

        lax.fori_loop(0, cbs, chunk, 0)

    bidx = lambda d, n: (NB - 1 - n) + d * (2 * n - (NB - 1))
    drow = pl.BlockSpec((None, cbs * CH, D), lambda d, n: (d, bidx(d, n), 0))
    erow = pl.BlockSpec((None, cbs, NH, 128), lambda d, n: (d, bidx(d, n), 0, 0))
    mat = pl.BlockSpec((None, cbs, NG, GR, GR), lambda d, n: (d, bidx(d, n), 0, 0, 0))
    return pl.pallas_call(
        body, name="gdn_scan_bwd", grid=(2, NB),
        in_specs=[drow] * 4 + [mat, erow, pl.BlockSpec((None, cbs, NH, HD, HD), lambda d, n: (d, bidx(d, n), 0, 0, 0)),
                               pl.BlockSpec((cbs * CH, D), lambda d, n: (bidx(d, n), 0))],
        out_specs=[drow] * 4 + [mat, erow],
        out_shape=[_sds((2, T, D), MM)] * 2 + [_sds((2, T, D))] * 2 + [_sds((2, N, NG, GR, GR))]
                  + [_sds((2, N, NH, 128))],
        scratch_shapes=[pltpu.VMEM((NH, HD, HD), F32)],
        compiler_params=_cp(("arbitrary", "arbitrary")))(u, w, qd, kd, pp, eg, s0, do)


def gdn_intra_bwd(q, k, v, gb, ys, du, dw, dqd, dkd, dpp, deg):
    T = q.shape[0]
    N = T // CH
    nn, nt, tn, inv_saved = _make_vjp_ops()

    def body(q_ref, k_ref, v_ref, gb_ref, ys_ref, du_ref, dw_ref, dqd_ref, dkd_ref, dpp_ref, deg_ref,
             dq_ref, dk_ref, dv_ref, dgb_ref):
        rev = pl.program_id(0) == 1
        masks = _group_masks(rev)
        lane = lax.broadcasted_iota(jnp.int32, (CH, 128), 1)
        grow = lax.broadcasted_iota(jnp.int32, (GR, 1), 0)
        where = [(c, g) for c in range(CBI) for g in range(NG)]
        chains = [_load_chain(q_ref, k_ref, v_ref, gb_ref, c, g, rev) for c, g in where]
        Ys = [ys_ref[c, g].astype(F32) for c, g in where]
        inv_all = lambda As: [inv_saved(A, Y) for A, Y in zip(As, Ys)]
        _, vjp = jax.vjp(lambda ch: _intra_groups(ch, *masks, (nn, nt, tn), inv_all), chains)
        cts = []
        for c, g in where:
            r = slice(c * CH, (c + 1) * CH)
            heads = range(g * GH, (g + 1) * GH)
            stack = lambda ref: jnp.concatenate([ref[r, _hs(h)].astype(F32) for h in heads], axis=0)
            degl = jnp.zeros((GR, 1), F32)
            for j, h in enumerate(heads):
                degl = degl + jnp.where(grow == j * CH, jnp.sum(deg_ref[c, h:h + 1, :], axis=1, keepdims=True), 0.0)
            cts.append(((stack(du_ref), stack(dw_ref), dpp_ref[c, g], stack(dqd_ref), stack(dkd_ref), degl),
                        jnp.zeros((GR, GR), F32)))
        (dchains,) = vjp(cts)
        dgbs = [jnp.zeros((CH, 128), F32) for _ in range(CBI)]
        for (c, g), (dqs, dks, dvs, dgs, dbs) in zip(where, dchains):
            r = slice(c * CH, (c + 1) * CH)
            for j, h in enumerate(range(g * GH, (g + 1) * GH)):
                dq_ref[r, _hs(h)], dk_ref[r, _hs(h)], dv_ref[r, _hs(h)] = dqs[j], dks[j], dvs[j]
                glane = jnp.where(rev, NH + h, h)
                dgbs[c] = dgbs[c] + jnp.where(lane == glane, dgs[j], 0.0) + jnp.where(lane == glane + 2 * NH, dbs[j], 0.0)
        for c in range(CBI):
            dgb_ref[c * CH:(c + 1) * CH, :] = dgbs[c]

    row = pl.BlockSpec((CBI * CH, D), lambda d, n: (n, 0))
    drow = pl.BlockSpec((None, CBI * CH, D), lambda d, n: (d, n, 0))
    mat = pl.BlockSpec((None, CBI, NG, GR, GR), lambda d, n: (d, n, 0, 0, 0))
    return pl.pallas_call(
        body, name="gdn_intra_bwd", grid=(2, N // CBI),
        in_specs=[row, row, row, pl.BlockSpec((CBI * CH, 128), lambda d, n: (n, 0)), mat, drow, drow, drow, drow, mat]
                 + [pl.BlockSpec((None, CBI, NH, 128), lambda d, n: (d, n, 0, 0))],
        out_specs=[drow, drow, drow, pl.BlockSpec((None, CBI * CH, 128), lambda d, n: (d, n, 0))],
        out_shape=[_sds((2, T, D))] * 3 + [_sds((2, T, 128))],
        compiler_params=_cp(("parallel", "parallel")))(q, k, v, gb, ys, du, dw, dqd, dkd, dpp, deg)


def _post_rows(o2a, o2b, z, nw):
    o = o2a + o2b
    outs = []
    for h in range(NH):
        s = slice(h * HD, (h + 1) * HD)
        oh = o[:, s]
        outs.append(oh * lax.rsqrt(jnp.mean(oh * oh, axis=-1, keepdims=True) + RMS_EPS) * nw * _silu(z[:, s]))
    return jnp.concatenate(outs, axis=1)


def post_fwd(o2, pm, nw, *, bt=512):
    T = pm.shape[0]
    bt = min(bt, T)

    def body(o_ref, z_ref, nw_ref, og_ref):
        og_ref[...] = _post_rows(o_ref[0], o_ref[1], z_ref[...], nw_ref[0:1, :]).astype(og_ref.dtype)

    return pl.pallas_call(
        body, name="post_fwd", grid=(T // bt,),
        in_specs=[pl.BlockSpec((2, bt, D), lambda i: (0, i, 0)), pl.BlockSpec((bt, D), lambda i: (i, CZ)),
                  pl.BlockSpec((8, 128), lambda i: (0, 0))],
        out_specs=pl.BlockSpec((bt, D), lambda i: (i, 0)), out_shape=_sds((T, D), MM),
        compiler_params=_cp(("parallel",)))(o2, pm, nw)


def post_bwd(o2, pm, nw, dog, dpm, *, bt=512):
    T = pm.shape[0]
    bt = min(bt, T)

    def body(o_ref, z_ref, nw_ref, dog_ref, _alias, do_ref, dz_ref, dnw_ref):
        @pl.when(pl.program_id(0) == 0)
        def _():
            dnw_ref[...] = jnp.zeros_like(dnw_ref)

        _, vjp = jax.vjp(_post_rows, o_ref[0], o_ref[1], z_ref[...], nw_ref[0:1, :])
        doa, _unused, dz, dnw = vjp(dog_ref[...])
        do_ref[...] = doa.astype(do_ref.dtype)
        dz_ref[...] = dz.astype(dz_ref.dtype)
        row8 = lax.broadcasted_iota(jnp.int32, (8, 128), 0)
        dnw_ref[...] += jnp.where(row8 == 0, dnw, 0.0)

    in_specs = [pl.BlockSpec((2, bt, D), lambda i: (0, i, 0)), pl.BlockSpec((bt, D), lambda i: (i, CZ)),
                pl.BlockSpec((8, 128), lambda i: (0, 0)), pl.BlockSpec((bt, D), lambda i: (i, 0)),
                pl.BlockSpec(memory_space=pl.ANY)]
    return pl.pallas_call(
        body, name="post_bwd", grid=(T // bt,), in_specs=in_specs,
        out_specs=[pl.BlockSpec((bt, D), lambda i: (i, 0)), pl.BlockSpec((bt, D), lambda i: (i, CZ)),
                   pl.BlockSpec((8, 128), lambda i: (0, 0))],
        out_shape=[_sds((T, D), MM), _sds(dpm.shape, dpm.dtype), _sds((8, 128))], input_output_aliases={4: 1},
        compiler_params=_cp(("arbitrary",)))(o2, pm, nw, dog, dpm)


def merge_fwd(ya, yb, pm, *, bt=512):
    T = pm.shape[0]
    bt = min(bt, T)

    def body(ya_ref, yb_ref, ga_ref, gb_ref, o_ref):
        o_ref[...] = (_sigmoid(ga_ref[...]) * ya_ref[...] + _sigmoid(gb_ref[...]) * yb_ref[...]).astype(o_ref.dtype)

    row = pl.BlockSpec((bt, D), lambda i: (i, 0))
    return pl.pallas_call(
        body, name="merge_fwd", grid=(T // bt,),
        in_specs=[row, row, pl.BlockSpec((bt, D), lambda i: (i, CGA)), pl.BlockSpec((bt, D), lambda i: (i, CGB))],
        out_specs=row, out_shape=_sds((T, D), MM), compiler_params=_cp(("parallel",)))(ya, yb, pm, pm)


def merge_bwd(ya, yb, pm, dmix, *, bt=512):
    T = pm.shape[0]
    bt = min(bt, T)

    def body(ya_ref, yb_ref, ga_ref, gb_ref, dm_ref, dya_ref, dyb_ref, dg_ref):
        dm = dm_ref[...]
        sa, sb = _sigmoid(ga_ref[...]), _sigmoid(gb_ref[...])
        dya_ref[...] = (dm * sa).astype(dya_ref.dtype)
        dyb_ref[...] = (dm * sb).astype(dyb_ref.dtype)
        dg_ref[:, 0:D] = (dm * ya_ref[...] * sa * (1.0 - sa)).astype(dg_ref.dtype)
        dg_ref[:, D:2 * D] = (dm * yb_ref[...] * sb * (1.0 - sb)).astype(dg_ref.dtype)

    row = pl.BlockSpec((bt, D), lambda i: (i, 0))
    return pl.pallas_call(
        body, name="merge_bwd", grid=(T // bt,),
        in_specs=[row, row, pl.BlockSpec((bt, D), lambda i: (i, CGA)), pl.BlockSpec((bt, D), lambda i: (i, CGB)), row],
        out_specs=[row, row, pl.BlockSpec((bt, 2 * D), lambda i: (i, CGA // 2))],
        out_shape=[_sds((T, D), MM), _sds((T, D), MM), _sds((T, NMAIN), MM)],
        compiler_params=_cp(("parallel",)))(ya, yb, pm, pm, dmix)


def _ln_rows(x, y, bias, g, b):
    r = ALPHA * x + y + bias
    mu = jnp.mean(r, axis=-1, keepdims=True)
    var = jnp.mean(jnp.square(r - mu), axis=-1, keepdims=True)
    return (r - mu) * lax.rsqrt(var + LN_EPS) * g + b


def ln_fwd(x, y, p, *, name, bt=512):
    T = x.shape[0]
    bt = min(bt, T)

    def body(x_ref, y_ref, p_ref, o_ref, ob_ref):
        r = _ln_rows(x_ref[...], y_ref[...], p_ref[0:1, :], p_ref[1:2, :], p_ref[2:3, :])
        o_ref[...] = r
        ob_ref[...] = r.astype(ob_ref.dtype)

    row = pl.BlockSpec((bt, D), lambda i: (i, 0))
    return pl.pallas_call(
        body, name=name, grid=(T // bt,), in_specs=[row, row, pl.BlockSpec((8, D), lambda i: (0, 0))],
        out_specs=[row, row], out_shape=[_sds((T, D)), _sds((T, D), MM)],
        compiler_params=_cp(("parallel",)))(x, y, p)


def ln_bwd(x, y, p, ct, ct2=None, *, name, bt=512):
    T = x.shape[0]
    bt = min(bt, T)

    def body(*refs):
        it = iter(refs)
        x_ref, y_ref, p_ref, c_ref = next(it), next(it), next(it), next(it)
        c2_ref = next(it) if ct2 is not None else None
        dxa_ref, dr_ref, dp_ref = next(it), next(it), next(it)

        @pl.when(pl.program_id(0) == 0)
        def _():
            dp_ref[...] = jnp.zeros_like(dp_ref)

        c = c_ref[...]
        if c2_ref is not None:
            c = c + c2_ref[...]
        _, vjp = jax.vjp(_ln_rows, x_ref[...], y_ref[...], p_ref[0:1, :], p_ref[1:2, :], p_ref[2:3, :])
        _dx, dy, dbias, dg, db = vjp(c)
        dxa_ref[...] = ALPHA * dy
        dr_ref[...] = dy.astype(dr_ref.dtype)
        row8 = lax.broadcasted_iota(jnp.int32, (8, D), 0)
        dp_ref[...] += jnp.where(row8 == 0, dbias, jnp.where(row8 == 1, dg, jnp.where(row8 == 2, db, 0.0)))

    row = pl.BlockSpec((bt, D), lambda i: (i, 0))
    in_specs = [row, row, pl.BlockSpec((8, D), lambda i: (0, 0)), row] + ([row] if ct2 is not None else [])
    args = [x, y, p, ct] + ([ct2] if ct2 is not None else [])
    return pl.pallas_call(
        body, name=name, grid=(T // bt,), in_specs=in_specs,
        out_specs=[row, row, pl.BlockSpec((8, D), lambda i: (0, 0))],
        out_shape=[_sds((T, D)), _sds((T, D), MM), _sds((8, D))],
        compiler_params=_cp(("arbitrary",)))(*args)


def loss_fwd_bwd(xl, target, *, bt=512):
    T = xl.shape[0]
    bt = min(bt, T)

    def body(x_ref, t_ref, l_ref, d_ref):
        @pl.when(pl.program_id(0) == 0)
        def _():
            l_ref[...] = jnp.zeros_like(l_ref)

        e = x_ref[...] - t_ref[...]
        d_ref[...] = e * (1.0 / D)
        l_ref[...] += 0.5 * jnp.sum(jnp.mean(e * e, axis=-1, keepdims=True), axis=0, keepdims=True)

    row = pl.BlockSpec((bt, D), lambda i: (i, 0))
    return pl.pallas_call(
        body, name="loss", grid=(T // bt,), in_specs=[row, row],
        out_specs=[pl.BlockSpec((8, 128), lambda i: (0, 0)), row], out_shape=[_sds((8, 128)), _sds((T, D))],
        compiler_params=_cp(("arbitrary",)))(xl, target)


def _row_tile(R, Cc, elems=1 << 18):
    if R * Cc <= elems:
        return R
    tr = 8
    while tr * 2 * Cc <= elems and R % (tr * 2) == 0:
        tr *= 2
    return tr


def adam(w, m, v, ga, gb=None, *, name):
    R, Cc = w.shape
    tr = _row_tile(R, Cc)

    def body(*refs):
        it = iter(refs)
        w_ref, m_ref, v_ref, a_ref = next(it), next(it), next(it), next(it)
        b_ref = next(it) if gb is not None else None
        g_ref, d_ref, mo_ref, vo_ref = next(it), next(it), next(it), next(it)
        g = a_ref[...]
        if b_ref is not None:
            g = g + b_ref[...]
        m2 = B1 * m_ref[...] + (1.0 - B1) * g
        v2 = B2 * v_ref[...] + (1.0 - B2) * jnp.square(g)
        m_hat = m2 / (1.0 - B1 ** STEP)
        v_hat = v2 / (1.0 - B2 ** STEP)
        g_ref[...] = g
        d_ref[...] = -LR * (m_hat / (jnp.sqrt(v_hat) + EPS) + WD * w_ref[...])
        mo_ref[...] = m2
        vo_ref[...] = v2

    blk = pl.BlockSpec((tr, Cc), lambda i: (i, 0))
    args = [w, m, v, ga] + ([gb] if gb is not None else [])
    return pl.pallas_call(
        body, name=name, grid=(R // tr,), in_specs=[blk] * len(args), out_specs=[blk] * 4,
        out_shape=[_sds((R, Cc))] * 4, compiler_params=_cp(("parallel",)))(*args)


def sum4(own, recv, *, name):
    R, Cc = own.shape
    tr = _row_tile(R, Cc)

    def body(o_ref, r_ref, out_ref):
        f = lambda t: t.astype(F32)
        out_ref[...] = ((f(o_ref[...]) + f(r_ref[0])) + f(r_ref[1])) + f(r_ref[2])

    return pl.pallas_call(
        body, name=name, grid=(R // tr,),
        in_specs=[pl.BlockSpec((tr, Cc), lambda i: (i, 0)), pl.BlockSpec((3, tr, Cc), lambda i: (0, i, 0))],
        out_specs=pl.BlockSpec((tr, Cc), lambda i: (i, 0)), out_shape=_sds((R, Cc)),
        compiler_params=_cp(("parallel",)))(own, recv)


def _place():
    return lax.axis_index("x"), lax.axis_index("y"), lax.axis_index("c")


def _other_chips(x, y):
    return [(1 - x, y), (x, 1 - y), (1 - x, 1 - y)]


_ANY = pl.BlockSpec(memory_space=pl.ANY)


def allgather_xy(arrs):
    n = len(arrs)
    halves = [a.shape[0] // 2 for a in arrs]

    def body(*refs):
        ins, outs = refs[:n], refs[n:2 * n]
        send, recv, fsend, frecv, loc = refs[2 * n:]
        x, y, c = _place()
        me = 2 * x + y
        peers = _other_chips(x, y)
        local = [pltpu.make_async_copy(ins[a], outs[a].at[me], loc.at[a]) for a in range(n)]
        for cp in local:
            cp.start()

        def over_ici(a, j, block, src=None):
            px, py = peers[j]
            mine = pl.ds(c * halves[a], halves[a])
            dst = outs[a].at[block, mine]
            return pltpu.make_async_remote_copy(
                src_ref=dst if src is None else src.at[mine], dst_ref=dst, send_sem=send.at[3 * a + j],
                recv_sem=recv.at[3 * a + j], device_id=(px, py, c), device_id_type=MESH)

        def over_d2d(a, j, half):
            px, py = peers[j]
            rows = outs[a].at[2 * px + py, pl.ds(half * halves[a], halves[a])]
            return pltpu.make_async_remote_copy(
                src_ref=rows, dst_ref=rows, send_sem=fsend.at[3 * a + j], recv_sem=frecv.at[3 * a + j],
                device_id=(x, y, 1 - c), device_id_type=MESH)

        sends = [over_ici(a, j, me, src=ins[a]) for a in range(n) for j in range(3)]
        for cp in sends:
            cp.start()
        passed = []
        for a in range(n):
            for j, (px, py) in enumerate(peers):
                over_ici(a, j, 2 * px + py).wait_recv()
                passed.append(over_d2d(a, j, c))
                passed[-1].start()
        for a in range(n):
            for j in range(3):
                over_d2d(a, j, 1 - c).wait_recv()
        for cp in sends + passed:
            cp.wait_send()
        for cp in local:
            cp.wait()

    return pl.pallas_call(
        body, name="allgather_xy", in_specs=[_ANY] * n, out_specs=[_ANY] * n,
        out_shape=[_sds((4,) + a.shape, a.dtype) for a in arrs],
        scratch_shapes=[pltpu.SemaphoreType.DMA((3 * n,))] * 4 + [pltpu.SemaphoreType.DMA((n,))],
        compiler_params=pltpu.CompilerParams(has_side_effects=True))(*arrs)


def scatter_xy(groups):
    L = len(groups[0])
    n = len(groups) * L
    flat = [g for grp in groups for g in grp]

    def body(*refs):
        ins = refs[:n]
        owns, recvs = refs[n:n + len(groups)], refs[n + len(groups):n + 2 * len(groups)]
        send, recv, loc = refs[n + 2 * len(groups):]
        x, y, c = _place()
        me = 2 * x + y
        peers = _other_chips(x, y)
        local, sends = [], []
        for a in range(len(groups)):
            for l in range(L):
                i = a * L + l
                local.append(pltpu.make_async_copy(ins[i].at[me], owns[a].at[l], loc.at[i]))
                for j, (px, py) in enumerate(peers):
                    sends.append(pltpu.make_async_remote_copy(
                        src_ref=ins[i].at[2 * px + py], dst_ref=recvs[a].at[j, l], send_sem=send.at[3 * i + j],
                        recv_sem=recv.at[3 * i + j], device_id=(px, py, c), device_id_type=MESH))
        for cp in local + sends:
            cp.start()
        for cp in sends:
            cp.wait_recv()
        for cp in sends:
            cp.wait_send()
        for cp in local:
            cp.wait()

    out_shape = ([_sds((L,) + grp[0].shape[1:], grp[0].dtype) for grp in groups]
                 + [_sds((3, L) + grp[0].shape[1:], grp[0].dtype) for grp in groups])
    outs = pl.pallas_call(
        body, name="scatter_xy", in_specs=[_ANY] * n, out_specs=[_ANY] * (2 * len(groups)), out_shape=out_shape,
        scratch_shapes=[pltpu.SemaphoreType.DMA((3 * n,)), pltpu.SemaphoreType.DMA((3 * n,)),
                        pltpu.SemaphoreType.DMA((n,))],
        compiler_params=pltpu.CompilerParams(has_side_effects=True))(*flat)
    return outs[:len(groups)], outs[len(groups):]


def swap_c(arrs):
    n = len(arrs)

    def body(*refs):
        ins, outs = refs[:n], refs[n:2 * n]
        send, recv = refs[2 * n:]
        x, y, c = _place()
        cps = [pltpu.make_async_remote_copy(src_ref=ins[a], dst_ref=outs[a], send_sem=send.at[a], recv_sem=recv.at[a],
                                            device_id=(x, y, 1 - c), device_id_type=MESH) for a in range(n)]
        for cp in cps:
            cp.start()
        for cp in cps:
            cp.wait_recv()
        for cp in cps:
            cp.wait_send()

    return pl.pallas_call(
        body, name="swap_c", in_specs=[_ANY] * n, out_specs=[_ANY] * n, out_shape=[_sds(a.shape, a.dtype) for a in arrs],
        scratch_shapes=[pltpu.SemaphoreType.DMA((n,)), pltpu.SemaphoreType.DMA((n,))],
        compiler_params=pltpu.CompilerParams(has_side_effects=True))(*arrs)


def allreduce_small(v):
    R = v.shape[0]

    def body(v_ref, o_ref, buf, send, recv):
        x, y, c = _place()
        me = 4 * x + 2 * y + c
        buf[0] = v_ref[...]

        def cp(k):
            dx, dy, dc = (k >> 2) & 1, (k >> 1) & 1, k & 1
            return pltpu.make_async_remote_copy(
                src_ref=v_ref, dst_ref=buf.at[k], send_sem=send.at[k - 1], recv_sem=recv.at[k - 1],
                device_id=(x ^ dx, y ^ dy, c ^ dc), device_id_type=MESH)

        cps = [cp(k) for k in range(1, 8)]
        for t in cps:
            t.start()
        for t in cps:
            t.wait_recv()
        acc = buf[me]
        for dev in range(1, 8):
            acc = acc + buf[jnp.bitwise_xor(me, dev)]
        o_ref[...] = acc
        for t in cps:
            t.wait_send()

    vm = pl.BlockSpec(memory_space=pltpu.VMEM)
    return pl.pallas_call(
        body, name="allreduce_small", in_specs=[vm], out_specs=vm, out_shape=_sds((R, 128)),
        scratch_shapes=[pltpu.VMEM((8, R, 128), F32), pltpu.SemaphoreType.DMA((7,)), pltpu.SemaphoreType.DMA((7,))],
        compiler_params=pltpu.CompilerParams(has_side_effects=True, vmem_limit_bytes=VMEM_LIMIT))(v)


def _rows8(*rows):
    n = rows[0].shape[-1]
    t = jnp.stack([r.reshape(n).astype(F32) for r in rows])
    return jnp.pad(t, ((0, 8 - len(rows)), (0, 0)))


def _lanes128(a):
    f = a.reshape(-1).astype(F32)
    return jnp.pad(f, (0, 128 - f.shape[0]))


def _layer_fwd(x, xb, W):
    pm = mm_nn(xb, W["w_main"], name="proj_main")
    pab = mm_nn(xb, W["w_ab"], name="proj_ab")
    q, k, v, gb = pre_qkv_fwd(pm, pab, W["cw"], W["gp"])
    sc = pre_sc_fwd(pm, W["csc"])
    u, w, qd, kd, pp, ys, eg = gdn_intra_fwd(q, k, v, gb)
    o2, s0 = gdn_scan_fwd(u, w, qd, kd, pp, eg)
    og = post_fwd(o2, pm, W["nw"])
    ya = mm_nn(og, W["w_og"], name="proj_og")
    yb = mm_nn(sc, W["w_osc"], name="proj_osc")
    mixed = merge_fwd(ya, yb, pm)
    out = mm_nn(mixed, W["w_out"], name="proj_out")
    x1, x1b = ln_fwd(x, out, W["ln1"], name="ln1_fwd")
    hpre, h = mm_nn(x1b, W["w_up"], bias=W["b_up"], relu2=True, name="mlp_up")
    dn = mm_nn(h, W["w_down"], name="mlp_down")
    x2, x2b = ln_fwd(x1, dn, W["ln2"], name="ln2_fwd")
    saved = dict(x=x, xb=xb, pm=pm, pab=pab, q=q, k=k, v=v, gb=gb, sc=sc, o2=o2, s0=s0, og=og, ya=ya, yb=yb,
                 u=u, w=w, qd=qd, kd=kd, pp=pp, ys=ys, eg=eg,
                 mixed=mixed, out=out, x1=x1, x1b=x1b, hpre=hpre, h=h, dn=dn)
    return x2, x2b, saved


def _layer_bwd(ct, W, S):
    dxa2, dr2b, dp2 = ln_bwd(S["x1"], S["dn"], W["ln2"], ct, name="ln2_bwd")
    g_down = mm_tn(S["h"], dr2b, name="dw_down")
    dhpre, db_up = mm_nt(dr2b, W["w_down"], dact=S["hpre"], out_dtype=MM, name="mlp_down_bwd")
    g_up = mm_tn(S["x1b"], dhpre, name="dw_up")
    dx1 = mm_nt(dhpre, W["w_up"], add=dxa2, name="mlp_up_bwd")
    dxa1, dr1b, dp1 = ln_bwd(S["x"], S["out"], W["ln1"], dx1, name="ln1_bwd")
    g_out = mm_tn(S["mixed"], dr1b, name="dw_out")
    dmix = mm_nt(dr1b, W["w_out"], name="proj_out_bwd")
    dya, dyb, dpm = merge_bwd(S["ya"], S["yb"], S["pm"], dmix)
    g_og = mm_tn(S["og"], dya, name="dw_og")
    g_osc = mm_tn(S["sc"], dyb, name="dw_osc")
    dog = mm_nt(dya, W["w_og"], name="proj_og_bwd")
    dsc = mm_nt(dyb, W["w_osc"], name="proj_osc_bwd")
    do, dpm, dnw = post_bwd(S["o2"], S["pm"], W["nw"], dog, dpm)
    du, dw, dqd, dkd, dpp, deg = gdn_scan_bwd(S["u"], S["w"], S["qd"], S["kd"], S["pp"], S["eg"], S["s0"], do)
    dq2, dk2, dv2, dgb2 = gdn_intra_bwd(S["q"], S["k"], S["v"], S["gb"], S["ys"], du, dw, dqd, dkd, dpp, deg)
    dpm, dpab, dcw, dgp = pre_qkv_bwd(S["pm"], S["pab"], W["cw"], W["gp"], dq2, dk2, dv2, dgb2, dpm)
    dpm, dcsc = pre_sc_bwd(S["pm"], W["csc"], dsc, dpm)
    g_main = mm_tn(S["xb"], dpm, name="dw_main")
    g_ab = mm_tn(S["xb"], dpab, name="dw_ab")
    t = mm_nt(dpab, W["w_ab"], add=dxa1, name="proj_ab_bwd")
    dx = mm_nt(dpm, W["w_main"], add=t, name="proj_main_bwd")
    g_in = jnp.concatenate([g_main[:, :3 * D], g_main[:, 8 * D:], g_ab[:, :4 * NH], g_main[:, 3 * D:8 * D]], axis=1)
    grads = dict(
        w_in=g_in, w_o_gdn=g_og, w_o_sc=g_osc, w_out=g_out, w_up=g_up, w_down=g_down,
        conv_qkv=dcw[:3], conv_sc=dcsc[:3], a_log=dgp[0, :2 * NH].reshape(2, NH), dt_bias=dgp[1, :2 * NH].reshape(2, NH),
        gdn_norm_w=dnw[0], ln1_g=dp1[1], ln1_b=dp1[2], b_up=db_up[0], b_down=dp2[0], ln2_g=dp2[1], ln2_b=dp2[2])
    return dx, grads


def _layer_weights(l, full, a_log, dt_bias, gdn_norm_w, ln1_g, ln1_b, b_up, b_down, ln2_g, ln2_b):
    w_in = full["w_in"][l]
    w_main = jnp.concatenate([w_in[:, :3 * D], w_in[:, 4 * D + 4 * NH:], w_in[:, 3 * D:4 * D]], axis=1)
    w_ab = jnp.pad(w_in[:, 4 * D:4 * D + 4 * NH], ((0, 0), (0, 128 - 4 * NH)))
    return dict(
        w_main=w_main, w_ab=w_ab, w_og=full["w_o_gdn"][l], w_osc=full["w_o_sc"][l], w_out=full["w_out"][l],
        w_up=full["w_up"][l], w_down=full["w_down"][l],
        cw=jnp.pad(full["conv_qkv"][l].astype(F32), ((0, 5), (0, 0))),
        csc=jnp.pad(full["conv_sc"][l].astype(F32), ((0, 5), (0, 0))),
        gp=_rows8(_lanes128(a_log[l]), _lanes128(dt_bias[l])), nw=_rows8(gdn_norm_w[l]),
        ln1=_rows8(jnp.zeros((D,), F32), ln1_g[l], ln1_b[l]), ln2=_rows8(b_down[l], ln2_g[l], ln2_b[l]),
        b_up=b_up[l].reshape(1, DFF).astype(F32))


def local_step(xs, target, full, a_log, dt_bias, gdn_norm_w, ln1_g, ln1_b, b_up, b_down, ln2_g, ln2_b):
    Ws = [_layer_weights(l, full, a_log, dt_bias, gdn_norm_w, ln1_g, ln1_b, b_up, b_down, ln2_g, ln2_b)
          for l in range(DEPTH)]
    x, xb = xs, xs.astype(MM)
    saved = []
    for l in range(DEPTH):
        x, xb, S = _layer_fwd(x, xb, Ws[l])
        saved.append(S)
    loss_tile, ct = loss_fwd_bwd(x, target)
    grads = [None] * DEPTH
    for l in reversed(range(DEPTH)):
        ct, grads[l] = _layer_bwd(ct, Ws[l], saved[l])
    return loss_tile, ct, grads


BIG = ("w_in", "w_o_gdn", "w_o_sc", "w_out", "w_up", "w_down")
SMALL = ("conv_qkv", "a_log", "dt_bias", "gdn_norm_w", "conv_sc", "ln1_g", "ln1_b", "b_up", "b_down", "ln2_g", "ln2_b")
ORDER = ("w_in", "conv_qkv", "a_log", "dt_bias", "gdn_norm_w", "w_o_gdn", "conv_sc", "w_o_sc", "w_out", "ln1_g",
         "ln1_b", "w_up", "b_up", "w_down", "b_down", "ln2_g", "ln2_b")


def _pack(arrs):
    flat = jnp.concatenate([a.reshape(-1).astype(F32) for a in arrs])
    n = flat.shape[0]
    rows = -(-n // 1024) * 8
    return jnp.pad(flat, (0, rows * 128 - n)).reshape(rows, 128)


def _unpack(buf, like):
    flat = buf.reshape(-1)
    out, o = [], 0
    for a in like:
        n = 1
        for s in a.shape:
            n *= s
        out.append(flat[o:o + n].reshape(a.shape))
        o += n
    return out


def _gathered(name, g):
    if name in ("w_in", "w_up", "conv_qkv", "conv_sc"):
        t = jnp.moveaxis(g, 0, -2)
        return t.reshape(t.shape[:-2] + (t.shape[-2] * t.shape[-1],))
    t = jnp.moveaxis(g, 0, 1)
    return t.reshape((t.shape[0], t.shape[1] * t.shape[2]) + t.shape[3:])


def _by_chip(name, g):
    if name in ("w_in", "w_up"):
        r, ccols = g.shape
        return jnp.moveaxis(g.reshape(r, 4, ccols // 4), 1, 0)
    return g.reshape((4, g.shape[0] // 4) + g.shape[1:])


def kernel(x, w_in, conv_qkv, a_log, dt_bias, gdn_norm_w, w_o_gdn, conv_sc, w_o_sc, w_out, ln1_g, ln1_b, w_up, b_up, w_down, b_down, ln2_g, ln2_b, loss_target, m_w_in, m_conv_qkv, m_a_log, m_dt_bias, m_gdn_norm_w, m_w_o_gdn, m_conv_sc, m_w_o_sc, m_w_out, m_ln1_g, m_ln1_b, m_w_up, m_b_up, m_w_down, m_b_down, m_ln2_g, m_ln2_b, v_w_in, v_conv_qkv, v_a_log, v_dt_bias, v_gdn_norm_w, v_w_o_gdn, v_conv_sc, v_w_o_sc, v_w_out, v_ln1_g, v_ln1_b, v_w_up, v_b_up, v_w_down, v_b_down, v_ln2_g, v_ln2_b):
    w = dict(w_in=w_in, conv_qkv=conv_qkv, a_log=a_log, dt_bias=dt_bias, gdn_norm_w=gdn_norm_w, w_o_gdn=w_o_gdn,
             conv_sc=conv_sc, w_o_sc=w_o_sc, w_out=w_out, ln1_g=ln1_g, ln1_b=ln1_b, w_up=w_up, b_up=b_up,
             w_down=w_down, b_down=b_down, ln2_g=ln2_g, ln2_b=ln2_b)
    m = dict(w_in=m_w_in, conv_qkv=m_conv_qkv, a_log=m_a_log, dt_bias=m_dt_bias, gdn_norm_w=m_gdn_norm_w,
             w_o_gdn=m_w_o_gdn, conv_sc=m_conv_sc, w_o_sc=m_w_o_sc, w_out=m_w_out, ln1_g=m_ln1_g, ln1_b=m_ln1_b,
             w_up=m_w_up, b_up=m_b_up, w_down=m_w_down, b_down=m_b_down, ln2_g=m_ln2_g, ln2_b=m_ln2_b)
    v = dict(w_in=v_w_in, conv_qkv=v_conv_qkv, a_log=v_a_log, dt_bias=v_dt_bias, gdn_norm_w=v_gdn_norm_w,
             w_o_gdn=v_w_o_gdn, conv_sc=v_conv_sc, w_o_sc=v_w_o_sc, w_out=v_w_out, ln1_g=v_ln1_g, ln1_b=v_ln1_b,
             w_up=v_w_up, b_up=v_b_up, w_down=v_w_down, b_down=v_b_down, ln2_g=v_ln2_g, ln2_b=v_ln2_b)
    chip = 2 * lax.axis_index("x") + lax.axis_index("y")

    names = BIG + ("conv_qkv", "conv_sc")
    got = allgather_xy([w[n].astype(MM) if n in BIG else w[n] for n in names])
    full = {n: _gathered(n, g) for n, g in zip(names, got)}

    loss_tile, dx, grads = local_step(x[0], loss_target[0], full, a_log, dt_bias, gdn_norm_w, ln1_g, ln1_b, b_up,
                                      b_down, ln2_g, ln2_b)
    loss = lax.psum(loss_tile[0, 0], ("x", "y", "c"))

    groups = [[_by_chip(n, grads[l][n]).astype(MM) for l in range(DEPTH)] for n in BIG]
    owns, recvs = scatter_xy(groups)
    part = []
    for n, own, rec in zip(BIG, owns, recvs):
        cols = own.shape[-1]
        part.append(sum4(own.reshape(-1, cols), rec.reshape(3, -1, cols), name="sum_" + n))
    other = swap_c(part)
    out = {}
    for n, mine, theirs in zip(BIG, part, other):
        cols = mine.shape[-1]
        res = adam(w[n].reshape(-1, cols), m[n].reshape(-1, cols), v[n].reshape(-1, cols), mine, theirs, name="adam_" + n)
        out[n] = [r.reshape(w[n].shape) for r in res]

    stacked = [jnp.stack([grads[l][n] for l in range(DEPTH)]) for n in SMALL]
    summed = _unpack(allreduce_small(_pack(stacked)), stacked)
    gs = []
    for n, g in zip(SMALL, summed):
        if n in ("conv_qkv", "conv_sc"):
            blk = w[n].shape[-1]
            g = lax.dynamic_slice_in_dim(g, chip * blk, blk, axis=2)
        gs.append(g)
    res = adam(_pack([w[n] for n in SMALL]), _pack([m[n] for n in SMALL]), _pack([v[n] for n in SMALL]), _pack(gs),
               name="adam_small")
    for n, parts in zip(SMALL, zip(*[_unpack(r, gs) for r in res])):
        out[n] = list(parts)

    outs = [loss, dx[None]]
    for kind in range(4):
        outs += [out[n][kind] for n in ORDER]
    return tuple(outs)
```

```python
import functools

import jax
import jax.numpy as jnp
from jax import lax
from jax.experimental import pallas as pl
from jax.experimental.pallas import tpu as pltpu

F32 = jnp.float32
MM = jnp.bfloat16
HI = lax.Precision.HIGHEST

D = 1024
NH = 8
HD = 128
CH = 64
DFF = 4 * D
DEPTH = 4
LN_EPS = 1e-5
RMS_EPS = 1e-6
L2_EPS = 1e-6
ALPHA = (2 * DEPTH) ** 0.25
LR, B1, B2, EPS, WD, STEP = 0.001, 0.9, 0.999, 1e-08, 0.01, 10

NMAIN = 9 * D
CQ, CK, CV, CSB, CSC, CSX, CGA, CGB, CZ = range(9)
HALO = 8
VMEM_LIMIT = 56 * 1024 * 1024
MESH = pl.DeviceIdType.MESH


def _cp(sem=None, vmem=VMEM_LIMIT):
    return pltpu.CompilerParams(dimension_semantics=sem, vmem_limit_bytes=vmem)


def _sds(shape, dtype=F32):
    return jax.ShapeDtypeStruct(tuple(shape), dtype)


def _accumulate(acc, product, k, nk, finish):
    if nk == 1:
        finish(product())
        return

    @pl.when(k == 0)
    def _():
        acc[...] = jnp.zeros_like(acc)

    acc[...] += product()

    @pl.when(k == nk - 1)
    def _():
        finish(acc[...])


def mm_nn(a, b, *, name, bias=None, relu2=False, add=None, out_dtype=F32, tm=1024, tn=1024, tk=1024):
    M, K = a.shape
    N = b.shape[1]
    tm, tn, tk = min(tm, M), min(tn, N), min(tk, K)
    nk = K // tk

    def body(*refs):
        it = iter(refs)
        a_ref, b_ref = next(it), next(it)
        bias_ref = next(it) if bias is not None else None
        add_ref = next(it) if add is not None else None
        o_ref = next(it)
        h_ref = next(it) if relu2 else None
        acc = next(it) if nk > 1 else None
        prod = lambda: jnp.dot(a_ref[...].astype(MM), b_ref[...].astype(MM), preferred_element_type=F32)

        def finish(r):
            if bias_ref is not None:
                r = r + bias_ref[...]
            if add_ref is not None:
                r = r + add_ref[...]
            o_ref[...] = r.astype(o_ref.dtype)
            if relu2:
                t = jnp.maximum(r, 0.0)
                h_ref[...] = (t * t).astype(h_ref.dtype)

        _accumulate(acc, prod, pl.program_id(2), nk, finish)

    in_specs = [pl.BlockSpec((tm, tk), lambda i, j, k: (i, k)), pl.BlockSpec((tk, tn), lambda i, j, k: (k, j))]
    args = [a, b]
    if bias is not None:
        in_specs.append(pl.BlockSpec((1, tn), lambda i, j, k: (0, j)))
        args.append(bias)
    if add is not None:
        in_specs.append(pl.BlockSpec((tm, tn), lambda i, j, k: (i, j)))
        args.append(add)
    out_shape = [_sds((M, N), out_dtype)]
    out_specs = [pl.BlockSpec((tm, tn), lambda i, j, k: (i, j))]
    if relu2:
        out_shape.append(_sds((M, N), MM))
        out_specs.append(pl.BlockSpec((tm, tn), lambda i, j, k: (i, j)))
    res = pl.pallas_call(
        body, name=name, grid=(M // tm, N // tn, nk), in_specs=in_specs, out_specs=out_specs, out_shape=out_shape,
        scratch_shapes=[pltpu.VMEM((tm, tn), F32)] if nk > 1 else [],
        compiler_params=_cp(("parallel", "parallel", "arbitrary")))(*args)
    return res if relu2 else res[0]


def mm_nt(a, b, *, name, add=None, dact=None, out_dtype=F32, tm=1024, tn=1024, tk=1024):
    M, Nc = a.shape
    Ko = b.shape[0]
    tm, tn, tk = min(tm, M), min(tn, Ko), min(tk, Nc)
    nk = Nc // tk
    ni = M // tm

    def body(*refs):
        it = iter(refs)
        a_ref, b_ref = next(it), next(it)
        add_ref = next(it) if add is not None else None
        d_ref = next(it) if dact is not None else None
        o_ref = next(it)
        db_ref = next(it) if dact is not None else None
        acc = next(it) if nk > 1 else None
        i = pl.program_id(1)
        prod = lambda: lax.dot_general(a_ref[...].astype(MM), b_ref[...].astype(MM), (((1,), (1,)), ((), ())),
                                       preferred_element_type=F32)

        def finish(r):
            if add_ref is not None:
                r = r + add_ref[...]
            if d_ref is not None:
                r = r * (2.0 * jnp.maximum(d_ref[...], 0.0))
                s = jnp.sum(r, axis=0, keepdims=True)
                row0 = lax.broadcasted_iota(jnp.int32, db_ref.shape, 0) == 0

                @pl.when(i == 0)
                def _():
                    db_ref[...] = jnp.zeros_like(db_ref)

                db_ref[...] += jnp.where(row0, s, 0.0)
            o_ref[...] = r.astype(o_ref.dtype)

        _accumulate(acc, prod, pl.program_id(2), nk, finish)

    in_specs = [pl.BlockSpec((tm, tk), lambda j, i, k: (i, k)), pl.BlockSpec((tn, tk), lambda j, i, k: (j, k))]
    args = [a, b]
    for extra in (add, dact):
        if extra is not None:
            in_specs.append(pl.BlockSpec((tm, tn), lambda j, i, k: (i, j)))
            args.append(extra)
    out_shape = [_sds((M, Ko), out_dtype)]
    out_specs = [pl.BlockSpec((tm, tn), lambda j, i, k: (i, j))]
    if dact is not None:
        out_shape.append(_sds((8, Ko), F32))
        out_specs.append(pl.BlockSpec((8, tn), lambda j, i, k: (0, j)))
    res = pl.pallas_call(
        body, name=name, grid=(Ko // tn, ni, nk), in_specs=in_specs, out_specs=out_specs, out_shape=out_shape,
        scratch_shapes=[pltpu.VMEM((tm, tn), F32)] if nk > 1 else [],
        compiler_params=_cp(("parallel", "arbitrary", "arbitrary")))(*args)
    return res if dact is not None else res[0]


def mm_tn(a, b, *, name, tm=1024, tn=1024, tk=1024):
    T, M = a.shape
    N = b.shape[1]
    tm, tn, tk = min(tm, M), min(tn, N), min(tk, T)

    def body(a_ref, b_ref, o_ref):
        @pl.when(pl.program_id(2) == 0)
        def _():
            o_ref[...] = jnp.zeros_like(o_ref)

        o_ref[...] += lax.dot_general(a_ref[...].astype(MM), b_ref[...].astype(MM), (((0,), (0,)), ((), ())),
                                      preferred_element_type=F32)

    return pl.pallas_call(
        body, name=name, grid=(M // tm, N // tn, T // tk),
        in_specs=[pl.BlockSpec((tk, tm), lambda i, j, k: (k, i)), pl.BlockSpec((tk, tn), lambda i, j, k: (k, j))],
        out_specs=pl.BlockSpec((tm, tn), lambda i, j, k: (i, j)), out_shape=_sds((M, N)),
        compiler_params=_cp(("parallel", "parallel", "arbitrary")))(a, b)


def _sigmoid(x):
    return 1.0 / (1.0 + jnp.exp(-x))


def _silu(x):
    return x * _sigmoid(x)


def _softplus(x):
    return jnp.maximum(x, 0.0) + jnp.log1p(jnp.exp(-jnp.abs(x)))


def _ext(main_ref, prev_ref, next_ref, first, last):
    p = jnp.where(first, 0.0, prev_ref[...].astype(F32))
    n = jnp.where(last, 0.0, next_ref[...].astype(F32))
    return jnp.concatenate([p, main_ref[...].astype(F32), n], axis=0)


def _shift_dn(x):
    return pltpu.roll(x, 1, 0)


def _shift_up(x):
    return pltpu.roll(x, x.shape[0] - 1, 0)


def _conv3(xe, w):
    return w[0:1, :] * _shift_dn(xe) + w[1:2, :] * xe + w[2:3, :] * _shift_up(xe)


def _conv3_t(de, w):
    return w[0:1, :] * _shift_up(de) + w[1:2, :] * de + w[2:3, :] * _shift_dn(de)


def _halo_specs(bt, T, col, lead=None):
    r = bt // HALO
    last = T // HALO - 1
    if lead is None:
        return [pl.BlockSpec((bt, D), lambda i: (i, col)),
                pl.BlockSpec((HALO, D), lambda i: (jnp.maximum(i * r - 1, 0), col)),
                pl.BlockSpec((HALO, D), lambda i: (jnp.minimum((i + 1) * r, last), col))]
    return [pl.BlockSpec((lead, bt, D), lambda i: (0, i, col)),
            pl.BlockSpec((lead, HALO, D), lambda i: (0, jnp.maximum(i * r - 1, 0), col)),
            pl.BlockSpec((lead, HALO, D), lambda i: (0, jnp.minimum((i + 1) * r, last), col))]


def _qkv_rows(cq, ck, cv):
    sq, sk, sv = _silu(cq), _silu(ck), _silu(cv)
    qs, ks = [], []
    for h in range(NH):
        s = slice(h * HD, (h + 1) * HD)
        qh, kh = sq[:, s], sk[:, s]
        qs.append(qh * lax.rsqrt(jnp.sum(qh * qh, axis=-1, keepdims=True) + L2_EPS) * (HD ** -0.5))
        ks.append(kh * lax.rsqrt(jnp.sum(kh * kh, axis=-1, keepdims=True) + L2_EPS))
    return jnp.concatenate(qs, axis=1), jnp.concatenate(ks, axis=1), sv


def _chunk_masks(bt):
    row = lax.broadcasted_iota(jnp.int32, (bt, bt), 0)
    col = lax.broadcasted_iota(jnp.int32, (bt, bt), 1)
    same = (row // CH) == (col // CH)
    lower = jnp.where(same & (col <= row), 1.0, 0.0).astype(F32)
    upper = jnp.where(same & (col >= row), 1.0, 0.0).astype(F32)
    return lower, upper


def _gate_rows(ab, gp, lower, upper):
    lane = lax.broadcasted_iota(jnp.int32, ab.shape, 1)
    g = -jnp.exp(gp[0:1, :]) * _softplus(ab + gp[1:2, :])
    g = jnp.where(lane < 2 * NH, g, 0.0)
    gf = jnp.dot(lower, g, precision=HI, preferred_element_type=F32)
    gr = jnp.dot(upper, g, precision=HI, preferred_element_type=F32)
    gc = jnp.where(lane < NH, gf, gr)
    beta = _sigmoid(ab)
    return jnp.where(lane < 2 * NH, gc, jnp.where(lane < 4 * NH, beta, 0.0))


def pre_qkv_fwd(pm, pab, cw, gp, *, bt=256):
    T = pm.shape[0]
    bt = min(bt, T)
    n = T // bt

    def body(q0, q1, q2, k0, k1, k2, v0, v1, v2, ab_ref, cw_ref, gp_ref, q_ref, k_ref, v_ref, gb_ref):
        i = pl.program_id(0)
        first, last = i == 0, i == n - 1
        cs = []
        for c, (m, p, x) in enumerate(((q0, q1, q2), (k0, k1, k2), (v0, v1, v2))):
            xe = _ext(m, p, x, first, last)
            cs.append(_conv3(xe, cw_ref[:, c * D:(c + 1) * D])[HALO:HALO + bt])
        q, k, v = _qkv_rows(*cs)
        q_ref[...], k_ref[...], v_ref[...] = q, k, v
        lower, upper = _chunk_masks(bt)
        gb_ref[...] = _gate_rows(ab_ref[...], gp_ref[...], lower, upper)

    in_specs = (_halo_specs(bt, T, CQ) + _halo_specs(bt, T, CK) + _halo_specs(bt, T, CV)
                + [pl.BlockSpec((bt, 128), lambda i: (i, 0)), pl.BlockSpec((8, 3 * D), lambda i: (0, 0)),
                   pl.BlockSpec((8, 128), lambda i: (0, 0))])
    row = pl.BlockSpec((bt, D), lambda i: (i, 0))
    return pl.pallas_call(
        body, name="pre_qkv_fwd", grid=(n,), in_specs=in_specs,
        out_specs=[row, row, row, pl.BlockSpec((bt, 128), lambda i: (i, 0))],
        out_shape=[_sds((T, D)), _sds((T, D)), _sds((T, D)), _sds((T, 128))],
        compiler_params=_cp(("parallel",)))(*([pm] * 9), pab, cw, gp)


def pre_qkv_bwd(pm, pab, cw, gp, dq2, dk2, dv2, dgb2, dpm, *, bt=128):
    T = pm.shape[0]
    bt = min(bt, T)
    n = T // bt
    E = bt + 2 * HALO

    def body(*refs):
        it = iter(refs)
        xs = [[next(it) for _ in range(3)] for _ in range(3)]
        ds = [[next(it) for _ in range(3)] for _ in range(3)]
        ab_ref, dgb_ref, cw_ref, gp_ref, _alias = next(it), next(it), next(it), next(it), next(it)
        o_ref, dab_ref, dcw_ref, dgp_ref = (next(it) for _ in range(4))
        i = pl.program_id(0)
        first, last = i == 0, i == n - 1

        @pl.when(first)
        def _():
            dcw_ref[...] = jnp.zeros_like(dcw_ref)
            dgp_ref[...] = jnp.zeros_like(dgp_ref)

        xes = [_ext(*xs[c], first, last) for c in range(3)]
        ces = [_conv3(xes[c], cw_ref[:, c * D:(c + 1) * D]) for c in range(3)]
        cts = []
        for c in range(3):
            m, p, x = ds[c]
            pe = jnp.where(first, 0.0, p[0] + p[1])
            ne = jnp.where(last, 0.0, x[0] + x[1])
            cts.append(jnp.concatenate([pe, m[0] + m[1], ne], axis=0))
        _, vjp = jax.vjp(_qkv_rows, *ces)
        dces = vjp(tuple(cts))
        rowi = lax.broadcasted_iota(jnp.int32, (E, 1), 0)
        central = (rowi >= HALO) & (rowi < HALO + bt)
        row8 = lax.broadcasted_iota(jnp.int32, (8, D), 0)
        for c in range(3):
            w = cw_ref[:, c * D:(c + 1) * D]
            o_ref[:, c * D:(c + 1) * D] = _conv3_t(dces[c], w)[HALO:HALO + bt].astype(o_ref.dtype)
            dc = jnp.where(central, dces[c], 0.0)
            taps = (jnp.sum(dc * _shift_dn(xes[c]), axis=0, keepdims=True),
                    jnp.sum(dc * xes[c], axis=0, keepdims=True),
                    jnp.sum(dc * _shift_up(xes[c]), axis=0, keepdims=True))
            upd = jnp.where(row8 == 0, taps[0], jnp.where(row8 == 1, taps[1], jnp.where(row8 == 2, taps[2], 0.0)))
            dcw_ref[:, c * D:(c + 1) * D] += upd
        lower, upper = _chunk_masks(bt)
        _, gvjp = jax.vjp(lambda ab, gp: _gate_rows(ab, gp, lower, upper), ab_ref[...], gp_ref[...])
        dab, dgp = gvjp(dgb_ref[0] + dgb_ref[1])
        dab_ref[...] = dab
        dgp_ref[...] += dgp

    in_specs = (_halo_specs(bt, T, CQ) + _halo_specs(bt, T, CK) + _halo_specs(bt, T, CV)
                + _halo_specs(bt, T, 0, lead=2) * 3
                + [pl.BlockSpec((bt, 128), lambda i: (i, 0)), pl.BlockSpec((2, bt, 128), lambda i: (0, i, 0)),
                   pl.BlockSpec((8, 3 * D), lambda i: (0, 0)), pl.BlockSpec((8, 128), lambda i: (0, 0)),
                   pl.BlockSpec(memory_space=pl.ANY)])
    out_specs = [pl.BlockSpec((bt, 3 * D), lambda i: (i, 0)), pl.BlockSpec((bt, 128), lambda i: (i, 0)),
                 pl.BlockSpec((8, 3 * D), lambda i: (0, 0)), pl.BlockSpec((8, 128), lambda i: (0, 0))]
    return pl.pallas_call(
        body, name="pre_qkv_bwd", grid=(n,), in_specs=in_specs, out_specs=out_specs,
        out_shape=[_sds(dpm.shape, dpm.dtype), _sds((T, 128)), _sds((8, 3 * D)), _sds((8, 128))],
        input_output_aliases={len(in_specs) - 1: 0},
        compiler_params=_cp(("arbitrary",)))(
            *([pm] * 9), dq2, dq2, dq2, dk2, dk2, dk2, dv2, dv2, dv2, pab, dgb2, cw, gp, dpm)


def pre_sc_fwd(pm, cw, *, bt=256):
    T = pm.shape[0]
    bt = min(bt, T)
    n = T // bt

    def body(b_ref, c0, c1, c2, x0, x1, x2, cw_ref, o_ref):
        i = pl.program_id(0)
        first, last = i == 0, i == n - 1
        pe = _ext(c0, c1, c2, first, last) * _ext(x0, x1, x2, first, last)
        o_ref[...] = (b_ref[...] * _conv3(pe, cw_ref[...])[HALO:HALO + bt]).astype(o_ref.dtype)

    in_specs = ([pl.BlockSpec((bt, D), lambda i: (i, CSB))] + _halo_specs(bt, T, CSC) + _halo_specs(bt, T, CSX)
                + [pl.BlockSpec((8, D), lambda i: (0, 0))])
    return pl.pallas_call(
        body, name="pre_sc_fwd", grid=(n,), in_specs=in_specs, out_specs=pl.BlockSpec((bt, D), lambda i: (i, 0)),
        out_shape=_sds((T, D), MM), compiler_params=_cp(("parallel",)))(*([pm] * 7), cw)


def pre_sc_bwd(pm, cw, dsc, dpm, *, bt=256):
    T = pm.shape[0]
    bt = min(bt, T)
    n = T // bt
    E = bt + 2 * HALO

    def body(b0, b1, b2, c0, c1, c2, x0, x1, x2, d0, d1, d2, cw_ref, _alias, o_ref, dcw_ref):
        i = pl.program_id(0)
        first, last = i == 0, i == n - 1

        @pl.when(first)
        def _():
            dcw_ref[...] = jnp.zeros_like(dcw_ref)

        ce, xe = _ext(c0, c1, c2, first, last), _ext(x0, x1, x2, first, last)
        pe = ce * xe
        w = cw_ref[...]
        dout = d0[...]
        o_ref[:, 0:D] = (dout * _conv3(pe, w)[HALO:HALO + bt]).astype(o_ref.dtype)
        dce = _ext(d0, d1, d2, first, last) * _ext(b0, b1, b2, first, last)
        dp = _conv3_t(dce, w)[HALO:HALO + bt]
        o_ref[:, D:2 * D] = (dp * x0[...]).astype(o_ref.dtype)
        o_ref[:, 2 * D:3 * D] = (dp * c0[...]).astype(o_ref.dtype)
        rowi = lax.broadcasted_iota(jnp.int32, (E, 1), 0)
        dc = jnp.where((rowi >= HALO) & (rowi < HALO + bt), dce, 0.0)
        row8 = lax.broadcasted_iota(jnp.int32, (8, D), 0)
        taps = (jnp.sum(dc * _shift_dn(pe), axis=0, keepdims=True), jnp.sum(dc * pe, axis=0, keepdims=True),
                jnp.sum(dc * _shift_up(pe), axis=0, keepdims=True))
        dcw_ref[...] += jnp.where(row8 == 0, taps[0], jnp.where(row8 == 1, taps[1], jnp.where(row8 == 2, taps[2], 0.0)))

    dsc_specs = [pl.BlockSpec((bt, D), lambda i: (i, 0)),
                 pl.BlockSpec((HALO, D), lambda i: (jnp.maximum(i * (bt // HALO) - 1, 0), 0)),
                 pl.BlockSpec((HALO, D), lambda i: (jnp.minimum((i + 1) * (bt // HALO), T // HALO - 1), 0))]
    in_specs = (_halo_specs(bt, T, CSB) + _halo_specs(bt, T, CSC) + _halo_specs(bt, T, CSX) + dsc_specs
                + [pl.BlockSpec((8, D), lambda i: (0, 0)), pl.BlockSpec(memory_space=pl.ANY)])
    return pl.pallas_call(
        body, name="pre_sc_bwd", grid=(n,), in_specs=in_specs,
        out_specs=[pl.BlockSpec((bt, 3 * D), lambda i: (i, 1)), pl.BlockSpec((8, D), lambda i: (0, 0))],
        out_shape=[_sds(dpm.shape, dpm.dtype), _sds((8, D))], input_output_aliases={len(in_specs) - 1: 0},
        compiler_params=_cp(("arbitrary",)))(*([pm] * 9), dsc, dsc, dsc, cw, dpm)


def _bdot(a, b, dims):
    return lax.dot_general(a.astype(MM), b.astype(MM), (dims, ((), ())), preferred_element_type=F32)


_NN, _NT, _TN = ((1,), (0,)), ((1,), (1,)), ((0,), (0,))


def _raw_nn(a, b):
    return _bdot(a, b, _NN)


def _raw_nt(a, b):
    return _bdot(a, b, _NT)


def _raw_tn(a, b):
    return _bdot(a, b, _TN)


def _make_vjp_ops():
    @jax.custom_vjp
    def nn(a, b):
        return _raw_nn(a, b)

    @jax.custom_vjp
    def nt(a, b):
        return _raw_nt(a, b)

    @jax.custom_vjp
    def tn(a, b):
        return _raw_tn(a, b)

    nn.defvjp(lambda a, b: (_raw_nn(a, b), (a, b)), lambda r, g: (_raw_nt(g, r[1]), _raw_tn(r[0], g)))
    nt.defvjp(lambda a, b: (_raw_nt(a, b), (a, b)), lambda r, g: (_raw_nn(g, r[1]), _raw_tn(g, r[0])))
    tn.defvjp(lambda a, b: (_raw_tn(a, b), (a, b)), lambda r, g: (_raw_nt(r[1], g), _raw_nn(r[0], g)))

    @jax.custom_vjp
    def inv_saved(A, Y):
        return Y

    def inv_bwd(Y, g):
        M = g + _raw_tn(Y, g)
        return -(M + _raw_nt(M, Y)), jnp.zeros_like(Y)

    inv_saved.defvjp(lambda A, Y: (Y, Y), inv_bwd)
    return nn, nt, tn, inv_saved


GH = 4
GR = GH * CH
NG = NH // GH
CBI = 2


def _scan_chunks(n_chunks):
    return 4 if n_chunks % 4 == 0 else 2


def _tri_inv_y_all(As):
    Ys = [-A for A in As]
    Ps = [_raw_nn(A, A) for A in As]
    for stage in range(5):
        squares = [_raw_nn(P, P) for P in Ps] if stage < 4 else None
        Ys = [Y + P + _raw_nn(Y, P) for Y, P in zip(Ys, Ps)]
        Ps = squares
    return Ys


def _intra_groups(chains, incl, strict, eye, lastc, ops, inv_all):
    nn, nt, tn = ops
    st = []
    for qs, ks, vs, gcols, bcols in chains:
        q, k, v = (jnp.concatenate(t, axis=0) for t in (qs, ks, vs))
        gcol, bcol = jnp.concatenate(gcols, axis=0), jnp.concatenate(bcols, axis=0)
        grow = jnp.sum(eye * gcol, axis=0, keepdims=True)
        gam = jnp.where(incl, jnp.exp(jnp.where(incl, gcol - grow, 0.0)), 0.0)
        glast = jnp.sum(jnp.where(lastc, grow, 0.0), axis=1, keepdims=True)
        st.append((q, k, v, gcol, bcol, gam, glast, jnp.exp(gcol), k * bcol))
    As = [jnp.where(strict, nt(kb, k) * gam, 0.0) for (q, k, v, gcol, bcol, gam, glast, eg, kb) in st]
    Ys = inv_all(As)
    vbs = [v * bcol for (q, k, v, gcol, bcol, gam, glast, eg, kb) in st]
    kgs = [kb * eg for (q, k, v, gcol, bcol, gam, glast, eg, kb) in st]
    us = [vb + nn(Y, vb) for Y, vb in zip(Ys, vbs)]
    ws = [kg + nn(Y, kg) for Y, kg in zip(Ys, kgs)]
    Ps = [nt(q, k) * gam for (q, k, v, gcol, bcol, gam, glast, eg, kb) in st]
    return [((u, w, P, q * eg, k * jnp.exp(glast - gcol), jnp.exp(glast)), Y)
            for u, w, P, Y, (q, k, v, gcol, bcol, gam, glast, eg, kb) in zip(us, ws, Ps, Ys, st)]


def _scan_groups(chains, ops):
    nn, nt, tn, _ = ops
    vns = [[us[j] - nn(ws[j], Ss[j]) for j in range(GH)] for Ss, us, ws, P, qds, kds, egls in chains]
    os_ = [jnp.concatenate([nn(qds[j], Ss[j]) for j in range(GH)], axis=0) + nn(P, jnp.concatenate(vn, axis=0))
           for (Ss, us, ws, P, qds, kds, egls), vn in zip(chains, vns)]
    S2s = [[Ss[j] * egls[j] + tn(kds[j], vn[j]) for j in range(GH)]
           for (Ss, us, ws, P, qds, kds, egls), vn in zip(chains, vns)]
    return list(zip(os_, S2s))


def _group_masks(rev):
    r = lax.broadcasted_iota(jnp.int32, (GR, GR), 0)
    c = lax.broadcasted_iota(jnp.int32, (GR, GR), 1)
    same = (r // CH) == (c // CH)
    ahead = jnp.where(rev, c - r, r - c)
    incl = same & (ahead >= 0)
    strict = same & (ahead > 0)
    eye = jnp.where(r == c, 1.0, 0.0).astype(F32)
    lastc = same & ((c % CH) == jnp.where(rev, 0, CH - 1))
    return incl, strict, eye, lastc


def _head_gates(gb, h, rev):
    gcol = jnp.where(rev, gb[:, NH + h:NH + h + 1], gb[:, h:h + 1])
    bcol = jnp.where(rev, gb[:, 3 * NH + h:3 * NH + h + 1], gb[:, 2 * NH + h:2 * NH + h + 1])
    return gcol, bcol


def _hs(h):
    return slice(h * HD, (h + 1) * HD)


def _load_chain(q_ref, k_ref, v_ref, gb_ref, c, g, rev):
    r = slice(c * CH, (c + 1) * CH)
    heads = range(g * GH, (g + 1) * GH)
    gates = [_head_gates(gb_ref[r, :], h, rev) for h in heads]
    return ([q_ref[r, _hs(h)] for h in heads], [k_ref[r, _hs(h)] for h in heads], [v_ref[r, _hs(h)] for h in heads],
            [t[0] for t in gates], [t[1] for t in gates])


def gdn_intra_fwd(q, k, v, gb):
    T = q.shape[0]
    N = T // CH
    ops = (_raw_nn, _raw_nt, _raw_tn)

    def body(q_ref, k_ref, v_ref, gb_ref, u_ref, w_ref, qd_ref, kd_ref, pp_ref, ys_ref, eg_ref):
        rev = pl.program_id(0) == 1
        masks = _group_masks(rev)
        where = [(c, g) for c in range(CBI) for g in range(NG)]
        chains = [_load_chain(q_ref, k_ref, v_ref, gb_ref, c, g, rev) for c, g in where]
        for (c, g), ((u, w, P, qd, kd, egl), Y) in zip(where, _intra_groups(chains, *masks, ops, _tri_inv_y_all)):
            r = slice(c * CH, (c + 1) * CH)
            pp_ref[c, g] = P.astype(MM)
            ys_ref[c, g] = Y.astype(MM)
            for j, h in enumerate(range(g * GH, (g + 1) * GH)):
                rows = slice(j * CH, (j + 1) * CH)
                u_ref[r, _hs(h)] = u[rows]
                w_ref[r, _hs(h)] = w[rows].astype(MM)
                qd_ref[r, _hs(h)] = qd[rows].astype(MM)
                kd_ref[r, _hs(h)] = kd[rows].astype(MM)
                eg_ref[c, h:h + 1, :] = jnp.broadcast_to(egl[j * CH:j * CH + 1, :], (1, 128))

    row = pl.BlockSpec((CBI * CH, D), lambda d, n: (n, 0))
    drow = pl.BlockSpec((None, CBI * CH, D), lambda d, n: (d, n, 0))
    mat = pl.BlockSpec((None, CBI, NG, GR, GR), lambda d, n: (d, n, 0, 0, 0))
    return pl.pallas_call(
        body, name="gdn_intra_fwd", grid=(2, N // CBI),
        in_specs=[row, row, row, pl.BlockSpec((CBI * CH, 128), lambda d, n: (n, 0))],
        out_specs=[drow] * 4 + [mat, mat, pl.BlockSpec((None, CBI, NH, 128), lambda d, n: (d, n, 0, 0))],
        out_shape=[_sds((2, T, D))] + [_sds((2, T, D), MM)] * 3 + [_sds((2, N, NG, GR, GR), MM)] * 2
                  + [_sds((2, N, NH, 128))],
        compiler_params=_cp(("parallel", "parallel")))(q, k, v, gb)


def gdn_scan_fwd(u, w, qd, kd, pp, eg):
    T = u.shape[1]
    N = T // CH
    ops = (_raw_nn, _raw_nt, _raw_tn, None)

    cbs = _scan_chunks(N)
    NB = N // cbs

    def body(u_ref, w_ref, qd_ref, kd_ref, pp_ref, eg_ref, o_ref, s0_ref, S):
        d = pl.program_id(0)

        @pl.when(pl.program_id(1) == 0)
        def _():
            S[...] = jnp.zeros_like(S)

        def chunk(c, carry):
            pc = c + d * (cbs - 1 - 2 * c)
            r = pl.ds(pl.multiple_of(pc * CH, CH), CH)
            chains = []
            for g in range(NG):
                heads = range(g * GH, (g + 1) * GH)
                Ss = [S[h] for h in heads]
                for h, Sh in zip(heads, Ss):
                    s0_ref[pc, h] = Sh
                chains.append((Ss, [u_ref[r, _hs(h)] for h in heads], [w_ref[r, _hs(h)] for h in heads], pp_ref[pc, g],
                               [qd_ref[r, _hs(h)] for h in heads], [kd_ref[r, _hs(h)] for h in heads],
                               [eg_ref[pc, h:h + 1, :] for h in heads]))
            for g, (o, S2) in enumerate(_scan_groups(chains, ops)):
                for j, h in enumerate(range(g * GH, (g + 1) * GH)):
                    o_ref[r, _hs(h)] = o[j * CH:(j + 1) * CH]
                    S[h] = S2[j]
            return carry

        lax.fori_loop(0, cbs, chunk, 0)

    bidx = lambda d, n: n + d * (NB - 1 - 2 * n)
    drow = pl.BlockSpec((None, cbs * CH, D), lambda d, n: (d, bidx(d, n), 0))
    mat = pl.BlockSpec((None, cbs, NG, GR, GR), lambda d, n: (d, bidx(d, n), 0, 0, 0))
    return pl.pallas_call(
        body, name="gdn_scan_fwd", grid=(2, NB),
        in_specs=[drow] * 4 + [mat, pl.BlockSpec((None, cbs, NH, 128), lambda d, n: (d, bidx(d, n), 0, 0))],
        out_specs=[drow, pl.BlockSpec((None, cbs, NH, HD, HD), lambda d, n: (d, bidx(d, n), 0, 0, 0))],
        out_shape=[_sds((2, T, D)), _sds((2, N, NH, HD, HD))],
        scratch_shapes=[pltpu.VMEM((NH, HD, HD), F32)],
        compiler_params=_cp(("arbitrary", "arbitrary")))(u, w, qd, kd, pp, eg)


def gdn_scan_bwd(u, w, qd, kd, pp, eg, s0, do):
    T = u.shape[1]
    N = T // CH
    ops = _make_vjp_ops()
    cbs = _scan_chunks(N)
    NB = N // cbs

    def body(u_ref, w_ref, qd_ref, kd_ref, pp_ref, eg_ref, s0_ref, do_ref,
             du_ref, dw_ref, dqd_ref, dkd_ref, dpp_ref, deg_ref, dS):
        d = pl.program_id(0)

        @pl.when(pl.program_id(1) == 0)
        def _():
            dS[...] = jnp.zeros_like(dS)

        def chunk(c, carry):
            pc = (cbs - 1 - c) + d * (2 * c - (cbs - 1))
            r = pl.ds(pl.multiple_of(pc * CH, CH), CH)
            chains, cts = [], []
            for g in range(NG):
                heads = range(g * GH, (g + 1) * GH)
                chains.append(([s0_ref[pc, h] for h in heads], [u_ref[r, _hs(h)] for h in heads],
                               [w_ref[r, _hs(h)].astype(F32) for h in heads], pp_ref[pc, g].astype(F32),
                               [qd_ref[r, _hs(h)].astype(F32) for h in heads],
                               [kd_ref[r, _hs(h)].astype(F32) for h in heads], [eg_ref[pc, h:h + 1, :] for h in heads]))
                cts.append((jnp.concatenate([do_ref[r, _hs(h)].astype(F32) for h in heads], axis=0),
                            [dS[h] for h in heads]))
            _, vjp = jax.vjp(lambda ch: _scan_groups(ch, ops), chains)
            (dchains,) = vjp(cts)
            for g, (dSs, dus, dws, dP, dqds, dkds, degs) in enumerate(dchains):
                dpp_ref[pc, g] = dP
                for j, h in enumerate(range(g * GH, (g + 1) * GH)):
                    dS[h] = dSs[j]
                    du_ref[r, _hs(h)] = dus[j].astype(MM)
                    dw_ref[r, _hs(h)] = dws[j].astype(MM)
                    dqd_ref[r, _hs(h)], dkd_ref[r, _hs(h)] = dqds[j], dkds[j]
                    deg_ref[pc, h:h + 1, :] = degs[j]
            return carry

        lax.fori_loop(0, cbs, chunk, 0)

    bidx = lambda d, n: (NB - 1 - n) + d * (2 * n - (NB - 1))
    drow = pl.BlockSpec((None, cbs * CH, D), lambda d, n: (d, bidx(d, n), 0))
    erow = pl.BlockSpec((None, cbs, NH, 128), lambda d, n: (d, bidx(d, n), 0, 0))
    mat = pl.BlockSpec((None, cbs, NG, GR, GR), lambda d, n: (d, bidx(d, n), 0, 0, 0))
    return pl.pallas_call(
        body, name="gdn_scan_bwd", grid=(2, NB),
        in_specs=[drow] * 4 + [mat, erow, pl.BlockSpec((None, cbs, NH, HD, HD), lambda d, n: (d, bidx(d, n), 0, 0, 0)),
                               pl.BlockSpec((cbs * CH, D), lambda d, n: (bidx(d, n), 0))],
        out_specs=[drow] * 4 + [mat, erow],
        out_shape=[_sds((2, T, D), MM)] * 2 + [_sds((2, T, D))] * 2 + [_sds((2, N, NG, GR, GR))]
                  + [_sds((2, N, NH, 128))],
        scratch_shapes=[pltpu.VMEM((NH, HD, HD), F32)],
        compiler_params=_cp(("arbitrary", "arbitrary")))(u, w, qd, kd, pp, eg, s0, do)


def gdn_intra_bwd(q, k, v, gb, ys, du, dw, dqd, dkd, dpp, deg):
    T = q.shape[0]
    N = T // CH
    nn, nt, tn, inv_saved = _make_vjp_ops()

    def body(q_ref, k_ref, v_ref, gb_ref, ys_ref, du_ref, dw_ref, dqd_ref, dkd_ref, dpp_ref, deg_ref,
             dq_ref, dk_ref, dv_ref, dgb_ref):
        rev = pl.program_id(0) == 1
        masks = _group_masks(rev)
        lane = lax.broadcasted_iota(jnp.int32, (CH, 128), 1)
        grow = lax.broadcasted_iota(jnp.int32, (GR, 1), 0)
        where = [(c, g) for c in range(CBI) for g in range(NG)]
        chains = [_load_chain(q_ref, k_ref, v_ref, gb_ref, c, g, rev) for c, g in where]
        Ys = [ys_ref[c, g].astype(F32) for c, g in where]
        inv_all = lambda As: [inv_saved(A, Y) for A, Y in zip(As, Ys)]
        _, vjp = jax.vjp(lambda ch: _intra_groups(ch, *masks, (nn, nt, tn), inv_all), chains)
        cts = []
        for c, g in where:
            r = slice(c * CH, (c + 1) * CH)
            heads = range(g * GH, (g + 1) * GH)
            stack = lambda ref: jnp.concatenate([ref[r, _hs(h)].astype(F32) for h in heads], axis=0)
            degl = jnp.zeros((GR, 1), F32)
            for j, h in enumerate(heads):
                degl = degl + jnp.where(grow == j * CH, jnp.sum(deg_ref[c, h:h + 1, :], axis=1, keepdims=True), 0.0)
            cts.append(((stack(du_ref), stack(dw_ref), dpp_ref[c, g], stack(dqd_ref), stack(dkd_ref), degl),
                        jnp.zeros((GR, GR), F32)))
        (dchains,) = vjp(cts)
        dgbs = [jnp.zeros((CH, 128), F32) for _ in range(CBI)]
        for (c, g), (dqs, dks, dvs, dgs, dbs) in zip(where, dchains):
            r = slice(c * CH, (c + 1) * CH)
            for j, h in enumerate(range(g * GH, (g + 1) * GH)):
                dq_ref[r, _hs(h)], dk_ref[r, _hs(h)], dv_ref[r, _hs(h)] = dqs[j], dks[j], dvs[j]
                glane = jnp.where(rev, NH + h, h)
                dgbs[c] = dgbs[c] + jnp.where(lane == glane, dgs[j], 0.0) + jnp.where(lane == glane + 2 * NH, dbs[j], 0.0)
        for c in range(CBI):
            dgb_ref[c * CH:(c + 1) * CH, :] = dgbs[c]

    row = pl.BlockSpec((CBI * CH, D), lambda d, n: (n, 0))
    drow = pl.BlockSpec((None, CBI * CH, D), lambda d, n: (d, n, 0))
    mat = pl.BlockSpec((None, CBI, NG, GR, GR), lambda d, n: (d, n, 0, 0, 0))
    return pl.pallas_call(
        body, name="gdn_intra_bwd", grid=(2, N // CBI),
        in_specs=[row, row, row, pl.BlockSpec((CBI * CH, 128), lambda d, n: (n, 0)), mat, drow, drow, drow, drow, mat]
                 + [pl.BlockSpec((None, CBI, NH, 128), lambda d, n: (d, n, 0, 0))],
        out_specs=[drow, drow, drow, pl.BlockSpec((None, CBI * CH, 128), lambda d, n: (d, n, 0))],
        out_shape=[_sds((2, T, D))] * 3 + [_sds((2, T, 128))],
        compiler_params=_cp(("parallel", "parallel")))(q, k, v, gb, ys, du, dw, dqd, dkd, dpp, deg)


def _post_rows(o2a, o2b, z, nw):
    o = o2a + o2b
    outs = []
    for h in range(NH):
        s = slice(h * HD, (h + 1) * HD)
        oh = o[:, s]
        outs.append(oh * lax.rsqrt(jnp.mean(oh * oh, axis=-1, keepdims=True) + RMS_EPS) * nw * _silu(z[:, s]))
    return jnp.concatenate(outs, axis=1)


def post_fwd(o2, pm, nw, *, bt=512):
    T = pm.shape[0]
    bt = min(bt, T)

    def body(o_ref, z_ref, nw_ref, og_ref):
        og_ref[...] = _post_rows(o_ref[0], o_ref[1], z_ref[...], nw_ref[0:1, :]).astype(og_ref.dtype)

    return pl.pallas_call(
        body, name="post_fwd", grid=(T // bt,),
        in_specs=[pl.BlockSpec((2, bt, D), lambda i: (0, i, 0)), pl.BlockSpec((bt, D), lambda i: (i, CZ)),
                  pl.BlockSpec((8, 128), lambda i: (0, 0))],
        out_specs=pl.BlockSpec((bt, D), lambda i: (i, 0)), out_shape=_sds((T, D), MM),
        compiler_params=_cp(("parallel",)))(o2, pm, nw)


def post_bwd(o2, pm, nw, dog, dpm, *, bt=512):
    T = pm.shape[0]
    bt = min(bt, T)

    def body(o_ref, z_ref, nw_ref, dog_ref, _alias, do_ref, dz_ref, dnw_ref):
        @pl.when(pl.program_id(0) == 0)
        def _():
            dnw_ref[...] = jnp.zeros_like(dnw_ref)

        _, vjp = jax.vjp(_post_rows, o_ref[0], o_ref[1], z_ref[...], nw_ref[0:1, :])
        doa, _unused, dz, dnw = vjp(dog_ref[...])
        do_ref[...] = doa.astype(do_ref.dtype)
        dz_ref[...] = dz.astype(dz_ref.dtype)
        row8 = lax.broadcasted_iota(jnp.int32, (8, 128), 0)
        dnw_ref[...] += jnp.where(row8 == 0, dnw, 0.0)

    in_specs = [pl.BlockSpec((2, bt, D), lambda i: (0, i, 0)), pl.BlockSpec((bt, D), lambda i: (i, CZ)),
                pl.BlockSpec((8, 128), lambda i: (0, 0)), pl.BlockSpec((bt, D), lambda i: (i, 0)),
                pl.BlockSpec(memory_space=pl.ANY)]
    return pl.pallas_call(
        body, name="post_bwd", grid=(T // bt,), in_specs=in_specs,
        out_specs=[pl.BlockSpec((bt, D), lambda i: (i, 0)), pl.BlockSpec((bt, D), lambda i: (i, CZ)),
                   pl.BlockSpec((8, 128), lambda i: (0, 0))],
        out_shape=[_sds((T, D), MM), _sds(dpm.shape, dpm.dtype), _sds((8, 128))], input_output_aliases={4: 1},
        compiler_params=_cp(("arbitrary",)))(o2, pm, nw, dog, dpm)


def merge_fwd(ya, yb, pm, *, bt=512):
    T = pm.shape[0]
    bt = min(bt, T)

    def body(ya_ref, yb_ref, ga_ref, gb_ref, o_ref):
        o_ref[...] = (_sigmoid(ga_ref[...]) * ya_ref[...] + _sigmoid(gb_ref[...]) * yb_ref[...]).astype(o_ref.dtype)

    row = pl.BlockSpec((bt, D), lambda i: (i, 0))
    return pl.pallas_call(
        body, name="merge_fwd", grid=(T // bt,),
        in_specs=[row, row, pl.BlockSpec((bt, D), lambda i: (i, CGA)), pl.BlockSpec((bt, D), lambda i: (i, CGB))],
        out_specs=row, out_shape=_sds((T, D), MM), compiler_params=_cp(("parallel",)))(ya, yb, pm, pm)


def merge_bwd(ya, yb, pm, dmix, *, bt=512):
    T = pm.shape[0]
    bt = min(bt, T)

    def body(ya_ref, yb_ref, ga_ref, gb_ref, dm_ref, dya_ref, dyb_ref, dg_ref):
        dm = dm_ref[...]
        sa, sb = _sigmoid(ga_ref[...]), _sigmoid(gb_ref[...])
        dya_ref[...] = (dm * sa).astype(dya_ref.dtype)
        dyb_ref[...] = (dm * sb).astype(dyb_ref.dtype)
        dg_ref[:, 0:D] = (dm * ya_ref[...] * sa * (1.0 - sa)).astype(dg_ref.dtype)
        dg_ref[:, D:2 * D] = (dm * yb_ref[...] * sb * (1.0 - sb)).astype(dg_ref.dtype)

    row = pl.BlockSpec((bt, D), lambda i: (i, 0))
    return pl.pallas_call(
        body, name="merge_bwd", grid=(T // bt,),
        in_specs=[row, row, pl.BlockSpec((bt, D), lambda i: (i, CGA)), pl.BlockSpec((bt, D), lambda i: (i, CGB)), row],
        out_specs=[row, row, pl.BlockSpec((bt, 2 * D), lambda i: (i, CGA // 2))],
        out_shape=[_sds((T, D), MM), _sds((T, D), MM), _sds((T, NMAIN), MM)],
        compiler_params=_cp(("parallel",)))(ya, yb, pm, pm, dmix)


def _ln_rows(x, y, bias, g, b):
    r = ALPHA * x + y + bias
    mu = jnp.mean(r, axis=-1, keepdims=True)
    var = jnp.mean(jnp.square(r - mu), axis=-1, keepdims=True)
    return (r - mu) * lax.rsqrt(var + LN_EPS) * g + b


def ln_fwd(x, y, p, *, name, bt=512):
    T = x.shape[0]
    bt = min(bt, T)

    def body(x_ref, y_ref, p_ref, o_ref, ob_ref):
        r = _ln_rows(x_ref[...], y_ref[...], p_ref[0:1, :], p_ref[1:2, :], p_ref[2:3, :])
        o_ref[...] = r
        ob_ref[...] = r.astype(ob_ref.dtype)

    row = pl.BlockSpec((bt, D), lambda i: (i, 0))
    return pl.pallas_call(
        body, name=name, grid=(T // bt,), in_specs=[row, row, pl.BlockSpec((8, D), lambda i: (0, 0))],
        out_specs=[row, row], out_shape=[_sds((T, D)), _sds((T, D), MM)],
        compiler_params=_cp(("parallel",)))(x, y, p)


def ln_bwd(x, y, p, ct, ct2=None, *, name, bt=512):
    T = x.shape[0]
    bt = min(bt, T)

    def body(*refs):
        it = iter(refs)
        x_ref, y_ref, p_ref, c_ref = next(it), next(it), next(it), next(it)
        c2_ref = next(it) if ct2 is not None else None
        dxa_ref, dr_ref, dp_ref = next(it), next(it), next(it)

        @pl.when(pl.program_id(0) == 0)
        def _():
            dp_ref[...] = jnp.zeros_like(dp_ref)

        c = c_ref[...]
        if c2_ref is not None:
            c = c + c2_ref[...]
        _, vjp = jax.vjp(_ln_rows, x_ref[...], y_ref[...], p_ref[0:1, :], p_ref[1:2, :], p_ref[2:3, :])
        _dx, dy, dbias, dg, db = vjp(c)
        dxa_ref[...] = ALPHA * dy
        dr_ref[...] = dy.astype(dr_ref.dtype)
        row8 = lax.broadcasted_iota(jnp.int32, (8, D), 0)
        dp_ref[...] += jnp.where(row8 == 0, dbias, jnp.where(row8 == 1, dg, jnp.where(row8 == 2, db, 0.0)))

    row = pl.BlockSpec((bt, D), lambda i: (i, 0))
    in_specs = [row, row, pl.BlockSpec((8, D), lambda i: (0, 0)), row] + ([row] if ct2 is not None else [])
    args = [x, y, p, ct] + ([ct2] if ct2 is not None else [])
    return pl.pallas_call(
        body, name=name, grid=(T // bt,), in_specs=in_specs,
        out_specs=[row, row, pl.BlockSpec((8, D), lambda i: (0, 0))],
        out_shape=[_sds((T, D)), _sds((T, D), MM), _sds((8, D))],
        compiler_params=_cp(("arbitrary",)))(*args)


def loss_fwd_bwd(xl, target, *, bt=512):
    T = xl.shape[0]
    bt = min(bt, T)

    def body(x_ref, t_ref, l_ref, d_ref):
        @pl.when(pl.program_id(0) == 0)
        def _():
            l_ref[...] = jnp.zeros_like(l_ref)

        e = x_ref[...] - t_ref[...]
        d_ref[...] = e * (1.0 / D)
        l_ref[...] += 0.5 * jnp.sum(jnp.mean(e * e, axis=-1, keepdims=True), axis=0, keepdims=True)

    row = pl.BlockSpec((bt, D), lambda i: (i, 0))
    return pl.pallas_call(
        body, name="loss", grid=(T // bt,), in_specs=[row, row],
        out_specs=[pl.BlockSpec((8, 128), lambda i: (0, 0)), row], out_shape=[_sds((8, 128)), _sds((T, D))],
        compiler_params=_cp(("arbitrary",)))(xl, target)


def _row_tile(R, Cc, elems=1 << 18):
    if R * Cc <= elems:
        return R
    tr = 8
    while tr * 2 * Cc <= elems and R % (tr * 2) == 0:
        tr *= 2
    return tr


def adam(w, m, v, ga, gb=None, *, name):
    R, Cc = w.shape
    tr = _row_tile(R, Cc)

    def body(*refs):
        it = iter(refs)
        w_ref, m_ref, v_ref, a_ref = next(it), next(it), next(it), next(it)
        b_ref = next(it) if gb is not None else None
        g_ref, d_ref, mo_ref, vo_ref = next(it), next(it), next(it), next(it)
        g = a_ref[...]
        if b_ref is not None:
            g = g + b_ref[...]
        m2 = B1 * m_ref[...] + (1.0 - B1) * g
        v2 = B2 * v_ref[...] + (1.0 - B2) * jnp.square(g)
        m_hat = m2 / (1.0 - B1 ** STEP)
        v_hat = v2 / (1.0 - B2 ** STEP)
        g_ref[...] = g
        d_ref[...] = -LR * (m_hat / (jnp.sqrt(v_hat) + EPS) + WD * w_ref[...])
        mo_ref[...] = m2
        vo_ref[...] = v2

    blk = pl.BlockSpec((tr, Cc), lambda i: (i, 0))
    args = [w, m, v, ga] + ([gb] if gb is not None else [])
    return pl.pallas_call(
        body, name=name, grid=(R // tr,), in_specs=[blk] * len(args), out_specs=[blk] * 4,
        out_shape=[_sds((R, Cc))] * 4, compiler_params=_cp(("parallel",)))(*args)


def sum4(parts, *, name):
    _, R, Cc = parts.shape
    tr = _row_tile(R, Cc)

    def body(p_ref, out_ref):
        f = lambda t: t.astype(F32)
        out_ref[...] = ((f(p_ref[0]) + f(p_ref[1])) + f(p_ref[2])) + f(p_ref[3])

    return pl.pallas_call(
        body, name=name, grid=(R // tr,), in_specs=[pl.BlockSpec((4, tr, Cc), lambda i: (0, i, 0))],
        out_specs=pl.BlockSpec((tr, Cc), lambda i: (i, 0)), out_shape=_sds((R, Cc)),
        compiler_params=_cp(("parallel",)))(parts)


def _place():
    return lax.axis_index("x"), lax.axis_index("y"), lax.axis_index("c")


def _other_chips(x, y):
    return [(1 - x, y), (x, 1 - y), (1 - x, 1 - y)]


_ANY = pl.BlockSpec(memory_space=pl.ANY)


def allgather_xy(arrs):
    n = len(arrs)
    halves = [a.shape[0] // 2 for a in arrs]

    def body(*refs):
        ins, outs = refs[:n], refs[n:2 * n]
        send, recv, fsend, frecv, loc = refs[2 * n:]
        x, y, c = _place()
        me = 2 * x + y
        peers = _other_chips(x, y)
        local = [pltpu.make_async_copy(ins[a], outs[a].at[me], loc.at[a]) for a in range(n)]
        for cp in local:
            cp.start()

        def over_ici(a, j, block, src=None):
            px, py = peers[j]
            mine = pl.ds(c * halves[a], halves[a])
            dst = outs[a].at[block, mine]
            return pltpu.make_async_remote_copy(
                src_ref=dst if src is None else src.at[mine], dst_ref=dst, send_sem=send.at[3 * a + j],
                recv_sem=recv.at[3 * a + j], device_id=(px, py, c), device_id_type=MESH)

        def over_d2d(a, j, half):
            px, py = peers[j]
            rows = outs[a].at[2 * px + py, pl.ds(half * halves[a], halves[a])]
            return pltpu.make_async_remote_copy(
                src_ref=rows, dst_ref=rows, send_sem=fsend.at[3 * a + j], recv_sem=frecv.at[3 * a + j],
                device_id=(x, y, 1 - c), device_id_type=MESH)

        sends = [over_ici(a, j, me, src=ins[a]) for a in range(n) for j in range(3)]
        for cp in sends:
            cp.start()
        passed = []
        for a in range(n):
            for j, (px, py) in enumerate(peers):
                over_ici(a, j, 2 * px + py).wait_recv()
                passed.append(over_d2d(a, j, c))
                passed[-1].start()
        for a in range(n):
            for j in range(3):
                over_d2d(a, j, 1 - c).wait_recv()
        for cp in sends + passed:
            cp.wait_send()
        for cp in local:
            cp.wait()

    return pl.pallas_call(
        body, name="allgather_xy", in_specs=[_ANY] * n, out_specs=[_ANY] * n,
        out_shape=[_sds((4,) + a.shape, a.dtype) for a in arrs],
        scratch_shapes=[pltpu.SemaphoreType.DMA((3 * n,))] * 4 + [pltpu.SemaphoreType.DMA((n,))],
        compiler_params=pltpu.CompilerParams(has_side_effects=True))(*arrs)


_HBM = pl.BlockSpec(memory_space=pltpu.HBM)
_SEM = pl.BlockSpec(memory_space=pltpu.SEMAPHORE)
_EFFECT = pltpu.SideEffectType.DATAFLOW_SIDE_EFFECTING


def _xy_copy(kind, layer, src, land, send, recv, a, j, c, arriving):
    x, y = lax.axis_index("x"), lax.axis_index("y")
    px, py = ((1 - x, y), (x, 1 - y), (1 - x, 1 - y), (x, y))[j]
    if kind == "gather":
        dst = land.at[2 * px + py] if arriving else land.at[2 * x + y]
        src_view = dst if arriving else src
    else:
        dst = land.at[j, layer]
        src_view = src.at[2 * px + py]
    return pltpu.make_async_remote_copy(src_ref=src_view, dst_ref=dst, send_sem=send.at[4 * a + j],
                                        recv_sem=recv.at[4 * a + j], device_id=(px, py, c), device_id_type=MESH)


def xy_start(kind, name, arrs, after, zones=None, layer=0):
    n = len(arrs)
    if zones is None:
        shape = lambda a: ((4,) + a.shape) if kind == "gather" else ((4, DEPTH) + a.shape[1:])
        zones = [lax.empty(shape(a), a.dtype) for a in arrs]

    def body(*refs):
        ins, lands = refs[:n], refs[n:2 * n]
        send, recv = refs[2 * n + 1], refs[2 * n + 2]
        token = refs[-1]
        c = lax.axis_index("c")
        for a in range(n):
            for j in range(4):
                _xy_copy(kind, layer, ins[a], lands[a], send, recv, a, j, c, False).start()
        token[...] = jnp.zeros_like(token)

    hbm = lambda v: pltpu.with_memory_space_constraint(v, pltpu.HBM)
    outs = pl.pallas_call(
        body, name=name,
        out_shape=(pltpu.SemaphoreType.DMA((4 * n,)), pltpu.SemaphoreType.DMA((4 * n,)),
                   *[pltpu.HBM(a.shape, a.dtype) for a in arrs], *[pltpu.HBM(z.shape, z.dtype) for z in zones],
                   _sds((8, 128))),
        in_specs=[_HBM] * (2 * n) + [_ANY],
        out_specs=(_SEM, _SEM, *[_HBM] * (2 * n), pl.BlockSpec(memory_space=pltpu.VMEM)),
        input_output_aliases={i: 2 + i for i in range(2 * n)},
        compiler_params=pltpu.CompilerParams(has_side_effects=_EFFECT))(
            *[hbm(a) for a in arrs], *[hbm(z) for z in zones], after)
    return (kind, layer, outs[0], outs[1], list(outs[2:2 + n])), list(outs[2 + n:2 + 2 * n]), outs[-1]


def xy_wait(name, handles, zones, after):
    kind, layer, send_sems, recv_sems, srcs = handles
    n = len(srcs)

    def body(*refs):
        ins, lands = refs[:n], refs[n:2 * n]
        send, recv = refs[2 * n], refs[2 * n + 1]
        c = lax.axis_index("c")
        for a in range(n):
            for j in range(4):
                _xy_copy(kind, layer, ins[a], lands[a], send, recv, a, j, c, False).wait_send()
                _xy_copy(kind, layer, ins[a], lands[a], send, recv, a, j, c, True).wait_recv()

    outs = pl.pallas_call(
        body, name=name,
        out_shape=tuple(pltpu.HBM(v.shape, v.dtype) for v in srcs + zones),
        in_specs=[_HBM] * (2 * n) + [_SEM, _SEM, _ANY], out_specs=tuple([_HBM] * (2 * n)),
        input_output_aliases={i: i for i in range(2 * n)},
        compiler_params=pltpu.CompilerParams(has_side_effects=_EFFECT))(*srcs, *zones, send_sems, recv_sems, after)
    return list(outs[n:])


def swap_c(arrs):
    n = len(arrs)

    def body(*refs):
        ins, outs = refs[:n], refs[n:2 * n]
        send, recv = refs[2 * n:]
        x, y, c = _place()
        cps = [pltpu.make_async_remote_copy(src_ref=ins[a], dst_ref=outs[a], send_sem=send.at[a], recv_sem=recv.at[a],
                                            device_id=(x, y, 1 - c), device_id_type=MESH) for a in range(n)]
        for cp in cps:
            cp.start()
        for cp in cps:
            cp.wait_recv()
        for cp in cps:
            cp.wait_send()

    return pl.pallas_call(
        body, name="swap_c", in_specs=[_ANY] * n, out_specs=[_ANY] * n, out_shape=[_sds(a.shape, a.dtype) for a in arrs],
        scratch_shapes=[pltpu.SemaphoreType.DMA((n,)), pltpu.SemaphoreType.DMA((n,))],
        compiler_params=pltpu.CompilerParams(has_side_effects=True))(*arrs)


def allreduce_small(v):
    R = v.shape[0]

    def body(v_ref, o_ref, buf, send, recv):
        x, y, c = _place()
        me = 4 * x + 2 * y + c
        buf[0] = v_ref[...]

        def cp(k):
            dx, dy, dc = (k >> 2) & 1, (k >> 1) & 1, k & 1
            return pltpu.make_async_remote_copy(
                src_ref=v_ref, dst_ref=buf.at[k], send_sem=send.at[k - 1], recv_sem=recv.at[k - 1],
                device_id=(x ^ dx, y ^ dy, c ^ dc), device_id_type=MESH)

        cps = [cp(k) for k in range(1, 8)]
        for t in cps:
            t.start()
        for t in cps:
            t.wait_recv()
        acc = buf[me]
        for dev in range(1, 8):
            acc = acc + buf[jnp.bitwise_xor(me, dev)]
        o_ref[...] = acc
        for t in cps:
            t.wait_send()

    vm = pl.BlockSpec(memory_space=pltpu.VMEM)
    return pl.pallas_call(
        body, name="allreduce_small", in_specs=[vm], out_specs=vm, out_shape=_sds((R, 128)),
        scratch_shapes=[pltpu.VMEM((8, R, 128), F32), pltpu.SemaphoreType.DMA((7,)), pltpu.SemaphoreType.DMA((7,))],
        compiler_params=pltpu.CompilerParams(has_side_effects=True, vmem_limit_bytes=VMEM_LIMIT))(v)


def _rows8(*rows):
    n = rows[0].shape[-1]
    t = jnp.stack([r.reshape(n).astype(F32) for r in rows])
    return jnp.pad(t, ((0, 8 - len(rows)), (0, 0)))


def _lanes128(a):
    f = a.reshape(-1).astype(F32)
    return jnp.pad(f, (0, 128 - f.shape[0]))


def _layer_fwd(x, xb, W):
    pm = mm_nn(xb, W["w_main"], name="proj_main")
    pab = mm_nn(xb, W["w_ab"], name="proj_ab")
    q, k, v, gb = pre_qkv_fwd(pm, pab, W["cw"], W["gp"])
    sc = pre_sc_fwd(pm, W["csc"])
    u, w, qd, kd, pp, ys, eg = gdn_intra_fwd(q, k, v, gb)
    o2, s0 = gdn_scan_fwd(u, w, qd, kd, pp, eg)
    og = post_fwd(o2, pm, W["nw"])
    ya = mm_nn(og, W["w_og"], name="proj_og")
    yb = mm_nn(sc, W["w_osc"], name="proj_osc")
    mixed = merge_fwd(ya, yb, pm)
    out = mm_nn(mixed, W["w_out"], name="proj_out")
    x1, x1b = ln_fwd(x, out, W["ln1"], name="ln1_fwd")
    hpre, h = mm_nn(x1b, W["w_up"], bias=W["b_up"], relu2=True, name="mlp_up")
    dn = mm_nn(h, W["w_down"], name="mlp_down")
    x2, x2b = ln_fwd(x1, dn, W["ln2"], name="ln2_fwd")
    saved = dict(x=x, xb=xb, pm=pm, pab=pab, q=q, k=k, v=v, gb=gb, sc=sc, o2=o2, s0=s0, og=og, ya=ya, yb=yb,
                 u=u, w=w, qd=qd, kd=kd, pp=pp, ys=ys, eg=eg,
                 mixed=mixed, out=out, x1=x1, x1b=x1b, hpre=hpre, h=h, dn=dn)
    return x2, x2b, saved


def _layer_bwd(ct, W, S):
    dxa2, dr2b, dp2 = ln_bwd(S["x1"], S["dn"], W["ln2"], ct, name="ln2_bwd")
    g_down = mm_tn(S["h"], dr2b, name="dw_down")
    dhpre, db_up = mm_nt(dr2b, W["w_down"], dact=S["hpre"], out_dtype=MM, name="mlp_down_bwd")
    g_up = mm_tn(S["x1b"], dhpre, name="dw_up")
    dx1 = mm_nt(dhpre, W["w_up"], add=dxa2, name="mlp_up_bwd")
    dxa1, dr1b, dp1 = ln_bwd(S["x"], S["out"], W["ln1"], dx1, name="ln1_bwd")
    g_out = mm_tn(S["mixed"], dr1b, name="dw_out")
    dmix = mm_nt(dr1b, W["w_out"], name="proj_out_bwd")
    dya, dyb, dpm = merge_bwd(S["ya"], S["yb"], S["pm"], dmix)
    g_og = mm_tn(S["og"], dya, name="dw_og")
    g_osc = mm_tn(S["sc"], dyb, name="dw_osc")
    dog = mm_nt(dya, W["w_og"], name="proj_og_bwd")
    dsc = mm_nt(dyb, W["w_osc"], name="proj_osc_bwd")
    do, dpm, dnw = post_bwd(S["o2"], S["pm"], W["nw"], dog, dpm)
    du, dw, dqd, dkd, dpp, deg = gdn_scan_bwd(S["u"], S["w"], S["qd"], S["kd"], S["pp"], S["eg"], S["s0"], do)
    dq2, dk2, dv2, dgb2 = gdn_intra_bwd(S["q"], S["k"], S["v"], S["gb"], S["ys"], du, dw, dqd, dkd, dpp, deg)
    dpm, dpab, dcw, dgp = pre_qkv_bwd(S["pm"], S["pab"], W["cw"], W["gp"], dq2, dk2, dv2, dgb2, dpm)
    dpm, dcsc = pre_sc_bwd(S["pm"], W["csc"], dsc, dpm)
    g_main = mm_tn(S["xb"], dpm, name="dw_main")
    g_ab = mm_tn(S["xb"], dpab, name="dw_ab")
    t = mm_nt(dpab, W["w_ab"], add=dxa1, name="proj_ab_bwd")
    dx = mm_nt(dpm, W["w_main"], add=t, name="proj_main_bwd")
    g_in = jnp.concatenate([g_main[:, :3 * D], g_main[:, 8 * D:], g_ab[:, :4 * NH], g_main[:, 3 * D:8 * D]], axis=1)
    grads = dict(
        w_in=g_in, w_o_gdn=g_og, w_o_sc=g_osc, w_out=g_out, w_up=g_up, w_down=g_down,
        conv_qkv=dcw[:3], conv_sc=dcsc[:3], a_log=dgp[0, :2 * NH].reshape(2, NH), dt_bias=dgp[1, :2 * NH].reshape(2, NH),
        gdn_norm_w=dnw[0], ln1_g=dp1[1], ln1_b=dp1[2], b_up=db_up[0], b_down=dp2[0], ln2_g=dp2[1], ln2_b=dp2[2])
    return dx, grads


def _layer_weights(l, full, i, a_log, dt_bias, gdn_norm_w, ln1_g, ln1_b, b_up, b_down, ln2_g, ln2_b):
    w_in = full["w_in"][i]
    w_main = jnp.concatenate([w_in[:, :3 * D], w_in[:, 4 * D + 4 * NH:], w_in[:, 3 * D:4 * D]], axis=1)
    w_ab = jnp.pad(w_in[:, 4 * D:4 * D + 4 * NH], ((0, 0), (0, 128 - 4 * NH)))
    return dict(
        w_main=w_main, w_ab=w_ab, w_og=full["w_o_gdn"][i], w_osc=full["w_o_sc"][i], w_out=full["w_out"][i],
        w_up=full["w_up"][i], w_down=full["w_down"][i],
        cw=jnp.pad(full["conv_qkv"][i].astype(F32), ((0, 5), (0, 0))),
        csc=jnp.pad(full["conv_sc"][i].astype(F32), ((0, 5), (0, 0))),
        gp=_rows8(_lanes128(a_log[l]), _lanes128(dt_bias[l])), nw=_rows8(gdn_norm_w[l]),
        ln1=_rows8(jnp.zeros((D,), F32), ln1_g[l], ln1_b[l]), ln2=_rows8(b_down[l], ln2_g[l], ln2_b[l]),
        b_up=b_up[l].reshape(1, DFF).astype(F32))


def local_step(xs, target, weights_of, after_bwd):
    x, xb = xs, xs.astype(MM)
    Ws, saved = [], []
    for l in range(DEPTH):
        Ws.append(weights_of(l, x))
        x, xb, S = _layer_fwd(x, xb, Ws[l])
        saved.append(S)
    loss_tile, ct = loss_fwd_bwd(x, target)
    token = None
    for l in reversed(range(DEPTH)):
        W = Ws[l] if token is None else dict(Ws[l], ln2=Ws[l]["ln2"] + token[0, 0])
        ct, grads = _layer_bwd(ct, W, saved[l])
        token = after_bwd(l, grads)
    return loss_tile, ct


EARLY = 2
BIG = ("w_in", "w_o_gdn", "w_o_sc", "w_out", "w_up", "w_down")
SMALL = ("conv_qkv", "a_log", "dt_bias", "gdn_norm_w", "conv_sc", "ln1_g", "ln1_b", "b_up", "b_down", "ln2_g", "ln2_b")
ORDER = ("w_in", "conv_qkv", "a_log", "dt_bias", "gdn_norm_w", "w_o_gdn", "conv_sc", "w_o_sc", "w_out", "ln1_g",
         "ln1_b", "w_up", "b_up", "w_down", "b_down", "ln2_g", "ln2_b")


def _pack(arrs):
    flat = jnp.concatenate([a.reshape(-1).astype(F32) for a in arrs])
    n = flat.shape[0]
    rows = -(-n // 1024) * 8
    return jnp.pad(flat, (0, rows * 128 - n)).reshape(rows, 128)


def _unpack(buf, like):
    flat = buf.reshape(-1)
    out, o = [], 0
    for a in like:
        n = 1
        for s in a.shape:
            n *= s
        out.append(flat[o:o + n].reshape(a.shape))
        o += n
    return out


def _gathered(name, g):
    if name in ("w_in", "w_up", "conv_qkv", "conv_sc"):
        t = jnp.moveaxis(g, 0, -2)
        return t.reshape(t.shape[:-2] + (t.shape[-2] * t.shape[-1],))
    t = jnp.moveaxis(g, 0, 1)
    return t.reshape((t.shape[0], t.shape[1] * t.shape[2]) + t.shape[3:])


def _by_chip(name, g):
    if name in ("w_in", "w_up"):
        r, ccols = g.shape
        return jnp.moveaxis(g.reshape(r, 4, ccols // 4), 1, 0)
    return g.reshape((4, g.shape[0] // 4) + g.shape[1:])


def kernel(x, w_in, conv_qkv, a_log, dt_bias, gdn_norm_w, w_o_gdn, conv_sc, w_o_sc, w_out, ln1_g, ln1_b, w_up, b_up, w_down, b_down, ln2_g, ln2_b, loss_target, m_w_in, m_conv_qkv, m_a_log, m_dt_bias, m_gdn_norm_w, m_w_o_gdn, m_conv_sc, m_w_o_sc, m_w_out, m_ln1_g, m_ln1_b, m_w_up, m_b_up, m_w_down, m_b_down, m_ln2_g, m_ln2_b, v_w_in, v_conv_qkv, v_a_log, v_dt_bias, v_gdn_norm_w, v_w_o_gdn, v_conv_sc, v_w_o_sc, v_w_out, v_ln1_g, v_ln1_b, v_w_up, v_b_up, v_w_down, v_b_down, v_ln2_g, v_ln2_b):
    w = dict(w_in=w_in, conv_qkv=conv_qkv, a_log=a_log, dt_bias=dt_bias, gdn_norm_w=gdn_norm_w, w_o_gdn=w_o_gdn,
             conv_sc=conv_sc, w_o_sc=w_o_sc, w_out=w_out, ln1_g=ln1_g, ln1_b=ln1_b, w_up=w_up, b_up=b_up,
             w_down=w_down, b_down=b_down, ln2_g=ln2_g, ln2_b=ln2_b)
    m = dict(w_in=m_w_in, conv_qkv=m_conv_qkv, a_log=m_a_log, dt_bias=m_dt_bias, gdn_norm_w=m_gdn_norm_w,
             w_o_gdn=m_w_o_gdn, conv_sc=m_conv_sc, w_o_sc=m_w_o_sc, w_out=m_w_out, ln1_g=m_ln1_g, ln1_b=m_ln1_b,
             w_up=m_w_up, b_up=m_b_up, w_down=m_w_down, b_down=m_b_down, ln2_g=m_ln2_g, ln2_b=m_ln2_b)
    v = dict(w_in=v_w_in, conv_qkv=v_conv_qkv, a_log=v_a_log, dt_bias=v_dt_bias, gdn_norm_w=v_gdn_norm_w,
             w_o_gdn=v_w_o_gdn, conv_sc=v_conv_sc, w_o_sc=v_w_o_sc, w_out=v_w_out, ln1_g=v_ln1_g, ln1_b=v_ln1_b,
             w_up=v_w_up, b_up=v_b_up, w_down=v_w_down, b_down=v_b_down, ln2_g=v_ln2_g, ln2_b=v_ln2_b)
    chip = 2 * lax.axis_index("x") + lax.axis_index("y")

    names = BIG + ("conv_qkv", "conv_sc")
    blocks = [w[n].astype(MM) if n in BIG else w[n] for n in names]
    got = allgather_xy([b[:EARLY] for b in blocks])
    early = {n: _gathered(n, g) for n, g in zip(names, got)}
    gather, gather_zones, token = xy_start("gather", "gather_start", [b[EARLY:] for b in blocks], after=got[0])
    vectors = (a_log, dt_bias, gdn_norm_w, ln1_g, ln1_b, b_up, b_down, ln2_g, ln2_b)
    late, grads, scatters, zones = {}, [None] * DEPTH, [None] * DEPTH, [None]

    def weights_of(l, x_l):
        if l < EARLY:
            return _layer_weights(l, early, l, *vectors)
        if not late:
            for n, zone in zip(names, xy_wait("gather_wait", gather, gather_zones, after=x_l)):
                late[n] = _gathered(n, zone)
        return _layer_weights(l, late, l - EARLY, *vectors)

    def after_bwd(l, g):
        grads[l] = g
        scatters[l], zones[0], tok = xy_start(
            "scatter", "scatter_start_%d" % l, [_by_chip(n, g[n]).astype(MM) for n in BIG], after=g["ln2_g"],
            zones=zones[0], layer=l)
        return tok

    loss_tile, dx = local_step(x[0] + token[0, 0], loss_target[0], weights_of, after_bwd)
    loss = lax.psum(loss_tile[0, 0], ("x", "y", "c"))

    arrived = zones[0]
    for l in range(DEPTH):
        arrived = xy_wait("scatter_wait_%d" % l, scatters[l], arrived, after=dx)
    part = [sum4(z.reshape(4, -1, z.shape[-1]), name="sum_" + n) for n, z in zip(BIG, arrived)]
    other = swap_c(part)
    out = {}
    for n, mine, theirs in zip(BIG, part, other):
        cols = mine.shape[-1]
        res = adam(w[n].reshape(-1, cols), m[n].reshape(-1, cols), v[n].reshape(-1, cols), mine, theirs, name="adam_" + n)
        out[n] = [r.reshape(w[n].shape) for r in res]

    stacked = [jnp.stack([grads[l][n] for l in range(DEPTH)]) for n in SMALL]
    summed = _unpack(allreduce_small(_pack(stacked)), stacked)
    gs = []
    for n, g in zip(SMALL, summed):
        if n in ("conv_qkv", "conv_sc"):
            blk = w[n].shape[-1]
            g = lax.dynamic_slice_in_dim(g, chip * blk, blk, axis=2)
        gs.append(g)
    res = adam(_pack([w[n] for n in SMALL]), _pack([m[n] for n in SMALL]), _pack([v[n] for n in SMALL]), _pack(gs),
               name="adam_small")
    for n, parts in zip(SMALL, zip(*[_unpack(r, gs) for r in res])):
        out[n] = list(parts)

    outs = [loss, dx[None]]
    for kind in range(4):
        outs += [out[n][kind] for n in ORDER]
    return tuple(outs)
```

```python
import functools

import jax
import jax.numpy as jnp
from jax import lax
from jax.experimental import pallas as pl
from jax.experimental.pallas import tpu as pltpu

F32 = jnp.float32
MM = jnp.bfloat16
HI = lax.Precision.HIGHEST

D = 1024
NH = 8
HD = 128
CH = 64
DFF = 4 * D
DEPTH = 4
LN_EPS = 1e-5
RMS_EPS = 1e-6
L2_EPS = 1e-6
ALPHA = (2 * DEPTH) ** 0.25
LR, B1, B2, EPS, WD, STEP = 0.001, 0.9, 0.999, 1e-08, 0.01, 10

NMAIN = 9 * D
CQ, CK, CV, CSB, CSC, CSX, CGA, CGB, CZ = range(9)
HALO = 8
VMEM_LIMIT = 56 * 1024 * 1024
MESH = pl.DeviceIdType.MESH


def _cp(sem=None, vmem=VMEM_LIMIT):
    return pltpu.CompilerParams(dimension_semantics=sem, vmem_limit_bytes=vmem)


def _sds(shape, dtype=F32):
    return jax.ShapeDtypeStruct(tuple(shape), dtype)


def _accumulate(acc, product, k, nk, finish):
    if nk == 1:
        finish(product())
        return

    @pl.when(k == 0)
    def _():
        acc[...] = jnp.zeros_like(acc)

    acc[...] += product()

    @pl.when(k == nk - 1)
    def _():
        finish(acc[...])


def mm_nn(a, b, *, name, bias=None, relu2=False, add=None, out_dtype=F32, tm=1024, tn=1024, tk=1024):
    M, K = a.shape
    N = b.shape[1]
    tm, tn, tk = min(tm, M), min(tn, N), min(tk, K)
    nk = K // tk

    def body(*refs):
        it = iter(refs)
        a_ref, b_ref = next(it), next(it)
        bias_ref = next(it) if bias is not None else None
        add_ref = next(it) if add is not None else None
        o_ref = next(it)
        h_ref = next(it) if relu2 else None
        acc = next(it) if nk > 1 else None
        prod = lambda: jnp.dot(a_ref[...].astype(MM), b_ref[...].astype(MM), preferred_element_type=F32)

        def finish(r):
            if bias_ref is not None:
                r = r + bias_ref[...]
            if add_ref is not None:
                r = r + add_ref[...]
            if relu2:
                t = jnp.maximum(r, 0.0)
                o_ref[...] = (2.0 * t).astype(o_ref.dtype)
                h_ref[...] = (t * t).astype(h_ref.dtype)
            else:
                o_ref[...] = r.astype(o_ref.dtype)

        _accumulate(acc, prod, pl.program_id(2), nk, finish)

    in_specs = [pl.BlockSpec((tm, tk), lambda i, j, k: (i, k)), pl.BlockSpec((tk, tn), lambda i, j, k: (k, j))]
    args = [a, b]
    if bias is not None:
        in_specs.append(pl.BlockSpec((1, tn), lambda i, j, k: (0, j)))
        args.append(bias)
    if add is not None:
        in_specs.append(pl.BlockSpec((tm, tn), lambda i, j, k: (i, j)))
        args.append(add)
    out_shape = [_sds((M, N), out_dtype)]
    out_specs = [pl.BlockSpec((tm, tn), lambda i, j, k: (i, j))]
    if relu2:
        out_shape.append(_sds((M, N), MM))
        out_specs.append(pl.BlockSpec((tm, tn), lambda i, j, k: (i, j)))
    res = pl.pallas_call(
        body, name=name, grid=(M // tm, N // tn, nk), in_specs=in_specs, out_specs=out_specs, out_shape=out_shape,
        scratch_shapes=[pltpu.VMEM((tm, tn), F32)] if nk > 1 else [],
        compiler_params=_cp(("parallel", "parallel", "arbitrary")))(*args)
    return res if relu2 else res[0]


def mm_nt(a, b, *, name, add=None, dact=None, out_dtype=F32, tm=1024, tn=1024, tk=1024):
    M, Nc = a.shape
    Ko = b.shape[0]
    tm, tn, tk = min(tm, M), min(tn, Ko), min(tk, Nc)
    nk = Nc // tk
    ni = M // tm

    def body(*refs):
        it = iter(refs)
        a_ref, b_ref = next(it), next(it)
        add_ref = next(it) if add is not None else None
        d_ref = next(it) if dact is not None else None
        o_ref = next(it)
        db_ref = next(it) if dact is not None else None
        acc = next(it) if nk > 1 else None
        i = pl.program_id(1)
        prod = lambda: lax.dot_general(a_ref[...].astype(MM), b_ref[...].astype(MM), (((1,), (1,)), ((), ())),
                                       preferred_element_type=F32)

        def finish(r):
            if add_ref is not None:
                r = r + add_ref[...]
            if d_ref is not None:
                r = r * d_ref[...].astype(F32)
                s = jnp.sum(r, axis=0, keepdims=True)
                row0 = lax.broadcasted_iota(jnp.int32, db_ref.shape, 0) == 0

                @pl.when(i == 0)
                def _():
                    db_ref[...] = jnp.zeros_like(db_ref)

                db_ref[...] += jnp.where(row0, s, 0.0)
            o_ref[...] = r.astype(o_ref.dtype)

        _accumulate(acc, prod, pl.program_id(2), nk, finish)

    in_specs = [pl.BlockSpec((tm, tk), lambda j, i, k: (i, k)), pl.BlockSpec((tn, tk), lambda j, i, k: (j, k))]
    args = [a, b]
    for extra in (add, dact):
        if extra is not None:
            in_specs.append(pl.BlockSpec((tm, tn), lambda j, i, k: (i, j)))
            args.append(extra)
    out_shape = [_sds((M, Ko), out_dtype)]
    out_specs = [pl.BlockSpec((tm, tn), lambda j, i, k: (i, j))]
    if dact is not None:
        out_shape.append(_sds((8, Ko), F32))
        out_specs.append(pl.BlockSpec((8, tn), lambda j, i, k: (0, j)))
    res = pl.pallas_call(
        body, name=name, grid=(Ko // tn, ni, nk), in_specs=in_specs, out_specs=out_specs, out_shape=out_shape,
        scratch_shapes=[pltpu.VMEM((tm, tn), F32)] if nk > 1 else [],
        compiler_params=_cp(("parallel", "arbitrary", "arbitrary")))(*args)
    return res if dact is not None else res[0]


def mm_tn(a, b, *, name, tm=1024, tn=1024, tk=1024):
    T, M = a.shape
    N = b.shape[1]
    tm, tn, tk = min(tm, M), min(tn, N), min(tk, T)

    def body(a_ref, b_ref, o_ref):
        @pl.when(pl.program_id(2) == 0)
        def _():
            o_ref[...] = jnp.zeros_like(o_ref)

        o_ref[...] += lax.dot_general(a_ref[...].astype(MM), b_ref[...].astype(MM), (((0,), (0,)), ((), ())),
                                      preferred_element_type=F32)

    return pl.pallas_call(
        body, name=name, grid=(M // tm, N // tn, T // tk),
        in_specs=[pl.BlockSpec((tk, tm), lambda i, j, k: (k, i)), pl.BlockSpec((tk, tn), lambda i, j, k: (k, j))],
        out_specs=pl.BlockSpec((tm, tn), lambda i, j, k: (i, j)), out_shape=_sds((M, N)),
        compiler_params=_cp(("parallel", "parallel", "arbitrary")))(a, b)


def _sigmoid(x):
    return 1.0 / (1.0 + jnp.exp(-x))


def _silu(x):
    return x * _sigmoid(x)


def _softplus(x):
    return jnp.maximum(x, 0.0) + jnp.log1p(jnp.exp(-jnp.abs(x)))


def _ext(main_ref, prev_ref, next_ref, first, last):
    p = jnp.where(first, 0.0, prev_ref[...].astype(F32))
    n = jnp.where(last, 0.0, next_ref[...].astype(F32))
    return jnp.concatenate([p, main_ref[...].astype(F32), n], axis=0)


def _shift_dn(x):
    return pltpu.roll(x, 1, 0)


def _shift_up(x):
    return pltpu.roll(x, x.shape[0] - 1, 0)


def _conv3(xe, w):
    return w[0:1, :] * _shift_dn(xe) + w[1:2, :] * xe + w[2:3, :] * _shift_up(xe)


def _conv3_t(de, w):
    return w[0:1, :] * _shift_up(de) + w[1:2, :] * de + w[2:3, :] * _shift_dn(de)


def _halo_specs(bt, T, col, lead=None):
    r = bt // HALO
    last = T // HALO - 1
    if lead is None:
        return [pl.BlockSpec((bt, D), lambda i: (i, col)),
                pl.BlockSpec((HALO, D), lambda i: (jnp.maximum(i * r - 1, 0), col)),
                pl.BlockSpec((HALO, D), lambda i: (jnp.minimum((i + 1) * r, last), col))]
    return [pl.BlockSpec((lead, bt, D), lambda i: (0, i, col)),
            pl.BlockSpec((lead, HALO, D), lambda i: (0, jnp.maximum(i * r - 1, 0), col)),
            pl.BlockSpec((lead, HALO, D), lambda i: (0, jnp.minimum((i + 1) * r, last), col))]


def _qkv_rows(cq, ck, cv):
    sq, sk, sv = _silu(cq), _silu(ck), _silu(cv)
    qs, ks = [], []
    for h in range(NH):
        s = slice(h * HD, (h + 1) * HD)
        qh, kh = sq[:, s], sk[:, s]
        qs.append(qh * lax.rsqrt(jnp.sum(qh * qh, axis=-1, keepdims=True) + L2_EPS) * (HD ** -0.5))
        ks.append(kh * lax.rsqrt(jnp.sum(kh * kh, axis=-1, keepdims=True) + L2_EPS))
    return jnp.concatenate(qs, axis=1), jnp.concatenate(ks, axis=1), sv


def _chunk_masks(bt):
    row = lax.broadcasted_iota(jnp.int32, (bt, bt), 0)
    col = lax.broadcasted_iota(jnp.int32, (bt, bt), 1)
    same = (row // CH) == (col // CH)
    lower = jnp.where(same & (col <= row), 1.0, 0.0).astype(F32)
    upper = jnp.where(same & (col >= row), 1.0, 0.0).astype(F32)
    return lower, upper


def _gate_rows(ab, gp, lower, upper):
    lane = lax.broadcasted_iota(jnp.int32, ab.shape, 1)
    g = -jnp.exp(gp[0:1, :]) * _softplus(ab + gp[1:2, :])
    g = jnp.where(lane < 2 * NH, g, 0.0)
    gf = jnp.dot(lower, g, precision=HI, preferred_element_type=F32)
    gr = jnp.dot(upper, g, precision=HI, preferred_element_type=F32)
    gc = jnp.where(lane < NH, gf, gr)
    beta = _sigmoid(ab)
    return jnp.where(lane < 2 * NH, gc, jnp.where(lane < 4 * NH, beta, 0.0))


def pre_qkv_fwd(pm, pab, cw, gp, *, bt=256):
    T = pm.shape[0]
    bt = min(bt, T)
    n = T // bt

    def body(q0, q1, q2, k0, k1, k2, v0, v1, v2, ab_ref, cw_ref, gp_ref, q_ref, k_ref, v_ref, gb_ref):
        i = pl.program_id(0)
        first, last = i == 0, i == n - 1
        cs = []
        for c, (m, p, x) in enumerate(((q0, q1, q2), (k0, k1, k2), (v0, v1, v2))):
            xe = _ext(m, p, x, first, last)
            cs.append(_conv3(xe, cw_ref[:, c * D:(c + 1) * D])[HALO:HALO + bt])
        q, k, v = _qkv_rows(*cs)
        q_ref[...], k_ref[...], v_ref[...] = q, k, v
        lower, upper = _chunk_masks(bt)
        gb_ref[...] = _gate_rows(ab_ref[...], gp_ref[...], lower, upper)

    in_specs = (_halo_specs(bt, T, CQ) + _halo_specs(bt, T, CK) + _halo_specs(bt, T, CV)
                + [pl.BlockSpec((bt, 128), lambda i: (i, 0)), pl.BlockSpec((8, 3 * D), lambda i: (0, 0)),
                   pl.BlockSpec((8, 128), lambda i: (0, 0))])
    row = pl.BlockSpec((bt, D), lambda i: (i, 0))
    return pl.pallas_call(
        body, name="pre_qkv_fwd", grid=(n,), in_specs=in_specs,
        out_specs=[row, row, row, pl.BlockSpec((bt, 128), lambda i: (i, 0))],
        out_shape=[_sds((T, D)), _sds((T, D)), _sds((T, D)), _sds((T, 128))],
        compiler_params=_cp(("parallel",)))(*([pm] * 9), pab, cw, gp)


def pre_qkv_bwd(pm, pab, cw, gp, dq2, dk2, dv2, dgb2, dpm, *, bt=128):
    T = pm.shape[0]
    bt = min(bt, T)
    n = T // bt
    E = bt + 2 * HALO

    def body(*refs):
        it = iter(refs)
        xs = [[next(it) for _ in range(3)] for _ in range(3)]
        ds = [[next(it) for _ in range(3)] for _ in range(3)]
        ab_ref, dgb_ref, cw_ref, gp_ref, _alias = next(it), next(it), next(it), next(it), next(it)
        o_ref, dab_ref, dcw_ref, dgp_ref = (next(it) for _ in range(4))
        i = pl.program_id(0)
        first, last = i == 0, i == n - 1

        @pl.when(first)
        def _():
            dcw_ref[...] = jnp.zeros_like(dcw_ref)
            dgp_ref[...] = jnp.zeros_like(dgp_ref)

        xes = [_ext(*xs[c], first, last) for c in range(3)]
        ces = [_conv3(xes[c], cw_ref[:, c * D:(c + 1) * D]) for c in range(3)]
        cts = []
        for c in range(3):
            m, p, x = ds[c]
            pe = jnp.where(first, 0.0, p[0] + p[1])
            ne = jnp.where(last, 0.0, x[0] + x[1])
            cts.append(jnp.concatenate([pe, m[0] + m[1], ne], axis=0))
        _, vjp = jax.vjp(_qkv_rows, *ces)
        dces = vjp(tuple(cts))
        rowi = lax.broadcasted_iota(jnp.int32, (E, 1), 0)
        central = (rowi >= HALO) & (rowi < HALO + bt)
        row8 = lax.broadcasted_iota(jnp.int32, (8, D), 0)
        for c in range(3):
            w = cw_ref[:, c * D:(c + 1) * D]
            o_ref[:, c * D:(c + 1) * D] = _conv3_t(dces[c], w)[HALO:HALO + bt].astype(o_ref.dtype)
            dc = jnp.where(central, dces[c], 0.0)
            taps = (jnp.sum(dc * _shift_dn(xes[c]), axis=0, keepdims=True),
                    jnp.sum(dc * xes[c], axis=0, keepdims=True),
                    jnp.sum(dc * _shift_up(xes[c]), axis=0, keepdims=True))
            upd = jnp.where(row8 == 0, taps[0], jnp.where(row8 == 1, taps[1], jnp.where(row8 == 2, taps[2], 0.0)))
            dcw_ref[:, c * D:(c + 1) * D] += upd
        lower, upper = _chunk_masks(bt)
        _, gvjp = jax.vjp(lambda ab, gp: _gate_rows(ab, gp, lower, upper), ab_ref[...], gp_ref[...])
        dab, dgp = gvjp(dgb_ref[0] + dgb_ref[1])
        dab_ref[...] = dab
        dgp_ref[...] += dgp

    in_specs = (_halo_specs(bt, T, CQ) + _halo_specs(bt, T, CK) + _halo_specs(bt, T, CV)
                + _halo_specs(bt, T, 0, lead=2) * 3
                + [pl.BlockSpec((bt, 128), lambda i: (i, 0)), pl.BlockSpec((2, bt, 128), lambda i: (0, i, 0)),
                   pl.BlockSpec((8, 3 * D), lambda i: (0, 0)), pl.BlockSpec((8, 128), lambda i: (0, 0)),
                   pl.BlockSpec(memory_space=pl.ANY)])
    out_specs = [pl.BlockSpec((bt, 3 * D), lambda i: (i, 0)), pl.BlockSpec((bt, 128), lambda i: (i, 0)),
                 pl.BlockSpec((8, 3 * D), lambda i: (0, 0)), pl.BlockSpec((8, 128), lambda i: (0, 0))]
    return pl.pallas_call(
        body, name="pre_qkv_bwd", grid=(n,), in_specs=in_specs, out_specs=out_specs,
        out_shape=[_sds(dpm.shape, dpm.dtype), _sds((T, 128)), _sds((8, 3 * D)), _sds((8, 128))],
        input_output_aliases={len(in_specs) - 1: 0},
        compiler_params=_cp(("arbitrary",)))(
            *([pm] * 9), dq2, dq2, dq2, dk2, dk2, dk2, dv2, dv2, dv2, pab, dgb2, cw, gp, dpm)


def pre_sc_fwd(pm, cw, *, bt=256):
    T = pm.shape[0]
    bt = min(bt, T)
    n = T // bt

    def body(b_ref, c0, c1, c2, x0, x1, x2, cw_ref, o_ref):
        i = pl.program_id(0)
        first, last = i == 0, i == n - 1
        pe = _ext(c0, c1, c2, first, last) * _ext(x0, x1, x2, first, last)
        o_ref[...] = (b_ref[...] * _conv3(pe, cw_ref[...])[HALO:HALO + bt]).astype(o_ref.dtype)

    in_specs = ([pl.BlockSpec((bt, D), lambda i: (i, CSB))] + _halo_specs(bt, T, CSC) + _halo_specs(bt, T, CSX)
                + [pl.BlockSpec((8, D), lambda i: (0, 0))])
    return pl.pallas_call(
        body, name="pre_sc_fwd", grid=(n,), in_specs=in_specs, out_specs=pl.BlockSpec((bt, D), lambda i: (i, 0)),
        out_shape=_sds((T, D), MM), compiler_params=_cp(("parallel",)))(*([pm] * 7), cw)


def pre_sc_bwd(pm, cw, dsc, dpm, *, bt=256):
    T = pm.shape[0]
    bt = min(bt, T)
    n = T // bt
    E = bt + 2 * HALO

    def body(b0, b1, b2, c0, c1, c2, x0, x1, x2, d0, d1, d2, cw_ref, _alias, o_ref, dcw_ref):
        i = pl.program_id(0)
        first, last = i == 0, i == n - 1

        @pl.when(first)
        def _():
            dcw_ref[...] = jnp.zeros_like(dcw_ref)

        ce, xe = _ext(c0, c1, c2, first, last), _ext(x0, x1, x2, first, last)
        pe = ce * xe
        w = cw_ref[...]
        dout = d0[...]
        o_ref[:, 0:D] = (dout * _conv3(pe, w)[HALO:HALO + bt]).astype(o_ref.dtype)
        dce = _ext(d0, d1, d2, first, last) * _ext(b0, b1, b2, first, last)
        dp = _conv3_t(dce, w)[HALO:HALO + bt]
        o_ref[:, D:2 * D] = (dp * x0[...]).astype(o_ref.dtype)
        o_ref[:, 2 * D:3 * D] = (dp * c0[...]).astype(o_ref.dtype)
        rowi = lax.broadcasted_iota(jnp.int32, (E, 1), 0)
        dc = jnp.where((rowi >= HALO) & (rowi < HALO + bt), dce, 0.0)
        row8 = lax.broadcasted_iota(jnp.int32, (8, D), 0)
        taps = (jnp.sum(dc * _shift_dn(pe), axis=0, keepdims=True), jnp.sum(dc * pe, axis=0, keepdims=True),
                jnp.sum(dc * _shift_up(pe), axis=0, keepdims=True))
        dcw_ref[...] += jnp.where(row8 == 0, taps[0], jnp.where(row8 == 1, taps[1], jnp.where(row8 == 2, taps[2], 0.0)))

    dsc_specs = [pl.BlockSpec((bt, D), lambda i: (i, 0)),
                 pl.BlockSpec((HALO, D), lambda i: (jnp.maximum(i * (bt // HALO) - 1, 0), 0)),
                 pl.BlockSpec((HALO, D), lambda i: (jnp.minimum((i + 1) * (bt // HALO), T // HALO - 1), 0))]
    in_specs = (_halo_specs(bt, T, CSB) + _halo_specs(bt, T, CSC) + _halo_specs(bt, T, CSX) + dsc_specs
                + [pl.BlockSpec((8, D), lambda i: (0, 0)), pl.BlockSpec(memory_space=pl.ANY)])
    return pl.pallas_call(
        body, name="pre_sc_bwd", grid=(n,), in_specs=in_specs,
        out_specs=[pl.BlockSpec((bt, 3 * D), lambda i: (i, 1)), pl.BlockSpec((8, D), lambda i: (0, 0))],
        out_shape=[_sds(dpm.shape, dpm.dtype), _sds((8, D))], input_output_aliases={len(in_specs) - 1: 0},
        compiler_params=_cp(("arbitrary",)))(*([pm] * 9), dsc, dsc, dsc, cw, dpm)


def _bdot(a, b, dims):
    return lax.dot_general(a.astype(MM), b.astype(MM), (dims, ((), ())), preferred_element_type=F32)


_NN, _NT, _TN = ((1,), (0,)), ((1,), (1,)), ((0,), (0,))


def _raw_nn(a, b):
    return _bdot(a, b, _NN)


def _raw_nt(a, b):
    return _bdot(a, b, _NT)


def _raw_tn(a, b):
    return _bdot(a, b, _TN)


def _make_vjp_ops():
    @jax.custom_vjp
    def nn(a, b):
        return _raw_nn(a, b)

    @jax.custom_vjp
    def nt(a, b):
        return _raw_nt(a, b)

    @jax.custom_vjp
    def tn(a, b):
        return _raw_tn(a, b)

    nn.defvjp(lambda a, b: (_raw_nn(a, b), (a, b)), lambda r, g: (_raw_nt(g, r[1]), _raw_tn(r[0], g)))
    nt.defvjp(lambda a, b: (_raw_nt(a, b), (a, b)), lambda r, g: (_raw_nn(g, r[1]), _raw_tn(g, r[0])))
    tn.defvjp(lambda a, b: (_raw_tn(a, b), (a, b)), lambda r, g: (_raw_nt(r[1], g), _raw_nn(r[0], g)))

    @jax.custom_vjp
    def inv_saved(A, Y):
        return Y

    def inv_bwd(Y, g):
        M = g + _raw_tn(Y, g)
        return -(M + _raw_nt(M, Y)), jnp.zeros_like(Y)

    inv_saved.defvjp(lambda A, Y: (Y, Y), inv_bwd)
    return nn, nt, tn, inv_saved


GH = 2
GR = GH * CH
NG = NH // GH
CBI = 2
CBB = 2


def _scan_chunks(n_chunks):
    return 4 if n_chunks % 4 == 0 else 2


def _tri_inv_y_all(As):
    Ys = [-A for A in As]
    Ps = [_raw_nn(A, A) for A in As]
    for stage in range(5):
        squares = [_raw_nn(P, P) for P in Ps] if stage < 4 else None
        Ys = [Y + P + _raw_nn(Y, P) for Y, P in zip(Ys, Ps)]
        Ps = squares
    return Ys


def _intra_groups(chains, incl, strict, eye, lastc, ops, inv_all):
    nn, nt, tn = ops
    st = []
    for qs, ks, vs, gcols, bcols in chains:
        q, k, v = (jnp.concatenate(t, axis=0) for t in (qs, ks, vs))
        gcol, bcol = jnp.concatenate(gcols, axis=0), jnp.concatenate(bcols, axis=0)
        grow = jnp.sum(eye * gcol, axis=0, keepdims=True)
        gam = jnp.where(incl, jnp.exp(jnp.where(incl, gcol - grow, 0.0)), 0.0)
        glast = jnp.sum(jnp.where(lastc, grow, 0.0), axis=1, keepdims=True)
        st.append((q, k, v, gcol, bcol, gam, glast, jnp.exp(gcol), k * bcol))
    As = [jnp.where(strict, nt(kb, k) * gam, 0.0) for (q, k, v, gcol, bcol, gam, glast, eg, kb) in st]
    Ys = inv_all(As)
    vbs = [v * bcol for (q, k, v, gcol, bcol, gam, glast, eg, kb) in st]
    kgs = [kb * eg for (q, k, v, gcol, bcol, gam, glast, eg, kb) in st]
    us = [vb + nn(Y, vb) for Y, vb in zip(Ys, vbs)]
    ws = [kg + nn(Y, kg) for Y, kg in zip(Ys, kgs)]
    Ps = [nt(q, k) * gam for (q, k, v, gcol, bcol, gam, glast, eg, kb) in st]
    return [((u, w, P, q * eg, k * jnp.exp(glast - gcol), jnp.exp(glast)), Y)
            for u, w, P, Y, (q, k, v, gcol, bcol, gam, glast, eg, kb) in zip(us, ws, Ps, Ys, st)]


def _scan_groups(chains, ops):
    nn, nt, tn, _ = ops
    vns = [[us[j] - nn(ws[j], Ss[j]) for j in range(GH)] for Ss, us, ws, P, qds, kds, egls in chains]
    os_ = [jnp.concatenate([nn(qds[j], Ss[j]) for j in range(GH)], axis=0) + nn(P, jnp.concatenate(vn, axis=0))
           for (Ss, us, ws, P, qds, kds, egls), vn in zip(chains, vns)]
    S2s = [[Ss[j] * egls[j] + tn(kds[j], vn[j]) for j in range(GH)]
           for (Ss, us, ws, P, qds, kds, egls), vn in zip(chains, vns)]
    return list(zip(os_, S2s))


def _group_masks(rev, rows=GR):
    r = lax.broadcasted_iota(jnp.int32, (rows, rows), 0)
    c = lax.broadcasted_iota(jnp.int32, (rows, rows), 1)
    same = (r // CH) == (c // CH)
    ahead = jnp.where(rev, c - r, r - c)
    incl = same & (ahead >= 0)
    strict = same & (ahead > 0)
    eye = jnp.where(r == c, 1.0, 0.0).astype(F32)
    lastc = same & ((c % CH) == jnp.where(rev, 0, CH - 1))
    return incl, strict, eye, lastc


def _head_gates(gb, h, rev):
    gcol = jnp.where(rev, gb[:, NH + h:NH + h + 1], gb[:, h:h + 1])
    bcol = jnp.where(rev, gb[:, 3 * NH + h:3 * NH + h + 1], gb[:, 2 * NH + h:2 * NH + h + 1])
    return gcol, bcol


def _hs(h):
    return slice(h * HD, (h + 1) * HD)


def _blockdiag(a, b):
    z = jnp.zeros_like(a)
    return jnp.concatenate([jnp.concatenate([a, z], axis=1), jnp.concatenate([z, b], axis=1)], axis=0)


def _load_chain(q_ref, k_ref, v_ref, gb_ref, c, g, rev, gh=GH):
    r = slice(c * CH, (c + 1) * CH)
    heads = range(g * gh, (g + 1) * gh)
    gates = [_head_gates(gb_ref[r, :], h, rev) for h in heads]
    return ([q_ref[r, _hs(h)] for h in heads], [k_ref[r, _hs(h)] for h in heads], [v_ref[r, _hs(h)] for h in heads],
            [t[0] for t in gates], [t[1] for t in gates])


def gdn_intra_fwd(q, k, v, gb):
    T = q.shape[0]
    N = T // CH
    ops = (_raw_nn, _raw_nt, _raw_tn)

    def body(q_ref, k_ref, v_ref, gb_ref, u_ref, w_ref, qd_ref, kd_ref, pp_ref, ys_ref, eg_ref):
        rev = pl.program_id(0) == 1
        masks = _group_masks(rev)
        where = [(c, g) for c in range(CBI) for g in range(NG)]
        chains = [_load_chain(q_ref, k_ref, v_ref, gb_ref, c, g, rev) for c, g in where]
        for (c, g), ((u, w, P, qd, kd, egl), Y) in zip(where, _intra_groups(chains, *masks, ops, _tri_inv_y_all)):
            r = slice(c * CH, (c + 1) * CH)
            pp_ref[c, g] = P.astype(MM)
            ys_ref[c, g] = Y.astype(MM)
            for j, h in enumerate(range(g * GH, (g + 1) * GH)):
                rows = slice(j * CH, (j + 1) * CH)
                u_ref[r, _hs(h)] = u[rows]
                w_ref[r, _hs(h)] = w[rows].astype(MM)
                qd_ref[r, _hs(h)] = qd[rows].astype(MM)
                kd_ref[r, _hs(h)] = kd[rows].astype(MM)
                eg_ref[c, h:h + 1, :] = jnp.broadcast_to(egl[j * CH:j * CH + 1, :], (1, 128))

    row = pl.BlockSpec((CBI * CH, D), lambda d, n: (n, 0))
    drow = pl.BlockSpec((None, CBI * CH, D), lambda d, n: (d, n, 0))
    mat = pl.BlockSpec((None, CBI, NG, GR, GR), lambda d, n: (d, n, 0, 0, 0))
    return pl.pallas_call(
        body, name="gdn_intra_fwd", grid=(2, N // CBI),
        in_specs=[row, row, row, pl.BlockSpec((CBI * CH, 128), lambda d, n: (n, 0))],
        out_specs=[drow] * 4 + [mat, mat, pl.BlockSpec((None, CBI, NH, 128), lambda d, n: (d, n, 0, 0))],
        out_shape=[_sds((2, T, D))] + [_sds((2, T, D), MM)] * 3 + [_sds((2, N, NG, GR, GR), MM)] * 2
                  + [_sds((2, N, NH, 128))],
        compiler_params=_cp(("parallel", "parallel")))(q, k, v, gb)


def gdn_scan_fwd(u, w, qd, kd, pp, eg):
    T = u.shape[1]
    N = T // CH
    ops = (_raw_nn, _raw_nt, _raw_tn, None)

    cbs = _scan_chunks(N)
    NB = N // cbs

    def body(u_ref, w_ref, qd_ref, kd_ref, pp_ref, eg_ref, o_ref, s0_ref, S):
        d = pl.program_id(0)

        @pl.when(pl.program_id(1) == 0)
        def _():
            S[...] = jnp.zeros_like(S)

        def chunk(c, carry):
            pc = c + d * (cbs - 1 - 2 * c)
            r = pl.ds(pl.multiple_of(pc * CH, CH), CH)
            chains = []
            for g in range(NG):
                heads = range(g * GH, (g + 1) * GH)
                Ss = [S[h] for h in heads]
                for h, Sh in zip(heads, Ss):
                    s0_ref[pc, h] = Sh
                chains.append((Ss, [u_ref[r, _hs(h)] for h in heads], [w_ref[r, _hs(h)] for h in heads], pp_ref[pc, g],
                               [qd_ref[r, _hs(h)] for h in heads], [kd_ref[r, _hs(h)] for h in heads],
                               [eg_ref[pc, h:h + 1, :] for h in heads]))
            for g, (o, S2) in enumerate(_scan_groups(chains, ops)):
                for j, h in enumerate(range(g * GH, (g + 1) * GH)):
                    o_ref[r, _hs(h)] = o[j * CH:(j + 1) * CH]
                    S[h] = S2[j]
            return carry

        lax.fori_loop(0, cbs, chunk, 0)

    bidx = lambda d, n: n + d * (NB - 1 - 2 * n)
    drow = pl.BlockSpec((None, cbs * CH, D), lambda d, n: (d, bidx(d, n), 0))
    mat = pl.BlockSpec((None, cbs, NG, GR, GR), lambda d, n: (d, bidx(d, n), 0, 0, 0))
    return pl.pallas_call(
        body, name="gdn_scan_fwd", grid=(2, NB),
        in_specs=[drow] * 4 + [mat, pl.BlockSpec((None, cbs, NH, 128), lambda d, n: (d, bidx(d, n), 0, 0))],
        out_specs=[drow, pl.BlockSpec((None, cbs, NH, HD, HD), lambda d, n: (d, bidx(d, n), 0, 0, 0))],
        out_shape=[_sds((2, T, D)), _sds((2, N, NH, HD, HD))],
        scratch_shapes=[pltpu.VMEM((NH, HD, HD), F32)],
        compiler_params=_cp(("arbitrary", "arbitrary")))(u, w, qd, kd, pp, eg)


def gdn_scan_bwd(u, w, qd, kd, pp, eg, s0, do):
    T = u.shape[1]
    N = T // CH
    ops = _make_vjp_ops()
    cbs = _scan_chunks(N)
    NB = N // cbs

    def body(u_ref, w_ref, qd_ref, kd_ref, pp_ref, eg_ref, s0_ref, do_ref,
             du_ref, dw_ref, dqd_ref, dkd_ref, dpp_ref, deg_ref, dS):
        d = pl.program_id(0)

        @pl.when(pl.program_id(1) == 0)
        def _():
            dS[...] = jnp.zeros_like(dS)

        def chunk(c, carry):
            pc = (cbs - 1 - c) + d * (2 * c - (cbs - 1))
            r = pl.ds(pl.multiple_of(pc * CH, CH), CH)
            chains, cts = [], []
            for g in range(NG):
                heads = range(g * GH, (g + 1) * GH)
                chains.append(([s0_ref[pc, h] for h in heads], [u_ref[r, _hs(h)] for h in heads],
                               [w_ref[r, _hs(h)].astype(F32) for h in heads], pp_ref[pc, g].astype(F32),
                               [qd_ref[r, _hs(h)].astype(F32) for h in heads],
                               [kd_ref[r, _hs(h)].astype(F32) for h in heads], [eg_ref[pc, h:h + 1, :] for h in heads]))
                cts.append((jnp.concatenate([do_ref[r, _hs(h)].astype(F32) for h in heads], axis=0),
                            [dS[h] for h in heads]))
            _, vjp = jax.vjp(lambda ch: _scan_groups(ch, ops), chains)
            (dchains,) = vjp(cts)
            for g, (dSs, dus, dws, dP, dqds, dkds, degs) in enumerate(dchains):
                dpp_ref[pc, g] = dP
                for j, h in enumerate(range(g * GH, (g + 1) * GH)):
                    dS[h] = dSs[j]
                    du_ref[r, _hs(h)] = dus[j].astype(MM)
                    dw_ref[r, _hs(h)] = dws[j].astype(MM)
                    dqd_ref[r, _hs(h)], dkd_ref[r, _hs(h)] = dqds[j], dkds[j]
                    deg_ref[pc, h:h + 1, :] = degs[j]
            return carry

        lax.fori_loop(0, cbs, chunk, 0)

    bidx = lambda d, n: (NB - 1 - n) + d * (2 * n - (NB - 1))
    drow = pl.BlockSpec((None, cbs * CH, D), lambda d, n: (d, bidx(d, n), 0))
    erow = pl.BlockSpec((None, cbs, NH, 128), lambda d, n: (d, bidx(d, n), 0, 0))
    mat = pl.BlockSpec((None, cbs, NG, GR, GR), lambda d, n: (d, bidx(d, n), 0, 0, 0))
    return pl.pallas_call(
        body, name="gdn_scan_bwd", grid=(2, NB),
        in_specs=[drow] * 4 + [mat, erow, pl.BlockSpec((None, cbs, NH, HD, HD), lambda d, n: (d, bidx(d, n), 0, 0, 0)),
                               pl.BlockSpec((cbs * CH, D), lambda d, n: (bidx(d, n), 0))],
        out_specs=[drow] * 4 + [mat, erow],
        out_shape=[_sds((2, T, D), MM)] * 2 + [_sds((2, T, D))] * 2 + [_sds((2, N, NG, GR, GR))]
                  + [_sds((2, N, NH, 128))],
        scratch_shapes=[pltpu.VMEM((NH, HD, HD), F32)],
        compiler_params=_cp(("arbitrary", "arbitrary")))(u, w, qd, kd, pp, eg, s0, do)


def gdn_intra_bwd(q, k, v, gb, ys, du, dw, dqd, dkd, dpp, deg):
    T = q.shape[0]
    N = T // CH
    nn, nt, tn, inv_saved = _make_vjp_ops()

    def body(q_ref, k_ref, v_ref, gb_ref, ys_ref, du_ref, dw_ref, dqd_ref, dkd_ref, dpp_ref, deg_ref,
             dq_ref, dk_ref, dv_ref, dgb_ref):
        rev = pl.program_id(0) == 1
        gh, rows = 2 * GH, 2 * GR
        masks = _group_masks(rev, rows)
        lane = lax.broadcasted_iota(jnp.int32, (CH, 128), 1)
        grow = lax.broadcasted_iota(jnp.int32, (rows, 1), 0)
        where = [(c, g) for c in range(CBB) for g in range(NG // 2)]
        pair = lambda ref, c, g: _blockdiag(ref[c, 2 * g].astype(F32), ref[c, 2 * g + 1].astype(F32))
        chains = [_load_chain(q_ref, k_ref, v_ref, gb_ref, c, g, rev, gh) for c, g in where]
        Ys = [pair(ys_ref, c, g) for c, g in where]
        inv_all = lambda As: [inv_saved(A, Y) for A, Y in zip(As, Ys)]
        _, vjp = jax.vjp(lambda ch: _intra_groups(ch, *masks, (nn, nt, tn), inv_all), chains)
        cts = []
        for c, g in where:
            r = slice(c * CH, (c + 1) * CH)
            heads = range(g * gh, (g + 1) * gh)
            stack = lambda ref: jnp.concatenate([ref[r, _hs(h)].astype(F32) for h in heads], axis=0)
            degl = jnp.zeros((rows, 1), F32)
            for j, h in enumerate(heads):
                degl = degl + jnp.where(grow == j * CH, jnp.sum(deg_ref[c, h:h + 1, :], axis=1, keepdims=True), 0.0)
            cts.append(((stack(du_ref), stack(dw_ref), pair(dpp_ref, c, g), stack(dqd_ref), stack(dkd_ref), degl),
                        jnp.zeros((rows, rows), F32)))
        (dchains,) = vjp(cts)
        dgbs = [jnp.zeros((CH, 128), F32) for _ in range(CBB)]
        for (c, g), (dqs, dks, dvs, dgs, dbs) in zip(where, dchains):
            r = slice(c * CH, (c + 1) * CH)
            for j, h in enumerate(range(g * gh, (g + 1) * gh)):
                dq_ref[r, _hs(h)], dk_ref[r, _hs(h)], dv_ref[r, _hs(h)] = dqs[j], dks[j], dvs[j]
                glane = jnp.where(rev, NH + h, h)
                dgbs[c] = dgbs[c] + jnp.where(lane == glane, dgs[j], 0.0) + jnp.where(lane == glane + 2 * NH, dbs[j], 0.0)
        for c in range(CBB):
            dgb_ref[c * CH:(c + 1) * CH, :] = dgbs[c]

    row = pl.BlockSpec((CBB * CH, D), lambda d, n: (n, 0))
    drow = pl.BlockSpec((None, CBB * CH, D), lambda d, n: (d, n, 0))
    mat = pl.BlockSpec((None, CBB, NG, GR, GR), lambda d, n: (d, n, 0, 0, 0))
    return pl.pallas_call(
        body, name="gdn_intra_bwd", grid=(2, N // CBB),
        in_specs=[row, row, row, pl.BlockSpec((CBB * CH, 128), lambda d, n: (n, 0)), mat, drow, drow, drow, drow, mat]
                 + [pl.BlockSpec((None, CBB, NH, 128), lambda d, n: (d, n, 0, 0))],
        out_specs=[drow, drow, drow, pl.BlockSpec((None, CBB * CH, 128), lambda d, n: (d, n, 0))],
        out_shape=[_sds((2, T, D))] * 3 + [_sds((2, T, 128))],
        compiler_params=_cp(("parallel", "parallel")))(q, k, v, gb, ys, du, dw, dqd, dkd, dpp, deg)


def _post_rows(o2a, o2b, z, nw):
    o = o2a + o2b
    outs = []
    for h in range(NH):
        s = slice(h * HD, (h + 1) * HD)
        oh = o[:, s]
        outs.append(oh * lax.rsqrt(jnp.mean(oh * oh, axis=-1, keepdims=True) + RMS_EPS) * nw * _silu(z[:, s]))
    return jnp.concatenate(outs, axis=1)


def post_fwd(o2, pm, nw, *, bt=512):
    T = pm.shape[0]
    bt = min(bt, T)

    def body(o_ref, z_ref, nw_ref, og_ref):
        og_ref[...] = _post_rows(o_ref[0], o_ref[1], z_ref[...], nw_ref[0:1, :]).astype(og_ref.dtype)

    return pl.pallas_call(
        body, name="post_fwd", grid=(T // bt,),
        in_specs=[pl.BlockSpec((2, bt, D), lambda i: (0, i, 0)), pl.BlockSpec((bt, D), lambda i: (i, CZ)),
                  pl.BlockSpec((8, 128), lambda i: (0, 0))],
        out_specs=pl.BlockSpec((bt, D), lambda i: (i, 0)), out_shape=_sds((T, D), MM),
        compiler_params=_cp(("parallel",)))(o2, pm, nw)


def post_bwd(o2, pm, nw, dog, dpm, *, bt=512):
    T = pm.shape[0]
    bt = min(bt, T)

    def body(o_ref, z_ref, nw_ref, dog_ref, _alias, do_ref, dz_ref, dnw_ref):
        @pl.when(pl.program_id(0) == 0)
        def _():
            dnw_ref[...] = jnp.zeros_like(dnw_ref)

        _, vjp = jax.vjp(_post_rows, o_ref[0], o_ref[1], z_ref[...], nw_ref[0:1, :])
        doa, _unused, dz, dnw = vjp(dog_ref[...])
        do_ref[...] = doa.astype(do_ref.dtype)
        dz_ref[...] = dz.astype(dz_ref.dtype)
        row8 = lax.broadcasted_iota(jnp.int32, (8, 128), 0)
        dnw_ref[...] += jnp.where(row8 == 0, dnw, 0.0)

    in_specs = [pl.BlockSpec((2, bt, D), lambda i: (0, i, 0)), pl.BlockSpec((bt, D), lambda i: (i, CZ)),
                pl.BlockSpec((8, 128), lambda i: (0, 0)), pl.BlockSpec((bt, D), lambda i: (i, 0)),
                pl.BlockSpec(memory_space=pl.ANY)]
    return pl.pallas_call(
        body, name="post_bwd", grid=(T // bt,), in_specs=in_specs,
        out_specs=[pl.BlockSpec((bt, D), lambda i: (i, 0)), pl.BlockSpec((bt, D), lambda i: (i, CZ)),
                   pl.BlockSpec((8, 128), lambda i: (0, 0))],
        out_shape=[_sds((T, D), MM), _sds(dpm.shape, dpm.dtype), _sds((8, 128))], input_output_aliases={4: 1},
        compiler_params=_cp(("arbitrary",)))(o2, pm, nw, dog, dpm)


def merge_fwd(ya, yb, pm, *, bt=512):
    T = pm.shape[0]
    bt = min(bt, T)

    def body(ya_ref, yb_ref, ga_ref, gb_ref, o_ref):
        o_ref[...] = (_sigmoid(ga_ref[...]) * ya_ref[...] + _sigmoid(gb_ref[...]) * yb_ref[...]).astype(o_ref.dtype)

    row = pl.BlockSpec((bt, D), lambda i: (i, 0))
    return pl.pallas_call(
        body, name="merge_fwd", grid=(T // bt,),
        in_specs=[row, row, pl.BlockSpec((bt, D), lambda i: (i, CGA)), pl.BlockSpec((bt, D), lambda i: (i, CGB))],
        out_specs=row, out_shape=_sds((T, D), MM), compiler_params=_cp(("parallel",)))(ya, yb, pm, pm)


def merge_bwd(ya, yb, pm, dmix, *, bt=512):
    T = pm.shape[0]
    bt = min(bt, T)

    def body(ya_ref, yb_ref, ga_ref, gb_ref, dm_ref, dya_ref, dyb_ref, dg_ref):
        dm = dm_ref[...]
        sa, sb = _sigmoid(ga_ref[...]), _sigmoid(gb_ref[...])
        dya_ref[...] = (dm * sa).astype(dya_ref.dtype)
        dyb_ref[...] = (dm * sb).astype(dyb_ref.dtype)
        dg_ref[:, 0:D] = (dm * ya_ref[...] * sa * (1.0 - sa)).astype(dg_ref.dtype)
        dg_ref[:, D:2 * D] = (dm * yb_ref[...] * sb * (1.0 - sb)).astype(dg_ref.dtype)

    row = pl.BlockSpec((bt, D), lambda i: (i, 0))
    return pl.pallas_call(
        body, name="merge_bwd", grid=(T // bt,),
        in_specs=[row, row, pl.BlockSpec((bt, D), lambda i: (i, CGA)), pl.BlockSpec((bt, D), lambda i: (i, CGB)), row],
        out_specs=[row, row, pl.BlockSpec((bt, 2 * D), lambda i: (i, CGA // 2))],
        out_shape=[_sds((T, D), MM), _sds((T, D), MM), _sds((T, NMAIN), MM)],
        compiler_params=_cp(("parallel",)))(ya, yb, pm, pm, dmix)


def _ln_rows(x, y, bias, g, b):
    r = ALPHA * x + y + bias
    mu = jnp.mean(r, axis=-1, keepdims=True)
    var = jnp.mean(jnp.square(r - mu), axis=-1, keepdims=True)
    return (r - mu) * lax.rsqrt(var + LN_EPS) * g + b


def ln_fwd(x, y, p, *, name, bt=512):
    T = x.shape[0]
    bt = min(bt, T)

    def body(x_ref, y_ref, p_ref, o_ref, ob_ref):
        r = _ln_rows(x_ref[...], y_ref[...], p_ref[0:1, :], p_ref[1:2, :], p_ref[2:3, :])
        o_ref[...] = r
        ob_ref[...] = r.astype(ob_ref.dtype)

    row = pl.BlockSpec((bt, D), lambda i: (i, 0))
    return pl.pallas_call(
        body, name=name, grid=(T // bt,), in_specs=[row, row, pl.BlockSpec((8, D), lambda i: (0, 0))],
        out_specs=[row, row], out_shape=[_sds((T, D)), _sds((T, D), MM)],
        compiler_params=_cp(("parallel",)))(x, y, p)


def ln_bwd(x, y, p, ct, ct2=None, *, name, bt=512):
    T = x.shape[0]
    bt = min(bt, T)

    def body(*refs):
        it = iter(refs)
        x_ref, y_ref, p_ref, c_ref = next(it), next(it), next(it), next(it)
        c2_ref = next(it) if ct2 is not None else None
        dxa_ref, dr_ref, dp_ref = next(it), next(it), next(it)

        @pl.when(pl.program_id(0) == 0)
        def _():
            dp_ref[...] = jnp.zeros_like(dp_ref)

        c = c_ref[...]
        if c2_ref is not None:
            c = c + c2_ref[...]
        _, vjp = jax.vjp(_ln_rows, x_ref[...], y_ref[...], p_ref[0:1, :], p_ref[1:2, :], p_ref[2:3, :])
        _dx, dy, dbias, dg, db = vjp(c)
        dxa_ref[...] = ALPHA * dy
        dr_ref[...] = dy.astype(dr_ref.dtype)
        row8 = lax.broadcasted_iota(jnp.int32, (8, D), 0)
        dp_ref[...] += jnp.where(row8 == 0, dbias, jnp.where(row8 == 1, dg, jnp.where(row8 == 2, db, 0.0)))

    row = pl.BlockSpec((bt, D), lambda i: (i, 0))
    in_specs = [row, row, pl.BlockSpec((8, D), lambda i: (0, 0)), row] + ([row] if ct2 is not None else [])
    args = [x, y, p, ct] + ([ct2] if ct2 is not None else [])
    return pl.pallas_call(
        body, name=name, grid=(T // bt,), in_specs=in_specs,
        out_specs=[row, row, pl.BlockSpec((8, D), lambda i: (0, 0))],
        out_shape=[_sds((T, D)), _sds((T, D), MM), _sds((8, D))],
        compiler_params=_cp(("arbitrary",)))(*args)


def loss_fwd_bwd(xl, target, *, bt=512):
    T = xl.shape[0]
    bt = min(bt, T)

    def body(x_ref, t_ref, l_ref, d_ref):
        @pl.when(pl.program_id(0) == 0)
        def _():
            l_ref[...] = jnp.zeros_like(l_ref)

        e = x_ref[...] - t_ref[...]
        d_ref[...] = e * (1.0 / D)
        l_ref[...] += 0.5 * jnp.sum(jnp.mean(e * e, axis=-1, keepdims=True), axis=0, keepdims=True)

    row = pl.BlockSpec((bt, D), lambda i: (i, 0))
    return pl.pallas_call(
        body, name="loss", grid=(T // bt,), in_specs=[row, row],
        out_specs=[pl.BlockSpec((8, 128), lambda i: (0, 0)), row], out_shape=[_sds((8, 128)), _sds((T, D))],
        compiler_params=_cp(("arbitrary",)))(xl, target)


def _row_tile(R, Cc, elems=1 << 18):
    if R * Cc <= elems:
        return R
    tr = 8
    while tr * 2 * Cc <= elems and R % (tr * 2) == 0:
        tr *= 2
    return tr


def adam(w, m, v, ga, gb=None, *, name):
    R, Cc = w.shape
    tr = _row_tile(R, Cc)

    def body(*refs):
        it = iter(refs)
        w_ref, m_ref, v_ref, a_ref = next(it), next(it), next(it), next(it)
        b_ref = next(it) if gb is not None else None
        g_ref, d_ref, mo_ref, vo_ref = next(it), next(it), next(it), next(it)
        g = a_ref[...]
        if b_ref is not None:
            g = g + b_ref[...]
        m2 = B1 * m_ref[...] + (1.0 - B1) * g
        v2 = B2 * v_ref[...] + (1.0 - B2) * jnp.square(g)
        m_hat = m2 / (1.0 - B1 ** STEP)
        v_hat = v2 / (1.0 - B2 ** STEP)
        g_ref[...] = g
        d_ref[...] = -LR * (m_hat / (jnp.sqrt(v_hat) + EPS) + WD * w_ref[...])
        mo_ref[...] = m2
        vo_ref[...] = v2

    blk = pl.BlockSpec((tr, Cc), lambda i: (i, 0))
    args = [w, m, v, ga] + ([gb] if gb is not None else [])
    return pl.pallas_call(
        body, name=name, grid=(R // tr,), in_specs=[blk] * len(args), out_specs=[blk] * 4,
        out_shape=[_sds((R, Cc))] * 4, compiler_params=_cp(("parallel",)))(*args)


def sum4(parts, *, name):
    _, R, Cc = parts.shape
    tr = _row_tile(R, Cc)

    def body(p_ref, out_ref):
        f = lambda t: t.astype(F32)
        out_ref[...] = ((f(p_ref[0]) + f(p_ref[1])) + f(p_ref[2])) + f(p_ref[3])

    return pl.pallas_call(
        body, name=name, grid=(R // tr,), in_specs=[pl.BlockSpec((4, tr, Cc), lambda i: (0, i, 0))],
        out_specs=pl.BlockSpec((tr, Cc), lambda i: (i, 0)), out_shape=_sds((R, Cc)),
        compiler_params=_cp(("parallel",)))(parts)


def _place():
    return lax.axis_index("x"), lax.axis_index("y"), lax.axis_index("c")


def _other_chips(x, y):
    return [(1 - x, y), (x, 1 - y), (1 - x, 1 - y)]


_ANY = pl.BlockSpec(memory_space=pl.ANY)


def allgather_xy(arrs):
    n = len(arrs)
    halves = [a.shape[0] // 2 for a in arrs]

    def body(*refs):
        ins, outs = refs[:n], refs[n:2 * n]
        send, recv, fsend, frecv, loc = refs[2 * n:]
        x, y, c = _place()
        me = 2 * x + y
        peers = _other_chips(x, y)
        local = [pltpu.make_async_copy(ins[a], outs[a].at[me], loc.at[a]) for a in range(n)]
        for cp in local:
            cp.start()

        def over_ici(a, j, block, src=None):
            px, py = peers[j]
            mine = pl.ds(c * halves[a], halves[a])
            dst = outs[a].at[block, mine]
            return pltpu.make_async_remote_copy(
                src_ref=dst if src is None else src.at[mine], dst_ref=dst, send_sem=send.at[3 * a + j],
                recv_sem=recv.at[3 * a + j], device_id=(px, py, c), device_id_type=MESH)

        def over_d2d(a, j, half):
            px, py = peers[j]
            rows = outs[a].at[2 * px + py, pl.ds(half * halves[a], halves[a])]
            return pltpu.make_async_remote_copy(
                src_ref=rows, dst_ref=rows, send_sem=fsend.at[3 * a + j], recv_sem=frecv.at[3 * a + j],
                device_id=(x, y, 1 - c), device_id_type=MESH)

        sends = [over_ici(a, j, me, src=ins[a]) for a in range(n) for j in range(3)]
        for cp in sends:
            cp.start()
        passed = []
        for a in range(n):
            for j, (px, py) in enumerate(peers):
                over_ici(a, j, 2 * px + py).wait_recv()
                passed.append(over_d2d(a, j, c))
                passed[-1].start()
        for a in range(n):
            for j in range(3):
                over_d2d(a, j, 1 - c).wait_recv()
        for cp in sends + passed:
            cp.wait_send()
        for cp in local:
            cp.wait()

    return pl.pallas_call(
        body, name="allgather_xy", in_specs=[_ANY] * n, out_specs=[_ANY] * n,
        out_shape=[_sds((4,) + a.shape, a.dtype) for a in arrs],
        scratch_shapes=[pltpu.SemaphoreType.DMA((3 * n,))] * 4 + [pltpu.SemaphoreType.DMA((n,))],
        compiler_params=pltpu.CompilerParams(has_side_effects=True))(*arrs)


_HBM = pl.BlockSpec(memory_space=pltpu.HBM)
_SEM = pl.BlockSpec(memory_space=pltpu.SEMAPHORE)
_EFFECT = pltpu.SideEffectType.DATAFLOW_SIDE_EFFECTING


def _xy_copy(kind, layer, src, land, send, recv, a, j, c, arriving):
    x, y = lax.axis_index("x"), lax.axis_index("y")
    px, py = ((1 - x, y), (x, 1 - y), (1 - x, 1 - y), (x, y))[j]
    if kind == "gather":
        dst = land.at[2 * px + py] if arriving else land.at[2 * x + y]
        src_view = dst if arriving else src
    else:
        dst = land.at[j, layer]
        src_view = src.at[2 * px + py]
    return pltpu.make_async_remote_copy(src_ref=src_view, dst_ref=dst, send_sem=send.at[4 * a + j],
                                        recv_sem=recv.at[4 * a + j], device_id=(px, py, c), device_id_type=MESH)


def xy_start(kind, name, arrs, after, zones=None, layer=0):
    n = len(arrs)
    if zones is None:
        shape = lambda a: ((4,) + a.shape) if kind == "gather" else ((4, DEPTH) + a.shape[1:])
        zones = [lax.empty(shape(a), a.dtype) for a in arrs]

    def body(*refs):
        ins, lands = refs[:n], refs[n:2 * n]
        send, recv = refs[2 * n + 1], refs[2 * n + 2]
        token = refs[-1]
        c = lax.axis_index("c")
        for a in range(n):
            for j in range(4):
                _xy_copy(kind, layer, ins[a], lands[a], send, recv, a, j, c, False).start()
        token[...] = jnp.zeros_like(token)

    hbm = lambda v: pltpu.with_memory_space_constraint(v, pltpu.HBM)
    outs = pl.pallas_call(
        body, name=name,
        out_shape=(pltpu.SemaphoreType.DMA((4 * n,)), pltpu.SemaphoreType.DMA((4 * n,)),
                   *[pltpu.HBM(a.shape, a.dtype) for a in arrs], *[pltpu.HBM(z.shape, z.dtype) for z in zones],
                   _sds((8, 128))),
        in_specs=[_HBM] * (2 * n) + [_ANY],
        out_specs=(_SEM, _SEM, *[_HBM] * (2 * n), pl.BlockSpec(memory_space=pltpu.VMEM)),
        input_output_aliases={i: 2 + i for i in range(2 * n)},
        compiler_params=pltpu.CompilerParams(has_side_effects=_EFFECT))(
            *[hbm(a) for a in arrs], *[hbm(z) for z in zones], after)
    return (kind, layer, outs[0], outs[1], list(outs[2:2 + n])), list(outs[2 + n:2 + 2 * n]), outs[-1]


def xy_wait(name, handles, zones, after):
    kind, layer, send_sems, recv_sems, srcs = handles
    n = len(srcs)

    def body(*refs):
        ins, lands = refs[:n], refs[n:2 * n]
        send, recv = refs[2 * n], refs[2 * n + 1]
        c = lax.axis_index("c")
        for a in range(n):
            for j in range(4):
                _xy_copy(kind, layer, ins[a], lands[a], send, recv, a, j, c, False).wait_send()
                _xy_copy(kind, layer, ins[a], lands[a], send, recv, a, j, c, True).wait_recv()

    outs = pl.pallas_call(
        body, name=name,
        out_shape=tuple(pltpu.HBM(v.shape, v.dtype) for v in srcs + zones),
        in_specs=[_HBM] * (2 * n) + [_SEM, _SEM, _ANY], out_specs=tuple([_HBM] * (2 * n)),
        input_output_aliases={i: i for i in range(2 * n)},
        compiler_params=pltpu.CompilerParams(has_side_effects=_EFFECT))(*srcs, *zones, send_sems, recv_sems, after)
    return list(outs[n:])


def swap_c(arrs):
    n = len(arrs)

    def body(*refs):
        ins, outs = refs[:n], refs[n:2 * n]
        send, recv = refs[2 * n:]
        x, y, c = _place()
        cps = [pltpu.make_async_remote_copy(src_ref=ins[a], dst_ref=outs[a], send_sem=send.at[a], recv_sem=recv.at[a],
                                            device_id=(x, y, 1 - c), device_id_type=MESH) for a in range(n)]
        for cp in cps:
            cp.start()
        for cp in cps:
            cp.wait_recv()
        for cp in cps:
            cp.wait_send()

    return pl.pallas_call(
        body, name="swap_c", in_specs=[_ANY] * n, out_specs=[_ANY] * n, out_shape=[_sds(a.shape, a.dtype) for a in arrs],
        scratch_shapes=[pltpu.SemaphoreType.DMA((n,)), pltpu.SemaphoreType.DMA((n,))],
        compiler_params=pltpu.CompilerParams(has_side_effects=True))(*arrs)


def allreduce_small(v):
    R = v.shape[0]

    def body(v_ref, o_ref, buf, send, recv):
        x, y, c = _place()
        me = 4 * x + 2 * y + c
        buf[0] = v_ref[...]

        def cp(k):
            dx, dy, dc = (k >> 2) & 1, (k >> 1) & 1, k & 1
            return pltpu.make_async_remote_copy(
                src_ref=v_ref, dst_ref=buf.at[k], send_sem=send.at[k - 1], recv_sem=recv.at[k - 1],
                device_id=(x ^ dx, y ^ dy, c ^ dc), device_id_type=MESH)

        cps = [cp(k) for k in range(1, 8)]
        for t in cps:
            t.start()
        for t in cps:
            t.wait_recv()
        acc = buf[me]
        for dev in range(1, 8):
            acc = acc + buf[jnp.bitwise_xor(me, dev)]
        o_ref[...] = acc
        for t in cps:
            t.wait_send()

    vm = pl.BlockSpec(memory_space=pltpu.VMEM)
    return pl.pallas_call(
        body, name="allreduce_small", in_specs=[vm], out_specs=vm, out_shape=_sds((R, 128)),
        scratch_shapes=[pltpu.VMEM((8, R, 128), F32), pltpu.SemaphoreType.DMA((7,)), pltpu.SemaphoreType.DMA((7,))],
        compiler_params=pltpu.CompilerParams(has_side_effects=True, vmem_limit_bytes=VMEM_LIMIT))(v)


def _rows8(*rows):
    n = rows[0].shape[-1]
    t = jnp.stack([r.reshape(n).astype(F32) for r in rows])
    return jnp.pad(t, ((0, 8 - len(rows)), (0, 0)))


def _lanes128(a):
    f = a.reshape(-1).astype(F32)
    return jnp.pad(f, (0, 128 - f.shape[0]))


def _layer_fwd(x, xb, W):
    pm = mm_nn(xb, W["w_main"], name="proj_main")
    pab = mm_nn(xb, W["w_ab"], name="proj_ab")
    q, k, v, gb = pre_qkv_fwd(pm, pab, W["cw"], W["gp"])
    sc = pre_sc_fwd(pm, W["csc"])
    u, w, qd, kd, pp, ys, eg = gdn_intra_fwd(q, k, v, gb)
    o2, s0 = gdn_scan_fwd(u, w, qd, kd, pp, eg)
    og = post_fwd(o2, pm, W["nw"])
    ya = mm_nn(og, W["w_og"], name="proj_og")
    yb = mm_nn(sc, W["w_osc"], name="proj_osc")
    mixed = merge_fwd(ya, yb, pm)
    out = mm_nn(mixed, W["w_out"], name="proj_out")
    x1, x1b = ln_fwd(x, out, W["ln1"], name="ln1_fwd")
    hfac, h = mm_nn(x1b, W["w_up"], bias=W["b_up"], relu2=True, out_dtype=MM, name="mlp_up")
    dn = mm_nn(h, W["w_down"], name="mlp_down")
    x2, x2b = ln_fwd(x1, dn, W["ln2"], name="ln2_fwd")
    saved = dict(x=x, xb=xb, pm=pm, pab=pab, q=q, k=k, v=v, gb=gb, sc=sc, o2=o2, s0=s0, og=og, ya=ya, yb=yb,
                 u=u, w=w, qd=qd, kd=kd, pp=pp, ys=ys, eg=eg,
                 mixed=mixed, out=out, x1=x1, x1b=x1b, hfac=hfac, h=h, dn=dn)
    return x2, x2b, saved


def _layer_bwd(ct, W, S):
    dxa2, dr2b, dp2 = ln_bwd(S["x1"], S["dn"], W["ln2"], ct, name="ln2_bwd")
    g_down = mm_tn(S["h"], dr2b, name="dw_down")
    dhpre, db_up = mm_nt(dr2b, W["w_down"], dact=S["hfac"], out_dtype=MM, name="mlp_down_bwd")
    g_up = mm_tn(S["x1b"], dhpre, name="dw_up")
    dx1 = mm_nt(dhpre, W["w_up"], add=dxa2, name="mlp_up_bwd")
    dxa1, dr1b, dp1 = ln_bwd(S["x"], S["out"], W["ln1"], dx1, name="ln1_bwd")
    g_out = mm_tn(S["mixed"], dr1b, name="dw_out")
    dmix = mm_nt(dr1b, W["w_out"], name="proj_out_bwd")
    dya, dyb, dpm = merge_bwd(S["ya"], S["yb"], S["pm"], dmix)
    g_og = mm_tn(S["og"], dya, name="dw_og")
    g_osc = mm_tn(S["sc"], dyb, name="dw_osc")
    dog = mm_nt(dya, W["w_og"], name="proj_og_bwd")
    dsc = mm_nt(dyb, W["w_osc"], name="proj_osc_bwd")
    do, dpm, dnw = post_bwd(S["o2"], S["pm"], W["nw"], dog, dpm)
    du, dw, dqd, dkd, dpp, deg = gdn_scan_bwd(S["u"], S["w"], S["qd"], S["kd"], S["pp"], S["eg"], S["s0"], do)
    dq2, dk2, dv2, dgb2 = gdn_intra_bwd(S["q"], S["k"], S["v"], S["gb"], S["ys"], du, dw, dqd, dkd, dpp, deg)
    dpm, dpab, dcw, dgp = pre_qkv_bwd(S["pm"], S["pab"], W["cw"], W["gp"], dq2, dk2, dv2, dgb2, dpm)
    dpm, dcsc = pre_sc_bwd(S["pm"], W["csc"], dsc, dpm)
    g_main = mm_tn(S["xb"], dpm, name="dw_main")
    g_ab = mm_tn(S["xb"], dpab, name="dw_ab")
    t = mm_nt(dpab, W["w_ab"], add=dxa1, name="proj_ab_bwd")
    dx = mm_nt(dpm, W["w_main"], add=t, name="proj_main_bwd")
    g_in = jnp.concatenate([g_main[:, :3 * D], g_main[:, 8 * D:], g_ab[:, :4 * NH], g_main[:, 3 * D:8 * D]], axis=1)
    grads = dict(
        w_in=g_in, w_o_gdn=g_og, w_o_sc=g_osc, w_out=g_out, w_up=g_up, w_down=g_down,
        conv_qkv=dcw[:3], conv_sc=dcsc[:3], a_log=dgp[0, :2 * NH].reshape(2, NH), dt_bias=dgp[1, :2 * NH].reshape(2, NH),
        gdn_norm_w=dnw[0], ln1_g=dp1[1], ln1_b=dp1[2], b_up=db_up[0], b_down=dp2[0], ln2_g=dp2[1], ln2_b=dp2[2])
    return dx, grads


def _layer_weights(l, full, i, a_log, dt_bias, gdn_norm_w, ln1_g, ln1_b, b_up, b_down, ln2_g, ln2_b):
    w_in = full["w_in"][i]
    w_main = jnp.concatenate([w_in[:, :3 * D], w_in[:, 4 * D + 4 * NH:], w_in[:, 3 * D:4 * D]], axis=1)
    w_ab = jnp.pad(w_in[:, 4 * D:4 * D + 4 * NH], ((0, 0), (0, 128 - 4 * NH)))
    return dict(
        w_main=w_main, w_ab=w_ab, w_og=full["w_o_gdn"][i], w_osc=full["w_o_sc"][i], w_out=full["w_out"][i],
        w_up=full["w_up"][i], w_down=full["w_down"][i],
        cw=jnp.pad(full["conv_qkv"][i].astype(F32), ((0, 5), (0, 0))),
        csc=jnp.pad(full["conv_sc"][i].astype(F32), ((0, 5), (0, 0))),
        gp=_rows8(_lanes128(a_log[l]), _lanes128(dt_bias[l])), nw=_rows8(gdn_norm_w[l]),
        ln1=_rows8(jnp.zeros((D,), F32), ln1_g[l], ln1_b[l]), ln2=_rows8(b_down[l], ln2_g[l], ln2_b[l]),
        b_up=b_up[l].reshape(1, DFF).astype(F32))


def local_step(xs, target, weights_of, after_bwd):
    x, xb = xs, xs.astype(MM)
    Ws, saved = [], []
    for l in range(DEPTH):
        Ws.append(weights_of(l, x))
        x, xb, S = _layer_fwd(x, xb, Ws[l])
        saved.append(S)
    loss_tile, ct = loss_fwd_bwd(x, target)
    token = None
    for l in reversed(range(DEPTH)):
        W = Ws[l] if token is None else dict(Ws[l], ln2=Ws[l]["ln2"] + token[0, 0])
        ct, grads = _layer_bwd(ct, W, saved[l])
        token = after_bwd(l, grads)
    return loss_tile, ct


EARLY = 2
BIG = ("w_in", "w_o_gdn", "w_o_sc", "w_out", "w_up", "w_down")
SMALL = ("conv_qkv", "a_log", "dt_bias", "gdn_norm_w", "conv_sc", "ln1_g", "ln1_b", "b_up", "b_down", "ln2_g", "ln2_b")
ORDER = ("w_in", "conv_qkv", "a_log", "dt_bias", "gdn_norm_w", "w_o_gdn", "conv_sc", "w_o_sc", "w_out", "ln1_g",
         "ln1_b", "w_up", "b_up", "w_down", "b_down", "ln2_g", "ln2_b")


def _pack(arrs):
    flat = jnp.concatenate([a.reshape(-1).astype(F32) for a in arrs])
    n = flat.shape[0]
    rows = -(-n // 1024) * 8
    return jnp.pad(flat, (0, rows * 128 - n)).reshape(rows, 128)


def _unpack(buf, like):
    flat = buf.reshape(-1)
    out, o = [], 0
    for a in like:
        n = 1
        for s in a.shape:
            n *= s
        out.append(flat[o:o + n].reshape(a.shape))
        o += n
    return out


def _gathered(name, g):
    if name in ("w_in", "w_up", "conv_qkv", "conv_sc"):
        t = jnp.moveaxis(g, 0, -2)
        return t.reshape(t.shape[:-2] + (t.shape[-2] * t.shape[-1],))
    t = jnp.moveaxis(g, 0, 1)
    return t.reshape((t.shape[0], t.shape[1] * t.shape[2]) + t.shape[3:])


def _by_chip(name, g):
    if name in ("w_in", "w_up"):
        r, ccols = g.shape
        return jnp.moveaxis(g.reshape(r, 4, ccols // 4), 1, 0)
    return g.reshape((4, g.shape[0] // 4) + g.shape[1:])


def kernel(x, w_in, conv_qkv, a_log, dt_bias, gdn_norm_w, w_o_gdn, conv_sc, w_o_sc, w_out, ln1_g, ln1_b, w_up, b_up, w_down, b_down, ln2_g, ln2_b, loss_target, m_w_in, m_conv_qkv, m_a_log, m_dt_bias, m_gdn_norm_w, m_w_o_gdn, m_conv_sc, m_w_o_sc, m_w_out, m_ln1_g, m_ln1_b, m_w_up, m_b_up, m_w_down, m_b_down, m_ln2_g, m_ln2_b, v_w_in, v_conv_qkv, v_a_log, v_dt_bias, v_gdn_norm_w, v_w_o_gdn, v_conv_sc, v_w_o_sc, v_w_out, v_ln1_g, v_ln1_b, v_w_up, v_b_up, v_w_down, v_b_down, v_ln2_g, v_ln2_b):
    w = dict(w_in=w_in, conv_qkv=conv_qkv, a_log=a_log, dt_bias=dt_bias, gdn_norm_w=gdn_norm_w, w_o_gdn=w_o_gdn,
             conv_sc=conv_sc, w_o_sc=w_o_sc, w_out=w_out, ln1_g=ln1_g, ln1_b=ln1_b, w_up=w_up, b_up=b_up,
             w_down=w_down, b_down=b_down, ln2_g=ln2_g, ln2_b=ln2_b)
    m = dict(w_in=m_w_in, conv_qkv=m_conv_qkv, a_log=m_a_log, dt_bias=m_dt_bias, gdn_norm_w=m_gdn_norm_w,
             w_o_gdn=m_w_o_gdn, conv_sc=m_conv_sc, w_o_sc=m_w_o_sc, w_out=m_w_out, ln1_g=m_ln1_g, ln1_b=m_ln1_b,
             w_up=m_w_up, b_up=m_b_up, w_down=m_w_down, b_down=m_b_down, ln2_g=m_ln2_g, ln2_b=m_ln2_b)
    v = dict(w_in=v_w_in, conv_qkv=v_conv_qkv, a_log=v_a_log, dt_bias=v_dt_bias, gdn_norm_w=v_gdn_norm_w,
             w_o_gdn=v_w_o_gdn, conv_sc=v_conv_sc, w_o_sc=v_w_o_sc, w_out=v_w_out, ln1_g=v_ln1_g, ln1_b=v_ln1_b,
             w_up=v_w_up, b_up=v_b_up, w_down=v_w_down, b_down=v_b_down, ln2_g=v_ln2_g, ln2_b=v_ln2_b)
    chip = 2 * lax.axis_index("x") + lax.axis_index("y")

    names = BIG + ("conv_qkv", "conv_sc")
    blocks = [w[n].astype(MM) if n in BIG else w[n] for n in names]
    got = allgather_xy([b[:EARLY] for b in blocks])
    early = {n: _gathered(n, g) for n, g in zip(names, got)}
    gather, gather_zones, token = xy_start("gather", "gather_start", [b[EARLY:] for b in blocks], after=got[0])
    vectors = (a_log, dt_bias, gdn_norm_w, ln1_g, ln1_b, b_up, b_down, ln2_g, ln2_b)
    late, grads, scatters, zones = {}, [None] * DEPTH, [None] * DEPTH, [None]

    def weights_of(l, x_l):
        if l < EARLY:
            return _layer_weights(l, early, l, *vectors)
        if not late:
            for n, zone in zip(names, xy_wait("gather_wait", gather, gather_zones, after=x_l)):
                late[n] = _gathered(n, zone)
        return _layer_weights(l, late, l - EARLY, *vectors)

    def after_bwd(l, g):
        grads[l] = g
        scatters[l], zones[0], tok = xy_start(
            "scatter", "scatter_start_%d" % l, [_by_chip(n, g[n]).astype(MM) for n in BIG], after=g["ln2_g"],
            zones=zones[0], layer=l)
        return tok

    loss_tile, dx = local_step(x[0] + token[0, 0], loss_target[0], weights_of, after_bwd)
    loss = lax.psum(loss_tile[0, 0], ("x", "y", "c"))

    arrived = zones[0]
    for l in range(DEPTH):
        arrived = xy_wait("scatter_wait_%d" % l, scatters[l], arrived, after=dx)
    part = [sum4(z.reshape(4, -1, z.shape[-1]), name="sum_" + n) for n, z in zip(BIG, arrived)]
    other = swap_c(part)
    out = {}
    for n, mine, theirs in zip(BIG, part, other):
        cols = mine.shape[-1]
        res = adam(w[n].reshape(-1, cols), m[n].reshape(-1, cols), v[n].reshape(-1, cols), mine, theirs, name="adam_" + n)
        out[n] = [r.reshape(w[n].shape) for r in res]

    stacked = [jnp.stack([grads[l][n] for l in range(DEPTH)]) for n in SMALL]
    summed = _unpack(allreduce_small(_pack(stacked)), stacked)
    gs = []
    for n, g in zip(SMALL, summed):
        if n in ("conv_qkv", "conv_sc"):
            blk = w[n].shape[-1]
            g = lax.dynamic_slice_in_dim(g, chip * blk, blk, axis=2)
        gs.append(g)
    res = adam(_pack([w[n] for n in SMALL]), _pack([m[n] for n in SMALL]), _pack([v[n] for n in SMALL]), _pack(gs),
               name="adam_small")
    for n, parts in zip(SMALL, zip(*[_unpack(r, gs) for r in res])):
        out[n] = list(parts)

    outs = [loss, dx[None]]
    for kind in range(4):
        outs += [out[n][kind] for n in ORDER]
    return tuple(outs)
```

```python
import functools

import jax
import jax.numpy as jnp
from jax import lax
from jax.experimental import pallas as pl
from jax.experimental.pallas import tpu as pltpu

F32 = jnp.float32
MM = jnp.bfloat16
HI = lax.Precision.HIGHEST

D = 1024
NH = 8
HD = 128
CH = 64
DFF = 4 * D
DEPTH = 4
LN_EPS = 1e-5
RMS_EPS = 1e-6
L2_EPS = 1e-6
ALPHA = (2 * DEPTH) ** 0.25
LR, B1, B2, EPS, WD, STEP = 0.001, 0.9, 0.999, 1e-08, 0.01, 10

NMAIN = 9 * D
CQ, CK, CV, CSB, CSC, CSX, CGA, CGB, CZ = range(9)
HALO = 8
VMEM_LIMIT = 56 * 1024 * 1024
MESH = pl.DeviceIdType.MESH


def _cp(sem=None, vmem=VMEM_LIMIT):
    return pltpu.CompilerParams(dimension_semantics=sem, vmem_limit_bytes=vmem)


def _sds(shape, dtype=F32):
    return jax.ShapeDtypeStruct(tuple(shape), dtype)


def _accumulate(acc, product, k, nk, finish):
    if nk == 1:
        finish(product())
        return

    @pl.when(k == 0)
    def _():
        acc[...] = jnp.zeros_like(acc)

    acc[...] += product()

    @pl.when(k == nk - 1)
    def _():
        finish(acc[...])


def mm_nn(a, b, *, name, bias=None, relu2=False, add=None, out_dtype=F32, tm=1024, tn=1024, tk=1024):
    M, K = a.shape
    N = b.shape[1]
    tm, tn, tk = min(tm, M), min(tn, N), min(tk, K)
    nk = K // tk

    def body(*refs):
        it = iter(refs)
        a_ref, b_ref = next(it), next(it)
        bias_ref = next(it) if bias is not None else None
        add_ref = next(it) if add is not None else None
        o_ref = next(it)
        h_ref = next(it) if relu2 else None
        acc = next(it) if nk > 1 else None
        prod = lambda: jnp.dot(a_ref[...].astype(MM), b_ref[...].astype(MM), preferred_element_type=F32)

        def finish(r):
            if bias_ref is not None:
                r = r + bias_ref[...]
            if add_ref is not None:
                r = r + add_ref[...]
            if relu2:
                t = jnp.maximum(r, 0.0)
                o_ref[...] = (2.0 * t).astype(o_ref.dtype)
                h_ref[...] = (t * t).astype(h_ref.dtype)
            else:
                o_ref[...] = r.astype(o_ref.dtype)

        _accumulate(acc, prod, pl.program_id(2), nk, finish)

    in_specs = [pl.BlockSpec((tm, tk), lambda i, j, k: (i, k)), pl.BlockSpec((tk, tn), lambda i, j, k: (k, j))]
    args = [a, b]
    if bias is not None:
        in_specs.append(pl.BlockSpec((1, tn), lambda i, j, k: (0, j)))
        args.append(bias)
    if add is not None:
        in_specs.append(pl.BlockSpec((tm, tn), lambda i, j, k: (i, j)))
        args.append(add)
    out_shape = [_sds((M, N), out_dtype)]
    out_specs = [pl.BlockSpec((tm, tn), lambda i, j, k: (i, j))]
    if relu2:
        out_shape.append(_sds((M, N), MM))
        out_specs.append(pl.BlockSpec((tm, tn), lambda i, j, k: (i, j)))
    res = pl.pallas_call(
        body, name=name, grid=(M // tm, N // tn, nk), in_specs=in_specs, out_specs=out_specs, out_shape=out_shape,
        scratch_shapes=[pltpu.VMEM((tm, tn), F32)] if nk > 1 else [],
        compiler_params=_cp(("parallel", "parallel", "arbitrary")))(*args)
    return res if relu2 else res[0]


def mm_nt(a, b, *, name, add=None, dact=None, out_dtype=F32, tm=1024, tn=1024, tk=1024):
    M, Nc = a.shape
    Ko = b.shape[0]
    tm, tn, tk = min(tm, M), min(tn, Ko), min(tk, Nc)
    nk = Nc // tk
    ni = M // tm

    def body(*refs):
        it = iter(refs)
        a_ref, b_ref = next(it), next(it)
        add_ref = next(it) if add is not None else None
        d_ref = next(it) if dact is not None else None
        o_ref = next(it)
        db_ref = next(it) if dact is not None else None
        acc = next(it) if nk > 1 else None
        i = pl.program_id(1)
        prod = lambda: lax.dot_general(a_ref[...].astype(MM), b_ref[...].astype(MM), (((1,), (1,)), ((), ())),
                                       preferred_element_type=F32)

        def finish(r):
            if add_ref is not None:
                r = r + add_ref[...]
            if d_ref is not None:
                r = r * d_ref[...].astype(F32)
                s = jnp.sum(r, axis=0, keepdims=True)
                row0 = lax.broadcasted_iota(jnp.int32, db_ref.shape, 0) == 0

                @pl.when(i == 0)
                def _():
                    db_ref[...] = jnp.zeros_like(db_ref)

                db_ref[...] += jnp.where(row0, s, 0.0)
            o_ref[...] = r.astype(o_ref.dtype)

        _accumulate(acc, prod, pl.program_id(2), nk, finish)

    in_specs = [pl.BlockSpec((tm, tk), lambda j, i, k: (i, k)), pl.BlockSpec((tn, tk), lambda j, i, k: (j, k))]
    args = [a, b]
    for extra in (add, dact):
        if extra is not None:
            in_specs.append(pl.BlockSpec((tm, tn), lambda j, i, k: (i, j)))
            args.append(extra)
    out_shape = [_sds((M, Ko), out_dtype)]
    out_specs = [pl.BlockSpec((tm, tn), lambda j, i, k: (i, j))]
    if dact is not None:
        out_shape.append(_sds((8, Ko), F32))
        out_specs.append(pl.BlockSpec((8, tn), lambda j, i, k: (0, j)))
    res = pl.pallas_call(
        body, name=name, grid=(Ko // tn, ni, nk), in_specs=in_specs, out_specs=out_specs, out_shape=out_shape,
        scratch_shapes=[pltpu.VMEM((tm, tn), F32)] if nk > 1 else [],
        compiler_params=_cp(("parallel", "arbitrary", "arbitrary")))(*args)
    return res if dact is not None else res[0]


def mm_tn(a, b, *, name, tm=1024, tn=1024, tk=1024):
    T, M = a.shape
    N = b.shape[1]
    tm, tn, tk = min(tm, M), min(tn, N), min(tk, T)

    def body(a_ref, b_ref, o_ref):
        @pl.when(pl.program_id(2) == 0)
        def _():
            o_ref[...] = jnp.zeros_like(o_ref)

        o_ref[...] += lax.dot_general(a_ref[...].astype(MM), b_ref[...].astype(MM), (((0,), (0,)), ((), ())),
                                      preferred_element_type=F32)

    return pl.pallas_call(
        body, name=name, grid=(M // tm, N // tn, T // tk),
        in_specs=[pl.BlockSpec((tk, tm), lambda i, j, k: (k, i)), pl.BlockSpec((tk, tn), lambda i, j, k: (k, j))],
        out_specs=pl.BlockSpec((tm, tn), lambda i, j, k: (i, j)), out_shape=_sds((M, N)),
        compiler_params=_cp(("parallel", "parallel", "arbitrary")))(a, b)


def _sigmoid(x):
    return 1.0 / (1.0 + jnp.exp(-x))


def _silu(x):
    return x * _sigmoid(x)


def _softplus(x):
    return jnp.maximum(x, 0.0) + jnp.log1p(jnp.exp(-jnp.abs(x)))


def _ext(main_ref, prev_ref, next_ref, first, last):
    p = jnp.where(first, 0.0, prev_ref[...].astype(F32))
    n = jnp.where(last, 0.0, next_ref[...].astype(F32))
    return jnp.concatenate([p, main_ref[...].astype(F32), n], axis=0)


def _shift_dn(x):
    return pltpu.roll(x, 1, 0)


def _shift_up(x):
    return pltpu.roll(x, x.shape[0] - 1, 0)


def _conv3(xe, w):
    return w[0:1, :] * _shift_dn(xe) + w[1:2, :] * xe + w[2:3, :] * _shift_up(xe)


def _conv3_t(de, w):
    return w[0:1, :] * _shift_up(de) + w[1:2, :] * de + w[2:3, :] * _shift_dn(de)


def _halo_specs(bt, T, col, lead=None):
    r = bt // HALO
    last = T // HALO - 1
    if lead is None:
        return [pl.BlockSpec((bt, D), lambda i: (i, col)),
                pl.BlockSpec((HALO, D), lambda i: (jnp.maximum(i * r - 1, 0), col)),
                pl.BlockSpec((HALO, D), lambda i: (jnp.minimum((i + 1) * r, last), col))]
    return [pl.BlockSpec((lead, bt, D), lambda i: (0, i, col)),
            pl.BlockSpec((lead, HALO, D), lambda i: (0, jnp.maximum(i * r - 1, 0), col)),
            pl.BlockSpec((lead, HALO, D), lambda i: (0, jnp.minimum((i + 1) * r, last), col))]


def _qkv_rows(cq, ck, cv):
    sq, sk, sv = _silu(cq), _silu(ck), _silu(cv)
    qs, ks = [], []
    for h in range(NH):
        s = slice(h * HD, (h + 1) * HD)
        qh, kh = sq[:, s], sk[:, s]
        qs.append(qh * lax.rsqrt(jnp.sum(qh * qh, axis=-1, keepdims=True) + L2_EPS) * (HD ** -0.5))
        ks.append(kh * lax.rsqrt(jnp.sum(kh * kh, axis=-1, keepdims=True) + L2_EPS))
    return jnp.concatenate(qs, axis=1), jnp.concatenate(ks, axis=1), sv


def _chunk_masks(bt):
    row = lax.broadcasted_iota(jnp.int32, (bt, bt), 0)
    col = lax.broadcasted_iota(jnp.int32, (bt, bt), 1)
    same = (row // CH) == (col // CH)
    lower = jnp.where(same & (col <= row), 1.0, 0.0).astype(F32)
    upper = jnp.where(same & (col >= row), 1.0, 0.0).astype(F32)
    return lower, upper


def _gate_rows(ab, gp, lower, upper):
    lane = lax.broadcasted_iota(jnp.int32, ab.shape, 1)
    g = -jnp.exp(gp[0:1, :]) * _softplus(ab + gp[1:2, :])
    g = jnp.where(lane < 2 * NH, g, 0.0)
    gf = jnp.dot(lower, g, precision=HI, preferred_element_type=F32)
    gr = jnp.dot(upper, g, precision=HI, preferred_element_type=F32)
    gc = jnp.where(lane < NH, gf, gr)
    beta = _sigmoid(ab)
    return jnp.where(lane < 2 * NH, gc, jnp.where(lane < 4 * NH, beta, 0.0))


def pre_qkv_fwd(pm, pab, cw, gp, *, bt=256):
    T = pm.shape[0]
    bt = min(bt, T)
    n = T // bt

    def body(q0, q1, q2, k0, k1, k2, v0, v1, v2, ab_ref, cw_ref, gp_ref, q_ref, k_ref, v_ref, gb_ref):
        i = pl.program_id(0)
        first, last = i == 0, i == n - 1
        cs = []
        for c, (m, p, x) in enumerate(((q0, q1, q2), (k0, k1, k2), (v0, v1, v2))):
            xe = _ext(m, p, x, first, last)
            cs.append(_conv3(xe, cw_ref[:, c * D:(c + 1) * D])[HALO:HALO + bt])
        q, k, v = _qkv_rows(*cs)
        q_ref[...], k_ref[...], v_ref[...] = q, k, v
        lower, upper = _chunk_masks(bt)
        gb_ref[...] = _gate_rows(ab_ref[...], gp_ref[...], lower, upper)

    in_specs = (_halo_specs(bt, T, CQ) + _halo_specs(bt, T, CK) + _halo_specs(bt, T, CV)
                + [pl.BlockSpec((bt, 128), lambda i: (i, 0)), pl.BlockSpec((8, 3 * D), lambda i: (0, 0)),
                   pl.BlockSpec((8, 128), lambda i: (0, 0))])
    row = pl.BlockSpec((bt, D), lambda i: (i, 0))
    return pl.pallas_call(
        body, name="pre_qkv_fwd", grid=(n,), in_specs=in_specs,
        out_specs=[row, row, row, pl.BlockSpec((bt, 128), lambda i: (i, 0))],
        out_shape=[_sds((T, D)), _sds((T, D)), _sds((T, D)), _sds((T, 128))],
        compiler_params=_cp(("parallel",)))(*([pm] * 9), pab, cw, gp)


def pre_qkv_bwd(pm, pab, cw, gp, dq2, dk2, dv2, dgb2, dpm, *, bt=128):
    T = pm.shape[0]
    bt = min(bt, T)
    n = T // bt
    E = bt + 2 * HALO

    def body(*refs):
        it = iter(refs)
        xs = [[next(it) for _ in range(3)] for _ in range(3)]
        ds = [[next(it) for _ in range(3)] for _ in range(3)]
        ab_ref, dgb_ref, cw_ref, gp_ref, _alias = next(it), next(it), next(it), next(it), next(it)
        o_ref, dab_ref, dcw_ref, dgp_ref = (next(it) for _ in range(4))
        i = pl.program_id(0)
        first, last = i == 0, i == n - 1

        @pl.when(first)
        def _():
            dcw_ref[...] = jnp.zeros_like(dcw_ref)
            dgp_ref[...] = jnp.zeros_like(dgp_ref)

        xes = [_ext(*xs[c], first, last) for c in range(3)]
        ces = [_conv3(xes[c], cw_ref[:, c * D:(c + 1) * D]) for c in range(3)]
        cts = []
        for c in range(3):
            m, p, x = ds[c]
            pe = jnp.where(first, 0.0, p[0] + p[1])
            ne = jnp.where(last, 0.0, x[0] + x[1])
            cts.append(jnp.concatenate([pe, m[0] + m[1], ne], axis=0))
        _, vjp = jax.vjp(_qkv_rows, *ces)
        dces = vjp(tuple(cts))
        rowi = lax.broadcasted_iota(jnp.int32, (E, 1), 0)
        central = (rowi >= HALO) & (rowi < HALO + bt)
        row8 = lax.broadcasted_iota(jnp.int32, (8, D), 0)
        for c in range(3):
            w = cw_ref[:, c * D:(c + 1) * D]
            o_ref[:, c * D:(c + 1) * D] = _conv3_t(dces[c], w)[HALO:HALO + bt].astype(o_ref.dtype)
            dc = jnp.where(central, dces[c], 0.0)
            taps = (jnp.sum(dc * _shift_dn(xes[c]), axis=0, keepdims=True),
                    jnp.sum(dc * xes[c], axis=0, keepdims=True),
                    jnp.sum(dc * _shift_up(xes[c]), axis=0, keepdims=True))
            upd = jnp.where(row8 == 0, taps[0], jnp.where(row8 == 1, taps[1], jnp.where(row8 == 2, taps[2], 0.0)))
            dcw_ref[:, c * D:(c + 1) * D] += upd
        lower, upper = _chunk_masks(bt)
        _, gvjp = jax.vjp(lambda ab, gp: _gate_rows(ab, gp, lower, upper), ab_ref[...], gp_ref[...])
        dab, dgp = gvjp(dgb_ref[0] + dgb_ref[1])
        dab_ref[...] = dab
        dgp_ref[...] += dgp

    in_specs = (_halo_specs(bt, T, CQ) + _halo_specs(bt, T, CK) + _halo_specs(bt, T, CV)
                + _halo_specs(bt, T, 0, lead=2) * 3
                + [pl.BlockSpec((bt, 128), lambda i: (i, 0)), pl.BlockSpec((2, bt, 128), lambda i: (0, i, 0)),
                   pl.BlockSpec((8, 3 * D), lambda i: (0, 0)), pl.BlockSpec((8, 128), lambda i: (0, 0)),
                   pl.BlockSpec(memory_space=pl.ANY)])
    out_specs = [pl.BlockSpec((bt, 3 * D), lambda i: (i, 0)), pl.BlockSpec((bt, 128), lambda i: (i, 0)),
                 pl.BlockSpec((8, 3 * D), lambda i: (0, 0)), pl.BlockSpec((8, 128), lambda i: (0, 0))]
    return pl.pallas_call(
        body, name="pre_qkv_bwd", grid=(n,), in_specs=in_specs, out_specs=out_specs,
        out_shape=[_sds(dpm.shape, dpm.dtype), _sds((T, 128)), _sds((8, 3 * D)), _sds((8, 128))],
        input_output_aliases={len(in_specs) - 1: 0},
        compiler_params=_cp(("arbitrary",)))(
            *([pm] * 9), dq2, dq2, dq2, dk2, dk2, dk2, dv2, dv2, dv2, pab, dgb2, cw, gp, dpm)


def pre_sc_fwd(pm, cw, *, bt=256):
    T = pm.shape[0]
    bt = min(bt, T)
    n = T // bt

    def body(b_ref, c0, c1, c2, x0, x1, x2, cw_ref, o_ref):
        i = pl.program_id(0)
        first, last = i == 0, i == n - 1
        pe = _ext(c0, c1, c2, first, last) * _ext(x0, x1, x2, first, last)
        o_ref[...] = (b_ref[...] * _conv3(pe, cw_ref[...])[HALO:HALO + bt]).astype(o_ref.dtype)

    in_specs = ([pl.BlockSpec((bt, D), lambda i: (i, CSB))] + _halo_specs(bt, T, CSC) + _halo_specs(bt, T, CSX)
                + [pl.BlockSpec((8, D), lambda i: (0, 0))])
    return pl.pallas_call(
        body, name="pre_sc_fwd", grid=(n,), in_specs=in_specs, out_specs=pl.BlockSpec((bt, D), lambda i: (i, 0)),
        out_shape=_sds((T, D), MM), compiler_params=_cp(("parallel",)))(*([pm] * 7), cw)


def pre_sc_bwd(pm, cw, dsc, dpm, *, bt=256):
    T = pm.shape[0]
    bt = min(bt, T)
    n = T // bt
    E = bt + 2 * HALO

    def body(b0, b1, b2, c0, c1, c2, x0, x1, x2, d0, d1, d2, cw_ref, _alias, o_ref, dcw_ref):
        i = pl.program_id(0)
        first, last = i == 0, i == n - 1

        @pl.when(first)
        def _():
            dcw_ref[...] = jnp.zeros_like(dcw_ref)

        ce, xe = _ext(c0, c1, c2, first, last), _ext(x0, x1, x2, first, last)
        pe = ce * xe
        w = cw_ref[...]
        dout = d0[...]
        o_ref[:, 0:D] = (dout * _conv3(pe, w)[HALO:HALO + bt]).astype(o_ref.dtype)
        dce = _ext(d0, d1, d2, first, last) * _ext(b0, b1, b2, first, last)
        dp = _conv3_t(dce, w)[HALO:HALO + bt]
        o_ref[:, D:2 * D] = (dp * x0[...]).astype(o_ref.dtype)
        o_ref[:, 2 * D:3 * D] = (dp * c0[...]).astype(o_ref.dtype)
        rowi = lax.broadcasted_iota(jnp.int32, (E, 1), 0)
        dc = jnp.where((rowi >= HALO) & (rowi < HALO + bt), dce, 0.0)
        row8 = lax.broadcasted_iota(jnp.int32, (8, D), 0)
        taps = (jnp.sum(dc * _shift_dn(pe), axis=0, keepdims=True), jnp.sum(dc * pe, axis=0, keepdims=True),
                jnp.sum(dc * _shift_up(pe), axis=0, keepdims=True))
        dcw_ref[...] += jnp.where(row8 == 0, taps[0], jnp.where(row8 == 1, taps[1], jnp.where(row8 == 2, taps[2], 0.0)))

    dsc_specs = [pl.BlockSpec((bt, D), lambda i: (i, 0)),
                 pl.BlockSpec((HALO, D), lambda i: (jnp.maximum(i * (bt // HALO) - 1, 0), 0)),
                 pl.BlockSpec((HALO, D), lambda i: (jnp.minimum((i + 1) * (bt // HALO), T // HALO - 1), 0))]
    in_specs = (_halo_specs(bt, T, CSB) + _halo_specs(bt, T, CSC) + _halo_specs(bt, T, CSX) + dsc_specs
                + [pl.BlockSpec((8, D), lambda i: (0, 0)), pl.BlockSpec(memory_space=pl.ANY)])
    return pl.pallas_call(
        body, name="pre_sc_bwd", grid=(n,), in_specs=in_specs,
        out_specs=[pl.BlockSpec((bt, 3 * D), lambda i: (i, 1)), pl.BlockSpec((8, D), lambda i: (0, 0))],
        out_shape=[_sds(dpm.shape, dpm.dtype), _sds((8, D))], input_output_aliases={len(in_specs) - 1: 0},
        compiler_params=_cp(("arbitrary",)))(*([pm] * 9), dsc, dsc, dsc, cw, dpm)


def _bdot(a, b, dims):
    return lax.dot_general(a.astype(MM), b.astype(MM), (dims, ((), ())), preferred_element_type=F32)


_NN, _NT, _TN = ((1,), (0,)), ((1,), (1,)), ((0,), (0,))


def _raw_nn(a, b):
    return _bdot(a, b, _NN)


def _raw_nt(a, b):
    return _bdot(a, b, _NT)


def _raw_tn(a, b):
    return _bdot(a, b, _TN)


def _make_vjp_ops():
    @jax.custom_vjp
    def nn(a, b):
        return _raw_nn(a, b)

    @jax.custom_vjp
    def nt(a, b):
        return _raw_nt(a, b)

    @jax.custom_vjp
    def tn(a, b):
        return _raw_tn(a, b)

    nn.defvjp(lambda a, b: (_raw_nn(a, b), (a, b)), lambda r, g: (_raw_nt(g, r[1]), _raw_tn(r[0], g)))
    nt.defvjp(lambda a, b: (_raw_nt(a, b), (a, b)), lambda r, g: (_raw_nn(g, r[1]), _raw_tn(g, r[0])))
    tn.defvjp(lambda a, b: (_raw_tn(a, b), (a, b)), lambda r, g: (_raw_nt(r[1], g), _raw_nn(r[0], g)))

    @jax.custom_vjp
    def inv_saved(A, Y):
        return Y

    def inv_bwd(Y, g):
        M = g + _raw_tn(Y, g)
        return -(M + _raw_nt(M, Y)), jnp.zeros_like(Y)

    inv_saved.defvjp(lambda A, Y: (Y, Y), inv_bwd)
    return nn, nt, tn, inv_saved


GH = 2
GR = GH * CH
NG = NH // GH
CBI = 2
CBB = 2


def _scan_chunks(n_chunks):
    return 4 if n_chunks % 4 == 0 else 2


def _tri_inv_y_all(As):
    Ys = [-A for A in As]
    Ps = [_raw_nn(A, A) for A in As]
    for stage in range(5):
        squares = [_raw_nn(P, P) for P in Ps] if stage < 4 else None
        Ys = [Y + P + _raw_nn(Y, P) for Y, P in zip(Ys, Ps)]
        Ps = squares
    return Ys


def _intra_groups(chains, incl, strict, eye, lastc, ops, inv_all):
    nn, nt, tn = ops
    st = []
    for qs, ks, vs, gcols, bcols in chains:
        q, k, v = (jnp.concatenate(t, axis=0) for t in (qs, ks, vs))
        gcol, bcol = jnp.concatenate(gcols, axis=0), jnp.concatenate(bcols, axis=0)
        grow = jnp.sum(eye * gcol, axis=0, keepdims=True)
        gam = jnp.where(incl, jnp.exp(jnp.where(incl, gcol - grow, 0.0)), 0.0)
        glast = jnp.sum(jnp.where(lastc, grow, 0.0), axis=1, keepdims=True)
        st.append((q, k, v, gcol, bcol, gam, glast, jnp.exp(gcol), k * bcol))
    As = [jnp.where(strict, nt(kb, k) * gam, 0.0) for (q, k, v, gcol, bcol, gam, glast, eg, kb) in st]
    Ys = inv_all(As)
    vbs = [v * bcol for (q, k, v, gcol, bcol, gam, glast, eg, kb) in st]
    kgs = [kb * eg for (q, k, v, gcol, bcol, gam, glast, eg, kb) in st]
    us = [vb + nn(Y, vb) for Y, vb in zip(Ys, vbs)]
    ws = [kg + nn(Y, kg) for Y, kg in zip(Ys, kgs)]
    Ps = [nt(q, k) * gam for (q, k, v, gcol, bcol, gam, glast, eg, kb) in st]
    return [((u, w, P, q * eg, k * jnp.exp(glast - gcol), jnp.exp(glast)), Y)
            for u, w, P, Y, (q, k, v, gcol, bcol, gam, glast, eg, kb) in zip(us, ws, Ps, Ys, st)]


def _scan_groups(chains, ops):
    nn, nt, tn, _ = ops
    vns = [[us[j] - nn(ws[j], Ss[j]) for j in range(GH)] for Ss, us, ws, P, qds, kds, egls in chains]
    os_ = [jnp.concatenate([nn(qds[j], Ss[j]) for j in range(GH)], axis=0) + nn(P, jnp.concatenate(vn, axis=0))
           for (Ss, us, ws, P, qds, kds, egls), vn in zip(chains, vns)]
    S2s = [[Ss[j] * egls[j] + tn(kds[j], vn[j]) for j in range(GH)]
           for (Ss, us, ws, P, qds, kds, egls), vn in zip(chains, vns)]
    return list(zip(os_, S2s))


def _group_masks(rev, rows=GR):
    r = lax.broadcasted_iota(jnp.int32, (rows, rows), 0)
    c = lax.broadcasted_iota(jnp.int32, (rows, rows), 1)
    same = (r // CH) == (c // CH)
    ahead = jnp.where(rev, c - r, r - c)
    incl = same & (ahead >= 0)
    strict = same & (ahead > 0)
    eye = jnp.where(r == c, 1.0, 0.0).astype(F32)
    lastc = same & ((c % CH) == jnp.where(rev, 0, CH - 1))
    return incl, strict, eye, lastc


def _head_gates(gb, h, rev):
    gcol = jnp.where(rev, gb[:, NH + h:NH + h + 1], gb[:, h:h + 1])
    bcol = jnp.where(rev, gb[:, 3 * NH + h:3 * NH + h + 1], gb[:, 2 * NH + h:2 * NH + h + 1])
    return gcol, bcol


def _hs(h):
    return slice(h * HD, (h + 1) * HD)


def _blockdiag(a, b):
    z = jnp.zeros_like(a)
    return jnp.concatenate([jnp.concatenate([a, z], axis=1), jnp.concatenate([z, b], axis=1)], axis=0)


def _load_chain(q_ref, k_ref, v_ref, gb_ref, c, g, rev, gh=GH):
    r = slice(c * CH, (c + 1) * CH)
    heads = range(g * gh, (g + 1) * gh)
    gates = [_head_gates(gb_ref[r, :], h, rev) for h in heads]
    return ([q_ref[r, _hs(h)] for h in heads], [k_ref[r, _hs(h)] for h in heads], [v_ref[r, _hs(h)] for h in heads],
            [t[0] for t in gates], [t[1] for t in gates])


def gdn_intra_fwd(q, k, v, gb):
    T = q.shape[0]
    N = T // CH
    ops = (_raw_nn, _raw_nt, _raw_tn)

    def body(q_ref, k_ref, v_ref, gb_ref, u_ref, w_ref, qd_ref, kd_ref, pp_ref, ys_ref, eg_ref):
        rev = pl.program_id(0) == 1
        masks = _group_masks(rev)
        where = [(c, g) for c in range(CBI) for g in range(NG)]
        chains = [_load_chain(q_ref, k_ref, v_ref, gb_ref, c, g, rev) for c, g in where]
        for (c, g), ((u, w, P, qd, kd, egl), Y) in zip(where, _intra_groups(chains, *masks, ops, _tri_inv_y_all)):
            r = slice(c * CH, (c + 1) * CH)
            pp_ref[c, g] = P.astype(MM)
            ys_ref[c, g] = Y.astype(MM)
            for j, h in enumerate(range(g * GH, (g + 1) * GH)):
                rows = slice(j * CH, (j + 1) * CH)
                u_ref[r, _hs(h)] = u[rows]
                w_ref[r, _hs(h)] = w[rows].astype(MM)
                qd_ref[r, _hs(h)] = qd[rows].astype(MM)
                kd_ref[r, _hs(h)] = kd[rows].astype(MM)
                eg_ref[c, h:h + 1, :] = jnp.broadcast_to(egl[j * CH:j * CH + 1, :], (1, 128))

    row = pl.BlockSpec((CBI * CH, D), lambda d, n: (n, 0))
    drow = pl.BlockSpec((None, CBI * CH, D), lambda d, n: (d, n, 0))
    mat = pl.BlockSpec((None, CBI, NG, GR, GR), lambda d, n: (d, n, 0, 0, 0))
    return pl.pallas_call(
        body, name="gdn_intra_fwd", grid=(2, N // CBI),
        in_specs=[row, row, row, pl.BlockSpec((CBI * CH, 128), lambda d, n: (n, 0))],
        out_specs=[drow] * 4 + [mat, mat, pl.BlockSpec((None, CBI, NH, 128), lambda d, n: (d, n, 0, 0))],
        out_shape=[_sds((2, T, D))] + [_sds((2, T, D), MM)] * 3 + [_sds((2, N, NG, GR, GR), MM)] * 2
                  + [_sds((2, N, NH, 128))],
        compiler_params=_cp(("parallel", "parallel")))(q, k, v, gb)


def gdn_scan_fwd(u, w, qd, kd, pp, eg):
    T = u.shape[1]
    N = T // CH
    ops = (_raw_nn, _raw_nt, _raw_tn, None)

    cbs = _scan_chunks(N)
    NB = N // cbs

    def body(u_ref, w_ref, qd_ref, kd_ref, pp_ref, eg_ref, o_ref, s0_ref, S):
        d = pl.program_id(0)

        @pl.when(pl.program_id(1) == 0)
        def _():
            S[...] = jnp.zeros_like(S)

        def chunk(c, carry):
            pc = c + d * (cbs - 1 - 2 * c)
            r = pl.ds(pl.multiple_of(pc * CH, CH), CH)
            chains = []
            for g in range(NG):
                heads = range(g * GH, (g + 1) * GH)
                Ss = [S[h] for h in heads]
                for h, Sh in zip(heads, Ss):
                    s0_ref[pc, h] = Sh
                chains.append((Ss, [u_ref[r, _hs(h)] for h in heads], [w_ref[r, _hs(h)] for h in heads], pp_ref[pc, g],
                               [qd_ref[r, _hs(h)] for h in heads], [kd_ref[r, _hs(h)] for h in heads],
                               [eg_ref[pc, h:h + 1, :] for h in heads]))
            for g, (o, S2) in enumerate(_scan_groups(chains, ops)):
                for j, h in enumerate(range(g * GH, (g + 1) * GH)):
                    o_ref[r, _hs(h)] = o[j * CH:(j + 1) * CH]
                    S[h] = S2[j]
            return carry

        lax.fori_loop(0, cbs, chunk, 0)

    bidx = lambda d, n: n + d * (NB - 1 - 2 * n)
    drow = pl.BlockSpec((None, cbs * CH, D), lambda d, n: (d, bidx(d, n), 0))
    mat = pl.BlockSpec((None, cbs, NG, GR, GR), lambda d, n: (d, bidx(d, n), 0, 0, 0))
    return pl.pallas_call(
        body, name="gdn_scan_fwd", grid=(2, NB),
        in_specs=[drow] * 4 + [mat, pl.BlockSpec((None, cbs, NH, 128), lambda d, n: (d, bidx(d, n), 0, 0))],
        out_specs=[drow, pl.BlockSpec((None, cbs, NH, HD, HD), lambda d, n: (d, bidx(d, n), 0, 0, 0))],
        out_shape=[_sds((2, T, D)), _sds((2, N, NH, HD, HD))],
        scratch_shapes=[pltpu.VMEM((NH, HD, HD), F32)],
        compiler_params=_cp(("arbitrary", "arbitrary")))(u, w, qd, kd, pp, eg)


def gdn_scan_bwd(u, w, qd, kd, pp, eg, s0, do):
    T = u.shape[1]
    N = T // CH
    ops = _make_vjp_ops()
    cbs = _scan_chunks(N)
    NB = N // cbs

    def body(u_ref, w_ref, qd_ref, kd_ref, pp_ref, eg_ref, s0_ref, do_ref,
             du_ref, dw_ref, dqd_ref, dkd_ref, dpp_ref, deg_ref, dS):
        d = pl.program_id(0)

        @pl.when(pl.program_id(1) == 0)
        def _():
            dS[...] = jnp.zeros_like(dS)

        def chunk(c, carry):
            pc = (cbs - 1 - c) + d * (2 * c - (cbs - 1))
            r = pl.ds(pl.multiple_of(pc * CH, CH), CH)
            chains, cts = [], []
            for g in range(NG):
                heads = range(g * GH, (g + 1) * GH)
                chains.append(([s0_ref[pc, h] for h in heads], [u_ref[r, _hs(h)] for h in heads],
                               [w_ref[r, _hs(h)].astype(F32) for h in heads], pp_ref[pc, g].astype(F32),
                               [qd_ref[r, _hs(h)].astype(F32) for h in heads],
                               [kd_ref[r, _hs(h)].astype(F32) for h in heads], [eg_ref[pc, h:h + 1, :] for h in heads]))
                cts.append((jnp.concatenate([do_ref[r, _hs(h)].astype(F32) for h in heads], axis=0),
                            [dS[h] for h in heads]))
            _, vjp = jax.vjp(lambda ch: _scan_groups(ch, ops), chains)
            (dchains,) = vjp(cts)
            for g, (dSs, dus, dws, dP, dqds, dkds, degs) in enumerate(dchains):
                dpp_ref[pc, g] = dP
                for j, h in enumerate(range(g * GH, (g + 1) * GH)):
                    dS[h] = dSs[j]
                    du_ref[r, _hs(h)] = dus[j].astype(MM)
                    dw_ref[r, _hs(h)] = dws[j].astype(MM)
                    dqd_ref[r, _hs(h)], dkd_ref[r, _hs(h)] = dqds[j], dkds[j]
                    deg_ref[pc, h:h + 1, :] = degs[j]
            return carry

        lax.fori_loop(0, cbs, chunk, 0)

    bidx = lambda d, n: (NB - 1 - n) + d * (2 * n - (NB - 1))
    drow = pl.BlockSpec((None, cbs * CH, D), lambda d, n: (d, bidx(d, n), 0))
    erow = pl.BlockSpec((None, cbs, NH, 128), lambda d, n: (d, bidx(d, n), 0, 0))
    mat = pl.BlockSpec((None, cbs, NG, GR, GR), lambda d, n: (d, bidx(d, n), 0, 0, 0))
    return pl.pallas_call(
        body, name="gdn_scan_bwd", grid=(2, NB),
        in_specs=[drow] * 4 + [mat, erow, pl.BlockSpec((None, cbs, NH, HD, HD), lambda d, n: (d, bidx(d, n), 0, 0, 0)),
                               pl.BlockSpec((cbs * CH, D), lambda d, n: (bidx(d, n), 0))],
        out_specs=[drow] * 4 + [mat, erow],
        out_shape=[_sds((2, T, D), MM)] * 2 + [_sds((2, T, D))] * 2 + [_sds((2, N, NG, GR, GR))]
                  + [_sds((2, N, NH, 128))],
        scratch_shapes=[pltpu.VMEM((NH, HD, HD), F32)],
        compiler_params=_cp(("arbitrary", "arbitrary")))(u, w, qd, kd, pp, eg, s0, do)


def gdn_intra_bwd(q, k, v, gb, ys, du, dw, dqd, dkd, dpp, deg):
    T = q.shape[0]
    N = T // CH
    nn, nt, tn, inv_saved = _make_vjp_ops()

    def body(q_ref, k_ref, v_ref, gb_ref, ys_ref, du_ref, dw_ref, dqd_ref, dkd_ref, dpp_ref, deg_ref,
             dq_ref, dk_ref, dv_ref, dgb_ref):
        rev = pl.program_id(0) == 1
        gh, rows = 2 * GH, 2 * GR
        masks = _group_masks(rev, rows)
        lane = lax.broadcasted_iota(jnp.int32, (CH, 128), 1)
        grow = lax.broadcasted_iota(jnp.int32, (rows, 1), 0)
        where = [(c, g) for c in range(CBB) for g in range(NG // 2)]
        pair = lambda ref, c, g: _blockdiag(ref[c, 2 * g].astype(F32), ref[c, 2 * g + 1].astype(F32))
        chains = [_load_chain(q_ref, k_ref, v_ref, gb_ref, c, g, rev, gh) for c, g in where]
        Ys = [pair(ys_ref, c, g) for c, g in where]
        inv_all = lambda As: [inv_saved(A, Y) for A, Y in zip(As, Ys)]
        _, vjp = jax.vjp(lambda ch: _intra_groups(ch, *masks, (nn, nt, tn), inv_all), chains)
        cts = []
        for c, g in where:
            r = slice(c * CH, (c + 1) * CH)
            heads = range(g * gh, (g + 1) * gh)
            stack = lambda ref: jnp.concatenate([ref[r, _hs(h)].astype(F32) for h in heads], axis=0)
            degl = jnp.zeros((rows, 1), F32)
            for j, h in enumerate(heads):
                degl = degl + jnp.where(grow == j * CH, jnp.sum(deg_ref[c, h:h + 1, :], axis=1, keepdims=True), 0.0)
            cts.append(((stack(du_ref), stack(dw_ref), pair(dpp_ref, c, g), stack(dqd_ref), stack(dkd_ref), degl),
                        jnp.zeros((rows, rows), F32)))
        (dchains,) = vjp(cts)
        dgbs = [jnp.zeros((CH, 128), F32) for _ in range(CBB)]
        for (c, g), (dqs, dks, dvs, dgs, dbs) in zip(where, dchains):
            r = slice(c * CH, (c + 1) * CH)
            for j, h in enumerate(range(g * gh, (g + 1) * gh)):
                dq_ref[r, _hs(h)], dk_ref[r, _hs(h)], dv_ref[r, _hs(h)] = dqs[j], dks[j], dvs[j]
                glane = jnp.where(rev, NH + h, h)
                dgbs[c] = dgbs[c] + jnp.where(lane == glane, dgs[j], 0.0) + jnp.where(lane == glane + 2 * NH, dbs[j], 0.0)
        for c in range(CBB):
            dgb_ref[c * CH:(c + 1) * CH, :] = dgbs[c]

    row = pl.BlockSpec((CBB * CH, D), lambda d, n: (n, 0))
    drow = pl.BlockSpec((None, CBB * CH, D), lambda d, n: (d, n, 0))
    mat = pl.BlockSpec((None, CBB, NG, GR, GR), lambda d, n: (d, n, 0, 0, 0))
    return pl.pallas_call(
        body, name="gdn_intra_bwd", grid=(2, N // CBB),
        in_specs=[row, row, row, pl.BlockSpec((CBB * CH, 128), lambda d, n: (n, 0)), mat, drow, drow, drow, drow, mat]
                 + [pl.BlockSpec((None, CBB, NH, 128), lambda d, n: (d, n, 0, 0))],
        out_specs=[drow, drow, drow, pl.BlockSpec((None, CBB * CH, 128), lambda d, n: (d, n, 0))],
        out_shape=[_sds((2, T, D))] * 3 + [_sds((2, T, 128))],
        compiler_params=_cp(("parallel", "parallel")))(q, k, v, gb, ys, du, dw, dqd, dkd, dpp, deg)


def _post_rows(o2a, o2b, z, nw):
    o = o2a + o2b
    outs = []
    for h in range(NH):
        s = slice(h * HD, (h + 1) * HD)
        oh = o[:, s]
        outs.append(oh * lax.rsqrt(jnp.mean(oh * oh, axis=-1, keepdims=True) + RMS_EPS) * nw * _silu(z[:, s]))
    return jnp.concatenate(outs, axis=1)


def post_fwd(o2, pm, nw, *, bt=512):
    T = pm.shape[0]
    bt = min(bt, T)

    def body(o_ref, z_ref, nw_ref, og_ref):
        og_ref[...] = _post_rows(o_ref[0], o_ref[1], z_ref[...], nw_ref[0:1, :]).astype(og_ref.dtype)

    return pl.pallas_call(
        body, name="post_fwd", grid=(T // bt,),
        in_specs=[pl.BlockSpec((2, bt, D), lambda i: (0, i, 0)), pl.BlockSpec((bt, D), lambda i: (i, CZ)),
                  pl.BlockSpec((8, 128), lambda i: (0, 0))],
        out_specs=pl.BlockSpec((bt, D), lambda i: (i, 0)), out_shape=_sds((T, D), MM),
        compiler_params=_cp(("parallel",)))(o2, pm, nw)


def post_bwd(o2, pm, nw, dog, dpm, *, bt=512):
    T = pm.shape[0]
    bt = min(bt, T)

    def body(o_ref, z_ref, nw_ref, dog_ref, _alias, do_ref, dz_ref, dnw_ref):
        @pl.when(pl.program_id(0) == 0)
        def _():
            dnw_ref[...] = jnp.zeros_like(dnw_ref)

        _, vjp = jax.vjp(_post_rows, o_ref[0], o_ref[1], z_ref[...], nw_ref[0:1, :])
        doa, _unused, dz, dnw = vjp(dog_ref[...])
        do_ref[...] = doa.astype(do_ref.dtype)
        dz_ref[...] = dz.astype(dz_ref.dtype)
        row8 = lax.broadcasted_iota(jnp.int32, (8, 128), 0)
        dnw_ref[...] += jnp.where(row8 == 0, dnw, 0.0)

    in_specs = [pl.BlockSpec((2, bt, D), lambda i: (0, i, 0)), pl.BlockSpec((bt, D), lambda i: (i, CZ)),
                pl.BlockSpec((8, 128), lambda i: (0, 0)), pl.BlockSpec((bt, D), lambda i: (i, 0)),
                pl.BlockSpec(memory_space=pl.ANY)]
    return pl.pallas_call(
        body, name="post_bwd", grid=(T // bt,), in_specs=in_specs,
        out_specs=[pl.BlockSpec((bt, D), lambda i: (i, 0)), pl.BlockSpec((bt, D), lambda i: (i, CZ)),
                   pl.BlockSpec((8, 128), lambda i: (0, 0))],
        out_shape=[_sds((T, D), MM), _sds(dpm.shape, dpm.dtype), _sds((8, 128))], input_output_aliases={4: 1},
        compiler_params=_cp(("arbitrary",)))(o2, pm, nw, dog, dpm)


def merge_fwd(ya, yb, pm, *, bt=512):
    T = pm.shape[0]
    bt = min(bt, T)

    def body(ya_ref, yb_ref, ga_ref, gb_ref, o_ref):
        o_ref[...] = (_sigmoid(ga_ref[...]) * ya_ref[...] + _sigmoid(gb_ref[...]) * yb_ref[...]).astype(o_ref.dtype)

    row = pl.BlockSpec((bt, D), lambda i: (i, 0))
    return pl.pallas_call(
        body, name="merge_fwd", grid=(T // bt,),
        in_specs=[row, row, pl.BlockSpec((bt, D), lambda i: (i, CGA)), pl.BlockSpec((bt, D), lambda i: (i, CGB))],
        out_specs=row, out_shape=_sds((T, D), MM), compiler_params=_cp(("parallel",)))(ya, yb, pm, pm)


def merge_bwd(ya, yb, pm, dmix, *, bt=512):
    T = pm.shape[0]
    bt = min(bt, T)

    def body(ya_ref, yb_ref, ga_ref, gb_ref, dm_ref, dya_ref, dyb_ref, dg_ref):
        dm = dm_ref[...]
        sa, sb = _sigmoid(ga_ref[...]), _sigmoid(gb_ref[...])
        dya_ref[...] = (dm * sa).astype(dya_ref.dtype)
        dyb_ref[...] = (dm * sb).astype(dyb_ref.dtype)
        dg_ref[:, 0:D] = (dm * ya_ref[...] * sa * (1.0 - sa)).astype(dg_ref.dtype)
        dg_ref[:, D:2 * D] = (dm * yb_ref[...] * sb * (1.0 - sb)).astype(dg_ref.dtype)

    row = pl.BlockSpec((bt, D), lambda i: (i, 0))
    return pl.pallas_call(
        body, name="merge_bwd", grid=(T // bt,),
        in_specs=[row, row, pl.BlockSpec((bt, D), lambda i: (i, CGA)), pl.BlockSpec((bt, D), lambda i: (i, CGB)), row],
        out_specs=[row, row, pl.BlockSpec((bt, 2 * D), lambda i: (i, CGA // 2))],
        out_shape=[_sds((T, D), MM), _sds((T, D), MM), _sds((T, NMAIN), MM)],
        compiler_params=_cp(("parallel",)))(ya, yb, pm, pm, dmix)


def _ln_rows(x, y, bias, g, b):
    r = ALPHA * x + y + bias
    mu = jnp.mean(r, axis=-1, keepdims=True)
    var = jnp.mean(jnp.square(r - mu), axis=-1, keepdims=True)
    return (r - mu) * lax.rsqrt(var + LN_EPS) * g + b


def ln_fwd(x, y, p, *, name, bt=512):
    T = x.shape[0]
    bt = min(bt, T)

    def body(x_ref, y_ref, p_ref, o_ref, ob_ref):
        r = _ln_rows(x_ref[...], y_ref[...], p_ref[0:1, :], p_ref[1:2, :], p_ref[2:3, :])
        o_ref[...] = r
        ob_ref[...] = r.astype(ob_ref.dtype)

    row = pl.BlockSpec((bt, D), lambda i: (i, 0))
    return pl.pallas_call(
        body, name=name, grid=(T // bt,), in_specs=[row, row, pl.BlockSpec((8, D), lambda i: (0, 0))],
        out_specs=[row, row], out_shape=[_sds((T, D)), _sds((T, D), MM)],
        compiler_params=_cp(("parallel",)))(x, y, p)


def ln_bwd(x, y, p, ct, ct2=None, *, name, bt=512):
    T = x.shape[0]
    bt = min(bt, T)

    def body(*refs):
        it = iter(refs)
        x_ref, y_ref, p_ref, c_ref = next(it), next(it), next(it), next(it)
        c2_ref = next(it) if ct2 is not None else None
        dxa_ref, dr_ref, dp_ref = next(it), next(it), next(it)

        @pl.when(pl.program_id(0) == 0)
        def _():
            dp_ref[...] = jnp.zeros_like(dp_ref)

        c = c_ref[...]
        if c2_ref is not None:
            c = c + c2_ref[...]
        _, vjp = jax.vjp(_ln_rows, x_ref[...], y_ref[...], p_ref[0:1, :], p_ref[1:2, :], p_ref[2:3, :])
        _dx, dy, dbias, dg, db = vjp(c)
        dxa_ref[...] = ALPHA * dy
        dr_ref[...] = dy.astype(dr_ref.dtype)
        row8 = lax.broadcasted_iota(jnp.int32, (8, D), 0)
        dp_ref[...] += jnp.where(row8 == 0, dbias, jnp.where(row8 == 1, dg, jnp.where(row8 == 2, db, 0.0)))

    row = pl.BlockSpec((bt, D), lambda i: (i, 0))
    in_specs = [row, row, pl.BlockSpec((8, D), lambda i: (0, 0)), row] + ([row] if ct2 is not None else [])
    args = [x, y, p, ct] + ([ct2] if ct2 is not None else [])
    return pl.pallas_call(
        body, name=name, grid=(T // bt,), in_specs=in_specs,
        out_specs=[row, row, pl.BlockSpec((8, D), lambda i: (0, 0))],
        out_shape=[_sds((T, D)), _sds((T, D), MM), _sds((8, D))],
        compiler_params=_cp(("arbitrary",)))(*args)


def loss_fwd_bwd(xl, target, *, bt=512):
    T = xl.shape[0]
    bt = min(bt, T)

    def body(x_ref, t_ref, l_ref, d_ref):
        @pl.when(pl.program_id(0) == 0)
        def _():
            l_ref[...] = jnp.zeros_like(l_ref)

        e = x_ref[...] - t_ref[...]
        d_ref[...] = e * (1.0 / D)
        l_ref[...] += 0.5 * jnp.sum(jnp.mean(e * e, axis=-1, keepdims=True), axis=0, keepdims=True)

    row = pl.BlockSpec((bt, D), lambda i: (i, 0))
    return pl.pallas_call(
        body, name="loss", grid=(T // bt,), in_specs=[row, row],
        out_specs=[pl.BlockSpec((8, 128), lambda i: (0, 0)), row], out_shape=[_sds((8, 128)), _sds((T, D))],
        compiler_params=_cp(("arbitrary",)))(xl, target)


def _row_tile(R, Cc, elems=1 << 18):
    if R * Cc <= elems:
        return R
    tr = 8
    while tr * 2 * Cc <= elems and R % (tr * 2) == 0:
        tr *= 2
    return tr


def adam(w, m, v, ga, gb=None, *, name):
    R, Cc = w.shape
    tr = _row_tile(R, Cc)

    def body(*refs):
        it = iter(refs)
        w_ref, m_ref, v_ref, a_ref = next(it), next(it), next(it), next(it)
        b_ref = next(it) if gb is not None else None
        g_ref, d_ref, mo_ref, vo_ref = next(it), next(it), next(it), next(it)
        g = a_ref[...]
        if b_ref is not None:
            g = g + b_ref[...]
        m2 = B1 * m_ref[...] + (1.0 - B1) * g
        v2 = B2 * v_ref[...] + (1.0 - B2) * jnp.square(g)
        m_hat = m2 / (1.0 - B1 ** STEP)
        v_hat = v2 / (1.0 - B2 ** STEP)
        g_ref[...] = g
        d_ref[...] = -LR * (m_hat / (jnp.sqrt(v_hat) + EPS) + WD * w_ref[...])
        mo_ref[...] = m2
        vo_ref[...] = v2

    blk = pl.BlockSpec((tr, Cc), lambda i: (i, 0))
    args = [w, m, v, ga] + ([gb] if gb is not None else [])
    return pl.pallas_call(
        body, name=name, grid=(R // tr,), in_specs=[blk] * len(args), out_specs=[blk] * 4,
        out_shape=[_sds((R, Cc))] * 4, compiler_params=_cp(("parallel",)))(*args)


def sum4(parts, *, name):
    _, R, Cc = parts.shape
    tr = _row_tile(R, Cc)

    def body(p_ref, out_ref):
        f = lambda t: t.astype(F32)
        out_ref[...] = ((f(p_ref[0]) + f(p_ref[1])) + f(p_ref[2])) + f(p_ref[3])

    return pl.pallas_call(
        body, name=name, grid=(R // tr,), in_specs=[pl.BlockSpec((4, tr, Cc), lambda i: (0, i, 0))],
        out_specs=pl.BlockSpec((tr, Cc), lambda i: (i, 0)), out_shape=_sds((R, Cc)),
        compiler_params=_cp(("parallel",)))(parts)


def _place():
    return lax.axis_index("x"), lax.axis_index("y"), lax.axis_index("c")


def _other_chips(x, y):
    return [(1 - x, y), (x, 1 - y), (1 - x, 1 - y)]


_ANY = pl.BlockSpec(memory_space=pl.ANY)


def allgather_xy(arrs):
    n = len(arrs)
    axes = [0 if a.shape[0] % 2 == 0 else 1 for a in arrs]
    halves = [a.shape[ax] // 2 for a, ax in zip(arrs, axes)]

    def half_of(ref, a, which):
        part = pl.ds(which * halves[a], halves[a])
        return ref.at[part] if axes[a] == 0 else ref.at[:, part]

    def body(*refs):
        ins, outs = refs[:n], refs[n:2 * n]
        send, recv, fsend, frecv, loc = refs[2 * n:]
        x, y, c = _place()
        me = 2 * x + y
        peers = _other_chips(x, y)
        local = [pltpu.make_async_copy(ins[a], outs[a].at[me], loc.at[a]) for a in range(n)]
        for cp in local:
            cp.start()

        def over_ici(a, j, block, src=None):
            px, py = peers[j]
            dst = half_of(outs[a].at[block], a, c)
            return pltpu.make_async_remote_copy(
                src_ref=dst if src is None else half_of(src, a, c), dst_ref=dst, send_sem=send.at[3 * a + j],
                recv_sem=recv.at[3 * a + j], device_id=(px, py, c), device_id_type=MESH)

        def over_d2d(a, j, half):
            px, py = peers[j]
            rows = half_of(outs[a].at[2 * px + py], a, half)
            return pltpu.make_async_remote_copy(
                src_ref=rows, dst_ref=rows, send_sem=fsend.at[3 * a + j], recv_sem=frecv.at[3 * a + j],
                device_id=(x, y, 1 - c), device_id_type=MESH)

        sends = [over_ici(a, j, me, src=ins[a]) for a in range(n) for j in range(3)]
        for cp in sends:
            cp.start()
        passed = []
        for a in range(n):
            for j, (px, py) in enumerate(peers):
                over_ici(a, j, 2 * px + py).wait_recv()
                passed.append(over_d2d(a, j, c))
                passed[-1].start()
        for a in range(n):
            for j in range(3):
                over_d2d(a, j, 1 - c).wait_recv()
        for cp in sends + passed:
            cp.wait_send()
        for cp in local:
            cp.wait()

    return pl.pallas_call(
        body, name="allgather_xy", in_specs=[_ANY] * n, out_specs=[_ANY] * n,
        out_shape=[_sds((4,) + a.shape, a.dtype) for a in arrs],
        scratch_shapes=[pltpu.SemaphoreType.DMA((3 * n,))] * 4 + [pltpu.SemaphoreType.DMA((n,))],
        compiler_params=pltpu.CompilerParams(has_side_effects=True))(*arrs)


_HBM = pl.BlockSpec(memory_space=pltpu.HBM)
_SEM = pl.BlockSpec(memory_space=pltpu.SEMAPHORE)
_EFFECT = pltpu.SideEffectType.DATAFLOW_SIDE_EFFECTING


def _xy_copy(kind, layer, src, land, send, recv, a, j, c, arriving):
    x, y = lax.axis_index("x"), lax.axis_index("y")
    px, py = ((1 - x, y), (x, 1 - y), (1 - x, 1 - y), (x, y))[j]
    if kind == "gather":
        dst = land.at[2 * px + py] if arriving else land.at[2 * x + y]
        src_view = dst if arriving else src
    else:
        dst = land.at[j, layer]
        src_view = src.at[2 * px + py]
    return pltpu.make_async_remote_copy(src_ref=src_view, dst_ref=dst, send_sem=send.at[4 * a + j],
                                        recv_sem=recv.at[4 * a + j], device_id=(px, py, c), device_id_type=MESH)


def xy_start(kind, name, arrs, after, zones=None, layer=0):
    n = len(arrs)
    if zones is None:
        shape = lambda a: ((4,) + a.shape) if kind == "gather" else ((4, DEPTH) + a.shape[1:])
        zones = [lax.empty(shape(a), a.dtype) for a in arrs]

    def body(*refs):
        ins, lands = refs[:n], refs[n:2 * n]
        send, recv = refs[2 * n + 1], refs[2 * n + 2]
        token = refs[-1]
        c = lax.axis_index("c")
        for a in range(n):
            for j in range(4):
                _xy_copy(kind, layer, ins[a], lands[a], send, recv, a, j, c, False).start()
        token[...] = jnp.zeros_like(token)

    hbm = lambda v: pltpu.with_memory_space_constraint(v, pltpu.HBM)
    outs = pl.pallas_call(
        body, name=name,
        out_shape=(pltpu.SemaphoreType.DMA((4 * n,)), pltpu.SemaphoreType.DMA((4 * n,)),
                   *[pltpu.HBM(a.shape, a.dtype) for a in arrs], *[pltpu.HBM(z.shape, z.dtype) for z in zones],
                   _sds((8, 128))),
        in_specs=[_HBM] * (2 * n) + [_ANY],
        out_specs=(_SEM, _SEM, *[_HBM] * (2 * n), pl.BlockSpec(memory_space=pltpu.VMEM)),
        input_output_aliases={i: 2 + i for i in range(2 * n)},
        compiler_params=pltpu.CompilerParams(has_side_effects=_EFFECT))(
            *[hbm(a) for a in arrs], *[hbm(z) for z in zones], after)
    return (kind, layer, outs[0], outs[1], list(outs[2:2 + n])), list(outs[2 + n:2 + 2 * n]), outs[-1]


def xy_wait(name, handles, zones, after):
    kind, layer, send_sems, recv_sems, srcs = handles
    n = len(srcs)

    def body(*refs):
        ins, lands = refs[:n], refs[n:2 * n]
        send, recv = refs[2 * n], refs[2 * n + 1]
        c = lax.axis_index("c")
        for a in range(n):
            for j in range(4):
                _xy_copy(kind, layer, ins[a], lands[a], send, recv, a, j, c, False).wait_send()
                _xy_copy(kind, layer, ins[a], lands[a], send, recv, a, j, c, True).wait_recv()

    outs = pl.pallas_call(
        body, name=name,
        out_shape=tuple(pltpu.HBM(v.shape, v.dtype) for v in srcs + zones),
        in_specs=[_HBM] * (2 * n) + [_SEM, _SEM, _ANY], out_specs=tuple([_HBM] * (2 * n)),
        input_output_aliases={i: i for i in range(2 * n)},
        compiler_params=pltpu.CompilerParams(has_side_effects=_EFFECT))(*srcs, *zones, send_sems, recv_sems, after)
    return list(outs[n:])


def swap_c(arrs):
    n = len(arrs)

    def body(*refs):
        ins, outs = refs[:n], refs[n:2 * n]
        send, recv = refs[2 * n:]
        x, y, c = _place()
        cps = [pltpu.make_async_remote_copy(src_ref=ins[a], dst_ref=outs[a], send_sem=send.at[a], recv_sem=recv.at[a],
                                            device_id=(x, y, 1 - c), device_id_type=MESH) for a in range(n)]
        for cp in cps:
            cp.start()
        for cp in cps:
            cp.wait_recv()
        for cp in cps:
            cp.wait_send()

    return pl.pallas_call(
        body, name="swap_c", in_specs=[_ANY] * n, out_specs=[_ANY] * n, out_shape=[_sds(a.shape, a.dtype) for a in arrs],
        scratch_shapes=[pltpu.SemaphoreType.DMA((n,)), pltpu.SemaphoreType.DMA((n,))],
        compiler_params=pltpu.CompilerParams(has_side_effects=True))(*arrs)


def allreduce_small(v):
    R = v.shape[0]

    def body(v_ref, o_ref, buf, send, recv):
        x, y, c = _place()
        me = 4 * x + 2 * y + c
        buf[0] = v_ref[...]

        def cp(k):
            dx, dy, dc = (k >> 2) & 1, (k >> 1) & 1, k & 1
            return pltpu.make_async_remote_copy(
                src_ref=v_ref, dst_ref=buf.at[k], send_sem=send.at[k - 1], recv_sem=recv.at[k - 1],
                device_id=(x ^ dx, y ^ dy, c ^ dc), device_id_type=MESH)

        cps = [cp(k) for k in range(1, 8)]
        for t in cps:
            t.start()
        for t in cps:
            t.wait_recv()
        acc = buf[me]
        for dev in range(1, 8):
            acc = acc + buf[jnp.bitwise_xor(me, dev)]
        o_ref[...] = acc
        for t in cps:
            t.wait_send()

    vm = pl.BlockSpec(memory_space=pltpu.VMEM)
    return pl.pallas_call(
        body, name="allreduce_small", in_specs=[vm], out_specs=vm, out_shape=_sds((R, 128)),
        scratch_shapes=[pltpu.VMEM((8, R, 128), F32), pltpu.SemaphoreType.DMA((7,)), pltpu.SemaphoreType.DMA((7,))],
        compiler_params=pltpu.CompilerParams(has_side_effects=True, vmem_limit_bytes=VMEM_LIMIT))(v)


def _rows8(*rows):
    n = rows[0].shape[-1]
    t = jnp.stack([r.reshape(n).astype(F32) for r in rows])
    return jnp.pad(t, ((0, 8 - len(rows)), (0, 0)))


def _lanes128(a):
    f = a.reshape(-1).astype(F32)
    return jnp.pad(f, (0, 128 - f.shape[0]))


def _layer_fwd(x, xb, W):
    pm = mm_nn(xb, W["w_main"], name="proj_main", tm=2048)
    pab = mm_nn(xb, W["w_ab"], name="proj_ab")
    q, k, v, gb = pre_qkv_fwd(pm, pab, W["cw"], W["gp"])
    sc = pre_sc_fwd(pm, W["csc"])
    u, w, qd, kd, pp, ys, eg = gdn_intra_fwd(q, k, v, gb)
    o2, s0 = gdn_scan_fwd(u, w, qd, kd, pp, eg)
    og = post_fwd(o2, pm, W["nw"])
    ya = mm_nn(og, W["w_og"], name="proj_og")
    yb = mm_nn(sc, W["w_osc"], name="proj_osc")
    mixed = merge_fwd(ya, yb, pm)
    out = mm_nn(mixed, W["w_out"], name="proj_out")
    x1, x1b = ln_fwd(x, out, W["ln1"], name="ln1_fwd")
    hfac, h = mm_nn(x1b, W["w_up"], bias=W["b_up"], relu2=True, out_dtype=MM, name="mlp_up")
    dn = mm_nn(h, W["w_down"], name="mlp_down")
    x2, x2b = ln_fwd(x1, dn, W["ln2"], name="ln2_fwd")
    saved = dict(x=x, xb=xb, pm=pm, pab=pab, q=q, k=k, v=v, gb=gb, sc=sc, o2=o2, s0=s0, og=og, ya=ya, yb=yb,
                 u=u, w=w, qd=qd, kd=kd, pp=pp, ys=ys, eg=eg,
                 mixed=mixed, out=out, x1=x1, x1b=x1b, hfac=hfac, h=h, dn=dn)
    return x2, x2b, saved


def _layer_bwd(ct, W, S):
    dxa2, dr2b, dp2 = ln_bwd(S["x1"], S["dn"], W["ln2"], ct, name="ln2_bwd")
    g_down = mm_tn(S["h"], dr2b, name="dw_down")
    dhpre, db_up = mm_nt(dr2b, W["w_down"], dact=S["hfac"], out_dtype=MM, name="mlp_down_bwd")
    g_up = mm_tn(S["x1b"], dhpre, name="dw_up")
    dx1 = mm_nt(dhpre, W["w_up"], add=dxa2, name="mlp_up_bwd")
    dxa1, dr1b, dp1 = ln_bwd(S["x"], S["out"], W["ln1"], dx1, name="ln1_bwd")
    g_out = mm_tn(S["mixed"], dr1b, name="dw_out")
    dmix = mm_nt(dr1b, W["w_out"], name="proj_out_bwd")
    dya, dyb, dpm = merge_bwd(S["ya"], S["yb"], S["pm"], dmix)
    g_og = mm_tn(S["og"], dya, name="dw_og")
    g_osc = mm_tn(S["sc"], dyb, name="dw_osc")
    dog = mm_nt(dya, W["w_og"], name="proj_og_bwd")
    dsc = mm_nt(dyb, W["w_osc"], name="proj_osc_bwd")
    do, dpm, dnw = post_bwd(S["o2"], S["pm"], W["nw"], dog, dpm)
    du, dw, dqd, dkd, dpp, deg = gdn_scan_bwd(S["u"], S["w"], S["qd"], S["kd"], S["pp"], S["eg"], S["s0"], do)
    dq2, dk2, dv2, dgb2 = gdn_intra_bwd(S["q"], S["k"], S["v"], S["gb"], S["ys"], du, dw, dqd, dkd, dpp, deg)
    dpm, dpab, dcw, dgp = pre_qkv_bwd(S["pm"], S["pab"], W["cw"], W["gp"], dq2, dk2, dv2, dgb2, dpm)
    dpm, dcsc = pre_sc_bwd(S["pm"], W["csc"], dsc, dpm)
    g_main = mm_tn(S["xb"], dpm, name="dw_main", tn=NMAIN // 4)
    g_ab = mm_tn(S["xb"], dpab, name="dw_ab")
    t = mm_nt(dpab, W["w_ab"], add=dxa1, name="proj_ab_bwd")
    dx = mm_nt(dpm, W["w_main"], add=t, name="proj_main_bwd")
    g_in = jnp.concatenate([g_main[:, :3 * D], g_main[:, 8 * D:], g_ab[:, :4 * NH], g_main[:, 3 * D:8 * D]], axis=1)
    grads = dict(
        w_in=g_in, w_o_gdn=g_og, w_o_sc=g_osc, w_out=g_out, w_up=g_up, w_down=g_down,
        conv_qkv=dcw[:3], conv_sc=dcsc[:3], a_log=dgp[0, :2 * NH].reshape(2, NH), dt_bias=dgp[1, :2 * NH].reshape(2, NH),
        gdn_norm_w=dnw[0], ln1_g=dp1[1], ln1_b=dp1[2], b_up=db_up[0], b_down=dp2[0], ln2_g=dp2[1], ln2_b=dp2[2])
    return dx, grads


def _layer_weights(l, full, i, conv, a_log, dt_bias, gdn_norm_w, ln1_g, ln1_b, b_up, b_down, ln2_g, ln2_b):
    w_in = full["w_in"][i]
    w_main = jnp.concatenate([w_in[:, :3 * D], w_in[:, 4 * D + 4 * NH:], w_in[:, 3 * D:4 * D]], axis=1)
    w_ab = jnp.pad(w_in[:, 4 * D:4 * D + 4 * NH], ((0, 0), (0, 128 - 4 * NH)))
    return dict(
        w_main=w_main, w_ab=w_ab, w_og=full["w_o_gdn"][i], w_osc=full["w_o_sc"][i], w_out=full["w_out"][i],
        w_up=full["w_up"][i], w_down=full["w_down"][i],
        cw=jnp.pad(conv["conv_qkv"][l].astype(F32), ((0, 5), (0, 0))),
        csc=jnp.pad(conv["conv_sc"][l].astype(F32), ((0, 5), (0, 0))),
        gp=_rows8(_lanes128(a_log[l]), _lanes128(dt_bias[l])), nw=_rows8(gdn_norm_w[l]),
        ln1=_rows8(jnp.zeros((D,), F32), ln1_g[l], ln1_b[l]), ln2=_rows8(b_down[l], ln2_g[l], ln2_b[l]),
        b_up=b_up[l].reshape(1, DFF).astype(F32))


def local_step(xs, target, weights_of, after_bwd):
    x, xb = xs, xs.astype(MM)
    Ws, saved = [], []
    for l in range(DEPTH):
        Ws.append(weights_of(l, x))
        x, xb, S = _layer_fwd(x, xb, Ws[l])
        saved.append(S)
    loss_tile, ct = loss_fwd_bwd(x, target)
    token = None
    for l in reversed(range(DEPTH)):
        W = Ws[l] if token is None else dict(Ws[l], ln2=Ws[l]["ln2"] + token[0, 0])
        ct, grads = _layer_bwd(ct, W, saved[l])
        token = after_bwd(l, grads)
    return loss_tile, ct


EARLY = 1
BIG = ("w_in", "w_o_gdn", "w_o_sc", "w_out", "w_up", "w_down")
SMALL = ("conv_qkv", "a_log", "dt_bias", "gdn_norm_w", "conv_sc", "ln1_g", "ln1_b", "b_up", "b_down", "ln2_g", "ln2_b")
ORDER = ("w_in", "conv_qkv", "a_log", "dt_bias", "gdn_norm_w", "w_o_gdn", "conv_sc", "w_o_sc", "w_out", "ln1_g",
         "ln1_b", "w_up", "b_up", "w_down", "b_down", "ln2_g", "ln2_b")


def _pack(arrs):
    flat = jnp.concatenate([a.reshape(-1).astype(F32) for a in arrs])
    n = flat.shape[0]
    rows = -(-n // 1024) * 8
    return jnp.pad(flat, (0, rows * 128 - n)).reshape(rows, 128)


def _unpack(buf, like):
    flat = buf.reshape(-1)
    out, o = [], 0
    for a in like:
        n = 1
        for s in a.shape:
            n *= s
        out.append(flat[o:o + n].reshape(a.shape))
        o += n
    return out


def _gathered(name, g):
    if name in ("w_in", "w_up", "conv_qkv", "conv_sc"):
        t = jnp.moveaxis(g, 0, -2)
        return t.reshape(t.shape[:-2] + (t.shape[-2] * t.shape[-1],))
    t = jnp.moveaxis(g, 0, 1)
    return t.reshape((t.shape[0], t.shape[1] * t.shape[2]) + t.shape[3:])


def _by_chip(name, g):
    if name in ("w_in", "w_up"):
        r, ccols = g.shape
        return jnp.moveaxis(g.reshape(r, 4, ccols // 4), 1, 0)
    return g.reshape((4, g.shape[0] // 4) + g.shape[1:])


def kernel(x, w_in, conv_qkv, a_log, dt_bias, gdn_norm_w, w_o_gdn, conv_sc, w_o_sc, w_out, ln1_g, ln1_b, w_up, b_up, w_down, b_down, ln2_g, ln2_b, loss_target, m_w_in, m_conv_qkv, m_a_log, m_dt_bias, m_gdn_norm_w, m_w_o_gdn, m_conv_sc, m_w_o_sc, m_w_out, m_ln1_g, m_ln1_b, m_w_up, m_b_up, m_w_down, m_b_down, m_ln2_g, m_ln2_b, v_w_in, v_conv_qkv, v_a_log, v_dt_bias, v_gdn_norm_w, v_w_o_gdn, v_conv_sc, v_w_o_sc, v_w_out, v_ln1_g, v_ln1_b, v_w_up, v_b_up, v_w_down, v_b_down, v_ln2_g, v_ln2_b):
    w = dict(w_in=w_in, conv_qkv=conv_qkv, a_log=a_log, dt_bias=dt_bias, gdn_norm_w=gdn_norm_w, w_o_gdn=w_o_gdn,
             conv_sc=conv_sc, w_o_sc=w_o_sc, w_out=w_out, ln1_g=ln1_g, ln1_b=ln1_b, w_up=w_up, b_up=b_up,
             w_down=w_down, b_down=b_down, ln2_g=ln2_g, ln2_b=ln2_b)
    m = dict(w_in=m_w_in, conv_qkv=m_conv_qkv, a_log=m_a_log, dt_bias=m_dt_bias, gdn_norm_w=m_gdn_norm_w,
             w_o_gdn=m_w_o_gdn, conv_sc=m_conv_sc, w_o_sc=m_w_o_sc, w_out=m_w_out, ln1_g=m_ln1_g, ln1_b=m_ln1_b,
             w_up=m_w_up, b_up=m_b_up, w_down=m_w_down, b_down=m_b_down, ln2_g=m_ln2_g, ln2_b=m_ln2_b)
    v = dict(w_in=v_w_in, conv_qkv=v_conv_qkv, a_log=v_a_log, dt_bias=v_dt_bias, gdn_norm_w=v_gdn_norm_w,
             w_o_gdn=v_w_o_gdn, conv_sc=v_conv_sc, w_o_sc=v_w_o_sc, w_out=v_w_out, ln1_g=v_ln1_g, ln1_b=v_ln1_b,
             w_up=v_w_up, b_up=v_b_up, w_down=v_w_down, b_down=v_b_down, ln2_g=v_ln2_g, ln2_b=v_ln2_b)
    chip = 2 * lax.axis_index("x") + lax.axis_index("y")

    names = BIG + ("conv_qkv", "conv_sc")
    blocks = [w[n].astype(MM) if n in BIG else w[n] for n in names]
    got = allgather_xy([b[:EARLY] if n in BIG else b for n, b in zip(names, blocks)])
    early = {n: _gathered(n, g) for n, g in zip(names, got)}
    gather, gather_zones, token = xy_start("gather", "gather_start", [b[EARLY:] for b in blocks[:len(BIG)]],
                                           after=got[0])
    vectors = (a_log, dt_bias, gdn_norm_w, ln1_g, ln1_b, b_up, b_down, ln2_g, ln2_b)
    late, grads, scatters, zones = {}, [None] * DEPTH, [None] * DEPTH, [None]

    def weights_of(l, x_l):
        if l < EARLY:
            return _layer_weights(l, early, l, early, *vectors)
        if not late:
            for n, zone in zip(BIG, xy_wait("gather_wait", gather, gather_zones, after=x_l)):
                late[n] = _gathered(n, zone)
        return _layer_weights(l, late, l - EARLY, early, *vectors)

    def after_bwd(l, g):
        grads[l] = g
        scatters[l], zones[0], tok = xy_start(
            "scatter", "scatter_start_%d" % l, [_by_chip(n, g[n]).astype(MM) for n in BIG], after=g["ln2_g"],
            zones=zones[0], layer=l)
        return tok

    loss_tile, dx = local_step(x[0] + token[0, 0], loss_target[0], weights_of, after_bwd)
    loss = lax.psum(loss_tile[0, 0], ("x", "y", "c"))

    arrived = zones[0]
    for l in range(DEPTH):
        arrived = xy_wait("scatter_wait_%d" % l, scatters[l], arrived, after=dx)
    part = [sum4(z.reshape(4, -1, z.shape[-1]), name="sum_" + n) for n, z in zip(BIG, arrived)]
    other = swap_c(part)
    out = {}
    for n, mine, theirs in zip(BIG, part, other):
        cols = mine.shape[-1]
        res = adam(w[n].reshape(-1, cols), m[n].reshape(-1, cols), v[n].reshape(-1, cols), mine, theirs, name="adam_" + n)
        out[n] = [r.reshape(w[n].shape) for r in res]

    stacked = [jnp.stack([grads[l][n] for l in range(DEPTH)]) for n in SMALL]
    summed = _unpack(allreduce_small(_pack(stacked)), stacked)
    gs = []
    for n, g in zip(SMALL, summed):
        if n in ("conv_qkv", "conv_sc"):
            blk = w[n].shape[-1]
            g = lax.dynamic_slice_in_dim(g, chip * blk, blk, axis=2)
        gs.append(g)
    res = adam(_pack([w[n] for n in SMALL]), _pack([m[n] for n in SMALL]), _pack([v[n] for n in SMALL]), _pack(gs),
               name="adam_small")
    for n, parts in zip(SMALL, zip(*[_unpack(r, gs) for r in res])):
        out[n] = list(parts)

    outs = [loss, dx[None]]
    for kind in range(4):
        outs += [out[n][kind] for n in ORDER]
    return tuple(outs)
```

```python
import jax
import jax.numpy as jnp
from jax import lax
from jax.experimental import pallas as pl
from jax.experimental.pallas import tpu as pltpu

F32 = jnp.float32
MM = jnp.bfloat16
HI = lax.Precision.HIGHEST

D = 1024
NH = 8
HD = 128
CH = 64
DFF = 4 * D
DEPTH = 4
LN_EPS = 1e-5
RMS_EPS = 1e-6
L2_EPS = 1e-6
ALPHA = (2 * DEPTH) ** 0.25
LR, B1, B2, EPS, WD, STEP = 0.001, 0.9, 0.999, 1e-08, 0.01, 10

NMAIN = 9 * D
CQ, CK, CV, CSB, CSC, CSX, CGA, CGB, CZ = range(9)
HALO = 8
VMEM_LIMIT = 56 * 1024 * 1024
MESH = pl.DeviceIdType.MESH


def _cp(sem=None, vmem=VMEM_LIMIT):
    return pltpu.CompilerParams(dimension_semantics=sem, vmem_limit_bytes=vmem)


def _sds(shape, dtype=F32):
    return jax.ShapeDtypeStruct(tuple(shape), dtype)


def _accumulate(acc, product, k, nk, finish):
    if nk == 1:
        finish(product())
        return

    @pl.when(k == 0)
    def _():
        acc[...] = jnp.zeros_like(acc)

    acc[...] += product()

    @pl.when(k == nk - 1)
    def _():
        finish(acc[...])


def mm_nn(a, b, *, name, bias=None, relu2=False, add=None, out_dtype=F32, tm=1024, tn=1024, tk=1024):
    M, K = a.shape
    N = b.shape[1]
    tm, tn, tk = min(tm, M), min(tn, N), min(tk, K)
    nk = K // tk

    def body(*refs):
        it = iter(refs)
        a_ref, b_ref = next(it), next(it)
        bias_ref = next(it) if bias is not None else None
        add_ref = next(it) if add is not None else None
        o_ref = next(it)
        h_ref = next(it) if relu2 else None
        acc = next(it) if nk > 1 else None
        prod = lambda: jnp.dot(a_ref[...].astype(MM), b_ref[...].astype(MM), preferred_element_type=F32)

        def finish(r):
            if bias_ref is not None:
                r = r + bias_ref[...]
            if add_ref is not None:
                r = r + add_ref[...]
            if relu2:
                t = jnp.maximum(r, 0.0)
                o_ref[...] = (2.0 * t).astype(o_ref.dtype)
                h_ref[...] = (t * t).astype(h_ref.dtype)
            else:
                o_ref[...] = r.astype(o_ref.dtype)

        _accumulate(acc, prod, pl.program_id(2), nk, finish)

    in_specs = [pl.BlockSpec((tm, tk), lambda i, j, k: (i, k)), pl.BlockSpec((tk, tn), lambda i, j, k: (k, j))]
    args = [a, b]
    if bias is not None:
        in_specs.append(pl.BlockSpec((1, tn), lambda i, j, k: (0, j)))
        args.append(bias)
    if add is not None:
        in_specs.append(pl.BlockSpec((tm, tn), lambda i, j, k: (i, j)))
        args.append(add)
    out_shape = [_sds((M, N), out_dtype)]
    out_specs = [pl.BlockSpec((tm, tn), lambda i, j, k: (i, j))]
    if relu2:
        out_shape.append(_sds((M, N), MM))
        out_specs.append(pl.BlockSpec((tm, tn), lambda i, j, k: (i, j)))
    res = pl.pallas_call(
        body, name=name, grid=(M // tm, N // tn, nk), in_specs=in_specs, out_specs=out_specs, out_shape=out_shape,
        scratch_shapes=[pltpu.VMEM((tm, tn), F32)] if nk > 1 else [],
        compiler_params=_cp(("parallel", "parallel", "arbitrary")))(*args)
    return res if relu2 else res[0]


def mm_nt(a, b, *, name, add=None, dact=None, out_dtype=F32, tm=1024, tn=1024, tk=1024):
    M, Nc = a.shape
    Ko = b.shape[0]
    tm, tn, tk = min(tm, M), min(tn, Ko), min(tk, Nc)
    nk = Nc // tk
    ni = M // tm

    def body(*refs):
        it = iter(refs)
        a_ref, b_ref = next(it), next(it)
        add_ref = next(it) if add is not None else None
        d_ref = next(it) if dact is not None else None
        o_ref = next(it)
        db_ref = next(it) if dact is not None else None
        acc = next(it) if nk > 1 else None
        i = pl.program_id(1)
        prod = lambda: lax.dot_general(a_ref[...].astype(MM), b_ref[...].astype(MM), (((1,), (1,)), ((), ())),
                                       preferred_element_type=F32)

        def finish(r):
            if add_ref is not None:
                r = r + add_ref[...]
            if d_ref is not None:
                r = r * d_ref[...].astype(F32)
                s = jnp.sum(r, axis=0, keepdims=True)
                row0 = lax.broadcasted_iota(jnp.int32, db_ref.shape, 0) == 0

                @pl.when(i == 0)
                def _():
                    db_ref[...] = jnp.zeros_like(db_ref)

                db_ref[...] += jnp.where(row0, s, 0.0)
            o_ref[...] = r.astype(o_ref.dtype)

        _accumulate(acc, prod, pl.program_id(2), nk, finish)

    in_specs = [pl.BlockSpec((tm, tk), lambda j, i, k: (i, k)), pl.BlockSpec((tn, tk), lambda j, i, k: (j, k))]
    args = [a, b]
    for extra in (add, dact):
        if extra is not None:
            in_specs.append(pl.BlockSpec((tm, tn), lambda j, i, k: (i, j)))
            args.append(extra)
    out_shape = [_sds((M, Ko), out_dtype)]
    out_specs = [pl.BlockSpec((tm, tn), lambda j, i, k: (i, j))]
    if dact is not None:
        out_shape.append(_sds((8, Ko), F32))
        out_specs.append(pl.BlockSpec((8, tn), lambda j, i, k: (0, j)))
    res = pl.pallas_call(
        body, name=name, grid=(Ko // tn, ni, nk), in_specs=in_specs, out_specs=out_specs, out_shape=out_shape,
        scratch_shapes=[pltpu.VMEM((tm, tn), F32)] if nk > 1 else [],
        compiler_params=_cp(("parallel", "arbitrary", "arbitrary")))(*args)
    return res if dact is not None else res[0]


def mm_tn(a, b, *, name, tm=1024, tn=1024, tk=1024):
    T, M = a.shape
    N = b.shape[1]
    tm, tn, tk = min(tm, M), min(tn, N), min(tk, T)

    def body(a_ref, b_ref, o_ref):
        @pl.when(pl.program_id(2) == 0)
        def _():
            o_ref[...] = jnp.zeros_like(o_ref)

        o_ref[...] += lax.dot_general(a_ref[...].astype(MM), b_ref[...].astype(MM), (((0,), (0,)), ((), ())),
                                      preferred_element_type=F32)

    return pl.pallas_call(
        body, name=name, grid=(M // tm, N // tn, T // tk),
        in_specs=[pl.BlockSpec((tk, tm), lambda i, j, k: (k, i)), pl.BlockSpec((tk, tn), lambda i, j, k: (k, j))],
        out_specs=pl.BlockSpec((tm, tn), lambda i, j, k: (i, j)), out_shape=_sds((M, N)),
        compiler_params=_cp(("parallel", "parallel", "arbitrary")))(a, b)


def _sigmoid(x):
    return 1.0 / (1.0 + jnp.exp(-x))


def _silu(x):
    return x * _sigmoid(x)


def _softplus(x):
    return jnp.maximum(x, 0.0) + jnp.log1p(jnp.exp(-jnp.abs(x)))


def _ext(main_ref, prev_ref, next_ref, first, last):
    p = jnp.where(first, 0.0, prev_ref[...].astype(F32))
    n = jnp.where(last, 0.0, next_ref[...].astype(F32))
    return jnp.concatenate([p, main_ref[...].astype(F32), n], axis=0)


def _shift_dn(x):
    return pltpu.roll(x, 1, 0)


def _shift_up(x):
    return pltpu.roll(x, x.shape[0] - 1, 0)


def _conv3(xe, w):
    return w[0:1, :] * _shift_dn(xe) + w[1:2, :] * xe + w[2:3, :] * _shift_up(xe)


def _conv3_t(de, w):
    return w[0:1, :] * _shift_up(de) + w[1:2, :] * de + w[2:3, :] * _shift_dn(de)


def _halo_specs(bt, T, col, lead=None):
    r = bt // HALO
    last = T // HALO - 1
    if lead is None:
        return [pl.BlockSpec((bt, D), lambda i: (i, col)),
                pl.BlockSpec((HALO, D), lambda i: (jnp.maximum(i * r - 1, 0), col)),
                pl.BlockSpec((HALO, D), lambda i: (jnp.minimum((i + 1) * r, last), col))]
    return [pl.BlockSpec((lead, bt, D), lambda i: (0, i, col)),
            pl.BlockSpec((lead, HALO, D), lambda i: (0, jnp.maximum(i * r - 1, 0), col)),
            pl.BlockSpec((lead, HALO, D), lambda i: (0, jnp.minimum((i + 1) * r, last), col))]


def _qkv_rows(cq, ck, cv):
    sq, sk, sv = _silu(cq), _silu(ck), _silu(cv)
    qs, ks = [], []
    for h in range(NH):
        s = slice(h * HD, (h + 1) * HD)
        qh, kh = sq[:, s], sk[:, s]
        qs.append(qh * lax.rsqrt(jnp.sum(qh * qh, axis=-1, keepdims=True) + L2_EPS) * (HD ** -0.5))
        ks.append(kh * lax.rsqrt(jnp.sum(kh * kh, axis=-1, keepdims=True) + L2_EPS))
    return jnp.concatenate(qs, axis=1), jnp.concatenate(ks, axis=1), sv


def _chunk_masks(bt):
    row = lax.broadcasted_iota(jnp.int32, (bt, bt), 0)
    col = lax.broadcasted_iota(jnp.int32, (bt, bt), 1)
    same = (row // CH) == (col // CH)
    lower = jnp.where(same & (col <= row), 1.0, 0.0).astype(F32)
    upper = jnp.where(same & (col >= row), 1.0, 0.0).astype(F32)
    return lower, upper


def _gate_rows(ab, gp, lower, upper):
    lane = lax.broadcasted_iota(jnp.int32, ab.shape, 1)
    g = -jnp.exp(gp[0:1, :]) * _softplus(ab + gp[1:2, :])
    g = jnp.where(lane < 2 * NH, g, 0.0)
    gf = jnp.dot(lower, g, precision=HI, preferred_element_type=F32)
    gr = jnp.dot(upper, g, precision=HI, preferred_element_type=F32)
    gc = jnp.where(lane < NH, gf, gr)
    beta = _sigmoid(ab)
    return jnp.where(lane < 2 * NH, gc, jnp.where(lane < 4 * NH, beta, 0.0))


def pre_qkv_fwd(pm, pab, cw, gp, *, bt=256):
    T = pm.shape[0]
    bt = min(bt, T)
    n = T // bt

    def body(q0, q1, q2, k0, k1, k2, v0, v1, v2, ab_ref, cw_ref, gp_ref, q_ref, k_ref, v_ref, gb_ref):
        i = pl.program_id(0)
        first, last = i == 0, i == n - 1
        cs = []
        for c, (m, p, x) in enumerate(((q0, q1, q2), (k0, k1, k2), (v0, v1, v2))):
            xe = _ext(m, p, x, first, last)
            cs.append(_conv3(xe, cw_ref[:, c * D:(c + 1) * D])[HALO:HALO + bt])
        q, k, v = _qkv_rows(*cs)
        q_ref[...], k_ref[...], v_ref[...] = q, k, v
        lower, upper = _chunk_masks(bt)
        gb_ref[...] = _gate_rows(ab_ref[...], gp_ref[...], lower, upper)

    in_specs = (_halo_specs(bt, T, CQ) + _halo_specs(bt, T, CK) + _halo_specs(bt, T, CV)
                + [pl.BlockSpec((bt, 128), lambda i: (i, 0)), pl.BlockSpec((8, 3 * D), lambda i: (0, 0)),
                   pl.BlockSpec((8, 128), lambda i: (0, 0))])
    row = pl.BlockSpec((bt, D), lambda i: (i, 0))
    return pl.pallas_call(
        body, name="pre_qkv_fwd", grid=(n,), in_specs=in_specs,
        out_specs=[row, row, row, pl.BlockSpec((bt, 128), lambda i: (i, 0))],
        out_shape=[_sds((T, D)), _sds((T, D)), _sds((T, D)), _sds((T, 128))],
        compiler_params=_cp(("parallel",)))(*([pm] * 9), pab, cw, gp)


def pre_qkv_bwd(pm, pab, cw, gp, dq2, dk2, dv2, dgb2, dpm, *, bt=128):
    T = pm.shape[0]
    bt = min(bt, T)
    n = T // bt
    E = bt + 2 * HALO

    def body(*refs):
        it = iter(refs)
        xs = [[next(it) for _ in range(3)] for _ in range(3)]
        ds = [[next(it) for _ in range(3)] for _ in range(3)]
        ab_ref, dgb_ref, cw_ref, gp_ref, _alias = next(it), next(it), next(it), next(it), next(it)
        o_ref, dab_ref, dcw_ref, dgp_ref = (next(it) for _ in range(4))
        i = pl.program_id(0)
        first, last = i == 0, i == n - 1

        @pl.when(first)
        def _():
            dcw_ref[...] = jnp.zeros_like(dcw_ref)
            dgp_ref[...] = jnp.zeros_like(dgp_ref)

        xes = [_ext(*xs[c], first, last) for c in range(3)]
        ces = [_conv3(xes[c], cw_ref[:, c * D:(c + 1) * D]) for c in range(3)]
        cts = []
        for c in range(3):
            m, p, x = ds[c]
            pe = jnp.where(first, 0.0, p[0] + p[1])
            ne = jnp.where(last, 0.0, x[0] + x[1])
            cts.append(jnp.concatenate([pe, m[0] + m[1], ne], axis=0))
        _, vjp = jax.vjp(_qkv_rows, *ces)
        dces = vjp(tuple(cts))
        rowi = lax.broadcasted_iota(jnp.int32, (E, 1), 0)
        central = (rowi >= HALO) & (rowi < HALO + bt)
        row8 = lax.broadcasted_iota(jnp.int32, (8, D), 0)
        for c in range(3):
            w = cw_ref[:, c * D:(c + 1) * D]
            o_ref[:, c * D:(c + 1) * D] = _conv3_t(dces[c], w)[HALO:HALO + bt].astype(o_ref.dtype)
            dc = jnp.where(central, dces[c], 0.0)
            taps = (jnp.sum(dc * _shift_dn(xes[c]), axis=0, keepdims=True),
                    jnp.sum(dc * xes[c], axis=0, keepdims=True),
                    jnp.sum(dc * _shift_up(xes[c]), axis=0, keepdims=True))
            upd = jnp.where(row8 == 0, taps[0], jnp.where(row8 == 1, taps[1], jnp.where(row8 == 2, taps[2], 0.0)))
            dcw_ref[:, c * D:(c + 1) * D] += upd
        lower, upper = _chunk_masks(bt)
        _, gvjp = jax.vjp(lambda ab, gp: _gate_rows(ab, gp, lower, upper), ab_ref[...], gp_ref[...])
        dab, dgp = gvjp(dgb_ref[0] + dgb_ref[1])
        dab_ref[...] = dab
        dgp_ref[...] += dgp

    in_specs = (_halo_specs(bt, T, CQ) + _halo_specs(bt, T, CK) + _halo_specs(bt, T, CV)
                + _halo_specs(bt, T, 0, lead=2) * 3
                + [pl.BlockSpec((bt, 128), lambda i: (i, 0)), pl.BlockSpec((2, bt, 128), lambda i: (0, i, 0)),
                   pl.BlockSpec((8, 3 * D), lambda i: (0, 0)), pl.BlockSpec((8, 128), lambda i: (0, 0)),
                   pl.BlockSpec(memory_space=pl.ANY)])
    out_specs = [pl.BlockSpec((bt, 3 * D), lambda i: (i, 0)), pl.BlockSpec((bt, 128), lambda i: (i, 0)),
                 pl.BlockSpec((8, 3 * D), lambda i: (0, 0)), pl.BlockSpec((8, 128), lambda i: (0, 0))]
    return pl.pallas_call(
        body, name="pre_qkv_bwd", grid=(n,), in_specs=in_specs, out_specs=out_specs,
        out_shape=[_sds(dpm.shape, dpm.dtype), _sds((T, 128)), _sds((8, 3 * D)), _sds((8, 128))],
        input_output_aliases={len(in_specs) - 1: 0},
        compiler_params=_cp(("arbitrary",)))(
            *([pm] * 9), dq2, dq2, dq2, dk2, dk2, dk2, dv2, dv2, dv2, pab, dgb2, cw, gp, dpm)


def pre_sc_fwd(pm, cw, *, bt=256):
    T = pm.shape[0]
    bt = min(bt, T)
    n = T // bt

    def body(b_ref, c0, c1, c2, x0, x1, x2, cw_ref, o_ref):
        i = pl.program_id(0)
        first, last = i == 0, i == n - 1
        pe = _ext(c0, c1, c2, first, last) * _ext(x0, x1, x2, first, last)
        o_ref[...] = (b_ref[...] * _conv3(pe, cw_ref[...])[HALO:HALO + bt]).astype(o_ref.dtype)

    in_specs = ([pl.BlockSpec((bt, D), lambda i: (i, CSB))] + _halo_specs(bt, T, CSC) + _halo_specs(bt, T, CSX)
                + [pl.BlockSpec((8, D), lambda i: (0, 0))])
    return pl.pallas_call(
        body, name="pre_sc_fwd", grid=(n,), in_specs=in_specs, out_specs=pl.BlockSpec((bt, D), lambda i: (i, 0)),
        out_shape=_sds((T, D), MM), compiler_params=_cp(("parallel",)))(*([pm] * 7), cw)


def pre_sc_bwd(pm, cw, dsc, dpm, *, bt=256):
    T = pm.shape[0]
    bt = min(bt, T)
    n = T // bt
    E = bt + 2 * HALO

    def body(b0, b1, b2, c0, c1, c2, x0, x1, x2, d0, d1, d2, cw_ref, _alias, o_ref, dcw_ref):
        i = pl.program_id(0)
        first, last = i == 0, i == n - 1

        @pl.when(first)
        def _():
            dcw_ref[...] = jnp.zeros_like(dcw_ref)

        ce, xe = _ext(c0, c1, c2, first, last), _ext(x0, x1, x2, first, last)
        pe = ce * xe
        w = cw_ref[...]
        dout = d0[...]
        o_ref[:, 0:D] = (dout * _conv3(pe, w)[HALO:HALO + bt]).astype(o_ref.dtype)
        dce = _ext(d0, d1, d2, first, last) * _ext(b0, b1, b2, first, last)
        dp = _conv3_t(dce, w)[HALO:HALO + bt]
        o_ref[:, D:2 * D] = (dp * x0[...]).astype(o_ref.dtype)
        o_ref[:, 2 * D:3 * D] = (dp * c0[...]).astype(o_ref.dtype)
        rowi = lax.broadcasted_iota(jnp.int32, (E, 1), 0)
        dc = jnp.where((rowi >= HALO) & (rowi < HALO + bt), dce, 0.0)
        row8 = lax.broadcasted_iota(jnp.int32, (8, D), 0)
        taps = (jnp.sum(dc * _shift_dn(pe), axis=0, keepdims=True), jnp.sum(dc * pe, axis=0, keepdims=True),
                jnp.sum(dc * _shift_up(pe), axis=0, keepdims=True))
        dcw_ref[...] += jnp.where(row8 == 0, taps[0], jnp.where(row8 == 1, taps[1], jnp.where(row8 == 2, taps[2], 0.0)))

    dsc_specs = [pl.BlockSpec((bt, D), lambda i: (i, 0)),
                 pl.BlockSpec((HALO, D), lambda i: (jnp.maximum(i * (bt // HALO) - 1, 0), 0)),
                 pl.BlockSpec((HALO, D), lambda i: (jnp.minimum((i + 1) * (bt // HALO), T // HALO - 1), 0))]
    in_specs = (_halo_specs(bt, T, CSB) + _halo_specs(bt, T, CSC) + _halo_specs(bt, T, CSX) + dsc_specs
                + [pl.BlockSpec((8, D), lambda i: (0, 0)), pl.BlockSpec(memory_space=pl.ANY)])
    return pl.pallas_call(
        body, name="pre_sc_bwd", grid=(n,), in_specs=in_specs,
        out_specs=[pl.BlockSpec((bt, 3 * D), lambda i: (i, 1)), pl.BlockSpec((8, D), lambda i: (0, 0))],
        out_shape=[_sds(dpm.shape, dpm.dtype), _sds((8, D))], input_output_aliases={len(in_specs) - 1: 0},
        compiler_params=_cp(("arbitrary",)))(*([pm] * 9), dsc, dsc, dsc, cw, dpm)


def _bdot(a, b, dims):
    return lax.dot_general(a.astype(MM), b.astype(MM), (dims, ((), ())), preferred_element_type=F32)


_NN, _NT, _TN = ((1,), (0,)), ((1,), (1,)), ((0,), (0,))


def _raw_nn(a, b):
    return _bdot(a, b, _NN)


def _raw_nt(a, b):
    return _bdot(a, b, _NT)


def _raw_tn(a, b):
    return _bdot(a, b, _TN)


def _make_vjp_ops():
    @jax.custom_vjp
    def nn(a, b):
        return _raw_nn(a, b)

    @jax.custom_vjp
    def nt(a, b):
        return _raw_nt(a, b)

    @jax.custom_vjp
    def tn(a, b):
        return _raw_tn(a, b)

    nn.defvjp(lambda a, b: (_raw_nn(a, b), (a, b)), lambda r, g: (_raw_nt(g, r[1]), _raw_tn(r[0], g)))
    nt.defvjp(lambda a, b: (_raw_nt(a, b), (a, b)), lambda r, g: (_raw_nn(g, r[1]), _raw_tn(g, r[0])))
    tn.defvjp(lambda a, b: (_raw_tn(a, b), (a, b)), lambda r, g: (_raw_nt(r[1], g), _raw_nn(r[0], g)))

    @jax.custom_vjp
    def inv_saved(A, Y):
        return Y

    def inv_bwd(Y, g):
        M = g + _raw_tn(Y, g)
        return -(M + _raw_nt(M, Y)), jnp.zeros_like(Y)

    inv_saved.defvjp(lambda A, Y: (Y, Y), inv_bwd)
    return nn, nt, tn, inv_saved


GH = 2
GR = GH * CH
NG = NH // GH
CBI = 2
CBB = 2


def _scan_chunks(n_chunks):
    return 4 if n_chunks % 4 == 0 else 2


def _tri_inv_y_all(As):
    Ys = [-A for A in As]
    Ps = [_raw_nn(A, A) for A in As]
    for stage in range(5):
        squares = [_raw_nn(P, P) for P in Ps] if stage < 4 else None
        Ys = [Y + P + _raw_nn(Y, P) for Y, P in zip(Ys, Ps)]
        Ps = squares
    return Ys


def _intra_groups(chains, incl, strict, eye, lastc, ops, inv_all):
    nn, nt, tn = ops
    st = []
    for qs, ks, vs, gcols, bcols in chains:
        q, k, v = (jnp.concatenate(t, axis=0) for t in (qs, ks, vs))
        gcol, bcol = jnp.concatenate(gcols, axis=0), jnp.concatenate(bcols, axis=0)
        grow = jnp.sum(eye * gcol, axis=0, keepdims=True)
        gam = jnp.where(incl, jnp.exp(jnp.where(incl, gcol - grow, 0.0)), 0.0)
        glast = jnp.sum(jnp.where(lastc, grow, 0.0), axis=1, keepdims=True)
        st.append((q, k, v, gcol, bcol, gam, glast, jnp.exp(gcol), k * bcol))
    As = [jnp.where(strict, nt(kb, k) * gam, 0.0) for (q, k, v, gcol, bcol, gam, glast, eg, kb) in st]
    Ys = inv_all(As)
    vbs = [v * bcol for (q, k, v, gcol, bcol, gam, glast, eg, kb) in st]
    kgs = [kb * eg for (q, k, v, gcol, bcol, gam, glast, eg, kb) in st]
    us = [vb + nn(Y, vb) for Y, vb in zip(Ys, vbs)]
    ws = [kg + nn(Y, kg) for Y, kg in zip(Ys, kgs)]
    Ps = [nt(q, k) * gam for (q, k, v, gcol, bcol, gam, glast, eg, kb) in st]
    return [((u, w, P, q * eg, k * jnp.exp(glast - gcol), jnp.exp(glast)), Y)
            for u, w, P, Y, (q, k, v, gcol, bcol, gam, glast, eg, kb) in zip(us, ws, Ps, Ys, st)]


def _scan_groups(chains, ops):
    nn, nt, tn, _ = ops
    vns = [[us[j] - nn(ws[j], Ss[j]) for j in range(GH)] for Ss, us, ws, P, qds, kds, egls in chains]
    os_ = [jnp.concatenate([nn(qds[j], Ss[j]) for j in range(GH)], axis=0) + nn(P, jnp.concatenate(vn, axis=0))
           for (Ss, us, ws, P, qds, kds, egls), vn in zip(chains, vns)]
    S2s = [[Ss[j] * egls[j] + tn(kds[j], vn[j]) for j in range(GH)]
           for (Ss, us, ws, P, qds, kds, egls), vn in zip(chains, vns)]
    return list(zip(os_, S2s))


def _group_masks(rev, rows=GR):
    r = lax.broadcasted_iota(jnp.int32, (rows, rows), 0)
    c = lax.broadcasted_iota(jnp.int32, (rows, rows), 1)
    same = (r // CH) == (c // CH)
    ahead = jnp.where(rev, c - r, r - c)
    incl = same & (ahead >= 0)
    strict = same & (ahead > 0)
    eye = jnp.where(r == c, 1.0, 0.0).astype(F32)
    lastc = same & ((c % CH) == jnp.where(rev, 0, CH - 1))
    return incl, strict, eye, lastc


def _head_gates(gb, h, rev):
    gcol = jnp.where(rev, gb[:, NH + h:NH + h + 1], gb[:, h:h + 1])
    bcol = jnp.where(rev, gb[:, 3 * NH + h:3 * NH + h + 1], gb[:, 2 * NH + h:2 * NH + h + 1])
    return gcol, bcol


def _hs(h):
    return slice(h * HD, (h + 1) * HD)


def _blockdiag(a, b):
    z = jnp.zeros_like(a)
    return jnp.concatenate([jnp.concatenate([a, z], axis=1), jnp.concatenate([z, b], axis=1)], axis=0)


def _load_chain(q_ref, k_ref, v_ref, gb_ref, c, g, rev, gh=GH):
    r = slice(c * CH, (c + 1) * CH)
    heads = range(g * gh, (g + 1) * gh)
    gates = [_head_gates(gb_ref[r, :], h, rev) for h in heads]
    return ([q_ref[r, _hs(h)] for h in heads], [k_ref[r, _hs(h)] for h in heads], [v_ref[r, _hs(h)] for h in heads],
            [t[0] for t in gates], [t[1] for t in gates])


def gdn_intra_fwd(q, k, v, gb):
    T = q.shape[0]
    N = T // CH
    ops = (_raw_nn, _raw_nt, _raw_tn)

    def body(q_ref, k_ref, v_ref, gb_ref, u_ref, w_ref, qd_ref, kd_ref, pp_ref, ys_ref, eg_ref):
        rev = pl.program_id(0) == 1
        masks = _group_masks(rev)
        where = [(c, g) for c in range(CBI) for g in range(NG)]
        chains = [_load_chain(q_ref, k_ref, v_ref, gb_ref, c, g, rev) for c, g in where]
        for (c, g), ((u, w, P, qd, kd, egl), Y) in zip(where, _intra_groups(chains, *masks, ops, _tri_inv_y_all)):
            r = slice(c * CH, (c + 1) * CH)
            pp_ref[c, g] = P.astype(MM)
            ys_ref[c, g] = Y.astype(MM)
            for j, h in enumerate(range(g * GH, (g + 1) * GH)):
                rows = slice(j * CH, (j + 1) * CH)
                u_ref[r, _hs(h)] = u[rows]
                w_ref[r, _hs(h)] = w[rows].astype(MM)
                qd_ref[r, _hs(h)] = qd[rows].astype(MM)
                kd_ref[r, _hs(h)] = kd[rows].astype(MM)
                eg_ref[c, h:h + 1, :] = jnp.broadcast_to(egl[j * CH:j * CH + 1, :], (1, 128))

    row = pl.BlockSpec((CBI * CH, D), lambda d, n: (n, 0))
    drow = pl.BlockSpec((None, CBI * CH, D), lambda d, n: (d, n, 0))
    mat = pl.BlockSpec((None, CBI, NG, GR, GR), lambda d, n: (d, n, 0, 0, 0))
    return pl.pallas_call(
        body, name="gdn_intra_fwd", grid=(2, N // CBI),
        in_specs=[row, row, row, pl.BlockSpec((CBI * CH, 128), lambda d, n: (n, 0))],
        out_specs=[drow] * 4 + [mat, mat, pl.BlockSpec((None, CBI, NH, 128), lambda d, n: (d, n, 0, 0))],
        out_shape=[_sds((2, T, D))] + [_sds((2, T, D), MM)] * 3 + [_sds((2, N, NG, GR, GR), MM)] * 2
                  + [_sds((2, N, NH, 128))],
        compiler_params=_cp(("parallel", "parallel")))(q, k, v, gb)


def gdn_scan_fwd(u, w, qd, kd, pp, eg):
    T = u.shape[1]
    N = T // CH
    ops = (_raw_nn, _raw_nt, _raw_tn, None)

    cbs = _scan_chunks(N)
    NB = N // cbs

    def body(u_ref, w_ref, qd_ref, kd_ref, pp_ref, eg_ref, o_ref, s0_ref, S):
        d = pl.program_id(0)

        @pl.when(pl.program_id(1) == 0)
        def _():
            S[...] = jnp.zeros_like(S)

        def chunk(c, carry):
            pc = c + d * (cbs - 1 - 2 * c)
            r = pl.ds(pl.multiple_of(pc * CH, CH), CH)
            chains = []
            for g in range(NG):
                heads = range(g * GH, (g + 1) * GH)
                Ss = [S[h] for h in heads]
                for h, Sh in zip(heads, Ss):
                    s0_ref[pc, h] = Sh
                chains.append((Ss, [u_ref[r, _hs(h)] for h in heads], [w_ref[r, _hs(h)] for h in heads], pp_ref[pc, g],
                               [qd_ref[r, _hs(h)] for h in heads], [kd_ref[r, _hs(h)] for h in heads],
                               [eg_ref[pc, h:h + 1, :] for h in heads]))
            for g, (o, S2) in enumerate(_scan_groups(chains, ops)):
                for j, h in enumerate(range(g * GH, (g + 1) * GH)):
                    o_ref[r, _hs(h)] = o[j * CH:(j + 1) * CH]
                    S[h] = S2[j]
            return carry

        lax.fori_loop(0, cbs, chunk, 0)

    bidx = lambda d, n: n + d * (NB - 1 - 2 * n)
    drow = pl.BlockSpec((None, cbs * CH, D), lambda d, n: (d, bidx(d, n), 0))
    mat = pl.BlockSpec((None, cbs, NG, GR, GR), lambda d, n: (d, bidx(d, n), 0, 0, 0))
    return pl.pallas_call(
        body, name="gdn_scan_fwd", grid=(2, NB),
        in_specs=[drow] * 4 + [mat, pl.BlockSpec((None, cbs, NH, 128), lambda d, n: (d, bidx(d, n), 0, 0))],
        out_specs=[drow, pl.BlockSpec((None, cbs, NH, HD, HD), lambda d, n: (d, bidx(d, n), 0, 0, 0))],
        out_shape=[_sds((2, T, D)), _sds((2, N, NH, HD, HD))],
        scratch_shapes=[pltpu.VMEM((NH, HD, HD), F32)],
        compiler_params=_cp(("arbitrary", "arbitrary")))(u, w, qd, kd, pp, eg)


def gdn_scan_bwd(u, w, qd, kd, pp, eg, s0, do):
    T = u.shape[1]
    N = T // CH
    ops = _make_vjp_ops()
    cbs = _scan_chunks(N)
    NB = N // cbs

    def body(u_ref, w_ref, qd_ref, kd_ref, pp_ref, eg_ref, s0_ref, do_ref,
             du_ref, dw_ref, dqd_ref, dkd_ref, dpp_ref, deg_ref, dS):
        d = pl.program_id(0)

        @pl.when(pl.program_id(1) == 0)
        def _():
            dS[...] = jnp.zeros_like(dS)

        def chunk(c, carry):
            pc = (cbs - 1 - c) + d * (2 * c - (cbs - 1))
            r = pl.ds(pl.multiple_of(pc * CH, CH), CH)
            chains, cts = [], []
            for g in range(NG):
                heads = range(g * GH, (g + 1) * GH)
                chains.append(([s0_ref[pc, h] for h in heads], [u_ref[r, _hs(h)] for h in heads],
                               [w_ref[r, _hs(h)].astype(F32) for h in heads], pp_ref[pc, g].astype(F32),
                               [qd_ref[r, _hs(h)].astype(F32) for h in heads],
                               [kd_ref[r, _hs(h)].astype(F32) for h in heads], [eg_ref[pc, h:h + 1, :] for h in heads]))
                cts.append((jnp.concatenate([do_ref[r, _hs(h)].astype(F32) for h in heads], axis=0),
                            [dS[h] for h in heads]))
            _, vjp = jax.vjp(lambda ch: _scan_groups(ch, ops), chains)
            (dchains,) = vjp(cts)
            for g, (dSs, dus, dws, dP, dqds, dkds, degs) in enumerate(dchains):
                dpp_ref[pc, g] = dP
                for j, h in enumerate(range(g * GH, (g + 1) * GH)):
                    dS[h] = dSs[j]
                    du_ref[r, _hs(h)] = dus[j].astype(MM)
                    dw_ref[r, _hs(h)] = dws[j].astype(MM)
                    dqd_ref[r, _hs(h)], dkd_ref[r, _hs(h)] = dqds[j], dkds[j]
                    deg_ref[pc, h:h + 1, :] = degs[j]
            return carry

        lax.fori_loop(0, cbs, chunk, 0)

    bidx = lambda d, n: (NB - 1 - n) + d * (2 * n - (NB - 1))
    drow = pl.BlockSpec((None, cbs * CH, D), lambda d, n: (d, bidx(d, n), 0))
    erow = pl.BlockSpec((None, cbs, NH, 128), lambda d, n: (d, bidx(d, n), 0, 0))
    mat = pl.BlockSpec((None, cbs, NG, GR, GR), lambda d, n: (d, bidx(d, n), 0, 0, 0))
    return pl.pallas_call(
        body, name="gdn_scan_bwd", grid=(2, NB),
        in_specs=[drow] * 4 + [mat, erow, pl.BlockSpec((None, cbs, NH, HD, HD), lambda d, n: (d, bidx(d, n), 0, 0, 0)),
                               pl.BlockSpec((cbs * CH, D), lambda d, n: (bidx(d, n), 0))],
        out_specs=[drow] * 4 + [mat, erow],
        out_shape=[_sds((2, T, D), MM)] * 2 + [_sds((2, T, D))] * 2 + [_sds((2, N, NG, GR, GR))]
                  + [_sds((2, N, NH, 128))],
        scratch_shapes=[pltpu.VMEM((NH, HD, HD), F32)],
        compiler_params=_cp(("arbitrary", "arbitrary")))(u, w, qd, kd, pp, eg, s0, do)


def gdn_intra_bwd(q, k, v, gb, ys, du, dw, dqd, dkd, dpp, deg):
    T = q.shape[0]
    N = T // CH
    nn, nt, tn, inv_saved = _make_vjp_ops()

    def body(q_ref, k_ref, v_ref, gb_ref, ys_ref, du_ref, dw_ref, dqd_ref, dkd_ref, dpp_ref, deg_ref,
             dq_ref, dk_ref, dv_ref, dgb_ref):
        rev = pl.program_id(0) == 1
        gh, rows = 2 * GH, 2 * GR
        masks = _group_masks(rev, rows)
        lane = lax.broadcasted_iota(jnp.int32, (CH, 128), 1)
        grow = lax.broadcasted_iota(jnp.int32, (rows, 1), 0)
        where = [(c, g) for c in range(CBB) for g in range(NG // 2)]
        pair = lambda ref, c, g: _blockdiag(ref[c, 2 * g].astype(F32), ref[c, 2 * g + 1].astype(F32))
        chains = [_load_chain(q_ref, k_ref, v_ref, gb_ref, c, g, rev, gh) for c, g in where]
        Ys = [pair(ys_ref, c, g) for c, g in where]
        inv_all = lambda As: [inv_saved(A, Y) for A, Y in zip(As, Ys)]
        _, vjp = jax.vjp(lambda ch: _intra_groups(ch, *masks, (nn, nt, tn), inv_all), chains)
        cts = []
        for c, g in where:
            r = slice(c * CH, (c + 1) * CH)
            heads = range(g * gh, (g + 1) * gh)
            stack = lambda ref: jnp.concatenate([ref[r, _hs(h)].astype(F32) for h in heads], axis=0)
            degl = jnp.zeros((rows, 1), F32)
            for j, h in enumerate(heads):
                degl = degl + jnp.where(grow == j * CH, jnp.sum(deg_ref[c, h:h + 1, :], axis=1, keepdims=True), 0.0)
            cts.append(((stack(du_ref), stack(dw_ref), pair(dpp_ref, c, g), stack(dqd_ref), stack(dkd_ref), degl),
                        jnp.zeros((rows, rows), F32)))
        (dchains,) = vjp(cts)
        dgbs = [jnp.zeros((CH, 128), F32) for _ in range(CBB)]
        for (c, g), (dqs, dks, dvs, dgs, dbs) in zip(where, dchains):
            r = slice(c * CH, (c + 1) * CH)
            for j, h in enumerate(range(g * gh, (g + 1) * gh)):
                dq_ref[r, _hs(h)], dk_ref[r, _hs(h)], dv_ref[r, _hs(h)] = dqs[j], dks[j], dvs[j]
                glane = jnp.where(rev, NH + h, h)
                dgbs[c] = dgbs[c] + jnp.where(lane == glane, dgs[j], 0.0) + jnp.where(lane == glane + 2 * NH, dbs[j], 0.0)
        for c in range(CBB):
            dgb_ref[c * CH:(c + 1) * CH, :] = dgbs[c]

    row = pl.BlockSpec((CBB * CH, D), lambda d, n: (n, 0))
    drow = pl.BlockSpec((None, CBB * CH, D), lambda d, n: (d, n, 0))
    mat = pl.BlockSpec((None, CBB, NG, GR, GR), lambda d, n: (d, n, 0, 0, 0))
    return pl.pallas_call(
        body, name="gdn_intra_bwd", grid=(2, N // CBB),
        in_specs=[row, row, row, pl.BlockSpec((CBB * CH, 128), lambda d, n: (n, 0)), mat, drow, drow, drow, drow, mat]
                 + [pl.BlockSpec((None, CBB, NH, 128), lambda d, n: (d, n, 0, 0))],
        out_specs=[drow, drow, drow, pl.BlockSpec((None, CBB * CH, 128), lambda d, n: (d, n, 0))],
        out_shape=[_sds((2, T, D))] * 3 + [_sds((2, T, 128))],
        compiler_params=_cp(("parallel", "parallel")))(q, k, v, gb, ys, du, dw, dqd, dkd, dpp, deg)


def _post_rows(o2a, o2b, z, nw):
    o = o2a + o2b
    outs = []
    for h in range(NH):
        s = slice(h * HD, (h + 1) * HD)
        oh = o[:, s]
        outs.append(oh * lax.rsqrt(jnp.mean(oh * oh, axis=-1, keepdims=True) + RMS_EPS) * nw * _silu(z[:, s]))
    return jnp.concatenate(outs, axis=1)


def post_fwd(o2, pm, nw, *, bt=512):
    T = pm.shape[0]
    bt = min(bt, T)

    def body(o_ref, z_ref, nw_ref, og_ref):
        og_ref[...] = _post_rows(o_ref[0], o_ref[1], z_ref[...], nw_ref[0:1, :]).astype(og_ref.dtype)

    return pl.pallas_call(
        body, name="post_fwd", grid=(T // bt,),
        in_specs=[pl.BlockSpec((2, bt, D), lambda i: (0, i, 0)), pl.BlockSpec((bt, D), lambda i: (i, CZ)),
                  pl.BlockSpec((8, 128), lambda i: (0, 0))],
        out_specs=pl.BlockSpec((bt, D), lambda i: (i, 0)), out_shape=_sds((T, D), MM),
        compiler_params=_cp(("parallel",)))(o2, pm, nw)


def post_bwd(o2, pm, nw, dog, dpm, *, bt=512):
    T = pm.shape[0]
    bt = min(bt, T)

    def body(o_ref, z_ref, nw_ref, dog_ref, _alias, do_ref, dz_ref, dnw_ref):
        @pl.when(pl.program_id(0) == 0)
        def _():
            dnw_ref[...] = jnp.zeros_like(dnw_ref)

        _, vjp = jax.vjp(_post_rows, o_ref[0], o_ref[1], z_ref[...], nw_ref[0:1, :])
        doa, _unused, dz, dnw = vjp(dog_ref[...])
        do_ref[...] = doa.astype(do_ref.dtype)
        dz_ref[...] = dz.astype(dz_ref.dtype)
        row8 = lax.broadcasted_iota(jnp.int32, (8, 128), 0)
        dnw_ref[...] += jnp.where(row8 == 0, dnw, 0.0)

    in_specs = [pl.BlockSpec((2, bt, D), lambda i: (0, i, 0)), pl.BlockSpec((bt, D), lambda i: (i, CZ)),
                pl.BlockSpec((8, 128), lambda i: (0, 0)), pl.BlockSpec((bt, D), lambda i: (i, 0)),
                pl.BlockSpec(memory_space=pl.ANY)]
    return pl.pallas_call(
        body, name="post_bwd", grid=(T // bt,), in_specs=in_specs,
        out_specs=[pl.BlockSpec((bt, D), lambda i: (i, 0)), pl.BlockSpec((bt, D), lambda i: (i, CZ)),
                   pl.BlockSpec((8, 128), lambda i: (0, 0))],
        out_shape=[_sds((T, D), MM), _sds(dpm.shape, dpm.dtype), _sds((8, 128))], input_output_aliases={4: 1},
        compiler_params=_cp(("arbitrary",)))(o2, pm, nw, dog, dpm)


def merge_fwd(ya, yb, pm, *, bt=512):
    T = pm.shape[0]
    bt = min(bt, T)

    def body(ya_ref, yb_ref, ga_ref, gb_ref, o_ref):
        o_ref[...] = (_sigmoid(ga_ref[...]) * ya_ref[...] + _sigmoid(gb_ref[...]) * yb_ref[...]).astype(o_ref.dtype)

    row = pl.BlockSpec((bt, D), lambda i: (i, 0))
    return pl.pallas_call(
        body, name="merge_fwd", grid=(T // bt,),
        in_specs=[row, row, pl.BlockSpec((bt, D), lambda i: (i, CGA)), pl.BlockSpec((bt, D), lambda i: (i, CGB))],
        out_specs=row, out_shape=_sds((T, D), MM), compiler_params=_cp(("parallel",)))(ya, yb, pm, pm)


def merge_bwd(ya, yb, pm, dmix, *, bt=512):
    T = pm.shape[0]
    bt = min(bt, T)

    def body(ya_ref, yb_ref, ga_ref, gb_ref, dm_ref, dya_ref, dyb_ref, dg_ref):
        dm = dm_ref[...]
        sa, sb = _sigmoid(ga_ref[...]), _sigmoid(gb_ref[...])
        dya_ref[...] = (dm * sa).astype(dya_ref.dtype)
        dyb_ref[...] = (dm * sb).astype(dyb_ref.dtype)
        dg_ref[:, 0:D] = (dm * ya_ref[...] * sa * (1.0 - sa)).astype(dg_ref.dtype)
        dg_ref[:, D:2 * D] = (dm * yb_ref[...] * sb * (1.0 - sb)).astype(dg_ref.dtype)

    row = pl.BlockSpec((bt, D), lambda i: (i, 0))
    return pl.pallas_call(
        body, name="merge_bwd", grid=(T // bt,),
        in_specs=[row, row, pl.BlockSpec((bt, D), lambda i: (i, CGA)), pl.BlockSpec((bt, D), lambda i: (i, CGB)), row],
        out_specs=[row, row, pl.BlockSpec((bt, 2 * D), lambda i: (i, CGA // 2))],
        out_shape=[_sds((T, D), MM), _sds((T, D), MM), _sds((T, NMAIN), MM)],
        compiler_params=_cp(("parallel",)))(ya, yb, pm, pm, dmix)


def _ln_rows(x, y, bias, g, b):
    r = ALPHA * x + y + bias
    mu = jnp.mean(r, axis=-1, keepdims=True)
    var = jnp.mean(jnp.square(r - mu), axis=-1, keepdims=True)
    return (r - mu) * lax.rsqrt(var + LN_EPS) * g + b


def ln_fwd(x, y, p, *, name, bt=512):
    T = x.shape[0]
    bt = min(bt, T)

    def body(x_ref, y_ref, p_ref, o_ref, ob_ref):
        r = _ln_rows(x_ref[...], y_ref[...], p_ref[0:1, :], p_ref[1:2, :], p_ref[2:3, :])
        o_ref[...] = r
        ob_ref[...] = r.astype(ob_ref.dtype)

    row = pl.BlockSpec((bt, D), lambda i: (i, 0))
    return pl.pallas_call(
        body, name=name, grid=(T // bt,), in_specs=[row, row, pl.BlockSpec((8, D), lambda i: (0, 0))],
        out_specs=[row, row], out_shape=[_sds((T, D)), _sds((T, D), MM)],
        compiler_params=_cp(("parallel",)))(x, y, p)


def ln_bwd(x, y, p, ct, ct2=None, *, name, bt=512):
    T = x.shape[0]
    bt = min(bt, T)

    def body(*refs):
        it = iter(refs)
        x_ref, y_ref, p_ref, c_ref = next(it), next(it), next(it), next(it)
        c2_ref = next(it) if ct2 is not None else None
        dxa_ref, dr_ref, dp_ref = next(it), next(it), next(it)

        @pl.when(pl.program_id(0) == 0)
        def _():
            dp_ref[...] = jnp.zeros_like(dp_ref)

        c = c_ref[...]
        if c2_ref is not None:
            c = c + c2_ref[...]
        _, vjp = jax.vjp(_ln_rows, x_ref[...], y_ref[...], p_ref[0:1, :], p_ref[1:2, :], p_ref[2:3, :])
        _dx, dy, dbias, dg, db = vjp(c)
        dxa_ref[...] = ALPHA * dy
        dr_ref[...] = dy.astype(dr_ref.dtype)
        row8 = lax.broadcasted_iota(jnp.int32, (8, D), 0)
        dp_ref[...] += jnp.where(row8 == 0, dbias, jnp.where(row8 == 1, dg, jnp.where(row8 == 2, db, 0.0)))

    row = pl.BlockSpec((bt, D), lambda i: (i, 0))
    in_specs = [row, row, pl.BlockSpec((8, D), lambda i: (0, 0)), row] + ([row] if ct2 is not None else [])
    args = [x, y, p, ct] + ([ct2] if ct2 is not None else [])
    return pl.pallas_call(
        body, name=name, grid=(T // bt,), in_specs=in_specs,
        out_specs=[row, row, pl.BlockSpec((8, D), lambda i: (0, 0))],
        out_shape=[_sds((T, D)), _sds((T, D), MM), _sds((8, D))],
        compiler_params=_cp(("arbitrary",)))(*args)


def loss_fwd_bwd(xl, target, *, bt=512):
    T = xl.shape[0]
    bt = min(bt, T)

    def body(x_ref, t_ref, l_ref, d_ref):
        @pl.when(pl.program_id(0) == 0)
        def _():
            l_ref[...] = jnp.zeros_like(l_ref)

        e = x_ref[...] - t_ref[...]
        d_ref[...] = e * (1.0 / D)
        l_ref[...] += 0.5 * jnp.sum(jnp.mean(e * e, axis=-1, keepdims=True), axis=0, keepdims=True)

    row = pl.BlockSpec((bt, D), lambda i: (i, 0))
    return pl.pallas_call(
        body, name="loss", grid=(T // bt,), in_specs=[row, row],
        out_specs=[pl.BlockSpec((8, 128), lambda i: (0, 0)), row], out_shape=[_sds((8, 128)), _sds((T, D))],
        compiler_params=_cp(("arbitrary",)))(xl, target)


def _row_tile(R, Cc, elems=1 << 18):
    if R * Cc <= elems:
        return R
    tr = 8
    while tr * 2 * Cc <= elems and R % (tr * 2) == 0:
        tr *= 2
    return tr


def adam(w, m, v, ga, gb=None, *, name):
    R, Cc = w.shape
    tr = _row_tile(R, Cc)

    def body(*refs):
        it = iter(refs)
        w_ref, m_ref, v_ref, a_ref = next(it), next(it), next(it), next(it)
        b_ref = next(it) if gb is not None else None
        g_ref, d_ref, mo_ref, vo_ref = next(it), next(it), next(it), next(it)
        g = a_ref[...]
        if b_ref is not None:
            g = g + b_ref[...]
        m2 = B1 * m_ref[...] + (1.0 - B1) * g
        v2 = B2 * v_ref[...] + (1.0 - B2) * jnp.square(g)
        m_hat = m2 / (1.0 - B1 ** STEP)
        v_hat = v2 / (1.0 - B2 ** STEP)
        g_ref[...] = g
        d_ref[...] = -LR * (m_hat / (jnp.sqrt(v_hat) + EPS) + WD * w_ref[...])
        mo_ref[...] = m2
        vo_ref[...] = v2

    blk = pl.BlockSpec((tr, Cc), lambda i: (i, 0))
    args = [w, m, v, ga] + ([gb] if gb is not None else [])
    return pl.pallas_call(
        body, name=name, grid=(R // tr,), in_specs=[blk] * len(args), out_specs=[blk] * 4,
        out_shape=[_sds((R, Cc))] * 4, compiler_params=_cp(("parallel",)))(*args)


def sum4(parts, *, name):
    _, R, Cc = parts.shape
    tr = _row_tile(R, Cc)

    def body(p_ref, out_ref):
        f = lambda t: t.astype(F32)
        out_ref[...] = ((f(p_ref[0]) + f(p_ref[1])) + f(p_ref[2])) + f(p_ref[3])

    return pl.pallas_call(
        body, name=name, grid=(R // tr,), in_specs=[pl.BlockSpec((4, tr, Cc), lambda i: (0, i, 0))],
        out_specs=pl.BlockSpec((tr, Cc), lambda i: (i, 0)), out_shape=_sds((R, Cc)),
        compiler_params=_cp(("parallel",)))(parts)


def _place():
    return lax.axis_index("x"), lax.axis_index("y"), lax.axis_index("c")


def _other_chips(x, y):
    return [(1 - x, y), (x, 1 - y), (1 - x, 1 - y)]


_ANY = pl.BlockSpec(memory_space=pl.ANY)


def allgather_xy(arrs):
    n = len(arrs)
    axes = [0 if a.shape[0] % 2 == 0 else 1 for a in arrs]
    halves = [a.shape[ax] // 2 for a, ax in zip(arrs, axes)]

    def half_of(ref, a, which):
        part = pl.ds(which * halves[a], halves[a])
        return ref.at[part] if axes[a] == 0 else ref.at[:, part]

    def body(*refs):
        ins, outs = refs[:n], refs[n:2 * n]
        send, recv, fsend, frecv, loc = refs[2 * n:]
        x, y, c = _place()
        me = 2 * x + y
        peers = _other_chips(x, y)
        local = [pltpu.make_async_copy(ins[a], outs[a].at[me], loc.at[a]) for a in range(n)]
        for cp in local:
            cp.start()

        def over_ici(a, j, block, src=None):
            px, py = peers[j]
            dst = half_of(outs[a].at[block], a, c)
            return pltpu.make_async_remote_copy(
                src_ref=dst if src is None else half_of(src, a, c), dst_ref=dst, send_sem=send.at[3 * a + j],
                recv_sem=recv.at[3 * a + j], device_id=(px, py, c), device_id_type=MESH)

        def over_d2d(a, j, half):
            px, py = peers[j]
            rows = half_of(outs[a].at[2 * px + py], a, half)
            return pltpu.make_async_remote_copy(
                src_ref=rows, dst_ref=rows, send_sem=fsend.at[3 * a + j], recv_sem=frecv.at[3 * a + j],
                device_id=(x, y, 1 - c), device_id_type=MESH)

        sends = [over_ici(a, j, me, src=ins[a]) for a in range(n) for j in range(3)]
        for cp in sends:
            cp.start()
        passed = []
        for a in range(n):
            for j, (px, py) in enumerate(peers):
                over_ici(a, j, 2 * px + py).wait_recv()
                passed.append(over_d2d(a, j, c))
                passed[-1].start()
        for a in range(n):
            for j in range(3):
                over_d2d(a, j, 1 - c).wait_recv()
        for cp in sends + passed:
            cp.wait_send()
        for cp in local:
            cp.wait()

    return pl.pallas_call(
        body, name="allgather_xy", in_specs=[_ANY] * n, out_specs=[_ANY] * n,
        out_shape=[_sds((4,) + a.shape, a.dtype) for a in arrs],
        scratch_shapes=[pltpu.SemaphoreType.DMA((3 * n,))] * 4 + [pltpu.SemaphoreType.DMA((n,))],
        compiler_params=pltpu.CompilerParams(has_side_effects=True))(*arrs)


_HBM = pl.BlockSpec(memory_space=pltpu.HBM)
_SEM = pl.BlockSpec(memory_space=pltpu.SEMAPHORE)
_EFFECT = pltpu.SideEffectType.DATAFLOW_SIDE_EFFECTING


def _xy_copy(kind, layer, src, land, send, recv, a, j, c, arriving):
    x, y = lax.axis_index("x"), lax.axis_index("y")
    px, py = ((1 - x, y), (x, 1 - y), (1 - x, 1 - y), (x, y))[j]
    if kind == "gather":
        dst = land.at[2 * px + py] if arriving else land.at[2 * x + y]
        src_view = dst if arriving else src
    else:
        dst = land.at[j, layer]
        src_view = src.at[2 * px + py]
    return pltpu.make_async_remote_copy(src_ref=src_view, dst_ref=dst, send_sem=send.at[4 * a + j],
                                        recv_sem=recv.at[4 * a + j], device_id=(px, py, c), device_id_type=MESH)


def xy_start(kind, name, arrs, after, zones=None, layer=0):
    n = len(arrs)
    if zones is None:
        shape = lambda a: ((4,) + a.shape) if kind == "gather" else ((4, DEPTH) + a.shape[1:])
        zones = [lax.empty(shape(a), a.dtype) for a in arrs]

    def body(*refs):
        ins, lands = refs[:n], refs[n:2 * n]
        send, recv = refs[2 * n + 1], refs[2 * n + 2]
        token = refs[-1]
        c = lax.axis_index("c")
        for a in range(n):
            for j in range(4):
                _xy_copy(kind, layer, ins[a], lands[a], send, recv, a, j, c, False).start()
        token[...] = jnp.zeros_like(token)

    hbm = lambda v: pltpu.with_memory_space_constraint(v, pltpu.HBM)
    outs = pl.pallas_call(
        body, name=name,
        out_shape=(pltpu.SemaphoreType.DMA((4 * n,)), pltpu.SemaphoreType.DMA((4 * n,)),
                   *[pltpu.HBM(a.shape, a.dtype) for a in arrs], *[pltpu.HBM(z.shape, z.dtype) for z in zones],
                   _sds((8, 128))),
        in_specs=[_HBM] * (2 * n) + [_ANY],
        out_specs=(_SEM, _SEM, *[_HBM] * (2 * n), pl.BlockSpec(memory_space=pltpu.VMEM)),
        input_output_aliases={i: 2 + i for i in range(2 * n)},
        compiler_params=pltpu.CompilerParams(has_side_effects=_EFFECT))(
            *[hbm(a) for a in arrs], *[hbm(z) for z in zones], after)
    return (kind, layer, outs[0], outs[1], list(outs[2:2 + n])), list(outs[2 + n:2 + 2 * n]), outs[-1]


def xy_wait(name, handles, zones, after):
    kind, layer, send_sems, recv_sems, srcs = handles
    n = len(srcs)

    def body(*refs):
        ins, lands = refs[:n], refs[n:2 * n]
        send, recv = refs[2 * n], refs[2 * n + 1]
        c = lax.axis_index("c")
        for a in range(n):
            for j in range(4):
                _xy_copy(kind, layer, ins[a], lands[a], send, recv, a, j, c, False).wait_send()
                _xy_copy(kind, layer, ins[a], lands[a], send, recv, a, j, c, True).wait_recv()

    outs = pl.pallas_call(
        body, name=name,
        out_shape=tuple(pltpu.HBM(v.shape, v.dtype) for v in srcs + zones),
        in_specs=[_HBM] * (2 * n) + [_SEM, _SEM, _ANY], out_specs=tuple([_HBM] * (2 * n)),
        input_output_aliases={i: i for i in range(2 * n)},
        compiler_params=pltpu.CompilerParams(has_side_effects=_EFFECT))(*srcs, *zones, send_sems, recv_sems, after)
    return list(outs[n:])


def swap_c(arrs):
    n = len(arrs)

    def body(*refs):
        ins, outs = refs[:n], refs[n:2 * n]
        send, recv = refs[2 * n:]
        x, y, c = _place()
        cps = [pltpu.make_async_remote_copy(src_ref=ins[a], dst_ref=outs[a], send_sem=send.at[a], recv_sem=recv.at[a],
                                            device_id=(x, y, 1 - c), device_id_type=MESH) for a in range(n)]
        for cp in cps:
            cp.start()
        for cp in cps:
            cp.wait_recv()
        for cp in cps:
            cp.wait_send()

    return pl.pallas_call(
        body, name="swap_c", in_specs=[_ANY] * n, out_specs=[_ANY] * n, out_shape=[_sds(a.shape, a.dtype) for a in arrs],
        scratch_shapes=[pltpu.SemaphoreType.DMA((n,)), pltpu.SemaphoreType.DMA((n,))],
        compiler_params=pltpu.CompilerParams(has_side_effects=True))(*arrs)


def allreduce_small(v):
    R = v.shape[0]

    def body(v_ref, o_ref, buf, send, recv):
        x, y, c = _place()
        me = 4 * x + 2 * y + c
        buf[0] = v_ref[...]

        def cp(k):
            dx, dy, dc = (k >> 2) & 1, (k >> 1) & 1, k & 1
            return pltpu.make_async_remote_copy(
                src_ref=v_ref, dst_ref=buf.at[k], send_sem=send.at[k - 1], recv_sem=recv.at[k - 1],
                device_id=(x ^ dx, y ^ dy, c ^ dc), device_id_type=MESH)

        cps = [cp(k) for k in range(1, 8)]
        for t in cps:
            t.start()
        for t in cps:
            t.wait_recv()
        acc = buf[me]
        for dev in range(1, 8):
            acc = acc + buf[jnp.bitwise_xor(me, dev)]
        o_ref[...] = acc
        for t in cps:
            t.wait_send()

    vm = pl.BlockSpec(memory_space=pltpu.VMEM)
    return pl.pallas_call(
        body, name="allreduce_small", in_specs=[vm], out_specs=vm, out_shape=_sds((R, 128)),
        scratch_shapes=[pltpu.VMEM((8, R, 128), F32), pltpu.SemaphoreType.DMA((7,)), pltpu.SemaphoreType.DMA((7,))],
        compiler_params=pltpu.CompilerParams(has_side_effects=True, vmem_limit_bytes=VMEM_LIMIT))(v)


def _rows8(*rows):
    n = rows[0].shape[-1]
    t = jnp.stack([r.reshape(n).astype(F32) for r in rows])
    return jnp.pad(t, ((0, 8 - len(rows)), (0, 0)))


def _lanes128(a):
    f = a.reshape(-1).astype(F32)
    return jnp.pad(f, (0, 128 - f.shape[0]))


def _layer_fwd(x, xb, W):
    pm = mm_nn(xb, W["w_main"], name="proj_main", tm=2048)
    pab = mm_nn(xb, W["w_ab"], name="proj_ab")
    q, k, v, gb = pre_qkv_fwd(pm, pab, W["cw"], W["gp"])
    sc = pre_sc_fwd(pm, W["csc"])
    u, w, qd, kd, pp, ys, eg = gdn_intra_fwd(q, k, v, gb)
    o2, s0 = gdn_scan_fwd(u, w, qd, kd, pp, eg)
    og = post_fwd(o2, pm, W["nw"])
    ya = mm_nn(og, W["w_og"], name="proj_og")
    yb = mm_nn(sc, W["w_osc"], name="proj_osc")
    mixed = merge_fwd(ya, yb, pm)
    out = mm_nn(mixed, W["w_out"], name="proj_out")
    x1, x1b = ln_fwd(x, out, W["ln1"], name="ln1_fwd")
    hfac, h = mm_nn(x1b, W["w_up"], bias=W["b_up"], relu2=True, out_dtype=MM, name="mlp_up", tm=2048)
    dn = mm_nn(h, W["w_down"], name="mlp_down", tm=2048)
    x2, x2b = ln_fwd(x1, dn, W["ln2"], name="ln2_fwd")
    saved = dict(x=x, xb=xb, pm=pm, pab=pab, q=q, k=k, v=v, gb=gb, sc=sc, o2=o2, s0=s0, og=og, ya=ya, yb=yb,
                 u=u, w=w, qd=qd, kd=kd, pp=pp, ys=ys, eg=eg,
                 mixed=mixed, out=out, x1=x1, x1b=x1b, hfac=hfac, h=h, dn=dn)
    return x2, x2b, saved


def _layer_bwd(ct, W, S, early=None):
    dxa2, dr2b, dp2 = ln_bwd(S["x1"], S["dn"], W["ln2"], ct, name="ln2_bwd")
    g_down = mm_tn(S["h"], dr2b, name="dw_down", tm=2048)
    dhpre, db_up = mm_nt(dr2b, W["w_down"], dact=S["hfac"], out_dtype=MM, name="mlp_down_bwd", tm=2048)
    g_up = mm_tn(S["x1b"], dhpre, name="dw_up", tn=2048)
    dx1 = mm_nt(dhpre, W["w_up"], add=dxa2, name="mlp_up_bwd")
    dxa1, dr1b, dp1 = ln_bwd(S["x"], S["out"], W["ln1"], dx1, name="ln1_bwd")
    g_out = mm_tn(S["mixed"], dr1b, name="dw_out")
    dmix = mm_nt(dr1b, W["w_out"], name="proj_out_bwd")
    dya, dyb, dpm = merge_bwd(S["ya"], S["yb"], S["pm"], dmix)
    g_og = mm_tn(S["og"], dya, name="dw_og")
    g_osc = mm_tn(S["sc"], dyb, name="dw_osc")
    dog = mm_nt(dya, W["w_og"], name="proj_og_bwd")
    dsc = mm_nt(dyb, W["w_osc"], name="proj_osc_bwd")
    token = None if early is None else early(dict(w_o_gdn=g_og, w_o_sc=g_osc, w_out=g_out, w_up=g_up, w_down=g_down))
    nw = W["nw"] if token is None else W["nw"] + token[0, 0]
    do, dpm, dnw = post_bwd(S["o2"], S["pm"], nw, dog, dpm)
    du, dw, dqd, dkd, dpp, deg = gdn_scan_bwd(S["u"], S["w"], S["qd"], S["kd"], S["pp"], S["eg"], S["s0"], do)
    dq2, dk2, dv2, dgb2 = gdn_intra_bwd(S["q"], S["k"], S["v"], S["gb"], S["ys"], du, dw, dqd, dkd, dpp, deg)
    dpm, dpab, dcw, dgp = pre_qkv_bwd(S["pm"], S["pab"], W["cw"], W["gp"], dq2, dk2, dv2, dgb2, dpm)
    dpm, dcsc = pre_sc_bwd(S["pm"], W["csc"], dsc, dpm)
    g_main = mm_tn(S["xb"], dpm, name="dw_main", tn=NMAIN // 4)
    g_ab = mm_tn(S["xb"], dpab, name="dw_ab")
    t = mm_nt(dpab, W["w_ab"], add=dxa1, name="proj_ab_bwd")
    dx = mm_nt(dpm, W["w_main"], add=t, name="proj_main_bwd")
    g_in = jnp.concatenate([g_main[:, :3 * D], g_main[:, 8 * D:], g_ab[:, :4 * NH], g_main[:, 3 * D:8 * D]], axis=1)
    grads = dict(
        w_in=g_in, w_o_gdn=g_og, w_o_sc=g_osc, w_out=g_out, w_up=g_up, w_down=g_down,
        conv_qkv=dcw[:3], conv_sc=dcsc[:3], a_log=dgp[0, :2 * NH].reshape(2, NH), dt_bias=dgp[1, :2 * NH].reshape(2, NH),
        gdn_norm_w=dnw[0], ln1_g=dp1[1], ln1_b=dp1[2], b_up=db_up[0], b_down=dp2[0], ln2_g=dp2[1], ln2_b=dp2[2])
    return dx, grads


def _layer_weights(l, full, i, conv, a_log, dt_bias, gdn_norm_w, ln1_g, ln1_b, b_up, b_down, ln2_g, ln2_b):
    w_in = full["w_in"][i]
    w_main = jnp.concatenate([w_in[:, :3 * D], w_in[:, 4 * D + 4 * NH:], w_in[:, 3 * D:4 * D]], axis=1)
    w_ab = jnp.pad(w_in[:, 4 * D:4 * D + 4 * NH], ((0, 0), (0, 128 - 4 * NH)))
    return dict(
        w_main=w_main, w_ab=w_ab, w_og=full["w_o_gdn"][i], w_osc=full["w_o_sc"][i], w_out=full["w_out"][i],
        w_up=full["w_up"][i], w_down=full["w_down"][i],
        cw=jnp.pad(conv["conv_qkv"][l].astype(F32), ((0, 5), (0, 0))),
        csc=jnp.pad(conv["conv_sc"][l].astype(F32), ((0, 5), (0, 0))),
        gp=_rows8(_lanes128(a_log[l]), _lanes128(dt_bias[l])), nw=_rows8(gdn_norm_w[l]),
        ln1=_rows8(jnp.zeros((D,), F32), ln1_g[l], ln1_b[l]), ln2=_rows8(b_down[l], ln2_g[l], ln2_b[l]),
        b_up=b_up[l].reshape(1, DFF).astype(F32))


def local_step(xs, target, weights_of, after_bwd, mid_bwd=None):
    x, xb = xs, xs.astype(MM)
    Ws, saved = [], []
    for l in range(DEPTH):
        Ws.append(weights_of(l, x))
        x, xb, S = _layer_fwd(x, xb, Ws[l])
        saved.append(S)
    loss_tile, ct = loss_fwd_bwd(x, target)
    token = None
    for l in reversed(range(DEPTH)):
        W = Ws[l] if token is None else dict(Ws[l], ln2=Ws[l]["ln2"] + token[0, 0])
        ct, grads = _layer_bwd(ct, W, saved[l], None if mid_bwd is None else (lambda part, l=l: mid_bwd(l, part)))
        token = after_bwd(l, grads)
    return loss_tile, ct


EARLY = 1
BIG = ("w_in", "w_o_gdn", "w_o_sc", "w_out", "w_up", "w_down")
SMALL = ("conv_qkv", "a_log", "dt_bias", "gdn_norm_w", "conv_sc", "ln1_g", "ln1_b", "b_up", "b_down", "ln2_g", "ln2_b")
ORDER = ("w_in", "conv_qkv", "a_log", "dt_bias", "gdn_norm_w", "w_o_gdn", "conv_sc", "w_o_sc", "w_out", "ln1_g",
         "ln1_b", "w_up", "b_up", "w_down", "b_down", "ln2_g", "ln2_b")


def _pack(arrs):
    flat = jnp.concatenate([a.reshape(-1).astype(F32) for a in arrs])
    n = flat.shape[0]
    rows = -(-n // 1024) * 8
    return jnp.pad(flat, (0, rows * 128 - n)).reshape(rows, 128)


def _unpack(buf, like):
    flat = buf.reshape(-1)
    out, o = [], 0
    for a in like:
        n = 1
        for s in a.shape:
            n *= s
        out.append(flat[o:o + n].reshape(a.shape))
        o += n
    return out


def _gathered(name, g):
    if name in ("w_in", "w_up", "conv_qkv", "conv_sc"):
        t = jnp.moveaxis(g, 0, -2)
        return t.reshape(t.shape[:-2] + (t.shape[-2] * t.shape[-1],))
    t = jnp.moveaxis(g, 0, 1)
    return t.reshape((t.shape[0], t.shape[1] * t.shape[2]) + t.shape[3:])


def _by_chip(name, g):
    if name in ("w_in", "w_up"):
        r, ccols = g.shape
        return jnp.moveaxis(g.reshape(r, 4, ccols // 4), 1, 0)
    return g.reshape((4, g.shape[0] // 4) + g.shape[1:])


def kernel(x, w_in, conv_qkv, a_log, dt_bias, gdn_norm_w, w_o_gdn, conv_sc, w_o_sc, w_out, ln1_g, ln1_b, w_up, b_up, w_down, b_down, ln2_g, ln2_b, loss_target, m_w_in, m_conv_qkv, m_a_log, m_dt_bias, m_gdn_norm_w, m_w_o_gdn, m_conv_sc, m_w_o_sc, m_w_out, m_ln1_g, m_ln1_b, m_w_up, m_b_up, m_w_down, m_b_down, m_ln2_g, m_ln2_b, v_w_in, v_conv_qkv, v_a_log, v_dt_bias, v_gdn_norm_w, v_w_o_gdn, v_conv_sc, v_w_o_sc, v_w_out, v_ln1_g, v_ln1_b, v_w_up, v_b_up, v_w_down, v_b_down, v_ln2_g, v_ln2_b):
    w = dict(w_in=w_in, conv_qkv=conv_qkv, a_log=a_log, dt_bias=dt_bias, gdn_norm_w=gdn_norm_w, w_o_gdn=w_o_gdn,
             conv_sc=conv_sc, w_o_sc=w_o_sc, w_out=w_out, ln1_g=ln1_g, ln1_b=ln1_b, w_up=w_up, b_up=b_up,
             w_down=w_down, b_down=b_down, ln2_g=ln2_g, ln2_b=ln2_b)
    m = dict(w_in=m_w_in, conv_qkv=m_conv_qkv, a_log=m_a_log, dt_bias=m_dt_bias, gdn_norm_w=m_gdn_norm_w,
             w_o_gdn=m_w_o_gdn, conv_sc=m_conv_sc, w_o_sc=m_w_o_sc, w_out=m_w_out, ln1_g=m_ln1_g, ln1_b=m_ln1_b,
             w_up=m_w_up, b_up=m_b_up, w_down=m_w_down, b_down=m_b_down, ln2_g=m_ln2_g, ln2_b=m_ln2_b)
    v = dict(w_in=v_w_in, conv_qkv=v_conv_qkv, a_log=v_a_log, dt_bias=v_dt_bias, gdn_norm_w=v_gdn_norm_w,
             w_o_gdn=v_w_o_gdn, conv_sc=v_conv_sc, w_o_sc=v_w_o_sc, w_out=v_w_out, ln1_g=v_ln1_g, ln1_b=v_ln1_b,
             w_up=v_w_up, b_up=v_b_up, w_down=v_w_down, b_down=v_b_down, ln2_g=v_ln2_g, ln2_b=v_ln2_b)
    chip = 2 * lax.axis_index("x") + lax.axis_index("y")

    names = BIG + ("conv_qkv", "conv_sc")
    blocks = [w[n].astype(MM) if n in BIG else w[n] for n in names]
    got = allgather_xy([b[:EARLY] if n in BIG else b for n, b in zip(names, blocks)])
    early = {n: _gathered(n, g) for n, g in zip(names, got)}
    gather, gather_zones, token = xy_start("gather", "gather_start", [b[EARLY:] for b in blocks[:len(BIG)]],
                                           after=got[0])
    vectors = (a_log, dt_bias, gdn_norm_w, ln1_g, ln1_b, b_up, b_down, ln2_g, ln2_b)
    late, grads = {}, [None] * DEPTH
    scatters = {"mid": [None] * DEPTH, "end": [None] * DEPTH}
    zones = {"mid": None, "end": None}

    def weights_of(l, x_l):
        if l < EARLY:
            return _layer_weights(l, early, l, early, *vectors)
        if not late:
            for n, zone in zip(BIG, xy_wait("gather_wait", gather, gather_zones, after=x_l)):
                late[n] = _gathered(n, zone)
        return _layer_weights(l, late, l - EARLY, early, *vectors)

    def scatter(when, l, g, which):
        scatters[when][l], zones[when], tok = xy_start(
            "scatter", "scatter_%s_start_%d" % (when, l), [_by_chip(n, g[n]).astype(MM) for n in which], after=a_log,
            zones=zones[when], layer=l)
        return tok

    def after_bwd(l, g):
        grads[l] = g
        return scatter("end", l, g, BIG[:1])

    loss_tile, dx = local_step(x[0] + token[0, 0], loss_target[0], weights_of, after_bwd,
                               lambda l, g: scatter("mid", l, g, BIG[1:]))
    loss = lax.psum(loss_tile[0, 0], ("x", "y", "c"))

    for when in ("mid", "end"):
        for l in range(DEPTH):
            zones[when] = xy_wait("scatter_%s_wait_%d" % (when, l), scatters[when][l], zones[when], after=dx)
    part = [sum4(z.reshape(4, -1, z.shape[-1]), name="sum_" + n) for n, z in zip(BIG, zones["end"] + zones["mid"])]
    other = swap_c(part)
    out = {}
    for n, mine, theirs in zip(BIG, part, other):
        cols = mine.shape[-1]
        res = adam(w[n].reshape(-1, cols), m[n].reshape(-1, cols), v[n].reshape(-1, cols), mine, theirs, name="adam_" + n)
        out[n] = [r.reshape(w[n].shape) for r in res]

    stacked = [jnp.stack([grads[l][n] for l in range(DEPTH)]) for n in SMALL]
    summed = _unpack(allreduce_small(_pack(stacked)), stacked)
    gs = []
    for n, g in zip(SMALL, summed):
        if n in ("conv_qkv", "conv_sc"):
            blk = w[n].shape[-1]
            g = lax.dynamic_slice_in_dim(g, chip * blk, blk, axis=2)
        gs.append(g)
    res = adam(_pack([w[n] for n in SMALL]), _pack([m[n] for n in SMALL]), _pack([v[n] for n in SMALL]), _pack(gs),
               name="adam_small")
    for n, parts in zip(SMALL, zip(*[_unpack(r, gs) for r in res])):
        out[n] = list(parts)

    outs = [loss, dx[None]]
    for kind in range(4):
        outs += [out[n][kind] for n in ORDER]
    return tuple(outs)
```

```python
import jax
import jax.numpy as jnp
from jax import lax
from jax.experimental import pallas as pl
from jax.experimental.pallas import tpu as pltpu

F32 = jnp.float32
MM = jnp.bfloat16
HI = lax.Precision.HIGHEST

D = 1024
NH = 8
HD = 128
CH = 64
DFF = 4 * D
DEPTH = 4
LN_EPS = 1e-5
RMS_EPS = 1e-6
L2_EPS = 1e-6
ALPHA = (2 * DEPTH) ** 0.25
LR, B1, B2, EPS, WD, STEP = 0.001, 0.9, 0.999, 1e-08, 0.01, 10

NMAIN = 9 * D
CQ, CK, CV, CSB, CSC, CSX, CGA, CGB, CZ = range(9)
HALO = 8
VMEM_LIMIT = 56 * 1024 * 1024
MESH = pl.DeviceIdType.MESH


def _cp(sem=None, vmem=VMEM_LIMIT):
    return pltpu.CompilerParams(dimension_semantics=sem, vmem_limit_bytes=vmem)


def _sds(shape, dtype=F32):
    return jax.ShapeDtypeStruct(tuple(shape), dtype)


def _accumulate(acc, product, k, nk, finish):
    if nk == 1:
        finish(product())
        return

    @pl.when(k == 0)
    def _():
        acc[...] = jnp.zeros_like(acc)

    acc[...] += product()

    @pl.when(k == nk - 1)
    def _():
        finish(acc[...])


def mm_nn(a, b, *, name, bias=None, relu2=False, add=None, out_dtype=F32, tm=1024, tn=1024, tk=1024):
    M, K = a.shape
    N = b.shape[1]
    tm, tn, tk = min(tm, M), min(tn, N), min(tk, K)
    nk = K // tk

    def body(*refs):
        it = iter(refs)
        a_ref, b_ref = next(it), next(it)
        bias_ref = next(it) if bias is not None else None
        add_ref = next(it) if add is not None else None
        o_ref = next(it)
        h_ref = next(it) if relu2 else None
        acc = next(it) if nk > 1 else None
        prod = lambda: jnp.dot(a_ref[...].astype(MM), b_ref[...].astype(MM), preferred_element_type=F32)

        def finish(r):
            if bias_ref is not None:
                r = r + bias_ref[...]
            if add_ref is not None:
                r = r + add_ref[...]
            if relu2:
                t = jnp.maximum(r, 0.0)
                o_ref[...] = (2.0 * t).astype(o_ref.dtype)
                h_ref[...] = (t * t).astype(h_ref.dtype)
            else:
                o_ref[...] = r.astype(o_ref.dtype)

        _accumulate(acc, prod, pl.program_id(2), nk, finish)

    in_specs = [pl.BlockSpec((tm, tk), lambda i, j, k: (i, k)), pl.BlockSpec((tk, tn), lambda i, j, k: (k, j))]
    args = [a, b]
    if bias is not None:
        in_specs.append(pl.BlockSpec((1, tn), lambda i, j, k: (0, j)))
        args.append(bias)
    if add is not None:
        in_specs.append(pl.BlockSpec((tm, tn), lambda i, j, k: (i, j)))
        args.append(add)
    out_shape = [_sds((M, N), out_dtype)]
    out_specs = [pl.BlockSpec((tm, tn), lambda i, j, k: (i, j))]
    if relu2:
        out_shape.append(_sds((M, N), MM))
        out_specs.append(pl.BlockSpec((tm, tn), lambda i, j, k: (i, j)))
    res = pl.pallas_call(
        body, name=name, grid=(M // tm, N // tn, nk), in_specs=in_specs, out_specs=out_specs, out_shape=out_shape,
        scratch_shapes=[pltpu.VMEM((tm, tn), F32)] if nk > 1 else [],
        compiler_params=_cp(("parallel", "parallel", "arbitrary")))(*args)
    return res if relu2 else res[0]


def mm_nt(a, b, *, name, add=None, dact=None, out_dtype=F32, tm=1024, tn=1024, tk=1024):
    M, Nc = a.shape
    Ko = b.shape[0]
    tm, tn, tk = min(tm, M), min(tn, Ko), min(tk, Nc)
    nk = Nc // tk
    ni = M // tm

    def body(*refs):
        it = iter(refs)
        a_ref, b_ref = next(it), next(it)
        add_ref = next(it) if add is not None else None
        d_ref = next(it) if dact is not None else None
        o_ref = next(it)
        db_ref = next(it) if dact is not None else None
        acc = next(it) if nk > 1 else None
        i = pl.program_id(1)
        prod = lambda: lax.dot_general(a_ref[...].astype(MM), b_ref[...].astype(MM), (((1,), (1,)), ((), ())),
                                       preferred_element_type=F32)

        def finish(r):
            if add_ref is not None:
                r = r + add_ref[...]
            if d_ref is not None:
                r = r * d_ref[...].astype(F32)
                s = jnp.sum(r, axis=0, keepdims=True)
                row0 = lax.broadcasted_iota(jnp.int32, db_ref.shape, 0) == 0

                @pl.when(i == 0)
                def _():
                    db_ref[...] = jnp.zeros_like(db_ref)

                db_ref[...] += jnp.where(row0, s, 0.0)
            o_ref[...] = r.astype(o_ref.dtype)

        _accumulate(acc, prod, pl.program_id(2), nk, finish)

    in_specs = [pl.BlockSpec((tm, tk), lambda j, i, k: (i, k)), pl.BlockSpec((tn, tk), lambda j, i, k: (j, k))]
    args = [a, b]
    for extra in (add, dact):
        if extra is not None:
            in_specs.append(pl.BlockSpec((tm, tn), lambda j, i, k: (i, j)))
            args.append(extra)
    out_shape = [_sds((M, Ko), out_dtype)]
    out_specs = [pl.BlockSpec((tm, tn), lambda j, i, k: (i, j))]
    if dact is not None:
        out_shape.append(_sds((8, Ko), F32))
        out_specs.append(pl.BlockSpec((8, tn), lambda j, i, k: (0, j)))
    res = pl.pallas_call(
        body, name=name, grid=(Ko // tn, ni, nk), in_specs=in_specs, out_specs=out_specs, out_shape=out_shape,
        scratch_shapes=[pltpu.VMEM((tm, tn), F32)] if nk > 1 else [],
        compiler_params=_cp(("parallel", "arbitrary", "arbitrary")))(*args)
    return res if dact is not None else res[0]


def mm_tn(a, b, *, name, tm=1024, tn=1024, tk=1024):
    T, M = a.shape
    N = b.shape[1]
    tm, tn, tk = min(tm, M), min(tn, N), min(tk, T)

    def body(a_ref, b_ref, o_ref):
        @pl.when(pl.program_id(2) == 0)
        def _():
            o_ref[...] = jnp.zeros_like(o_ref)

        o_ref[...] += lax.dot_general(a_ref[...].astype(MM), b_ref[...].astype(MM), (((0,), (0,)), ((), ())),
                                      preferred_element_type=F32)

    return pl.pallas_call(
        body, name=name, grid=(M // tm, N // tn, T // tk),
        in_specs=[pl.BlockSpec((tk, tm), lambda i, j, k: (k, i)), pl.BlockSpec((tk, tn), lambda i, j, k: (k, j))],
        out_specs=pl.BlockSpec((tm, tn), lambda i, j, k: (i, j)), out_shape=_sds((M, N)),
        compiler_params=_cp(("parallel", "parallel", "arbitrary")))(a, b)


def _sigmoid(x):
    return 1.0 / (1.0 + jnp.exp(-x))


def _silu(x):
    return x * _sigmoid(x)


def _softplus(x):
    return jnp.maximum(x, 0.0) + jnp.log1p(jnp.exp(-jnp.abs(x)))


def _ext(main_ref, prev_ref, next_ref, first, last):
    p = jnp.where(first, 0.0, prev_ref[...].astype(F32))
    n = jnp.where(last, 0.0, next_ref[...].astype(F32))
    return jnp.concatenate([p, main_ref[...].astype(F32), n], axis=0)


def _shift_dn(x):
    return pltpu.roll(x, 1, 0)


def _shift_up(x):
    return pltpu.roll(x, x.shape[0] - 1, 0)


def _conv3(xe, w):
    return w[0:1, :] * _shift_dn(xe) + w[1:2, :] * xe + w[2:3, :] * _shift_up(xe)


def _conv3_t(de, w):
    return w[0:1, :] * _shift_up(de) + w[1:2, :] * de + w[2:3, :] * _shift_dn(de)


def _halo_specs(bt, T, col, lead=None):
    r = bt // HALO
    last = T // HALO - 1
    if lead is None:
        return [pl.BlockSpec((bt, D), lambda i: (i, col)),
                pl.BlockSpec((HALO, D), lambda i: (jnp.maximum(i * r - 1, 0), col)),
                pl.BlockSpec((HALO, D), lambda i: (jnp.minimum((i + 1) * r, last), col))]
    return [pl.BlockSpec((lead, bt, D), lambda i: (0, i, col)),
            pl.BlockSpec((lead, HALO, D), lambda i: (0, jnp.maximum(i * r - 1, 0), col)),
            pl.BlockSpec((lead, HALO, D), lambda i: (0, jnp.minimum((i + 1) * r, last), col))]


def _qkv_rows(cq, ck, cv):
    sq, sk, sv = _silu(cq), _silu(ck), _silu(cv)
    qs, ks = [], []
    for h in range(NH):
        s = slice(h * HD, (h + 1) * HD)
        qh, kh = sq[:, s], sk[:, s]
        qs.append(qh * lax.rsqrt(jnp.sum(qh * qh, axis=-1, keepdims=True) + L2_EPS) * (HD ** -0.5))
        ks.append(kh * lax.rsqrt(jnp.sum(kh * kh, axis=-1, keepdims=True) + L2_EPS))
    return jnp.concatenate(qs, axis=1), jnp.concatenate(ks, axis=1), sv


def _chunk_masks(bt):
    row = lax.broadcasted_iota(jnp.int32, (bt, bt), 0)
    col = lax.broadcasted_iota(jnp.int32, (bt, bt), 1)
    same = (row // CH) == (col // CH)
    lower = jnp.where(same & (col <= row), 1.0, 0.0).astype(F32)
    upper = jnp.where(same & (col >= row), 1.0, 0.0).astype(F32)
    return lower, upper


def _gate_rows(ab, gp, lower, upper):
    lane = lax.broadcasted_iota(jnp.int32, ab.shape, 1)
    g = -jnp.exp(gp[0:1, :]) * _softplus(ab + gp[1:2, :])
    g = jnp.where(lane < 2 * NH, g, 0.0)
    gf = jnp.dot(lower, g, precision=HI, preferred_element_type=F32)
    gr = jnp.dot(upper, g, precision=HI, preferred_element_type=F32)
    gc = jnp.where(lane < NH, gf, gr)
    beta = _sigmoid(ab)
    return jnp.where(lane < 2 * NH, gc, jnp.where(lane < 4 * NH, beta, 0.0))


def pre_qkv_fwd(pm, pab, cw, gp, *, bt=256):
    T = pm.shape[0]
    bt = min(bt, T)
    n = T // bt

    def body(q0, q1, q2, k0, k1, k2, v0, v1, v2, ab_ref, cw_ref, gp_ref, q_ref, k_ref, v_ref, gb_ref):
        i = pl.program_id(0)
        first, last = i == 0, i == n - 1
        cs = []
        for c, (m, p, x) in enumerate(((q0, q1, q2), (k0, k1, k2), (v0, v1, v2))):
            xe = _ext(m, p, x, first, last)
            cs.append(_conv3(xe, cw_ref[:, c * D:(c + 1) * D])[HALO:HALO + bt])
        q, k, v = _qkv_rows(*cs)
        q_ref[...], k_ref[...], v_ref[...] = q, k, v
        lower, upper = _chunk_masks(bt)
        gb_ref[...] = _gate_rows(ab_ref[...], gp_ref[...], lower, upper)

    in_specs = (_halo_specs(bt, T, CQ) + _halo_specs(bt, T, CK) + _halo_specs(bt, T, CV)
                + [pl.BlockSpec((bt, 128), lambda i: (i, 0)), pl.BlockSpec((8, 3 * D), lambda i: (0, 0)),
                   pl.BlockSpec((8, 128), lambda i: (0, 0))])
    row = pl.BlockSpec((bt, D), lambda i: (i, 0))
    return pl.pallas_call(
        body, name="pre_qkv_fwd", grid=(n,), in_specs=in_specs,
        out_specs=[row, row, row, pl.BlockSpec((bt, 128), lambda i: (i, 0))],
        out_shape=[_sds((T, D)), _sds((T, D)), _sds((T, D)), _sds((T, 128))],
        compiler_params=_cp(("parallel",)))(*([pm] * 9), pab, cw, gp)


def pre_qkv_bwd(pm, pab, cw, gp, dq2, dk2, dv2, dgb2, dpm, *, bt=256):
    T = pm.shape[0]
    bt = min(bt, T)
    n = T // bt
    E = bt + 2 * HALO

    def body(*refs):
        it = iter(refs)
        xs = [[next(it) for _ in range(3)] for _ in range(3)]
        ds = [[next(it) for _ in range(3)] for _ in range(3)]
        ab_ref, dgb_ref, cw_ref, gp_ref, _alias = next(it), next(it), next(it), next(it), next(it)
        o_ref, dab_ref, dcw_ref, dgp_ref = (next(it) for _ in range(4))
        i = pl.program_id(0)
        first, last = i == 0, i == n - 1

        @pl.when(first)
        def _():
            dcw_ref[...] = jnp.zeros_like(dcw_ref)
            dgp_ref[...] = jnp.zeros_like(dgp_ref)

        xes = [_ext(*xs[c], first, last) for c in range(3)]
        ces = [_conv3(xes[c], cw_ref[:, c * D:(c + 1) * D]) for c in range(3)]
        cts = []
        for c in range(3):
            m, p, x = ds[c]
            pe = jnp.where(first, 0.0, p[0] + p[1])
            ne = jnp.where(last, 0.0, x[0] + x[1])
            cts.append(jnp.concatenate([pe, m[0] + m[1], ne], axis=0))
        _, vjp = jax.vjp(_qkv_rows, *ces)
        dces = vjp(tuple(cts))
        rowi = lax.broadcasted_iota(jnp.int32, (E, 1), 0)
        central = (rowi >= HALO) & (rowi < HALO + bt)
        row8 = lax.broadcasted_iota(jnp.int32, (8, D), 0)
        for c in range(3):
            w = cw_ref[:, c * D:(c + 1) * D]
            o_ref[:, c * D:(c + 1) * D] = _conv3_t(dces[c], w)[HALO:HALO + bt].astype(o_ref.dtype)
            dc = jnp.where(central, dces[c], 0.0)
            taps = (jnp.sum(dc * _shift_dn(xes[c]), axis=0, keepdims=True),
                    jnp.sum(dc * xes[c], axis=0, keepdims=True),
                    jnp.sum(dc * _shift_up(xes[c]), axis=0, keepdims=True))
            upd = jnp.where(row8 == 0, taps[0], jnp.where(row8 == 1, taps[1], jnp.where(row8 == 2, taps[2], 0.0)))
            dcw_ref[:, c * D:(c + 1) * D] += upd
        lower, upper = _chunk_masks(bt)
        _, gvjp = jax.vjp(lambda ab, gp: _gate_rows(ab, gp, lower, upper), ab_ref[...], gp_ref[...])
        dab, dgp = gvjp(dgb_ref[0] + dgb_ref[1])
        dab_ref[...] = dab
        dgp_ref[...] += dgp

    in_specs = (_halo_specs(bt, T, CQ) + _halo_specs(bt, T, CK) + _halo_specs(bt, T, CV)
                + _halo_specs(bt, T, 0, lead=2) * 3
                + [pl.BlockSpec((bt, 128), lambda i: (i, 0)), pl.BlockSpec((2, bt, 128), lambda i: (0, i, 0)),
                   pl.BlockSpec((8, 3 * D), lambda i: (0, 0)), pl.BlockSpec((8, 128), lambda i: (0, 0)),
                   pl.BlockSpec(memory_space=pl.ANY)])
    out_specs = [pl.BlockSpec((bt, 3 * D), lambda i: (i, 0)), pl.BlockSpec((bt, 128), lambda i: (i, 0)),
                 pl.BlockSpec((8, 3 * D), lambda i: (0, 0)), pl.BlockSpec((8, 128), lambda i: (0, 0))]
    return pl.pallas_call(
        body, name="pre_qkv_bwd", grid=(n,), in_specs=in_specs, out_specs=out_specs,
        out_shape=[_sds(dpm.shape, dpm.dtype), _sds((T, 128)), _sds((8, 3 * D)), _sds((8, 128))],
        input_output_aliases={len(in_specs) - 1: 0},
        compiler_params=_cp(("arbitrary",)))(
            *([pm] * 9), dq2, dq2, dq2, dk2, dk2, dk2, dv2, dv2, dv2, pab, dgb2, cw, gp, dpm)


def pre_sc_fwd(pm, cw, *, bt=256):
    T = pm.shape[0]
    bt = min(bt, T)
    n = T // bt

    def body(b_ref, c0, c1, c2, x0, x1, x2, cw_ref, o_ref):
        i = pl.program_id(0)
        first, last = i == 0, i == n - 1
        pe = _ext(c0, c1, c2, first, last) * _ext(x0, x1, x2, first, last)
        o_ref[...] = (b_ref[...] * _conv3(pe, cw_ref[...])[HALO:HALO + bt]).astype(o_ref.dtype)

    in_specs = ([pl.BlockSpec((bt, D), lambda i: (i, CSB))] + _halo_specs(bt, T, CSC) + _halo_specs(bt, T, CSX)
                + [pl.BlockSpec((8, D), lambda i: (0, 0))])
    return pl.pallas_call(
        body, name="pre_sc_fwd", grid=(n,), in_specs=in_specs, out_specs=pl.BlockSpec((bt, D), lambda i: (i, 0)),
        out_shape=_sds((T, D), MM), compiler_params=_cp(("parallel",)))(*([pm] * 7), cw)


def pre_sc_bwd(pm, cw, dsc, dpm, *, bt=256):
    T = pm.shape[0]
    bt = min(bt, T)
    n = T // bt
    E = bt + 2 * HALO

    def body(b0, b1, b2, c0, c1, c2, x0, x1, x2, d0, d1, d2, cw_ref, _alias, o_ref, dcw_ref):
        i = pl.program_id(0)
        first, last = i == 0, i == n - 1

        @pl.when(first)
        def _():
            dcw_ref[...] = jnp.zeros_like(dcw_ref)

        ce, xe = _ext(c0, c1, c2, first, last), _ext(x0, x1, x2, first, last)
        pe = ce * xe
        w = cw_ref[...]
        dout = d0[...]
        o_ref[:, 0:D] = (dout * _conv3(pe, w)[HALO:HALO + bt]).astype(o_ref.dtype)
        dce = _ext(d0, d1, d2, first, last) * _ext(b0, b1, b2, first, last)
        dp = _conv3_t(dce, w)[HALO:HALO + bt]
        o_ref[:, D:2 * D] = (dp * x0[...]).astype(o_ref.dtype)
        o_ref[:, 2 * D:3 * D] = (dp * c0[...]).astype(o_ref.dtype)
        rowi = lax.broadcasted_iota(jnp.int32, (E, 1), 0)
        dc = jnp.where((rowi >= HALO) & (rowi < HALO + bt), dce, 0.0)
        row8 = lax.broadcasted_iota(jnp.int32, (8, D), 0)
        taps = (jnp.sum(dc * _shift_dn(pe), axis=0, keepdims=True), jnp.sum(dc * pe, axis=0, keepdims=True),
                jnp.sum(dc * _shift_up(pe), axis=0, keepdims=True))
        dcw_ref[...] += jnp.where(row8 == 0, taps[0], jnp.where(row8 == 1, taps[1], jnp.where(row8 == 2, taps[2], 0.0)))

    dsc_specs = [pl.BlockSpec((bt, D), lambda i: (i, 0)),
                 pl.BlockSpec((HALO, D), lambda i: (jnp.maximum(i * (bt // HALO) - 1, 0), 0)),
                 pl.BlockSpec((HALO, D), lambda i: (jnp.minimum((i + 1) * (bt // HALO), T // HALO - 1), 0))]
    in_specs = (_halo_specs(bt, T, CSB) + _halo_specs(bt, T, CSC) + _halo_specs(bt, T, CSX) + dsc_specs
                + [pl.BlockSpec((8, D), lambda i: (0, 0)), pl.BlockSpec(memory_space=pl.ANY)])
    return pl.pallas_call(
        body, name="pre_sc_bwd", grid=(n,), in_specs=in_specs,
        out_specs=[pl.BlockSpec((bt, 3 * D), lambda i: (i, 1)), pl.BlockSpec((8, D), lambda i: (0, 0))],
        out_shape=[_sds(dpm.shape, dpm.dtype), _sds((8, D))], input_output_aliases={len(in_specs) - 1: 0},
        compiler_params=_cp(("arbitrary",)))(*([pm] * 9), dsc, dsc, dsc, cw, dpm)


def _bdot(a, b, dims):
    return lax.dot_general(a.astype(MM), b.astype(MM), (dims, ((), ())), preferred_element_type=F32)


_NN, _NT, _TN = ((1,), (0,)), ((1,), (1,)), ((0,), (0,))


def _raw_nn(a, b):
    return _bdot(a, b, _NN)


def _raw_nt(a, b):
    return _bdot(a, b, _NT)


def _raw_tn(a, b):
    return _bdot(a, b, _TN)


def _make_vjp_ops():
    @jax.custom_vjp
    def nn(a, b):
        return _raw_nn(a, b)

    @jax.custom_vjp
    def nt(a, b):
        return _raw_nt(a, b)

    @jax.custom_vjp
    def tn(a, b):
        return _raw_tn(a, b)

    nn.defvjp(lambda a, b: (_raw_nn(a, b), (a, b)), lambda r, g: (_raw_nt(g, r[1]), _raw_tn(r[0], g)))
    nt.defvjp(lambda a, b: (_raw_nt(a, b), (a, b)), lambda r, g: (_raw_nn(g, r[1]), _raw_tn(g, r[0])))
    tn.defvjp(lambda a, b: (_raw_tn(a, b), (a, b)), lambda r, g: (_raw_nt(r[1], g), _raw_nn(r[0], g)))

    @jax.custom_vjp
    def inv_saved(A, Y):
        return Y

    def inv_bwd(Y, g):
        M = g + _raw_tn(Y, g)
        return -(M + _raw_nt(M, Y)), jnp.zeros_like(Y)

    inv_saved.defvjp(lambda A, Y: (Y, Y), inv_bwd)
    return nn, nt, tn, inv_saved


GH = 2
GR = GH * CH
NG = NH // GH
CBI = 2
CBB = 2


def _scan_chunks(n_chunks):
    return 4 if n_chunks % 4 == 0 else 2


def _tri_inv_y_all(As):
    Ys = [-A for A in As]
    Ps = [_raw_nn(A, A) for A in As]
    for stage in range(5):
        squares = [_raw_nn(P, P) for P in Ps] if stage < 4 else None
        Ys = [Y + P + _raw_nn(Y, P) for Y, P in zip(Ys, Ps)]
        Ps = squares
    return Ys


def _intra_groups(chains, incl, strict, eye, lastc, ops, inv_all):
    nn, nt, tn = ops
    st = []
    for qs, ks, vs, gcols, bcols in chains:
        q, k, v = (jnp.concatenate(t, axis=0) for t in (qs, ks, vs))
        gcol, bcol = jnp.concatenate(gcols, axis=0), jnp.concatenate(bcols, axis=0)
        grow = jnp.sum(eye * gcol, axis=0, keepdims=True)
        gam = jnp.where(incl, jnp.exp(jnp.where(incl, gcol - grow, 0.0)), 0.0)
        glast = jnp.sum(jnp.where(lastc, grow, 0.0), axis=1, keepdims=True)
        st.append((q, k, v, gcol, bcol, gam, glast, jnp.exp(gcol), k * bcol))
    As = [jnp.where(strict, nt(kb, k) * gam, 0.0) for (q, k, v, gcol, bcol, gam, glast, eg, kb) in st]
    Ys = inv_all(As)
    vbs = [v * bcol for (q, k, v, gcol, bcol, gam, glast, eg, kb) in st]
    kgs = [kb * eg for (q, k, v, gcol, bcol, gam, glast, eg, kb) in st]
    us = [vb + nn(Y, vb) for Y, vb in zip(Ys, vbs)]
    ws = [kg + nn(Y, kg) for Y, kg in zip(Ys, kgs)]
    Ps = [nt(q, k) * gam for (q, k, v, gcol, bcol, gam, glast, eg, kb) in st]
    return [((u, w, P, q * eg, k * jnp.exp(glast - gcol), jnp.exp(glast)), Y)
            for u, w, P, Y, (q, k, v, gcol, bcol, gam, glast, eg, kb) in zip(us, ws, Ps, Ys, st)]


def _scan_groups(chains, ops):
    nn, nt, tn, _ = ops
    vns = [[us[j] - nn(ws[j], Ss[j]) for j in range(GH)] for Ss, us, ws, P, qds, kds, egls in chains]
    os_ = [jnp.concatenate([nn(qds[j], Ss[j]) for j in range(GH)], axis=0) + nn(P, jnp.concatenate(vn, axis=0))
           for (Ss, us, ws, P, qds, kds, egls), vn in zip(chains, vns)]
    S2s = [[Ss[j] * egls[j] + tn(kds[j], vn[j]) for j in range(GH)]
           for (Ss, us, ws, P, qds, kds, egls), vn in zip(chains, vns)]
    return list(zip(os_, S2s))


def _group_masks(rev, rows=GR):
    r = lax.broadcasted_iota(jnp.int32, (rows, rows), 0)
    c = lax.broadcasted_iota(jnp.int32, (rows, rows), 1)
    same = (r // CH) == (c // CH)
    ahead = jnp.where(rev, c - r, r - c)
    incl = same & (ahead >= 0)
    strict = same & (ahead > 0)
    eye = jnp.where(r == c, 1.0, 0.0).astype(F32)
    lastc = same & ((c % CH) == jnp.where(rev, 0, CH - 1))
    return incl, strict, eye, lastc


def _head_gates(gb, h, rev):
    gcol = jnp.where(rev, gb[:, NH + h:NH + h + 1], gb[:, h:h + 1])
    bcol = jnp.where(rev, gb[:, 3 * NH + h:3 * NH + h + 1], gb[:, 2 * NH + h:2 * NH + h + 1])
    return gcol, bcol


def _hs(h):
    return slice(h * HD, (h + 1) * HD)


def _blockdiag(a, b):
    z = jnp.zeros_like(a)
    return jnp.concatenate([jnp.concatenate([a, z], axis=1), jnp.concatenate([z, b], axis=1)], axis=0)


def _load_chain(q_ref, k_ref, v_ref, gb_ref, c, g, rev, gh=GH):
    r = slice(c * CH, (c + 1) * CH)
    heads = range(g * gh, (g + 1) * gh)
    gates = [_head_gates(gb_ref[r, :], h, rev) for h in heads]
    return ([q_ref[r, _hs(h)] for h in heads], [k_ref[r, _hs(h)] for h in heads], [v_ref[r, _hs(h)] for h in heads],
            [t[0] for t in gates], [t[1] for t in gates])


def gdn_intra_fwd(q, k, v, gb):
    T = q.shape[0]
    N = T // CH
    ops = (_raw_nn, _raw_nt, _raw_tn)

    def body(q_ref, k_ref, v_ref, gb_ref, u_ref, w_ref, qd_ref, kd_ref, pp_ref, ys_ref, eg_ref):
        rev = pl.program_id(0) == 1
        masks = _group_masks(rev)
        where = [(c, g) for c in range(CBI) for g in range(NG)]
        chains = [_load_chain(q_ref, k_ref, v_ref, gb_ref, c, g, rev) for c, g in where]
        for (c, g), ((u, w, P, qd, kd, egl), Y) in zip(where, _intra_groups(chains, *masks, ops, _tri_inv_y_all)):
            r = slice(c * CH, (c + 1) * CH)
            pp_ref[c, g] = P.astype(MM)
            ys_ref[c, g] = Y.astype(MM)
            for j, h in enumerate(range(g * GH, (g + 1) * GH)):
                rows = slice(j * CH, (j + 1) * CH)
                u_ref[r, _hs(h)] = u[rows]
                w_ref[r, _hs(h)] = w[rows].astype(MM)
                qd_ref[r, _hs(h)] = qd[rows].astype(MM)
                kd_ref[r, _hs(h)] = kd[rows].astype(MM)
                eg_ref[c, h:h + 1, :] = jnp.broadcast_to(egl[j * CH:j * CH + 1, :], (1, 128))

    row = pl.BlockSpec((CBI * CH, D), lambda d, n: (n, 0))
    drow = pl.BlockSpec((None, CBI * CH, D), lambda d, n: (d, n, 0))
    mat = pl.BlockSpec((None, CBI, NG, GR, GR), lambda d, n: (d, n, 0, 0, 0))
    return pl.pallas_call(
        body, name="gdn_intra_fwd", grid=(2, N // CBI),
        in_specs=[row, row, row, pl.BlockSpec((CBI * CH, 128), lambda d, n: (n, 0))],
        out_specs=[drow] * 4 + [mat, mat, pl.BlockSpec((None, CBI, NH, 128), lambda d, n: (d, n, 0, 0))],
        out_shape=[_sds((2, T, D))] + [_sds((2, T, D), MM)] * 3 + [_sds((2, N, NG, GR, GR), MM)] * 2
                  + [_sds((2, N, NH, 128))],
        compiler_params=_cp(("parallel", "parallel")))(q, k, v, gb)


def gdn_scan_fwd(u, w, qd, kd, pp, eg):
    T = u.shape[1]
    N = T // CH
    ops = (_raw_nn, _raw_nt, _raw_tn, None)

    cbs = _scan_chunks(N)
    NB = N // cbs

    def body(u_ref, w_ref, qd_ref, kd_ref, pp_ref, eg_ref, o_ref, s0_ref, S):
        d = pl.program_id(0)

        @pl.when(pl.program_id(1) == 0)
        def _():
            S[...] = jnp.zeros_like(S)

        def chunk(c, carry):
            pc = c + d * (cbs - 1 - 2 * c)
            r = pl.ds(pl.multiple_of(pc * CH, CH), CH)
            chains = []
            for g in range(NG):
                heads = range(g * GH, (g + 1) * GH)
                Ss = [S[h] for h in heads]
                for h, Sh in zip(heads, Ss):
                    s0_ref[pc, h] = Sh
                chains.append((Ss, [u_ref[r, _hs(h)] for h in heads], [w_ref[r, _hs(h)] for h in heads], pp_ref[pc, g],
                               [qd_ref[r, _hs(h)] for h in heads], [kd_ref[r, _hs(h)] for h in heads],
                               [eg_ref[pc, h:h + 1, :] for h in heads]))
            for g, (o, S2) in enumerate(_scan_groups(chains, ops)):
                for j, h in enumerate(range(g * GH, (g + 1) * GH)):
                    o_ref[r, _hs(h)] = o[j * CH:(j + 1) * CH]
                    S[h] = S2[j]
            return carry

        lax.fori_loop(0, cbs, chunk, 0)

    bidx = lambda d, n: n + d * (NB - 1 - 2 * n)
    drow = pl.BlockSpec((None, cbs * CH, D), lambda d, n: (d, bidx(d, n), 0))
    mat = pl.BlockSpec((None, cbs, NG, GR, GR), lambda d, n: (d, bidx(d, n), 0, 0, 0))
    return pl.pallas_call(
        body, name="gdn_scan_fwd", grid=(2, NB),
        in_specs=[drow] * 4 + [mat, pl.BlockSpec((None, cbs, NH, 128), lambda d, n: (d, bidx(d, n), 0, 0))],
        out_specs=[drow, pl.BlockSpec((None, cbs, NH, HD, HD), lambda d, n: (d, bidx(d, n), 0, 0, 0))],
        out_shape=[_sds((2, T, D)), _sds((2, N, NH, HD, HD))],
        scratch_shapes=[pltpu.VMEM((NH, HD, HD), F32)],
        compiler_params=_cp(("arbitrary", "arbitrary")))(u, w, qd, kd, pp, eg)


def gdn_scan_bwd(u, w, qd, kd, pp, eg, s0, do):
    T = u.shape[1]
    N = T // CH
    ops = _make_vjp_ops()
    cbs = _scan_chunks(N)
    NB = N // cbs

    def body(u_ref, w_ref, qd_ref, kd_ref, pp_ref, eg_ref, s0_ref, do_ref,
             du_ref, dw_ref, dqd_ref, dkd_ref, dpp_ref, deg_ref, dS):
        d = pl.program_id(0)

        @pl.when(pl.program_id(1) == 0)
        def _():
            dS[...] = jnp.zeros_like(dS)

        def chunk(c, carry):
            pc = (cbs - 1 - c) + d * (2 * c - (cbs - 1))
            r = pl.ds(pl.multiple_of(pc * CH, CH), CH)
            chains, cts = [], []
            for g in range(NG):
                heads = range(g * GH, (g + 1) * GH)
                chains.append(([s0_ref[pc, h] for h in heads], [u_ref[r, _hs(h)] for h in heads],
                               [w_ref[r, _hs(h)].astype(F32) for h in heads], pp_ref[pc, g].astype(F32),
                               [qd_ref[r, _hs(h)].astype(F32) for h in heads],
                               [kd_ref[r, _hs(h)].astype(F32) for h in heads], [eg_ref[pc, h:h + 1, :] for h in heads]))
                cts.append((jnp.concatenate([do_ref[r, _hs(h)].astype(F32) for h in heads], axis=0),
                            [dS[h] for h in heads]))
            _, vjp = jax.vjp(lambda ch: _scan_groups(ch, ops), chains)
            (dchains,) = vjp(cts)
            for g, (dSs, dus, dws, dP, dqds, dkds, degs) in enumerate(dchains):
                dpp_ref[pc, g] = dP
                for j, h in enumerate(range(g * GH, (g + 1) * GH)):
                    dS[h] = dSs[j]
                    du_ref[r, _hs(h)] = dus[j].astype(MM)
                    dw_ref[r, _hs(h)] = dws[j].astype(MM)
                    dqd_ref[r, _hs(h)], dkd_ref[r, _hs(h)] = dqds[j], dkds[j]
                    deg_ref[pc, h:h + 1, :] = degs[j]
            return carry

        lax.fori_loop(0, cbs, chunk, 0)

    bidx = lambda d, n: (NB - 1 - n) + d * (2 * n - (NB - 1))
    drow = pl.BlockSpec((None, cbs * CH, D), lambda d, n: (d, bidx(d, n), 0))
    erow = pl.BlockSpec((None, cbs, NH, 128), lambda d, n: (d, bidx(d, n), 0, 0))
    mat = pl.BlockSpec((None, cbs, NG, GR, GR), lambda d, n: (d, bidx(d, n), 0, 0, 0))
    return pl.pallas_call(
        body, name="gdn_scan_bwd", grid=(2, NB),
        in_specs=[drow] * 4 + [mat, erow, pl.BlockSpec((None, cbs, NH, HD, HD), lambda d, n: (d, bidx(d, n), 0, 0, 0)),
                               pl.BlockSpec((cbs * CH, D), lambda d, n: (bidx(d, n), 0))],
        out_specs=[drow] * 4 + [mat, erow],
        out_shape=[_sds((2, T, D), MM)] * 2 + [_sds((2, T, D))] * 2 + [_sds((2, N, NG, GR, GR))]
                  + [_sds((2, N, NH, 128))],
        scratch_shapes=[pltpu.VMEM((NH, HD, HD), F32)],
        compiler_params=_cp(("arbitrary", "arbitrary")))(u, w, qd, kd, pp, eg, s0, do)


def gdn_intra_bwd(q, k, v, gb, ys, du, dw, dqd, dkd, dpp, deg):
    T = q.shape[0]
    N = T // CH
    nn, nt, tn, inv_saved = _make_vjp_ops()

    def body(q_ref, k_ref, v_ref, gb_ref, ys_ref, du_ref, dw_ref, dqd_ref, dkd_ref, dpp_ref, deg_ref,
             dq_ref, dk_ref, dv_ref, dgb_ref):
        rev = pl.program_id(0) == 1
        gh, rows = 2 * GH, 2 * GR
        masks = _group_masks(rev, rows)
        lane = lax.broadcasted_iota(jnp.int32, (CH, 128), 1)
        grow = lax.broadcasted_iota(jnp.int32, (rows, 1), 0)
        where = [(c, g) for c in range(CBB) for g in range(NG // 2)]
        pair = lambda ref, c, g: _blockdiag(ref[c, 2 * g].astype(F32), ref[c, 2 * g + 1].astype(F32))
        chains = [_load_chain(q_ref, k_ref, v_ref, gb_ref, c, g, rev, gh) for c, g in where]
        Ys = [pair(ys_ref, c, g) for c, g in where]
        inv_all = lambda As: [inv_saved(A, Y) for A, Y in zip(As, Ys)]
        _, vjp = jax.vjp(lambda ch: _intra_groups(ch, *masks, (nn, nt, tn), inv_all), chains)
        cts = []
        for c, g in where:
            r = slice(c * CH, (c + 1) * CH)
            heads = range(g * gh, (g + 1) * gh)
            stack = lambda ref: jnp.concatenate([ref[r, _hs(h)].astype(F32) for h in heads], axis=0)
            degl = jnp.zeros((rows, 1), F32)
            for j, h in enumerate(heads):
                degl = degl + jnp.where(grow == j * CH, jnp.sum(deg_ref[c, h:h + 1, :], axis=1, keepdims=True), 0.0)
            cts.append(((stack(du_ref), stack(dw_ref), pair(dpp_ref, c, g), stack(dqd_ref), stack(dkd_ref), degl),
                        jnp.zeros((rows, rows), F32)))
        (dchains,) = vjp(cts)
        dgbs = [jnp.zeros((CH, 128), F32) for _ in range(CBB)]
        for (c, g), (dqs, dks, dvs, dgs, dbs) in zip(where, dchains):
            r = slice(c * CH, (c + 1) * CH)
            for j, h in enumerate(range(g * gh, (g + 1) * gh)):
                dq_ref[r, _hs(h)], dk_ref[r, _hs(h)], dv_ref[r, _hs(h)] = dqs[j], dks[j], dvs[j]
                glane = jnp.where(rev, NH + h, h)
                dgbs[c] = dgbs[c] + jnp.where(lane == glane, dgs[j], 0.0) + jnp.where(lane == glane + 2 * NH, dbs[j], 0.0)
        for c in range(CBB):
            dgb_ref[c * CH:(c + 1) * CH, :] = dgbs[c]

    row = pl.BlockSpec((CBB * CH, D), lambda d, n: (n, 0))
    drow = pl.BlockSpec((None, CBB * CH, D), lambda d, n: (d, n, 0))
    mat = pl.BlockSpec((None, CBB, NG, GR, GR), lambda d, n: (d, n, 0, 0, 0))
    return pl.pallas_call(
        body, name="gdn_intra_bwd", grid=(2, N // CBB),
        in_specs=[row, row, row, pl.BlockSpec((CBB * CH, 128), lambda d, n: (n, 0)), mat, drow, drow, drow, drow, mat]
                 + [pl.BlockSpec((None, CBB, NH, 128), lambda d, n: (d, n, 0, 0))],
        out_specs=[drow, drow, drow, pl.BlockSpec((None, CBB * CH, 128), lambda d, n: (d, n, 0))],
        out_shape=[_sds((2, T, D))] * 3 + [_sds((2, T, 128))],
        compiler_params=_cp(("parallel", "parallel")))(q, k, v, gb, ys, du, dw, dqd, dkd, dpp, deg)


def _post_rows(o2a, o2b, z, nw):
    o = o2a + o2b
    outs = []
    for h in range(NH):
        s = slice(h * HD, (h + 1) * HD)
        oh = o[:, s]
        outs.append(oh * lax.rsqrt(jnp.mean(oh * oh, axis=-1, keepdims=True) + RMS_EPS) * nw * _silu(z[:, s]))
    return jnp.concatenate(outs, axis=1)


def post_fwd(o2, pm, nw, *, bt=512):
    T = pm.shape[0]
    bt = min(bt, T)

    def body(o_ref, z_ref, nw_ref, og_ref):
        og_ref[...] = _post_rows(o_ref[0], o_ref[1], z_ref[...], nw_ref[0:1, :]).astype(og_ref.dtype)

    return pl.pallas_call(
        body, name="post_fwd", grid=(T // bt,),
        in_specs=[pl.BlockSpec((2, bt, D), lambda i: (0, i, 0)), pl.BlockSpec((bt, D), lambda i: (i, CZ)),
                  pl.BlockSpec((8, 128), lambda i: (0, 0))],
        out_specs=pl.BlockSpec((bt, D), lambda i: (i, 0)), out_shape=_sds((T, D), MM),
        compiler_params=_cp(("parallel",)))(o2, pm, nw)


def post_bwd(o2, pm, nw, dog, dpm, *, bt=512):
    T = pm.shape[0]
    bt = min(bt, T)

    def body(o_ref, z_ref, nw_ref, dog_ref, _alias, do_ref, dz_ref, dnw_ref):
        @pl.when(pl.program_id(0) == 0)
        def _():
            dnw_ref[...] = jnp.zeros_like(dnw_ref)

        _, vjp = jax.vjp(_post_rows, o_ref[0], o_ref[1], z_ref[...], nw_ref[0:1, :])
        doa, _unused, dz, dnw = vjp(dog_ref[...])
        do_ref[...] = doa.astype(do_ref.dtype)
        dz_ref[...] = dz.astype(dz_ref.dtype)
        row8 = lax.broadcasted_iota(jnp.int32, (8, 128), 0)
        dnw_ref[...] += jnp.where(row8 == 0, dnw, 0.0)

    in_specs = [pl.BlockSpec((2, bt, D), lambda i: (0, i, 0)), pl.BlockSpec((bt, D), lambda i: (i, CZ)),
                pl.BlockSpec((8, 128), lambda i: (0, 0)), pl.BlockSpec((bt, D), lambda i: (i, 0)),
                pl.BlockSpec(memory_space=pl.ANY)]
    return pl.pallas_call(
        body, name="post_bwd", grid=(T // bt,), in_specs=in_specs,
        out_specs=[pl.BlockSpec((bt, D), lambda i: (i, 0)), pl.BlockSpec((bt, D), lambda i: (i, CZ)),
                   pl.BlockSpec((8, 128), lambda i: (0, 0))],
        out_shape=[_sds((T, D), MM), _sds(dpm.shape, dpm.dtype), _sds((8, 128))], input_output_aliases={4: 1},
        compiler_params=_cp(("arbitrary",)))(o2, pm, nw, dog, dpm)


def merge_fwd(ya, yb, pm, *, bt=512):
    T = pm.shape[0]
    bt = min(bt, T)

    def body(ya_ref, yb_ref, ga_ref, gb_ref, o_ref):
        o_ref[...] = (_sigmoid(ga_ref[...]) * ya_ref[...] + _sigmoid(gb_ref[...]) * yb_ref[...]).astype(o_ref.dtype)

    row = pl.BlockSpec((bt, D), lambda i: (i, 0))
    return pl.pallas_call(
        body, name="merge_fwd", grid=(T // bt,),
        in_specs=[row, row, pl.BlockSpec((bt, D), lambda i: (i, CGA)), pl.BlockSpec((bt, D), lambda i: (i, CGB))],
        out_specs=row, out_shape=_sds((T, D), MM), compiler_params=_cp(("parallel",)))(ya, yb, pm, pm)


def merge_bwd(ya, yb, pm, dmix, *, bt=512):
    T = pm.shape[0]
    bt = min(bt, T)

    def body(ya_ref, yb_ref, ga_ref, gb_ref, dm_ref, dya_ref, dyb_ref, dg_ref):
        dm = dm_ref[...]
        sa, sb = _sigmoid(ga_ref[...]), _sigmoid(gb_ref[...])
        dya_ref[...] = (dm * sa).astype(dya_ref.dtype)
        dyb_ref[...] = (dm * sb).astype(dyb_ref.dtype)
        dg_ref[:, 0:D] = (dm * ya_ref[...] * sa * (1.0 - sa)).astype(dg_ref.dtype)
        dg_ref[:, D:2 * D] = (dm * yb_ref[...] * sb * (1.0 - sb)).astype(dg_ref.dtype)

    row = pl.BlockSpec((bt, D), lambda i: (i, 0))
    return pl.pallas_call(
        body, name="merge_bwd", grid=(T // bt,),
        in_specs=[row, row, pl.BlockSpec((bt, D), lambda i: (i, CGA)), pl.BlockSpec((bt, D), lambda i: (i, CGB)), row],
        out_specs=[row, row, pl.BlockSpec((bt, 2 * D), lambda i: (i, CGA // 2))],
        out_shape=[_sds((T, D), MM), _sds((T, D), MM), _sds((T, NMAIN), MM)],
        compiler_params=_cp(("parallel",)))(ya, yb, pm, pm, dmix)


def _ln_rows(x, y, bias, g, b):
    r = ALPHA * x + y + bias
    mu = jnp.mean(r, axis=-1, keepdims=True)
    var = jnp.mean(jnp.square(r - mu), axis=-1, keepdims=True)
    return (r - mu) * lax.rsqrt(var + LN_EPS) * g + b


def ln_fwd(x, y, p, *, name, bt=512):
    T = x.shape[0]
    bt = min(bt, T)

    def body(x_ref, y_ref, p_ref, o_ref, ob_ref):
        r = _ln_rows(x_ref[...], y_ref[...], p_ref[0:1, :], p_ref[1:2, :], p_ref[2:3, :])
        o_ref[...] = r
        ob_ref[...] = r.astype(ob_ref.dtype)

    row = pl.BlockSpec((bt, D), lambda i: (i, 0))
    return pl.pallas_call(
        body, name=name, grid=(T // bt,), in_specs=[row, row, pl.BlockSpec((8, D), lambda i: (0, 0))],
        out_specs=[row, row], out_shape=[_sds((T, D)), _sds((T, D), MM)],
        compiler_params=_cp(("parallel",)))(x, y, p)


def ln_bwd(x, y, p, ct, ct2=None, *, name, bt=512):
    T = x.shape[0]
    bt = min(bt, T)

    def body(*refs):
        it = iter(refs)
        x_ref, y_ref, p_ref, c_ref = next(it), next(it), next(it), next(it)
        c2_ref = next(it) if ct2 is not None else None
        dxa_ref, dr_ref, dp_ref = next(it), next(it), next(it)

        @pl.when(pl.program_id(0) == 0)
        def _():
            dp_ref[...] = jnp.zeros_like(dp_ref)

        c = c_ref[...]
        if c2_ref is not None:
            c = c + c2_ref[...]
        _, vjp = jax.vjp(_ln_rows, x_ref[...], y_ref[...], p_ref[0:1, :], p_ref[1:2, :], p_ref[2:3, :])
        _dx, dy, dbias, dg, db = vjp(c)
        dxa_ref[...] = ALPHA * dy
        dr_ref[...] = dy.astype(dr_ref.dtype)
        row8 = lax.broadcasted_iota(jnp.int32, (8, D), 0)
        dp_ref[...] += jnp.where(row8 == 0, dbias, jnp.where(row8 == 1, dg, jnp.where(row8 == 2, db, 0.0)))

    row = pl.BlockSpec((bt, D), lambda i: (i, 0))
    in_specs = [row, row, pl.BlockSpec((8, D), lambda i: (0, 0)), row] + ([row] if ct2 is not None else [])
    args = [x, y, p, ct] + ([ct2] if ct2 is not None else [])
    return pl.pallas_call(
        body, name=name, grid=(T // bt,), in_specs=in_specs,
        out_specs=[row, row, pl.BlockSpec((8, D), lambda i: (0, 0))],
        out_shape=[_sds((T, D)), _sds((T, D), MM), _sds((8, D))],
        compiler_params=_cp(("arbitrary",)))(*args)


def loss_fwd_bwd(xl, target, *, bt=512):
    T = xl.shape[0]
    bt = min(bt, T)

    def body(x_ref, t_ref, l_ref, d_ref):
        @pl.when(pl.program_id(0) == 0)
        def _():
            l_ref[...] = jnp.zeros_like(l_ref)

        e = x_ref[...] - t_ref[...]
        d_ref[...] = e * (1.0 / D)
        l_ref[...] += 0.5 * jnp.sum(jnp.mean(e * e, axis=-1, keepdims=True), axis=0, keepdims=True)

    row = pl.BlockSpec((bt, D), lambda i: (i, 0))
    return pl.pallas_call(
        body, name="loss", grid=(T // bt,), in_specs=[row, row],
        out_specs=[pl.BlockSpec((8, 128), lambda i: (0, 0)), row], out_shape=[_sds((8, 128)), _sds((T, D))],
        compiler_params=_cp(("arbitrary",)))(xl, target)


def _row_tile(R, Cc, elems=1 << 18):
    if R * Cc <= elems:
        return R
    tr = 8
    while tr * 2 * Cc <= elems and R % (tr * 2) == 0:
        tr *= 2
    return tr


def adam(w, m, v, ga, gb=None, *, name):
    R, Cc = w.shape
    tr = _row_tile(R, Cc)

    def body(*refs):
        it = iter(refs)
        w_ref, m_ref, v_ref, a_ref = next(it), next(it), next(it), next(it)
        b_ref = next(it) if gb is not None else None
        g_ref, d_ref, mo_ref, vo_ref = next(it), next(it), next(it), next(it)
        g = a_ref[...]
        if b_ref is not None:
            g = g + b_ref[...]
        m2 = B1 * m_ref[...] + (1.0 - B1) * g
        v2 = B2 * v_ref[...] + (1.0 - B2) * jnp.square(g)
        m_hat = m2 / (1.0 - B1 ** STEP)
        v_hat = v2 / (1.0 - B2 ** STEP)
        g_ref[...] = g
        d_ref[...] = -LR * (m_hat / (jnp.sqrt(v_hat) + EPS) + WD * w_ref[...])
        mo_ref[...] = m2
        vo_ref[...] = v2

    blk = pl.BlockSpec((tr, Cc), lambda i: (i, 0))
    args = [w, m, v, ga] + ([gb] if gb is not None else [])
    return pl.pallas_call(
        body, name=name, grid=(R // tr,), in_specs=[blk] * len(args), out_specs=[blk] * 4,
        out_shape=[_sds((R, Cc))] * 4, compiler_params=_cp(("parallel",)))(*args)


def sum4(parts, *, name):
    _, R, Cc = parts.shape
    tr = _row_tile(R, Cc)

    def body(p_ref, out_ref):
        f = lambda t: t.astype(F32)
        out_ref[...] = ((f(p_ref[0]) + f(p_ref[1])) + f(p_ref[2])) + f(p_ref[3])

    return pl.pallas_call(
        body, name=name, grid=(R // tr,), in_specs=[pl.BlockSpec((4, tr, Cc), lambda i: (0, i, 0))],
        out_specs=pl.BlockSpec((tr, Cc), lambda i: (i, 0)), out_shape=_sds((R, Cc)),
        compiler_params=_cp(("parallel",)))(parts)


def _place():
    return lax.axis_index("x"), lax.axis_index("y"), lax.axis_index("c")


def _other_chips(x, y):
    return [(1 - x, y), (x, 1 - y), (1 - x, 1 - y)]


_ANY = pl.BlockSpec(memory_space=pl.ANY)


def allgather_xy(arrs):
    n = len(arrs)
    axes = [0 if a.shape[0] % 2 == 0 else 1 for a in arrs]
    halves = [a.shape[ax] // 2 for a, ax in zip(arrs, axes)]

    def half_of(ref, a, which):
        part = pl.ds(which * halves[a], halves[a])
        return ref.at[part] if axes[a] == 0 else ref.at[:, part]

    def body(*refs):
        ins, outs = refs[:n], refs[n:2 * n]
        send, recv, fsend, frecv, loc = refs[2 * n:]
        x, y, c = _place()
        me = 2 * x + y
        peers = _other_chips(x, y)
        local = [pltpu.make_async_copy(ins[a], outs[a].at[me], loc.at[a]) for a in range(n)]
        for cp in local:
            cp.start()

        def over_ici(a, j, block, src=None):
            px, py = peers[j]
            dst = half_of(outs[a].at[block], a, c)
            return pltpu.make_async_remote_copy(
                src_ref=dst if src is None else half_of(src, a, c), dst_ref=dst, send_sem=send.at[3 * a + j],
                recv_sem=recv.at[3 * a + j], device_id=(px, py, c), device_id_type=MESH)

        def over_d2d(a, j, half):
            px, py = peers[j]
            rows = half_of(outs[a].at[2 * px + py], a, half)
            return pltpu.make_async_remote_copy(
                src_ref=rows, dst_ref=rows, send_sem=fsend.at[3 * a + j], recv_sem=frecv.at[3 * a + j],
                device_id=(x, y, 1 - c), device_id_type=MESH)

        sends = [over_ici(a, j, me, src=ins[a]) for a in range(n) for j in range(3)]
        for cp in sends:
            cp.start()
        passed = []
        for a in range(n):
            for j, (px, py) in enumerate(peers):
                over_ici(a, j, 2 * px + py).wait_recv()
                passed.append(over_d2d(a, j, c))
                passed[-1].start()
        for a in range(n):
            for j in range(3):
                over_d2d(a, j, 1 - c).wait_recv()
        for cp in sends + passed:
            cp.wait_send()
        for cp in local:
            cp.wait()

    return pl.pallas_call(
        body, name="allgather_xy", in_specs=[_ANY] * n, out_specs=[_ANY] * n,
        out_shape=[_sds((4,) + a.shape, a.dtype) for a in arrs],
        scratch_shapes=[pltpu.SemaphoreType.DMA((3 * n,))] * 4 + [pltpu.SemaphoreType.DMA((n,))],
        compiler_params=pltpu.CompilerParams(has_side_effects=True))(*arrs)


_HBM = pl.BlockSpec(memory_space=pltpu.HBM)
_SEM = pl.BlockSpec(memory_space=pltpu.SEMAPHORE)
_EFFECT = pltpu.SideEffectType.DATAFLOW_SIDE_EFFECTING


def _xy_copy(kind, layer, src, land, send, recv, a, j, c, arriving):
    x, y = lax.axis_index("x"), lax.axis_index("y")
    px, py = ((1 - x, y), (x, 1 - y), (1 - x, 1 - y), (x, y))[j]
    if kind == "gather":
        dst = land.at[2 * px + py] if arriving else land.at[2 * x + y]
        src_view = dst if arriving else src
    else:
        dst = land.at[j, layer]
        src_view = src.at[2 * px + py]
    return pltpu.make_async_remote_copy(src_ref=src_view, dst_ref=dst, send_sem=send.at[4 * a + j],
                                        recv_sem=recv.at[4 * a + j], device_id=(px, py, c), device_id_type=MESH)


def xy_start(kind, name, arrs, after, zones=None, layer=0):
    n = len(arrs)
    if zones is None:
        shape = lambda a: ((4,) + a.shape) if kind == "gather" else ((4, DEPTH) + a.shape[1:])
        zones = [lax.empty(shape(a), a.dtype) for a in arrs]

    def body(*refs):
        ins, lands = refs[:n], refs[n:2 * n]
        send, recv = refs[2 * n + 1], refs[2 * n + 2]
        token = refs[-1]
        c = lax.axis_index("c")
        for a in range(n):
            for j in range(4):
                _xy_copy(kind, layer, ins[a], lands[a], send, recv, a, j, c, False).start()
        token[...] = jnp.zeros_like(token)

    hbm = lambda v: pltpu.with_memory_space_constraint(v, pltpu.HBM)
    outs = pl.pallas_call(
        body, name=name,
        out_shape=(pltpu.SemaphoreType.DMA((4 * n,)), pltpu.SemaphoreType.DMA((4 * n,)),
                   *[pltpu.HBM(a.shape, a.dtype) for a in arrs], *[pltpu.HBM(z.shape, z.dtype) for z in zones],
                   _sds((8, 128))),
        in_specs=[_HBM] * (2 * n) + [_ANY],
        out_specs=(_SEM, _SEM, *[_HBM] * (2 * n), pl.BlockSpec(memory_space=pltpu.VMEM)),
        input_output_aliases={i: 2 + i for i in range(2 * n)},
        compiler_params=pltpu.CompilerParams(has_side_effects=_EFFECT))(
            *[hbm(a) for a in arrs], *[hbm(z) for z in zones], after)
    return (kind, layer, outs[0], outs[1], list(outs[2:2 + n])), list(outs[2 + n:2 + 2 * n]), outs[-1]


def xy_wait(name, handles, zones, after):
    kind, layer, send_sems, recv_sems, srcs = handles
    n = len(srcs)

    def body(*refs):
        ins, lands = refs[:n], refs[n:2 * n]
        send, recv = refs[2 * n], refs[2 * n + 1]
        c = lax.axis_index("c")
        for a in range(n):
            for j in range(4):
                _xy_copy(kind, layer, ins[a], lands[a], send, recv, a, j, c, False).wait_send()
                _xy_copy(kind, layer, ins[a], lands[a], send, recv, a, j, c, True).wait_recv()

    outs = pl.pallas_call(
        body, name=name,
        out_shape=tuple(pltpu.HBM(v.shape, v.dtype) for v in srcs + zones),
        in_specs=[_HBM] * (2 * n) + [_SEM, _SEM, _ANY], out_specs=tuple([_HBM] * (2 * n)),
        input_output_aliases={i: i for i in range(2 * n)},
        compiler_params=pltpu.CompilerParams(has_side_effects=_EFFECT))(*srcs, *zones, send_sems, recv_sems, after)
    return list(outs[n:])


def swap_c(arrs):
    n = len(arrs)

    def body(*refs):
        ins, outs = refs[:n], refs[n:2 * n]
        send, recv = refs[2 * n:]
        x, y, c = _place()
        cps = [pltpu.make_async_remote_copy(src_ref=ins[a], dst_ref=outs[a], send_sem=send.at[a], recv_sem=recv.at[a],
                                            device_id=(x, y, 1 - c), device_id_type=MESH) for a in range(n)]
        for cp in cps:
            cp.start()
        for cp in cps:
            cp.wait_recv()
        for cp in cps:
            cp.wait_send()

    return pl.pallas_call(
        body, name="swap_c", in_specs=[_ANY] * n, out_specs=[_ANY] * n, out_shape=[_sds(a.shape, a.dtype) for a in arrs],
        scratch_shapes=[pltpu.SemaphoreType.DMA((n,)), pltpu.SemaphoreType.DMA((n,))],
        compiler_params=pltpu.CompilerParams(has_side_effects=True))(*arrs)


def allreduce_small(v):
    R = v.shape[0]

    def body(v_ref, o_ref, buf, send, recv):
        x, y, c = _place()
        me = 4 * x + 2 * y + c
        buf[0] = v_ref[...]

        def cp(k):
            dx, dy, dc = (k >> 2) & 1, (k >> 1) & 1, k & 1
            return pltpu.make_async_remote_copy(
                src_ref=v_ref, dst_ref=buf.at[k], send_sem=send.at[k - 1], recv_sem=recv.at[k - 1],
                device_id=(x ^ dx, y ^ dy, c ^ dc), device_id_type=MESH)

        cps = [cp(k) for k in range(1, 8)]
        for t in cps:
            t.start()
        for t in cps:
            t.wait_recv()
        acc = buf[me]
        for dev in range(1, 8):
            acc = acc + buf[jnp.bitwise_xor(me, dev)]
        o_ref[...] = acc
        for t in cps:
            t.wait_send()

    vm = pl.BlockSpec(memory_space=pltpu.VMEM)
    return pl.pallas_call(
        body, name="allreduce_small", in_specs=[vm], out_specs=vm, out_shape=_sds((R, 128)),
        scratch_shapes=[pltpu.VMEM((8, R, 128), F32), pltpu.SemaphoreType.DMA((7,)), pltpu.SemaphoreType.DMA((7,))],
        compiler_params=pltpu.CompilerParams(has_side_effects=True, vmem_limit_bytes=VMEM_LIMIT))(v)


def _rows8(*rows):
    n = rows[0].shape[-1]
    t = jnp.stack([r.reshape(n).astype(F32) for r in rows])
    return jnp.pad(t, ((0, 8 - len(rows)), (0, 0)))


def _lanes128(a):
    f = a.reshape(-1).astype(F32)
    return jnp.pad(f, (0, 128 - f.shape[0]))


def _layer_fwd(x, xb, W):
    pm = mm_nn(xb, W["w_main"], name="proj_main", tm=2048)
    pab = mm_nn(xb, W["w_ab"], name="proj_ab")
    q, k, v, gb = pre_qkv_fwd(pm, pab, W["cw"], W["gp"])
    sc = pre_sc_fwd(pm, W["csc"])
    u, w, qd, kd, pp, ys, eg = gdn_intra_fwd(q, k, v, gb)
    o2, s0 = gdn_scan_fwd(u, w, qd, kd, pp, eg)
    og = post_fwd(o2, pm, W["nw"])
    ya = mm_nn(og, W["w_og"], name="proj_og")
    yb = mm_nn(sc, W["w_osc"], name="proj_osc")
    mixed = merge_fwd(ya, yb, pm)
    out = mm_nn(mixed, W["w_out"], name="proj_out")
    x1, x1b = ln_fwd(x, out, W["ln1"], name="ln1_fwd")
    hfac, h = mm_nn(x1b, W["w_up"], bias=W["b_up"], relu2=True, out_dtype=MM, name="mlp_up", tm=2048)
    dn = mm_nn(h, W["w_down"], name="mlp_down", tm=2048)
    x2, x2b = ln_fwd(x1, dn, W["ln2"], name="ln2_fwd")
    saved = dict(x=x, xb=xb, pm=pm, pab=pab, q=q, k=k, v=v, gb=gb, sc=sc, o2=o2, s0=s0, og=og, ya=ya, yb=yb,
                 u=u, w=w, qd=qd, kd=kd, pp=pp, ys=ys, eg=eg,
                 mixed=mixed, out=out, x1=x1, x1b=x1b, hfac=hfac, h=h, dn=dn)
    return x2, x2b, saved


def _layer_bwd(ct, W, S, early=None):
    dxa2, dr2b, dp2 = ln_bwd(S["x1"], S["dn"], W["ln2"], ct, name="ln2_bwd")
    g_down = mm_tn(S["h"], dr2b, name="dw_down", tm=2048)
    dhpre, db_up = mm_nt(dr2b, W["w_down"], dact=S["hfac"], out_dtype=MM, name="mlp_down_bwd", tm=2048)
    g_up = mm_tn(S["x1b"], dhpre, name="dw_up", tn=2048)
    dx1 = mm_nt(dhpre, W["w_up"], add=dxa2, name="mlp_up_bwd", tk=2048)
    dxa1, dr1b, dp1 = ln_bwd(S["x"], S["out"], W["ln1"], dx1, name="ln1_bwd")
    g_out = mm_tn(S["mixed"], dr1b, name="dw_out")
    dmix = mm_nt(dr1b, W["w_out"], name="proj_out_bwd")
    dya, dyb, dpm = merge_bwd(S["ya"], S["yb"], S["pm"], dmix)
    g_og = mm_tn(S["og"], dya, name="dw_og")
    g_osc = mm_tn(S["sc"], dyb, name="dw_osc")
    dog = mm_nt(dya, W["w_og"], name="proj_og_bwd")
    dsc = mm_nt(dyb, W["w_osc"], name="proj_osc_bwd")
    token = None if early is None else early(dict(w_o_gdn=g_og, w_o_sc=g_osc, w_out=g_out, w_up=g_up, w_down=g_down))
    nw = W["nw"] if token is None else W["nw"] + token[0, 0]
    do, dpm, dnw = post_bwd(S["o2"], S["pm"], nw, dog, dpm)
    du, dw, dqd, dkd, dpp, deg = gdn_scan_bwd(S["u"], S["w"], S["qd"], S["kd"], S["pp"], S["eg"], S["s0"], do)
    dq2, dk2, dv2, dgb2 = gdn_intra_bwd(S["q"], S["k"], S["v"], S["gb"], S["ys"], du, dw, dqd, dkd, dpp, deg)
    dpm, dpab, dcw, dgp = pre_qkv_bwd(S["pm"], S["pab"], W["cw"], W["gp"], dq2, dk2, dv2, dgb2, dpm)
    dpm, dcsc = pre_sc_bwd(S["pm"], W["csc"], dsc, dpm)
    g_main = mm_tn(S["xb"], dpm, name="dw_main", tn=NMAIN // 4)
    g_ab = mm_tn(S["xb"], dpab, name="dw_ab")
    t = mm_nt(dpab, W["w_ab"], add=dxa1, name="proj_ab_bwd")
    dx = mm_nt(dpm, W["w_main"], add=t, name="proj_main_bwd", tk=NMAIN // 4)
    g_in = jnp.concatenate([g_main[:, :3 * D], g_main[:, 8 * D:], g_ab[:, :4 * NH], g_main[:, 3 * D:8 * D]], axis=1)
    grads = dict(
        w_in=g_in, w_o_gdn=g_og, w_o_sc=g_osc, w_out=g_out, w_up=g_up, w_down=g_down,
        conv_qkv=dcw[:3], conv_sc=dcsc[:3], a_log=dgp[0, :2 * NH].reshape(2, NH), dt_bias=dgp[1, :2 * NH].reshape(2, NH),
        gdn_norm_w=dnw[0], ln1_g=dp1[1], ln1_b=dp1[2], b_up=db_up[0], b_down=dp2[0], ln2_g=dp2[1], ln2_b=dp2[2])
    return dx, grads


def _layer_weights(l, full, i, conv, a_log, dt_bias, gdn_norm_w, ln1_g, ln1_b, b_up, b_down, ln2_g, ln2_b):
    w_in = full["w_in"][i]
    w_main = jnp.concatenate([w_in[:, :3 * D], w_in[:, 4 * D + 4 * NH:], w_in[:, 3 * D:4 * D]], axis=1)
    w_ab = jnp.pad(w_in[:, 4 * D:4 * D + 4 * NH], ((0, 0), (0, 128 - 4 * NH)))
    return dict(
        w_main=w_main, w_ab=w_ab, w_og=full["w_o_gdn"][i], w_osc=full["w_o_sc"][i], w_out=full["w_out"][i],
        w_up=full["w_up"][i], w_down=full["w_down"][i],
        cw=jnp.pad(conv["conv_qkv"][l].astype(F32), ((0, 5), (0, 0))),
        csc=jnp.pad(conv["conv_sc"][l].astype(F32), ((0, 5), (0, 0))),
        gp=_rows8(_lanes128(a_log[l]), _lanes128(dt_bias[l])), nw=_rows8(gdn_norm_w[l]),
        ln1=_rows8(jnp.zeros((D,), F32), ln1_g[l], ln1_b[l]), ln2=_rows8(b_down[l], ln2_g[l], ln2_b[l]),
        b_up=b_up[l].reshape(1, DFF).astype(F32))


def local_step(xs, target, weights_of, after_bwd, mid_bwd=None):
    x, xb = xs, xs.astype(MM)
    Ws, saved = [], []
    for l in range(DEPTH):
        Ws.append(weights_of(l, x))
        x, xb, S = _layer_fwd(x, xb, Ws[l])
        saved.append(S)
    loss_tile, ct = loss_fwd_bwd(x, target)
    token = None
    for l in reversed(range(DEPTH)):
        W = Ws[l] if token is None else dict(Ws[l], ln2=Ws[l]["ln2"] + token[0, 0])
        ct, grads = _layer_bwd(ct, W, saved[l], None if mid_bwd is None else (lambda part, l=l: mid_bwd(l, part)))
        token = after_bwd(l, grads)
    return loss_tile, ct


EARLY = 1
BIG = ("w_in", "w_o_gdn", "w_o_sc", "w_out", "w_up", "w_down")
SMALL = ("conv_qkv", "a_log", "dt_bias", "gdn_norm_w", "conv_sc", "ln1_g", "ln1_b", "b_up", "b_down", "ln2_g", "ln2_b")
ORDER = ("w_in", "conv_qkv", "a_log", "dt_bias", "gdn_norm_w", "w_o_gdn", "conv_sc", "w_o_sc", "w_out", "ln1_g",
         "ln1_b", "w_up", "b_up", "w_down", "b_down", "ln2_g", "ln2_b")


def _pack(arrs):
    flat = jnp.concatenate([a.reshape(-1).astype(F32) for a in arrs])
    n = flat.shape[0]
    rows = -(-n // 1024) * 8
    return jnp.pad(flat, (0, rows * 128 - n)).reshape(rows, 128)


def _unpack(buf, like):
    flat = buf.reshape(-1)
    out, o = [], 0
    for a in like:
        n = 1
        for s in a.shape:
            n *= s
        out.append(flat[o:o + n].reshape(a.shape))
        o += n
    return out


def _gathered(name, g):
    if name in ("w_in", "w_up", "conv_qkv", "conv_sc"):
        t = jnp.moveaxis(g, 0, -2)
        return t.reshape(t.shape[:-2] + (t.shape[-2] * t.shape[-1],))
    t = jnp.moveaxis(g, 0, 1)
    return t.reshape((t.shape[0], t.shape[1] * t.shape[2]) + t.shape[3:])


def _by_chip(name, g):
    if name in ("w_in", "w_up"):
        r, ccols = g.shape
        return jnp.moveaxis(g.reshape(r, 4, ccols // 4), 1, 0)
    return g.reshape((4, g.shape[0] // 4) + g.shape[1:])


def kernel(x, w_in, conv_qkv, a_log, dt_bias, gdn_norm_w, w_o_gdn, conv_sc, w_o_sc, w_out, ln1_g, ln1_b, w_up, b_up, w_down, b_down, ln2_g, ln2_b, loss_target, m_w_in, m_conv_qkv, m_a_log, m_dt_bias, m_gdn_norm_w, m_w_o_gdn, m_conv_sc, m_w_o_sc, m_w_out, m_ln1_g, m_ln1_b, m_w_up, m_b_up, m_w_down, m_b_down, m_ln2_g, m_ln2_b, v_w_in, v_conv_qkv, v_a_log, v_dt_bias, v_gdn_norm_w, v_w_o_gdn, v_conv_sc, v_w_o_sc, v_w_out, v_ln1_g, v_ln1_b, v_w_up, v_b_up, v_w_down, v_b_down, v_ln2_g, v_ln2_b):
    w = dict(w_in=w_in, conv_qkv=conv_qkv, a_log=a_log, dt_bias=dt_bias, gdn_norm_w=gdn_norm_w, w_o_gdn=w_o_gdn,
             conv_sc=conv_sc, w_o_sc=w_o_sc, w_out=w_out, ln1_g=ln1_g, ln1_b=ln1_b, w_up=w_up, b_up=b_up,
             w_down=w_down, b_down=b_down, ln2_g=ln2_g, ln2_b=ln2_b)
    m = dict(w_in=m_w_in, conv_qkv=m_conv_qkv, a_log=m_a_log, dt_bias=m_dt_bias, gdn_norm_w=m_gdn_norm_w,
             w_o_gdn=m_w_o_gdn, conv_sc=m_conv_sc, w_o_sc=m_w_o_sc, w_out=m_w_out, ln1_g=m_ln1_g, ln1_b=m_ln1_b,
             w_up=m_w_up, b_up=m_b_up, w_down=m_w_down, b_down=m_b_down, ln2_g=m_ln2_g, ln2_b=m_ln2_b)
    v = dict(w_in=v_w_in, conv_qkv=v_conv_qkv, a_log=v_a_log, dt_bias=v_dt_bias, gdn_norm_w=v_gdn_norm_w,
             w_o_gdn=v_w_o_gdn, conv_sc=v_conv_sc, w_o_sc=v_w_o_sc, w_out=v_w_out, ln1_g=v_ln1_g, ln1_b=v_ln1_b,
             w_up=v_w_up, b_up=v_b_up, w_down=v_w_down, b_down=v_b_down, ln2_g=v_ln2_g, ln2_b=v_ln2_b)
    chip = 2 * lax.axis_index("x") + lax.axis_index("y")

    names = BIG + ("conv_qkv", "conv_sc")
    blocks = [w[n].astype(MM) if n in BIG else w[n] for n in names]
    got = allgather_xy([b[:EARLY] if n in BIG else b for n, b in zip(names, blocks)])
    early = {n: _gathered(n, g) for n, g in zip(names, got)}
    gather, gather_zones, token = xy_start("gather", "gather_start", [b[EARLY:] for b in blocks[:len(BIG)]],
                                           after=got[0])
    vectors = (a_log, dt_bias, gdn_norm_w, ln1_g, ln1_b, b_up, b_down, ln2_g, ln2_b)
    late, grads = {}, [None] * DEPTH
    scatters = {"mid": [None] * DEPTH, "end": [None] * DEPTH}
    zones = {"mid": None, "end": None}

    def weights_of(l, x_l):
        if l < EARLY:
            return _layer_weights(l, early, l, early, *vectors)
        if not late:
            for n, zone in zip(BIG, xy_wait("gather_wait", gather, gather_zones, after=x_l)):
                late[n] = _gathered(n, zone)
        return _layer_weights(l, late, l - EARLY, early, *vectors)

    def scatter(when, l, g, which):
        scatters[when][l], zones[when], tok = xy_start(
            "scatter", "scatter_%s_start_%d" % (when, l), [_by_chip(n, g[n]).astype(MM) for n in which], after=a_log,
            zones=zones[when], layer=l)
        return tok

    def after_bwd(l, g):
        grads[l] = g
        return scatter("end", l, g, BIG[:1])

    loss_tile, dx = local_step(x[0] + token[0, 0], loss_target[0], weights_of, after_bwd,
                               lambda l, g: scatter("mid", l, g, BIG[1:]))
    loss = lax.psum(loss_tile[0, 0], ("x", "y", "c"))

    for when in ("mid", "end"):
        for l in range(DEPTH):
            zones[when] = xy_wait("scatter_%s_wait_%d" % (when, l), scatters[when][l], zones[when], after=dx)
    part = [sum4(z.reshape(4, -1, z.shape[-1]), name="sum_" + n) for n, z in zip(BIG, zones["end"] + zones["mid"])]
    other = swap_c(part)
    out = {}
    for n, mine, theirs in zip(BIG, part, other):
        cols = mine.shape[-1]
        res = adam(w[n].reshape(-1, cols), m[n].reshape(-1, cols), v[n].reshape(-1, cols), mine, theirs, name="adam_" + n)
        out[n] = [r.reshape(w[n].shape) for r in res]

    stacked = [jnp.stack([grads[l][n] for l in range(DEPTH)]) for n in SMALL]
    summed = _unpack(allreduce_small(_pack(stacked)), stacked)
    gs = []
    for n, g in zip(SMALL, summed):
        if n in ("conv_qkv", "conv_sc"):
            blk = w[n].shape[-1]
            g = lax.dynamic_slice_in_dim(g, chip * blk, blk, axis=2)
        gs.append(g)
    res = adam(_pack([w[n] for n in SMALL]), _pack([m[n] for n in SMALL]), _pack([v[n] for n in SMALL]), _pack(gs),
               name="adam_small")
    for n, parts in zip(SMALL, zip(*[_unpack(r, gs) for r in res])):
        out[n] = list(parts)

    outs = [loss, dx[None]]
    for kind in range(4):
        outs += [out[n][kind] for n in ORDER]
    return tuple(outs)
```

```python
import jax
import jax.numpy as jnp
from jax import lax
from jax.experimental import pallas as pl
from jax.experimental.pallas import tpu as pltpu

F32 = jnp.float32
MM = jnp.bfloat16
HI = lax.Precision.HIGHEST

D = 1024
NH = 8
HD = 128
CH = 64
DFF = 4 * D
DEPTH = 4
LN_EPS = 1e-5
RMS_EPS = 1e-6
L2_EPS = 1e-6
ALPHA = (2 * DEPTH) ** 0.25
LR, B1, B2, EPS, WD, STEP = 0.001, 0.9, 0.999, 1e-08, 0.01, 10

NMAIN = 9 * D
CQ, CK, CV, CSB, CSC, CSX, CGA, CGB, CZ = range(9)
HALO = 8
VMEM_LIMIT = 56 * 1024 * 1024
MESH = pl.DeviceIdType.MESH


def _cp(sem=None, vmem=VMEM_LIMIT):
    return pltpu.CompilerParams(dimension_semantics=sem, vmem_limit_bytes=vmem)


def _sds(shape, dtype=F32):
    return jax.ShapeDtypeStruct(tuple(shape), dtype)


def _accumulate(acc, product, k, nk, finish):
    if nk == 1:
        finish(product())
        return

    @pl.when(k == 0)
    def _():
        acc[...] = jnp.zeros_like(acc)

    acc[...] += product()

    @pl.when(k == nk - 1)
    def _():
        finish(acc[...])


def mm_nn(a, b, *, name, bias=None, relu2=False, add=None, out_dtype=F32, tm=1024, tn=1024, tk=1024):
    M, K = a.shape
    N = b.shape[1]
    tm, tn, tk = min(tm, M), min(tn, N), min(tk, K)
    nk = K // tk

    def body(*refs):
        it = iter(refs)
        a_ref, b_ref = next(it), next(it)
        bias_ref = next(it) if bias is not None else None
        add_ref = next(it) if add is not None else None
        o_ref = next(it)
        h_ref = next(it) if relu2 else None
        acc = next(it) if nk > 1 else None
        prod = lambda: jnp.dot(a_ref[...].astype(MM), b_ref[...].astype(MM), preferred_element_type=F32)

        def finish(r):
            if bias_ref is not None:
                r = r + bias_ref[...]
            if add_ref is not None:
                r = r + add_ref[...]
            if relu2:
                t = jnp.maximum(r, 0.0)
                o_ref[...] = (2.0 * t).astype(o_ref.dtype)
                h_ref[...] = (t * t).astype(h_ref.dtype)
            else:
                o_ref[...] = r.astype(o_ref.dtype)

        _accumulate(acc, prod, pl.program_id(2), nk, finish)

    in_specs = [pl.BlockSpec((tm, tk), lambda i, j, k: (i, k)), pl.BlockSpec((tk, tn), lambda i, j, k: (k, j))]
    args = [a, b]
    if bias is not None:
        in_specs.append(pl.BlockSpec((1, tn), lambda i, j, k: (0, j)))
        args.append(bias)
    if add is not None:
        in_specs.append(pl.BlockSpec((tm, tn), lambda i, j, k: (i, j)))
        args.append(add)
    out_shape = [_sds((M, N), out_dtype)]
    out_specs = [pl.BlockSpec((tm, tn), lambda i, j, k: (i, j))]
    if relu2:
        out_shape.append(_sds((M, N), MM))
        out_specs.append(pl.BlockSpec((tm, tn), lambda i, j, k: (i, j)))
    res = pl.pallas_call(
        body, name=name, grid=(M // tm, N // tn, nk), in_specs=in_specs, out_specs=out_specs, out_shape=out_shape,
        scratch_shapes=[pltpu.VMEM((tm, tn), F32)] if nk > 1 else [],
        compiler_params=_cp(("parallel", "parallel", "arbitrary")))(*args)
    return res if relu2 else res[0]


def mm_nt(a, b, *, name, add=None, dact=None, out_dtype=F32, tm=1024, tn=1024, tk=1024):
    M, Nc = a.shape
    Ko = b.shape[0]
    tm, tn, tk = min(tm, M), min(tn, Ko), min(tk, Nc)
    nk = Nc // tk
    ni = M // tm

    def body(*refs):
        it = iter(refs)
        a_ref, b_ref = next(it), next(it)
        add_ref = next(it) if add is not None else None
        d_ref = next(it) if dact is not None else None
        o_ref = next(it)
        db_ref = next(it) if dact is not None else None
        acc = next(it) if nk > 1 else None
        i = pl.program_id(1)
        prod = lambda: lax.dot_general(a_ref[...].astype(MM), b_ref[...].astype(MM), (((1,), (1,)), ((), ())),
                                       preferred_element_type=F32)

        def finish(r):
            if add_ref is not None:
                r = r + add_ref[...]
            if d_ref is not None:
                r = r * d_ref[...].astype(F32)
                s = jnp.sum(r, axis=0, keepdims=True)
                row0 = lax.broadcasted_iota(jnp.int32, db_ref.shape, 0) == 0

                @pl.when(i == 0)
                def _():
                    db_ref[...] = jnp.zeros_like(db_ref)

                db_ref[...] += jnp.where(row0, s, 0.0)
            o_ref[...] = r.astype(o_ref.dtype)

        _accumulate(acc, prod, pl.program_id(2), nk, finish)

    in_specs = [pl.BlockSpec((tm, tk), lambda j, i, k: (i, k)), pl.BlockSpec((tn, tk), lambda j, i, k: (j, k))]
    args = [a, b]
    for extra in (add, dact):
        if extra is not None:
            in_specs.append(pl.BlockSpec((tm, tn), lambda j, i, k: (i, j)))
            args.append(extra)
    out_shape = [_sds((M, Ko), out_dtype)]
    out_specs = [pl.BlockSpec((tm, tn), lambda j, i, k: (i, j))]
    if dact is not None:
        out_shape.append(_sds((8, Ko), F32))
        out_specs.append(pl.BlockSpec((8, tn), lambda j, i, k: (0, j)))
    res = pl.pallas_call(
        body, name=name, grid=(Ko // tn, ni, nk), in_specs=in_specs, out_specs=out_specs, out_shape=out_shape,
        scratch_shapes=[pltpu.VMEM((tm, tn), F32)] if nk > 1 else [],
        compiler_params=_cp(("parallel", "arbitrary", "arbitrary")))(*args)
    return res if dact is not None else res[0]


def mm_tn(a, b, *, name, tm=1024, tn=1024, tk=1024):
    T, M = a.shape
    N = b.shape[1]
    tm, tn, tk = min(tm, M), min(tn, N), min(tk, T)

    def body(a_ref, b_ref, o_ref):
        @pl.when(pl.program_id(2) == 0)
        def _():
            o_ref[...] = jnp.zeros_like(o_ref)

        o_ref[...] += lax.dot_general(a_ref[...].astype(MM), b_ref[...].astype(MM), (((0,), (0,)), ((), ())),
                                      preferred_element_type=F32)

    return pl.pallas_call(
        body, name=name, grid=(M // tm, N // tn, T // tk),
        in_specs=[pl.BlockSpec((tk, tm), lambda i, j, k: (k, i)), pl.BlockSpec((tk, tn), lambda i, j, k: (k, j))],
        out_specs=pl.BlockSpec((tm, tn), lambda i, j, k: (i, j)), out_shape=_sds((M, N)),
        compiler_params=_cp(("parallel", "parallel", "arbitrary")))(a, b)


def _sigmoid(x):
    return 1.0 / (1.0 + jnp.exp(-x))


def _silu(x):
    return x * _sigmoid(x)


def _softplus(x):
    return jnp.maximum(x, 0.0) + jnp.log1p(jnp.exp(-jnp.abs(x)))


def _ext(main_ref, prev_ref, next_ref, first, last):
    p = jnp.where(first, 0.0, prev_ref[...].astype(F32))
    n = jnp.where(last, 0.0, next_ref[...].astype(F32))
    return jnp.concatenate([p, main_ref[...].astype(F32), n], axis=0)


def _shift_dn(x):
    return pltpu.roll(x, 1, 0)


def _shift_up(x):
    return pltpu.roll(x, x.shape[0] - 1, 0)


def _conv3(xe, w):
    return w[0:1, :] * _shift_dn(xe) + w[1:2, :] * xe + w[2:3, :] * _shift_up(xe)


def _conv3_t(de, w):
    return w[0:1, :] * _shift_up(de) + w[1:2, :] * de + w[2:3, :] * _shift_dn(de)


def _halo_specs(bt, T, col, lead=None):
    r = bt // HALO
    last = T // HALO - 1
    if lead is None:
        return [pl.BlockSpec((bt, D), lambda i: (i, col)),
                pl.BlockSpec((HALO, D), lambda i: (jnp.maximum(i * r - 1, 0), col)),
                pl.BlockSpec((HALO, D), lambda i: (jnp.minimum((i + 1) * r, last), col))]
    return [pl.BlockSpec((lead, bt, D), lambda i: (0, i, col)),
            pl.BlockSpec((lead, HALO, D), lambda i: (0, jnp.maximum(i * r - 1, 0), col)),
            pl.BlockSpec((lead, HALO, D), lambda i: (0, jnp.minimum((i + 1) * r, last), col))]


def _qkv_rows(cq, ck, cv):
    sq, sk, sv = _silu(cq), _silu(ck), _silu(cv)
    qs, ks = [], []
    for h in range(NH):
        s = slice(h * HD, (h + 1) * HD)
        qh, kh = sq[:, s], sk[:, s]
        qs.append(qh * lax.rsqrt(jnp.sum(qh * qh, axis=-1, keepdims=True) + L2_EPS) * (HD ** -0.5))
        ks.append(kh * lax.rsqrt(jnp.sum(kh * kh, axis=-1, keepdims=True) + L2_EPS))
    return jnp.concatenate(qs, axis=1), jnp.concatenate(ks, axis=1), sv


def _chunk_masks(bt):
    row = lax.broadcasted_iota(jnp.int32, (bt, bt), 0)
    col = lax.broadcasted_iota(jnp.int32, (bt, bt), 1)
    same = (row // CH) == (col // CH)
    lower = jnp.where(same & (col <= row), 1.0, 0.0).astype(F32)
    upper = jnp.where(same & (col >= row), 1.0, 0.0).astype(F32)
    return lower, upper


def _gate_rows(ab, gp, lower, upper):
    lane = lax.broadcasted_iota(jnp.int32, ab.shape, 1)
    g = -jnp.exp(gp[0:1, :]) * _softplus(ab + gp[1:2, :])
    g = jnp.where(lane < 2 * NH, g, 0.0)
    gf = jnp.dot(lower, g, precision=HI, preferred_element_type=F32)
    gr = jnp.dot(upper, g, precision=HI, preferred_element_type=F32)
    gc = jnp.where(lane < NH, gf, gr)
    beta = _sigmoid(ab)
    return jnp.where(lane < 2 * NH, gc, jnp.where(lane < 4 * NH, beta, 0.0))


def pre_qkv_fwd(pm, pab, cw, gp, *, bt=256):
    T = pm.shape[0]
    bt = min(bt, T)
    n = T // bt

    def body(q0, q1, q2, k0, k1, k2, v0, v1, v2, ab_ref, cw_ref, gp_ref, q_ref, k_ref, v_ref, gb_ref):
        i = pl.program_id(0)
        first, last = i == 0, i == n - 1
        cs = []
        for c, (m, p, x) in enumerate(((q0, q1, q2), (k0, k1, k2), (v0, v1, v2))):
            xe = _ext(m, p, x, first, last)
            cs.append(_conv3(xe, cw_ref[:, c * D:(c + 1) * D])[HALO:HALO + bt])
        q, k, v = _qkv_rows(*cs)
        q_ref[...], k_ref[...], v_ref[...] = q, k, v
        lower, upper = _chunk_masks(bt)
        gb_ref[...] = _gate_rows(ab_ref[...], gp_ref[...], lower, upper)

    in_specs = (_halo_specs(bt, T, CQ) + _halo_specs(bt, T, CK) + _halo_specs(bt, T, CV)
                + [pl.BlockSpec((bt, 128), lambda i: (i, 0)), pl.BlockSpec((8, 3 * D), lambda i: (0, 0)),
                   pl.BlockSpec((8, 128), lambda i: (0, 0))])
    row = pl.BlockSpec((bt, D), lambda i: (i, 0))
    return pl.pallas_call(
        body, name="pre_qkv_fwd", grid=(n,), in_specs=in_specs,
        out_specs=[row, row, row, pl.BlockSpec((bt, 128), lambda i: (i, 0))],
        out_shape=[_sds((T, D)), _sds((T, D)), _sds((T, D)), _sds((T, 128))],
        compiler_params=_cp(("parallel",)))(*([pm] * 9), pab, cw, gp)


def pre_qkv_bwd(pm, pab, cw, gp, dq2, dk2, dv2, dgb2, dpm, *, bt=256):
    T = pm.shape[0]
    bt = min(bt, T)
    n = T // bt
    E = bt + 2 * HALO

    def body(*refs):
        it = iter(refs)
        xs = [[next(it) for _ in range(3)] for _ in range(3)]
        ds = [[next(it) for _ in range(3)] for _ in range(3)]
        ab_ref, dgb_ref, cw_ref, gp_ref, _alias = next(it), next(it), next(it), next(it), next(it)
        o_ref, dab_ref, dcw_ref, dgp_ref = (next(it) for _ in range(4))
        i = pl.program_id(0)
        first, last = i == 0, i == n - 1

        @pl.when(first)
        def _():
            dcw_ref[...] = jnp.zeros_like(dcw_ref)
            dgp_ref[...] = jnp.zeros_like(dgp_ref)

        xes = [_ext(*xs[c], first, last) for c in range(3)]
        ces = [_conv3(xes[c], cw_ref[:, c * D:(c + 1) * D]) for c in range(3)]
        cts = []
        for c in range(3):
            m, p, x = ds[c]
            pe = jnp.where(first, 0.0, p[0] + p[1])
            ne = jnp.where(last, 0.0, x[0] + x[1])
            cts.append(jnp.concatenate([pe, m[0] + m[1], ne], axis=0))
        _, vjp = jax.vjp(_qkv_rows, *ces)
        dces = vjp(tuple(cts))
        rowi = lax.broadcasted_iota(jnp.int32, (E, 1), 0)
        central = (rowi >= HALO) & (rowi < HALO + bt)
        row8 = lax.broadcasted_iota(jnp.int32, (8, D), 0)
        for c in range(3):
            w = cw_ref[:, c * D:(c + 1) * D]
            o_ref[:, c * D:(c + 1) * D] = _conv3_t(dces[c], w)[HALO:HALO + bt].astype(o_ref.dtype)
            dc = jnp.where(central, dces[c], 0.0)
            taps = (jnp.sum(dc * _shift_dn(xes[c]), axis=0, keepdims=True),
                    jnp.sum(dc * xes[c], axis=0, keepdims=True),
                    jnp.sum(dc * _shift_up(xes[c]), axis=0, keepdims=True))
            upd = jnp.where(row8 == 0, taps[0], jnp.where(row8 == 1, taps[1], jnp.where(row8 == 2, taps[2], 0.0)))
            dcw_ref[:, c * D:(c + 1) * D] += upd
        lower, upper = _chunk_masks(bt)
        _, gvjp = jax.vjp(lambda ab, gp: _gate_rows(ab, gp, lower, upper), ab_ref[...], gp_ref[...])
        dab, dgp = gvjp(dgb_ref[0] + dgb_ref[1])
        dab_ref[...] = dab
        dgp_ref[...] += dgp

    in_specs = (_halo_specs(bt, T, CQ) + _halo_specs(bt, T, CK) + _halo_specs(bt, T, CV)
                + _halo_specs(bt, T, 0, lead=2) * 3
                + [pl.BlockSpec((bt, 128), lambda i: (i, 0)), pl.BlockSpec((2, bt, 128), lambda i: (0, i, 0)),
                   pl.BlockSpec((8, 3 * D), lambda i: (0, 0)), pl.BlockSpec((8, 128), lambda i: (0, 0)),
                   pl.BlockSpec(memory_space=pl.ANY)])
    out_specs = [pl.BlockSpec((bt, 3 * D), lambda i: (i, 0)), pl.BlockSpec((bt, 128), lambda i: (i, 0)),
                 pl.BlockSpec((8, 3 * D), lambda i: (0, 0)), pl.BlockSpec((8, 128), lambda i: (0, 0))]
    return pl.pallas_call(
        body, name="pre_qkv_bwd", grid=(n,), in_specs=in_specs, out_specs=out_specs,
        out_shape=[_sds(dpm.shape, dpm.dtype), _sds((T, 128)), _sds((8, 3 * D)), _sds((8, 128))],
        input_output_aliases={len(in_specs) - 1: 0},
        compiler_params=_cp(("arbitrary",)))(
            *([pm] * 9), dq2, dq2, dq2, dk2, dk2, dk2, dv2, dv2, dv2, pab, dgb2, cw, gp, dpm)


def pre_sc_fwd(pm, cw, *, bt=256):
    T = pm.shape[0]
    bt = min(bt, T)
    n = T // bt

    def body(b_ref, c0, c1, c2, x0, x1, x2, cw_ref, o_ref):
        i = pl.program_id(0)
        first, last = i == 0, i == n - 1
        pe = _ext(c0, c1, c2, first, last) * _ext(x0, x1, x2, first, last)
        o_ref[...] = (b_ref[...] * _conv3(pe, cw_ref[...])[HALO:HALO + bt]).astype(o_ref.dtype)

    in_specs = ([pl.BlockSpec((bt, D), lambda i: (i, CSB))] + _halo_specs(bt, T, CSC) + _halo_specs(bt, T, CSX)
                + [pl.BlockSpec((8, D), lambda i: (0, 0))])
    return pl.pallas_call(
        body, name="pre_sc_fwd", grid=(n,), in_specs=in_specs, out_specs=pl.BlockSpec((bt, D), lambda i: (i, 0)),
        out_shape=_sds((T, D), MM), compiler_params=_cp(("parallel",)))(*([pm] * 7), cw)


def pre_sc_bwd(pm, cw, dsc, dpm, *, bt=256):
    T = pm.shape[0]
    bt = min(bt, T)
    n = T // bt
    E = bt + 2 * HALO

    def body(b0, b1, b2, c0, c1, c2, x0, x1, x2, d0, d1, d2, cw_ref, _alias, o_ref, dcw_ref):
        i = pl.program_id(0)
        first, last = i == 0, i == n - 1

        @pl.when(first)
        def _():
            dcw_ref[...] = jnp.zeros_like(dcw_ref)

        ce, xe = _ext(c0, c1, c2, first, last), _ext(x0, x1, x2, first, last)
        pe = ce * xe
        w = cw_ref[...]
        dout = d0[...]
        o_ref[:, 0:D] = (dout * _conv3(pe, w)[HALO:HALO + bt]).astype(o_ref.dtype)
        dce = _ext(d0, d1, d2, first, last) * _ext(b0, b1, b2, first, last)
        dp = _conv3_t(dce, w)[HALO:HALO + bt]
        o_ref[:, D:2 * D] = (dp * x0[...]).astype(o_ref.dtype)
        o_ref[:, 2 * D:3 * D] = (dp * c0[...]).astype(o_ref.dtype)
        rowi = lax.broadcasted_iota(jnp.int32, (E, 1), 0)
        dc = jnp.where((rowi >= HALO) & (rowi < HALO + bt), dce, 0.0)
        row8 = lax.broadcasted_iota(jnp.int32, (8, D), 0)
        taps = (jnp.sum(dc * _shift_dn(pe), axis=0, keepdims=True), jnp.sum(dc * pe, axis=0, keepdims=True),
                jnp.sum(dc * _shift_up(pe), axis=0, keepdims=True))
        dcw_ref[...] += jnp.where(row8 == 0, taps[0], jnp.where(row8 == 1, taps[1], jnp.where(row8 == 2, taps[2], 0.0)))

    dsc_specs = [pl.BlockSpec((bt, D), lambda i: (i, 0)),
                 pl.BlockSpec((HALO, D), lambda i: (jnp.maximum(i * (bt // HALO) - 1, 0), 0)),
                 pl.BlockSpec((HALO, D), lambda i: (jnp.minimum((i + 1) * (bt // HALO), T // HALO - 1), 0))]
    in_specs = (_halo_specs(bt, T, CSB) + _halo_specs(bt, T, CSC) + _halo_specs(bt, T, CSX) + dsc_specs
                + [pl.BlockSpec((8, D), lambda i: (0, 0)), pl.BlockSpec(memory_space=pl.ANY)])
    return pl.pallas_call(
        body, name="pre_sc_bwd", grid=(n,), in_specs=in_specs,
        out_specs=[pl.BlockSpec((bt, 3 * D), lambda i: (i, 1)), pl.BlockSpec((8, D), lambda i: (0, 0))],
        out_shape=[_sds(dpm.shape, dpm.dtype), _sds((8, D))], input_output_aliases={len(in_specs) - 1: 0},
        compiler_params=_cp(("arbitrary",)))(*([pm] * 9), dsc, dsc, dsc, cw, dpm)


def _bdot(a, b, dims):
    return lax.dot_general(a.astype(MM), b.astype(MM), (dims, ((), ())), preferred_element_type=F32)


_NN, _NT, _TN = ((1,), (0,)), ((1,), (1,)), ((0,), (0,))


def _raw_nn(a, b):
    return _bdot(a, b, _NN)


def _raw_nt(a, b):
    return _bdot(a, b, _NT)


def _raw_tn(a, b):
    return _bdot(a, b, _TN)


def _make_vjp_ops():
    @jax.custom_vjp
    def nn(a, b):
        return _raw_nn(a, b)

    @jax.custom_vjp
    def nt(a, b):
        return _raw_nt(a, b)

    @jax.custom_vjp
    def tn(a, b):
        return _raw_tn(a, b)

    nn.defvjp(lambda a, b: (_raw_nn(a, b), (a, b)), lambda r, g: (_raw_nt(g, r[1]), _raw_tn(r[0], g)))
    nt.defvjp(lambda a, b: (_raw_nt(a, b), (a, b)), lambda r, g: (_raw_nn(g, r[1]), _raw_tn(g, r[0])))
    tn.defvjp(lambda a, b: (_raw_tn(a, b), (a, b)), lambda r, g: (_raw_nt(r[1], g), _raw_nn(r[0], g)))

    @jax.custom_vjp
    def inv_saved(A, Y):
        return Y

    def inv_bwd(Y, g):
        M = g + _raw_tn(Y, g)
        return -(M + _raw_nt(M, Y)), jnp.zeros_like(Y)

    inv_saved.defvjp(lambda A, Y: (Y, Y), inv_bwd)
    return nn, nt, tn, inv_saved


GH = 2
GR = GH * CH
NG = NH // GH
CBI = 2
CBB = 2


def _scan_chunks(n_chunks):
    return 4 if n_chunks % 4 == 0 else 2


def _tri_inv_y_all(As):
    Ys = [-A for A in As]
    Ps = [_raw_nn(A, A) for A in As]
    for stage in range(5):
        squares = [_raw_nn(P, P) for P in Ps] if stage < 4 else None
        Ys = [Y + P + _raw_nn(Y, P) for Y, P in zip(Ys, Ps)]
        Ps = squares
    return Ys


def _intra_groups(chains, incl, strict, eye, lastc, ops, inv_all):
    nn, nt, tn = ops
    st = []
    for qs, ks, vs, gcols, bcols in chains:
        q, k, v = (jnp.concatenate(t, axis=0) for t in (qs, ks, vs))
        gcol, bcol = jnp.concatenate(gcols, axis=0), jnp.concatenate(bcols, axis=0)
        grow = jnp.sum(eye * gcol, axis=0, keepdims=True)
        gam = jnp.where(incl, jnp.exp(jnp.where(incl, gcol - grow, 0.0)), 0.0)
        glast = jnp.sum(jnp.where(lastc, grow, 0.0), axis=1, keepdims=True)
        st.append((q, k, v, gcol, bcol, gam, glast, jnp.exp(gcol), k * bcol))
    As = [jnp.where(strict, nt(kb, k) * gam, 0.0) for (q, k, v, gcol, bcol, gam, glast, eg, kb) in st]
    Ys = inv_all(As)
    vbs = [v * bcol for (q, k, v, gcol, bcol, gam, glast, eg, kb) in st]
    kgs = [kb * eg for (q, k, v, gcol, bcol, gam, glast, eg, kb) in st]
    us = [vb + nn(Y, vb) for Y, vb in zip(Ys, vbs)]
    ws = [kg + nn(Y, kg) for Y, kg in zip(Ys, kgs)]
    Ps = [nt(q, k) * gam for (q, k, v, gcol, bcol, gam, glast, eg, kb) in st]
    return [((u, w, P, q * eg, k * jnp.exp(glast - gcol), jnp.exp(glast)), Y)
            for u, w, P, Y, (q, k, v, gcol, bcol, gam, glast, eg, kb) in zip(us, ws, Ps, Ys, st)]


def _scan_groups(chains, ops):
    nn, nt, tn, _ = ops
    vns = [[us[j] - nn(ws[j], Ss[j]) for j in range(GH)] for Ss, us, ws, P, qds, kds, egls in chains]
    os_ = [jnp.concatenate([nn(qds[j], Ss[j]) for j in range(GH)], axis=0) + nn(P, jnp.concatenate(vn, axis=0))
           for (Ss, us, ws, P, qds, kds, egls), vn in zip(chains, vns)]
    S2s = [[Ss[j] * egls[j] + tn(kds[j], vn[j]) for j in range(GH)]
           for (Ss, us, ws, P, qds, kds, egls), vn in zip(chains, vns)]
    return list(zip(os_, S2s))


def _group_masks(rev, rows=GR):
    r = lax.broadcasted_iota(jnp.int32, (rows, rows), 0)
    c = lax.broadcasted_iota(jnp.int32, (rows, rows), 1)
    same = (r // CH) == (c // CH)
    ahead = jnp.where(rev, c - r, r - c)
    incl = same & (ahead >= 0)
    strict = same & (ahead > 0)
    eye = jnp.where(r == c, 1.0, 0.0).astype(F32)
    lastc = same & ((c % CH) == jnp.where(rev, 0, CH - 1))
    return incl, strict, eye, lastc


def _head_gates(gb, h, rev):
    gcol = jnp.where(rev, gb[:, NH + h:NH + h + 1], gb[:, h:h + 1])
    bcol = jnp.where(rev, gb[:, 3 * NH + h:3 * NH + h + 1], gb[:, 2 * NH + h:2 * NH + h + 1])
    return gcol, bcol


def _hs(h):
    return slice(h * HD, (h + 1) * HD)


def _blockdiag(a, b):
    z = jnp.zeros_like(a)
    return jnp.concatenate([jnp.concatenate([a, z], axis=1), jnp.concatenate([z, b], axis=1)], axis=0)


def _load_chain(q_ref, k_ref, v_ref, gb_ref, c, g, rev, gh=GH):
    r = slice(c * CH, (c + 1) * CH)
    heads = range(g * gh, (g + 1) * gh)
    gates = [_head_gates(gb_ref[r, :], h, rev) for h in heads]
    return ([q_ref[r, _hs(h)] for h in heads], [k_ref[r, _hs(h)] for h in heads], [v_ref[r, _hs(h)] for h in heads],
            [t[0] for t in gates], [t[1] for t in gates])


def gdn_intra_fwd(q, k, v, gb):
    T = q.shape[0]
    N = T // CH
    ops = (_raw_nn, _raw_nt, _raw_tn)

    def body(q_ref, k_ref, v_ref, gb_ref, u_ref, w_ref, qd_ref, kd_ref, pp_ref, ys_ref, eg_ref):
        rev = pl.program_id(0) == 1
        masks = _group_masks(rev)
        where = [(c, g) for c in range(CBI) for g in range(NG)]
        chains = [_load_chain(q_ref, k_ref, v_ref, gb_ref, c, g, rev) for c, g in where]
        for (c, g), ((u, w, P, qd, kd, egl), Y) in zip(where, _intra_groups(chains, *masks, ops, _tri_inv_y_all)):
            r = slice(c * CH, (c + 1) * CH)
            pp_ref[c, g] = P.astype(MM)
            ys_ref[c, g] = Y.astype(MM)
            for j, h in enumerate(range(g * GH, (g + 1) * GH)):
                rows = slice(j * CH, (j + 1) * CH)
                u_ref[r, _hs(h)] = u[rows]
                w_ref[r, _hs(h)] = w[rows].astype(MM)
                qd_ref[r, _hs(h)] = qd[rows].astype(MM)
                kd_ref[r, _hs(h)] = kd[rows].astype(MM)
                eg_ref[c, h:h + 1, :] = jnp.broadcast_to(egl[j * CH:j * CH + 1, :], (1, 128))

    row = pl.BlockSpec((CBI * CH, D), lambda d, n: (n, 0))
    drow = pl.BlockSpec((None, CBI * CH, D), lambda d, n: (d, n, 0))
    mat = pl.BlockSpec((None, CBI, NG, GR, GR), lambda d, n: (d, n, 0, 0, 0))
    return pl.pallas_call(
        body, name="gdn_intra_fwd", grid=(2, N // CBI),
        in_specs=[row, row, row, pl.BlockSpec((CBI * CH, 128), lambda d, n: (n, 0))],
        out_specs=[drow] * 4 + [mat, mat, pl.BlockSpec((None, CBI, NH, 128), lambda d, n: (d, n, 0, 0))],
        out_shape=[_sds((2, T, D))] + [_sds((2, T, D), MM)] * 3 + [_sds((2, N, NG, GR, GR), MM)] * 2
                  + [_sds((2, N, NH, 128))],
        compiler_params=_cp(("parallel", "parallel")))(q, k, v, gb)


def gdn_scan_fwd(u, w, qd, kd, pp, eg):
    T = u.shape[1]
    N = T // CH
    ops = (_raw_nn, _raw_nt, _raw_tn, None)

    cbs = _scan_chunks(N)
    NB = N // cbs

    def body(u_ref, w_ref, qd_ref, kd_ref, pp_ref, eg_ref, o_ref, s0_ref, S):
        d = pl.program_id(0)

        @pl.when(pl.program_id(1) == 0)
        def _():
            S[...] = jnp.zeros_like(S)

        def chunk(c, carry):
            pc = c + d * (cbs - 1 - 2 * c)
            r = pl.ds(pl.multiple_of(pc * CH, CH), CH)
            chains = []
            for g in range(NG):
                heads = range(g * GH, (g + 1) * GH)
                Ss = [S[h] for h in heads]
                for h, Sh in zip(heads, Ss):
                    s0_ref[pc, h] = Sh
                chains.append((Ss, [u_ref[r, _hs(h)] for h in heads], [w_ref[r, _hs(h)] for h in heads], pp_ref[pc, g],
                               [qd_ref[r, _hs(h)] for h in heads], [kd_ref[r, _hs(h)] for h in heads],
                               [eg_ref[pc, h:h + 1, :] for h in heads]))
            for g, (o, S2) in enumerate(_scan_groups(chains, ops)):
                for j, h in enumerate(range(g * GH, (g + 1) * GH)):
                    o_ref[r, _hs(h)] = o[j * CH:(j + 1) * CH]
                    S[h] = S2[j]
            return carry

        lax.fori_loop(0, cbs, chunk, 0)

    bidx = lambda d, n: n + d * (NB - 1 - 2 * n)
    drow = pl.BlockSpec((None, cbs * CH, D), lambda d, n: (d, bidx(d, n), 0))
    mat = pl.BlockSpec((None, cbs, NG, GR, GR), lambda d, n: (d, bidx(d, n), 0, 0, 0))
    return pl.pallas_call(
        body, name="gdn_scan_fwd", grid=(2, NB),
        in_specs=[drow] * 4 + [mat, pl.BlockSpec((None, cbs, NH, 128), lambda d, n: (d, bidx(d, n), 0, 0))],
        out_specs=[drow, pl.BlockSpec((None, cbs, NH, HD, HD), lambda d, n: (d, bidx(d, n), 0, 0, 0))],
        out_shape=[_sds((2, T, D)), _sds((2, N, NH, HD, HD))],
        scratch_shapes=[pltpu.VMEM((NH, HD, HD), F32)],
        compiler_params=_cp(("arbitrary", "arbitrary")))(u, w, qd, kd, pp, eg)


def gdn_scan_bwd(u, w, qd, kd, pp, eg, s0, do):
    T = u.shape[1]
    N = T // CH
    ops = _make_vjp_ops()
    cbs = _scan_chunks(N)
    NB = N // cbs

    def body(u_ref, w_ref, qd_ref, kd_ref, pp_ref, eg_ref, s0_ref, do_ref,
             du_ref, dw_ref, dqd_ref, dkd_ref, dpp_ref, deg_ref, dS):
        d = pl.program_id(0)

        @pl.when(pl.program_id(1) == 0)
        def _():
            dS[...] = jnp.zeros_like(dS)

        def chunk(c, carry):
            pc = (cbs - 1 - c) + d * (2 * c - (cbs - 1))
            r = pl.ds(pl.multiple_of(pc * CH, CH), CH)
            chains, cts = [], []
            for g in range(NG):
                heads = range(g * GH, (g + 1) * GH)
                chains.append(([s0_ref[pc, h] for h in heads], [u_ref[r, _hs(h)] for h in heads],
                               [w_ref[r, _hs(h)].astype(F32) for h in heads], pp_ref[pc, g].astype(F32),
                               [qd_ref[r, _hs(h)].astype(F32) for h in heads],
                               [kd_ref[r, _hs(h)].astype(F32) for h in heads], [eg_ref[pc, h:h + 1, :] for h in heads]))
                cts.append((jnp.concatenate([do_ref[r, _hs(h)].astype(F32) for h in heads], axis=0),
                            [dS[h] for h in heads]))
            _, vjp = jax.vjp(lambda ch: _scan_groups(ch, ops), chains)
            (dchains,) = vjp(cts)
            for g, (dSs, dus, dws, dP, dqds, dkds, degs) in enumerate(dchains):
                dpp_ref[pc, g] = dP
                for j, h in enumerate(range(g * GH, (g + 1) * GH)):
                    dS[h] = dSs[j]
                    du_ref[r, _hs(h)] = dus[j].astype(MM)
                    dw_ref[r, _hs(h)] = dws[j].astype(MM)
                    dqd_ref[r, _hs(h)], dkd_ref[r, _hs(h)] = dqds[j], dkds[j]
                    deg_ref[pc, h:h + 1, :] = degs[j]
            return carry

        lax.fori_loop(0, cbs, chunk, 0)

    bidx = lambda d, n: (NB - 1 - n) + d * (2 * n - (NB - 1))
    drow = pl.BlockSpec((None, cbs * CH, D), lambda d, n: (d, bidx(d, n), 0))
    erow = pl.BlockSpec((None, cbs, NH, 128), lambda d, n: (d, bidx(d, n), 0, 0))
    mat = pl.BlockSpec((None, cbs, NG, GR, GR), lambda d, n: (d, bidx(d, n), 0, 0, 0))
    return pl.pallas_call(
        body, name="gdn_scan_bwd", grid=(2, NB),
        in_specs=[drow] * 4 + [mat, erow, pl.BlockSpec((None, cbs, NH, HD, HD), lambda d, n: (d, bidx(d, n), 0, 0, 0)),
                               pl.BlockSpec((cbs * CH, D), lambda d, n: (bidx(d, n), 0))],
        out_specs=[drow] * 4 + [mat, erow],
        out_shape=[_sds((2, T, D), MM)] * 2 + [_sds((2, T, D))] * 2 + [_sds((2, N, NG, GR, GR))]
                  + [_sds((2, N, NH, 128))],
        scratch_shapes=[pltpu.VMEM((NH, HD, HD), F32)],
        compiler_params=_cp(("arbitrary", "arbitrary")))(u, w, qd, kd, pp, eg, s0, do)


def gdn_intra_bwd(q, k, v, gb, ys, du, dw, dqd, dkd, dpp, deg):
    T = q.shape[0]
    N = T // CH
    nn, nt, tn, inv_saved = _make_vjp_ops()

    def body(q_ref, k_ref, v_ref, gb_ref, ys_ref, du_ref, dw_ref, dqd_ref, dkd_ref, dpp_ref, deg_ref,
             dq_ref, dk_ref, dv_ref, dgb_ref):
        rev = pl.program_id(0) == 1
        gh, rows = 2 * GH, 2 * GR
        masks = _group_masks(rev, rows)
        lane = lax.broadcasted_iota(jnp.int32, (CH, 128), 1)
        grow = lax.broadcasted_iota(jnp.int32, (rows, 1), 0)
        where = [(c, g) for c in range(CBB) for g in range(NG // 2)]
        pair = lambda ref, c, g: _blockdiag(ref[c, 2 * g].astype(F32), ref[c, 2 * g + 1].astype(F32))
        chains = [_load_chain(q_ref, k_ref, v_ref, gb_ref, c, g, rev, gh) for c, g in where]
        Ys = [pair(ys_ref, c, g) for c, g in where]
        inv_all = lambda As: [inv_saved(A, Y) for A, Y in zip(As, Ys)]
        _, vjp = jax.vjp(lambda ch: _intra_groups(ch, *masks, (nn, nt, tn), inv_all), chains)
        cts = []
        for c, g in where:
            r = slice(c * CH, (c + 1) * CH)
            heads = range(g * gh, (g + 1) * gh)
            stack = lambda ref: jnp.concatenate([ref[r, _hs(h)].astype(F32) for h in heads], axis=0)
            degl = jnp.zeros((rows, 1), F32)
            for j, h in enumerate(heads):
                degl = degl + jnp.where(grow == j * CH, jnp.sum(deg_ref[c, h:h + 1, :], axis=1, keepdims=True), 0.0)
            cts.append(((stack(du_ref), stack(dw_ref), pair(dpp_ref, c, g), stack(dqd_ref), stack(dkd_ref), degl),
                        jnp.zeros((rows, rows), F32)))
        (dchains,) = vjp(cts)
        dgbs = [jnp.zeros((CH, 128), F32) for _ in range(CBB)]
        for (c, g), (dqs, dks, dvs, dgs, dbs) in zip(where, dchains):
            r = slice(c * CH, (c + 1) * CH)
            for j, h in enumerate(range(g * gh, (g + 1) * gh)):
                dq_ref[r, _hs(h)], dk_ref[r, _hs(h)], dv_ref[r, _hs(h)] = dqs[j], dks[j], dvs[j]
                glane = jnp.where(rev, NH + h, h)
                dgbs[c] = dgbs[c] + jnp.where(lane == glane, dgs[j], 0.0) + jnp.where(lane == glane + 2 * NH, dbs[j], 0.0)
        for c in range(CBB):
            dgb_ref[c * CH:(c + 1) * CH, :] = dgbs[c]

    row = pl.BlockSpec((CBB * CH, D), lambda d, n: (n, 0))
    drow = pl.BlockSpec((None, CBB * CH, D), lambda d, n: (d, n, 0))
    mat = pl.BlockSpec((None, CBB, NG, GR, GR), lambda d, n: (d, n, 0, 0, 0))
    return pl.pallas_call(
        body, name="gdn_intra_bwd", grid=(2, N // CBB),
        in_specs=[row, row, row, pl.BlockSpec((CBB * CH, 128), lambda d, n: (n, 0)), mat, drow, drow, drow, drow, mat]
                 + [pl.BlockSpec((None, CBB, NH, 128), lambda d, n: (d, n, 0, 0))],
        out_specs=[drow, drow, drow, pl.BlockSpec((None, CBB * CH, 128), lambda d, n: (d, n, 0))],
        out_shape=[_sds((2, T, D))] * 3 + [_sds((2, T, 128))],
        compiler_params=_cp(("parallel", "parallel")))(q, k, v, gb, ys, du, dw, dqd, dkd, dpp, deg)


def _post_rows(o2a, o2b, z, nw):
    o = o2a + o2b
    outs = []
    for h in range(NH):
        s = slice(h * HD, (h + 1) * HD)
        oh = o[:, s]
        outs.append(oh * lax.rsqrt(jnp.mean(oh * oh, axis=-1, keepdims=True) + RMS_EPS) * nw * _silu(z[:, s]))
    return jnp.concatenate(outs, axis=1)


def post_fwd(o2, pm, nw, *, bt=512):
    T = pm.shape[0]
    bt = min(bt, T)

    def body(o_ref, z_ref, nw_ref, og_ref):
        og_ref[...] = _post_rows(o_ref[0], o_ref[1], z_ref[...], nw_ref[0:1, :]).astype(og_ref.dtype)

    return pl.pallas_call(
        body, name="post_fwd", grid=(T // bt,),
        in_specs=[pl.BlockSpec((2, bt, D), lambda i: (0, i, 0)), pl.BlockSpec((bt, D), lambda i: (i, CZ)),
                  pl.BlockSpec((8, 128), lambda i: (0, 0))],
        out_specs=pl.BlockSpec((bt, D), lambda i: (i, 0)), out_shape=_sds((T, D), MM),
        compiler_params=_cp(("parallel",)))(o2, pm, nw)


def post_bwd(o2, pm, nw, dog, dpm, *, bt=512):
    T = pm.shape[0]
    bt = min(bt, T)

    def body(o_ref, z_ref, nw_ref, dog_ref, _alias, do_ref, dz_ref, dnw_ref):
        @pl.when(pl.program_id(0) == 0)
        def _():
            dnw_ref[...] = jnp.zeros_like(dnw_ref)

        _, vjp = jax.vjp(_post_rows, o_ref[0], o_ref[1], z_ref[...], nw_ref[0:1, :])
        doa, _unused, dz, dnw = vjp(dog_ref[...])
        do_ref[...] = doa.astype(do_ref.dtype)
        dz_ref[...] = dz.astype(dz_ref.dtype)
        row8 = lax.broadcasted_iota(jnp.int32, (8, 128), 0)
        dnw_ref[...] += jnp.where(row8 == 0, dnw, 0.0)

    in_specs = [pl.BlockSpec((2, bt, D), lambda i: (0, i, 0)), pl.BlockSpec((bt, D), lambda i: (i, CZ)),
                pl.BlockSpec((8, 128), lambda i: (0, 0)), pl.BlockSpec((bt, D), lambda i: (i, 0)),
                pl.BlockSpec(memory_space=pl.ANY)]
    return pl.pallas_call(
        body, name="post_bwd", grid=(T // bt,), in_specs=in_specs,
        out_specs=[pl.BlockSpec((bt, D), lambda i: (i, 0)), pl.BlockSpec((bt, D), lambda i: (i, CZ)),
                   pl.BlockSpec((8, 128), lambda i: (0, 0))],
        out_shape=[_sds((T, D), MM), _sds(dpm.shape, dpm.dtype), _sds((8, 128))], input_output_aliases={4: 1},
        compiler_params=_cp(("arbitrary",)))(o2, pm, nw, dog, dpm)


def merge_fwd(ya, yb, pm, *, bt=512):
    T = pm.shape[0]
    bt = min(bt, T)

    def body(ya_ref, yb_ref, ga_ref, gb_ref, o_ref):
        o_ref[...] = (_sigmoid(ga_ref[...]) * ya_ref[...] + _sigmoid(gb_ref[...]) * yb_ref[...]).astype(o_ref.dtype)

    row = pl.BlockSpec((bt, D), lambda i: (i, 0))
    return pl.pallas_call(
        body, name="merge_fwd", grid=(T // bt,),
        in_specs=[row, row, pl.BlockSpec((bt, D), lambda i: (i, CGA)), pl.BlockSpec((bt, D), lambda i: (i, CGB))],
        out_specs=row, out_shape=_sds((T, D), MM), compiler_params=_cp(("parallel",)))(ya, yb, pm, pm)


def merge_bwd(ya, yb, pm, dmix, *, bt=512):
    T = pm.shape[0]
    bt = min(bt, T)

    def body(ya_ref, yb_ref, ga_ref, gb_ref, dm_ref, dya_ref, dyb_ref, dg_ref):
        dm = dm_ref[...]
        sa, sb = _sigmoid(ga_ref[...]), _sigmoid(gb_ref[...])
        dya_ref[...] = (dm * sa).astype(dya_ref.dtype)
        dyb_ref[...] = (dm * sb).astype(dyb_ref.dtype)
        dg_ref[:, 0:D] = (dm * ya_ref[...] * sa * (1.0 - sa)).astype(dg_ref.dtype)
        dg_ref[:, D:2 * D] = (dm * yb_ref[...] * sb * (1.0 - sb)).astype(dg_ref.dtype)

    row = pl.BlockSpec((bt, D), lambda i: (i, 0))
    return pl.pallas_call(
        body, name="merge_bwd", grid=(T // bt,),
        in_specs=[row, row, pl.BlockSpec((bt, D), lambda i: (i, CGA)), pl.BlockSpec((bt, D), lambda i: (i, CGB)), row],
        out_specs=[row, row, pl.BlockSpec((bt, 2 * D), lambda i: (i, CGA // 2))],
        out_shape=[_sds((T, D), MM), _sds((T, D), MM), _sds((T, NMAIN), MM)],
        compiler_params=_cp(("parallel",)))(ya, yb, pm, pm, dmix)


def _ln_rows(x, y, bias, g, b):
    r = ALPHA * x + y + bias
    mu = jnp.mean(r, axis=-1, keepdims=True)
    var = jnp.mean(jnp.square(r - mu), axis=-1, keepdims=True)
    return (r - mu) * lax.rsqrt(var + LN_EPS) * g + b


def ln_fwd(x, y, p, *, name, bt=512):
    T = x.shape[0]
    bt = min(bt, T)

    def body(x_ref, y_ref, p_ref, o_ref, ob_ref):
        r = _ln_rows(x_ref[...], y_ref[...], p_ref[0:1, :], p_ref[1:2, :], p_ref[2:3, :])
        o_ref[...] = r
        ob_ref[...] = r.astype(ob_ref.dtype)

    row = pl.BlockSpec((bt, D), lambda i: (i, 0))
    return pl.pallas_call(
        body, name=name, grid=(T // bt,), in_specs=[row, row, pl.BlockSpec((8, D), lambda i: (0, 0))],
        out_specs=[row, row], out_shape=[_sds((T, D)), _sds((T, D), MM)],
        compiler_params=_cp(("parallel",)))(x, y, p)


def ln_bwd(x, y, p, ct, ct2=None, *, name, bt=512):
    T = x.shape[0]
    bt = min(bt, T)

    def body(*refs):
        it = iter(refs)
        x_ref, y_ref, p_ref, c_ref = next(it), next(it), next(it), next(it)
        c2_ref = next(it) if ct2 is not None else None
        dxa_ref, dr_ref, dp_ref = next(it), next(it), next(it)

        @pl.when(pl.program_id(0) == 0)
        def _():
            dp_ref[...] = jnp.zeros_like(dp_ref)

        c = c_ref[...]
        if c2_ref is not None:
            c = c + c2_ref[...]
        _, vjp = jax.vjp(_ln_rows, x_ref[...], y_ref[...], p_ref[0:1, :], p_ref[1:2, :], p_ref[2:3, :])
        _dx, dy, dbias, dg, db = vjp(c)
        dxa_ref[...] = ALPHA * dy
        dr_ref[...] = dy.astype(dr_ref.dtype)
        row8 = lax.broadcasted_iota(jnp.int32, (8, D), 0)
        dp_ref[...] += jnp.where(row8 == 0, dbias, jnp.where(row8 == 1, dg, jnp.where(row8 == 2, db, 0.0)))

    row = pl.BlockSpec((bt, D), lambda i: (i, 0))
    in_specs = [row, row, pl.BlockSpec((8, D), lambda i: (0, 0)), row] + ([row] if ct2 is not None else [])
    args = [x, y, p, ct] + ([ct2] if ct2 is not None else [])
    return pl.pallas_call(
        body, name=name, grid=(T // bt,), in_specs=in_specs,
        out_specs=[row, row, pl.BlockSpec((8, D), lambda i: (0, 0))],
        out_shape=[_sds((T, D)), _sds((T, D), MM), _sds((8, D))],
        compiler_params=_cp(("arbitrary",)))(*args)


def loss_fwd_bwd(xl, target, *, bt=512):
    T = xl.shape[0]
    bt = min(bt, T)

    def body(x_ref, t_ref, l_ref, d_ref):
        @pl.when(pl.program_id(0) == 0)
        def _():
            l_ref[...] = jnp.zeros_like(l_ref)

        e = x_ref[...] - t_ref[...]
        d_ref[...] = e * (1.0 / D)
        l_ref[...] += 0.5 * jnp.sum(jnp.mean(e * e, axis=-1, keepdims=True), axis=0, keepdims=True)

    row = pl.BlockSpec((bt, D), lambda i: (i, 0))
    return pl.pallas_call(
        body, name="loss", grid=(T // bt,), in_specs=[row, row],
        out_specs=[pl.BlockSpec((8, 128), lambda i: (0, 0)), row], out_shape=[_sds((8, 128)), _sds((T, D))],
        compiler_params=_cp(("arbitrary",)))(xl, target)


def _row_tile(R, Cc, elems=1 << 18):
    if R * Cc <= elems:
        return R
    tr = 8
    while tr * 2 * Cc <= elems and R % (tr * 2) == 0:
        tr *= 2
    return tr


def adam(w, m, v, ga, gb=None, *, name):
    R, Cc = w.shape
    tr = _row_tile(R, Cc)

    def body(*refs):
        it = iter(refs)
        w_ref, m_ref, v_ref, a_ref = next(it), next(it), next(it), next(it)
        b_ref = next(it) if gb is not None else None
        g_ref, d_ref, mo_ref, vo_ref = next(it), next(it), next(it), next(it)
        g = a_ref[...]
        if b_ref is not None:
            g = g + b_ref[...]
        m2 = B1 * m_ref[...] + (1.0 - B1) * g
        v2 = B2 * v_ref[...] + (1.0 - B2) * jnp.square(g)
        m_hat = m2 / (1.0 - B1 ** STEP)
        v_hat = v2 / (1.0 - B2 ** STEP)
        g_ref[...] = g
        d_ref[...] = -LR * (m_hat / (jnp.sqrt(v_hat) + EPS) + WD * w_ref[...])
        mo_ref[...] = m2
        vo_ref[...] = v2

    blk = pl.BlockSpec((tr, Cc), lambda i: (i, 0))
    args = [w, m, v, ga] + ([gb] if gb is not None else [])
    return pl.pallas_call(
        body, name=name, grid=(R // tr,), in_specs=[blk] * len(args), out_specs=[blk] * 4,
        out_shape=[_sds((R, Cc))] * 4, compiler_params=_cp(("parallel",)))(*args)


def sum4(parts, *, name):
    _, R, Cc = parts.shape
    tr = _row_tile(R, Cc)

    def body(p_ref, out_ref):
        f = lambda t: t.astype(F32)
        out_ref[...] = ((f(p_ref[0]) + f(p_ref[1])) + f(p_ref[2])) + f(p_ref[3])

    return pl.pallas_call(
        body, name=name, grid=(R // tr,), in_specs=[pl.BlockSpec((4, tr, Cc), lambda i: (0, i, 0))],
        out_specs=pl.BlockSpec((tr, Cc), lambda i: (i, 0)), out_shape=_sds((R, Cc)),
        compiler_params=_cp(("parallel",)))(parts)


def _place():
    return lax.axis_index("x"), lax.axis_index("y"), lax.axis_index("c")


def _other_chips(x, y):
    return [(1 - x, y), (x, 1 - y), (1 - x, 1 - y)]


_ANY = pl.BlockSpec(memory_space=pl.ANY)


def allgather_xy(arrs):
    n = len(arrs)
    axes = [0 if a.shape[0] % 2 == 0 else 1 for a in arrs]
    halves = [a.shape[ax] // 2 for a, ax in zip(arrs, axes)]

    def half_of(ref, a, which):
        part = pl.ds(which * halves[a], halves[a])
        return ref.at[part] if axes[a] == 0 else ref.at[:, part]

    def body(*refs):
        ins, outs = refs[:n], refs[n:2 * n]
        send, recv, fsend, frecv, loc = refs[2 * n:]
        x, y, c = _place()
        me = 2 * x + y
        peers = _other_chips(x, y)
        local = [pltpu.make_async_copy(ins[a], outs[a].at[me], loc.at[a]) for a in range(n)]
        for cp in local:
            cp.start()

        def over_ici(a, j, block, src=None):
            px, py = peers[j]
            dst = half_of(outs[a].at[block], a, c)
            return pltpu.make_async_remote_copy(
                src_ref=dst if src is None else half_of(src, a, c), dst_ref=dst, send_sem=send.at[3 * a + j],
                recv_sem=recv.at[3 * a + j], device_id=(px, py, c), device_id_type=MESH)

        def over_d2d(a, j, half):
            px, py = peers[j]
            rows = half_of(outs[a].at[2 * px + py], a, half)
            return pltpu.make_async_remote_copy(
                src_ref=rows, dst_ref=rows, send_sem=fsend.at[3 * a + j], recv_sem=frecv.at[3 * a + j],
                device_id=(x, y, 1 - c), device_id_type=MESH)

        sends = [over_ici(a, j, me, src=ins[a]) for a in range(n) for j in range(3)]
        for cp in sends:
            cp.start()
        passed = []
        for a in range(n):
            for j, (px, py) in enumerate(peers):
                over_ici(a, j, 2 * px + py).wait_recv()
                passed.append(over_d2d(a, j, c))
                passed[-1].start()
        for a in range(n):
            for j in range(3):
                over_d2d(a, j, 1 - c).wait_recv()
        for cp in sends + passed:
            cp.wait_send()
        for cp in local:
            cp.wait()

    return pl.pallas_call(
        body, name="allgather_xy", in_specs=[_ANY] * n, out_specs=[_ANY] * n,
        out_shape=[_sds((4,) + a.shape, a.dtype) for a in arrs],
        scratch_shapes=[pltpu.SemaphoreType.DMA((3 * n,))] * 4 + [pltpu.SemaphoreType.DMA((n,))],
        compiler_params=pltpu.CompilerParams(has_side_effects=True))(*arrs)


_HBM = pl.BlockSpec(memory_space=pltpu.HBM)
_SEM = pl.BlockSpec(memory_space=pltpu.SEMAPHORE)
_EFFECT = pltpu.SideEffectType.DATAFLOW_SIDE_EFFECTING


def _xy_copy(kind, layer, src, land, send, recv, a, j, c, arriving):
    x, y = lax.axis_index("x"), lax.axis_index("y")
    px, py = ((1 - x, y), (x, 1 - y), (1 - x, 1 - y), (x, y))[j]
    if kind == "gather":
        dst = land.at[2 * px + py] if arriving else land.at[2 * x + y]
        src_view = dst if arriving else src
    else:
        dst = land.at[j, layer]
        src_view = src.at[2 * px + py]
    return pltpu.make_async_remote_copy(src_ref=src_view, dst_ref=dst, send_sem=send.at[4 * a + j],
                                        recv_sem=recv.at[4 * a + j], device_id=(px, py, c), device_id_type=MESH)


def xy_start(kind, name, arrs, after, zones=None, layer=0):
    n = len(arrs)
    if zones is None:
        shape = lambda a: ((4,) + a.shape) if kind == "gather" else ((4, DEPTH) + a.shape[1:])
        zones = [lax.empty(shape(a), a.dtype) for a in arrs]

    def body(*refs):
        ins, lands = refs[:n], refs[n:2 * n]
        send, recv = refs[2 * n + 1], refs[2 * n + 2]
        token = refs[-1]
        c = lax.axis_index("c")
        for a in range(n):
            for j in range(4):
                _xy_copy(kind, layer, ins[a], lands[a], send, recv, a, j, c, False).start()
        token[...] = jnp.zeros_like(token)

    hbm = lambda v: pltpu.with_memory_space_constraint(v, pltpu.HBM)
    outs = pl.pallas_call(
        body, name=name,
        out_shape=(pltpu.SemaphoreType.DMA((4 * n,)), pltpu.SemaphoreType.DMA((4 * n,)),
                   *[pltpu.HBM(a.shape, a.dtype) for a in arrs], *[pltpu.HBM(z.shape, z.dtype) for z in zones],
                   _sds((8, 128))),
        in_specs=[_HBM] * (2 * n) + [_ANY],
        out_specs=(_SEM, _SEM, *[_HBM] * (2 * n), pl.BlockSpec(memory_space=pltpu.VMEM)),
        input_output_aliases={i: 2 + i for i in range(2 * n)},
        compiler_params=pltpu.CompilerParams(has_side_effects=_EFFECT))(
            *[hbm(a) for a in arrs], *[hbm(z) for z in zones], after)
    return (kind, layer, outs[0], outs[1], list(outs[2:2 + n])), list(outs[2 + n:2 + 2 * n]), outs[-1]


def xy_wait(name, handles, zones, after):
    kind, layer, send_sems, recv_sems, srcs = handles
    n = len(srcs)

    def body(*refs):
        ins, lands = refs[:n], refs[n:2 * n]
        send, recv = refs[2 * n], refs[2 * n + 1]
        c = lax.axis_index("c")
        for a in range(n):
            for j in range(4):
                _xy_copy(kind, layer, ins[a], lands[a], send, recv, a, j, c, False).wait_send()
                _xy_copy(kind, layer, ins[a], lands[a], send, recv, a, j, c, True).wait_recv()

    outs = pl.pallas_call(
        body, name=name,
        out_shape=tuple(pltpu.HBM(v.shape, v.dtype) for v in srcs + zones),
        in_specs=[_HBM] * (2 * n) + [_SEM, _SEM, _ANY], out_specs=tuple([_HBM] * (2 * n)),
        input_output_aliases={i: i for i in range(2 * n)},
        compiler_params=pltpu.CompilerParams(has_side_effects=_EFFECT))(*srcs, *zones, send_sems, recv_sems, after)
    return list(outs[n:])


def swap_c(arrs):
    n = len(arrs)

    def body(*refs):
        ins, outs = refs[:n], refs[n:2 * n]
        send, recv = refs[2 * n:]
        x, y, c = _place()
        cps = [pltpu.make_async_remote_copy(src_ref=ins[a], dst_ref=outs[a], send_sem=send.at[a], recv_sem=recv.at[a],
                                            device_id=(x, y, 1 - c), device_id_type=MESH) for a in range(n)]
        for cp in cps:
            cp.start()
        for cp in cps:
            cp.wait_recv()
        for cp in cps:
            cp.wait_send()

    return pl.pallas_call(
        body, name="swap_c", in_specs=[_ANY] * n, out_specs=[_ANY] * n, out_shape=[_sds(a.shape, a.dtype) for a in arrs],
        scratch_shapes=[pltpu.SemaphoreType.DMA((n,)), pltpu.SemaphoreType.DMA((n,))],
        compiler_params=pltpu.CompilerParams(has_side_effects=True))(*arrs)


def allreduce_small(v):
    R = v.shape[0]

    def body(v_ref, o_ref, buf, send, recv):
        x, y, c = _place()
        me = 4 * x + 2 * y + c
        buf[0] = v_ref[...]

        def cp(k):
            dx, dy, dc = (k >> 2) & 1, (k >> 1) & 1, k & 1
            return pltpu.make_async_remote_copy(
                src_ref=v_ref, dst_ref=buf.at[k], send_sem=send.at[k - 1], recv_sem=recv.at[k - 1],
                device_id=(x ^ dx, y ^ dy, c ^ dc), device_id_type=MESH)

        cps = [cp(k) for k in range(1, 8)]
        for t in cps:
            t.start()
        for t in cps:
            t.wait_recv()
        acc = buf[me]
        for dev in range(1, 8):
            acc = acc + buf[jnp.bitwise_xor(me, dev)]
        o_ref[...] = acc
        for t in cps:
            t.wait_send()

    vm = pl.BlockSpec(memory_space=pltpu.VMEM)
    return pl.pallas_call(
        body, name="allreduce_small", in_specs=[vm], out_specs=vm, out_shape=_sds((R, 128)),
        scratch_shapes=[pltpu.VMEM((8, R, 128), F32), pltpu.SemaphoreType.DMA((7,)), pltpu.SemaphoreType.DMA((7,))],
        compiler_params=pltpu.CompilerParams(has_side_effects=True, vmem_limit_bytes=VMEM_LIMIT))(v)


def _rows8(*rows):
    n = rows[0].shape[-1]
    t = jnp.stack([r.reshape(n).astype(F32) for r in rows])
    return jnp.pad(t, ((0, 8 - len(rows)), (0, 0)))


def _lanes128(a):
    f = a.reshape(-1).astype(F32)
    return jnp.pad(f, (0, 128 - f.shape[0]))


def _layer_fwd(x, xb, W):
    pm = mm_nn(xb, W["w_main"], name="proj_main", tm=2048)
    pab = mm_nn(xb, W["w_ab"], name="proj_ab")
    q, k, v, gb = pre_qkv_fwd(pm, pab, W["cw"], W["gp"])
    sc = pre_sc_fwd(pm, W["csc"])
    u, w, qd, kd, pp, ys, eg = gdn_intra_fwd(q, k, v, gb)
    o2, s0 = gdn_scan_fwd(u, w, qd, kd, pp, eg)
    og = post_fwd(o2, pm, W["nw"])
    ya = mm_nn(og, W["w_og"], name="proj_og")
    yb = mm_nn(sc, W["w_osc"], name="proj_osc")
    mixed = merge_fwd(ya, yb, pm)
    out = mm_nn(mixed, W["w_out"], name="proj_out")
    x1, x1b = ln_fwd(x, out, W["ln1"], name="ln1_fwd")
    hfac, h = mm_nn(x1b, W["w_up"], bias=W["b_up"], relu2=True, out_dtype=MM, name="mlp_up", tm=2048)
    dn = mm_nn(h, W["w_down"], name="mlp_down", tk=DFF)
    x2, x2b = ln_fwd(x1, dn, W["ln2"], name="ln2_fwd")
    saved = dict(x=x, xb=xb, pm=pm, pab=pab, q=q, k=k, v=v, gb=gb, sc=sc, o2=o2, s0=s0, og=og, ya=ya, yb=yb,
                 u=u, w=w, qd=qd, kd=kd, pp=pp, ys=ys, eg=eg,
                 mixed=mixed, out=out, x1=x1, x1b=x1b, hfac=hfac, h=h, dn=dn)
    return x2, x2b, saved


def _layer_bwd(ct, W, S, early=None):
    dxa2, dr2b, dp2 = ln_bwd(S["x1"], S["dn"], W["ln2"], ct, name="ln2_bwd")
    g_down = mm_tn(S["h"], dr2b, name="dw_down", tm=2048, tk=2048)
    dhpre, db_up = mm_nt(dr2b, W["w_down"], dact=S["hfac"], out_dtype=MM, name="mlp_down_bwd", tm=2048)
    g_up = mm_tn(S["x1b"], dhpre, name="dw_up", tn=2048, tk=2048)
    dx1 = mm_nt(dhpre, W["w_up"], add=dxa2, name="mlp_up_bwd", tk=DFF)
    dxa1, dr1b, dp1 = ln_bwd(S["x"], S["out"], W["ln1"], dx1, name="ln1_bwd")
    g_out = mm_tn(S["mixed"], dr1b, name="dw_out")
    dmix = mm_nt(dr1b, W["w_out"], name="proj_out_bwd")
    dya, dyb, dpm = merge_bwd(S["ya"], S["yb"], S["pm"], dmix)
    g_og = mm_tn(S["og"], dya, name="dw_og")
    g_osc = mm_tn(S["sc"], dyb, name="dw_osc")
    dog = mm_nt(dya, W["w_og"], name="proj_og_bwd")
    dsc = mm_nt(dyb, W["w_osc"], name="proj_osc_bwd")
    token = None if early is None else early(dict(w_o_gdn=g_og, w_o_sc=g_osc, w_out=g_out, w_up=g_up, w_down=g_down))
    nw = W["nw"] if token is None else W["nw"] + token[0, 0]
    do, dpm, dnw = post_bwd(S["o2"], S["pm"], nw, dog, dpm)
    du, dw, dqd, dkd, dpp, deg = gdn_scan_bwd(S["u"], S["w"], S["qd"], S["kd"], S["pp"], S["eg"], S["s0"], do)
    dq2, dk2, dv2, dgb2 = gdn_intra_bwd(S["q"], S["k"], S["v"], S["gb"], S["ys"], du, dw, dqd, dkd, dpp, deg)
    dpm, dpab, dcw, dgp = pre_qkv_bwd(S["pm"], S["pab"], W["cw"], W["gp"], dq2, dk2, dv2, dgb2, dpm)
    dpm, dcsc = pre_sc_bwd(S["pm"], W["csc"], dsc, dpm)
    g_main = mm_tn(S["xb"], dpm, name="dw_main", tn=NMAIN // 4)
    g_ab = mm_tn(S["xb"], dpab, name="dw_ab")
    t = mm_nt(dpab, W["w_ab"], add=dxa1, name="proj_ab_bwd")
    dx = mm_nt(dpm, W["w_main"], add=t, name="proj_main_bwd", tk=NMAIN // 3)
    g_in = jnp.concatenate([g_main[:, :3 * D], g_main[:, 8 * D:], g_ab[:, :4 * NH], g_main[:, 3 * D:8 * D]], axis=1)
    grads = dict(
        w_in=g_in, w_o_gdn=g_og, w_o_sc=g_osc, w_out=g_out, w_up=g_up, w_down=g_down,
        conv_qkv=dcw[:3], conv_sc=dcsc[:3], a_log=dgp[0, :2 * NH].reshape(2, NH), dt_bias=dgp[1, :2 * NH].reshape(2, NH),
        gdn_norm_w=dnw[0], ln1_g=dp1[1], ln1_b=dp1[2], b_up=db_up[0], b_down=dp2[0], ln2_g=dp2[1], ln2_b=dp2[2])
    return dx, grads


def _layer_weights(l, full, i, conv, a_log, dt_bias, gdn_norm_w, ln1_g, ln1_b, b_up, b_down, ln2_g, ln2_b):
    w_in = full["w_in"][i]
    w_main = jnp.concatenate([w_in[:, :3 * D], w_in[:, 4 * D + 4 * NH:], w_in[:, 3 * D:4 * D]], axis=1)
    w_ab = jnp.pad(w_in[:, 4 * D:4 * D + 4 * NH], ((0, 0), (0, 128 - 4 * NH)))
    return dict(
        w_main=w_main, w_ab=w_ab, w_og=full["w_o_gdn"][i], w_osc=full["w_o_sc"][i], w_out=full["w_out"][i],
        w_up=full["w_up"][i], w_down=full["w_down"][i],
        cw=jnp.pad(conv["conv_qkv"][l].astype(F32), ((0, 5), (0, 0))),
        csc=jnp.pad(conv["conv_sc"][l].astype(F32), ((0, 5), (0, 0))),
        gp=_rows8(_lanes128(a_log[l]), _lanes128(dt_bias[l])), nw=_rows8(gdn_norm_w[l]),
        ln1=_rows8(jnp.zeros((D,), F32), ln1_g[l], ln1_b[l]), ln2=_rows8(b_down[l], ln2_g[l], ln2_b[l]),
        b_up=b_up[l].reshape(1, DFF).astype(F32))


def local_step(xs, target, weights_of, after_bwd, mid_bwd=None):
    x, xb = xs, xs.astype(MM)
    Ws, saved = [], []
    for l in range(DEPTH):
        Ws.append(weights_of(l, x))
        x, xb, S = _layer_fwd(x, xb, Ws[l])
        saved.append(S)
    loss_tile, ct = loss_fwd_bwd(x, target)
    token = None
    for l in reversed(range(DEPTH)):
        W = Ws[l] if token is None else dict(Ws[l], ln2=Ws[l]["ln2"] + token[0, 0])
        ct, grads = _layer_bwd(ct, W, saved[l], None if mid_bwd is None else (lambda part, l=l: mid_bwd(l, part)))
        token = after_bwd(l, grads)
    return loss_tile, ct


EARLY = 1
BIG = ("w_in", "w_o_gdn", "w_o_sc", "w_out", "w_up", "w_down")
SMALL = ("conv_qkv", "a_log", "dt_bias", "gdn_norm_w", "conv_sc", "ln1_g", "ln1_b", "b_up", "b_down", "ln2_g", "ln2_b")
ORDER = ("w_in", "conv_qkv", "a_log", "dt_bias", "gdn_norm_w", "w_o_gdn", "conv_sc", "w_o_sc", "w_out", "ln1_g",
         "ln1_b", "w_up", "b_up", "w_down", "b_down", "ln2_g", "ln2_b")


def _pack(arrs):
    flat = jnp.concatenate([a.reshape(-1).astype(F32) for a in arrs])
    n = flat.shape[0]
    rows = -(-n // 1024) * 8
    return jnp.pad(flat, (0, rows * 128 - n)).reshape(rows, 128)


def _unpack(buf, like):
    flat = buf.reshape(-1)
    out, o = [], 0
    for a in like:
        n = 1
        for s in a.shape:
            n *= s
        out.append(flat[o:o + n].reshape(a.shape))
        o += n
    return out


def _gathered(name, g):
    if name in ("w_in", "w_up", "conv_qkv", "conv_sc"):
        t = jnp.moveaxis(g, 0, -2)
        return t.reshape(t.shape[:-2] + (t.shape[-2] * t.shape[-1],))
    t = jnp.moveaxis(g, 0, 1)
    return t.reshape((t.shape[0], t.shape[1] * t.shape[2]) + t.shape[3:])


def _by_chip(name, g):
    if name in ("w_in", "w_up"):
        r, ccols = g.shape
        return jnp.moveaxis(g.reshape(r, 4, ccols // 4), 1, 0)
    return g.reshape((4, g.shape[0] // 4) + g.shape[1:])


def kernel(x, w_in, conv_qkv, a_log, dt_bias, gdn_norm_w, w_o_gdn, conv_sc, w_o_sc, w_out, ln1_g, ln1_b, w_up, b_up, w_down, b_down, ln2_g, ln2_b, loss_target, m_w_in, m_conv_qkv, m_a_log, m_dt_bias, m_gdn_norm_w, m_w_o_gdn, m_conv_sc, m_w_o_sc, m_w_out, m_ln1_g, m_ln1_b, m_w_up, m_b_up, m_w_down, m_b_down, m_ln2_g, m_ln2_b, v_w_in, v_conv_qkv, v_a_log, v_dt_bias, v_gdn_norm_w, v_w_o_gdn, v_conv_sc, v_w_o_sc, v_w_out, v_ln1_g, v_ln1_b, v_w_up, v_b_up, v_w_down, v_b_down, v_ln2_g, v_ln2_b):
    w = dict(w_in=w_in, conv_qkv=conv_qkv, a_log=a_log, dt_bias=dt_bias, gdn_norm_w=gdn_norm_w, w_o_gdn=w_o_gdn,
             conv_sc=conv_sc, w_o_sc=w_o_sc, w_out=w_out, ln1_g=ln1_g, ln1_b=ln1_b, w_up=w_up, b_up=b_up,
             w_down=w_down, b_down=b_down, ln2_g=ln2_g, ln2_b=ln2_b)
    m = dict(w_in=m_w_in, conv_qkv=m_conv_qkv, a_log=m_a_log, dt_bias=m_dt_bias, gdn_norm_w=m_gdn_norm_w,
             w_o_gdn=m_w_o_gdn, conv_sc=m_conv_sc, w_o_sc=m_w_o_sc, w_out=m_w_out, ln1_g=m_ln1_g, ln1_b=m_ln1_b,
             w_up=m_w_up, b_up=m_b_up, w_down=m_w_down, b_down=m_b_down, ln2_g=m_ln2_g, ln2_b=m_ln2_b)
    v = dict(w_in=v_w_in, conv_qkv=v_conv_qkv, a_log=v_a_log, dt_bias=v_dt_bias, gdn_norm_w=v_gdn_norm_w,
             w_o_gdn=v_w_o_gdn, conv_sc=v_conv_sc, w_o_sc=v_w_o_sc, w_out=v_w_out, ln1_g=v_ln1_g, ln1_b=v_ln1_b,
             w_up=v_w_up, b_up=v_b_up, w_down=v_w_down, b_down=v_b_down, ln2_g=v_ln2_g, ln2_b=v_ln2_b)
    chip = 2 * lax.axis_index("x") + lax.axis_index("y")

    names = BIG + ("conv_qkv", "conv_sc")
    blocks = [w[n].astype(MM) if n in BIG else w[n] for n in names]
    got = allgather_xy([b[:EARLY] if n in BIG else b for n, b in zip(names, blocks)])
    early = {n: _gathered(n, g) for n, g in zip(names, got)}
    gather, gather_zones, token = xy_start("gather", "gather_start", [b[EARLY:] for b in blocks[:len(BIG)]],
                                           after=got[0])
    vectors = (a_log, dt_bias, gdn_norm_w, ln1_g, ln1_b, b_up, b_down, ln2_g, ln2_b)
    late, grads = {}, [None] * DEPTH
    scatters = {"mid": [None] * DEPTH, "end": [None] * DEPTH}
    zones = {"mid": None, "end": None}

    def weights_of(l, x_l):
        if l < EARLY:
            return _layer_weights(l, early, l, early, *vectors)
        if not late:
            for n, zone in zip(BIG, xy_wait("gather_wait", gather, gather_zones, after=x_l)):
                late[n] = _gathered(n, zone)
        return _layer_weights(l, late, l - EARLY, early, *vectors)

    def scatter(when, l, g, which):
        scatters[when][l], zones[when], tok = xy_start(
            "scatter", "scatter_%s_start_%d" % (when, l), [_by_chip(n, g[n]).astype(MM) for n in which], after=a_log,
            zones=zones[when], layer=l)
        return tok

    def after_bwd(l, g):
        grads[l] = g
        return scatter("end", l, g, BIG[:1])

    loss_tile, dx = local_step(x[0] + token[0, 0], loss_target[0], weights_of, after_bwd,
                               lambda l, g: scatter("mid", l, g, BIG[1:]))
    loss = lax.psum(loss_tile[0, 0], ("x", "y", "c"))

    for when in ("mid", "end"):
        for l in range(DEPTH):
            zones[when] = xy_wait("scatter_%s_wait_%d" % (when, l), scatters[when][l], zones[when], after=dx)
    part = [sum4(z.reshape(4, -1, z.shape[-1]), name="sum_" + n) for n, z in zip(BIG, zones["end"] + zones["mid"])]
    other = swap_c(part)
    out = {}
    for n, mine, theirs in zip(BIG, part, other):
        cols = mine.shape[-1]
        res = adam(w[n].reshape(-1, cols), m[n].reshape(-1, cols), v[n].reshape(-1, cols), mine, theirs, name="adam_" + n)
        out[n] = [r.reshape(w[n].shape) for r in res]

    stacked = [jnp.stack([grads[l][n] for l in range(DEPTH)]) for n in SMALL]
    summed = _unpack(allreduce_small(_pack(stacked)), stacked)
    gs = []
    for n, g in zip(SMALL, summed):
        if n in ("conv_qkv", "conv_sc"):
            blk = w[n].shape[-1]
            g = lax.dynamic_slice_in_dim(g, chip * blk, blk, axis=2)
        gs.append(g)
    res = adam(_pack([w[n] for n in SMALL]), _pack([m[n] for n in SMALL]), _pack([v[n] for n in SMALL]), _pack(gs),
               name="adam_small")
    for n, parts in zip(SMALL, zip(*[_unpack(r, gs) for r in res])):
        out[n] = list(parts)

    outs = [loss, dx[None]]
    for kind in range(4):
        outs += [out[n][kind] for n in ORDER]
    return tuple(outs)
```

```python
import jax
import jax.numpy as jnp
from jax import lax
from jax.experimental import pallas as pl
from jax.experimental.pallas import tpu as pltpu

F32 = jnp.float32
MM = jnp.bfloat16
HI = lax.Precision.HIGHEST

D = 1024
NH = 8
HD = 128
CH = 64
DFF = 4 * D
DEPTH = 4
LN_EPS = 1e-5
RMS_EPS = 1e-6
L2_EPS = 1e-6
ALPHA = (2 * DEPTH) ** 0.25
LR, B1, B2, EPS, WD, STEP = 0.001, 0.9, 0.999, 1e-08, 0.01, 10

NMAIN = 9 * D
CQ, CK, CV, CSB, CSC, CSX, CGA, CGB, CZ = range(9)
HALO = 8
VMEM_LIMIT = 56 * 1024 * 1024
MESH = pl.DeviceIdType.MESH


def _cp(sem=None, vmem=VMEM_LIMIT):
    return pltpu.CompilerParams(dimension_semantics=sem, vmem_limit_bytes=vmem)


def _sds(shape, dtype=F32):
    return jax.ShapeDtypeStruct(tuple(shape), dtype)


def _accumulate(acc, product, k, nk, finish):
    if nk == 1:
        finish(product())
        return

    @pl.when(k == 0)
    def _():
        acc[...] = jnp.zeros_like(acc)

    acc[...] += product()

    @pl.when(k == nk - 1)
    def _():
        finish(acc[...])


def mm_nn(a, b, *, name, bias=None, relu2=False, add=None, out_dtype=F32, tm=1024, tn=1024, tk=1024):
    M, K = a.shape
    N = b.shape[1]
    tm, tn, tk = min(tm, M), min(tn, N), min(tk, K)
    nk = K // tk

    def body(*refs):
        it = iter(refs)
        a_ref, b_ref = next(it), next(it)
        bias_ref = next(it) if bias is not None else None
        add_ref = next(it) if add is not None else None
        o_ref = next(it)
        h_ref = next(it) if relu2 else None
        acc = next(it) if nk > 1 else None
        prod = lambda: jnp.dot(a_ref[...].astype(MM), b_ref[...].astype(MM), preferred_element_type=F32)

        def finish(r):
            if bias_ref is not None:
                r = r + bias_ref[...]
            if add_ref is not None:
                r = r + add_ref[...]
            if relu2:
                t = jnp.maximum(r, 0.0)
                o_ref[...] = (2.0 * t).astype(o_ref.dtype)
                h_ref[...] = (t * t).astype(h_ref.dtype)
            else:
                o_ref[...] = r.astype(o_ref.dtype)

        _accumulate(acc, prod, pl.program_id(2), nk, finish)

    in_specs = [pl.BlockSpec((tm, tk), lambda i, j, k: (i, k)), pl.BlockSpec((tk, tn), lambda i, j, k: (k, j))]
    args = [a, b]
    if bias is not None:
        in_specs.append(pl.BlockSpec((1, tn), lambda i, j, k: (0, j)))
        args.append(bias)
    if add is not None:
        in_specs.append(pl.BlockSpec((tm, tn), lambda i, j, k: (i, j)))
        args.append(add)
    out_shape = [_sds((M, N), out_dtype)]
    out_specs = [pl.BlockSpec((tm, tn), lambda i, j, k: (i, j))]
    if relu2:
        out_shape.append(_sds((M, N), MM))
        out_specs.append(pl.BlockSpec((tm, tn), lambda i, j, k: (i, j)))
    res = pl.pallas_call(
        body, name=name, grid=(M // tm, N // tn, nk), in_specs=in_specs, out_specs=out_specs, out_shape=out_shape,
        scratch_shapes=[pltpu.VMEM((tm, tn), F32)] if nk > 1 else [],
        compiler_params=_cp(("parallel", "parallel", "arbitrary")))(*args)
    return res if relu2 else res[0]


def mm_nt(a, b, *, name, add=None, dact=None, out_dtype=F32, tm=1024, tn=1024, tk=1024):
    M, Nc = a.shape
    Ko = b.shape[0]
    tm, tn, tk = min(tm, M), min(tn, Ko), min(tk, Nc)
    nk = Nc // tk
    ni = M // tm

    def body(*refs):
        it = iter(refs)
        a_ref, b_ref = next(it), next(it)
        add_ref = next(it) if add is not None else None
        d_ref = next(it) if dact is not None else None
        o_ref = next(it)
        db_ref = next(it) if dact is not None else None
        acc = next(it) if nk > 1 else None
        i = pl.program_id(1)
        prod = lambda: lax.dot_general(a_ref[...].astype(MM), b_ref[...].astype(MM), (((1,), (1,)), ((), ())),
                                       preferred_element_type=F32)

        def finish(r):
            if add_ref is not None:
                r = r + add_ref[...]
            if d_ref is not None:
                r = r * d_ref[...].astype(F32)
                s = jnp.sum(r, axis=0, keepdims=True)
                row0 = lax.broadcasted_iota(jnp.int32, db_ref.shape, 0) == 0

                @pl.when(i == 0)
                def _():
                    db_ref[...] = jnp.zeros_like(db_ref)

                db_ref[...] += jnp.where(row0, s, 0.0)
            o_ref[...] = r.astype(o_ref.dtype)

        _accumulate(acc, prod, pl.program_id(2), nk, finish)

    in_specs = [pl.BlockSpec((tm, tk), lambda j, i, k: (i, k)), pl.BlockSpec((tn, tk), lambda j, i, k: (j, k))]
    args = [a, b]
    for extra in (add, dact):
        if extra is not None:
            in_specs.append(pl.BlockSpec((tm, tn), lambda j, i, k: (i, j)))
            args.append(extra)
    out_shape = [_sds((M, Ko), out_dtype)]
    out_specs = [pl.BlockSpec((tm, tn), lambda j, i, k: (i, j))]
    if dact is not None:
        out_shape.append(_sds((8, Ko), F32))
        out_specs.append(pl.BlockSpec((8, tn), lambda j, i, k: (0, j)))
    res = pl.pallas_call(
        body, name=name, grid=(Ko // tn, ni, nk), in_specs=in_specs, out_specs=out_specs, out_shape=out_shape,
        scratch_shapes=[pltpu.VMEM((tm, tn), F32)] if nk > 1 else [],
        compiler_params=_cp(("parallel", "arbitrary", "arbitrary")))(*args)
    return res if dact is not None else res[0]


def mm_tn(a, b, *, name, tm=1024, tn=1024, tk=1024):
    T, M = a.shape
    N = b.shape[1]
    tm, tn, tk = min(tm, M), min(tn, N), min(tk, T)

    def body(a_ref, b_ref, o_ref):
        @pl.when(pl.program_id(2) == 0)
        def _():
            o_ref[...] = jnp.zeros_like(o_ref)

        o_ref[...] += lax.dot_general(a_ref[...].astype(MM), b_ref[...].astype(MM), (((0,), (0,)), ((), ())),
                                      preferred_element_type=F32)

    return pl.pallas_call(
        body, name=name, grid=(M // tm, N // tn, T // tk),
        in_specs=[pl.BlockSpec((tk, tm), lambda i, j, k: (k, i)), pl.BlockSpec((tk, tn), lambda i, j, k: (k, j))],
        out_specs=pl.BlockSpec((tm, tn), lambda i, j, k: (i, j)), out_shape=_sds((M, N)),
        compiler_params=_cp(("parallel", "parallel", "arbitrary")))(a, b)


def _sigmoid(x):
    return 1.0 / (1.0 + jnp.exp(-x))


def _silu(x):
    return x * _sigmoid(x)


def _softplus(x):
    return jnp.maximum(x, 0.0) + jnp.log1p(jnp.exp(-jnp.abs(x)))


def _ext(main_ref, prev_ref, next_ref, first, last):
    p = jnp.where(first, 0.0, prev_ref[...].astype(F32))
    n = jnp.where(last, 0.0, next_ref[...].astype(F32))
    return jnp.concatenate([p, main_ref[...].astype(F32), n], axis=0)


def _shift_dn(x):
    return pltpu.roll(x, 1, 0)


def _shift_up(x):
    return pltpu.roll(x, x.shape[0] - 1, 0)


def _conv3(xe, w):
    return w[0:1, :] * _shift_dn(xe) + w[1:2, :] * xe + w[2:3, :] * _shift_up(xe)


def _conv3_t(de, w):
    return w[0:1, :] * _shift_up(de) + w[1:2, :] * de + w[2:3, :] * _shift_dn(de)


def _halo_specs(bt, T, col, lead=None):
    r = bt // HALO
    last = T // HALO - 1
    if lead is None:
        return [pl.BlockSpec((bt, D), lambda i: (i, col)),
                pl.BlockSpec((HALO, D), lambda i: (jnp.maximum(i * r - 1, 0), col)),
                pl.BlockSpec((HALO, D), lambda i: (jnp.minimum((i + 1) * r, last), col))]
    return [pl.BlockSpec((lead, bt, D), lambda i: (0, i, col)),
            pl.BlockSpec((lead, HALO, D), lambda i: (0, jnp.maximum(i * r - 1, 0), col)),
            pl.BlockSpec((lead, HALO, D), lambda i: (0, jnp.minimum((i + 1) * r, last), col))]


def _qkv_rows(cq, ck, cv):
    sq, sk, sv = _silu(cq), _silu(ck), _silu(cv)
    qs, ks = [], []
    for h in range(NH):
        s = slice(h * HD, (h + 1) * HD)
        qh, kh = sq[:, s], sk[:, s]
        qs.append(qh * lax.rsqrt(jnp.sum(qh * qh, axis=-1, keepdims=True) + L2_EPS) * (HD ** -0.5))
        ks.append(kh * lax.rsqrt(jnp.sum(kh * kh, axis=-1, keepdims=True) + L2_EPS))
    return jnp.concatenate(qs, axis=1), jnp.concatenate(ks, axis=1), sv


def _chunk_masks(bt):
    row = lax.broadcasted_iota(jnp.int32, (bt, bt), 0)
    col = lax.broadcasted_iota(jnp.int32, (bt, bt), 1)
    same = (row // CH) == (col // CH)
    lower = jnp.where(same & (col <= row), 1.0, 0.0).astype(F32)
    upper = jnp.where(same & (col >= row), 1.0, 0.0).astype(F32)
    return lower, upper


def _gate_rows(ab, gp, lower, upper):
    lane = lax.broadcasted_iota(jnp.int32, ab.shape, 1)
    g = -jnp.exp(gp[0:1, :]) * _softplus(ab + gp[1:2, :])
    g = jnp.where(lane < 2 * NH, g, 0.0)
    gf = jnp.dot(lower, g, precision=HI, preferred_element_type=F32)
    gr = jnp.dot(upper, g, precision=HI, preferred_element_type=F32)
    gc = jnp.where(lane < NH, gf, gr)
    beta = _sigmoid(ab)
    return jnp.where(lane < 2 * NH, gc, jnp.where(lane < 4 * NH, beta, 0.0))


def pre_qkv_fwd(pm, pab, cw, gp, *, bt=256):
    T = pm.shape[0]
    bt = min(bt, T)
    n = T // bt

    def body(q0, q1, q2, k0, k1, k2, v0, v1, v2, ab_ref, cw_ref, gp_ref, q_ref, k_ref, v_ref, gb_ref):
        i = pl.program_id(0)
        first, last = i == 0, i == n - 1
        cs = []
        for c, (m, p, x) in enumerate(((q0, q1, q2), (k0, k1, k2), (v0, v1, v2))):
            xe = _ext(m, p, x, first, last)
            cs.append(_conv3(xe, cw_ref[:, c * D:(c + 1) * D])[HALO:HALO + bt])
        q, k, v = _qkv_rows(*cs)
        q_ref[...], k_ref[...], v_ref[...] = q, k, v
        lower, upper = _chunk_masks(bt)
        gb_ref[...] = _gate_rows(ab_ref[...], gp_ref[...], lower, upper)

    in_specs = (_halo_specs(bt, T, CQ) + _halo_specs(bt, T, CK) + _halo_specs(bt, T, CV)
                + [pl.BlockSpec((bt, 128), lambda i: (i, 0)), pl.BlockSpec((8, 3 * D), lambda i: (0, 0)),
                   pl.BlockSpec((8, 128), lambda i: (0, 0))])
    row = pl.BlockSpec((bt, D), lambda i: (i, 0))
    return pl.pallas_call(
        body, name="pre_qkv_fwd", grid=(n,), in_specs=in_specs,
        out_specs=[row, row, row, pl.BlockSpec((bt, 128), lambda i: (i, 0))],
        out_shape=[_sds((T, D)), _sds((T, D)), _sds((T, D)), _sds((T, 128))],
        compiler_params=_cp(("parallel",)))(*([pm] * 9), pab, cw, gp)


def pre_qkv_bwd(pm, pab, cw, gp, dq2, dk2, dv2, dgb2, dpm, *, bt=256):
    T = pm.shape[0]
    bt = min(bt, T)
    n = T // bt
    E = bt + 2 * HALO

    def body(*refs):
        it = iter(refs)
        xs = [[next(it) for _ in range(3)] for _ in range(3)]
        ds = [[next(it) for _ in range(3)] for _ in range(3)]
        ab_ref, dgb_ref, cw_ref, gp_ref, _alias = next(it), next(it), next(it), next(it), next(it)
        o_ref, dab_ref, dcw_ref, dgp_ref = (next(it) for _ in range(4))
        i = pl.program_id(0)
        first, last = i == 0, i == n - 1

        @pl.when(first)
        def _():
            dcw_ref[...] = jnp.zeros_like(dcw_ref)
            dgp_ref[...] = jnp.zeros_like(dgp_ref)

        xes = [_ext(*xs[c], first, last) for c in range(3)]
        ces = [_conv3(xes[c], cw_ref[:, c * D:(c + 1) * D]) for c in range(3)]
        cts = []
        for c in range(3):
            m, p, x = ds[c]
            pe = jnp.where(first, 0.0, p[0] + p[1])
            ne = jnp.where(last, 0.0, x[0] + x[1])
            cts.append(jnp.concatenate([pe, m[0] + m[1], ne], axis=0))
        _, vjp = jax.vjp(_qkv_rows, *ces)
        dces = vjp(tuple(cts))
        rowi = lax.broadcasted_iota(jnp.int32, (E, 1), 0)
        central = (rowi >= HALO) & (rowi < HALO + bt)
        row8 = lax.broadcasted_iota(jnp.int32, (8, D), 0)
        for c in range(3):
            w = cw_ref[:, c * D:(c + 1) * D]
            o_ref[:, c * D:(c + 1) * D] = _conv3_t(dces[c], w)[HALO:HALO + bt].astype(o_ref.dtype)
            dc = jnp.where(central, dces[c], 0.0)
            taps = (jnp.sum(dc * _shift_dn(xes[c]), axis=0, keepdims=True),
                    jnp.sum(dc * xes[c], axis=0, keepdims=True),
                    jnp.sum(dc * _shift_up(xes[c]), axis=0, keepdims=True))
            upd = jnp.where(row8 == 0, taps[0], jnp.where(row8 == 1, taps[1], jnp.where(row8 == 2, taps[2], 0.0)))
            dcw_ref[:, c * D:(c + 1) * D] += upd
        lower, upper = _chunk_masks(bt)
        _, gvjp = jax.vjp(lambda ab, gp: _gate_rows(ab, gp, lower, upper), ab_ref[...], gp_ref[...])
        dab, dgp = gvjp(dgb_ref[0] + dgb_ref[1])
        dab_ref[...] = dab
        dgp_ref[...] += dgp

    in_specs = (_halo_specs(bt, T, CQ) + _halo_specs(bt, T, CK) + _halo_specs(bt, T, CV)
                + _halo_specs(bt, T, 0, lead=2) * 3
                + [pl.BlockSpec((bt, 128), lambda i: (i, 0)), pl.BlockSpec((2, bt, 128), lambda i: (0, i, 0)),
                   pl.BlockSpec((8, 3 * D), lambda i: (0, 0)), pl.BlockSpec((8, 128), lambda i: (0, 0)),
                   pl.BlockSpec(memory_space=pl.ANY)])
    out_specs = [pl.BlockSpec((bt, 3 * D), lambda i: (i, 0)), pl.BlockSpec((bt, 128), lambda i: (i, 0)),
                 pl.BlockSpec((8, 3 * D), lambda i: (0, 0)), pl.BlockSpec((8, 128), lambda i: (0, 0))]
    return pl.pallas_call(
        body, name="pre_qkv_bwd", grid=(n,), in_specs=in_specs, out_specs=out_specs,
        out_shape=[_sds(dpm.shape, dpm.dtype), _sds((T, 128)), _sds((8, 3 * D)), _sds((8, 128))],
        input_output_aliases={len(in_specs) - 1: 0},
        compiler_params=_cp(("arbitrary",)))(
            *([pm] * 9), dq2, dq2, dq2, dk2, dk2, dk2, dv2, dv2, dv2, pab, dgb2, cw, gp, dpm)


def pre_sc_fwd(pm, cw, *, bt=256):
    T = pm.shape[0]
    bt = min(bt, T)
    n = T // bt

    def body(b_ref, c0, c1, c2, x0, x1, x2, cw_ref, o_ref):
        i = pl.program_id(0)
        first, last = i == 0, i == n - 1
        pe = _ext(c0, c1, c2, first, last) * _ext(x0, x1, x2, first, last)
        o_ref[...] = (b_ref[...] * _conv3(pe, cw_ref[...])[HALO:HALO + bt]).astype(o_ref.dtype)

    in_specs = ([pl.BlockSpec((bt, D), lambda i: (i, CSB))] + _halo_specs(bt, T, CSC) + _halo_specs(bt, T, CSX)
                + [pl.BlockSpec((8, D), lambda i: (0, 0))])
    return pl.pallas_call(
        body, name="pre_sc_fwd", grid=(n,), in_specs=in_specs, out_specs=pl.BlockSpec((bt, D), lambda i: (i, 0)),
        out_shape=_sds((T, D), MM), compiler_params=_cp(("parallel",)))(*([pm] * 7), cw)


def pre_sc_bwd(pm, cw, dsc, dpm, *, bt=256):
    T = pm.shape[0]
    bt = min(bt, T)
    n = T // bt
    E = bt + 2 * HALO

    def body(b0, b1, b2, c0, c1, c2, x0, x1, x2, d0, d1, d2, cw_ref, _alias, o_ref, dcw_ref):
        i = pl.program_id(0)
        first, last = i == 0, i == n - 1

        @pl.when(first)
        def _():
            dcw_ref[...] = jnp.zeros_like(dcw_ref)

        ce, xe = _ext(c0, c1, c2, first, last), _ext(x0, x1, x2, first, last)
        pe = ce * xe
        w = cw_ref[...]
        dout = d0[...]
        o_ref[:, 0:D] = (dout * _conv3(pe, w)[HALO:HALO + bt]).astype(o_ref.dtype)
        dce = _ext(d0, d1, d2, first, last) * _ext(b0, b1, b2, first, last)
        dp = _conv3_t(dce, w)[HALO:HALO + bt]
        o_ref[:, D:2 * D] = (dp * x0[...]).astype(o_ref.dtype)
        o_ref[:, 2 * D:3 * D] = (dp * c0[...]).astype(o_ref.dtype)
        rowi = lax.broadcasted_iota(jnp.int32, (E, 1), 0)
        dc = jnp.where((rowi >= HALO) & (rowi < HALO + bt), dce, 0.0)
        row8 = lax.broadcasted_iota(jnp.int32, (8, D), 0)
        taps = (jnp.sum(dc * _shift_dn(pe), axis=0, keepdims=True), jnp.sum(dc * pe, axis=0, keepdims=True),
                jnp.sum(dc * _shift_up(pe), axis=0, keepdims=True))
        dcw_ref[...] += jnp.where(row8 == 0, taps[0], jnp.where(row8 == 1, taps[1], jnp.where(row8 == 2, taps[2], 0.0)))

    dsc_specs = [pl.BlockSpec((bt, D), lambda i: (i, 0)),
                 pl.BlockSpec((HALO, D), lambda i: (jnp.maximum(i * (bt // HALO) - 1, 0), 0)),
                 pl.BlockSpec((HALO, D), lambda i: (jnp.minimum((i + 1) * (bt // HALO), T // HALO - 1), 0))]
    in_specs = (_halo_specs(bt, T, CSB) + _halo_specs(bt, T, CSC) + _halo_specs(bt, T, CSX) + dsc_specs
                + [pl.BlockSpec((8, D), lambda i: (0, 0)), pl.BlockSpec(memory_space=pl.ANY)])
    return pl.pallas_call(
        body, name="pre_sc_bwd", grid=(n,), in_specs=in_specs,
        out_specs=[pl.BlockSpec((bt, 3 * D), lambda i: (i, 1)), pl.BlockSpec((8, D), lambda i: (0, 0))],
        out_shape=[_sds(dpm.shape, dpm.dtype), _sds((8, D))], input_output_aliases={len(in_specs) - 1: 0},
        compiler_params=_cp(("arbitrary",)))(*([pm] * 9), dsc, dsc, dsc, cw, dpm)


def _bdot(a, b, dims):
    return lax.dot_general(a.astype(MM), b.astype(MM), (dims, ((), ())), preferred_element_type=F32)


_NN, _NT, _TN = ((1,), (0,)), ((1,), (1,)), ((0,), (0,))


def _raw_nn(a, b):
    return _bdot(a, b, _NN)


def _raw_nt(a, b):
    return _bdot(a, b, _NT)


def _raw_tn(a, b):
    return _bdot(a, b, _TN)


def _make_vjp_ops():
    @jax.custom_vjp
    def nn(a, b):
        return _raw_nn(a, b)

    @jax.custom_vjp
    def nt(a, b):
        return _raw_nt(a, b)

    @jax.custom_vjp
    def tn(a, b):
        return _raw_tn(a, b)

    nn.defvjp(lambda a, b: (_raw_nn(a, b), (a, b)), lambda r, g: (_raw_nt(g, r[1]), _raw_tn(r[0], g)))
    nt.defvjp(lambda a, b: (_raw_nt(a, b), (a, b)), lambda r, g: (_raw_nn(g, r[1]), _raw_tn(g, r[0])))
    tn.defvjp(lambda a, b: (_raw_tn(a, b), (a, b)), lambda r, g: (_raw_nt(r[1], g), _raw_nn(r[0], g)))

    @jax.custom_vjp
    def inv_saved(A, Y):
        return Y

    def inv_bwd(Y, g):
        M = g + _raw_tn(Y, g)
        return -(M + _raw_nt(M, Y)), jnp.zeros_like(Y)

    inv_saved.defvjp(lambda A, Y: (Y, Y), inv_bwd)
    return nn, nt, tn, inv_saved


GH = 2
GR = GH * CH
NG = NH // GH
CBI = 4
CBB = 2


def _scan_chunks(n_chunks):
    return 8 if n_chunks % 8 == 0 else 2


def _tri_inv_y_all(As):
    Ys = [-A for A in As]
    Ps = [_raw_nn(A, A) for A in As]
    for stage in range(5):
        squares = [_raw_nn(P, P) for P in Ps] if stage < 4 else None
        Ys = [Y + P + _raw_nn(Y, P) for Y, P in zip(Ys, Ps)]
        Ps = squares
    return Ys


def _intra_groups(chains, incl, strict, eye, lastc, ops, inv_all):
    nn, nt, tn = ops
    st = []
    for qs, ks, vs, gcols, bcols in chains:
        q, k, v = (jnp.concatenate(t, axis=0) for t in (qs, ks, vs))
        gcol, bcol = jnp.concatenate(gcols, axis=0), jnp.concatenate(bcols, axis=0)
        grow = jnp.sum(eye * gcol, axis=0, keepdims=True)
        gam = jnp.where(incl, jnp.exp(jnp.where(incl, gcol - grow, 0.0)), 0.0)
        glast = jnp.sum(jnp.where(lastc, grow, 0.0), axis=1, keepdims=True)
        st.append((q, k, v, gcol, bcol, gam, glast, jnp.exp(gcol), k * bcol))
    As = [jnp.where(strict, nt(kb, k) * gam, 0.0) for (q, k, v, gcol, bcol, gam, glast, eg, kb) in st]
    Ys = inv_all(As)
    vbs = [v * bcol for (q, k, v, gcol, bcol, gam, glast, eg, kb) in st]
    kgs = [kb * eg for (q, k, v, gcol, bcol, gam, glast, eg, kb) in st]
    us = [vb + nn(Y, vb) for Y, vb in zip(Ys, vbs)]
    ws = [kg + nn(Y, kg) for Y, kg in zip(Ys, kgs)]
    Ps = [nt(q, k) * gam for (q, k, v, gcol, bcol, gam, glast, eg, kb) in st]
    return [((u, w, P, q * eg, k * jnp.exp(glast - gcol), jnp.exp(glast)), Y)
            for u, w, P, Y, (q, k, v, gcol, bcol, gam, glast, eg, kb) in zip(us, ws, Ps, Ys, st)]


def _scan_groups(chains, ops):
    nn, nt, tn, _ = ops
    vns = [[us[j] - nn(ws[j], Ss[j]) for j in range(GH)] for Ss, us, ws, P, qds, kds, egls in chains]
    os_ = [jnp.concatenate([nn(qds[j], Ss[j]) for j in range(GH)], axis=0) + nn(P, jnp.concatenate(vn, axis=0))
           for (Ss, us, ws, P, qds, kds, egls), vn in zip(chains, vns)]
    S2s = [[Ss[j] * egls[j] + tn(kds[j], vn[j]) for j in range(GH)]
           for (Ss, us, ws, P, qds, kds, egls), vn in zip(chains, vns)]
    return list(zip(os_, S2s))


def _group_masks(rev, rows=GR):
    r = lax.broadcasted_iota(jnp.int32, (rows, rows), 0)
    c = lax.broadcasted_iota(jnp.int32, (rows, rows), 1)
    same = (r // CH) == (c // CH)
    ahead = jnp.where(rev, c - r, r - c)
    incl = same & (ahead >= 0)
    strict = same & (ahead > 0)
    eye = jnp.where(r == c, 1.0, 0.0).astype(F32)
    lastc = same & ((c % CH) == jnp.where(rev, 0, CH - 1))
    return incl, strict, eye, lastc


def _head_gates(gb, h, rev):
    gcol = jnp.where(rev, gb[:, NH + h:NH + h + 1], gb[:, h:h + 1])
    bcol = jnp.where(rev, gb[:, 3 * NH + h:3 * NH + h + 1], gb[:, 2 * NH + h:2 * NH + h + 1])
    return gcol, bcol


def _hs(h):
    return slice(h * HD, (h + 1) * HD)


def _blockdiag(a, b):
    z = jnp.zeros_like(a)
    return jnp.concatenate([jnp.concatenate([a, z], axis=1), jnp.concatenate([z, b], axis=1)], axis=0)


def _load_chain(q_ref, k_ref, v_ref, gb_ref, c, g, rev, gh=GH):
    r = slice(c * CH, (c + 1) * CH)
    heads = range(g * gh, (g + 1) * gh)
    gates = [_head_gates(gb_ref[r, :], h, rev) for h in heads]
    return ([q_ref[r, _hs(h)] for h in heads], [k_ref[r, _hs(h)] for h in heads], [v_ref[r, _hs(h)] for h in heads],
            [t[0] for t in gates], [t[1] for t in gates])


def gdn_intra_fwd(q, k, v, gb):
    T = q.shape[0]
    N = T // CH
    ops = (_raw_nn, _raw_nt, _raw_tn)

    def body(q_ref, k_ref, v_ref, gb_ref, u_ref, w_ref, qd_ref, kd_ref, pp_ref, ys_ref, eg_ref):
        rev = pl.program_id(0) == 1
        masks = _group_masks(rev)
        where = [(c, g) for c in range(CBI) for g in range(NG)]
        chains = [_load_chain(q_ref, k_ref, v_ref, gb_ref, c, g, rev) for c, g in where]
        for (c, g), ((u, w, P, qd, kd, egl), Y) in zip(where, _intra_groups(chains, *masks, ops, _tri_inv_y_all)):
            r = slice(c * CH, (c + 1) * CH)
            pp_ref[c, g] = P.astype(MM)
            ys_ref[c, g] = Y.astype(MM)
            for j, h in enumerate(range(g * GH, (g + 1) * GH)):
                rows = slice(j * CH, (j + 1) * CH)
                u_ref[r, _hs(h)] = u[rows]
                w_ref[r, _hs(h)] = w[rows].astype(MM)
                qd_ref[r, _hs(h)] = qd[rows].astype(MM)
                kd_ref[r, _hs(h)] = kd[rows].astype(MM)
                eg_ref[c, h:h + 1, :] = jnp.broadcast_to(egl[j * CH:j * CH + 1, :], (1, 128))

    row = pl.BlockSpec((CBI * CH, D), lambda d, n: (n, 0))
    drow = pl.BlockSpec((None, CBI * CH, D), lambda d, n: (d, n, 0))
    mat = pl.BlockSpec((None, CBI, NG, GR, GR), lambda d, n: (d, n, 0, 0, 0))
    return pl.pallas_call(
        body, name="gdn_intra_fwd", grid=(2, N // CBI),
        in_specs=[row, row, row, pl.BlockSpec((CBI * CH, 128), lambda d, n: (n, 0))],
        out_specs=[drow] * 4 + [mat, mat, pl.BlockSpec((None, CBI, NH, 128), lambda d, n: (d, n, 0, 0))],
        out_shape=[_sds((2, T, D))] + [_sds((2, T, D), MM)] * 3 + [_sds((2, N, NG, GR, GR), MM)] * 2
                  + [_sds((2, N, NH, 128))],
        compiler_params=_cp(("parallel", "parallel")))(q, k, v, gb)


def gdn_scan_fwd(u, w, qd, kd, pp, eg):
    T = u.shape[1]
    N = T // CH
    ops = (_raw_nn, _raw_nt, _raw_tn, None)

    cbs = _scan_chunks(N)
    NB = N // cbs

    def body(u_ref, w_ref, qd_ref, kd_ref, pp_ref, eg_ref, o_ref, s0_ref, S):
        d = pl.program_id(0)

        @pl.when(pl.program_id(1) == 0)
        def _():
            S[...] = jnp.zeros_like(S)

        def chunk(c, carry):
            pc = c + d * (cbs - 1 - 2 * c)
            r = pl.ds(pl.multiple_of(pc * CH, CH), CH)
            chains = []
            for g in range(NG):
                heads = range(g * GH, (g + 1) * GH)
                Ss = [S[h] for h in heads]
                for h, Sh in zip(heads, Ss):
                    s0_ref[pc, h] = Sh
                chains.append((Ss, [u_ref[r, _hs(h)] for h in heads], [w_ref[r, _hs(h)] for h in heads], pp_ref[pc, g],
                               [qd_ref[r, _hs(h)] for h in heads], [kd_ref[r, _hs(h)] for h in heads],
                               [eg_ref[pc, h:h + 1, :] for h in heads]))
            for g, (o, S2) in enumerate(_scan_groups(chains, ops)):
                for j, h in enumerate(range(g * GH, (g + 1) * GH)):
                    o_ref[r, _hs(h)] = o[j * CH:(j + 1) * CH]
                    S[h] = S2[j]
            return carry

        lax.fori_loop(0, cbs, chunk, 0)

    bidx = lambda d, n: n + d * (NB - 1 - 2 * n)
    drow = pl.BlockSpec((None, cbs * CH, D), lambda d, n: (d, bidx(d, n), 0))
    mat = pl.BlockSpec((None, cbs, NG, GR, GR), lambda d, n: (d, bidx(d, n), 0, 0, 0))
    return pl.pallas_call(
        body, name="gdn_scan_fwd", grid=(2, NB),
        in_specs=[drow] * 4 + [mat, pl.BlockSpec((None, cbs, NH, 128), lambda d, n: (d, bidx(d, n), 0, 0))],
        out_specs=[drow, pl.BlockSpec((None, cbs, NH, HD, HD), lambda d, n: (d, bidx(d, n), 0, 0, 0))],
        out_shape=[_sds((2, T, D)), _sds((2, N, NH, HD, HD))],
        scratch_shapes=[pltpu.VMEM((NH, HD, HD), F32)],
        compiler_params=_cp(("arbitrary", "arbitrary")))(u, w, qd, kd, pp, eg)


def gdn_scan_bwd(u, w, qd, kd, pp, eg, s0, do):
    T = u.shape[1]
    N = T // CH
    ops = _make_vjp_ops()
    cbs = _scan_chunks(N)
    NB = N // cbs

    def body(u_ref, w_ref, qd_ref, kd_ref, pp_ref, eg_ref, s0_ref, do_ref,
             du_ref, dw_ref, dqd_ref, dkd_ref, dpp_ref, deg_ref, dS):
        d = pl.program_id(0)

        @pl.when(pl.program_id(1) == 0)
        def _():
            dS[...] = jnp.zeros_like(dS)

        def chunk(c, carry):
            pc = (cbs - 1 - c) + d * (2 * c - (cbs - 1))
            r = pl.ds(pl.multiple_of(pc * CH, CH), CH)
            chains, cts = [], []
            for g in range(NG):
                heads = range(g * GH, (g + 1) * GH)
                chains.append(([s0_ref[pc, h] for h in heads], [u_ref[r, _hs(h)] for h in heads],
                               [w_ref[r, _hs(h)].astype(F32) for h in heads], pp_ref[pc, g].astype(F32),
                               [qd_ref[r, _hs(h)].astype(F32) for h in heads],
                               [kd_ref[r, _hs(h)].astype(F32) for h in heads], [eg_ref[pc, h:h + 1, :] for h in heads]))
                cts.append((jnp.concatenate([do_ref[r, _hs(h)].astype(F32) for h in heads], axis=0),
                            [dS[h] for h in heads]))
            _, vjp = jax.vjp(lambda ch: _scan_groups(ch, ops), chains)
            (dchains,) = vjp(cts)
            for g, (dSs, dus, dws, dP, dqds, dkds, degs) in enumerate(dchains):
                dpp_ref[pc, g] = dP
                for j, h in enumerate(range(g * GH, (g + 1) * GH)):
                    dS[h] = dSs[j]
                    du_ref[r, _hs(h)] = dus[j].astype(MM)
                    dw_ref[r, _hs(h)] = dws[j].astype(MM)
                    dqd_ref[r, _hs(h)], dkd_ref[r, _hs(h)] = dqds[j], dkds[j]
                    deg_ref[pc, h:h + 1, :] = degs[j]
            return carry

        lax.fori_loop(0, cbs, chunk, 0)

    bidx = lambda d, n: (NB - 1 - n) + d * (2 * n - (NB - 1))
    drow = pl.BlockSpec((None, cbs * CH, D), lambda d, n: (d, bidx(d, n), 0))
    erow = pl.BlockSpec((None, cbs, NH, 128), lambda d, n: (d, bidx(d, n), 0, 0))
    mat = pl.BlockSpec((None, cbs, NG, GR, GR), lambda d, n: (d, bidx(d, n), 0, 0, 0))
    return pl.pallas_call(
        body, name="gdn_scan_bwd", grid=(2, NB),
        in_specs=[drow] * 4 + [mat, erow, pl.BlockSpec((None, cbs, NH, HD, HD), lambda d, n: (d, bidx(d, n), 0, 0, 0)),
                               pl.BlockSpec((cbs * CH, D), lambda d, n: (bidx(d, n), 0))],
        out_specs=[drow] * 4 + [mat, erow],
        out_shape=[_sds((2, T, D), MM)] * 2 + [_sds((2, T, D))] * 2 + [_sds((2, N, NG, GR, GR))]
                  + [_sds((2, N, NH, 128))],
        scratch_shapes=[pltpu.VMEM((NH, HD, HD), F32)],
        compiler_params=_cp(("arbitrary", "arbitrary")))(u, w, qd, kd, pp, eg, s0, do)


def gdn_intra_bwd(q, k, v, gb, ys, du, dw, dqd, dkd, dpp, deg):
    T = q.shape[0]
    N = T // CH
    nn, nt, tn, inv_saved = _make_vjp_ops()

    def body(q_ref, k_ref, v_ref, gb_ref, ys_ref, du_ref, dw_ref, dqd_ref, dkd_ref, dpp_ref, deg_ref,
             dq_ref, dk_ref, dv_ref, dgb_ref):
        rev = pl.program_id(0) == 1
        gh, rows = 2 * GH, 2 * GR
        masks = _group_masks(rev, rows)
        lane = lax.broadcasted_iota(jnp.int32, (CH, 128), 1)
        grow = lax.broadcasted_iota(jnp.int32, (rows, 1), 0)
        where = [(c, g) for c in range(CBB) for g in range(NG // 2)]
        pair = lambda ref, c, g: _blockdiag(ref[c, 2 * g].astype(F32), ref[c, 2 * g + 1].astype(F32))
        chains = [_load_chain(q_ref, k_ref, v_ref, gb_ref, c, g, rev, gh) for c, g in where]
        Ys = [pair(ys_ref, c, g) for c, g in where]
        inv_all = lambda As: [inv_saved(A, Y) for A, Y in zip(As, Ys)]
        _, vjp = jax.vjp(lambda ch: _intra_groups(ch, *masks, (nn, nt, tn), inv_all), chains)
        cts = []
        for c, g in where:
            r = slice(c * CH, (c + 1) * CH)
            heads = range(g * gh, (g + 1) * gh)
            stack = lambda ref: jnp.concatenate([ref[r, _hs(h)].astype(F32) for h in heads], axis=0)
            degl = jnp.zeros((rows, 1), F32)
            for j, h in enumerate(heads):
                degl = degl + jnp.where(grow == j * CH, jnp.sum(deg_ref[c, h:h + 1, :], axis=1, keepdims=True), 0.0)
            cts.append(((stack(du_ref), stack(dw_ref), pair(dpp_ref, c, g), stack(dqd_ref), stack(dkd_ref), degl),
                        jnp.zeros((rows, rows), F32)))
        (dchains,) = vjp(cts)
        dgbs = [jnp.zeros((CH, 128), F32) for _ in range(CBB)]
        for (c, g), (dqs, dks, dvs, dgs, dbs) in zip(where, dchains):
            r = slice(c * CH, (c + 1) * CH)
            for j, h in enumerate(range(g * gh, (g + 1) * gh)):
                dq_ref[r, _hs(h)], dk_ref[r, _hs(h)], dv_ref[r, _hs(h)] = dqs[j], dks[j], dvs[j]
                glane = jnp.where(rev, NH + h, h)
                dgbs[c] = dgbs[c] + jnp.where(lane == glane, dgs[j], 0.0) + jnp.where(lane == glane + 2 * NH, dbs[j], 0.0)
        for c in range(CBB):
            dgb_ref[c * CH:(c + 1) * CH, :] = dgbs[c]

    row = pl.BlockSpec((CBB * CH, D), lambda d, n: (n, 0))
    drow = pl.BlockSpec((None, CBB * CH, D), lambda d, n: (d, n, 0))
    mat = pl.BlockSpec((None, CBB, NG, GR, GR), lambda d, n: (d, n, 0, 0, 0))
    return pl.pallas_call(
        body, name="gdn_intra_bwd", grid=(2, N // CBB),
        in_specs=[row, row, row, pl.BlockSpec((CBB * CH, 128), lambda d, n: (n, 0)), mat, drow, drow, drow, drow, mat]
                 + [pl.BlockSpec((None, CBB, NH, 128), lambda d, n: (d, n, 0, 0))],
        out_specs=[drow, drow, drow, pl.BlockSpec((None, CBB * CH, 128), lambda d, n: (d, n, 0))],
        out_shape=[_sds((2, T, D))] * 3 + [_sds((2, T, 128))],
        compiler_params=_cp(("parallel", "parallel")))(q, k, v, gb, ys, du, dw, dqd, dkd, dpp, deg)


def _post_rows(o2a, o2b, z, nw):
    o = o2a + o2b
    outs = []
    for h in range(NH):
        s = slice(h * HD, (h + 1) * HD)
        oh = o[:, s]
        outs.append(oh * lax.rsqrt(jnp.mean(oh * oh, axis=-1, keepdims=True) + RMS_EPS) * nw * _silu(z[:, s]))
    return jnp.concatenate(outs, axis=1)


def post_fwd(o2, pm, nw, *, bt=512):
    T = pm.shape[0]
    bt = min(bt, T)

    def body(o_ref, z_ref, nw_ref, og_ref):
        og_ref[...] = _post_rows(o_ref[0], o_ref[1], z_ref[...], nw_ref[0:1, :]).astype(og_ref.dtype)

    return pl.pallas_call(
        body, name="post_fwd", grid=(T // bt,),
        in_specs=[pl.BlockSpec((2, bt, D), lambda i: (0, i, 0)), pl.BlockSpec((bt, D), lambda i: (i, CZ)),
                  pl.BlockSpec((8, 128), lambda i: (0, 0))],
        out_specs=pl.BlockSpec((bt, D), lambda i: (i, 0)), out_shape=_sds((T, D), MM),
        compiler_params=_cp(("parallel",)))(o2, pm, nw)


def post_bwd(o2, pm, nw, dog, dpm, *, bt=512):
    T = pm.shape[0]
    bt = min(bt, T)

    def body(o_ref, z_ref, nw_ref, dog_ref, _alias, do_ref, dz_ref, dnw_ref):
        @pl.when(pl.program_id(0) == 0)
        def _():
            dnw_ref[...] = jnp.zeros_like(dnw_ref)

        _, vjp = jax.vjp(_post_rows, o_ref[0], o_ref[1], z_ref[...], nw_ref[0:1, :])
        doa, _unused, dz, dnw = vjp(dog_ref[...])
        do_ref[...] = doa.astype(do_ref.dtype)
        dz_ref[...] = dz.astype(dz_ref.dtype)
        row8 = lax.broadcasted_iota(jnp.int32, (8, 128), 0)
        dnw_ref[...] += jnp.where(row8 == 0, dnw, 0.0)

    in_specs = [pl.BlockSpec((2, bt, D), lambda i: (0, i, 0)), pl.BlockSpec((bt, D), lambda i: (i, CZ)),
                pl.BlockSpec((8, 128), lambda i: (0, 0)), pl.BlockSpec((bt, D), lambda i: (i, 0)),
                pl.BlockSpec(memory_space=pl.ANY)]
    return pl.pallas_call(
        body, name="post_bwd", grid=(T // bt,), in_specs=in_specs,
        out_specs=[pl.BlockSpec((bt, D), lambda i: (i, 0)), pl.BlockSpec((bt, D), lambda i: (i, CZ)),
                   pl.BlockSpec((8, 128), lambda i: (0, 0))],
        out_shape=[_sds((T, D), MM), _sds(dpm.shape, dpm.dtype), _sds((8, 128))], input_output_aliases={4: 1},
        compiler_params=_cp(("arbitrary",)))(o2, pm, nw, dog, dpm)


def merge_fwd(ya, yb, pm, *, bt=512):
    T = pm.shape[0]
    bt = min(bt, T)

    def body(ya_ref, yb_ref, ga_ref, gb_ref, o_ref):
        o_ref[...] = (_sigmoid(ga_ref[...]) * ya_ref[...] + _sigmoid(gb_ref[...]) * yb_ref[...]).astype(o_ref.dtype)

    row = pl.BlockSpec((bt, D), lambda i: (i, 0))
    return pl.pallas_call(
        body, name="merge_fwd", grid=(T // bt,),
        in_specs=[row, row, pl.BlockSpec((bt, D), lambda i: (i, CGA)), pl.BlockSpec((bt, D), lambda i: (i, CGB))],
        out_specs=row, out_shape=_sds((T, D), MM), compiler_params=_cp(("parallel",)))(ya, yb, pm, pm)


def merge_bwd(ya, yb, pm, dmix, *, bt=512):
    T = pm.shape[0]
    bt = min(bt, T)

    def body(ya_ref, yb_ref, ga_ref, gb_ref, dm_ref, dya_ref, dyb_ref, dg_ref):
        dm = dm_ref[...]
        sa, sb = _sigmoid(ga_ref[...]), _sigmoid(gb_ref[...])
        dya_ref[...] = (dm * sa).astype(dya_ref.dtype)
        dyb_ref[...] = (dm * sb).astype(dyb_ref.dtype)
        dg_ref[:, 0:D] = (dm * ya_ref[...] * sa * (1.0 - sa)).astype(dg_ref.dtype)
        dg_ref[:, D:2 * D] = (dm * yb_ref[...] * sb * (1.0 - sb)).astype(dg_ref.dtype)

    row = pl.BlockSpec((bt, D), lambda i: (i, 0))
    return pl.pallas_call(
        body, name="merge_bwd", grid=(T // bt,),
        in_specs=[row, row, pl.BlockSpec((bt, D), lambda i: (i, CGA)), pl.BlockSpec((bt, D), lambda i: (i, CGB)), row],
        out_specs=[row, row, pl.BlockSpec((bt, 2 * D), lambda i: (i, CGA // 2))],
        out_shape=[_sds((T, D), MM), _sds((T, D), MM), _sds((T, NMAIN), MM)],
        compiler_params=_cp(("parallel",)))(ya, yb, pm, pm, dmix)


def _ln_rows(x, y, bias, g, b):
    r = ALPHA * x + y + bias
    mu = jnp.mean(r, axis=-1, keepdims=True)
    var = jnp.mean(jnp.square(r - mu), axis=-1, keepdims=True)
    return (r - mu) * lax.rsqrt(var + LN_EPS) * g + b


def ln_fwd(x, y, p, *, name, bt=512):
    T = x.shape[0]
    bt = min(bt, T)

    def body(x_ref, y_ref, p_ref, o_ref, ob_ref):
        r = _ln_rows(x_ref[...], y_ref[...], p_ref[0:1, :], p_ref[1:2, :], p_ref[2:3, :])
        o_ref[...] = r
        ob_ref[...] = r.astype(ob_ref.dtype)

    row = pl.BlockSpec((bt, D), lambda i: (i, 0))
    return pl.pallas_call(
        body, name=name, grid=(T // bt,), in_specs=[row, row, pl.BlockSpec((8, D), lambda i: (0, 0))],
        out_specs=[row, row], out_shape=[_sds((T, D)), _sds((T, D), MM)],
        compiler_params=_cp(("parallel",)))(x, y, p)


def ln_bwd(x, y, p, ct, ct2=None, *, name, bt=512):
    T = x.shape[0]
    bt = min(bt, T)

    def body(*refs):
        it = iter(refs)
        x_ref, y_ref, p_ref, c_ref = next(it), next(it), next(it), next(it)
        c2_ref = next(it) if ct2 is not None else None
        dxa_ref, dr_ref, dp_ref = next(it), next(it), next(it)

        @pl.when(pl.program_id(0) == 0)
        def _():
            dp_ref[...] = jnp.zeros_like(dp_ref)

        c = c_ref[...]
        if c2_ref is not None:
            c = c + c2_ref[...]
        _, vjp = jax.vjp(_ln_rows, x_ref[...], y_ref[...], p_ref[0:1, :], p_ref[1:2, :], p_ref[2:3, :])
        _dx, dy, dbias, dg, db = vjp(c)
        dxa_ref[...] = ALPHA * dy
        dr_ref[...] = dy.astype(dr_ref.dtype)
        row8 = lax.broadcasted_iota(jnp.int32, (8, D), 0)
        dp_ref[...] += jnp.where(row8 == 0, dbias, jnp.where(row8 == 1, dg, jnp.where(row8 == 2, db, 0.0)))

    row = pl.BlockSpec((bt, D), lambda i: (i, 0))
    in_specs = [row, row, pl.BlockSpec((8, D), lambda i: (0, 0)), row] + ([row] if ct2 is not None else [])
    args = [x, y, p, ct] + ([ct2] if ct2 is not None else [])
    return pl.pallas_call(
        body, name=name, grid=(T // bt,), in_specs=in_specs,
        out_specs=[row, row, pl.BlockSpec((8, D), lambda i: (0, 0))],
        out_shape=[_sds((T, D)), _sds((T, D), MM), _sds((8, D))],
        compiler_params=_cp(("arbitrary",)))(*args)


def loss_fwd_bwd(xl, target, *, bt=512):
    T = xl.shape[0]
    bt = min(bt, T)

    def body(x_ref, t_ref, l_ref, d_ref):
        @pl.when(pl.program_id(0) == 0)
        def _():
            l_ref[...] = jnp.zeros_like(l_ref)

        e = x_ref[...] - t_ref[...]
        d_ref[...] = e * (1.0 / D)
        l_ref[...] += 0.5 * jnp.sum(jnp.mean(e * e, axis=-1, keepdims=True), axis=0, keepdims=True)

    row = pl.BlockSpec((bt, D), lambda i: (i, 0))
    return pl.pallas_call(
        body, name="loss", grid=(T // bt,), in_specs=[row, row],
        out_specs=[pl.BlockSpec((8, 128), lambda i: (0, 0)), row], out_shape=[_sds((8, 128)), _sds((T, D))],
        compiler_params=_cp(("arbitrary",)))(xl, target)


def _row_tile(R, Cc, elems=1 << 18):
    if R * Cc <= elems:
        return R
    tr = 8
    while tr * 2 * Cc <= elems and R % (tr * 2) == 0:
        tr *= 2
    return tr


def adam(w, m, v, ga, gb=None, *, name):
    R, Cc = w.shape
    tr = _row_tile(R, Cc)

    def body(*refs):
        it = iter(refs)
        w_ref, m_ref, v_ref, a_ref = next(it), next(it), next(it), next(it)
        b_ref = next(it) if gb is not None else None
        g_ref, d_ref, mo_ref, vo_ref = next(it), next(it), next(it), next(it)
        g = a_ref[...]
        if b_ref is not None:
            g = g + b_ref[...]
        m2 = B1 * m_ref[...] + (1.0 - B1) * g
        v2 = B2 * v_ref[...] + (1.0 - B2) * jnp.square(g)
        m_hat = m2 / (1.0 - B1 ** STEP)
        v_hat = v2 / (1.0 - B2 ** STEP)
        g_ref[...] = g
        d_ref[...] = -LR * (m_hat / (jnp.sqrt(v_hat) + EPS) + WD * w_ref[...])
        mo_ref[...] = m2
        vo_ref[...] = v2

    blk = pl.BlockSpec((tr, Cc), lambda i: (i, 0))
    args = [w, m, v, ga] + ([gb] if gb is not None else [])
    return pl.pallas_call(
        body, name=name, grid=(R // tr,), in_specs=[blk] * len(args), out_specs=[blk] * 4,
        out_shape=[_sds((R, Cc))] * 4, compiler_params=_cp(("parallel",)))(*args)


def sum4(parts, *, name):
    _, R, Cc = parts.shape
    tr = _row_tile(R, Cc)

    def body(p_ref, out_ref):
        f = lambda t: t.astype(F32)
        out_ref[...] = ((f(p_ref[0]) + f(p_ref[1])) + f(p_ref[2])) + f(p_ref[3])

    return pl.pallas_call(
        body, name=name, grid=(R // tr,), in_specs=[pl.BlockSpec((4, tr, Cc), lambda i: (0, i, 0))],
        out_specs=pl.BlockSpec((tr, Cc), lambda i: (i, 0)), out_shape=_sds((R, Cc)),
        compiler_params=_cp(("parallel",)))(parts)


def _place():
    return lax.axis_index("x"), lax.axis_index("y"), lax.axis_index("c")


def _other_chips(x, y):
    return [(1 - x, y), (x, 1 - y), (1 - x, 1 - y)]


_ANY = pl.BlockSpec(memory_space=pl.ANY)


def allgather_xy(arrs):
    n = len(arrs)
    axes = [0 if a.shape[0] % 2 == 0 else 1 for a in arrs]
    halves = [a.shape[ax] // 2 for a, ax in zip(arrs, axes)]

    def half_of(ref, a, which):
        part = pl.ds(which * halves[a], halves[a])
        return ref.at[part] if axes[a] == 0 else ref.at[:, part]

    def body(*refs):
        ins, outs = refs[:n], refs[n:2 * n]
        send, recv, fsend, frecv, loc = refs[2 * n:]
        x, y, c = _place()
        me = 2 * x + y
        peers = _other_chips(x, y)
        local = [pltpu.make_async_copy(ins[a], outs[a].at[me], loc.at[a]) for a in range(n)]
        for cp in local:
            cp.start()

        def over_ici(a, j, block, src=None):
            px, py = peers[j]
            dst = half_of(outs[a].at[block], a, c)
            return pltpu.make_async_remote_copy(
                src_ref=dst if src is None else half_of(src, a, c), dst_ref=dst, send_sem=send.at[3 * a + j],
                recv_sem=recv.at[3 * a + j], device_id=(px, py, c), device_id_type=MESH)

        def over_d2d(a, j, half):
            px, py = peers[j]
            rows = half_of(outs[a].at[2 * px + py], a, half)
            return pltpu.make_async_remote_copy(
                src_ref=rows, dst_ref=rows, send_sem=fsend.at[3 * a + j], recv_sem=frecv.at[3 * a + j],
                device_id=(x, y, 1 - c), device_id_type=MESH)

        sends = [over_ici(a, j, me, src=ins[a]) for a in range(n) for j in range(3)]
        for cp in sends:
            cp.start()
        passed = []
        for a in range(n):
            for j, (px, py) in enumerate(peers):
                over_ici(a, j, 2 * px + py).wait_recv()
                passed.append(over_d2d(a, j, c))
                passed[-1].start()
        for a in range(n):
            for j in range(3):
                over_d2d(a, j, 1 - c).wait_recv()
        for cp in sends + passed:
            cp.wait_send()
        for cp in local:
            cp.wait()

    return pl.pallas_call(
        body, name="allgather_xy", in_specs=[_ANY] * n, out_specs=[_ANY] * n,
        out_shape=[_sds((4,) + a.shape, a.dtype) for a in arrs],
        scratch_shapes=[pltpu.SemaphoreType.DMA((3 * n,))] * 4 + [pltpu.SemaphoreType.DMA((n,))],
        compiler_params=pltpu.CompilerParams(has_side_effects=True))(*arrs)


_HBM = pl.BlockSpec(memory_space=pltpu.HBM)
_SEM = pl.BlockSpec(memory_space=pltpu.SEMAPHORE)
_EFFECT = pltpu.SideEffectType.DATAFLOW_SIDE_EFFECTING


def _xy_copy(kind, layer, src, land, send, recv, a, j, c, arriving):
    x, y = lax.axis_index("x"), lax.axis_index("y")
    px, py = ((1 - x, y), (x, 1 - y), (1 - x, 1 - y), (x, y))[j]
    if kind == "gather":
        dst = land.at[2 * px + py] if arriving else land.at[2 * x + y]
        src_view = dst if arriving else src
    else:
        dst = land.at[j, layer]
        src_view = src.at[2 * px + py]
    return pltpu.make_async_remote_copy(src_ref=src_view, dst_ref=dst, send_sem=send.at[4 * a + j],
                                        recv_sem=recv.at[4 * a + j], device_id=(px, py, c), device_id_type=MESH)


def xy_start(kind, name, arrs, after, zones=None, layer=0):
    n = len(arrs)
    if zones is None:
        shape = lambda a: ((4,) + a.shape) if kind == "gather" else ((4, DEPTH) + a.shape[1:])
        zones = [lax.empty(shape(a), a.dtype) for a in arrs]

    def body(*refs):
        ins, lands = refs[:n], refs[n:2 * n]
        send, recv = refs[2 * n + 1], refs[2 * n + 2]
        token = refs[-1]
        c = lax.axis_index("c")
        for a in range(n):
            for j in range(4):
                _xy_copy(kind, layer, ins[a], lands[a], send, recv, a, j, c, False).start()
        token[...] = jnp.zeros_like(token)

    hbm = lambda v: pltpu.with_memory_space_constraint(v, pltpu.HBM)
    outs = pl.pallas_call(
        body, name=name,
        out_shape=(pltpu.SemaphoreType.DMA((4 * n,)), pltpu.SemaphoreType.DMA((4 * n,)),
                   *[pltpu.HBM(a.shape, a.dtype) for a in arrs], *[pltpu.HBM(z.shape, z.dtype) for z in zones],
                   _sds((8, 128))),
        in_specs=[_HBM] * (2 * n) + [_ANY],
        out_specs=(_SEM, _SEM, *[_HBM] * (2 * n), pl.BlockSpec(memory_space=pltpu.VMEM)),
        input_output_aliases={i: 2 + i for i in range(2 * n)},
        compiler_params=pltpu.CompilerParams(has_side_effects=_EFFECT))(
            *[hbm(a) for a in arrs], *[hbm(z) for z in zones], after)
    return (kind, layer, outs[0], outs[1], list(outs[2:2 + n])), list(outs[2 + n:2 + 2 * n]), outs[-1]


def xy_wait(name, handles, zones, after):
    kind, layer, send_sems, recv_sems, srcs = handles
    n = len(srcs)

    def body(*refs):
        ins, lands = refs[:n], refs[n:2 * n]
        send, recv = refs[2 * n], refs[2 * n + 1]
        c = lax.axis_index("c")
        for a in range(n):
            for j in range(4):
                _xy_copy(kind, layer, ins[a], lands[a], send, recv, a, j, c, False).wait_send()
                _xy_copy(kind, layer, ins[a], lands[a], send, recv, a, j, c, True).wait_recv()

    outs = pl.pallas_call(
        body, name=name,
        out_shape=tuple(pltpu.HBM(v.shape, v.dtype) for v in srcs + zones),
        in_specs=[_HBM] * (2 * n) + [_SEM, _SEM, _ANY], out_specs=tuple([_HBM] * (2 * n)),
        input_output_aliases={i: i for i in range(2 * n)},
        compiler_params=pltpu.CompilerParams(has_side_effects=_EFFECT))(*srcs, *zones, send_sems, recv_sems, after)
    return list(outs[n:])


def swap_c(arrs):
    n = len(arrs)

    def body(*refs):
        ins, outs = refs[:n], refs[n:2 * n]
        send, recv = refs[2 * n:]
        x, y, c = _place()
        cps = [pltpu.make_async_remote_copy(src_ref=ins[a], dst_ref=outs[a], send_sem=send.at[a], recv_sem=recv.at[a],
                                            device_id=(x, y, 1 - c), device_id_type=MESH) for a in range(n)]
        for cp in cps:
            cp.start()
        for cp in cps:
            cp.wait_recv()
        for cp in cps:
            cp.wait_send()

    return pl.pallas_call(
        body, name="swap_c", in_specs=[_ANY] * n, out_specs=[_ANY] * n, out_shape=[_sds(a.shape, a.dtype) for a in arrs],
        scratch_shapes=[pltpu.SemaphoreType.DMA((n,)), pltpu.SemaphoreType.DMA((n,))],
        compiler_params=pltpu.CompilerParams(has_side_effects=True))(*arrs)


def allreduce_small(v):
    R = v.shape[0]

    def body(v_ref, o_ref, buf, send, recv):
        x, y, c = _place()
        me = 4 * x + 2 * y + c
        buf[0] = v_ref[...]

        def cp(k):
            dx, dy, dc = (k >> 2) & 1, (k >> 1) & 1, k & 1
            return pltpu.make_async_remote_copy(
                src_ref=v_ref, dst_ref=buf.at[k], send_sem=send.at[k - 1], recv_sem=recv.at[k - 1],
                device_id=(x ^ dx, y ^ dy, c ^ dc), device_id_type=MESH)

        cps = [cp(k) for k in range(1, 8)]
        for t in cps:
            t.start()
        for t in cps:
            t.wait_recv()
        acc = buf[me]
        for dev in range(1, 8):
            acc = acc + buf[jnp.bitwise_xor(me, dev)]
        o_ref[...] = acc
        for t in cps:
            t.wait_send()

    vm = pl.BlockSpec(memory_space=pltpu.VMEM)
    return pl.pallas_call(
        body, name="allreduce_small", in_specs=[vm], out_specs=vm, out_shape=_sds((R, 128)),
        scratch_shapes=[pltpu.VMEM((8, R, 128), F32), pltpu.SemaphoreType.DMA((7,)), pltpu.SemaphoreType.DMA((7,))],
        compiler_params=pltpu.CompilerParams(has_side_effects=True, vmem_limit_bytes=VMEM_LIMIT))(v)


def _rows8(*rows):
    n = rows[0].shape[-1]
    t = jnp.stack([r.reshape(n).astype(F32) for r in rows])
    return jnp.pad(t, ((0, 8 - len(rows)), (0, 0)))


def _lanes128(a):
    f = a.reshape(-1).astype(F32)
    return jnp.pad(f, (0, 128 - f.shape[0]))


def _layer_fwd(x, xb, W):
    pm = mm_nn(xb, W["w_main"], name="proj_main", tm=2048)
    pab = mm_nn(xb, W["w_ab"], name="proj_ab")
    q, k, v, gb = pre_qkv_fwd(pm, pab, W["cw"], W["gp"])
    sc = pre_sc_fwd(pm, W["csc"])
    u, w, qd, kd, pp, ys, eg = gdn_intra_fwd(q, k, v, gb)
    o2, s0 = gdn_scan_fwd(u, w, qd, kd, pp, eg)
    og = post_fwd(o2, pm, W["nw"])
    ya = mm_nn(og, W["w_og"], name="proj_og")
    yb = mm_nn(sc, W["w_osc"], name="proj_osc")
    mixed = merge_fwd(ya, yb, pm)
    out = mm_nn(mixed, W["w_out"], name="proj_out")
    x1, x1b = ln_fwd(x, out, W["ln1"], name="ln1_fwd")
    hfac, h = mm_nn(x1b, W["w_up"], bias=W["b_up"], relu2=True, out_dtype=MM, name="mlp_up", tm=2048)
    dn = mm_nn(h, W["w_down"], name="mlp_down", tk=DFF)
    x2, x2b = ln_fwd(x1, dn, W["ln2"], name="ln2_fwd")
    saved = dict(x=x, xb=xb, pm=pm, pab=pab, q=q, k=k, v=v, gb=gb, sc=sc, o2=o2, s0=s0, og=og, ya=ya, yb=yb,
                 u=u, w=w, qd=qd, kd=kd, pp=pp, ys=ys, eg=eg,
                 mixed=mixed, out=out, x1=x1, x1b=x1b, hfac=hfac, h=h, dn=dn)
    return x2, x2b, saved


def _layer_bwd(ct, W, S, early=None):
    dxa2, dr2b, dp2 = ln_bwd(S["x1"], S["dn"], W["ln2"], ct, name="ln2_bwd")
    g_down = mm_tn(S["h"], dr2b, name="dw_down", tm=2048, tk=2048)
    dhpre, db_up = mm_nt(dr2b, W["w_down"], dact=S["hfac"], out_dtype=MM, name="mlp_down_bwd", tm=2048)
    g_up = mm_tn(S["x1b"], dhpre, name="dw_up", tn=2048, tk=2048)
    dx1 = mm_nt(dhpre, W["w_up"], add=dxa2, name="mlp_up_bwd", tk=DFF)
    dxa1, dr1b, dp1 = ln_bwd(S["x"], S["out"], W["ln1"], dx1, name="ln1_bwd")
    g_out = mm_tn(S["mixed"], dr1b, name="dw_out")
    dmix = mm_nt(dr1b, W["w_out"], name="proj_out_bwd")
    dya, dyb, dpm = merge_bwd(S["ya"], S["yb"], S["pm"], dmix)
    g_og = mm_tn(S["og"], dya, name="dw_og")
    g_osc = mm_tn(S["sc"], dyb, name="dw_osc")
    dog = mm_nt(dya, W["w_og"], name="proj_og_bwd")
    dsc = mm_nt(dyb, W["w_osc"], name="proj_osc_bwd")
    token = None if early is None else early(dict(w_o_gdn=g_og, w_o_sc=g_osc, w_out=g_out, w_up=g_up, w_down=g_down))
    nw = W["nw"] if token is None else W["nw"] + token[0, 0]
    do, dpm, dnw = post_bwd(S["o2"], S["pm"], nw, dog, dpm)
    du, dw, dqd, dkd, dpp, deg = gdn_scan_bwd(S["u"], S["w"], S["qd"], S["kd"], S["pp"], S["eg"], S["s0"], do)
    dq2, dk2, dv2, dgb2 = gdn_intra_bwd(S["q"], S["k"], S["v"], S["gb"], S["ys"], du, dw, dqd, dkd, dpp, deg)
    dpm, dpab, dcw, dgp = pre_qkv_bwd(S["pm"], S["pab"], W["cw"], W["gp"], dq2, dk2, dv2, dgb2, dpm)
    dpm, dcsc = pre_sc_bwd(S["pm"], W["csc"], dsc, dpm)
    g_main = mm_tn(S["xb"], dpm, name="dw_main", tn=NMAIN // 4)
    g_ab = mm_tn(S["xb"], dpab, name="dw_ab")
    t = mm_nt(dpab, W["w_ab"], add=dxa1, name="proj_ab_bwd")
    dx = mm_nt(dpm, W["w_main"], add=t, name="proj_main_bwd", tk=NMAIN // 3)
    g_in = jnp.concatenate([g_main[:, :3 * D], g_main[:, 8 * D:], g_ab[:, :4 * NH], g_main[:, 3 * D:8 * D]], axis=1)
    grads = dict(
        w_in=g_in, w_o_gdn=g_og, w_o_sc=g_osc, w_out=g_out, w_up=g_up, w_down=g_down,
        conv_qkv=dcw[:3], conv_sc=dcsc[:3], a_log=dgp[0, :2 * NH].reshape(2, NH), dt_bias=dgp[1, :2 * NH].reshape(2, NH),
        gdn_norm_w=dnw[0], ln1_g=dp1[1], ln1_b=dp1[2], b_up=db_up[0], b_down=dp2[0], ln2_g=dp2[1], ln2_b=dp2[2])
    return dx, grads


def _layer_weights(l, full, i, conv, a_log, dt_bias, gdn_norm_w, ln1_g, ln1_b, b_up, b_down, ln2_g, ln2_b):
    w_in = full["w_in"][i]
    w_main = jnp.concatenate([w_in[:, :3 * D], w_in[:, 4 * D + 4 * NH:], w_in[:, 3 * D:4 * D]], axis=1)
    w_ab = jnp.pad(w_in[:, 4 * D:4 * D + 4 * NH], ((0, 0), (0, 128 - 4 * NH)))
    return dict(
        w_main=w_main, w_ab=w_ab, w_og=full["w_o_gdn"][i], w_osc=full["w_o_sc"][i], w_out=full["w_out"][i],
        w_up=full["w_up"][i], w_down=full["w_down"][i],
        cw=jnp.pad(conv["conv_qkv"][l].astype(F32), ((0, 5), (0, 0))),
        csc=jnp.pad(conv["conv_sc"][l].astype(F32), ((0, 5), (0, 0))),
        gp=_rows8(_lanes128(a_log[l]), _lanes128(dt_bias[l])), nw=_rows8(gdn_norm_w[l]),
        ln1=_rows8(jnp.zeros((D,), F32), ln1_g[l], ln1_b[l]), ln2=_rows8(b_down[l], ln2_g[l], ln2_b[l]),
        b_up=b_up[l].reshape(1, DFF).astype(F32))


def local_step(xs, target, weights_of, after_bwd, mid_bwd=None):
    x, xb = xs, xs.astype(MM)
    Ws, saved = [], []
    for l in range(DEPTH):
        Ws.append(weights_of(l, x))
        x, xb, S = _layer_fwd(x, xb, Ws[l])
        saved.append(S)
    loss_tile, ct = loss_fwd_bwd(x, target)
    token = None
    for l in reversed(range(DEPTH)):
        W = Ws[l] if token is None else dict(Ws[l], ln2=Ws[l]["ln2"] + token[0, 0])
        ct, grads = _layer_bwd(ct, W, saved[l], None if mid_bwd is None else (lambda part, l=l: mid_bwd(l, part)))
        token = after_bwd(l, grads)
    return loss_tile, ct


EARLY = 1
BIG = ("w_in", "w_o_gdn", "w_o_sc", "w_out", "w_up", "w_down")
SMALL = ("conv_qkv", "a_log", "dt_bias", "gdn_norm_w", "conv_sc", "ln1_g", "ln1_b", "b_up", "b_down", "ln2_g", "ln2_b")
ORDER = ("w_in", "conv_qkv", "a_log", "dt_bias", "gdn_norm_w", "w_o_gdn", "conv_sc", "w_o_sc", "w_out", "ln1_g",
         "ln1_b", "w_up", "b_up", "w_down", "b_down", "ln2_g", "ln2_b")


def _pack(arrs):
    flat = jnp.concatenate([a.reshape(-1).astype(F32) for a in arrs])
    n = flat.shape[0]
    rows = -(-n // 1024) * 8
    return jnp.pad(flat, (0, rows * 128 - n)).reshape(rows, 128)


def _unpack(buf, like):
    flat = buf.reshape(-1)
    out, o = [], 0
    for a in like:
        n = 1
        for s in a.shape:
            n *= s
        out.append(flat[o:o + n].reshape(a.shape))
        o += n
    return out


def _gathered(name, g):
    if name in ("w_in", "w_up", "conv_qkv", "conv_sc"):
        t = jnp.moveaxis(g, 0, -2)
        return t.reshape(t.shape[:-2] + (t.shape[-2] * t.shape[-1],))
    t = jnp.moveaxis(g, 0, 1)
    return t.reshape((t.shape[0], t.shape[1] * t.shape[2]) + t.shape[3:])


def _by_chip(name, g):
    if name in ("w_in", "w_up"):
        r, ccols = g.shape
        return jnp.moveaxis(g.reshape(r, 4, ccols // 4), 1, 0)
    return g.reshape((4, g.shape[0] // 4) + g.shape[1:])


def kernel(x, w_in, conv_qkv, a_log, dt_bias, gdn_norm_w, w_o_gdn, conv_sc, w_o_sc, w_out, ln1_g, ln1_b, w_up, b_up, w_down, b_down, ln2_g, ln2_b, loss_target, m_w_in, m_conv_qkv, m_a_log, m_dt_bias, m_gdn_norm_w, m_w_o_gdn, m_conv_sc, m_w_o_sc, m_w_out, m_ln1_g, m_ln1_b, m_w_up, m_b_up, m_w_down, m_b_down, m_ln2_g, m_ln2_b, v_w_in, v_conv_qkv, v_a_log, v_dt_bias, v_gdn_norm_w, v_w_o_gdn, v_conv_sc, v_w_o_sc, v_w_out, v_ln1_g, v_ln1_b, v_w_up, v_b_up, v_w_down, v_b_down, v_ln2_g, v_ln2_b):
    w = dict(w_in=w_in, conv_qkv=conv_qkv, a_log=a_log, dt_bias=dt_bias, gdn_norm_w=gdn_norm_w, w_o_gdn=w_o_gdn,
             conv_sc=conv_sc, w_o_sc=w_o_sc, w_out=w_out, ln1_g=ln1_g, ln1_b=ln1_b, w_up=w_up, b_up=b_up,
             w_down=w_down, b_down=b_down, ln2_g=ln2_g, ln2_b=ln2_b)
    m = dict(w_in=m_w_in, conv_qkv=m_conv_qkv, a_log=m_a_log, dt_bias=m_dt_bias, gdn_norm_w=m_gdn_norm_w,
             w_o_gdn=m_w_o_gdn, conv_sc=m_conv_sc, w_o_sc=m_w_o_sc, w_out=m_w_out, ln1_g=m_ln1_g, ln1_b=m_ln1_b,
             w_up=m_w_up, b_up=m_b_up, w_down=m_w_down, b_down=m_b_down, ln2_g=m_ln2_g, ln2_b=m_ln2_b)
    v = dict(w_in=v_w_in, conv_qkv=v_conv_qkv, a_log=v_a_log, dt_bias=v_dt_bias, gdn_norm_w=v_gdn_norm_w,
             w_o_gdn=v_w_o_gdn, conv_sc=v_conv_sc, w_o_sc=v_w_o_sc, w_out=v_w_out, ln1_g=v_ln1_g, ln1_b=v_ln1_b,
             w_up=v_w_up, b_up=v_b_up, w_down=v_w_down, b_down=v_b_down, ln2_g=v_ln2_g, ln2_b=v_ln2_b)
    chip = 2 * lax.axis_index("x") + lax.axis_index("y")

    names = BIG + ("conv_qkv", "conv_sc")
    blocks = [w[n].astype(MM) if n in BIG else w[n] for n in names]
    got = allgather_xy([b[:EARLY] if n in BIG else b for n, b in zip(names, blocks)])
    early = {n: _gathered(n, g) for n, g in zip(names, got)}
    gather, gather_zones, token = xy_start("gather", "gather_start", [b[EARLY:] for b in blocks[:len(BIG)]],
                                           after=got[0])
    vectors = (a_log, dt_bias, gdn_norm_w, ln1_g, ln1_b, b_up, b_down, ln2_g, ln2_b)
    late, grads = {}, [None] * DEPTH
    scatters = {"mid": [None] * DEPTH, "end": [None] * DEPTH}
    zones = {"mid": None, "end": None}

    def weights_of(l, x_l):
        if l < EARLY:
            return _layer_weights(l, early, l, early, *vectors)
        if not late:
            for n, zone in zip(BIG, xy_wait("gather_wait", gather, gather_zones, after=x_l)):
                late[n] = _gathered(n, zone)
        return _layer_weights(l, late, l - EARLY, early, *vectors)

    def scatter(when, l, g, which):
        scatters[when][l], zones[when], tok = xy_start(
            "scatter", "scatter_%s_start_%d" % (when, l), [_by_chip(n, g[n]).astype(MM) for n in which], after=a_log,
            zones=zones[when], layer=l)
        return tok

    def after_bwd(l, g):
        grads[l] = g
        return scatter("end", l, g, BIG[:1])

    loss_tile, dx = local_step(x[0] + token[0, 0], loss_target[0], weights_of, after_bwd,
                               lambda l, g: scatter("mid", l, g, BIG[1:]))
    loss = lax.psum(loss_tile[0, 0], ("x", "y", "c"))

    for when in ("mid", "end"):
        for l in range(DEPTH):
            zones[when] = xy_wait("scatter_%s_wait_%d" % (when, l), scatters[when][l], zones[when], after=dx)
    part = [sum4(z.reshape(4, -1, z.shape[-1]), name="sum_" + n) for n, z in zip(BIG, zones["end"] + zones["mid"])]
    other = swap_c(part)
    out = {}
    for n, mine, theirs in zip(BIG, part, other):
        cols = mine.shape[-1]
        res = adam(w[n].reshape(-1, cols), m[n].reshape(-1, cols), v[n].reshape(-1, cols), mine, theirs, name="adam_" + n)
        out[n] = [r.reshape(w[n].shape) for r in res]

    stacked = [jnp.stack([grads[l][n] for l in range(DEPTH)]) for n in SMALL]
    summed = _unpack(allreduce_small(_pack(stacked)), stacked)
    gs = []
    for n, g in zip(SMALL, summed):
        if n in ("conv_qkv", "conv_sc"):
            blk = w[n].shape[-1]
            g = lax.dynamic_slice_in_dim(g, chip * blk, blk, axis=2)
        gs.append(g)
    res = adam(_pack([w[n] for n in SMALL]), _pack([m[n] for n in SMALL]), _pack([v[n] for n in SMALL]), _pack(gs),
               name="adam_small")
    for n, parts in zip(SMALL, zip(*[_unpack(r, gs) for r in res])):
        out[n] = list(parts)

    outs = [loss, dx[None]]
    for kind in range(4):
        outs += [out[n][kind] for n in ORDER]
    return tuple(outs)
```

```python
import jax
import jax.numpy as jnp
from jax import lax
from jax.experimental import pallas as pl
from jax.experimental.pallas import tpu as pltpu

F32 = jnp.float32
MM = jnp.bfloat16
HI = lax.Precision.HIGHEST

D = 1024
NH = 8
HD = 128
CH = 64
DFF = 4 * D
DEPTH = 4
LN_EPS = 1e-5
RMS_EPS = 1e-6
L2_EPS = 1e-6
ALPHA = (2 * DEPTH) ** 0.25
LR, B1, B2, EPS, WD, STEP = 0.001, 0.9, 0.999, 1e-08, 0.01, 10

NMAIN = 9 * D
CQ, CK, CV, CSB, CSC, CSX, CGA, CGB, CZ = range(9)
HALO = 8
VMEM_LIMIT = 56 * 1024 * 1024
MESH = pl.DeviceIdType.MESH


def _cp(sem=None, vmem=VMEM_LIMIT):
    return pltpu.CompilerParams(dimension_semantics=sem, vmem_limit_bytes=vmem)


def _sds(shape, dtype=F32):
    return jax.ShapeDtypeStruct(tuple(shape), dtype)


def _accumulate(acc, product, k, nk, finish):
    if nk == 1:
        finish(product())
        return

    @pl.when(k == 0)
    def _():
        acc[...] = jnp.zeros_like(acc)

    acc[...] += product()

    @pl.when(k == nk - 1)
    def _():
        finish(acc[...])


def mm_nn(a, b, *, name, bias=None, relu2=False, add=None, out_dtype=F32, tm=1024, tn=1024, tk=1024):
    M, K = a.shape
    N = b.shape[1]
    tm, tn, tk = min(tm, M), min(tn, N), min(tk, K)
    nk = K // tk

    def body(*refs):
        it = iter(refs)
        a_ref, b_ref = next(it), next(it)
        bias_ref = next(it) if bias is not None else None
        add_ref = next(it) if add is not None else None
        o_ref = next(it)
        h_ref = next(it) if relu2 else None
        acc = next(it) if nk > 1 else None
        prod = lambda: jnp.dot(a_ref[...].astype(MM), b_ref[...].astype(MM), preferred_element_type=F32)

        def finish(r):
            if bias_ref is not None:
                r = r + bias_ref[...]
            if add_ref is not None:
                r = r + add_ref[...]
            if relu2:
                t = jnp.maximum(r, 0.0)
                o_ref[...] = (2.0 * t).astype(o_ref.dtype)
                h_ref[...] = (t * t).astype(h_ref.dtype)
            else:
                o_ref[...] = r.astype(o_ref.dtype)

        _accumulate(acc, prod, pl.program_id(2), nk, finish)

    in_specs = [pl.BlockSpec((tm, tk), lambda i, j, k: (i, k)), pl.BlockSpec((tk, tn), lambda i, j, k: (k, j))]
    args = [a, b]
    if bias is not None:
        in_specs.append(pl.BlockSpec((1, tn), lambda i, j, k: (0, j)))
        args.append(bias)
    if add is not None:
        in_specs.append(pl.BlockSpec((tm, tn), lambda i, j, k: (i, j)))
        args.append(add)
    out_shape = [_sds((M, N), out_dtype)]
    out_specs = [pl.BlockSpec((tm, tn), lambda i, j, k: (i, j))]
    if relu2:
        out_shape.append(_sds((M, N), MM))
        out_specs.append(pl.BlockSpec((tm, tn), lambda i, j, k: (i, j)))
    res = pl.pallas_call(
        body, name=name, grid=(M // tm, N // tn, nk), in_specs=in_specs, out_specs=out_specs, out_shape=out_shape,
        scratch_shapes=[pltpu.VMEM((tm, tn), F32)] if nk > 1 else [],
        compiler_params=_cp(("parallel", "parallel", "arbitrary")))(*args)
    return res if relu2 else res[0]


def mm_nt(a, b, *, name, add=None, dact=None, out_dtype=F32, tm=1024, tn=1024, tk=1024):
    M, Nc = a.shape
    Ko = b.shape[0]
    tm, tn, tk = min(tm, M), min(tn, Ko), min(tk, Nc)
    nk = Nc // tk
    ni = M // tm

    def body(*refs):
        it = iter(refs)
        a_ref, b_ref = next(it), next(it)
        add_ref = next(it) if add is not None else None
        d_ref = next(it) if dact is not None else None
        o_ref = next(it)
        db_ref = next(it) if dact is not None else None
        acc = next(it) if nk > 1 else None
        i = pl.program_id(1)
        prod = lambda: lax.dot_general(a_ref[...].astype(MM), b_ref[...].astype(MM), (((1,), (1,)), ((), ())),
                                       preferred_element_type=F32)

        def finish(r):
            if add_ref is not None:
                r = r + add_ref[...]
            if d_ref is not None:
                r = r * d_ref[...].astype(F32)
                s = jnp.sum(r, axis=0, keepdims=True)
                row0 = lax.broadcasted_iota(jnp.int32, db_ref.shape, 0) == 0

                @pl.when(i == 0)
                def _():
                    db_ref[...] = jnp.zeros_like(db_ref)

                db_ref[...] += jnp.where(row0, s, 0.0)
            o_ref[...] = r.astype(o_ref.dtype)

        _accumulate(acc, prod, pl.program_id(2), nk, finish)

    in_specs = [pl.BlockSpec((tm, tk), lambda j, i, k: (i, k)), pl.BlockSpec((tn, tk), lambda j, i, k: (j, k))]
    args = [a, b]
    for extra in (add, dact):
        if extra is not None:
            in_specs.append(pl.BlockSpec((tm, tn), lambda j, i, k: (i, j)))
            args.append(extra)
    out_shape = [_sds((M, Ko), out_dtype)]
    out_specs = [pl.BlockSpec((tm, tn), lambda j, i, k: (i, j))]
    if dact is not None:
        out_shape.append(_sds((8, Ko), F32))
        out_specs.append(pl.BlockSpec((8, tn), lambda j, i, k: (0, j)))
    res = pl.pallas_call(
        body, name=name, grid=(Ko // tn, ni, nk), in_specs=in_specs, out_specs=out_specs, out_shape=out_shape,
        scratch_shapes=[pltpu.VMEM((tm, tn), F32)] if nk > 1 else [],
        compiler_params=_cp(("parallel", "arbitrary", "arbitrary")))(*args)
    return res if dact is not None else res[0]


def mm_tn(a, b, *, name, tm=1024, tn=1024, tk=1024):
    T, M = a.shape
    N = b.shape[1]
    tm, tn, tk = min(tm, M), min(tn, N), min(tk, T)

    def body(a_ref, b_ref, o_ref):
        @pl.when(pl.program_id(2) == 0)
        def _():
            o_ref[...] = jnp.zeros_like(o_ref)

        o_ref[...] += lax.dot_general(a_ref[...].astype(MM), b_ref[...].astype(MM), (((0,), (0,)), ((), ())),
                                      preferred_element_type=F32)

    return pl.pallas_call(
        body, name=name, grid=(M // tm, N // tn, T // tk),
        in_specs=[pl.BlockSpec((tk, tm), lambda i, j, k: (k, i)), pl.BlockSpec((tk, tn), lambda i, j, k: (k, j))],
        out_specs=pl.BlockSpec((tm, tn), lambda i, j, k: (i, j)), out_shape=_sds((M, N)),
        compiler_params=_cp(("parallel", "parallel", "arbitrary")))(a, b)


def _sigmoid(x):
    return 1.0 / (1.0 + jnp.exp(-x))


def _silu(x):
    return x * _sigmoid(x)


def _softplus(x):
    return jnp.maximum(x, 0.0) + jnp.log1p(jnp.exp(-jnp.abs(x)))


def _ext(main_ref, prev_ref, next_ref, first, last):
    p = jnp.where(first, 0.0, prev_ref[...].astype(F32))
    n = jnp.where(last, 0.0, next_ref[...].astype(F32))
    return jnp.concatenate([p, main_ref[...].astype(F32), n], axis=0)


def _shift_dn(x):
    return pltpu.roll(x, 1, 0)


def _shift_up(x):
    return pltpu.roll(x, x.shape[0] - 1, 0)


def _conv3(xe, w):
    return w[0:1, :] * _shift_dn(xe) + w[1:2, :] * xe + w[2:3, :] * _shift_up(xe)


def _conv3_t(de, w):
    return w[0:1, :] * _shift_up(de) + w[1:2, :] * de + w[2:3, :] * _shift_dn(de)


def _halo_specs(bt, T, col, lead=None):
    r = bt // HALO
    last = T // HALO - 1
    if lead is None:
        return [pl.BlockSpec((bt, D), lambda i: (i, col)),
                pl.BlockSpec((HALO, D), lambda i: (jnp.maximum(i * r - 1, 0), col)),
                pl.BlockSpec((HALO, D), lambda i: (jnp.minimum((i + 1) * r, last), col))]
    return [pl.BlockSpec((lead, bt, D), lambda i: (0, i, col)),
            pl.BlockSpec((lead, HALO, D), lambda i: (0, jnp.maximum(i * r - 1, 0), col)),
            pl.BlockSpec((lead, HALO, D), lambda i: (0, jnp.minimum((i + 1) * r, last), col))]


def _qkv_rows(cq, ck, cv):
    sq, sk, sv = _silu(cq), _silu(ck), _silu(cv)
    qs, ks = [], []
    for h in range(NH):
        s = slice(h * HD, (h + 1) * HD)
        qh, kh = sq[:, s], sk[:, s]
        qs.append(qh * lax.rsqrt(jnp.sum(qh * qh, axis=-1, keepdims=True) + L2_EPS) * (HD ** -0.5))
        ks.append(kh * lax.rsqrt(jnp.sum(kh * kh, axis=-1, keepdims=True) + L2_EPS))
    return jnp.concatenate(qs, axis=1), jnp.concatenate(ks, axis=1), sv


def _chunk_masks(bt):
    row = lax.broadcasted_iota(jnp.int32, (bt, bt), 0)
    col = lax.broadcasted_iota(jnp.int32, (bt, bt), 1)
    same = (row // CH) == (col // CH)
    lower = jnp.where(same & (col <= row), 1.0, 0.0).astype(F32)
    upper = jnp.where(same & (col >= row), 1.0, 0.0).astype(F32)
    return lower, upper


def _gate_rows(ab, gp, lower, upper):
    lane = lax.broadcasted_iota(jnp.int32, ab.shape, 1)
    g = -jnp.exp(gp[0:1, :]) * _softplus(ab + gp[1:2, :])
    g = jnp.where(lane < 2 * NH, g, 0.0)
    gf = jnp.dot(lower, g, precision=HI, preferred_element_type=F32)
    gr = jnp.dot(upper, g, precision=HI, preferred_element_type=F32)
    gc = jnp.where(lane < NH, gf, gr)
    beta = _sigmoid(ab)
    return jnp.where(lane < 2 * NH, gc, jnp.where(lane < 4 * NH, beta, 0.0))


def pre_qkv_fwd(pm, pab, cw, gp, *, bt=512):
    T = pm.shape[0]
    bt = min(bt, T)
    n = T // bt

    def body(q0, q1, q2, k0, k1, k2, v0, v1, v2, ab_ref, cw_ref, gp_ref, q_ref, k_ref, v_ref, gb_ref):
        i = pl.program_id(0)
        first, last = i == 0, i == n - 1
        cs = []
        for c, (m, p, x) in enumerate(((q0, q1, q2), (k0, k1, k2), (v0, v1, v2))):
            xe = _ext(m, p, x, first, last)
            cs.append(_conv3(xe, cw_ref[:, c * D:(c + 1) * D])[HALO:HALO + bt])
        q, k, v = _qkv_rows(*cs)
        q_ref[...], k_ref[...], v_ref[...] = q, k, v
        lower, upper = _chunk_masks(bt)
        gb_ref[...] = _gate_rows(ab_ref[...], gp_ref[...], lower, upper)

    in_specs = (_halo_specs(bt, T, CQ) + _halo_specs(bt, T, CK) + _halo_specs(bt, T, CV)
                + [pl.BlockSpec((bt, 128), lambda i: (i, 0)), pl.BlockSpec((8, 3 * D), lambda i: (0, 0)),
                   pl.BlockSpec((8, 128), lambda i: (0, 0))])
    row = pl.BlockSpec((bt, D), lambda i: (i, 0))
    return pl.pallas_call(
        body, name="pre_qkv_fwd", grid=(n,), in_specs=in_specs,
        out_specs=[row, row, row, pl.BlockSpec((bt, 128), lambda i: (i, 0))],
        out_shape=[_sds((T, D)), _sds((T, D)), _sds((T, D)), _sds((T, 128))],
        compiler_params=_cp(("parallel",)))(*([pm] * 9), pab, cw, gp)


def pre_qkv_bwd(pm, pab, cw, gp, dq2, dk2, dv2, dgb2, dpm, *, bt=256):
    T = pm.shape[0]
    bt = min(bt, T)
    n = T // bt
    E = bt + 2 * HALO

    def body(*refs):
        it = iter(refs)
        xs = [[next(it) for _ in range(3)] for _ in range(3)]
        ds = [[next(it) for _ in range(3)] for _ in range(3)]
        ab_ref, dgb_ref, cw_ref, gp_ref, _alias = next(it), next(it), next(it), next(it), next(it)
        o_ref, dab_ref, dcw_ref, dgp_ref = (next(it) for _ in range(4))
        i = pl.program_id(0)
        first, last = i == 0, i == n - 1

        @pl.when(first)
        def _():
            dcw_ref[...] = jnp.zeros_like(dcw_ref)
            dgp_ref[...] = jnp.zeros_like(dgp_ref)

        xes = [_ext(*xs[c], first, last) for c in range(3)]
        ces = [_conv3(xes[c], cw_ref[:, c * D:(c + 1) * D]) for c in range(3)]
        cts = []
        for c in range(3):
            m, p, x = ds[c]
            pe = jnp.where(first, 0.0, p[0] + p[1])
            ne = jnp.where(last, 0.0, x[0] + x[1])
            cts.append(jnp.concatenate([pe, m[0] + m[1], ne], axis=0))
        _, vjp = jax.vjp(_qkv_rows, *ces)
        dces = vjp(tuple(cts))
        rowi = lax.broadcasted_iota(jnp.int32, (E, 1), 0)
        central = (rowi >= HALO) & (rowi < HALO + bt)
        row8 = lax.broadcasted_iota(jnp.int32, (8, D), 0)
        for c in range(3):
            w = cw_ref[:, c * D:(c + 1) * D]
            o_ref[:, c * D:(c + 1) * D] = _conv3_t(dces[c], w)[HALO:HALO + bt].astype(o_ref.dtype)
            dc = jnp.where(central, dces[c], 0.0)
            taps = (jnp.sum(dc * _shift_dn(xes[c]), axis=0, keepdims=True),
                    jnp.sum(dc * xes[c], axis=0, keepdims=True),
                    jnp.sum(dc * _shift_up(xes[c]), axis=0, keepdims=True))
            upd = jnp.where(row8 == 0, taps[0], jnp.where(row8 == 1, taps[1], jnp.where(row8 == 2, taps[2], 0.0)))
            dcw_ref[:, c * D:(c + 1) * D] += upd
        lower, upper = _chunk_masks(bt)
        _, gvjp = jax.vjp(lambda ab, gp: _gate_rows(ab, gp, lower, upper), ab_ref[...], gp_ref[...])
        dab, dgp = gvjp(dgb_ref[0] + dgb_ref[1])
        dab_ref[...] = dab
        dgp_ref[...] += dgp

    in_specs = (_halo_specs(bt, T, CQ) + _halo_specs(bt, T, CK) + _halo_specs(bt, T, CV)
                + _halo_specs(bt, T, 0, lead=2) * 3
                + [pl.BlockSpec((bt, 128), lambda i: (i, 0)), pl.BlockSpec((2, bt, 128), lambda i: (0, i, 0)),
                   pl.BlockSpec((8, 3 * D), lambda i: (0, 0)), pl.BlockSpec((8, 128), lambda i: (0, 0)),
                   pl.BlockSpec(memory_space=pl.ANY)])
    out_specs = [pl.BlockSpec((bt, 3 * D), lambda i: (i, 0)), pl.BlockSpec((bt, 128), lambda i: (i, 0)),
                 pl.BlockSpec((8, 3 * D), lambda i: (0, 0)), pl.BlockSpec((8, 128), lambda i: (0, 0))]
    return pl.pallas_call(
        body, name="pre_qkv_bwd", grid=(n,), in_specs=in_specs, out_specs=out_specs,
        out_shape=[_sds(dpm.shape, dpm.dtype), _sds((T, 128)), _sds((8, 3 * D)), _sds((8, 128))],
        input_output_aliases={len(in_specs) - 1: 0},
        compiler_params=_cp(("arbitrary",)))(
            *([pm] * 9), dq2, dq2, dq2, dk2, dk2, dk2, dv2, dv2, dv2, pab, dgb2, cw, gp, dpm)


def pre_sc_fwd(pm, cw, *, bt=512):
    T = pm.shape[0]
    bt = min(bt, T)
    n = T // bt

    def body(b_ref, c0, c1, c2, x0, x1, x2, cw_ref, o_ref):
        i = pl.program_id(0)
        first, last = i == 0, i == n - 1
        pe = _ext(c0, c1, c2, first, last) * _ext(x0, x1, x2, first, last)
        o_ref[...] = (b_ref[...] * _conv3(pe, cw_ref[...])[HALO:HALO + bt]).astype(o_ref.dtype)

    in_specs = ([pl.BlockSpec((bt, D), lambda i: (i, CSB))] + _halo_specs(bt, T, CSC) + _halo_specs(bt, T, CSX)
                + [pl.BlockSpec((8, D), lambda i: (0, 0))])
    return pl.pallas_call(
        body, name="pre_sc_fwd", grid=(n,), in_specs=in_specs, out_specs=pl.BlockSpec((bt, D), lambda i: (i, 0)),
        out_shape=_sds((T, D), MM), compiler_params=_cp(("parallel",)))(*([pm] * 7), cw)


def pre_sc_bwd(pm, cw, dsc, dpm, *, bt=512):
    T = pm.shape[0]
    bt = min(bt, T)
    n = T // bt
    E = bt + 2 * HALO

    def body(b0, b1, b2, c0, c1, c2, x0, x1, x2, d0, d1, d2, cw_ref, _alias, o_ref, dcw_ref):
        i = pl.program_id(0)
        first, last = i == 0, i == n - 1

        @pl.when(first)
        def _():
            dcw_ref[...] = jnp.zeros_like(dcw_ref)

        ce, xe = _ext(c0, c1, c2, first, last), _ext(x0, x1, x2, first, last)
        pe = ce * xe
        w = cw_ref[...]
        dout = d0[...]
        o_ref[:, 0:D] = (dout * _conv3(pe, w)[HALO:HALO + bt]).astype(o_ref.dtype)
        dce = _ext(d0, d1, d2, first, last) * _ext(b0, b1, b2, first, last)
        dp = _conv3_t(dce, w)[HALO:HALO + bt]
        o_ref[:, D:2 * D] = (dp * x0[...]).astype(o_ref.dtype)
        o_ref[:, 2 * D:3 * D] = (dp * c0[...]).astype(o_ref.dtype)
        rowi = lax.broadcasted_iota(jnp.int32, (E, 1), 0)
        dc = jnp.where((rowi >= HALO) & (rowi < HALO + bt), dce, 0.0)
        row8 = lax.broadcasted_iota(jnp.int32, (8, D), 0)
        taps = (jnp.sum(dc * _shift_dn(pe), axis=0, keepdims=True), jnp.sum(dc * pe, axis=0, keepdims=True),
                jnp.sum(dc * _shift_up(pe), axis=0, keepdims=True))
        dcw_ref[...] += jnp.where(row8 == 0, taps[0], jnp.where(row8 == 1, taps[1], jnp.where(row8 == 2, taps[2], 0.0)))

    dsc_specs = [pl.BlockSpec((bt, D), lambda i: (i, 0)),
                 pl.BlockSpec((HALO, D), lambda i: (jnp.maximum(i * (bt // HALO) - 1, 0), 0)),
                 pl.BlockSpec((HALO, D), lambda i: (jnp.minimum((i + 1) * (bt // HALO), T // HALO - 1), 0))]
    in_specs = (_halo_specs(bt, T, CSB) + _halo_specs(bt, T, CSC) + _halo_specs(bt, T, CSX) + dsc_specs
                + [pl.BlockSpec((8, D), lambda i: (0, 0)), pl.BlockSpec(memory_space=pl.ANY)])
    return pl.pallas_call(
        body, name="pre_sc_bwd", grid=(n,), in_specs=in_specs,
        out_specs=[pl.BlockSpec((bt, 3 * D), lambda i: (i, 1)), pl.BlockSpec((8, D), lambda i: (0, 0))],
        out_shape=[_sds(dpm.shape, dpm.dtype), _sds((8, D))], input_output_aliases={len(in_specs) - 1: 0},
        compiler_params=_cp(("arbitrary",)))(*([pm] * 9), dsc, dsc, dsc, cw, dpm)


def _bdot(a, b, dims):
    return lax.dot_general(a.astype(MM), b.astype(MM), (dims, ((), ())), preferred_element_type=F32)


_NN, _NT, _TN = ((1,), (0,)), ((1,), (1,)), ((0,), (0,))


def _raw_nn(a, b):
    return _bdot(a, b, _NN)


def _raw_nt(a, b):
    return _bdot(a, b, _NT)


def _raw_tn(a, b):
    return _bdot(a, b, _TN)


def _make_vjp_ops():
    @jax.custom_vjp
    def nn(a, b):
        return _raw_nn(a, b)

    @jax.custom_vjp
    def nt(a, b):
        return _raw_nt(a, b)

    @jax.custom_vjp
    def tn(a, b):
        return _raw_tn(a, b)

    nn.defvjp(lambda a, b: (_raw_nn(a, b), (a, b)), lambda r, g: (_raw_nt(g, r[1]), _raw_tn(r[0], g)))
    nt.defvjp(lambda a, b: (_raw_nt(a, b), (a, b)), lambda r, g: (_raw_nn(g, r[1]), _raw_tn(g, r[0])))
    tn.defvjp(lambda a, b: (_raw_tn(a, b), (a, b)), lambda r, g: (_raw_nt(r[1], g), _raw_nn(r[0], g)))

    @jax.custom_vjp
    def inv_saved(A, Y):
        return Y

    def inv_bwd(Y, g):
        M = g + _raw_tn(Y, g)
        return -(M + _raw_nt(M, Y)), jnp.zeros_like(Y)

    inv_saved.defvjp(lambda A, Y: (Y, Y), inv_bwd)
    return nn, nt, tn, inv_saved


GH = 2
GR = GH * CH
NG = NH // GH
CBI = 4
CBB = 2


def _scan_chunks(n_chunks):
    return 8 if n_chunks % 8 == 0 else 2


def _tri_inv_y_all(As):
    Ys = [-A for A in As]
    Ps = [_raw_nn(A, A) for A in As]
    for stage in range(5):
        squares = [_raw_nn(P, P) for P in Ps] if stage < 4 else None
        Ys = [Y + P + _raw_nn(Y, P) for Y, P in zip(Ys, Ps)]
        Ps = squares
    return Ys


def _intra_groups(chains, incl, strict, eye, lastc, ops, inv_all):
    nn, nt, tn = ops
    st = []
    for qs, ks, vs, gcols, bcols in chains:
        q, k, v = (jnp.concatenate(t, axis=0) for t in (qs, ks, vs))
        gcol, bcol = jnp.concatenate(gcols, axis=0), jnp.concatenate(bcols, axis=0)
        grow = jnp.sum(eye * gcol, axis=0, keepdims=True)
        gam = jnp.where(incl, jnp.exp(jnp.where(incl, gcol - grow, 0.0)), 0.0)
        glast = jnp.sum(jnp.where(lastc, grow, 0.0), axis=1, keepdims=True)
        st.append((q, k, v, gcol, bcol, gam, glast, jnp.exp(gcol), k * bcol))
    As = [jnp.where(strict, nt(kb, k) * gam, 0.0) for (q, k, v, gcol, bcol, gam, glast, eg, kb) in st]
    Ys = inv_all(As)
    vbs = [v * bcol for (q, k, v, gcol, bcol, gam, glast, eg, kb) in st]
    kgs = [kb * eg for (q, k, v, gcol, bcol, gam, glast, eg, kb) in st]
    us = [vb + nn(Y, vb) for Y, vb in zip(Ys, vbs)]
    ws = [kg + nn(Y, kg) for Y, kg in zip(Ys, kgs)]
    Ps = [nt(q, k) * gam for (q, k, v, gcol, bcol, gam, glast, eg, kb) in st]
    return [((u, w, P, q * eg, k * jnp.exp(glast - gcol), jnp.exp(glast)), Y)
            for u, w, P, Y, (q, k, v, gcol, bcol, gam, glast, eg, kb) in zip(us, ws, Ps, Ys, st)]


def _scan_groups(chains, ops):
    nn, nt, tn, _ = ops
    vns = [[us[j] - nn(ws[j], Ss[j]) for j in range(GH)] for Ss, us, ws, P, qds, kds, egls in chains]
    os_ = [jnp.concatenate([nn(qds[j], Ss[j]) for j in range(GH)], axis=0) + nn(P, jnp.concatenate(vn, axis=0))
           for (Ss, us, ws, P, qds, kds, egls), vn in zip(chains, vns)]
    S2s = [[Ss[j] * egls[j] + tn(kds[j], vn[j]) for j in range(GH)]
           for (Ss, us, ws, P, qds, kds, egls), vn in zip(chains, vns)]
    return list(zip(os_, S2s))


def _group_masks(rev, rows=GR):
    r = lax.broadcasted_iota(jnp.int32, (rows, rows), 0)
    c = lax.broadcasted_iota(jnp.int32, (rows, rows), 1)
    same = (r // CH) == (c // CH)
    ahead = jnp.where(rev, c - r, r - c)
    incl = same & (ahead >= 0)
    strict = same & (ahead > 0)
    eye = jnp.where(r == c, 1.0, 0.0).astype(F32)
    lastc = same & ((c % CH) == jnp.where(rev, 0, CH - 1))
    return incl, strict, eye, lastc


def _head_gates(gb, h, rev):
    gcol = jnp.where(rev, gb[:, NH + h:NH + h + 1], gb[:, h:h + 1])
    bcol = jnp.where(rev, gb[:, 3 * NH + h:3 * NH + h + 1], gb[:, 2 * NH + h:2 * NH + h + 1])
    return gcol, bcol


def _hs(h):
    return slice(h * HD, (h + 1) * HD)


def _blockdiag(a, b):
    z = jnp.zeros_like(a)
    return jnp.concatenate([jnp.concatenate([a, z], axis=1), jnp.concatenate([z, b], axis=1)], axis=0)


def _load_chain(q_ref, k_ref, v_ref, gb_ref, c, g, rev, gh=GH):
    r = slice(c * CH, (c + 1) * CH)
    heads = range(g * gh, (g + 1) * gh)
    gates = [_head_gates(gb_ref[r, :], h, rev) for h in heads]
    return ([q_ref[r, _hs(h)] for h in heads], [k_ref[r, _hs(h)] for h in heads], [v_ref[r, _hs(h)] for h in heads],
            [t[0] for t in gates], [t[1] for t in gates])


def gdn_intra_fwd(q, k, v, gb):
    T = q.shape[0]
    N = T // CH
    ops = (_raw_nn, _raw_nt, _raw_tn)

    def body(q_ref, k_ref, v_ref, gb_ref, u_ref, w_ref, qd_ref, kd_ref, pp_ref, ys_ref, eg_ref):
        rev = pl.program_id(0) == 1
        masks = _group_masks(rev)
        where = [(c, g) for c in range(CBI) for g in range(NG)]
        chains = [_load_chain(q_ref, k_ref, v_ref, gb_ref, c, g, rev) for c, g in where]
        for (c, g), ((u, w, P, qd, kd, egl), Y) in zip(where, _intra_groups(chains, *masks, ops, _tri_inv_y_all)):
            r = slice(c * CH, (c + 1) * CH)
            pp_ref[c, g] = P.astype(MM)
            ys_ref[c, g] = Y.astype(MM)
            for j, h in enumerate(range(g * GH, (g + 1) * GH)):
                rows = slice(j * CH, (j + 1) * CH)
                u_ref[r, _hs(h)] = u[rows]
                w_ref[r, _hs(h)] = w[rows].astype(MM)
                qd_ref[r, _hs(h)] = qd[rows].astype(MM)
                kd_ref[r, _hs(h)] = kd[rows].astype(MM)
                eg_ref[c, h:h + 1, :] = jnp.broadcast_to(egl[j * CH:j * CH + 1, :], (1, 128))

    row = pl.BlockSpec((CBI * CH, D), lambda d, n: (n, 0))
    drow = pl.BlockSpec((None, CBI * CH, D), lambda d, n: (d, n, 0))
    mat = pl.BlockSpec((None, CBI, NG, GR, GR), lambda d, n: (d, n, 0, 0, 0))
    return pl.pallas_call(
        body, name="gdn_intra_fwd", grid=(2, N // CBI),
        in_specs=[row, row, row, pl.BlockSpec((CBI * CH, 128), lambda d, n: (n, 0))],
        out_specs=[drow] * 4 + [mat, mat, pl.BlockSpec((None, CBI, NH, 128), lambda d, n: (d, n, 0, 0))],
        out_shape=[_sds((2, T, D))] + [_sds((2, T, D), MM)] * 3 + [_sds((2, N, NG, GR, GR), MM)] * 2
                  + [_sds((2, N, NH, 128))],
        compiler_params=_cp(("parallel", "parallel")))(q, k, v, gb)


def gdn_scan_fwd(u, w, qd, kd, pp, eg):
    T = u.shape[1]
    N = T // CH
    ops = (_raw_nn, _raw_nt, _raw_tn, None)

    cbs = _scan_chunks(N)
    NB = N // cbs

    def body(u_ref, w_ref, qd_ref, kd_ref, pp_ref, eg_ref, o_ref, s0_ref, S):
        d = pl.program_id(0)

        @pl.when(pl.program_id(1) == 0)
        def _():
            S[...] = jnp.zeros_like(S)

        def chunk(c, carry):
            pc = c + d * (cbs - 1 - 2 * c)
            r = pl.ds(pl.multiple_of(pc * CH, CH), CH)
            chains = []
            for g in range(NG):
                heads = range(g * GH, (g + 1) * GH)
                Ss = [S[h] for h in heads]
                for h, Sh in zip(heads, Ss):
                    s0_ref[pc, h] = Sh
                chains.append((Ss, [u_ref[r, _hs(h)] for h in heads], [w_ref[r, _hs(h)] for h in heads], pp_ref[pc, g],
                               [qd_ref[r, _hs(h)] for h in heads], [kd_ref[r, _hs(h)] for h in heads],
                               [eg_ref[pc, h:h + 1, :] for h in heads]))
            for g, (o, S2) in enumerate(_scan_groups(chains, ops)):
                for j, h in enumerate(range(g * GH, (g + 1) * GH)):
                    o_ref[r, _hs(h)] = o[j * CH:(j + 1) * CH]
                    S[h] = S2[j]
            return carry

        lax.fori_loop(0, cbs, chunk, 0)

    bidx = lambda d, n: n + d * (NB - 1 - 2 * n)
    drow = pl.BlockSpec((None, cbs * CH, D), lambda d, n: (d, bidx(d, n), 0))
    mat = pl.BlockSpec((None, cbs, NG, GR, GR), lambda d, n: (d, bidx(d, n), 0, 0, 0))
    return pl.pallas_call(
        body, name="gdn_scan_fwd", grid=(2, NB),
        in_specs=[drow] * 4 + [mat, pl.BlockSpec((None, cbs, NH, 128), lambda d, n: (d, bidx(d, n), 0, 0))],
        out_specs=[drow, pl.BlockSpec((None, cbs, NH, HD, HD), lambda d, n: (d, bidx(d, n), 0, 0, 0))],
        out_shape=[_sds((2, T, D)), _sds((2, N, NH, HD, HD))],
        scratch_shapes=[pltpu.VMEM((NH, HD, HD), F32)],
        compiler_params=_cp(("arbitrary", "arbitrary")))(u, w, qd, kd, pp, eg)


def gdn_scan_bwd(u, w, qd, kd, pp, eg, s0, do):
    T = u.shape[1]
    N = T // CH
    ops = _make_vjp_ops()
    cbs = _scan_chunks(N)
    NB = N // cbs

    def body(u_ref, w_ref, qd_ref, kd_ref, pp_ref, eg_ref, s0_ref, do_ref,
             du_ref, dw_ref, dqd_ref, dkd_ref, dpp_ref, deg_ref, dS):
        d = pl.program_id(0)

        @pl.when(pl.program_id(1) == 0)
        def _():
            dS[...] = jnp.zeros_like(dS)

        def chunk(c, carry):
            pc = (cbs - 1 - c) + d * (2 * c - (cbs - 1))
            r = pl.ds(pl.multiple_of(pc * CH, CH), CH)
            chains, cts = [], []
            for g in range(NG):
                heads = range(g * GH, (g + 1) * GH)
                chains.append(([s0_ref[pc, h] for h in heads], [u_ref[r, _hs(h)] for h in heads],
                               [w_ref[r, _hs(h)].astype(F32) for h in heads], pp_ref[pc, g].astype(F32),
                               [qd_ref[r, _hs(h)].astype(F32) for h in heads],
                               [kd_ref[r, _hs(h)].astype(F32) for h in heads], [eg_ref[pc, h:h + 1, :] for h in heads]))
                cts.append((jnp.concatenate([do_ref[r, _hs(h)].astype(F32) for h in heads], axis=0),
                            [dS[h] for h in heads]))
            _, vjp = jax.vjp(lambda ch: _scan_groups(ch, ops), chains)
            (dchains,) = vjp(cts)
            for g, (dSs, dus, dws, dP, dqds, dkds, degs) in enumerate(dchains):
                dpp_ref[pc, g] = dP
                for j, h in enumerate(range(g * GH, (g + 1) * GH)):
                    dS[h] = dSs[j]
                    du_ref[r, _hs(h)] = dus[j].astype(MM)
                    dw_ref[r, _hs(h)] = dws[j].astype(MM)
                    dqd_ref[r, _hs(h)], dkd_ref[r, _hs(h)] = dqds[j], dkds[j]
                    deg_ref[pc, h:h + 1, :] = degs[j]
            return carry

        lax.fori_loop(0, cbs, chunk, 0)

    bidx = lambda d, n: (NB - 1 - n) + d * (2 * n - (NB - 1))
    drow = pl.BlockSpec((None, cbs * CH, D), lambda d, n: (d, bidx(d, n), 0))
    erow = pl.BlockSpec((None, cbs, NH, 128), lambda d, n: (d, bidx(d, n), 0, 0))
    mat = pl.BlockSpec((None, cbs, NG, GR, GR), lambda d, n: (d, bidx(d, n), 0, 0, 0))
    return pl.pallas_call(
        body, name="gdn_scan_bwd", grid=(2, NB),
        in_specs=[drow] * 4 + [mat, erow, pl.BlockSpec((None, cbs, NH, HD, HD), lambda d, n: (d, bidx(d, n), 0, 0, 0)),
                               pl.BlockSpec((cbs * CH, D), lambda d, n: (bidx(d, n), 0))],
        out_specs=[drow] * 4 + [mat, erow],
        out_shape=[_sds((2, T, D), MM)] * 2 + [_sds((2, T, D))] * 2 + [_sds((2, N, NG, GR, GR))]
                  + [_sds((2, N, NH, 128))],
        scratch_shapes=[pltpu.VMEM((NH, HD, HD), F32)],
        compiler_params=_cp(("arbitrary", "arbitrary")))(u, w, qd, kd, pp, eg, s0, do)


def gdn_intra_bwd(q, k, v, gb, ys, du, dw, dqd, dkd, dpp, deg):
    T = q.shape[0]
    N = T // CH
    nn, nt, tn, inv_saved = _make_vjp_ops()

    def body(q_ref, k_ref, v_ref, gb_ref, ys_ref, du_ref, dw_ref, dqd_ref, dkd_ref, dpp_ref, deg_ref,
             dq_ref, dk_ref, dv_ref, dgb_ref):
        rev = pl.program_id(0) == 1
        gh, rows = 2 * GH, 2 * GR
        masks = _group_masks(rev, rows)
        lane = lax.broadcasted_iota(jnp.int32, (CH, 128), 1)
        grow = lax.broadcasted_iota(jnp.int32, (rows, 1), 0)
        where = [(c, g) for c in range(CBB) for g in range(NG // 2)]
        pair = lambda ref, c, g: _blockdiag(ref[c, 2 * g].astype(F32), ref[c, 2 * g + 1].astype(F32))
        chains = [_load_chain(q_ref, k_ref, v_ref, gb_ref, c, g, rev, gh) for c, g in where]
        Ys = [pair(ys_ref, c, g) for c, g in where]
        inv_all = lambda As: [inv_saved(A, Y) for A, Y in zip(As, Ys)]
        _, vjp = jax.vjp(lambda ch: _intra_groups(ch, *masks, (nn, nt, tn), inv_all), chains)
        cts = []
        for c, g in where:
            r = slice(c * CH, (c + 1) * CH)
            heads = range(g * gh, (g + 1) * gh)
            stack = lambda ref: jnp.concatenate([ref[r, _hs(h)].astype(F32) for h in heads], axis=0)
            degl = jnp.zeros((rows, 1), F32)
            for j, h in enumerate(heads):
                degl = degl + jnp.where(grow == j * CH, jnp.sum(deg_ref[c, h:h + 1, :], axis=1, keepdims=True), 0.0)
            cts.append(((stack(du_ref), stack(dw_ref), pair(dpp_ref, c, g), stack(dqd_ref), stack(dkd_ref), degl),
                        jnp.zeros((rows, rows), F32)))
        (dchains,) = vjp(cts)
        dgbs = [jnp.zeros((CH, 128), F32) for _ in range(CBB)]
        for (c, g), (dqs, dks, dvs, dgs, dbs) in zip(where, dchains):
            r = slice(c * CH, (c + 1) * CH)
            for j, h in enumerate(range(g * gh, (g + 1) * gh)):
                dq_ref[r, _hs(h)], dk_ref[r, _hs(h)], dv_ref[r, _hs(h)] = dqs[j], dks[j], dvs[j]
                glane = jnp.where(rev, NH + h, h)
                dgbs[c] = dgbs[c] + jnp.where(lane == glane, dgs[j], 0.0) + jnp.where(lane == glane + 2 * NH, dbs[j], 0.0)
        for c in range(CBB):
            dgb_ref[c * CH:(c + 1) * CH, :] = dgbs[c]

    row = pl.BlockSpec((CBB * CH, D), lambda d, n: (n, 0))
    drow = pl.BlockSpec((None, CBB * CH, D), lambda d, n: (d, n, 0))
    mat = pl.BlockSpec((None, CBB, NG, GR, GR), lambda d, n: (d, n, 0, 0, 0))
    return pl.pallas_call(
        body, name="gdn_intra_bwd", grid=(2, N // CBB),
        in_specs=[row, row, row, pl.BlockSpec((CBB * CH, 128), lambda d, n: (n, 0)), mat, drow, drow, drow, drow, mat]
                 + [pl.BlockSpec((None, CBB, NH, 128), lambda d, n: (d, n, 0, 0))],
        out_specs=[drow, drow, drow, pl.BlockSpec((None, CBB * CH, 128), lambda d, n: (d, n, 0))],
        out_shape=[_sds((2, T, D))] * 3 + [_sds((2, T, 128))],
        compiler_params=_cp(("parallel", "parallel")))(q, k, v, gb, ys, du, dw, dqd, dkd, dpp, deg)


def _post_rows(o2a, o2b, z, nw):
    o = o2a + o2b
    outs = []
    for h in range(NH):
        s = slice(h * HD, (h + 1) * HD)
        oh = o[:, s]
        outs.append(oh * lax.rsqrt(jnp.mean(oh * oh, axis=-1, keepdims=True) + RMS_EPS) * nw * _silu(z[:, s]))
    return jnp.concatenate(outs, axis=1)


def post_fwd(o2, pm, nw, *, bt=1024):
    T = pm.shape[0]
    bt = min(bt, T)

    def body(o_ref, z_ref, nw_ref, og_ref):
        og_ref[...] = _post_rows(o_ref[0], o_ref[1], z_ref[...], nw_ref[0:1, :]).astype(og_ref.dtype)

    return pl.pallas_call(
        body, name="post_fwd", grid=(T // bt,),
        in_specs=[pl.BlockSpec((2, bt, D), lambda i: (0, i, 0)), pl.BlockSpec((bt, D), lambda i: (i, CZ)),
                  pl.BlockSpec((8, 128), lambda i: (0, 0))],
        out_specs=pl.BlockSpec((bt, D), lambda i: (i, 0)), out_shape=_sds((T, D), MM),
        compiler_params=_cp(("parallel",)))(o2, pm, nw)


def post_bwd(o2, pm, nw, dog, dpm, *, bt=512):
    T = pm.shape[0]
    bt = min(bt, T)

    def body(o_ref, z_ref, nw_ref, dog_ref, _alias, do_ref, dz_ref, dnw_ref):
        @pl.when(pl.program_id(0) == 0)
        def _():
            dnw_ref[...] = jnp.zeros_like(dnw_ref)

        _, vjp = jax.vjp(_post_rows, o_ref[0], o_ref[1], z_ref[...], nw_ref[0:1, :])
        doa, _unused, dz, dnw = vjp(dog_ref[...])
        do_ref[...] = doa.astype(do_ref.dtype)
        dz_ref[...] = dz.astype(dz_ref.dtype)
        row8 = lax.broadcasted_iota(jnp.int32, (8, 128), 0)
        dnw_ref[...] += jnp.where(row8 == 0, dnw, 0.0)

    in_specs = [pl.BlockSpec((2, bt, D), lambda i: (0, i, 0)), pl.BlockSpec((bt, D), lambda i: (i, CZ)),
                pl.BlockSpec((8, 128), lambda i: (0, 0)), pl.BlockSpec((bt, D), lambda i: (i, 0)),
                pl.BlockSpec(memory_space=pl.ANY)]
    return pl.pallas_call(
        body, name="post_bwd", grid=(T // bt,), in_specs=in_specs,
        out_specs=[pl.BlockSpec((bt, D), lambda i: (i, 0)), pl.BlockSpec((bt, D), lambda i: (i, CZ)),
                   pl.BlockSpec((8, 128), lambda i: (0, 0))],
        out_shape=[_sds((T, D), MM), _sds(dpm.shape, dpm.dtype), _sds((8, 128))], input_output_aliases={4: 1},
        compiler_params=_cp(("arbitrary",)))(o2, pm, nw, dog, dpm)


def merge_fwd(ya, yb, pm, *, bt=1024):
    T = pm.shape[0]
    bt = min(bt, T)

    def body(ya_ref, yb_ref, ga_ref, gb_ref, o_ref):
        o_ref[...] = (_sigmoid(ga_ref[...]) * ya_ref[...] + _sigmoid(gb_ref[...]) * yb_ref[...]).astype(o_ref.dtype)

    row = pl.BlockSpec((bt, D), lambda i: (i, 0))
    return pl.pallas_call(
        body, name="merge_fwd", grid=(T // bt,),
        in_specs=[row, row, pl.BlockSpec((bt, D), lambda i: (i, CGA)), pl.BlockSpec((bt, D), lambda i: (i, CGB))],
        out_specs=row, out_shape=_sds((T, D), MM), compiler_params=_cp(("parallel",)))(ya, yb, pm, pm)


def merge_bwd(ya, yb, pm, dmix, *, bt=512):
    T = pm.shape[0]
    bt = min(bt, T)

    def body(ya_ref, yb_ref, ga_ref, gb_ref, dm_ref, dya_ref, dyb_ref, dg_ref):
        dm = dm_ref[...]
        sa, sb = _sigmoid(ga_ref[...]), _sigmoid(gb_ref[...])
        dya_ref[...] = (dm * sa).astype(dya_ref.dtype)
        dyb_ref[...] = (dm * sb).astype(dyb_ref.dtype)
        dg_ref[:, 0:D] = (dm * ya_ref[...] * sa * (1.0 - sa)).astype(dg_ref.dtype)
        dg_ref[:, D:2 * D] = (dm * yb_ref[...] * sb * (1.0 - sb)).astype(dg_ref.dtype)

    row = pl.BlockSpec((bt, D), lambda i: (i, 0))
    return pl.pallas_call(
        body, name="merge_bwd", grid=(T // bt,),
        in_specs=[row, row, pl.BlockSpec((bt, D), lambda i: (i, CGA)), pl.BlockSpec((bt, D), lambda i: (i, CGB)), row],
        out_specs=[row, row, pl.BlockSpec((bt, 2 * D), lambda i: (i, CGA // 2))],
        out_shape=[_sds((T, D), MM), _sds((T, D), MM), _sds((T, NMAIN), MM)],
        compiler_params=_cp(("parallel",)))(ya, yb, pm, pm, dmix)


def _ln_rows(x, y, bias, g, b):
    r = ALPHA * x + y + bias
    mu = jnp.mean(r, axis=-1, keepdims=True)
    var = jnp.mean(jnp.square(r - mu), axis=-1, keepdims=True)
    return (r - mu) * lax.rsqrt(var + LN_EPS) * g + b


def ln_fwd(x, y, p, *, name, bt=1024):
    T = x.shape[0]
    bt = min(bt, T)

    def body(x_ref, y_ref, p_ref, o_ref, ob_ref):
        r = _ln_rows(x_ref[...], y_ref[...], p_ref[0:1, :], p_ref[1:2, :], p_ref[2:3, :])
        o_ref[...] = r
        ob_ref[...] = r.astype(ob_ref.dtype)

    row = pl.BlockSpec((bt, D), lambda i: (i, 0))
    return pl.pallas_call(
        body, name=name, grid=(T // bt,), in_specs=[row, row, pl.BlockSpec((8, D), lambda i: (0, 0))],
        out_specs=[row, row], out_shape=[_sds((T, D)), _sds((T, D), MM)],
        compiler_params=_cp(("parallel",)))(x, y, p)


def ln_bwd(x, y, p, ct, ct2=None, *, name, bt=1024):
    T = x.shape[0]
    bt = min(bt, T)

    def body(*refs):
        it = iter(refs)
        x_ref, y_ref, p_ref, c_ref = next(it), next(it), next(it), next(it)
        c2_ref = next(it) if ct2 is not None else None
        dxa_ref, dr_ref, dp_ref = next(it), next(it), next(it)

        @pl.when(pl.program_id(0) == 0)
        def _():
            dp_ref[...] = jnp.zeros_like(dp_ref)

        c = c_ref[...]
        if c2_ref is not None:
            c = c + c2_ref[...]
        _, vjp = jax.vjp(_ln_rows, x_ref[...], y_ref[...], p_ref[0:1, :], p_ref[1:2, :], p_ref[2:3, :])
        _dx, dy, dbias, dg, db = vjp(c)
        dxa_ref[...] = ALPHA * dy
        dr_ref[...] = dy.astype(dr_ref.dtype)
        row8 = lax.broadcasted_iota(jnp.int32, (8, D), 0)
        dp_ref[...] += jnp.where(row8 == 0, dbias, jnp.where(row8 == 1, dg, jnp.where(row8 == 2, db, 0.0)))

    row = pl.BlockSpec((bt, D), lambda i: (i, 0))
    in_specs = [row, row, pl.BlockSpec((8, D), lambda i: (0, 0)), row] + ([row] if ct2 is not None else [])
    args = [x, y, p, ct] + ([ct2] if ct2 is not None else [])
    return pl.pallas_call(
        body, name=name, grid=(T // bt,), in_specs=in_specs,
        out_specs=[row, row, pl.BlockSpec((8, D), lambda i: (0, 0))],
        out_shape=[_sds((T, D)), _sds((T, D), MM), _sds((8, D))],
        compiler_params=_cp(("arbitrary",)))(*args)


def loss_fwd_bwd(xl, target, *, bt=1024):
    T = xl.shape[0]
    bt = min(bt, T)

    def body(x_ref, t_ref, l_ref, d_ref):
        @pl.when(pl.program_id(0) == 0)
        def _():
            l_ref[...] = jnp.zeros_like(l_ref)

        e = x_ref[...] - t_ref[...]
        d_ref[...] = e * (1.0 / D)
        l_ref[...] += 0.5 * jnp.sum(jnp.mean(e * e, axis=-1, keepdims=True), axis=0, keepdims=True)

    row = pl.BlockSpec((bt, D), lambda i: (i, 0))
    return pl.pallas_call(
        body, name="loss", grid=(T // bt,), in_specs=[row, row],
        out_specs=[pl.BlockSpec((8, 128), lambda i: (0, 0)), row], out_shape=[_sds((8, 128)), _sds((T, D))],
        compiler_params=_cp(("arbitrary",)))(xl, target)


def _row_tile(R, Cc, elems=1 << 18):
    if R * Cc <= elems:
        return R
    tr = 8
    while tr * 2 * Cc <= elems and R % (tr * 2) == 0:
        tr *= 2
    return tr


def adam(w, m, v, ga, gb=None, *, name):
    R, Cc = w.shape
    tr = _row_tile(R, Cc)

    def body(*refs):
        it = iter(refs)
        w_ref, m_ref, v_ref, a_ref = next(it), next(it), next(it), next(it)
        b_ref = next(it) if gb is not None else None
        g_ref, d_ref, mo_ref, vo_ref = next(it), next(it), next(it), next(it)
        g = a_ref[...]
        if b_ref is not None:
            g = g + b_ref[...]
        m2 = B1 * m_ref[...] + (1.0 - B1) * g
        v2 = B2 * v_ref[...] + (1.0 - B2) * jnp.square(g)
        m_hat = m2 / (1.0 - B1 ** STEP)
        v_hat = v2 / (1.0 - B2 ** STEP)
        g_ref[...] = g
        d_ref[...] = -LR * (m_hat / (jnp.sqrt(v_hat) + EPS) + WD * w_ref[...])
        mo_ref[...] = m2
        vo_ref[...] = v2

    blk = pl.BlockSpec((tr, Cc), lambda i: (i, 0))
    args = [w, m, v, ga] + ([gb] if gb is not None else [])
    return pl.pallas_call(
        body, name=name, grid=(R // tr,), in_specs=[blk] * len(args), out_specs=[blk] * 4,
        out_shape=[_sds((R, Cc))] * 4, compiler_params=_cp(("parallel",)))(*args)


def sum4(parts, *, name):
    _, R, Cc = parts.shape
    tr = _row_tile(R, Cc)

    def body(p_ref, out_ref):
        f = lambda t: t.astype(F32)
        out_ref[...] = ((f(p_ref[0]) + f(p_ref[1])) + f(p_ref[2])) + f(p_ref[3])

    return pl.pallas_call(
        body, name=name, grid=(R // tr,), in_specs=[pl.BlockSpec((4, tr, Cc), lambda i: (0, i, 0))],
        out_specs=pl.BlockSpec((tr, Cc), lambda i: (i, 0)), out_shape=_sds((R, Cc)),
        compiler_params=_cp(("parallel",)))(parts)


def _place():
    return lax.axis_index("x"), lax.axis_index("y"), lax.axis_index("c")


def _other_chips(x, y):
    return [(1 - x, y), (x, 1 - y), (1 - x, 1 - y)]


_ANY = pl.BlockSpec(memory_space=pl.ANY)


def allgather_xy(arrs):
    n = len(arrs)
    axes = [0 if a.shape[0] % 2 == 0 else 1 for a in arrs]
    halves = [a.shape[ax] // 2 for a, ax in zip(arrs, axes)]

    def half_of(ref, a, which):
        part = pl.ds(which * halves[a], halves[a])
        return ref.at[part] if axes[a] == 0 else ref.at[:, part]

    def body(*refs):
        ins, outs = refs[:n], refs[n:2 * n]
        send, recv, fsend, frecv, loc = refs[2 * n:]
        x, y, c = _place()
        me = 2 * x + y
        peers = _other_chips(x, y)
        local = [pltpu.make_async_copy(ins[a], outs[a].at[me], loc.at[a]) for a in range(n)]
        for cp in local:
            cp.start()

        def over_ici(a, j, block, src=None):
            px, py = peers[j]
            dst = half_of(outs[a].at[block], a, c)
            return pltpu.make_async_remote_copy(
                src_ref=dst if src is None else half_of(src, a, c), dst_ref=dst, send_sem=send.at[3 * a + j],
                recv_sem=recv.at[3 * a + j], device_id=(px, py, c), device_id_type=MESH)

        def over_d2d(a, j, half):
            px, py = peers[j]
            rows = half_of(outs[a].at[2 * px + py], a, half)
            return pltpu.make_async_remote_copy(
                src_ref=rows, dst_ref=rows, send_sem=fsend.at[3 * a + j], recv_sem=frecv.at[3 * a + j],
                device_id=(x, y, 1 - c), device_id_type=MESH)

        sends = [over_ici(a, j, me, src=ins[a]) for a in range(n) for j in range(3)]
        for cp in sends:
            cp.start()
        passed = []
        for a in range(n):
            for j, (px, py) in enumerate(peers):
                over_ici(a, j, 2 * px + py).wait_recv()
                passed.append(over_d2d(a, j, c))
                passed[-1].start()
        for a in range(n):
            for j in range(3):
                over_d2d(a, j, 1 - c).wait_recv()
        for cp in sends + passed:
            cp.wait_send()
        for cp in local:
            cp.wait()

    return pl.pallas_call(
        body, name="allgather_xy", in_specs=[_ANY] * n, out_specs=[_ANY] * n,
        out_shape=[_sds((4,) + a.shape, a.dtype) for a in arrs],
        scratch_shapes=[pltpu.SemaphoreType.DMA((3 * n,))] * 4 + [pltpu.SemaphoreType.DMA((n,))],
        compiler_params=pltpu.CompilerParams(has_side_effects=True))(*arrs)


_HBM = pl.BlockSpec(memory_space=pltpu.HBM)
_SEM = pl.BlockSpec(memory_space=pltpu.SEMAPHORE)
_EFFECT = pltpu.SideEffectType.DATAFLOW_SIDE_EFFECTING


def _xy_copy(kind, layer, src, land, send, recv, a, j, c, arriving):
    x, y = lax.axis_index("x"), lax.axis_index("y")
    px, py = ((1 - x, y), (x, 1 - y), (1 - x, 1 - y), (x, y))[j]
    if kind == "gather":
        dst = land.at[2 * px + py] if arriving else land.at[2 * x + y]
        src_view = dst if arriving else src
    else:
        dst = land.at[j, layer]
        src_view = src.at[2 * px + py]
    return pltpu.make_async_remote_copy(src_ref=src_view, dst_ref=dst, send_sem=send.at[4 * a + j],
                                        recv_sem=recv.at[4 * a + j], device_id=(px, py, c), device_id_type=MESH)


def xy_start(kind, name, arrs, after, zones=None, layer=0):
    n = len(arrs)
    if zones is None:
        shape = lambda a: ((4,) + a.shape) if kind == "gather" else ((4, DEPTH) + a.shape[1:])
        zones = [lax.empty(shape(a), a.dtype) for a in arrs]

    def body(*refs):
        ins, lands = refs[:n], refs[n:2 * n]
        send, recv = refs[2 * n + 1], refs[2 * n + 2]
        token = refs[-1]
        c = lax.axis_index("c")
        for a in range(n):
            for j in range(4):
                _xy_copy(kind, layer, ins[a], lands[a], send, recv, a, j, c, False).start()
        token[...] = jnp.zeros_like(token)

    hbm = lambda v: pltpu.with_memory_space_constraint(v, pltpu.HBM)
    outs = pl.pallas_call(
        body, name=name,
        out_shape=(pltpu.SemaphoreType.DMA((4 * n,)), pltpu.SemaphoreType.DMA((4 * n,)),
                   *[pltpu.HBM(a.shape, a.dtype) for a in arrs], *[pltpu.HBM(z.shape, z.dtype) for z in zones],
                   _sds((8, 128))),
        in_specs=[_HBM] * (2 * n) + [_ANY],
        out_specs=(_SEM, _SEM, *[_HBM] * (2 * n), pl.BlockSpec(memory_space=pltpu.VMEM)),
        input_output_aliases={i: 2 + i for i in range(2 * n)},
        compiler_params=pltpu.CompilerParams(has_side_effects=_EFFECT))(
            *[hbm(a) for a in arrs], *[hbm(z) for z in zones], after)
    return (kind, layer, outs[0], outs[1], list(outs[2:2 + n])), list(outs[2 + n:2 + 2 * n]), outs[-1]


def xy_wait(name, handles, zones, after):
    kind, layer, send_sems, recv_sems, srcs = handles
    n = len(srcs)

    def body(*refs):
        ins, lands = refs[:n], refs[n:2 * n]
        send, recv = refs[2 * n], refs[2 * n + 1]
        c = lax.axis_index("c")
        for a in range(n):
            for j in range(4):
                _xy_copy(kind, layer, ins[a], lands[a], send, recv, a, j, c, False).wait_send()
                _xy_copy(kind, layer, ins[a], lands[a], send, recv, a, j, c, True).wait_recv()

    outs = pl.pallas_call(
        body, name=name,
        out_shape=tuple(pltpu.HBM(v.shape, v.dtype) for v in srcs + zones),
        in_specs=[_HBM] * (2 * n) + [_SEM, _SEM, _ANY], out_specs=tuple([_HBM] * (2 * n)),
        input_output_aliases={i: i for i in range(2 * n)},
        compiler_params=pltpu.CompilerParams(has_side_effects=_EFFECT))(*srcs, *zones, send_sems, recv_sems, after)
    return list(outs[n:])


def swap_c(arrs):
    n = len(arrs)

    def body(*refs):
        ins, outs = refs[:n], refs[n:2 * n]
        send, recv = refs[2 * n:]
        x, y, c = _place()
        cps = [pltpu.make_async_remote_copy(src_ref=ins[a], dst_ref=outs[a], send_sem=send.at[a], recv_sem=recv.at[a],
                                            device_id=(x, y, 1 - c), device_id_type=MESH) for a in range(n)]
        for cp in cps:
            cp.start()
        for cp in cps:
            cp.wait_recv()
        for cp in cps:
            cp.wait_send()

    return pl.pallas_call(
        body, name="swap_c", in_specs=[_ANY] * n, out_specs=[_ANY] * n, out_shape=[_sds(a.shape, a.dtype) for a in arrs],
        scratch_shapes=[pltpu.SemaphoreType.DMA((n,)), pltpu.SemaphoreType.DMA((n,))],
        compiler_params=pltpu.CompilerParams(has_side_effects=True))(*arrs)


def allreduce_small(v):
    R = v.shape[0]

    def body(v_ref, o_ref, buf, send, recv):
        x, y, c = _place()
        me = 4 * x + 2 * y + c
        buf[0] = v_ref[...]

        def cp(k):
            dx, dy, dc = (k >> 2) & 1, (k >> 1) & 1, k & 1
            return pltpu.make_async_remote_copy(
                src_ref=v_ref, dst_ref=buf.at[k], send_sem=send.at[k - 1], recv_sem=recv.at[k - 1],
                device_id=(x ^ dx, y ^ dy, c ^ dc), device_id_type=MESH)

        cps = [cp(k) for k in range(1, 8)]
        for t in cps:
            t.start()
        for t in cps:
            t.wait_recv()
        acc = buf[me]
        for dev in range(1, 8):
            acc = acc + buf[jnp.bitwise_xor(me, dev)]
        o_ref[...] = acc
        for t in cps:
            t.wait_send()

    vm = pl.BlockSpec(memory_space=pltpu.VMEM)
    return pl.pallas_call(
        body, name="allreduce_small", in_specs=[vm], out_specs=vm, out_shape=_sds((R, 128)),
        scratch_shapes=[pltpu.VMEM((8, R, 128), F32), pltpu.SemaphoreType.DMA((7,)), pltpu.SemaphoreType.DMA((7,))],
        compiler_params=pltpu.CompilerParams(has_side_effects=True, vmem_limit_bytes=VMEM_LIMIT))(v)


def _rows8(*rows):
    n = rows[0].shape[-1]
    t = jnp.stack([r.reshape(n).astype(F32) for r in rows])
    return jnp.pad(t, ((0, 8 - len(rows)), (0, 0)))


def _lanes128(a):
    f = a.reshape(-1).astype(F32)
    return jnp.pad(f, (0, 128 - f.shape[0]))


def _layer_fwd(x, xb, W):
    pm = mm_nn(xb, W["w_main"], name="proj_main", tm=2048)
    pab = mm_nn(xb, W["w_ab"], name="proj_ab")
    q, k, v, gb = pre_qkv_fwd(pm, pab, W["cw"], W["gp"])
    sc = pre_sc_fwd(pm, W["csc"])
    u, w, qd, kd, pp, ys, eg = gdn_intra_fwd(q, k, v, gb)
    o2, s0 = gdn_scan_fwd(u, w, qd, kd, pp, eg)
    og = post_fwd(o2, pm, W["nw"])
    ya = mm_nn(og, W["w_og"], name="proj_og")
    yb = mm_nn(sc, W["w_osc"], name="proj_osc")
    mixed = merge_fwd(ya, yb, pm)
    out = mm_nn(mixed, W["w_out"], name="proj_out")
    x1, x1b = ln_fwd(x, out, W["ln1"], name="ln1_fwd")
    hfac, h = mm_nn(x1b, W["w_up"], bias=W["b_up"], relu2=True, out_dtype=MM, name="mlp_up", tm=2048)
    dn = mm_nn(h, W["w_down"], name="mlp_down", tk=DFF)
    x2, x2b = ln_fwd(x1, dn, W["ln2"], name="ln2_fwd")
    saved = dict(x=x, xb=xb, pm=pm, pab=pab, q=q, k=k, v=v, gb=gb, sc=sc, o2=o2, s0=s0, og=og, ya=ya, yb=yb,
                 u=u, w=w, qd=qd, kd=kd, pp=pp, ys=ys, eg=eg,
                 mixed=mixed, out=out, x1=x1, x1b=x1b, hfac=hfac, h=h, dn=dn)
    return x2, x2b, saved


def _layer_bwd(ct, W, S, early=None):
    dxa2, dr2b, dp2 = ln_bwd(S["x1"], S["dn"], W["ln2"], ct, name="ln2_bwd")
    g_down = mm_tn(S["h"], dr2b, name="dw_down", tm=2048, tk=2048)
    dhpre, db_up = mm_nt(dr2b, W["w_down"], dact=S["hfac"], out_dtype=MM, name="mlp_down_bwd", tm=2048)
    g_up = mm_tn(S["x1b"], dhpre, name="dw_up", tn=2048, tk=2048)
    dx1 = mm_nt(dhpre, W["w_up"], add=dxa2, name="mlp_up_bwd", tk=DFF)
    dxa1, dr1b, dp1 = ln_bwd(S["x"], S["out"], W["ln1"], dx1, name="ln1_bwd")
    g_out = mm_tn(S["mixed"], dr1b, name="dw_out")
    dmix = mm_nt(dr1b, W["w_out"], name="proj_out_bwd")
    dya, dyb, dpm = merge_bwd(S["ya"], S["yb"], S["pm"], dmix)
    g_og = mm_tn(S["og"], dya, name="dw_og")
    g_osc = mm_tn(S["sc"], dyb, name="dw_osc")
    dog = mm_nt(dya, W["w_og"], name="proj_og_bwd")
    dsc = mm_nt(dyb, W["w_osc"], name="proj_osc_bwd")
    token = None if early is None else early(dict(w_o_gdn=g_og, w_o_sc=g_osc, w_out=g_out, w_up=g_up, w_down=g_down))
    nw = W["nw"] if token is None else W["nw"] + token[0, 0]
    do, dpm, dnw = post_bwd(S["o2"], S["pm"], nw, dog, dpm)
    du, dw, dqd, dkd, dpp, deg = gdn_scan_bwd(S["u"], S["w"], S["qd"], S["kd"], S["pp"], S["eg"], S["s0"], do)
    dq2, dk2, dv2, dgb2 = gdn_intra_bwd(S["q"], S["k"], S["v"], S["gb"], S["ys"], du, dw, dqd, dkd, dpp, deg)
    dpm, dpab, dcw, dgp = pre_qkv_bwd(S["pm"], S["pab"], W["cw"], W["gp"], dq2, dk2, dv2, dgb2, dpm)
    dpm, dcsc = pre_sc_bwd(S["pm"], W["csc"], dsc, dpm)
    g_main = mm_tn(S["xb"], dpm, name="dw_main", tn=NMAIN // 4)
    g_ab = mm_tn(S["xb"], dpab, name="dw_ab")
    t = mm_nt(dpab, W["w_ab"], add=dxa1, name="proj_ab_bwd")
    dx = mm_nt(dpm, W["w_main"], add=t, name="proj_main_bwd", tk=NMAIN // 3)
    g_in = jnp.concatenate([g_main[:, :3 * D], g_main[:, 8 * D:], g_ab[:, :4 * NH], g_main[:, 3 * D:8 * D]], axis=1)
    grads = dict(
        w_in=g_in, w_o_gdn=g_og, w_o_sc=g_osc, w_out=g_out, w_up=g_up, w_down=g_down,
        conv_qkv=dcw[:3], conv_sc=dcsc[:3], a_log=dgp[0, :2 * NH].reshape(2, NH), dt_bias=dgp[1, :2 * NH].reshape(2, NH),
        gdn_norm_w=dnw[0], ln1_g=dp1[1], ln1_b=dp1[2], b_up=db_up[0], b_down=dp2[0], ln2_g=dp2[1], ln2_b=dp2[2])
    return dx, grads


def _layer_weights(l, full, i, conv, a_log, dt_bias, gdn_norm_w, ln1_g, ln1_b, b_up, b_down, ln2_g, ln2_b):
    w_in = full["w_in"][i]
    w_main = jnp.concatenate([w_in[:, :3 * D], w_in[:, 4 * D + 4 * NH:], w_in[:, 3 * D:4 * D]], axis=1)
    w_ab = jnp.pad(w_in[:, 4 * D:4 * D + 4 * NH], ((0, 0), (0, 128 - 4 * NH)))
    return dict(
        w_main=w_main, w_ab=w_ab, w_og=full["w_o_gdn"][i], w_osc=full["w_o_sc"][i], w_out=full["w_out"][i],
        w_up=full["w_up"][i], w_down=full["w_down"][i],
        cw=jnp.pad(conv["conv_qkv"][l].astype(F32), ((0, 5), (0, 0))),
        csc=jnp.pad(conv["conv_sc"][l].astype(F32), ((0, 5), (0, 0))),
        gp=_rows8(_lanes128(a_log[l]), _lanes128(dt_bias[l])), nw=_rows8(gdn_norm_w[l]),
        ln1=_rows8(jnp.zeros((D,), F32), ln1_g[l], ln1_b[l]), ln2=_rows8(b_down[l], ln2_g[l], ln2_b[l]),
        b_up=b_up[l].reshape(1, DFF).astype(F32))


def local_step(xs, target, weights_of, after_bwd, mid_bwd=None):
    x, xb = xs, xs.astype(MM)
    Ws, saved = [], []
    for l in range(DEPTH):
        Ws.append(weights_of(l, x))
        x, xb, S = _layer_fwd(x, xb, Ws[l])
        saved.append(S)
    loss_tile, ct = loss_fwd_bwd(x, target)
    token = None
    for l in reversed(range(DEPTH)):
        W = Ws[l] if token is None else dict(Ws[l], ln2=Ws[l]["ln2"] + token[0, 0])
        ct, grads = _layer_bwd(ct, W, saved[l], None if mid_bwd is None else (lambda part, l=l: mid_bwd(l, part)))
        token = after_bwd(l, grads)
    return loss_tile, ct


EARLY = 1
BIG = ("w_in", "w_o_gdn", "w_o_sc", "w_out", "w_up", "w_down")
SMALL = ("conv_qkv", "a_log", "dt_bias", "gdn_norm_w", "conv_sc", "ln1_g", "ln1_b", "b_up", "b_down", "ln2_g", "ln2_b")
ORDER = ("w_in", "conv_qkv", "a_log", "dt_bias", "gdn_norm_w", "w_o_gdn", "conv_sc", "w_o_sc", "w_out", "ln1_g",
         "ln1_b", "w_up", "b_up", "w_down", "b_down", "ln2_g", "ln2_b")


def _pack(arrs):
    flat = jnp.concatenate([a.reshape(-1).astype(F32) for a in arrs])
    n = flat.shape[0]
    rows = -(-n // 1024) * 8
    return jnp.pad(flat, (0, rows * 128 - n)).reshape(rows, 128)


def _unpack(buf, like):
    flat = buf.reshape(-1)
    out, o = [], 0
    for a in like:
        n = 1
        for s in a.shape:
            n *= s
        out.append(flat[o:o + n].reshape(a.shape))
        o += n
    return out


def _gathered(name, g):
    if name in ("w_in", "w_up", "conv_qkv", "conv_sc"):
        t = jnp.moveaxis(g, 0, -2)
        return t.reshape(t.shape[:-2] + (t.shape[-2] * t.shape[-1],))
    t = jnp.moveaxis(g, 0, 1)
    return t.reshape((t.shape[0], t.shape[1] * t.shape[2]) + t.shape[3:])


def _by_chip(name, g):
    if name in ("w_in", "w_up"):
        r, ccols = g.shape
        return jnp.moveaxis(g.reshape(r, 4, ccols // 4), 1, 0)
    return g.reshape((4, g.shape[0] // 4) + g.shape[1:])


def kernel(x, w_in, conv_qkv, a_log, dt_bias, gdn_norm_w, w_o_gdn, conv_sc, w_o_sc, w_out, ln1_g, ln1_b, w_up, b_up, w_down, b_down, ln2_g, ln2_b, loss_target, m_w_in, m_conv_qkv, m_a_log, m_dt_bias, m_gdn_norm_w, m_w_o_gdn, m_conv_sc, m_w_o_sc, m_w_out, m_ln1_g, m_ln1_b, m_w_up, m_b_up, m_w_down, m_b_down, m_ln2_g, m_ln2_b, v_w_in, v_conv_qkv, v_a_log, v_dt_bias, v_gdn_norm_w, v_w_o_gdn, v_conv_sc, v_w_o_sc, v_w_out, v_ln1_g, v_ln1_b, v_w_up, v_b_up, v_w_down, v_b_down, v_ln2_g, v_ln2_b):
    w = dict(w_in=w_in, conv_qkv=conv_qkv, a_log=a_log, dt_bias=dt_bias, gdn_norm_w=gdn_norm_w, w_o_gdn=w_o_gdn,
             conv_sc=conv_sc, w_o_sc=w_o_sc, w_out=w_out, ln1_g=ln1_g, ln1_b=ln1_b, w_up=w_up, b_up=b_up,
             w_down=w_down, b_down=b_down, ln2_g=ln2_g, ln2_b=ln2_b)
    m = dict(w_in=m_w_in, conv_qkv=m_conv_qkv, a_log=m_a_log, dt_bias=m_dt_bias, gdn_norm_w=m_gdn_norm_w,
             w_o_gdn=m_w_o_gdn, conv_sc=m_conv_sc, w_o_sc=m_w_o_sc, w_out=m_w_out, ln1_g=m_ln1_g, ln1_b=m_ln1_b,
             w_up=m_w_up, b_up=m_b_up, w_down=m_w_down, b_down=m_b_down, ln2_g=m_ln2_g, ln2_b=m_ln2_b)
    v = dict(w_in=v_w_in, conv_qkv=v_conv_qkv, a_log=v_a_log, dt_bias=v_dt_bias, gdn_norm_w=v_gdn_norm_w,
             w_o_gdn=v_w_o_gdn, conv_sc=v_conv_sc, w_o_sc=v_w_o_sc, w_out=v_w_out, ln1_g=v_ln1_g, ln1_b=v_ln1_b,
             w_up=v_w_up, b_up=v_b_up, w_down=v_w_down, b_down=v_b_down, ln2_g=v_ln2_g, ln2_b=v_ln2_b)
    chip = 2 * lax.axis_index("x") + lax.axis_index("y")

    names = BIG + ("conv_qkv", "conv_sc")
    blocks = [w[n].astype(MM) if n in BIG else w[n] for n in names]
    got = allgather_xy([b[:EARLY] if n in BIG else b for n, b in zip(names, blocks)])
    early = {n: _gathered(n, g) for n, g in zip(names, got)}
    gather, gather_zones, token = xy_start("gather", "gather_start", [b[EARLY:] for b in blocks[:len(BIG)]],
                                           after=got[0])
    vectors = (a_log, dt_bias, gdn_norm_w, ln1_g, ln1_b, b_up, b_down, ln2_g, ln2_b)
    late, grads = {}, [None] * DEPTH
    scatters = {"mid": [None] * DEPTH, "end": [None] * DEPTH}
    zones = {"mid": None, "end": None}

    def weights_of(l, x_l):
        if l < EARLY:
            return _layer_weights(l, early, l, early, *vectors)
        if not late:
            for n, zone in zip(BIG, xy_wait("gather_wait", gather, gather_zones, after=x_l)):
                late[n] = _gathered(n, zone)
        return _layer_weights(l, late, l - EARLY, early, *vectors)

    def scatter(when, l, g, which):
        scatters[when][l], zones[when], tok = xy_start(
            "scatter", "scatter_%s_start_%d" % (when, l), [_by_chip(n, g[n]).astype(MM) for n in which], after=a_log,
            zones=zones[when], layer=l)
        return tok

    def after_bwd(l, g):
        grads[l] = g
        return scatter("end", l, g, BIG[:1])

    loss_tile, dx = local_step(x[0] + token[0, 0], loss_target[0], weights_of, after_bwd,
                               lambda l, g: scatter("mid", l, g, BIG[1:]))
    loss = lax.psum(loss_tile[0, 0], ("x", "y", "c"))

    for when in ("mid", "end"):
        for l in range(DEPTH):
            zones[when] = xy_wait("scatter_%s_wait_%d" % (when, l), scatters[when][l], zones[when], after=dx)
    part = [sum4(z.reshape(4, -1, z.shape[-1]), name="sum_" + n) for n, z in zip(BIG, zones["end"] + zones["mid"])]
    other = swap_c(part)
    out = {}
    for n, mine, theirs in zip(BIG, part, other):
        cols = mine.shape[-1]
        res = adam(w[n].reshape(-1, cols), m[n].reshape(-1, cols), v[n].reshape(-1, cols), mine, theirs, name="adam_" + n)
        out[n] = [r.reshape(w[n].shape) for r in res]

    stacked = [jnp.stack([grads[l][n] for l in range(DEPTH)]) for n in SMALL]
    summed = _unpack(allreduce_small(_pack(stacked)), stacked)
    gs = []
    for n, g in zip(SMALL, summed):
        if n in ("conv_qkv", "conv_sc"):
            blk = w[n].shape[-1]
            g = lax.dynamic_slice_in_dim(g, chip * blk, blk, axis=2)
        gs.append(g)
    res = adam(_pack([w[n] for n in SMALL]), _pack([m[n] for n in SMALL]), _pack([v[n] for n in SMALL]), _pack(gs),
               name="adam_small")
    for n, parts in zip(SMALL, zip(*[_unpack(r, gs) for r in res])):
        out[n] = list(parts)

    outs = [loss, dx[None]]
    for kind in range(4):
        outs += [out[n][kind] for n in ORDER]
    return tuple(outs)
```
